```python
import jax, jax.numpy as jnp
from jax import lax
import numpy as np

D_MODEL = 2048
BATCH = 8
SEQ = 2048
DEPTH = 2

HEAD_DIM = 128
N_MIX_HEADS = D_MODEL // HEAD_DIM
A_HEADS = N_MIX_HEADS // 4
B_HEADS = (N_MIX_HEADS - A_HEADS) // 2
C_HEADS = N_MIX_HEADS - A_HEADS - B_HEADS
C_KV_HEADS = 2
A_KEY_DIM = 128
A_CHUNK = 16
DILATED_PATTERNS = ((128, 1), (512, 4), (2048, 16))
C_WINDOW = 128
BLOCK = 128
ROPE_THETA = 500000.0
ROPE_DIM = HEAD_DIM // 4
D_FF = 5632
CONV_WIDTH = 3
LN_EPS = 1e-5
ALPHA = (2 * DEPTH) ** 0.25
BETA = (8 * DEPTH) ** -0.25

A_KEY_WIDTH = A_HEADS * A_KEY_DIM
A_WIDTH = A_HEADS * HEAD_DIM
B_WIDTH = B_HEADS * HEAD_DIM
C_WIDTH = C_HEADS * HEAD_DIM
C_KV_WIDTH = C_KV_HEADS * HEAD_DIM
MIX_WIDTH = A_WIDTH + B_WIDTH + C_WIDTH
PROJ_SIZES = (A_KEY_WIDTH, A_KEY_WIDTH, A_WIDTH, A_WIDTH,
              B_WIDTH, B_WIDTH, B_WIDTH,
              C_WIDTH, C_KV_WIDTH, C_KV_WIDTH)
VALUE_BLOCKS = (2, 6, 9)
IN_WIDTH = sum(PROJ_SIZES)

kernel_name = "hybrid_hgrn2_dilated_swa_sink_convffn"


def layer_norm(x, g, b):
    xf = x.astype(jnp.float32)
    mu = jnp.mean(xf, -1, keepdims=True)
    var = jnp.mean(jnp.square(xf - mu), -1, keepdims=True)
    return ((xf - mu) * lax.rsqrt(var + LN_EPS) * g.astype(jnp.float32) + b.astype(jnp.float32)).astype(x.dtype)


def rms_norm(x, g):
    xf = x.astype(jnp.float32)
    return xf * lax.rsqrt(jnp.mean(jnp.square(xf), -1, keepdims=True) + LN_EPS) * g.astype(jnp.float32)


def rope_tables(seq):
    inv = ROPE_THETA ** (-jnp.arange(0, ROPE_DIM, 2, dtype=jnp.float32) / ROPE_DIM)
    ang = jnp.arange(seq, dtype=jnp.float32)[:, None] * inv[None, :]
    return jnp.cos(ang), jnp.sin(ang)


def partial_rope(x, cos, sin):
    half = ROPE_DIM // 2
    xr = x[..., :ROPE_DIM].astype(jnp.float32)
    x1, x2 = xr[..., :half], xr[..., half:]
    rot = jnp.concatenate([x1 * cos - x2 * sin, x2 * cos + x1 * sin], -1)
    return jnp.concatenate([rot.astype(x.dtype), x[..., ROPE_DIM:]], -1)


def banded_attention(q, k, v, max_lag, sink=None):
    b, h, L, dh = q.shape
    nb = -(-L // BLOCK)
    pad = nb * BLOCK - L
    padw = ((0, 0), (0, 0), (0, pad), (0, 0))
    qb = jnp.pad(q, padw).reshape(b, h, nb, BLOCK, dh)
    kb = jnp.pad(k, padw).reshape(b, h, nb, BLOCK, dh)
    vb = jnp.pad(v, padw).reshape(b, h, nb, BLOCK, dh)
    shift = ((0, 0), (0, 0), (1, 0), (0, 0), (0, 0))
    kk = jnp.concatenate([jnp.pad(kb, shift)[:, :, :-1], kb], axis=3)
    vv = jnp.concatenate([jnp.pad(vb, shift)[:, :, :-1], vb], axis=3)
    s = jnp.einsum('bhnqd,bhnkd->bhnqk', qb, kk, preferred_element_type=jnp.float32) * (dh ** -0.5)
    r = jnp.arange(BLOCK)[:, None]
    c = jnp.arange(2 * BLOCK)[None, :]
    lag = BLOCK + r - c
    kpos = (jnp.arange(nb)[:, None, None] - 1) * BLOCK + c[None]
    valid = ((lag >= 0) & (lag <= max_lag))[None] & (kpos >= 0)
    s = jnp.where(valid, s, -jnp.inf)
    m = jnp.max(s, -1)
    if sink is not None:
        sk = sink.astype(jnp.float32)[None, :, None, None]
        m = jnp.maximum(m, sk)
    e = jnp.exp(s - m[..., None])
    denom = jnp.sum(e, -1)
    if sink is not None:
        denom = denom + jnp.exp(sk - m)
    out = jnp.einsum('bhnqk,bhnkd->bhnqd', e, vv.astype(jnp.float32)) / denom[..., None]
    lse = m + jnp.log(denom)
    out = out.reshape(b, h, nb * BLOCK, dh)[:, :, :L].astype(q.dtype)
    return out, lse.reshape(b, h, nb * BLOCK)[:, :, :L]


def dilated_attention(q, k, v):
    b, h, S, dh = q.shape
    outs, lses = [], []
    for window, dil in DILATED_PATTERNS:
        L = S // dil
        def split(t):
            return t.reshape(b, h, L, dil, dh).transpose(0, 1, 3, 2, 4).reshape(b, h * dil, L, dh)
        o, lse = banded_attention(split(q), split(k), split(v), window // dil)
        outs.append(o.reshape(b, h, dil, L, dh).transpose(0, 1, 3, 2, 4).reshape(b, h, S, dh))
        lses.append(lse.reshape(b, h, dil, L).transpose(0, 1, 3, 2).reshape(b, h, S))
    w = jax.nn.softmax(jnp.stack(lses, 0), axis=0)
    out = jnp.einsum('pbhs,pbhsd->bhsd', w, jnp.stack(outs, 0).astype(jnp.float32))
    return out.astype(q.dtype)


def hgrn2(q, f_logit, i, g, lb, norm_w):
    b, S, _ = q.shape
    n = S // A_CHUNK
    def chunks(t, dim):
        return t.astype(jnp.float32).reshape(b, n, A_CHUNK, A_HEADS, dim).transpose(0, 3, 1, 2, 4)
    qc = jax.nn.silu(chunks(q, A_KEY_DIM))
    fg = lb.astype(jnp.float32).reshape(A_HEADS, 1, 1, A_KEY_DIM) + \
        (1.0 - lb.astype(jnp.float32).reshape(A_HEADS, 1, 1, A_KEY_DIM)) * jax.nn.sigmoid(chunks(f_logit, A_KEY_DIM))
    kc = 1.0 - fg
    vc = chunks(i, HEAD_DIM)
    bcum = jnp.cumsum(jnp.log(fg), axis=3)
    causal = jnp.tril(jnp.ones((A_CHUNK, A_CHUNK), dtype=bool))
    diff = bcum[:, :, :, :, None, :] - bcum[:, :, :, None, :, :]
    decay = jnp.exp(jnp.where(causal[:, :, None], diff, -jnp.inf))
    att = jnp.einsum('bhnik,bhnjk,bhnijk->bhnij', qc, kc, decay)
    o_intra = jnp.einsum('bhnij,bhnjv->bhniv', att, vc)
    blast = bcum[:, :, :, -1:, :]
    upd = jnp.einsum('bhnck,bhncv->bhnkv', kc * jnp.exp(blast - bcum), vc)
    cdec = jnp.exp(blast[:, :, :, 0, :])
    def step(state, inp):
        dec, u = inp
        return dec[..., None] * state + u, state
    s0 = jnp.zeros((b, A_HEADS, A_KEY_DIM, HEAD_DIM), jnp.float32)
    _, s_prev = lax.scan(step, s0, (jnp.moveaxis(cdec, 2, 0), jnp.moveaxis(upd, 2, 0)))
    s_prev = jnp.moveaxis(s_prev, 0, 2)
    o_inter = jnp.einsum('bhnck,bhnkv->bhncv', qc * jnp.exp(bcum), s_prev)
    o = (o_intra + o_inter).transpose(0, 2, 3, 1, 4).reshape(b, S, A_HEADS, HEAD_DIM)
    o = rms_norm(o, norm_w).reshape(b, S, A_WIDTH)
    return (o * jax.nn.silu(g.astype(jnp.float32))).astype(q.dtype)


def mixer_sublayer(x, w_in, lb, a_norm_w, c_sink, w_out, cos, sin):
    b, S, _ = x.shape
    proj = x @ w_in
    idx = [int(t) for t in np.cumsum(PROJ_SIZES)[:-1]]
    qA, fA, iA, gA, qB, kB, vB, qC, kC, vC = jnp.split(proj, idx, axis=-1)
    def heads(t, nh):
        return t.reshape(b, S, nh, HEAD_DIM).transpose(0, 2, 1, 3)
    def merge(t):
        return t.transpose(0, 2, 1, 3).reshape(b, S, -1)
    oA = hgrn2(qA, fA, iA, gA, lb, a_norm_w)
    oB = dilated_attention(partial_rope(heads(qB, B_HEADS), cos, sin),
                           partial_rope(heads(kB, B_HEADS), cos, sin), heads(vB, B_HEADS))
    rep = C_HEADS // C_KV_HEADS
    kCh = jnp.repeat(partial_rope(heads(kC, C_KV_HEADS), cos, sin), rep, axis=1)
    vCh = jnp.repeat(heads(vC, C_KV_HEADS), rep, axis=1)
    oC, _ = banded_attention(partial_rope(heads(qC, C_HEADS), cos, sin), kCh, vCh, C_WINDOW - 1, c_sink)
    mixed = jnp.concatenate([oA, merge(oB), merge(oC)], axis=-1)
    return mixed @ w_out


def conv_ffn(x, w_gate, w_up, conv_w, conv_b, w_down):
    S = x.shape[1]
    g = x @ w_gate
    u = x @ w_up
    gp = jnp.pad(g, ((0, 0), (CONV_WIDTH - 1, 0), (0, 0)))
    gc = conv_b
    for j in range(CONV_WIDTH):
        gc = gc + conv_w[j] * gp[:, j:j + S]
    return (jax.nn.silu(gc) * u) @ w_down


def _fwd_setup_inputs(seed: int = 0) -> dict:
    key = jax.random.key(seed)
    ks = jax.random.split(key, 24)
    f32 = jnp.float32
    x = jax.random.normal(ks[0], (BATCH, SEQ, D_MODEL), f32)
    blocks = []
    for j, size in enumerate(PROJ_SIZES):
        w = jax.random.normal(ks[1 + j], (DEPTH, D_MODEL, size), f32) * D_MODEL ** -0.5
        if j in VALUE_BLOCKS:
            w = w * BETA
        blocks.append(w)
    w_in = jnp.concatenate(blocks, axis=-1)
    lb_logits = jax.random.normal(ks[11], (DEPTH, A_KEY_WIDTH), f32)
    a_norm_w = 1.0 + 0.02 * jax.random.normal(ks[12], (DEPTH, HEAD_DIM), f32)
    c_sinks = jax.random.normal(ks[13], (DEPTH, C_HEADS), f32)
    w_out = jax.random.normal(ks[14], (DEPTH, MIX_WIDTH, D_MODEL), f32) * (MIX_WIDTH ** -0.5) * BETA
    ln1_g = 1.0 + 0.02 * jax.random.normal(ks[15], (DEPTH, D_MODEL), f32)
    ln1_b = 0.02 * jax.random.normal(ks[16], (DEPTH, D_MODEL), f32)
    w_gate = jax.random.normal(ks[17], (DEPTH, D_MODEL, D_FF), f32) * D_MODEL ** -0.5
    w_up = jax.random.normal(ks[18], (DEPTH, D_MODEL, D_FF), f32) * (D_MODEL ** -0.5) * BETA
    conv_w = jax.random.normal(ks[19], (DEPTH, CONV_WIDTH, D_FF), f32) * CONV_WIDTH ** -0.5
    conv_b = 0.02 * jax.random.normal(ks[20], (DEPTH, D_FF), f32)
    w_down = jax.random.normal(ks[21], (DEPTH, D_FF, D_MODEL), f32) * (D_FF ** -0.5) * BETA
    ln2_g = 1.0 + 0.02 * jax.random.normal(ks[22], (DEPTH, D_MODEL), f32)
    ln2_b = 0.02 * jax.random.normal(ks[23], (DEPTH, D_MODEL), f32)
    return {"x": x, "w_in": w_in, "lb_logits": lb_logits, "a_norm_w": a_norm_w, "c_sinks": c_sinks,
            "w_out": w_out, "ln1_g": ln1_g, "ln1_b": ln1_b, "w_gate": w_gate, "w_up": w_up,
            "conv_w": conv_w, "conv_b": conv_b, "w_down": w_down, "ln2_g": ln2_g, "ln2_b": ln2_b}


def _fwd_reference(x, w_in, lb_logits, a_norm_w, c_sinks, w_out, ln1_g, ln1_b, w_gate, w_up,
              conv_w, conv_b, w_down, ln2_g, ln2_b):
    cos, sin = rope_tables(x.shape[1])
    lbs = jnp.cumsum(jax.nn.softmax(lb_logits.astype(jnp.float32), axis=0), axis=0)
    lbs = lbs - lbs[0]
    for l in range(DEPTH):
        y = mixer_sublayer(x, w_in[l], lbs[l], a_norm_w[l], c_sinks[l], w_out[l], cos, sin)
        x = layer_norm(ALPHA * x + y, ln1_g[l], ln1_b[l])
        y = conv_ffn(x, w_gate[l], w_up[l], conv_w[l], conv_b[l], w_down[l])
        x = layer_norm(ALPHA * x + y, ln2_g[l], ln2_b[l])
    return x


import jax as _jax
import jax.numpy as _jnp

TWIN_FORMAT = 'train_step'
FWD_PARAMS = ['x', 'w_in', 'lb_logits', 'a_norm_w', 'c_sinks', 'w_out', 'ln1_g', 'ln1_b', 'w_gate', 'w_up', 'conv_w', 'conv_b', 'w_down', 'ln2_g', 'ln2_b']
TWIN_WEIGHTS = ['w_in', 'lb_logits', 'a_norm_w', 'c_sinks', 'w_out', 'ln1_g', 'ln1_b', 'w_gate', 'w_up', 'conv_w', 'conv_b', 'w_down', 'ln2_g', 'ln2_b']
TWIN_DIFF_INPUT = 'x'
TWIN_INPUTS = ['x', 'w_in', 'lb_logits', 'a_norm_w', 'c_sinks', 'w_out', 'ln1_g', 'ln1_b', 'w_gate', 'w_up', 'conv_w', 'conv_b', 'w_down', 'ln2_g', 'ln2_b', 'loss_target', 'm_w_in', 'm_lb_logits', 'm_a_norm_w', 'm_c_sinks', 'm_w_out', 'm_ln1_g', 'm_ln1_b', 'm_w_gate', 'm_w_up', 'm_conv_w', 'm_conv_b', 'm_w_down', 'm_ln2_g', 'm_ln2_b', 'v_w_in', 'v_lb_logits', 'v_a_norm_w', 'v_c_sinks', 'v_w_out', 'v_ln1_g', 'v_ln1_b', 'v_w_gate', 'v_w_up', 'v_conv_w', 'v_conv_b', 'v_w_down', 'v_ln2_g', 'v_ln2_b']
TWIN_OUTPUTS = ['loss', 'grad_x', 'grad_w_in', 'grad_lb_logits', 'grad_a_norm_w', 'grad_c_sinks', 'grad_w_out', 'grad_ln1_g', 'grad_ln1_b', 'grad_w_gate', 'grad_w_up', 'grad_conv_w', 'grad_conv_b', 'grad_w_down', 'grad_ln2_g', 'grad_ln2_b', 'delta_w_in', 'delta_lb_logits', 'delta_a_norm_w', 'delta_c_sinks', 'delta_w_out', 'delta_ln1_g', 'delta_ln1_b', 'delta_w_gate', 'delta_w_up', 'delta_conv_w', 'delta_conv_b', 'delta_w_down', 'delta_ln2_g', 'delta_ln2_b', 'new_m_w_in', 'new_m_lb_logits', 'new_m_a_norm_w', 'new_m_c_sinks', 'new_m_w_out', 'new_m_ln1_g', 'new_m_ln1_b', 'new_m_w_gate', 'new_m_w_up', 'new_m_conv_w', 'new_m_conv_b', 'new_m_w_down', 'new_m_ln2_g', 'new_m_ln2_b', 'new_v_w_in', 'new_v_lb_logits', 'new_v_a_norm_w', 'new_v_c_sinks', 'new_v_w_out', 'new_v_ln1_g', 'new_v_ln1_b', 'new_v_w_gate', 'new_v_w_up', 'new_v_conv_w', 'new_v_conv_b', 'new_v_w_down', 'new_v_ln2_g', 'new_v_ln2_b']
TWIN_LEAF_KINDS = {'loss': 'loss', 'grad_x': 'grad_x', 'grad_w_in': 'grad_w', 'grad_lb_logits': 'grad_w', 'grad_a_norm_w': 'grad_w', 'grad_c_sinks': 'grad_w', 'grad_w_out': 'grad_w', 'grad_ln1_g': 'grad_w', 'grad_ln1_b': 'grad_w', 'grad_w_gate': 'grad_w', 'grad_w_up': 'grad_w', 'grad_conv_w': 'grad_w', 'grad_conv_b': 'grad_w', 'grad_w_down': 'grad_w', 'grad_ln2_g': 'grad_w', 'grad_ln2_b': 'grad_w', 'delta_w_in': 'delta_w', 'delta_lb_logits': 'delta_w', 'delta_a_norm_w': 'delta_w', 'delta_c_sinks': 'delta_w', 'delta_w_out': 'delta_w', 'delta_ln1_g': 'delta_w', 'delta_ln1_b': 'delta_w', 'delta_w_gate': 'delta_w', 'delta_w_up': 'delta_w', 'delta_conv_w': 'delta_w', 'delta_conv_b': 'delta_w', 'delta_w_down': 'delta_w', 'delta_ln2_g': 'delta_w', 'delta_ln2_b': 'delta_w', 'new_m_w_in': 'new_m', 'new_m_lb_logits': 'new_m', 'new_m_a_norm_w': 'new_m', 'new_m_c_sinks': 'new_m', 'new_m_w_out': 'new_m', 'new_m_ln1_g': 'new_m', 'new_m_ln1_b': 'new_m', 'new_m_w_gate': 'new_m', 'new_m_w_up': 'new_m', 'new_m_conv_w': 'new_m', 'new_m_conv_b': 'new_m', 'new_m_w_down': 'new_m', 'new_m_ln2_g': 'new_m', 'new_m_ln2_b': 'new_m', 'new_v_w_in': 'new_v', 'new_v_lb_logits': 'new_v', 'new_v_a_norm_w': 'new_v', 'new_v_c_sinks': 'new_v', 'new_v_w_out': 'new_v', 'new_v_ln1_g': 'new_v', 'new_v_ln1_b': 'new_v', 'new_v_w_gate': 'new_v', 'new_v_w_up': 'new_v', 'new_v_conv_w': 'new_v', 'new_v_conv_b': 'new_v', 'new_v_w_down': 'new_v', 'new_v_ln2_g': 'new_v', 'new_v_ln2_b': 'new_v'}


def _forward(args):
    return _fwd_reference(*[args[k] for k in FWD_PARAMS])


def _output_shape():
    out = _jax.eval_shape(lambda: _forward(_fwd_setup_inputs(0)))
    return out.shape, out.dtype

N_MICROBATCH = 1
ADAM_LR = 0.001
ADAM_B1 = 0.9
ADAM_B2 = 0.999
ADAM_EPS = 1e-08
ADAM_WD = 0.01
ADAM_STEP = 10
PER_EXAMPLE_BATCH_AXIS = {'x': 0, 'loss_target': 0}
SHARED_INPUTS = []
_WEIGHT_DTYPES = {'w_in': _jnp.float32, 'lb_logits': _jnp.float32, 'a_norm_w': _jnp.float32, 'c_sinks': _jnp.float32, 'w_out': _jnp.float32, 'ln1_g': _jnp.float32, 'ln1_b': _jnp.float32, 'w_gate': _jnp.float32, 'w_up': _jnp.float32, 'conv_w': _jnp.float32, 'conv_b': _jnp.float32, 'w_down': _jnp.float32, 'ln2_g': _jnp.float32, 'ln2_b': _jnp.float32}
MOMENT_SCALE = {'w_in': 9.567082e-03, 'lb_logits': 1.115428e-03, 'a_norm_w': 2.753221e-02, 'c_sinks': 3.351644e-03, 'w_out': 1.400142e-02, 'ln1_g': 2.720179e-01, 'ln1_b': 1.470713e-01, 'w_gate': 4.170132e-03, 'w_up': 8.111540e-03, 'conv_w': 4.256200e-03, 'conv_b': 4.047647e-03, 'w_down': 1.346642e-02, 'ln2_g': 5.673188e+00, 'ln2_b': 2.209858e-01}


def _to_microbatches(a, axis):
    t = _jnp.moveaxis(a, axis, 0)
    t = t.reshape((N_MICROBATCH, t.shape[0] // N_MICROBATCH) + t.shape[1:])
    return _jnp.moveaxis(t, 1, axis + 1)


def setup_inputs(seed: int = 0) -> dict:
    inp = _fwd_setup_inputs(seed)
    key = _jax.random.fold_in(_jax.random.key(seed), 7919)
    shape, _ = _output_shape()
    out = dict(inp)
    out["loss_target"] = _jax.random.normal(_jax.random.fold_in(key, 0), shape, _jnp.float32)
    for i, name in enumerate(TWIN_WEIGHTS):
        w = inp[name].astype(_jnp.float32)
        if MOMENT_SCALE is None:
            s = _jnp.sqrt(_jnp.mean(_jnp.square(w)) + 1e-30)
        else:
            s = MOMENT_SCALE[name]
        km, kv = _jax.random.split(_jax.random.fold_in(key, i + 1))
        out[name] = w
        out["m_" + name] = s * _jax.random.normal(km, w.shape, _jnp.float32)
        out["v_" + name] = (s * s) * _jax.random.uniform(kv, w.shape, _jnp.float32, 0.5, 1.5)
    if N_MICROBATCH > 1:
        for name, axis in PER_EXAMPLE_BATCH_AXIS.items():
            out[name] = _to_microbatches(out[name], axis)
    return {'x': out['x'], 'w_in': out['w_in'], 'lb_logits': out['lb_logits'], 'a_norm_w': out['a_norm_w'], 'c_sinks': out['c_sinks'], 'w_out': out['w_out'], 'ln1_g': out['ln1_g'], 'ln1_b': out['ln1_b'], 'w_gate': out['w_gate'], 'w_up': out['w_up'], 'conv_w': out['conv_w'], 'conv_b': out['conv_b'], 'w_down': out['w_down'], 'ln2_g': out['ln2_g'], 'ln2_b': out['ln2_b'], 'loss_target': out['loss_target'], 'm_w_in': out['m_w_in'], 'm_lb_logits': out['m_lb_logits'], 'm_a_norm_w': out['m_a_norm_w'], 'm_c_sinks': out['m_c_sinks'], 'm_w_out': out['m_w_out'], 'm_ln1_g': out['m_ln1_g'], 'm_ln1_b': out['m_ln1_b'], 'm_w_gate': out['m_w_gate'], 'm_w_up': out['m_w_up'], 'm_conv_w': out['m_conv_w'], 'm_conv_b': out['m_conv_b'], 'm_w_down': out['m_w_down'], 'm_ln2_g': out['m_ln2_g'], 'm_ln2_b': out['m_ln2_b'], 'v_w_in': out['v_w_in'], 'v_lb_logits': out['v_lb_logits'], 'v_a_norm_w': out['v_a_norm_w'], 'v_c_sinks': out['v_c_sinks'], 'v_w_out': out['v_w_out'], 'v_ln1_g': out['v_ln1_g'], 'v_ln1_b': out['v_ln1_b'], 'v_w_gate': out['v_w_gate'], 'v_w_up': out['v_w_up'], 'v_conv_w': out['v_conv_w'], 'v_conv_b': out['v_conv_b'], 'v_w_down': out['v_w_down'], 'v_ln2_g': out['v_ln2_g'], 'v_ln2_b': out['v_ln2_b']}


def _loss(weights, diff, rest, loss_target):
    with _jax.named_scope("forward"):
        args = {**rest, TWIN_DIFF_INPUT: diff, **{k: w.astype(_WEIGHT_DTYPES[k]) for k, w in weights.items()}}
        y = _forward(args)
    with _jax.named_scope("loss_head"):
        err = _jnp.square(y.astype(_jnp.float32) - loss_target)
        return 0.5 * _jnp.sum(_jnp.mean(err, axis=-1)) if err.ndim else 0.5 * err


def _adamw(w, g, m, v):
    m = ADAM_B1 * m + (1.0 - ADAM_B1) * g
    v = ADAM_B2 * v + (1.0 - ADAM_B2) * _jnp.square(g)
    m_hat = m / (1.0 - ADAM_B1 ** ADAM_STEP)
    v_hat = v / (1.0 - ADAM_B2 ** ADAM_STEP)
    delta = -ADAM_LR * (m_hat / (_jnp.sqrt(v_hat) + ADAM_EPS) + ADAM_WD * w)
    return delta, m, v


def reference(x, w_in, lb_logits, a_norm_w, c_sinks, w_out, ln1_g, ln1_b, w_gate, w_up, conv_w, conv_b, w_down, ln2_g, ln2_b, loss_target, m_w_in, m_lb_logits, m_a_norm_w, m_c_sinks, m_w_out, m_ln1_g, m_ln1_b, m_w_gate, m_w_up, m_conv_w, m_conv_b, m_w_down, m_ln2_g, m_ln2_b, v_w_in, v_lb_logits, v_a_norm_w, v_c_sinks, v_w_out, v_ln1_g, v_ln1_b, v_w_gate, v_w_up, v_conv_w, v_conv_b, v_w_down, v_ln2_g, v_ln2_b):
    given = dict(x=x, w_in=w_in, lb_logits=lb_logits, a_norm_w=a_norm_w, c_sinks=c_sinks, w_out=w_out, ln1_g=ln1_g, ln1_b=ln1_b, w_gate=w_gate, w_up=w_up, conv_w=conv_w, conv_b=conv_b, w_down=w_down, ln2_g=ln2_g, ln2_b=ln2_b, loss_target=loss_target, m_w_in=m_w_in, m_lb_logits=m_lb_logits, m_a_norm_w=m_a_norm_w, m_c_sinks=m_c_sinks, m_w_out=m_w_out, m_ln1_g=m_ln1_g, m_ln1_b=m_ln1_b, m_w_gate=m_w_gate, m_w_up=m_w_up, m_conv_w=m_conv_w, m_conv_b=m_conv_b, m_w_down=m_w_down, m_ln2_g=m_ln2_g, m_ln2_b=m_ln2_b, v_w_in=v_w_in, v_lb_logits=v_lb_logits, v_a_norm_w=v_a_norm_w, v_c_sinks=v_c_sinks, v_w_out=v_w_out, v_ln1_g=v_ln1_g, v_ln1_b=v_ln1_b, v_w_gate=v_w_gate, v_w_up=v_w_up, v_conv_w=v_conv_w, v_conv_b=v_conv_b, v_w_down=v_w_down, v_ln2_g=v_ln2_g, v_ln2_b=v_ln2_b)
    weights = {n: given[n] for n in TWIN_WEIGHTS}
    shared = {n: given[n] for n in SHARED_INPUTS}
    per_example = {n: given[n] for n in ['x']}
    grad_fn = _jax.value_and_grad(_loss, argnums=(0, 1))

    def one_microbatch(ex, loss_target):
        ex = dict(ex)
        diff = ex.pop(TWIN_DIFF_INPUT)
        return grad_fn(weights, diff, {**shared, **ex}, loss_target)

    if N_MICROBATCH == 1:
        loss, (grad_w, grad_x) = one_microbatch(per_example, given["loss_target"])
    else:
        def body(carry, xs):
            loss_sum, grad_sum = carry
            l_k, (gw_k, gx_k) = one_microbatch(xs[0], xs[1])
            with _jax.named_scope("update"):
                return (loss_sum + l_k, _jax.tree.map(_jnp.add, grad_sum, gw_k)), gx_k

        init = (_jnp.zeros((), _jnp.float32), _jax.tree.map(_jnp.zeros_like, weights))
        (loss, grad_w), grad_x = _jax.lax.scan(body, init, (per_example, given["loss_target"]))
    with _jax.named_scope("update"):
        delta_w, new_m, new_v = {}, {}, {}
        for n in TWIN_WEIGHTS:
            delta_w[n], new_m[n], new_v[n] = _adamw(weights[n], grad_w[n], given["m_" + n], given["v_" + n])
    return (loss, grad_x, *[grad_w[n] for n in TWIN_WEIGHTS], *[delta_w[n] for n in TWIN_WEIGHTS],
            *[new_m[n] for n in TWIN_WEIGHTS], *[new_v[n] for n in TWIN_WEIGHTS])
```

```python
import functools
import math

import jax
import jax.numpy as jnp
from jax import lax
from jax.experimental import pallas as pl
from jax.experimental.pallas import tpu as pltpu

F32 = jnp.float32
BF16 = jnp.bfloat16

D_MODEL = 2048
SEQ = 2048
DEPTH = 2
HEAD_DIM = 128
A_HEADS = 4
B_HEADS = 6
C_HEADS = 6
C_KV_HEADS = 2
A_CHUNK = 16
DILATED_PATTERNS = ((128, 1), (512, 4), (2048, 16))
C_WINDOW = 128
ROPE_THETA = 500000.0
ROPE_DIM = HEAD_DIM // 4
D_FF = 5632
CONV_WIDTH = 3
LN_EPS = 1e-5
ALPHA = (2 * DEPTH) ** 0.25
IN_WIDTH = 5632
MIX_WIDTH = 2048
ADAM_LR = 0.001
ADAM_B1 = 0.9
ADAM_B2 = 0.999
ADAM_EPS = 1e-08
ADAM_WD = 0.01
ADAM_STEP = 10

N_CHIPS = 4
N_DEV = 8
FF_SHARD = D_FF // N_CHIPS
OUT_SHARD = MIX_WIDTH // N_CHIPS
BLK = 128
N_CHUNK = SEQ // A_CHUNK
SLAB = 32

QA0, FA0, IA0, GA0 = 0, 4, 8, 12
QB0, KB0, VB0 = 16, 22, 28
QC0, KC0, VC0 = 34, 40, 42

VMEM_LIMIT_V7X = 56 * 1024 * 1024
HI = lax.Precision.HIGHEST
MESH = pl.DeviceIdType.MESH


def _cp(sem=None, vmem=VMEM_LIMIT_V7X, **kw):
    return pltpu.CompilerParams(dimension_semantics=sem, vmem_limit_bytes=vmem, **kw)


def _sigmoid(x):
    return 1.0 / (1.0 + jnp.exp(-x))


def _mm(name, pairs, dims, grid, a_specs, b_specs, out_spec, out_shape, nk=1, acc_shape=None):
    n_pairs = len(pairs)

    def body(*refs):
        o_ref = refs[2 * n_pairs]
        part = None
        for p in range(n_pairs):
            a = refs[2 * p][...].astype(BF16)
            b = refs[2 * p + 1][...].astype(BF16)
            t = lax.dot_general(a, b, dims, preferred_element_type=F32)
            part = t if part is None else part + t
        if nk == 1:
            o_ref[...] = part.astype(o_ref.dtype)
        else:
            acc = refs[2 * n_pairs + 1]
            k = pl.program_id(len(grid) - 1)

            @pl.when(k == 0)
            def _():
                acc[...] = part

            @pl.when(k > 0)
            def _():
                acc[...] += part

            @pl.when(k == nk - 1)
            def _():
                o_ref[...] = acc[...].astype(o_ref.dtype)

    in_specs, args = [], []
    for (a, b), sa, sb in zip(pairs, a_specs, b_specs):
        in_specs += [sa, sb]
        args += [a, b]
    sem = ("parallel",) * (len(grid) - (1 if nk > 1 else 0)) + (("arbitrary",) if nk > 1 else ())
    return pl.pallas_call(
        body, name=name, grid=grid, in_specs=in_specs, out_specs=out_spec, out_shape=out_shape,
        scratch_shapes=[pltpu.VMEM(acc_shape, F32)] if nk > 1 else [],
        compiler_params=_cp(sem),
    )(*args)


NN = (((1,), (0,)), ((), ()))
NT = (((1,), (1,)), ((), ()))
TN = (((0,), (0,)), ((), ()))
TM = 1024


def _fwd_colsharded(name, x, w_stk, layer):
    return _mm(name, [(x, w_stk)], NN, (N_CHIPS, SEQ // TM),
               [pl.BlockSpec((TM, D_MODEL), lambda j, i: (i, 0))],
               [pl.BlockSpec((None, None, D_MODEL, FF_SHARD), lambda j, i: (j, layer, 0, 0))],
               pl.BlockSpec((TM, FF_SHARD), lambda j, i: (i, j)),
               jax.ShapeDtypeStruct((SEQ, D_FF), F32))


def _fwd_rowsharded(name, a, w_stk, layer, shard):
    tn = 1024
    return _mm(name, [(a, w_stk)], NN, (SEQ // TM, D_MODEL // tn, N_CHIPS),
               [pl.BlockSpec((TM, shard), lambda i, j, k: (i, k))],
               [pl.BlockSpec((None, None, shard, tn), lambda i, j, k: (k, layer, 0, j))],
               pl.BlockSpec((TM, tn), lambda i, j, k: (i, j)),
               jax.ShapeDtypeStruct((SEQ, D_MODEL), F32), nk=N_CHIPS, acc_shape=(TM, tn))


def _bwd_act_colsharded(name, pairs, layer):
    tn = 1024
    n = len(pairs)
    return _mm(name, pairs, NT, (SEQ // TM, D_MODEL // tn, N_CHIPS),
               [pl.BlockSpec((TM, FF_SHARD), lambda i, j, k: (i, k))] * n,
               [pl.BlockSpec((None, None, tn, FF_SHARD), lambda i, j, k: (k, layer, j, 0))] * n,
               pl.BlockSpec((TM, tn), lambda i, j, k: (i, j)),
               jax.ShapeDtypeStruct((SEQ, D_MODEL), F32), nk=N_CHIPS, acc_shape=(TM, tn))


def _bwd_act_rowsharded(name, dy, w_stk, layer, shard):
    return _mm(name, [(dy, w_stk)], NT, (N_CHIPS, SEQ // TM),
               [pl.BlockSpec((TM, D_MODEL), lambda j, i: (i, 0))],
               [pl.BlockSpec((None, None, shard, D_MODEL), lambda j, i: (j, layer, 0, 0))],
               pl.BlockSpec((TM, shard), lambda j, i: (i, j)),
               jax.ShapeDtypeStruct((SEQ, N_CHIPS * shard), F32))


def _bwd_w_colsharded(name, x, dy):
    tm = 512
    return _mm(name, [(x, dy)], TN, (N_CHIPS, D_MODEL // tm),
               [pl.BlockSpec((SEQ, tm), lambda j, i: (0, i))],
               [pl.BlockSpec((SEQ, FF_SHARD), lambda j, i: (0, j))],
               pl.BlockSpec((None, tm, FF_SHARD), lambda j, i: (j, i, 0)),
               jax.ShapeDtypeStruct((N_CHIPS, D_MODEL, FF_SHARD), F32))


def _bwd_w_rowsharded(name, a, dy, shard):
    tn = 1024
    return _mm(name, [(a, dy)], TN, (N_CHIPS, D_MODEL // tn),
               [pl.BlockSpec((SEQ, shard), lambda j, i: (0, j))],
               [pl.BlockSpec((SEQ, tn), lambda j, i: (0, i))],
               pl.BlockSpec((None, shard, tn), lambda j, i: (j, 0, i)),
               jax.ShapeDtypeStruct((N_CHIPS, shard, D_MODEL), F32))


TR = 256


def _ln_fwd(name, x, y, g, b):
    def body(x_ref, y_ref, g_ref, b_ref, o_ref):
        z = ALPHA * x_ref[...] + y_ref[...]
        mu = jnp.mean(z, -1, keepdims=True)
        zc = z - mu
        var = jnp.mean(zc * zc, -1, keepdims=True)
        o_ref[...] = zc * lax.rsqrt(var + LN_EPS) * g_ref[...] + b_ref[...]

    row = pl.BlockSpec((TR, D_MODEL), lambda i: (i, 0))
    vec = pl.BlockSpec((1, D_MODEL), lambda i: (0, 0))
    return pl.pallas_call(body, name=name, grid=(SEQ // TR,), in_specs=[row, row, vec, vec], out_specs=row,
                          out_shape=jax.ShapeDtypeStruct((SEQ, D_MODEL), F32),
                          compiler_params=_cp(("parallel",)))(x, y, g.reshape(1, -1), b.reshape(1, -1))


def _ln_bwd(name, x, y, g, d_res, d_path):
    has_res = d_res is not None

    def body(*refs):
        if has_res:
            x_ref, y_ref, g_ref, r_ref, p_ref, dz_ref, dg_ref, db_ref = refs
            dout = ALPHA * r_ref[...] + p_ref[...]
        else:
            x_ref, y_ref, g_ref, p_ref, dz_ref, dg_ref, db_ref = refs
            dout = p_ref[...]
        z = ALPHA * x_ref[...] + y_ref[...]
        mu = jnp.mean(z, -1, keepdims=True)
        zc = z - mu
        rstd = lax.rsqrt(jnp.mean(zc * zc, -1, keepdims=True) + LN_EPS)
        zh = zc * rstd
        dzh = dout * g_ref[...]
        dz_ref[...] = rstd * (dzh - jnp.mean(dzh, -1, keepdims=True) - zh * jnp.mean(dzh * zh, -1, keepdims=True))
        pg = jnp.sum(dout * zh, 0, keepdims=True)
        pb = jnp.sum(dout, 0, keepdims=True)

        @pl.when(pl.program_id(0) == 0)
        def _():
            dg_ref[...] = pg
            db_ref[...] = pb

        @pl.when(pl.program_id(0) > 0)
        def _():
            dg_ref[...] += pg
            db_ref[...] += pb

    row = pl.BlockSpec((TR, D_MODEL), lambda i: (i, 0))
    vec = pl.BlockSpec((1, D_MODEL), lambda i: (0, 0))
    args = [x, y, g.reshape(1, -1)] + ([d_res] if has_res else []) + [d_path]
    in_specs = [row, row, vec] + ([row] if has_res else []) + [row]
    vshape = jax.ShapeDtypeStruct((1, D_MODEL), F32)
    return pl.pallas_call(body, name=name, grid=(SEQ // TR,), in_specs=in_specs, out_specs=[row, vec, vec],
                          out_shape=[jax.ShapeDtypeStruct((SEQ, D_MODEL), F32), vshape, vshape],
                          compiler_params=_cp(("arbitrary",)))(*args)


def _loss_head(y, target):
    def body(y_ref, t_ref, dy_ref, l_ref):
        e = y_ref[...] - t_ref[...]
        dy_ref[...] = e * (1.0 / D_MODEL)
        part = jnp.full((8, BLK), 0.5 / D_MODEL * jnp.sum(e * e), F32)

        @pl.when(pl.program_id(0) == 0)
        def _():
            l_ref[...] = part

        @pl.when(pl.program_id(0) > 0)
        def _():
            l_ref[...] += part

    row = pl.BlockSpec((TR, D_MODEL), lambda i: (i, 0))
    return pl.pallas_call(body, name="loss_head", grid=(SEQ // TR,), in_specs=[row, row],
                          out_specs=[row, pl.BlockSpec((8, BLK), lambda i: (0, 0))],
                          out_shape=[jax.ShapeDtypeStruct((SEQ, D_MODEL), F32), jax.ShapeDtypeStruct((8, BLK), F32)],
                          compiler_params=_cp(("arbitrary",)))(y, target)


def _axpy(name, a, b):
    def body(a_ref, b_ref, o_ref):
        o_ref[...] = ALPHA * a_ref[...] + b_ref[...]

    row = pl.BlockSpec((TR, D_MODEL), lambda i: (i, 0))
    return pl.pallas_call(body, name=name, grid=(SEQ // TR,), in_specs=[row, row], out_specs=row,
                          out_shape=jax.ShapeDtypeStruct((SEQ, D_MODEL), F32),
                          compiler_params=_cp(("parallel",)))(a, b)


TC = 512


def _shift_down(x, s, rows):
    if s == 0:
        return x
    return jnp.where(rows >= s, pltpu.roll(x, s, axis=0), 0.0)


def _shift_up(x, s, rows):
    if s == 0:
        return x
    return jnp.where(rows < SEQ - s, pltpu.roll(x, SEQ - s, axis=0), 0.0)


def _conv_gate_fwd(name, g, u, cw, cb):
    def body(g_ref, u_ref, w_ref, b_ref, h_ref):
        gg = g_ref[...]
        rows = lax.broadcasted_iota(jnp.int32, gg.shape, 0)
        gc = b_ref[...] + w_ref[2:3, :] * gg
        gc = gc + w_ref[1:2, :] * _shift_down(gg, 1, rows)
        gc = gc + w_ref[0:1, :] * _shift_down(gg, 2, rows)
        h_ref[...] = (gc * _sigmoid(gc) * u_ref[...]).astype(BF16)

    col = pl.BlockSpec((SEQ, TC), lambda j: (0, j))
    return pl.pallas_call(body, name=name, grid=(D_FF // TC,),
                          in_specs=[col, col, pl.BlockSpec((CONV_WIDTH, TC), lambda j: (0, j)),
                                    pl.BlockSpec((1, TC), lambda j: (0, j))],
                          out_specs=col, out_shape=jax.ShapeDtypeStruct((SEQ, D_FF), BF16),
                          compiler_params=_cp(("parallel",)))(g, u, cw, cb.reshape(1, -1))


def _conv_gate_bwd(name, g, u, cw, cb, dh):
    def body(g_ref, u_ref, w_ref, b_ref, dh_ref, dg_ref, du_ref, dw_ref, db_ref):
        gg = g_ref[...]
        rows = lax.broadcasted_iota(jnp.int32, gg.shape, 0)
        g1 = _shift_down(gg, 1, rows)
        g2 = _shift_down(gg, 2, rows)
        gc = b_ref[...] + w_ref[2:3, :] * gg + w_ref[1:2, :] * g1 + w_ref[0:1, :] * g2
        sg = _sigmoid(gc)
        act = gc * sg
        dh = dh_ref[...]
        du_ref[...] = dh * act
        dgc = dh * u_ref[...] * (sg * (1.0 + gc * (1.0 - sg)))
        db_ref[...] = jnp.sum(dgc, 0, keepdims=True)
        dw_ref[2:3, :] = jnp.sum(dgc * gg, 0, keepdims=True)
        dw_ref[1:2, :] = jnp.sum(dgc * g1, 0, keepdims=True)
        dw_ref[0:1, :] = jnp.sum(dgc * g2, 0, keepdims=True)
        dg_ref[...] = (w_ref[2:3, :] * dgc + w_ref[1:2, :] * _shift_up(dgc, 1, rows)
                       + w_ref[0:1, :] * _shift_up(dgc, 2, rows))

    col = pl.BlockSpec((SEQ, TC), lambda j: (0, j))
    w3 = pl.BlockSpec((CONV_WIDTH, TC), lambda j: (0, j))
    w1 = pl.BlockSpec((1, TC), lambda j: (0, j))
    big = jax.ShapeDtypeStruct((SEQ, D_FF), F32)
    return pl.pallas_call(body, name=name, grid=(D_FF // TC,), in_specs=[col, col, w3, w1, col],
                          out_specs=[col, col, w3, w1],
                          out_shape=[big, big, jax.ShapeDtypeStruct((CONV_WIDTH, D_FF), F32),
                                     jax.ShapeDtypeStruct((1, D_FF), F32)],
                          compiler_params=_cp(("parallel",)))(g, u, cw, cb.reshape(1, -1), dh)


def _lbs_of(logits, layer):
    m = jnp.max(logits, 0, keepdims=True)
    e = jnp.exp(logits - m)
    p = e / jnp.sum(e, 0, keepdims=True)
    lb = jnp.zeros((1, BLK), F32)
    for r in range(1, layer + 1):
        lb = lb + p[r:r + 1, :]
    return lb, p


def _dlogits_of(p, dlb, layer):
    rows = lax.broadcasted_iota(jnp.int32, p.shape, 0)
    dp = jnp.where((rows >= 1) & (rows <= layer), dlb, 0.0)
    return p * (dp - jnp.sum(p * dp, 0, keepdims=True))


SROWS = SLAB * A_CHUNK
N_SLAB = N_CHUNK // SLAB


def _chunk_prefix(x, rowi):
    for s in (1, 2, 4, 8):
        x = x + jnp.where(rowi >= s, pltpu.roll(x, s, axis=0), 0.0)
    return x


def _chunk_suffix(x, rowi):
    for s in (1, 2, 4, 8):
        x = x + jnp.where(rowi < A_CHUNK - s, pltpu.roll(x, SROWS - s, axis=0), 0.0)
    return x


def _c3(x):
    return x.reshape(SLAB, A_CHUNK, BLK)


def _c2(x):
    return x.reshape(SROWS, BLK)


def _lane_sum_b(x2, ones):
    return jnp.dot(x2, ones, preferred_element_type=F32, precision=HI)


def _bmm(eq, a, b):
    return jnp.einsum(eq, a, b, preferred_element_type=F32, precision=HI)


def _slab_rows(s):
    return pl.ds(s * SROWS, SROWS)


def _hgrn_prep(q, f, lb):
    rowi = lax.broadcasted_iota(jnp.int32, (SROWS, BLK), 0) & (A_CHUNK - 1)
    sq = _sigmoid(q)
    qc = q * sq
    sf = _sigmoid(f)
    fg = lb + (1.0 - lb) * sf
    kc = 1.0 - fg
    b = _chunk_prefix(jnp.log(fg), rowi)
    b3 = _c3(b)
    blast = b3[:, A_CHUNK - 1:A_CHUNK, :]
    eb = jnp.exp(b)
    ekb = _c2(jnp.exp(blast - b3))
    dec = jnp.exp(blast.reshape(SLAB, BLK))
    return rowi, sq, qc, sf, fg, kc, b, eb, ekb, dec


def _hgrn_slab_states(s, carry, v, ke, dec, dec_ref, u_ref, st_ref):
    dec_ref[pl.ds(s * SLAB, SLAB), :] = dec
    u_ref[...] = _bmm('ncv,nck->nvk', _c3(v), _c3(ke))

    def step(j, c):
        st_ref[j] = c
        return dec_ref[pl.ds(s * SLAB + j, 1), :] * c + u_ref[j]

    return lax.fori_loop(0, SLAB, step, carry)


def _hgrn_fwd(name, proj, lb_logits, nw, layer):
    def body(q_ref, f_ref, i_ref, g_ref, lg_ref, nw_ref, out_ref, raw_ref, dec_ref, u_ref, st_ref):
        lb, _ = _lbs_of(lg_ref[...], layer)
        ones = jnp.ones((BLK, BLK), F32)
        carry = jnp.zeros((BLK, BLK), F32)
        for s in range(N_SLAB):
            rows = _slab_rows(s)
            v = i_ref[rows, :]
            rowi, sq, qc, sf, fg, kc, b, eb, ekb, dec = _hgrn_prep(q_ref[rows, :], f_ref[rows, :], lb)
            carry = _hgrn_slab_states(s, carry, v, kc * ekb, dec, dec_ref, u_ref, st_ref)
            o = _c2(_bmm('nck,nvk->ncv', _c3(qc * eb), st_ref[...]))
            qc3, kc3, b3, v3, row3 = _c3(qc), _c3(kc), _c3(b), _c3(v), _c3(rowi)
            for j in range(A_CHUNK):
                dj = jnp.exp(jnp.where(row3 >= j, b3 - b3[:, j:j + 1, :], -jnp.inf))
                a = _lane_sum_b(_c2(qc3 * dj * kc3[:, j:j + 1, :]), ones)
                o = o + a * _c2(jnp.broadcast_to(v3[:, j:j + 1, :], v3.shape))
            raw_ref[rows, :] = o
            r = lax.rsqrt(jnp.mean(o * o, -1, keepdims=True) + LN_EPS)
            gg = g_ref[rows, :]
            out_ref[rows, :] = o * r * nw_ref[...] * (gg * _sigmoid(gg))

    def colblk(c0):
        return pl.BlockSpec((SEQ, BLK), lambda h: (0, c0 + h))

    big = jax.ShapeDtypeStruct((SEQ, A_HEADS * BLK), F32)
    return pl.pallas_call(
        body, name=name, grid=(A_HEADS,),
        in_specs=[colblk(QA0), colblk(FA0), colblk(IA0), colblk(GA0),
                  pl.BlockSpec((DEPTH, BLK), lambda h: (0, h)), pl.BlockSpec((1, BLK), lambda h: (0, 0))],
        out_specs=[colblk(0), colblk(0)], out_shape=[big, big],
        scratch_shapes=[pltpu.VMEM((N_CHUNK, BLK), F32), pltpu.VMEM((SLAB, BLK, BLK), F32),
                        pltpu.VMEM((SLAB, BLK, BLK), F32)],
        compiler_params=_cp(("parallel",)))(proj, proj, proj, proj, lb_logits, nw.reshape(1, -1))


def _hgrn_bwd(name, proj, raw, dmix, lb_logits, nw, layer):
    def body(q_ref, f_ref, i_ref, g_ref, raw_ref, do_ref, lg_ref, nw_ref,
             dq_ref, df_ref, di_ref, dg_ref, dnw_ref, dlg_ref,
             dec_ref, u_ref, st_ref, h_ref, dbs_ref, dkc_ref, tot_ref):
        lb, p = _lbs_of(lg_ref[...], layer)
        ones = jnp.ones((BLK, BLK), F32)
        nwv = nw_ref[...]

        carry = jnp.zeros((BLK, BLK), F32)
        for s in range(N_SLAB):
            rows = _slab_rows(s)
            rowi, sq, qc, sf, fg, kc, b, eb, ekb, dec = _hgrn_prep(q_ref[rows, :], f_ref[rows, :], lb)
            carry = _hgrn_slab_states(s, carry, i_ref[rows, :], kc * ekb, dec, dec_ref, u_ref,
                                      st_ref.at[pl.ds(s * SLAB, SLAB)])

        carry = jnp.zeros((BLK, BLK), F32)
        dnw = jnp.zeros((1, BLK), F32)
        for s in reversed(range(N_SLAB)):
            rows = _slab_rows(s)
            q, v = q_ref[rows, :], i_ref[rows, :]
            rowi, sq, qc, sf, fg, kc, b, eb, ekb, dec = _hgrn_prep(q, f_ref[rows, :], lb)
            ke = kc * ekb
            qe = qc * eb

            o = raw_ref[rows, :]
            gg = g_ref[rows, :]
            sgg = _sigmoid(gg)
            dout = do_ref[rows, :]
            r = lax.rsqrt(jnp.mean(o * o, -1, keepdims=True) + LN_EPS)
            oh = o * r
            dg_ref[rows, :] = dout * oh * nwv * (sgg * (1.0 + gg * (1.0 - sgg)))
            dn = dout * (gg * sgg)
            dnw = dnw + jnp.sum(dn * oh, 0, keepdims=True)
            doh = dn * nwv
            do = r * (doh - oh * jnp.mean(doh * oh, -1, keepdims=True))
            do3, qe3, v3, ke3 = _c3(do), _c3(qe), _c3(v), _c3(ke)

            u_ref[...] = _bmm('ncv,nck->nvk', do3, qe3)

            def step(jj, c, s=s):
                j = SLAB - 1 - jj
                h_ref[j] = c
                return u_ref[j] + dec_ref[pl.ds(s * SLAB + j, 1), :] * c

            carry = lax.fori_loop(0, SLAB, step, carry)

            hh = h_ref[...]
            dqc = _c2(_bmm('ncv,nvk->nck', do3, st_ref[pl.ds(s * SLAB, SLAB)])) * eb
            dkc = _c2(_bmm('ncv,nvk->nck', v3, hh)) * ekb
            dv = _c2(_bmm('nck,nvk->ncv', ke3, hh))

            qc3, kc3, b3, row3 = _c3(qc), _c3(kc), _c3(b), _c3(rowi)
            for j in range(A_CHUNK):
                dj = jnp.exp(jnp.where(row3 >= j, b3 - b3[:, j:j + 1, :], -jnp.inf))
                kj = kc3[:, j:j + 1, :]
                vj = jnp.broadcast_to(v3[:, j:j + 1, :], v3.shape)
                att = _c3(_lane_sum_b(_c2(qc3 * dj * kj), ones))
                datt = _c3(_lane_sum_b(_c2(do3 * vj), ones))
                md = dj * datt
                dqc = dqc + _c2(md * kj)
                sel = row3 == j
                dkc = dkc + _c2(jnp.where(sel, jnp.sum(md * qc3, 1, keepdims=True), 0.0))
                dv = dv + _c2(jnp.where(sel, jnp.sum(att * do3, 1, keepdims=True), 0.0))
            di_ref[rows, :] = dv
            dq_ref[rows, :] = dqc * (sq * (1.0 + q * (1.0 - sq)))

            dbs = _chunk_suffix(qc * dqc - kc * dkc, rowi)
            dbs_ref[rows, :] = dbs
            dkc_ref[rows, :] = dkc
            tot_ref[pl.ds(s * SLAB, SLAB), :] = _c3(dbs)[:, 0:1, :].reshape(SLAB, BLK)
        dnw_ref[...] = jnp.broadcast_to(dnw, (8, BLK))

        rn = lax.broadcasted_iota(jnp.int32, (N_CHUNK, N_CHUNK), 0)
        cn = lax.broadcasted_iota(jnp.int32, (N_CHUNK, N_CHUNK), 1)
        tot_ref[...] = jnp.dot((cn > rn).astype(F32), tot_ref[...], preferred_element_type=F32, precision=HI)
        dlb = jnp.zeros((1, BLK), F32)
        for s in range(N_SLAB):
            rows = _slab_rows(s)
            sf = _sigmoid(f_ref[rows, :])
            fg = lb + (1.0 - lb) * sf
            later = tot_ref[pl.ds(s * SLAB, SLAB), :]
            dlg = _c2(_c3(dbs_ref[rows, :]) + later[:, None, :])
            dfg = dlg / fg - dkc_ref[rows, :]
            df_ref[rows, :] = dfg * (1.0 - lb) * sf * (1.0 - sf)
            dlb = dlb + jnp.sum(dfg * (1.0 - sf), 0, keepdims=True)
        dlg_ref[...] = _dlogits_of(p, dlb, layer)

    def colblk(c0):
        return pl.BlockSpec((SEQ, BLK), lambda h: (0, c0 + h))

    big = jax.ShapeDtypeStruct((SEQ, A_HEADS * BLK), F32)
    return pl.pallas_call(
        body, name=name, grid=(A_HEADS,),
        in_specs=[colblk(QA0), colblk(FA0), colblk(IA0), colblk(GA0), colblk(0), colblk(0),
                  pl.BlockSpec((DEPTH, BLK), lambda h: (0, h)), pl.BlockSpec((1, BLK), lambda h: (0, 0))],
        out_specs=[colblk(0), colblk(0), colblk(0), colblk(0),
                   pl.BlockSpec((8, BLK), lambda h: (h, 0)), pl.BlockSpec((DEPTH, BLK), lambda h: (0, h))],
        out_shape=[big, big, big, big, jax.ShapeDtypeStruct((A_HEADS * 8, BLK), F32),
                   jax.ShapeDtypeStruct((DEPTH, A_HEADS * BLK), F32)],
        scratch_shapes=[pltpu.VMEM((N_CHUNK, BLK), F32), pltpu.VMEM((SLAB, BLK, BLK), F32),
                        pltpu.VMEM((N_CHUNK, BLK, BLK), F32), pltpu.VMEM((SLAB, BLK, BLK), F32),
                        pltpu.VMEM((SEQ, BLK), F32), pltpu.VMEM((SEQ, BLK), F32), pltpu.VMEM((N_CHUNK, BLK), F32)],
        compiler_params=_cp(("parallel",)))(proj, proj, proj, proj, raw, dmix, lb_logits, nw.reshape(1, -1))


SCALE = HEAD_DIM ** -0.5


def _rope_tables():
    half = ROPE_DIM // 2
    inv = ROPE_THETA ** (-jnp.arange(0, ROPE_DIM, 2, dtype=F32) / ROPE_DIM)
    ang = jnp.arange(SEQ, dtype=F32)[:, None] * inv[None, :]
    cos, sin = jnp.cos(ang), jnp.sin(ang)
    pad = jnp.zeros((SEQ, HEAD_DIM - ROPE_DIM), F32)
    zero = jnp.zeros((SEQ, half), F32)
    c = jnp.concatenate([cos, cos, pad + 1.0], 1)
    s_lo = jnp.concatenate([zero, sin, pad], 1)
    s_hi = jnp.concatenate([-sin, zero, pad], 1)
    return c, s_lo, s_hi


def _rope(x, c, s_lo, s_hi):
    half = ROPE_DIM // 2
    return x * c + pltpu.roll(x, half, axis=1) * s_lo + pltpu.roll(x, HEAD_DIM - half, axis=1) * s_hi


def _unrope(dy, c, s_lo, s_hi):
    half = ROPE_DIM // 2
    return dy * c + pltpu.roll(dy * s_lo, HEAD_DIM - half, axis=1) + pltpu.roll(dy * s_hi, half, axis=1)


def _rows(start, size, stride):
    return pl.ds(start, size) if stride == 1 else pl.ds(start, size, stride=stride)


def _band_blocks(patterns):
    out = []
    for p, (max_lag, dil) in enumerate(patterns):
        nb = SEQ // dil // BLK
        for r in range(dil):
            for n in range(nb):
                lo = max(n - 1, 0)
                kn = (n - lo + 1) * BLK
                out.append((p, _rows(r + n * BLK * dil, BLK, dil), _rows(r + lo * BLK * dil, kn, dil), kn,
                            (n - lo) * BLK, max_lag))
    return out


def _band_valid(kn, off, max_lag):
    lag = off + lax.broadcasted_iota(jnp.int32, (BLK, kn), 0) - lax.broadcasted_iota(jnp.int32, (BLK, kn), 1)
    return (lag >= 0) & (lag <= max_lag)


def _attn_fwd(name, proj, tables, sink_b, *, n_heads, rep, q0, k0, v0, patterns):
    n_pat = len(patterns)
    blocks = _band_blocks(patterns)
    has_sink = sink_b is not None

    def body(*refs):
        if has_sink:
            q_ref, k_ref, v_ref, c_ref, sl_ref, sh_ref, sink_ref, o_ref, lse_ref, qr, kr, op = refs
            sk = sink_ref[0:1, 0:1]
        else:
            q_ref, k_ref, v_ref, c_ref, sl_ref, sh_ref, o_ref, lse_ref, qr, kr, op = refs
        c, s_lo, s_hi = c_ref[...], sl_ref[...], sh_ref[...]
        qr[...] = _rope(q_ref[...], c, s_lo, s_hi)
        kr[...] = _rope(k_ref[...], c, s_lo, s_hi)
        for p, qrows, krows, kn, off, max_lag in blocks:
            qb = qr[qrows, :].astype(BF16)
            kb = kr[krows, :].astype(BF16)
            vb = v_ref[krows, :].astype(BF16)
            s = lax.dot_general(qb, kb, NT, preferred_element_type=F32) * SCALE
            s = jnp.where(_band_valid(kn, off, max_lag), s, -jnp.inf)
            m = jnp.max(s, -1, keepdims=True)
            if has_sink:
                m = jnp.maximum(m, sk)
            e = jnp.exp(s - m)
            den = jnp.sum(e, -1, keepdims=True)
            if has_sink:
                den = den + jnp.exp(sk - m)
            o = jnp.dot(e.astype(BF16), vb, preferred_element_type=F32) / den
            op.at[p][qrows, :] = o
            lse_ref.at[p][qrows, :] = jnp.broadcast_to(m + jnp.log(den), (BLK, BLK))
        if n_pat == 1:
            o_ref[...] = op[0]
        else:
            ls = [lse_ref[p] for p in range(n_pat)]
            m = functools.reduce(jnp.maximum, ls)
            es = [jnp.exp(l - m) for l in ls]
            tot = functools.reduce(jnp.add, es)
            acc = None
            for p in range(n_pat):
                t = (es[p] / tot) * op[p]
                acc = t if acc is None else acc + t
            o_ref[...] = acc

    def colblk(fn):
        return pl.BlockSpec((SEQ, BLK), fn)

    tab = pl.BlockSpec((SEQ, BLK), lambda h: (0, 0))
    in_specs = [colblk(lambda h: (0, q0 + h)), colblk(lambda h: (0, k0 + h // rep)), colblk(lambda h: (0, v0 + h // rep)),
                tab, tab, tab]
    args = [proj, proj, proj, *tables]
    if has_sink:
        in_specs.append(pl.BlockSpec((None, 8, BLK), lambda h: (h, 0, 0)))
        args.append(sink_b)
    return pl.pallas_call(
        body, name=name, grid=(n_heads,), in_specs=in_specs,
        out_specs=[colblk(lambda h: (0, h)), pl.BlockSpec((None, n_pat, SEQ, BLK), lambda h: (h, 0, 0, 0))],
        out_shape=[jax.ShapeDtypeStruct((SEQ, n_heads * BLK), F32),
                   jax.ShapeDtypeStruct((n_heads, n_pat, SEQ, BLK), F32)],
        scratch_shapes=[pltpu.VMEM((SEQ, BLK), F32), pltpu.VMEM((SEQ, BLK), F32), pltpu.VMEM((n_pat, SEQ, BLK), F32)],
        compiler_params=_cp(("parallel",)))(*args)


def _attn_bwd(name, proj, mixed, dmix, lse, tables, sink_b, *, n_kv, rep, q0, k0, v0, m0, patterns):
    n_pat = len(patterns)
    n_heads = n_kv * rep
    blocks = _band_blocks(patterns)
    has_sink = sink_b is not None

    def body(*refs):
        if has_sink:
            (q_ref, k_ref, v_ref, o_ref, do_ref, lse_ref, c_ref, sl_ref, sh_ref, sink_ref,
             dq_ref, dk_ref, dv_ref, dsk_ref, qr, kr, dqa, dka, dva, dd, ww) = refs
        else:
            (q_ref, k_ref, v_ref, o_ref, do_ref, lse_ref, c_ref, sl_ref, sh_ref,
             dq_ref, dk_ref, dv_ref, dsk_ref, qr, kr, dqa, dka, dva, dd, ww) = refs
        j = pl.program_id(1)
        c, s_lo, s_hi = c_ref[...], sl_ref[...], sh_ref[...]
        qr[...] = _rope(q_ref[...], c, s_lo, s_hi)
        kr[...] = _rope(k_ref[...], c, s_lo, s_hi)
        dcol = jnp.sum(do_ref[...] * o_ref[...], -1, keepdims=True)
        dd[...] = jnp.broadcast_to(dcol, (SEQ, BLK))
        if n_pat == 1:
            ww[0] = jnp.ones((SEQ, BLK), F32)
        else:
            ls = [lse_ref[p] for p in range(n_pat)]
            m = functools.reduce(jnp.maximum, ls)
            es = [jnp.exp(l - m) for l in ls]
            tot = functools.reduce(jnp.add, es)
            for p in range(n_pat):
                ww[p] = es[p] / tot
        dqa[...] = jnp.zeros((SEQ, BLK), F32)

        @pl.when(j == 0)
        def _():
            dka[...] = jnp.zeros((SEQ, BLK), F32)
            dva[...] = jnp.zeros((SEQ, BLK), F32)

        for p, qrows, krows, kn, off, max_lag in blocks:
            qb = qr[qrows, :].astype(BF16)
            kb = kr[krows, :].astype(BF16)
            vb = v_ref[krows, :].astype(BF16)
            dob = do_ref[qrows, :].astype(BF16)
            lcol = lse_ref.at[p][qrows, :][:, 0:1]
            wcol = ww.at[p][qrows, :][:, 0:1]
            dcb = dd[qrows, :][:, 0:1]
            s = lax.dot_general(qb, kb, NT, preferred_element_type=F32) * SCALE
            a = jnp.where(_band_valid(kn, off, max_lag), jnp.exp(s - lcol), 0.0) * wcol
            dp = lax.dot_general(dob, vb, NT, preferred_element_type=F32)
            ds = (a * (dp - dcb) * SCALE).astype(BF16)
            dqa[qrows, :] += jnp.dot(ds, kb, preferred_element_type=F32)
            dka[krows, :] += lax.dot_general(ds, qb, TN, preferred_element_type=F32)
            dva[krows, :] += lax.dot_general(a.astype(BF16), dob, TN, preferred_element_type=F32)

        if has_sink:
            sk = sink_ref[0:1, 0:1]
            ps = jnp.exp(sk - lse_ref[0][:, 0:1])
            dsk_ref[...] = jnp.full((8, BLK), -jnp.sum(ps * dcol), F32)
        else:
            dsk_ref[...] = jnp.zeros((8, BLK), F32)
        dq_ref[...] = _unrope(dqa[...], c, s_lo, s_hi)

        @pl.when(j == rep - 1)
        def _():
            dk_ref[...] = _unrope(dka[...], c, s_lo, s_hi)
            dv_ref[...] = dva[...]

    def colblk(fn):
        return pl.BlockSpec((SEQ, BLK), fn)

    tab = pl.BlockSpec((SEQ, BLK), lambda g, j: (0, 0))
    in_specs = [colblk(lambda g, j: (0, q0 + g * rep + j)), colblk(lambda g, j: (0, k0 + g)), colblk(lambda g, j: (0, v0 + g)),
                colblk(lambda g, j: (0, m0 + g * rep + j)), colblk(lambda g, j: (0, m0 + g * rep + j)),
                pl.BlockSpec((None, n_pat, SEQ, BLK), lambda g, j: (g * rep + j, 0, 0, 0)), tab, tab, tab]
    args = [proj, proj, proj, mixed, dmix, lse, *tables]
    if has_sink:
        in_specs.append(pl.BlockSpec((None, 8, BLK), lambda g, j: (g * rep + j, 0, 0)))
        args.append(sink_b)
    acc = pltpu.VMEM((SEQ, BLK), F32)
    return pl.pallas_call(
        body, name=name, grid=(n_kv, rep), in_specs=in_specs,
        out_specs=[colblk(lambda g, j: (0, g * rep + j)), colblk(lambda g, j: (0, g)), colblk(lambda g, j: (0, g)),
                   pl.BlockSpec((None, 8, BLK), lambda g, j: (g * rep + j, 0, 0))],
        out_shape=[jax.ShapeDtypeStruct((SEQ, n_heads * BLK), F32), jax.ShapeDtypeStruct((SEQ, n_kv * BLK), F32),
                   jax.ShapeDtypeStruct((SEQ, n_kv * BLK), F32), jax.ShapeDtypeStruct((n_heads, 8, BLK), F32)],
        scratch_shapes=[acc, acc, acc, acc, acc, acc, pltpu.VMEM((n_pat, SEQ, BLK), F32)],
        compiler_params=_cp(("parallel", "arbitrary")))(*args)


B_PATTERNS = tuple((w // d, d) for w, d in DILATED_PATTERNS)
C_PATTERNS = ((C_WINDOW - 1, 1),)


ANY = pl.BlockSpec(memory_space=pl.ANY)
CHIP_MASKS = ((1, 0), (0, 1), (1, 1))


def _coords():
    return lax.axis_index("x"), lax.axis_index("y"), lax.axis_index("c")


def _flip(v, m):
    return 1 - v if m else v


def _allgather_weights(shards):
    n = len(shards)
    nt = n * len(CHIP_MASKS)

    def body(*refs):
        ins, outs = refs[:n], refs[n:2 * n]
        ssem, rsem, lsem = refs[2 * n:]
        x, y, c = _coords()
        k_me = 2 * x + y
        sibling = (x, y, 1 - c)

        def remote(src, dst, t, to):
            return pltpu.make_async_remote_copy(src_ref=src, dst_ref=dst, send_sem=ssem.at[t], recv_sem=rsem.at[t],
                                                device_id=to, device_id_type=MESH)

        local, sends = [], []
        for a in range(n):
            cp = pltpu.make_async_copy(ins[a], outs[a].at[k_me], lsem.at[a])
            cp.start()
            local.append(cp)
            for m, (mx, my) in enumerate(CHIP_MASKS):
                cp = remote(ins[a].at[c], outs[a].at[k_me, c], a * 3 + m, (_flip(x, mx), _flip(y, my), c))
                cp.start()
                sends.append(cp)
        for a in range(n):
            for m, (mx, my) in enumerate(CHIP_MASKS):
                kp = 2 * _flip(x, mx) + _flip(y, my)
                blk = outs[a].at[kp, c]
                remote(blk, blk, a * 3 + m, sibling).wait_recv()
                cp = remote(blk, blk, nt + a * 3 + m, sibling)
                cp.start()
                sends.append(cp)
        for a in range(n):
            for m, (mx, my) in enumerate(CHIP_MASKS):
                kp = 2 * _flip(x, mx) + _flip(y, my)
                blk = outs[a].at[kp, 1 - c]
                remote(blk, blk, nt + a * 3 + m, sibling).wait_recv()
        for cp in sends:
            cp.wait_send()
        for cp in local:
            cp.wait()

    return pl.pallas_call(
        body, name="allgather_weights", in_specs=[ANY] * n, out_specs=[ANY] * n,
        out_shape=[jax.ShapeDtypeStruct((N_CHIPS,) + s.shape, s.dtype) for s in shards],
        scratch_shapes=[pltpu.SemaphoreType.DMA((2 * nt,)), pltpu.SemaphoreType.DMA((2 * nt,)),
                        pltpu.SemaphoreType.DMA((n,))],
        compiler_params=pltpu.CompilerParams(has_side_effects=True),
    )(*shards)


def _pair_swap(name, grads):
    n = len(grads)

    def body(*refs):
        ins, outs = refs[:n], refs[n:2 * n]
        ssem, rsem = refs[2 * n:]
        x, y, c = _coords()
        cps = []
        for a in range(n):
            for j in range(N_CHIPS):
                cp = pltpu.make_async_remote_copy(src_ref=ins[a].at[j, 1 - c], dst_ref=outs[a].at[j],
                                                  send_sem=ssem.at[a * N_CHIPS + j], recv_sem=rsem.at[a * N_CHIPS + j],
                                                  device_id=(x, y, 1 - c), device_id_type=MESH)
                cp.start()
                cps.append(cp)
        for cp in cps:
            cp.wait()

    return pl.pallas_call(
        body, name=name, in_specs=[ANY] * n, out_specs=[ANY] * n,
        out_shape=[jax.ShapeDtypeStruct((N_CHIPS,) + g.shape[2:], g.dtype) for g in grads],
        scratch_shapes=[pltpu.SemaphoreType.DMA((n * N_CHIPS,)), pltpu.SemaphoreType.DMA((n * N_CHIPS,))],
        compiler_params=pltpu.CompilerParams(has_side_effects=True),
    )(*grads)


def _chip_scatter(name, parts):
    n = len(parts)

    def body(*refs):
        ins, outs = refs[:n], refs[n:2 * n]
        ssem, rsem, lsem = refs[2 * n:]
        x, y, c = _coords()
        k_me = 2 * x + y
        cps = []
        for a in range(n):
            cp = pltpu.make_async_copy(ins[a].at[k_me], outs[a].at[k_me], lsem.at[a])
            cp.start()
            cps.append(cp)
            for m, (mx, my) in enumerate(CHIP_MASKS):
                px, py = _flip(x, mx), _flip(y, my)
                cp = pltpu.make_async_remote_copy(src_ref=ins[a].at[2 * px + py], dst_ref=outs[a].at[k_me],
                                                  send_sem=ssem.at[a * 3 + m], recv_sem=rsem.at[a * 3 + m],
                                                  device_id=(px, py, c), device_id_type=MESH)
                cp.start()
                cps.append(cp)
        for cp in cps:
            cp.wait()

    return pl.pallas_call(
        body, name=name, in_specs=[ANY] * n, out_specs=[ANY] * n,
        out_shape=[jax.ShapeDtypeStruct(p.shape, p.dtype) for p in parts],
        scratch_shapes=[pltpu.SemaphoreType.DMA((n * 3,)), pltpu.SemaphoreType.DMA((n * 3,)),
                        pltpu.SemaphoreType.DMA((n,))],
        compiler_params=pltpu.CompilerParams(has_side_effects=True),
    )(*parts)


def _pair_gather(name, halves):
    n = len(halves)

    def body(*refs):
        ins, outs = refs[:n], refs[n:2 * n]
        ssem, rsem, lsem = refs[2 * n:]
        x, y, c = _coords()
        cps = []
        for a in range(n):
            cp = pltpu.make_async_copy(ins[a], outs[a].at[c], lsem.at[a])
            cp.start()
            cps.append(cp)
            cp = pltpu.make_async_remote_copy(src_ref=ins[a], dst_ref=outs[a].at[c], send_sem=ssem.at[a],
                                              recv_sem=rsem.at[a], device_id=(x, y, 1 - c), device_id_type=MESH)
            cp.start()
            cps.append(cp)
        for cp in cps:
            cp.wait()

    return pl.pallas_call(
        body, name=name, in_specs=[ANY] * n, out_specs=[ANY] * n,
        out_shape=[jax.ShapeDtypeStruct((2,) + h.shape, h.dtype) for h in halves],
        scratch_shapes=[pltpu.SemaphoreType.DMA((n,)), pltpu.SemaphoreType.DMA((n,)), pltpu.SemaphoreType.DMA((n,))],
        compiler_params=pltpu.CompilerParams(has_side_effects=True),
    )(*halves)


DEV_MASKS = tuple((mx, my, mc) for mx in (0, 1) for my in (0, 1) for mc in (0, 1) if (mx, my, mc) != (0, 0, 0))


def _gather_small(buf):
    def body(in_ref, out_ref, ssem, rsem, lsem):
        x, y, c = _coords()
        me = 4 * x + 2 * y + c
        cps = [pltpu.make_async_copy(in_ref, out_ref.at[me], lsem)]
        cps[0].start()
        for t, (mx, my, mc) in enumerate(DEV_MASKS):
            cp = pltpu.make_async_remote_copy(src_ref=in_ref, dst_ref=out_ref.at[me], send_sem=ssem.at[t],
                                              recv_sem=rsem.at[t], device_id=(_flip(x, mx), _flip(y, my), _flip(c, mc)),
                                              device_id_type=MESH)
            cp.start()
            cps.append(cp)
        for cp in cps:
            cp.wait()

    return pl.pallas_call(
        body, name="gather_small", in_specs=[ANY], out_specs=ANY,
        out_shape=jax.ShapeDtypeStruct((N_DEV,) + buf.shape, buf.dtype),
        scratch_shapes=[pltpu.SemaphoreType.DMA((N_DEV - 1,)), pltpu.SemaphoreType.DMA((N_DEV - 1,)),
                        pltpu.SemaphoreType.DMA(())],
        compiler_params=pltpu.CompilerParams(has_side_effects=True),
    )(buf)


def _row_tile(rows):
    return rows // 2 if rows % 16 == 0 else rows


def _pair_add(name, grad, got, c_idx):
    _, _, r2, cols = grad.shape
    tr = _row_tile(r2)

    def body(c_ref, a_ref, b_ref, o_ref):
        o_ref[...] = (a_ref[...] + b_ref[...]).astype(BF16)

    return pl.pallas_call(
        body, name=name,
        grid_spec=pltpu.PrefetchScalarGridSpec(
            num_scalar_prefetch=1, grid=(N_CHIPS, r2 // tr),
            in_specs=[pl.BlockSpec((None, None, tr, cols), lambda j, i, c: (j, c[0], i, 0)),
                      pl.BlockSpec((None, tr, cols), lambda j, i, c: (j, i, 0))],
            out_specs=pl.BlockSpec((None, tr, cols), lambda j, i, c: (j, i, 0))),
        out_shape=jax.ShapeDtypeStruct((N_CHIPS, r2, cols), BF16),
        compiler_params=_cp(("parallel", "parallel")))(c_idx, grad, got)


def _chip_add(name, got):
    _, r2, cols = got.shape
    tr = _row_tile(r2)

    def body(g_ref, o_ref):
        acc = g_ref[0].astype(F32)
        for j in range(1, N_CHIPS):
            acc = acc + g_ref[j].astype(F32)
        o_ref[...] = acc

    return pl.pallas_call(
        body, name=name, grid=(r2 // tr,),
        in_specs=[pl.BlockSpec((N_CHIPS, tr, cols), lambda i: (0, i, 0))],
        out_specs=pl.BlockSpec((tr, cols), lambda i: (i, 0)),
        out_shape=jax.ShapeDtypeStruct((r2, cols), F32),
        compiler_params=_cp(("parallel",)))(got)


def _adam_math(w, g, m, v):
    m2 = ADAM_B1 * m + (1.0 - ADAM_B1) * g
    v2 = ADAM_B2 * v + (1.0 - ADAM_B2) * (g * g)
    m_hat = m2 / (1.0 - ADAM_B1 ** ADAM_STEP)
    v_hat = v2 / (1.0 - ADAM_B2 ** ADAM_STEP)
    delta = -ADAM_LR * (m_hat / (jnp.sqrt(v_hat) + ADAM_EPS) + ADAM_WD * w)
    return delta, m2, v2


def _adamw_matrix(name, w, g_layers, m, v):
    _, rows, cols = w.shape
    tr = rows // 16 if rows % 128 == 0 else rows // 8

    def body(w_ref, g0_ref, g1_ref, m_ref, v_ref, go_ref, d_ref, mo_ref, vo_ref):
        g = jnp.where(pl.program_id(0) == 0, g0_ref[...], g1_ref[...])
        go_ref[...] = g
        d_ref[...], mo_ref[...], vo_ref[...] = _adam_math(w_ref[...], g, m_ref[...], v_ref[...])

    lay = pl.BlockSpec((None, tr, cols), lambda l, i: (l, i, 0))
    flat = pl.BlockSpec((tr, cols), lambda l, i: (i, 0))
    shp = jax.ShapeDtypeStruct(w.shape, F32)
    return pl.pallas_call(body, name=name, grid=(DEPTH, rows // tr), in_specs=[lay, flat, flat, lay, lay],
                          out_specs=[lay, lay, lay, lay], out_shape=[shp, shp, shp, shp],
                          compiler_params=_cp(("parallel", "parallel")))(w, g_layers[0], g_layers[1], m, v)


def _sum_small(gathered):
    def body(g_ref, o_ref):
        acc = g_ref[0]
        for d in range(1, N_DEV):
            acc = acc + g_ref[d]
        o_ref[...] = acc

    return pl.pallas_call(body, name="sum_small", out_shape=jax.ShapeDtypeStruct(gathered.shape[1:], F32),
                          compiler_params=_cp())(gathered)


def _adamw_small(w, g, m, v):
    def body(w_ref, g_ref, m_ref, v_ref, d_ref, mo_ref, vo_ref):
        d_ref[...], mo_ref[...], vo_ref[...] = _adam_math(w_ref[...], g_ref[...], m_ref[...], v_ref[...])

    shp = jax.ShapeDtypeStruct(w.shape, F32)
    return pl.pallas_call(body, name="adamw_small", out_shape=[shp, shp, shp], compiler_params=_cp())(w, g, m, v)


def _pack(arrays, rows):
    flat = jnp.concatenate([a.reshape(-1) for a in arrays])
    return jnp.pad(flat, (0, rows * BLK - flat.shape[0])).reshape(rows, BLK)


def _unpack(buf, shapes):
    flat = buf.reshape(-1)
    out, pos = [], 0
    for s in shapes:
        n = math.prod(s)
        out.append(flat[pos:pos + n].reshape(s))
        pos += n
    return out


def _rows_for(shapes):
    n = sum(math.prod(s) for s in shapes)
    return -(-n // (8 * BLK)) * 8


def _reduce_scatter(tag, grads, c_idx):
    split = [g.reshape(N_CHIPS, 2, g.shape[1] // 2, g.shape[2]) for g in grads]
    got = _pair_swap(f"rs_pair_swap{tag}", split)
    parts = [_pair_add(f"rs_pair_add{tag}_{i}", s, r, c_idx) for i, (s, r) in enumerate(zip(split, got))]
    got = _chip_scatter(f"rs_chip_scatter{tag}", parts)
    halves = [_chip_add(f"rs_chip_add{tag}_{i}", r) for i, r in enumerate(got)]
    full = _pair_gather(f"rs_pair_gather{tag}", halves)
    return [f.reshape(g.shape[1], g.shape[2]) for f, g in zip(full, grads)]


def kernel(x, w_in, lb_logits, a_norm_w, c_sinks, w_out, ln1_g, ln1_b, w_gate, w_up, conv_w, conv_b, w_down, ln2_g, ln2_b, loss_target, m_w_in, m_lb_logits, m_a_norm_w, m_c_sinks, m_w_out, m_ln1_g, m_ln1_b, m_w_gate, m_w_up, m_conv_w, m_conv_b, m_w_down, m_ln2_g, m_ln2_b, v_w_in, v_lb_logits, v_a_norm_w, v_c_sinks, v_w_out, v_ln1_g, v_ln1_b, v_w_gate, v_w_up, v_conv_w, v_conv_b, v_w_down, v_ln2_g, v_ln2_b):
    cx, cy, cc = _coords()
    c_idx = jnp.reshape(cc, (1,)).astype(jnp.int32)
    k_me = 2 * cx + cy

    wi, wo, wg, wu, wd, cw_all = _allgather_weights(
        [w_in.astype(BF16), w_out.astype(BF16), w_gate.astype(BF16), w_up.astype(BF16), w_down.astype(BF16), conv_w])
    cw_full = jnp.transpose(cw_all, (1, 2, 0, 3)).reshape(DEPTH, CONV_WIDTH, D_FF)
    tables = _rope_tables()

    h = x[0]
    saved = []
    for l in range(DEPTH):
        proj = _fwd_colsharded(f"proj{l}", h, wi, l)
        o_a, raw = _hgrn_fwd(f"hgrn_fwd{l}", proj, lb_logits, a_norm_w[l], l)
        o_b, lse_b = _attn_fwd(f"dilated_fwd{l}", proj, tables, None, n_heads=B_HEADS, rep=1, q0=QB0, k0=KB0, v0=VB0,
                               patterns=B_PATTERNS)
        sink_b = jnp.broadcast_to(c_sinks[l][:, None, None], (C_HEADS, 8, BLK))
        o_c, lse_c = _attn_fwd(f"window_fwd{l}", proj, tables, sink_b, n_heads=C_HEADS, rep=C_HEADS // C_KV_HEADS,
                               q0=QC0, k0=KC0, v0=VC0, patterns=C_PATTERNS)
        mixed = jnp.concatenate([o_a, o_b, o_c], axis=1)
        y1 = _fwd_rowsharded(f"wout{l}", mixed, wo, l, OUT_SHARD)
        x1 = _ln_fwd(f"ln1_fwd{l}", h, y1, ln1_g[l], ln1_b[l])
        g = _fwd_colsharded(f"gate{l}", x1, wg, l)
        u = _fwd_colsharded(f"up{l}", x1, wu, l)
        hh = _conv_gate_fwd(f"conv_fwd{l}", g, u, cw_full[l], conv_b[l])
        y2 = _fwd_rowsharded(f"down{l}", hh, wd, l, FF_SHARD)
        x2 = _ln_fwd(f"ln2_fwd{l}", x1, y2, ln2_g[l], ln2_b[l])
        saved.append((h, proj, raw, lse_b, sink_b, lse_c, mixed, y1, x1, g, u, hh, y2))
        h = x2

    dy, loss_part = _loss_head(h, loss_target[0])

    d_res, d_path = None, dy
    small = [None] * DEPTH
    mat_grads = [None] * DEPTH
    for l in reversed(range(DEPTH)):
        h_in, proj, raw, lse_b, sink_b, lse_c, mixed, y1, x1, g, u, hh, y2 = saved[l]
        dz2, d_ln2g, d_ln2b = _ln_bwd(f"ln2_bwd{l}", x1, y2, ln2_g[l], d_res, d_path)
        dhh = _bwd_act_rowsharded(f"down_dx{l}", dz2, wd, l, FF_SHARD)
        d_wd = _bwd_w_rowsharded(f"down_dw{l}", hh, dz2, FF_SHARD)
        dg, du, d_cw, d_cb = _conv_gate_bwd(f"conv_bwd{l}", g, u, cw_full[l], conv_b[l], dhh)
        dx1 = _bwd_act_colsharded(f"gateup_dx{l}", [(dg, wg), (du, wu)], l)
        d_wg = _bwd_w_colsharded(f"gate_dw{l}", x1, dg)
        d_wu = _bwd_w_colsharded(f"up_dw{l}", x1, du)
        dz1, d_ln1g, d_ln1b = _ln_bwd(f"ln1_bwd{l}", h_in, y1, ln1_g[l], dz2, dx1)
        dmix = _bwd_act_rowsharded(f"wout_dx{l}", dz1, wo, l, OUT_SHARD)
        d_wo = _bwd_w_rowsharded(f"wout_dw{l}", mixed, dz1, OUT_SHARD)
        dq_a, df_a, di_a, dg_a, d_nw, d_lb = _hgrn_bwd(f"hgrn_bwd{l}", proj, raw, dmix, lb_logits, a_norm_w[l], l)
        dq_b, dk_b, dv_b, _ = _attn_bwd(f"dilated_bwd{l}", proj, mixed, dmix, lse_b, tables, None, n_kv=B_HEADS, rep=1,
                                        q0=QB0, k0=KB0, v0=VB0, m0=A_HEADS, patterns=B_PATTERNS)
        dq_c, dk_c, dv_c, d_sink = _attn_bwd(f"window_bwd{l}", proj, mixed, dmix, lse_c, tables, sink_b, n_kv=C_KV_HEADS,
                                             rep=C_HEADS // C_KV_HEADS, q0=QC0, k0=KC0, v0=VC0, m0=A_HEADS + B_HEADS,
                                             patterns=C_PATTERNS)
        dproj = jnp.concatenate([dq_a, df_a, di_a, dg_a, dq_b, dk_b, dv_b, dq_c, dk_c, dv_c], axis=1)
        dxp = _bwd_act_colsharded(f"proj_dx{l}", [(dproj, wi)], l)
        d_wi = _bwd_w_colsharded(f"proj_dw{l}", h_in, dproj)
        d_res, d_path = dz1, dxp
        mat_grads[l] = _reduce_scatter(l, [d_wi, d_wo, d_wg, d_wu, d_wd], c_idx)
        small[l] = (d_lb, d_nw.reshape(A_HEADS, 8, BLK)[:, 0].sum(0), d_sink[:, 0, 0], d_ln1g[0], d_ln1b[0],
                    d_cw, d_cb[0], d_ln2g[0], d_ln2b[0])
    grad_x = _axpy("grad_x", d_res, d_path)[None]

    g_lb = small[0][0] + small[1][0]
    per_layer = [jnp.stack([small[0][i], small[1][i]]) for i in range(1, 9)]
    small_shapes = [(DEPTH, 4 * BLK), (DEPTH, BLK), (DEPTH, C_HEADS), (DEPTH, D_MODEL), (DEPTH, D_MODEL),
                    (DEPTH, CONV_WIDTH, D_FF), (DEPTH, D_FF), (DEPTH, D_MODEL), (DEPTH, D_MODEL), (BLK,)]
    rows = _rows_for(small_shapes)
    total = _sum_small(_gather_small(_pack([g_lb] + per_layer + [loss_part[0]], rows)))
    g_lb, g_nw, g_sink, g_ln1g, g_ln1b, g_cw_full, g_cb, g_ln2g, g_ln2b, loss_row = _unpack(total, small_shapes)
    loss = loss_row[0]
    g_cw = lax.dynamic_slice_in_dim(g_cw_full, k_me * FF_SHARD, FF_SHARD, axis=2)

    sw = [lb_logits, a_norm_w, c_sinks, ln1_g, ln1_b, conv_w, conv_b, ln2_g, ln2_b]
    sg = [g_lb, g_nw, g_sink, g_ln1g, g_ln1b, g_cw, g_cb, g_ln2g, g_ln2b]
    sm = [m_lb_logits, m_a_norm_w, m_c_sinks, m_ln1_g, m_ln1_b, m_conv_w, m_conv_b, m_ln2_g, m_ln2_b]
    sv = [v_lb_logits, v_a_norm_w, v_c_sinks, v_ln1_g, v_ln1_b, v_conv_w, v_conv_b, v_ln2_g, v_ln2_b]
    shapes = [a.shape for a in sw]
    prow = _rows_for(shapes)
    sd, snm, snv = (_unpack(b, shapes) for b in _adamw_small(_pack(sw, prow), _pack(sg, prow), _pack(sm, prow), _pack(sv, prow)))

    names = ["w_in", "w_out", "w_gate", "w_up", "w_down"]
    mw = [w_in, w_out, w_gate, w_up, w_down]
    mm = [m_w_in, m_w_out, m_w_gate, m_w_up, m_w_down]
    mv = [v_w_in, v_w_out, v_w_gate, v_w_up, v_w_down]
    mg, md, mnm, mnv = [], [], [], []
    for i in range(5):
        go, d, nm, nv = _adamw_matrix(f"adamw_{names[i]}", mw[i], [mat_grads[0][i], mat_grads[1][i]], mm[i], mv[i])
        mg.append(go), md.append(d), mnm.append(nm), mnv.append(nv)

    def ordered(mat, sm_):
        return [mat[0], sm_[0], sm_[1], sm_[2], mat[1], sm_[3], sm_[4], mat[2], mat[3], sm_[5], sm_[6], mat[4], sm_[7], sm_[8]]

    return (loss, grad_x, *ordered(mg, sg), *ordered(md, sd), *ordered(mnm, snm), *ordered(mnv, snv))
```

```python
import functools
import math

import jax
import jax.numpy as jnp
from jax import lax
from jax.experimental import pallas as pl
from jax.experimental.pallas import tpu as pltpu

F32 = jnp.float32
BF16 = jnp.bfloat16

D_MODEL = 2048
SEQ = 2048
DEPTH = 2
HEAD_DIM = 128
A_HEADS = 4
B_HEADS = 6
C_HEADS = 6
C_KV_HEADS = 2
A_CHUNK = 16
DILATED_PATTERNS = ((128, 1), (512, 4), (2048, 16))
C_WINDOW = 128
ROPE_THETA = 500000.0
ROPE_DIM = HEAD_DIM // 4
D_FF = 5632
CONV_WIDTH = 3
LN_EPS = 1e-5
ALPHA = (2 * DEPTH) ** 0.25
IN_WIDTH = 5632
MIX_WIDTH = 2048
ADAM_LR = 0.001
ADAM_B1 = 0.9
ADAM_B2 = 0.999
ADAM_EPS = 1e-08
ADAM_WD = 0.01
ADAM_STEP = 10

N_CHIPS = 4
N_DEV = 8
FF_SHARD = D_FF // N_CHIPS
OUT_SHARD = MIX_WIDTH // N_CHIPS
BLK = 128
N_CHUNK = SEQ // A_CHUNK
SLAB = 32

QA0, FA0, IA0, GA0 = 0, 4, 8, 12
QB0, KB0, VB0 = 16, 22, 28
QC0, KC0, VC0 = 34, 40, 42

VMEM_LIMIT_V7X = 56 * 1024 * 1024
HI = lax.Precision.HIGHEST
MESH = pl.DeviceIdType.MESH


def _cp(sem=None, vmem=VMEM_LIMIT_V7X, **kw):
    return pltpu.CompilerParams(dimension_semantics=sem, vmem_limit_bytes=vmem, **kw)


def _sigmoid(x):
    return 1.0 / (1.0 + jnp.exp(-x))


def _mm(name, pairs, dims, grid, a_specs, b_specs, out_spec, out_shape, nk=1, acc_shape=None):
    n_pairs = len(pairs)

    def body(*refs):
        o_ref = refs[2 * n_pairs]
        part = None
        for p in range(n_pairs):
            a = refs[2 * p][...].astype(BF16)
            b = refs[2 * p + 1][...].astype(BF16)
            t = lax.dot_general(a, b, dims, preferred_element_type=F32)
            part = t if part is None else part + t
        if nk == 1:
            o_ref[...] = part.astype(o_ref.dtype)
        else:
            acc = refs[2 * n_pairs + 1]
            k = pl.program_id(len(grid) - 1)

            @pl.when(k == 0)
            def _():
                acc[...] = part

            @pl.when(k > 0)
            def _():
                acc[...] += part

            @pl.when(k == nk - 1)
            def _():
                o_ref[...] = acc[...].astype(o_ref.dtype)

    in_specs, args = [], []
    for (a, b), sa, sb in zip(pairs, a_specs, b_specs):
        in_specs += [sa, sb]
        args += [a, b]
    sem = ("parallel",) * (len(grid) - (1 if nk > 1 else 0)) + (("arbitrary",) if nk > 1 else ())
    return pl.pallas_call(
        body, name=name, grid=grid, in_specs=in_specs, out_specs=out_spec, out_shape=out_shape,
        scratch_shapes=[pltpu.VMEM(acc_shape, F32)] if nk > 1 else [],
        compiler_params=_cp(sem),
    )(*args)


NN = (((1,), (0,)), ((), ()))
NT = (((1,), (1,)), ((), ()))
TN = (((0,), (0,)), ((), ()))
TM = 1024


def _fwd_colsharded(name, x, w_stk, layer):
    return _mm(name, [(x, w_stk)], NN, (N_CHIPS, SEQ // TM),
               [pl.BlockSpec((TM, D_MODEL), lambda j, i: (i, 0))],
               [pl.BlockSpec((None, None, D_MODEL, FF_SHARD), lambda j, i: (j, layer, 0, 0))],
               pl.BlockSpec((TM, FF_SHARD), lambda j, i: (i, j)),
               jax.ShapeDtypeStruct((SEQ, D_FF), F32))


def _fwd_rowsharded(name, a, w_stk, layer, shard):
    tn = 1024
    return _mm(name, [(a, w_stk)], NN, (SEQ // TM, D_MODEL // tn, N_CHIPS),
               [pl.BlockSpec((TM, shard), lambda i, j, k: (i, k))],
               [pl.BlockSpec((None, None, shard, tn), lambda i, j, k: (k, layer, 0, j))],
               pl.BlockSpec((TM, tn), lambda i, j, k: (i, j)),
               jax.ShapeDtypeStruct((SEQ, D_MODEL), F32), nk=N_CHIPS, acc_shape=(TM, tn))


def _bwd_act_colsharded(name, pairs, layer):
    tn = 1024
    n = len(pairs)
    return _mm(name, pairs, NT, (SEQ // TM, D_MODEL // tn, N_CHIPS),
               [pl.BlockSpec((TM, FF_SHARD), lambda i, j, k: (i, k))] * n,
               [pl.BlockSpec((None, None, tn, FF_SHARD), lambda i, j, k: (k, layer, j, 0))] * n,
               pl.BlockSpec((TM, tn), lambda i, j, k: (i, j)),
               jax.ShapeDtypeStruct((SEQ, D_MODEL), F32), nk=N_CHIPS, acc_shape=(TM, tn))


def _bwd_act_rowsharded(name, dy, w_stk, layer, shard):
    return _mm(name, [(dy, w_stk)], NT, (N_CHIPS, SEQ // TM),
               [pl.BlockSpec((TM, D_MODEL), lambda j, i: (i, 0))],
               [pl.BlockSpec((None, None, shard, D_MODEL), lambda j, i: (j, layer, 0, 0))],
               pl.BlockSpec((TM, shard), lambda j, i: (i, j)),
               jax.ShapeDtypeStruct((SEQ, N_CHIPS * shard), F32))


def _bwd_w_colsharded(name, x, dy):
    tm = 512
    return _mm(name, [(x, dy)], TN, (N_CHIPS, D_MODEL // tm),
               [pl.BlockSpec((SEQ, tm), lambda j, i: (0, i))],
               [pl.BlockSpec((SEQ, FF_SHARD), lambda j, i: (0, j))],
               pl.BlockSpec((None, tm, FF_SHARD), lambda j, i: (j, i, 0)),
               jax.ShapeDtypeStruct((N_CHIPS, D_MODEL, FF_SHARD), F32))


def _bwd_w_rowsharded(name, a, dy, shard):
    tn = 1024
    return _mm(name, [(a, dy)], TN, (N_CHIPS, D_MODEL // tn),
               [pl.BlockSpec((SEQ, shard), lambda j, i: (0, j))],
               [pl.BlockSpec((SEQ, tn), lambda j, i: (0, i))],
               pl.BlockSpec((None, shard, tn), lambda j, i: (j, 0, i)),
               jax.ShapeDtypeStruct((N_CHIPS, shard, D_MODEL), F32))


TR = 256


def _ln_fwd(name, x, y, g, b):
    def body(x_ref, y_ref, g_ref, b_ref, o_ref):
        z = ALPHA * x_ref[...] + y_ref[...]
        mu = jnp.mean(z, -1, keepdims=True)
        zc = z - mu
        var = jnp.mean(zc * zc, -1, keepdims=True)
        o_ref[...] = zc * lax.rsqrt(var + LN_EPS) * g_ref[...] + b_ref[...]

    row = pl.BlockSpec((TR, D_MODEL), lambda i: (i, 0))
    vec = pl.BlockSpec((1, D_MODEL), lambda i: (0, 0))
    return pl.pallas_call(body, name=name, grid=(SEQ // TR,), in_specs=[row, row, vec, vec], out_specs=row,
                          out_shape=jax.ShapeDtypeStruct((SEQ, D_MODEL), F32),
                          compiler_params=_cp(("parallel",)))(x, y, g.reshape(1, -1), b.reshape(1, -1))


def _ln_bwd(name, x, y, g, d_res, d_path):
    has_res = d_res is not None

    def body(*refs):
        if has_res:
            x_ref, y_ref, g_ref, r_ref, p_ref, dz_ref, dg_ref, db_ref = refs
            dout = ALPHA * r_ref[...] + p_ref[...]
        else:
            x_ref, y_ref, g_ref, p_ref, dz_ref, dg_ref, db_ref = refs
            dout = p_ref[...]
        z = ALPHA * x_ref[...] + y_ref[...]
        mu = jnp.mean(z, -1, keepdims=True)
        zc = z - mu
        rstd = lax.rsqrt(jnp.mean(zc * zc, -1, keepdims=True) + LN_EPS)
        zh = zc * rstd
        dzh = dout * g_ref[...]
        dz_ref[...] = rstd * (dzh - jnp.mean(dzh, -1, keepdims=True) - zh * jnp.mean(dzh * zh, -1, keepdims=True))
        pg = jnp.sum(dout * zh, 0, keepdims=True)
        pb = jnp.sum(dout, 0, keepdims=True)

        @pl.when(pl.program_id(0) == 0)
        def _():
            dg_ref[...] = pg
            db_ref[...] = pb

        @pl.when(pl.program_id(0) > 0)
        def _():
            dg_ref[...] += pg
            db_ref[...] += pb

    row = pl.BlockSpec((TR, D_MODEL), lambda i: (i, 0))
    vec = pl.BlockSpec((1, D_MODEL), lambda i: (0, 0))
    args = [x, y, g.reshape(1, -1)] + ([d_res] if has_res else []) + [d_path]
    in_specs = [row, row, vec] + ([row] if has_res else []) + [row]
    vshape = jax.ShapeDtypeStruct((1, D_MODEL), F32)
    return pl.pallas_call(body, name=name, grid=(SEQ // TR,), in_specs=in_specs, out_specs=[row, vec, vec],
                          out_shape=[jax.ShapeDtypeStruct((SEQ, D_MODEL), F32), vshape, vshape],
                          compiler_params=_cp(("arbitrary",)))(*args)


def _loss_head(y, target):
    def body(y_ref, t_ref, dy_ref, l_ref):
        e = y_ref[...] - t_ref[...]
        dy_ref[...] = e * (1.0 / D_MODEL)
        part = jnp.full((8, BLK), 0.5 / D_MODEL * jnp.sum(e * e), F32)

        @pl.when(pl.program_id(0) == 0)
        def _():
            l_ref[...] = part

        @pl.when(pl.program_id(0) > 0)
        def _():
            l_ref[...] += part

    row = pl.BlockSpec((TR, D_MODEL), lambda i: (i, 0))
    return pl.pallas_call(body, name="loss_head", grid=(SEQ // TR,), in_specs=[row, row],
                          out_specs=[row, pl.BlockSpec((8, BLK), lambda i: (0, 0))],
                          out_shape=[jax.ShapeDtypeStruct((SEQ, D_MODEL), F32), jax.ShapeDtypeStruct((8, BLK), F32)],
                          compiler_params=_cp(("arbitrary",)))(y, target)


def _axpy(name, a, b):
    def body(a_ref, b_ref, o_ref):
        o_ref[...] = ALPHA * a_ref[...] + b_ref[...]

    row = pl.BlockSpec((TR, D_MODEL), lambda i: (i, 0))
    return pl.pallas_call(body, name=name, grid=(SEQ // TR,), in_specs=[row, row], out_specs=row,
                          out_shape=jax.ShapeDtypeStruct((SEQ, D_MODEL), F32),
                          compiler_params=_cp(("parallel",)))(a, b)


TC = 512


def _shift_down(x, s, rows):
    if s == 0:
        return x
    return jnp.where(rows >= s, pltpu.roll(x, s, axis=0), 0.0)


def _shift_up(x, s, rows):
    if s == 0:
        return x
    return jnp.where(rows < SEQ - s, pltpu.roll(x, SEQ - s, axis=0), 0.0)


def _conv_gate_fwd(name, g, u, cw, cb):
    def body(g_ref, u_ref, w_ref, b_ref, h_ref):
        gg = g_ref[...]
        rows = lax.broadcasted_iota(jnp.int32, gg.shape, 0)
        gc = b_ref[...] + w_ref[2:3, :] * gg
        gc = gc + w_ref[1:2, :] * _shift_down(gg, 1, rows)
        gc = gc + w_ref[0:1, :] * _shift_down(gg, 2, rows)
        h_ref[...] = (gc * _sigmoid(gc) * u_ref[...]).astype(BF16)

    col = pl.BlockSpec((SEQ, TC), lambda j: (0, j))
    return pl.pallas_call(body, name=name, grid=(D_FF // TC,),
                          in_specs=[col, col, pl.BlockSpec((CONV_WIDTH, TC), lambda j: (0, j)),
                                    pl.BlockSpec((1, TC), lambda j: (0, j))],
                          out_specs=col, out_shape=jax.ShapeDtypeStruct((SEQ, D_FF), BF16),
                          compiler_params=_cp(("parallel",)))(g, u, cw, cb.reshape(1, -1))


def _conv_gate_bwd(name, g, u, cw, cb, dh):
    def body(g_ref, u_ref, w_ref, b_ref, dh_ref, dg_ref, du_ref, dw_ref, db_ref):
        gg = g_ref[...]
        rows = lax.broadcasted_iota(jnp.int32, gg.shape, 0)
        g1 = _shift_down(gg, 1, rows)
        g2 = _shift_down(gg, 2, rows)
        gc = b_ref[...] + w_ref[2:3, :] * gg + w_ref[1:2, :] * g1 + w_ref[0:1, :] * g2
        sg = _sigmoid(gc)
        act = gc * sg
        dh = dh_ref[...]
        du_ref[...] = dh * act
        dgc = dh * u_ref[...] * (sg * (1.0 + gc * (1.0 - sg)))
        db_ref[...] = jnp.sum(dgc, 0, keepdims=True)
        dw_ref[2:3, :] = jnp.sum(dgc * gg, 0, keepdims=True)
        dw_ref[1:2, :] = jnp.sum(dgc * g1, 0, keepdims=True)
        dw_ref[0:1, :] = jnp.sum(dgc * g2, 0, keepdims=True)
        dg_ref[...] = (w_ref[2:3, :] * dgc + w_ref[1:2, :] * _shift_up(dgc, 1, rows)
                       + w_ref[0:1, :] * _shift_up(dgc, 2, rows))

    col = pl.BlockSpec((SEQ, TC), lambda j: (0, j))
    w3 = pl.BlockSpec((CONV_WIDTH, TC), lambda j: (0, j))
    w1 = pl.BlockSpec((1, TC), lambda j: (0, j))
    big = jax.ShapeDtypeStruct((SEQ, D_FF), F32)
    return pl.pallas_call(body, name=name, grid=(D_FF // TC,), in_specs=[col, col, w3, w1, col],
                          out_specs=[col, col, w3, w1],
                          out_shape=[big, big, jax.ShapeDtypeStruct((CONV_WIDTH, D_FF), F32),
                                     jax.ShapeDtypeStruct((1, D_FF), F32)],
                          compiler_params=_cp(("parallel",)))(g, u, cw, cb.reshape(1, -1), dh)


def _lbs_of(logits, layer):
    m = jnp.max(logits, 0, keepdims=True)
    e = jnp.exp(logits - m)
    p = e / jnp.sum(e, 0, keepdims=True)
    lb = jnp.zeros((1, BLK), F32)
    for r in range(1, layer + 1):
        lb = lb + p[r:r + 1, :]
    return lb, p


def _dlogits_of(p, dlb, layer):
    rows = lax.broadcasted_iota(jnp.int32, p.shape, 0)
    dp = jnp.where((rows >= 1) & (rows <= layer), dlb, 0.0)
    return p * (dp - jnp.sum(p * dp, 0, keepdims=True))


SROWS = SLAB * A_CHUNK
N_SLAB = N_CHUNK // SLAB


def _chunk_prefix(x, rowi):
    for s in (1, 2, 4, 8):
        x = x + jnp.where(rowi >= s, pltpu.roll(x, s, axis=0), 0.0)
    return x


def _chunk_suffix(x, rowi):
    for s in (1, 2, 4, 8):
        x = x + jnp.where(rowi < A_CHUNK - s, pltpu.roll(x, SROWS - s, axis=0), 0.0)
    return x


def _c3(x):
    return x.reshape(SLAB, A_CHUNK, BLK)


def _c2(x):
    return x.reshape(SROWS, BLK)


def _lane_sum_b(x2, ones):
    return jnp.dot(x2, ones, preferred_element_type=F32, precision=HI)


def _bmm(eq, a, b):
    return jnp.einsum(eq, a, b, preferred_element_type=F32, precision=HI)


def _slab_rows(s):
    return pl.ds(s * SROWS, SROWS)


def _hgrn_prep(q, f, lb):
    rowi = lax.broadcasted_iota(jnp.int32, (SROWS, BLK), 0) & (A_CHUNK - 1)
    sq = _sigmoid(q)
    qc = q * sq
    sf = _sigmoid(f)
    fg = lb + (1.0 - lb) * sf
    kc = 1.0 - fg
    b = _chunk_prefix(jnp.log(fg), rowi)
    b3 = _c3(b)
    blast = b3[:, A_CHUNK - 1:A_CHUNK, :]
    eb = jnp.exp(b)
    ekb = _c2(jnp.exp(blast - b3))
    dec = jnp.exp(blast.reshape(SLAB, BLK))
    return rowi, sq, qc, sf, fg, kc, b, eb, ekb, dec


def _hgrn_slab_states(s, carry, v, ke, dec, dec_ref, u_ref, st_ref):
    dec_ref[pl.ds(s * SLAB, SLAB), :] = dec
    u_ref[...] = _bmm('ncv,nck->nvk', _c3(v), _c3(ke))

    def step(j, c):
        st_ref[j] = c
        return dec_ref[pl.ds(s * SLAB + j, 1), :] * c + u_ref[j]

    return lax.fori_loop(0, SLAB, step, carry)


def _hgrn_fwd(name, proj, lb_logits, nw, layer):
    def body(q_ref, f_ref, i_ref, g_ref, lg_ref, nw_ref, out_ref, raw_ref, dec_ref, u_ref, st_ref):
        lb, _ = _lbs_of(lg_ref[...], layer)
        ones = jnp.ones((BLK, BLK), F32)
        carry = jnp.zeros((BLK, BLK), F32)
        for s in range(N_SLAB):
            rows = _slab_rows(s)
            v = i_ref[rows, :]
            rowi, sq, qc, sf, fg, kc, b, eb, ekb, dec = _hgrn_prep(q_ref[rows, :], f_ref[rows, :], lb)
            carry = _hgrn_slab_states(s, carry, v, kc * ekb, dec, dec_ref, u_ref, st_ref)
            o = _c2(_bmm('nck,nvk->ncv', _c3(qc * eb), st_ref[...]))
            qc3, kc3, b3, v3, row3 = _c3(qc), _c3(kc), _c3(b), _c3(v), _c3(rowi)
            for j in range(A_CHUNK):
                dj = jnp.exp(jnp.where(row3 >= j, b3 - b3[:, j:j + 1, :], -jnp.inf))
                a = _lane_sum_b(_c2(qc3 * dj * kc3[:, j:j + 1, :]), ones)
                o = o + a * _c2(jnp.broadcast_to(v3[:, j:j + 1, :], v3.shape))
            raw_ref[rows, :] = o
            r = lax.rsqrt(jnp.mean(o * o, -1, keepdims=True) + LN_EPS)
            gg = g_ref[rows, :]
            out_ref[rows, :] = o * r * nw_ref[...] * (gg * _sigmoid(gg))

    def colblk(c0):
        return pl.BlockSpec((SEQ, BLK), lambda h: (0, c0 + h))

    big = jax.ShapeDtypeStruct((SEQ, A_HEADS * BLK), F32)
    return pl.pallas_call(
        body, name=name, grid=(A_HEADS,),
        in_specs=[colblk(QA0), colblk(FA0), colblk(IA0), colblk(GA0),
                  pl.BlockSpec((DEPTH, BLK), lambda h: (0, h)), pl.BlockSpec((1, BLK), lambda h: (0, 0))],
        out_specs=[colblk(0), colblk(0)], out_shape=[big, big],
        scratch_shapes=[pltpu.VMEM((N_CHUNK, BLK), F32), pltpu.VMEM((SLAB, BLK, BLK), F32),
                        pltpu.VMEM((SLAB, BLK, BLK), F32)],
        compiler_params=_cp(("parallel",)))(proj, proj, proj, proj, lb_logits, nw.reshape(1, -1))


def _hgrn_bwd(name, proj, raw, dmix, lb_logits, nw, layer):
    def body(q_ref, f_ref, i_ref, g_ref, raw_ref, do_ref, lg_ref, nw_ref,
             dq_ref, df_ref, di_ref, dg_ref, dnw_ref, dlg_ref,
             dec_ref, u_ref, st_ref, h_ref, dbs_ref, dkc_ref, tot_ref):
        lb, p = _lbs_of(lg_ref[...], layer)
        ones = jnp.ones((BLK, BLK), F32)
        nwv = nw_ref[...]

        carry = jnp.zeros((BLK, BLK), F32)
        for s in range(N_SLAB):
            rows = _slab_rows(s)
            rowi, sq, qc, sf, fg, kc, b, eb, ekb, dec = _hgrn_prep(q_ref[rows, :], f_ref[rows, :], lb)
            carry = _hgrn_slab_states(s, carry, i_ref[rows, :], kc * ekb, dec, dec_ref, u_ref,
                                      st_ref.at[pl.ds(s * SLAB, SLAB)])

        carry = jnp.zeros((BLK, BLK), F32)
        dnw = jnp.zeros((1, BLK), F32)
        for s in reversed(range(N_SLAB)):
            rows = _slab_rows(s)
            q, v = q_ref[rows, :], i_ref[rows, :]
            rowi, sq, qc, sf, fg, kc, b, eb, ekb, dec = _hgrn_prep(q, f_ref[rows, :], lb)
            ke = kc * ekb
            qe = qc * eb

            o = raw_ref[rows, :]
            gg = g_ref[rows, :]
            sgg = _sigmoid(gg)
            dout = do_ref[rows, :]
            r = lax.rsqrt(jnp.mean(o * o, -1, keepdims=True) + LN_EPS)
            oh = o * r
            dg_ref[rows, :] = dout * oh * nwv * (sgg * (1.0 + gg * (1.0 - sgg)))
            dn = dout * (gg * sgg)
            dnw = dnw + jnp.sum(dn * oh, 0, keepdims=True)
            doh = dn * nwv
            do = r * (doh - oh * jnp.mean(doh * oh, -1, keepdims=True))
            do3, qe3, v3, ke3 = _c3(do), _c3(qe), _c3(v), _c3(ke)

            u_ref[...] = _bmm('ncv,nck->nvk', do3, qe3)

            def step(jj, c, s=s):
                j = SLAB - 1 - jj
                h_ref[j] = c
                return u_ref[j] + dec_ref[pl.ds(s * SLAB + j, 1), :] * c

            carry = lax.fori_loop(0, SLAB, step, carry)

            hh = h_ref[...]
            dqc = _c2(_bmm('ncv,nvk->nck', do3, st_ref[pl.ds(s * SLAB, SLAB)])) * eb
            dkc = _c2(_bmm('ncv,nvk->nck', v3, hh)) * ekb
            dv = _c2(_bmm('nck,nvk->ncv', ke3, hh))

            qc3, kc3, b3, row3 = _c3(qc), _c3(kc), _c3(b), _c3(rowi)
            for j in range(A_CHUNK):
                dj = jnp.exp(jnp.where(row3 >= j, b3 - b3[:, j:j + 1, :], -jnp.inf))
                kj = kc3[:, j:j + 1, :]
                vj = jnp.broadcast_to(v3[:, j:j + 1, :], v3.shape)
                att = _c3(_lane_sum_b(_c2(qc3 * dj * kj), ones))
                datt = _c3(_lane_sum_b(_c2(do3 * vj), ones))
                md = dj * datt
                dqc = dqc + _c2(md * kj)
                sel = row3 == j
                dkc = dkc + _c2(jnp.where(sel, jnp.sum(md * qc3, 1, keepdims=True), 0.0))
                dv = dv + _c2(jnp.where(sel, jnp.sum(att * do3, 1, keepdims=True), 0.0))
            di_ref[rows, :] = dv
            dq_ref[rows, :] = dqc * (sq * (1.0 + q * (1.0 - sq)))

            dbs = _chunk_suffix(qc * dqc - kc * dkc, rowi)
            dbs_ref[rows, :] = dbs
            dkc_ref[rows, :] = dkc
            tot_ref[pl.ds(s * SLAB, SLAB), :] = _c3(dbs)[:, 0:1, :].reshape(SLAB, BLK)
        dnw_ref[...] = jnp.broadcast_to(dnw, (8, BLK))

        rn = lax.broadcasted_iota(jnp.int32, (N_CHUNK, N_CHUNK), 0)
        cn = lax.broadcasted_iota(jnp.int32, (N_CHUNK, N_CHUNK), 1)
        tot_ref[...] = jnp.dot((cn > rn).astype(F32), tot_ref[...], preferred_element_type=F32, precision=HI)
        dlb = jnp.zeros((1, BLK), F32)
        for s in range(N_SLAB):
            rows = _slab_rows(s)
            sf = _sigmoid(f_ref[rows, :])
            fg = lb + (1.0 - lb) * sf
            later = tot_ref[pl.ds(s * SLAB, SLAB), :]
            dlg = _c2(_c3(dbs_ref[rows, :]) + later[:, None, :])
            dfg = dlg / fg - dkc_ref[rows, :]
            df_ref[rows, :] = dfg * (1.0 - lb) * sf * (1.0 - sf)
            dlb = dlb + jnp.sum(dfg * (1.0 - sf), 0, keepdims=True)
        dlg_ref[...] = _dlogits_of(p, dlb, layer)

    def colblk(c0):
        return pl.BlockSpec((SEQ, BLK), lambda h: (0, c0 + h))

    big = jax.ShapeDtypeStruct((SEQ, A_HEADS * BLK), F32)
    return pl.pallas_call(
        body, name=name, grid=(A_HEADS,),
        in_specs=[colblk(QA0), colblk(FA0), colblk(IA0), colblk(GA0), colblk(0), colblk(0),
                  pl.BlockSpec((DEPTH, BLK), lambda h: (0, h)), pl.BlockSpec((1, BLK), lambda h: (0, 0))],
        out_specs=[colblk(0), colblk(0), colblk(0), colblk(0),
                   pl.BlockSpec((8, BLK), lambda h: (h, 0)), pl.BlockSpec((DEPTH, BLK), lambda h: (0, h))],
        out_shape=[big, big, big, big, jax.ShapeDtypeStruct((A_HEADS * 8, BLK), F32),
                   jax.ShapeDtypeStruct((DEPTH, A_HEADS * BLK), F32)],
        scratch_shapes=[pltpu.VMEM((N_CHUNK, BLK), F32), pltpu.VMEM((SLAB, BLK, BLK), F32),
                        pltpu.VMEM((N_CHUNK, BLK, BLK), F32), pltpu.VMEM((SLAB, BLK, BLK), F32),
                        pltpu.VMEM((SEQ, BLK), F32), pltpu.VMEM((SEQ, BLK), F32), pltpu.VMEM((N_CHUNK, BLK), F32)],
        compiler_params=_cp(("parallel",)))(proj, proj, proj, proj, raw, dmix, lb_logits, nw.reshape(1, -1))


SCALE = HEAD_DIM ** -0.5


def _rope_tables():
    half = ROPE_DIM // 2
    inv = ROPE_THETA ** (-jnp.arange(0, ROPE_DIM, 2, dtype=F32) / ROPE_DIM)
    ang = jnp.arange(SEQ, dtype=F32)[:, None] * inv[None, :]
    cos, sin = jnp.cos(ang), jnp.sin(ang)
    pad = jnp.zeros((SEQ, HEAD_DIM - ROPE_DIM), F32)
    zero = jnp.zeros((SEQ, half), F32)
    c = jnp.concatenate([cos, cos, pad + 1.0], 1)
    s_lo = jnp.concatenate([zero, sin, pad], 1)
    s_hi = jnp.concatenate([-sin, zero, pad], 1)
    return c, s_lo, s_hi


def _rope(x, c, s_lo, s_hi):
    half = ROPE_DIM // 2
    return x * c + pltpu.roll(x, half, axis=1) * s_lo + pltpu.roll(x, HEAD_DIM - half, axis=1) * s_hi


def _unrope(dy, c, s_lo, s_hi):
    half = ROPE_DIM // 2
    return dy * c + pltpu.roll(dy * s_lo, HEAD_DIM - half, axis=1) + pltpu.roll(dy * s_hi, half, axis=1)


def _rows(start, size, stride):
    return pl.ds(start, size) if stride == 1 else pl.ds(start, size, stride=stride)


def _band_blocks(patterns):
    out = []
    for p, (max_lag, dil) in enumerate(patterns):
        nb = SEQ // dil // BLK
        for r in range(dil):
            for n in range(nb):
                lo = max(n - 1, 0)
                kn = (n - lo + 1) * BLK
                out.append((p, _rows(r + n * BLK * dil, BLK, dil), _rows(r + lo * BLK * dil, kn, dil), kn,
                            (n - lo) * BLK, max_lag))
    return out


def _band_valid(kn, off, max_lag):
    lag = off + lax.broadcasted_iota(jnp.int32, (BLK, kn), 0) - lax.broadcasted_iota(jnp.int32, (BLK, kn), 1)
    return (lag >= 0) & (lag <= max_lag)


def _attn_fwd(name, proj, tables, sink_b, *, n_heads, rep, q0, k0, v0, patterns):
    n_pat = len(patterns)
    blocks = _band_blocks(patterns)
    has_sink = sink_b is not None

    def body(*refs):
        if has_sink:
            q_ref, k_ref, v_ref, c_ref, sl_ref, sh_ref, sink_ref, o_ref, lse_ref, qr, kr, op = refs
            sk = sink_ref[0:1, 0:1]
        else:
            q_ref, k_ref, v_ref, c_ref, sl_ref, sh_ref, o_ref, lse_ref, qr, kr, op = refs
        c, s_lo, s_hi = c_ref[...], sl_ref[...], sh_ref[...]
        qr[...] = _rope(q_ref[...], c, s_lo, s_hi)
        kr[...] = _rope(k_ref[...], c, s_lo, s_hi)
        for p, qrows, krows, kn, off, max_lag in blocks:
            qb = qr[qrows, :].astype(BF16)
            kb = kr[krows, :].astype(BF16)
            vb = v_ref[krows, :].astype(BF16)
            s = lax.dot_general(qb, kb, NT, preferred_element_type=F32) * SCALE
            s = jnp.where(_band_valid(kn, off, max_lag), s, -jnp.inf)
            m = jnp.max(s, -1, keepdims=True)
            if has_sink:
                m = jnp.maximum(m, sk)
            e = jnp.exp(s - m)
            den = jnp.sum(e, -1, keepdims=True)
            if has_sink:
                den = den + jnp.exp(sk - m)
            o = jnp.dot(e.astype(BF16), vb, preferred_element_type=F32) / den
            op.at[p][qrows, :] = o
            lse_ref.at[p][qrows, :] = jnp.broadcast_to(m + jnp.log(den), (BLK, BLK))
        if n_pat == 1:
            o_ref[...] = op[0]
        else:
            ls = [lse_ref[p] for p in range(n_pat)]
            m = functools.reduce(jnp.maximum, ls)
            es = [jnp.exp(l - m) for l in ls]
            tot = functools.reduce(jnp.add, es)
            acc = None
            for p in range(n_pat):
                t = (es[p] / tot) * op[p]
                acc = t if acc is None else acc + t
            o_ref[...] = acc

    def colblk(fn):
        return pl.BlockSpec((SEQ, BLK), fn)

    tab = pl.BlockSpec((SEQ, BLK), lambda h: (0, 0))
    in_specs = [colblk(lambda h: (0, q0 + h)), colblk(lambda h: (0, k0 + h // rep)), colblk(lambda h: (0, v0 + h // rep)),
                tab, tab, tab]
    args = [proj, proj, proj, *tables]
    if has_sink:
        in_specs.append(pl.BlockSpec((None, 8, BLK), lambda h: (h, 0, 0)))
        args.append(sink_b)
    return pl.pallas_call(
        body, name=name, grid=(n_heads,), in_specs=in_specs,
        out_specs=[colblk(lambda h: (0, h)), pl.BlockSpec((None, n_pat, SEQ, BLK), lambda h: (h, 0, 0, 0))],
        out_shape=[jax.ShapeDtypeStruct((SEQ, n_heads * BLK), F32),
                   jax.ShapeDtypeStruct((n_heads, n_pat, SEQ, BLK), F32)],
        scratch_shapes=[pltpu.VMEM((SEQ, BLK), F32), pltpu.VMEM((SEQ, BLK), F32), pltpu.VMEM((n_pat, SEQ, BLK), F32)],
        compiler_params=_cp(("parallel",)))(*args)


def _attn_bwd(name, proj, mixed, dmix, lse, tables, sink_b, *, n_kv, rep, q0, k0, v0, m0, patterns):
    n_pat = len(patterns)
    n_heads = n_kv * rep
    blocks = _band_blocks(patterns)
    has_sink = sink_b is not None

    def body(*refs):
        if has_sink:
            (q_ref, k_ref, v_ref, o_ref, do_ref, lse_ref, c_ref, sl_ref, sh_ref, sink_ref,
             dq_ref, dk_ref, dv_ref, dsk_ref, qr, kr, dqa, dka, dva, dd, ww) = refs
        else:
            (q_ref, k_ref, v_ref, o_ref, do_ref, lse_ref, c_ref, sl_ref, sh_ref,
             dq_ref, dk_ref, dv_ref, dsk_ref, qr, kr, dqa, dka, dva, dd, ww) = refs
        j = pl.program_id(1)
        c, s_lo, s_hi = c_ref[...], sl_ref[...], sh_ref[...]
        qr[...] = _rope(q_ref[...], c, s_lo, s_hi)
        kr[...] = _rope(k_ref[...], c, s_lo, s_hi)
        dcol = jnp.sum(do_ref[...] * o_ref[...], -1, keepdims=True)
        dd[...] = jnp.broadcast_to(dcol, (SEQ, BLK))
        if n_pat == 1:
            ww[0] = jnp.ones((SEQ, BLK), F32)
        else:
            ls = [lse_ref[p] for p in range(n_pat)]
            m = functools.reduce(jnp.maximum, ls)
            es = [jnp.exp(l - m) for l in ls]
            tot = functools.reduce(jnp.add, es)
            for p in range(n_pat):
                ww[p] = es[p] / tot
        dqa[...] = jnp.zeros((SEQ, BLK), F32)

        @pl.when(j == 0)
        def _():
            dka[...] = jnp.zeros((SEQ, BLK), F32)
            dva[...] = jnp.zeros((SEQ, BLK), F32)

        for p, qrows, krows, kn, off, max_lag in blocks:
            qb = qr[qrows, :].astype(BF16)
            kb = kr[krows, :].astype(BF16)
            vb = v_ref[krows, :].astype(BF16)
            dob = do_ref[qrows, :].astype(BF16)
            lcol = lse_ref.at[p][qrows, :][:, 0:1]
            wcol = ww.at[p][qrows, :][:, 0:1]
            dcb = dd[qrows, :][:, 0:1]
            s = lax.dot_general(qb, kb, NT, preferred_element_type=F32) * SCALE
            a = jnp.where(_band_valid(kn, off, max_lag), jnp.exp(s - lcol), 0.0) * wcol
            dp = lax.dot_general(dob, vb, NT, preferred_element_type=F32)
            ds = (a * (dp - dcb) * SCALE).astype(BF16)
            dqa[qrows, :] += jnp.dot(ds, kb, preferred_element_type=F32)
            dka[krows, :] += lax.dot_general(ds, qb, TN, preferred_element_type=F32)
            dva[krows, :] += lax.dot_general(a.astype(BF16), dob, TN, preferred_element_type=F32)

        if has_sink:
            sk = sink_ref[0:1, 0:1]
            ps = jnp.exp(sk - lse_ref[0][:, 0:1])
            dsk_ref[...] = jnp.full((8, BLK), -jnp.sum(ps * dcol), F32)
        else:
            dsk_ref[...] = jnp.zeros((8, BLK), F32)
        dq_ref[...] = _unrope(dqa[...], c, s_lo, s_hi)

        @pl.when(j == rep - 1)
        def _():
            dk_ref[...] = _unrope(dka[...], c, s_lo, s_hi)
            dv_ref[...] = dva[...]

    def colblk(fn):
        return pl.BlockSpec((SEQ, BLK), fn)

    tab = pl.BlockSpec((SEQ, BLK), lambda g, j: (0, 0))
    in_specs = [colblk(lambda g, j: (0, q0 + g * rep + j)), colblk(lambda g, j: (0, k0 + g)), colblk(lambda g, j: (0, v0 + g)),
                colblk(lambda g, j: (0, m0 + g * rep + j)), colblk(lambda g, j: (0, m0 + g * rep + j)),
                pl.BlockSpec((None, n_pat, SEQ, BLK), lambda g, j: (g * rep + j, 0, 0, 0)), tab, tab, tab]
    args = [proj, proj, proj, mixed, dmix, lse, *tables]
    if has_sink:
        in_specs.append(pl.BlockSpec((None, 8, BLK), lambda g, j: (g * rep + j, 0, 0)))
        args.append(sink_b)
    acc = pltpu.VMEM((SEQ, BLK), F32)
    return pl.pallas_call(
        body, name=name, grid=(n_kv, rep), in_specs=in_specs,
        out_specs=[colblk(lambda g, j: (0, g * rep + j)), colblk(lambda g, j: (0, g)), colblk(lambda g, j: (0, g)),
                   pl.BlockSpec((None, 8, BLK), lambda g, j: (g * rep + j, 0, 0))],
        out_shape=[jax.ShapeDtypeStruct((SEQ, n_heads * BLK), F32), jax.ShapeDtypeStruct((SEQ, n_kv * BLK), F32),
                   jax.ShapeDtypeStruct((SEQ, n_kv * BLK), F32), jax.ShapeDtypeStruct((n_heads, 8, BLK), F32)],
        scratch_shapes=[acc, acc, acc, acc, acc, acc, pltpu.VMEM((n_pat, SEQ, BLK), F32)],
        compiler_params=_cp(("parallel", "arbitrary")))(*args)


B_PATTERNS = tuple((w // d, d) for w, d in DILATED_PATTERNS)
C_PATTERNS = ((C_WINDOW - 1, 1),)


ANY = pl.BlockSpec(memory_space=pl.ANY)
CHIP_MASKS = ((1, 0), (0, 1), (1, 1))


def _coords():
    return lax.axis_index("x"), lax.axis_index("y"), lax.axis_index("c")


def _flip(v, m):
    return 1 - v if m else v


def _into_slot(name, w, k_idx, dtype):
    _, rows, cols = w.shape
    tr = rows // 8 if rows % 64 == 0 else rows

    def body(k_ref, w_ref, o_ref):
        o_ref[...] = w_ref[...].astype(dtype)

    return pl.pallas_call(
        body, name=name,
        grid_spec=pltpu.PrefetchScalarGridSpec(
            num_scalar_prefetch=1, grid=(DEPTH, rows // tr),
            in_specs=[pl.BlockSpec((None, tr, cols), lambda l, i, k: (l, i, 0))],
            out_specs=pl.BlockSpec((None, None, tr, cols), lambda l, i, k: (k[0], l, i, 0))),
        out_shape=jax.ShapeDtypeStruct((N_CHIPS,) + w.shape, dtype),
        compiler_params=_cp(("parallel", "parallel")))(k_idx, w)


def _allgather_weights(bufs):
    n = len(bufs)
    nt = n * len(CHIP_MASKS)

    def body(*refs):
        outs = refs[n:2 * n]
        ssem, rsem = refs[2 * n:]
        x, y, c = _coords()
        k_me = 2 * x + y
        sibling = (x, y, 1 - c)

        def remote(src, dst, t, to):
            return pltpu.make_async_remote_copy(src_ref=src, dst_ref=dst, send_sem=ssem.at[t], recv_sem=rsem.at[t],
                                                device_id=to, device_id_type=MESH)

        sends = []
        for a in range(n):
            mine = outs[a].at[k_me, c]
            for m, (mx, my) in enumerate(CHIP_MASKS):
                cp = remote(mine, mine, a * 3 + m, (_flip(x, mx), _flip(y, my), c))
                cp.start()
                sends.append(cp)
        for a in range(n):
            for m, (mx, my) in enumerate(CHIP_MASKS):
                kp = 2 * _flip(x, mx) + _flip(y, my)
                blk = outs[a].at[kp, c]
                remote(blk, blk, a * 3 + m, sibling).wait_recv()
                cp = remote(blk, blk, nt + a * 3 + m, sibling)
                cp.start()
                sends.append(cp)
        for a in range(n):
            for m, (mx, my) in enumerate(CHIP_MASKS):
                kp = 2 * _flip(x, mx) + _flip(y, my)
                blk = outs[a].at[kp, 1 - c]
                remote(blk, blk, nt + a * 3 + m, sibling).wait_recv()
        for cp in sends:
            cp.wait_send()

    return pl.pallas_call(
        body, name="allgather_weights", in_specs=[ANY] * n, out_specs=[ANY] * n,
        out_shape=[jax.ShapeDtypeStruct(b.shape, b.dtype) for b in bufs],
        input_output_aliases={a: a for a in range(n)},
        scratch_shapes=[pltpu.SemaphoreType.DMA((2 * nt,)), pltpu.SemaphoreType.DMA((2 * nt,))],
        compiler_params=pltpu.CompilerParams(has_side_effects=True),
    )(*bufs)


def _pair_swap(name, grads):
    n = len(grads)

    def body(*refs):
        ins, outs = refs[:n], refs[n:2 * n]
        ssem, rsem = refs[2 * n:]
        x, y, c = _coords()
        cps = []
        for a in range(n):
            for j in range(N_CHIPS):
                cp = pltpu.make_async_remote_copy(src_ref=ins[a].at[j, 1 - c], dst_ref=outs[a].at[j],
                                                  send_sem=ssem.at[a * N_CHIPS + j], recv_sem=rsem.at[a * N_CHIPS + j],
                                                  device_id=(x, y, 1 - c), device_id_type=MESH)
                cp.start()
                cps.append(cp)
        for cp in cps:
            cp.wait()

    return pl.pallas_call(
        body, name=name, in_specs=[ANY] * n, out_specs=[ANY] * n,
        out_shape=[jax.ShapeDtypeStruct((N_CHIPS,) + g.shape[2:], g.dtype) for g in grads],
        scratch_shapes=[pltpu.SemaphoreType.DMA((n * N_CHIPS,)), pltpu.SemaphoreType.DMA((n * N_CHIPS,))],
        compiler_params=pltpu.CompilerParams(has_side_effects=True),
    )(*grads)


def _chip_scatter(name, parts):
    n = len(parts)

    def body(*refs):
        ins, outs = refs[:n], refs[n:2 * n]
        ssem, rsem = refs[2 * n:]
        x, y, c = _coords()
        k_me = 2 * x + y
        cps = []
        for a in range(n):
            for m, (mx, my) in enumerate(CHIP_MASKS):
                px, py = _flip(x, mx), _flip(y, my)
                cp = pltpu.make_async_remote_copy(src_ref=ins[a].at[2 * px + py], dst_ref=outs[a].at[k_me],
                                                  send_sem=ssem.at[a * 3 + m], recv_sem=rsem.at[a * 3 + m],
                                                  device_id=(px, py, c), device_id_type=MESH)
                cp.start()
                cps.append(cp)
        for cp in cps:
            cp.wait()

    return pl.pallas_call(
        body, name=name, in_specs=[ANY] * n, out_specs=[ANY] * n,
        out_shape=[jax.ShapeDtypeStruct(p.shape, p.dtype) for p in parts],
        scratch_shapes=[pltpu.SemaphoreType.DMA((n * 3,)), pltpu.SemaphoreType.DMA((n * 3,))],
        compiler_params=pltpu.CompilerParams(has_side_effects=True),
    )(*parts)


def _pair_gather(name, bufs):
    n = len(bufs)

    def body(*refs):
        outs = refs[n:2 * n]
        ssem, rsem = refs[2 * n:]
        x, y, c = _coords()
        cps = []
        for a in range(n):
            mine = outs[a].at[c]
            cp = pltpu.make_async_remote_copy(src_ref=mine, dst_ref=mine, send_sem=ssem.at[a],
                                              recv_sem=rsem.at[a], device_id=(x, y, 1 - c), device_id_type=MESH)
            cp.start()
            cps.append(cp)
        for cp in cps:
            cp.wait()

    return pl.pallas_call(
        body, name=name, in_specs=[ANY] * n, out_specs=[ANY] * n,
        out_shape=[jax.ShapeDtypeStruct(b.shape, b.dtype) for b in bufs],
        input_output_aliases={a: a for a in range(n)},
        scratch_shapes=[pltpu.SemaphoreType.DMA((n,)), pltpu.SemaphoreType.DMA((n,))],
        compiler_params=pltpu.CompilerParams(has_side_effects=True),
    )(*bufs)


DEV_MASKS = tuple((mx, my, mc) for mx in (0, 1) for my in (0, 1) for mc in (0, 1) if (mx, my, mc) != (0, 0, 0))


def _gather_small(buf):
    def body(in_ref, out_ref, ssem, rsem, lsem):
        x, y, c = _coords()
        me = 4 * x + 2 * y + c
        cps = [pltpu.make_async_copy(in_ref, out_ref.at[me], lsem)]
        cps[0].start()
        for t, (mx, my, mc) in enumerate(DEV_MASKS):
            cp = pltpu.make_async_remote_copy(src_ref=in_ref, dst_ref=out_ref.at[me], send_sem=ssem.at[t],
                                              recv_sem=rsem.at[t], device_id=(_flip(x, mx), _flip(y, my), _flip(c, mc)),
                                              device_id_type=MESH)
            cp.start()
            cps.append(cp)
        for cp in cps:
            cp.wait()

    return pl.pallas_call(
        body, name="gather_small", in_specs=[ANY], out_specs=ANY,
        out_shape=jax.ShapeDtypeStruct((N_DEV,) + buf.shape, buf.dtype),
        scratch_shapes=[pltpu.SemaphoreType.DMA((N_DEV - 1,)), pltpu.SemaphoreType.DMA((N_DEV - 1,)),
                        pltpu.SemaphoreType.DMA(())],
        compiler_params=pltpu.CompilerParams(has_side_effects=True),
    )(buf)


def _row_tile(rows):
    return rows // 2 if rows % 16 == 0 else rows


def _pair_add(name, grad, got, c_idx):
    _, _, r2, cols = grad.shape
    tr = _row_tile(r2)

    def body(c_ref, a_ref, b_ref, o_ref):
        o_ref[...] = (a_ref[...] + b_ref[...]).astype(BF16)

    return pl.pallas_call(
        body, name=name,
        grid_spec=pltpu.PrefetchScalarGridSpec(
            num_scalar_prefetch=1, grid=(N_CHIPS, r2 // tr),
            in_specs=[pl.BlockSpec((None, None, tr, cols), lambda j, i, c: (j, c[0], i, 0)),
                      pl.BlockSpec((None, tr, cols), lambda j, i, c: (j, i, 0))],
            out_specs=pl.BlockSpec((None, tr, cols), lambda j, i, c: (j, i, 0))),
        out_shape=jax.ShapeDtypeStruct((N_CHIPS, r2, cols), BF16),
        compiler_params=_cp(("parallel", "parallel")))(c_idx, grad, got)


def _chip_add(name, part, got, kc_idx):
    _, r2, cols = got.shape
    tr = _row_tile(r2)

    def body(k_ref, p_ref, g1_ref, g2_ref, g3_ref, o_ref):
        acc = p_ref[...].astype(F32)
        for g_ref in (g1_ref, g2_ref, g3_ref):
            acc = acc + g_ref[...].astype(F32)
        o_ref[...] = acc

    def slot(d):
        return pl.BlockSpec((None, tr, cols), lambda i, k: ((k[0] + d) % N_CHIPS, i, 0))

    return pl.pallas_call(
        body, name=name,
        grid_spec=pltpu.PrefetchScalarGridSpec(
            num_scalar_prefetch=1, grid=(r2 // tr,),
            in_specs=[slot(0), slot(1), slot(2), slot(3)],
            out_specs=pl.BlockSpec((None, tr, cols), lambda i, k: (k[1], i, 0))),
        out_shape=jax.ShapeDtypeStruct((2, r2, cols), F32),
        compiler_params=_cp(("parallel",)))(kc_idx, part, got, got, got)


def _adam_math(w, g, m, v):
    m2 = ADAM_B1 * m + (1.0 - ADAM_B1) * g
    v2 = ADAM_B2 * v + (1.0 - ADAM_B2) * (g * g)
    m_hat = m2 / (1.0 - ADAM_B1 ** ADAM_STEP)
    v_hat = v2 / (1.0 - ADAM_B2 ** ADAM_STEP)
    delta = -ADAM_LR * (m_hat / (jnp.sqrt(v_hat) + ADAM_EPS) + ADAM_WD * w)
    return delta, m2, v2


def _adamw_matrix(name, w, g_layers, m, v):
    _, rows, cols = w.shape
    tr = rows // 16 if rows % 128 == 0 else rows // 8

    def body(w_ref, g0_ref, g1_ref, m_ref, v_ref, go_ref, d_ref, mo_ref, vo_ref):
        g = jnp.where(pl.program_id(0) == 0, g0_ref[...], g1_ref[...])
        go_ref[...] = g
        d_ref[...], mo_ref[...], vo_ref[...] = _adam_math(w_ref[...], g, m_ref[...], v_ref[...])

    lay = pl.BlockSpec((None, tr, cols), lambda l, i: (l, i, 0))
    flat = pl.BlockSpec((tr, cols), lambda l, i: (i, 0))
    shp = jax.ShapeDtypeStruct(w.shape, F32)
    return pl.pallas_call(body, name=name, grid=(DEPTH, rows // tr), in_specs=[lay, flat, flat, lay, lay],
                          out_specs=[lay, lay, lay, lay], out_shape=[shp, shp, shp, shp],
                          compiler_params=_cp(("parallel", "parallel")))(w, g_layers[0], g_layers[1], m, v)


def _sum_small(gathered):
    def body(g_ref, o_ref):
        acc = g_ref[0]
        for d in range(1, N_DEV):
            acc = acc + g_ref[d]
        o_ref[...] = acc

    return pl.pallas_call(body, name="sum_small", out_shape=jax.ShapeDtypeStruct(gathered.shape[1:], F32),
                          compiler_params=_cp())(gathered)


def _adamw_small(w, g, m, v):
    def body(w_ref, g_ref, m_ref, v_ref, d_ref, mo_ref, vo_ref):
        d_ref[...], mo_ref[...], vo_ref[...] = _adam_math(w_ref[...], g_ref[...], m_ref[...], v_ref[...])

    shp = jax.ShapeDtypeStruct(w.shape, F32)
    return pl.pallas_call(body, name="adamw_small", out_shape=[shp, shp, shp], compiler_params=_cp())(w, g, m, v)


def _pack(arrays, rows):
    flat = jnp.concatenate([a.reshape(-1) for a in arrays])
    return jnp.pad(flat, (0, rows * BLK - flat.shape[0])).reshape(rows, BLK)


def _unpack(buf, shapes):
    flat = buf.reshape(-1)
    out, pos = [], 0
    for s in shapes:
        n = math.prod(s)
        out.append(flat[pos:pos + n].reshape(s))
        pos += n
    return out


def _rows_for(shapes):
    n = sum(math.prod(s) for s in shapes)
    return -(-n // (8 * BLK)) * 8


def _reduce_scatter(tag, grads, c_idx, kc_idx):
    split = [g.reshape(N_CHIPS, 2, g.shape[1] // 2, g.shape[2]) for g in grads]
    got = _pair_swap(f"rs_pair_swap{tag}", split)
    parts = [_pair_add(f"rs_pair_add{tag}_{i}", s, r, c_idx) for i, (s, r) in enumerate(zip(split, got))]
    got = _chip_scatter(f"rs_chip_scatter{tag}", parts)
    halves = [_chip_add(f"rs_chip_add{tag}_{i}", p, r, kc_idx) for i, (p, r) in enumerate(zip(parts, got))]
    full = _pair_gather(f"rs_pair_gather{tag}", halves)
    return [f.reshape(g.shape[1], g.shape[2]) for f, g in zip(full, grads)]


def kernel(x, w_in, lb_logits, a_norm_w, c_sinks, w_out, ln1_g, ln1_b, w_gate, w_up, conv_w, conv_b, w_down, ln2_g, ln2_b, loss_target, m_w_in, m_lb_logits, m_a_norm_w, m_c_sinks, m_w_out, m_ln1_g, m_ln1_b, m_w_gate, m_w_up, m_conv_w, m_conv_b, m_w_down, m_ln2_g, m_ln2_b, v_w_in, v_lb_logits, v_a_norm_w, v_c_sinks, v_w_out, v_ln1_g, v_ln1_b, v_w_gate, v_w_up, v_conv_w, v_conv_b, v_w_down, v_ln2_g, v_ln2_b):
    cx, cy, cc = _coords()
    c_idx = jnp.reshape(cc, (1,)).astype(jnp.int32)
    k_me = 2 * cx + cy
    k_idx = jnp.reshape(k_me, (1,)).astype(jnp.int32)
    kc_idx = jnp.stack([k_me, cc]).astype(jnp.int32)

    wi, wo, wg, wu, wd, cw_all = _allgather_weights(
        [_into_slot(f"slot_{nm}", w, k_idx, dt)
         for nm, w, dt in (("w_in", w_in, BF16), ("w_out", w_out, BF16), ("w_gate", w_gate, BF16), ("w_up", w_up, BF16),
                           ("w_down", w_down, BF16), ("conv_w", conv_w, F32))])
    cw_full = jnp.transpose(cw_all, (1, 2, 0, 3)).reshape(DEPTH, CONV_WIDTH, D_FF)
    tables = _rope_tables()

    h = x[0]
    saved = []
    for l in range(DEPTH):
        proj = _fwd_colsharded(f"proj{l}", h, wi, l)
        o_a, raw = _hgrn_fwd(f"hgrn_fwd{l}", proj, lb_logits, a_norm_w[l], l)
        o_b, lse_b = _attn_fwd(f"dilated_fwd{l}", proj, tables, None, n_heads=B_HEADS, rep=1, q0=QB0, k0=KB0, v0=VB0,
                               patterns=B_PATTERNS)
        sink_b = jnp.broadcast_to(c_sinks[l][:, None, None], (C_HEADS, 8, BLK))
        o_c, lse_c = _attn_fwd(f"window_fwd{l}", proj, tables, sink_b, n_heads=C_HEADS, rep=C_HEADS // C_KV_HEADS,
                               q0=QC0, k0=KC0, v0=VC0, patterns=C_PATTERNS)
        mixed = jnp.concatenate([o_a, o_b, o_c], axis=1)
        y1 = _fwd_rowsharded(f"wout{l}", mixed, wo, l, OUT_SHARD)
        x1 = _ln_fwd(f"ln1_fwd{l}", h, y1, ln1_g[l], ln1_b[l])
        g = _fwd_colsharded(f"gate{l}", x1, wg, l)
        u = _fwd_colsharded(f"up{l}", x1, wu, l)
        hh = _conv_gate_fwd(f"conv_fwd{l}", g, u, cw_full[l], conv_b[l])
        y2 = _fwd_rowsharded(f"down{l}", hh, wd, l, FF_SHARD)
        x2 = _ln_fwd(f"ln2_fwd{l}", x1, y2, ln2_g[l], ln2_b[l])
        saved.append((h, proj, raw, lse_b, sink_b, lse_c, mixed, y1, x1, g, u, hh, y2))
        h = x2

    dy, loss_part = _loss_head(h, loss_target[0])

    d_res, d_path = None, dy
    small = [None] * DEPTH
    mat_grads = [None] * DEPTH
    for l in reversed(range(DEPTH)):
        h_in, proj, raw, lse_b, sink_b, lse_c, mixed, y1, x1, g, u, hh, y2 = saved[l]
        dz2, d_ln2g, d_ln2b = _ln_bwd(f"ln2_bwd{l}", x1, y2, ln2_g[l], d_res, d_path)
        dhh = _bwd_act_rowsharded(f"down_dx{l}", dz2, wd, l, FF_SHARD)
        d_wd = _bwd_w_rowsharded(f"down_dw{l}", hh, dz2, FF_SHARD)
        dg, du, d_cw, d_cb = _conv_gate_bwd(f"conv_bwd{l}", g, u, cw_full[l], conv_b[l], dhh)
        dx1 = _bwd_act_colsharded(f"gateup_dx{l}", [(dg, wg), (du, wu)], l)
        d_wg = _bwd_w_colsharded(f"gate_dw{l}", x1, dg)
        d_wu = _bwd_w_colsharded(f"up_dw{l}", x1, du)
        dz1, d_ln1g, d_ln1b = _ln_bwd(f"ln1_bwd{l}", h_in, y1, ln1_g[l], dz2, dx1)
        dmix = _bwd_act_rowsharded(f"wout_dx{l}", dz1, wo, l, OUT_SHARD)
        d_wo = _bwd_w_rowsharded(f"wout_dw{l}", mixed, dz1, OUT_SHARD)
        dq_a, df_a, di_a, dg_a, d_nw, d_lb = _hgrn_bwd(f"hgrn_bwd{l}", proj, raw, dmix, lb_logits, a_norm_w[l], l)
        dq_b, dk_b, dv_b, _ = _attn_bwd(f"dilated_bwd{l}", proj, mixed, dmix, lse_b, tables, None, n_kv=B_HEADS, rep=1,
                                        q0=QB0, k0=KB0, v0=VB0, m0=A_HEADS, patterns=B_PATTERNS)
        dq_c, dk_c, dv_c, d_sink = _attn_bwd(f"window_bwd{l}", proj, mixed, dmix, lse_c, tables, sink_b, n_kv=C_KV_HEADS,
                                             rep=C_HEADS // C_KV_HEADS, q0=QC0, k0=KC0, v0=VC0, m0=A_HEADS + B_HEADS,
                                             patterns=C_PATTERNS)
        dproj = jnp.concatenate([dq_a, df_a, di_a, dg_a, dq_b, dk_b, dv_b, dq_c, dk_c, dv_c], axis=1)
        dxp = _bwd_act_colsharded(f"proj_dx{l}", [(dproj, wi)], l)
        d_wi = _bwd_w_colsharded(f"proj_dw{l}", h_in, dproj)
        d_res, d_path = dz1, dxp
        mat_grads[l] = _reduce_scatter(l, [d_wi, d_wo, d_wg, d_wu, d_wd], c_idx, kc_idx)
        small[l] = (d_lb, d_nw.reshape(A_HEADS, 8, BLK)[:, 0].sum(0), d_sink[:, 0, 0], d_ln1g[0], d_ln1b[0],
                    d_cw, d_cb[0], d_ln2g[0], d_ln2b[0])
    grad_x = _axpy("grad_x", d_res, d_path)[None]

    g_lb = small[0][0] + small[1][0]
    per_layer = [jnp.stack([small[0][i], small[1][i]]) for i in range(1, 9)]
    small_shapes = [(DEPTH, 4 * BLK), (DEPTH, BLK), (DEPTH, C_HEADS), (DEPTH, D_MODEL), (DEPTH, D_MODEL),
                    (DEPTH, CONV_WIDTH, D_FF), (DEPTH, D_FF), (DEPTH, D_MODEL), (DEPTH, D_MODEL), (BLK,)]
    rows = _rows_for(small_shapes)
    total = _sum_small(_gather_small(_pack([g_lb] + per_layer + [loss_part[0]], rows)))
    g_lb, g_nw, g_sink, g_ln1g, g_ln1b, g_cw_full, g_cb, g_ln2g, g_ln2b, loss_row = _unpack(total, small_shapes)
    loss = loss_row[0]
    g_cw = lax.dynamic_slice_in_dim(g_cw_full, k_me * FF_SHARD, FF_SHARD, axis=2)

    sw = [lb_logits, a_norm_w, c_sinks, ln1_g, ln1_b, conv_w, conv_b, ln2_g, ln2_b]
    sg = [g_lb, g_nw, g_sink, g_ln1g, g_ln1b, g_cw, g_cb, g_ln2g, g_ln2b]
    sm = [m_lb_logits, m_a_norm_w, m_c_sinks, m_ln1_g, m_ln1_b, m_conv_w, m_conv_b, m_ln2_g, m_ln2_b]
    sv = [v_lb_logits, v_a_norm_w, v_c_sinks, v_ln1_g, v_ln1_b, v_conv_w, v_conv_b, v_ln2_g, v_ln2_b]
    shapes = [a.shape for a in sw]
    prow = _rows_for(shapes)
    sd, snm, snv = (_unpack(b, shapes) for b in _adamw_small(_pack(sw, prow), _pack(sg, prow), _pack(sm, prow), _pack(sv, prow)))

    names = ["w_in", "w_out", "w_gate", "w_up", "w_down"]
    mw = [w_in, w_out, w_gate, w_up, w_down]
    mm = [m_w_in, m_w_out, m_w_gate, m_w_up, m_w_down]
    mv = [v_w_in, v_w_out, v_w_gate, v_w_up, v_w_down]
    mg, md, mnm, mnv = [], [], [], []
    for i in range(5):
        go, d, nm, nv = _adamw_matrix(f"adamw_{names[i]}", mw[i], [mat_grads[0][i], mat_grads[1][i]], mm[i], mv[i])
        mg.append(go), md.append(d), mnm.append(nm), mnv.append(nv)

    def ordered(mat, sm_):
        return [mat[0], sm_[0], sm_[1], sm_[2], mat[1], sm_[3], sm_[4], mat[2], mat[3], sm_[5], sm_[6], mat[4], sm_[7], sm_[8]]

    return (loss, grad_x, *ordered(mg, sg), *ordered(md, sd), *ordered(mnm, snm), *ordered(mnv, snv))
```

```python
import functools
import math

import jax
import jax.numpy as jnp
from jax import lax
from jax.experimental import pallas as pl
from jax.experimental.pallas import tpu as pltpu

F32 = jnp.float32
BF16 = jnp.bfloat16

D_MODEL = 2048
SEQ = 2048
DEPTH = 2
HEAD_DIM = 128
A_HEADS = 4
B_HEADS = 6
C_HEADS = 6
C_KV_HEADS = 2
A_CHUNK = 16
DILATED_PATTERNS = ((128, 1), (512, 4), (2048, 16))
C_WINDOW = 128
ROPE_THETA = 500000.0
ROPE_DIM = HEAD_DIM // 4
D_FF = 5632
CONV_WIDTH = 3
LN_EPS = 1e-5
ALPHA = (2 * DEPTH) ** 0.25
IN_WIDTH = 5632
MIX_WIDTH = 2048
ADAM_LR = 0.001
ADAM_B1 = 0.9
ADAM_B2 = 0.999
ADAM_EPS = 1e-08
ADAM_WD = 0.01
ADAM_STEP = 10

N_CHIPS = 4
N_DEV = 8
FF_SHARD = D_FF // N_CHIPS
OUT_SHARD = MIX_WIDTH // N_CHIPS
BLK = 128
N_CHUNK = SEQ // A_CHUNK
SLAB = 32

QA0, FA0, IA0, GA0 = 0, 4, 8, 12
QB0, KB0, VB0 = 16, 22, 28
QC0, KC0, VC0 = 34, 40, 42

VMEM_LIMIT_V7X = 56 * 1024 * 1024
HI = lax.Precision.HIGHEST
MESH = pl.DeviceIdType.MESH


def _cp(sem=None, vmem=VMEM_LIMIT_V7X, **kw):
    return pltpu.CompilerParams(dimension_semantics=sem, vmem_limit_bytes=vmem, **kw)


def _sigmoid(x):
    return 1.0 / (1.0 + jnp.exp(-x))


def _mm(name, pairs, dims, grid, a_specs, b_specs, out_spec, out_shape, nk=1, acc_shape=None):
    n_pairs = len(pairs)

    def body(*refs):
        o_ref = refs[2 * n_pairs]
        part = None
        for p in range(n_pairs):
            a = refs[2 * p][...].astype(BF16)
            b = refs[2 * p + 1][...].astype(BF16)
            t = lax.dot_general(a, b, dims, preferred_element_type=F32)
            part = t if part is None else part + t
        if nk == 1:
            o_ref[...] = part.astype(o_ref.dtype)
        else:
            acc = refs[2 * n_pairs + 1]
            k = pl.program_id(len(grid) - 1)

            @pl.when(k == 0)
            def _():
                acc[...] = part

            @pl.when(k > 0)
            def _():
                acc[...] += part

            @pl.when(k == nk - 1)
            def _():
                o_ref[...] = acc[...].astype(o_ref.dtype)

    in_specs, args = [], []
    for (a, b), sa, sb in zip(pairs, a_specs, b_specs):
        in_specs += [sa, sb]
        args += [a, b]
    sem = ("parallel",) * (len(grid) - (1 if nk > 1 else 0)) + (("arbitrary",) if nk > 1 else ())
    return pl.pallas_call(
        body, name=name, grid=grid, in_specs=in_specs, out_specs=out_spec, out_shape=out_shape,
        scratch_shapes=[pltpu.VMEM(acc_shape, F32)] if nk > 1 else [],
        compiler_params=_cp(sem),
    )(*args)


NN = (((1,), (0,)), ((), ()))
NT = (((1,), (1,)), ((), ()))
TN = (((0,), (0,)), ((), ()))
TM = 1024


def _fwd_colsharded(name, x, w_stk):
    return _mm(name, [(x, w_stk)], NN, (N_CHIPS, SEQ // TM),
               [pl.BlockSpec((TM, D_MODEL), lambda j, i: (i, 0))],
               [pl.BlockSpec((None, D_MODEL, FF_SHARD), lambda j, i: (j, 0, 0))],
               pl.BlockSpec((TM, FF_SHARD), lambda j, i: (i, j)),
               jax.ShapeDtypeStruct((SEQ, D_FF), F32))


def _fwd_rowsharded(name, a, w_stk, shard):
    tn = 1024
    return _mm(name, [(a, w_stk)], NN, (SEQ // TM, D_MODEL // tn, N_CHIPS),
               [pl.BlockSpec((TM, shard), lambda i, j, k: (i, k))],
               [pl.BlockSpec((None, shard, tn), lambda i, j, k: (k, 0, j))],
               pl.BlockSpec((TM, tn), lambda i, j, k: (i, j)),
               jax.ShapeDtypeStruct((SEQ, D_MODEL), F32), nk=N_CHIPS, acc_shape=(TM, tn))


def _bwd_act_colsharded(name, pairs):
    tn = 1024
    n = len(pairs)
    return _mm(name, pairs, NT, (SEQ // TM, D_MODEL // tn, N_CHIPS),
               [pl.BlockSpec((TM, FF_SHARD), lambda i, j, k: (i, k))] * n,
               [pl.BlockSpec((None, tn, FF_SHARD), lambda i, j, k: (k, j, 0))] * n,
               pl.BlockSpec((TM, tn), lambda i, j, k: (i, j)),
               jax.ShapeDtypeStruct((SEQ, D_MODEL), F32), nk=N_CHIPS, acc_shape=(TM, tn))


def _bwd_act_rowsharded(name, dy, w_stk, shard):
    return _mm(name, [(dy, w_stk)], NT, (N_CHIPS, SEQ // TM),
               [pl.BlockSpec((TM, D_MODEL), lambda j, i: (i, 0))],
               [pl.BlockSpec((None, shard, D_MODEL), lambda j, i: (j, 0, 0))],
               pl.BlockSpec((TM, shard), lambda j, i: (i, j)),
               jax.ShapeDtypeStruct((SEQ, N_CHIPS * shard), F32))


def _bwd_w_colsharded(name, x, dy):
    tm = 512
    return _mm(name, [(x, dy)], TN, (N_CHIPS, D_MODEL // tm),
               [pl.BlockSpec((SEQ, tm), lambda j, i: (0, i))],
               [pl.BlockSpec((SEQ, FF_SHARD), lambda j, i: (0, j))],
               pl.BlockSpec((None, tm, FF_SHARD), lambda j, i: (j, i, 0)),
               jax.ShapeDtypeStruct((N_CHIPS, D_MODEL, FF_SHARD), F32))


def _bwd_w_rowsharded(name, a, dy, shard):
    tn = 1024
    return _mm(name, [(a, dy)], TN, (N_CHIPS, D_MODEL // tn),
               [pl.BlockSpec((SEQ, shard), lambda j, i: (0, j))],
               [pl.BlockSpec((SEQ, tn), lambda j, i: (0, i))],
               pl.BlockSpec((None, shard, tn), lambda j, i: (j, 0, i)),
               jax.ShapeDtypeStruct((N_CHIPS, shard, D_MODEL), F32))


TR = 256


def _ln_fwd(name, x, y, g, b):
    def body(x_ref, y_ref, g_ref, b_ref, o_ref):
        z = ALPHA * x_ref[...] + y_ref[...]
        mu = jnp.mean(z, -1, keepdims=True)
        zc = z - mu
        var = jnp.mean(zc * zc, -1, keepdims=True)
        o_ref[...] = zc * lax.rsqrt(var + LN_EPS) * g_ref[...] + b_ref[...]

    row = pl.BlockSpec((TR, D_MODEL), lambda i: (i, 0))
    vec = pl.BlockSpec((1, D_MODEL), lambda i: (0, 0))
    return pl.pallas_call(body, name=name, grid=(SEQ // TR,), in_specs=[row, row, vec, vec], out_specs=row,
                          out_shape=jax.ShapeDtypeStruct((SEQ, D_MODEL), F32),
                          compiler_params=_cp(("parallel",)))(x, y, g.reshape(1, -1), b.reshape(1, -1))


def _ln_bwd(name, x, y, g, d_res, d_path):
    has_res = d_res is not None

    def body(*refs):
        if has_res:
            x_ref, y_ref, g_ref, r_ref, p_ref, dz_ref, dg_ref, db_ref = refs
            dout = ALPHA * r_ref[...] + p_ref[...]
        else:
            x_ref, y_ref, g_ref, p_ref, dz_ref, dg_ref, db_ref = refs
            dout = p_ref[...]
        z = ALPHA * x_ref[...] + y_ref[...]
        mu = jnp.mean(z, -1, keepdims=True)
        zc = z - mu
        rstd = lax.rsqrt(jnp.mean(zc * zc, -1, keepdims=True) + LN_EPS)
        zh = zc * rstd
        dzh = dout * g_ref[...]
        dz_ref[...] = rstd * (dzh - jnp.mean(dzh, -1, keepdims=True) - zh * jnp.mean(dzh * zh, -1, keepdims=True))
        pg = jnp.sum(dout * zh, 0, keepdims=True)
        pb = jnp.sum(dout, 0, keepdims=True)

        @pl.when(pl.program_id(0) == 0)
        def _():
            dg_ref[...] = pg
            db_ref[...] = pb

        @pl.when(pl.program_id(0) > 0)
        def _():
            dg_ref[...] += pg
            db_ref[...] += pb

    row = pl.BlockSpec((TR, D_MODEL), lambda i: (i, 0))
    vec = pl.BlockSpec((1, D_MODEL), lambda i: (0, 0))
    args = [x, y, g.reshape(1, -1)] + ([d_res] if has_res else []) + [d_path]
    in_specs = [row, row, vec] + ([row] if has_res else []) + [row]
    vshape = jax.ShapeDtypeStruct((1, D_MODEL), F32)
    return pl.pallas_call(body, name=name, grid=(SEQ // TR,), in_specs=in_specs, out_specs=[row, vec, vec],
                          out_shape=[jax.ShapeDtypeStruct((SEQ, D_MODEL), F32), vshape, vshape],
                          compiler_params=_cp(("arbitrary",)))(*args)


def _loss_head(y, target):
    def body(y_ref, t_ref, dy_ref, l_ref):
        e = y_ref[...] - t_ref[...]
        dy_ref[...] = e * (1.0 / D_MODEL)
        part = jnp.full((8, BLK), 0.5 / D_MODEL * jnp.sum(e * e), F32)

        @pl.when(pl.program_id(0) == 0)
        def _():
            l_ref[...] = part

        @pl.when(pl.program_id(0) > 0)
        def _():
            l_ref[...] += part

    row = pl.BlockSpec((TR, D_MODEL), lambda i: (i, 0))
    return pl.pallas_call(body, name="loss_head", grid=(SEQ // TR,), in_specs=[row, row],
                          out_specs=[row, pl.BlockSpec((8, BLK), lambda i: (0, 0))],
                          out_shape=[jax.ShapeDtypeStruct((SEQ, D_MODEL), F32), jax.ShapeDtypeStruct((8, BLK), F32)],
                          compiler_params=_cp(("arbitrary",)))(y, target)


def _axpy(name, a, b):
    def body(a_ref, b_ref, o_ref):
        o_ref[...] = ALPHA * a_ref[...] + b_ref[...]

    row = pl.BlockSpec((TR, D_MODEL), lambda i: (i, 0))
    return pl.pallas_call(body, name=name, grid=(SEQ // TR,), in_specs=[row, row], out_specs=row,
                          out_shape=jax.ShapeDtypeStruct((SEQ, D_MODEL), F32),
                          compiler_params=_cp(("parallel",)))(a, b)


TC = 512


def _shift_down(x, s, rows):
    if s == 0:
        return x
    return jnp.where(rows >= s, pltpu.roll(x, s, axis=0), 0.0)


def _shift_up(x, s, rows):
    if s == 0:
        return x
    return jnp.where(rows < SEQ - s, pltpu.roll(x, SEQ - s, axis=0), 0.0)


def _conv_gate_fwd(name, g, u, cw, cb):
    def body(g_ref, u_ref, w_ref, b_ref, h_ref):
        gg = g_ref[...]
        rows = lax.broadcasted_iota(jnp.int32, gg.shape, 0)
        gc = b_ref[...] + w_ref[2:3, :] * gg
        gc = gc + w_ref[1:2, :] * _shift_down(gg, 1, rows)
        gc = gc + w_ref[0:1, :] * _shift_down(gg, 2, rows)
        h_ref[...] = (gc * _sigmoid(gc) * u_ref[...]).astype(BF16)

    col = pl.BlockSpec((SEQ, TC), lambda j: (0, j))
    return pl.pallas_call(body, name=name, grid=(D_FF // TC,),
                          in_specs=[col, col, pl.BlockSpec((CONV_WIDTH, TC), lambda j: (0, j)),
                                    pl.BlockSpec((1, TC), lambda j: (0, j))],
                          out_specs=col, out_shape=jax.ShapeDtypeStruct((SEQ, D_FF), BF16),
                          compiler_params=_cp(("parallel",)))(g, u, cw, cb.reshape(1, -1))


def _conv_gate_bwd(name, g, u, cw, cb, dh):
    def body(g_ref, u_ref, w_ref, b_ref, dh_ref, dg_ref, du_ref, dw_ref, db_ref):
        gg = g_ref[...]
        rows = lax.broadcasted_iota(jnp.int32, gg.shape, 0)
        g1 = _shift_down(gg, 1, rows)
        g2 = _shift_down(gg, 2, rows)
        gc = b_ref[...] + w_ref[2:3, :] * gg + w_ref[1:2, :] * g1 + w_ref[0:1, :] * g2
        sg = _sigmoid(gc)
        act = gc * sg
        dh = dh_ref[...]
        du_ref[...] = dh * act
        dgc = dh * u_ref[...] * (sg * (1.0 + gc * (1.0 - sg)))
        db_ref[...] = jnp.sum(dgc, 0, keepdims=True)
        dw_ref[2:3, :] = jnp.sum(dgc * gg, 0, keepdims=True)
        dw_ref[1:2, :] = jnp.sum(dgc * g1, 0, keepdims=True)
        dw_ref[0:1, :] = jnp.sum(dgc * g2, 0, keepdims=True)
        dg_ref[...] = (w_ref[2:3, :] * dgc + w_ref[1:2, :] * _shift_up(dgc, 1, rows)
                       + w_ref[0:1, :] * _shift_up(dgc, 2, rows))

    col = pl.BlockSpec((SEQ, TC), lambda j: (0, j))
    w3 = pl.BlockSpec((CONV_WIDTH, TC), lambda j: (0, j))
    w1 = pl.BlockSpec((1, TC), lambda j: (0, j))
    big = jax.ShapeDtypeStruct((SEQ, D_FF), F32)
    return pl.pallas_call(body, name=name, grid=(D_FF // TC,), in_specs=[col, col, w3, w1, col],
                          out_specs=[col, col, w3, w1],
                          out_shape=[big, big, jax.ShapeDtypeStruct((CONV_WIDTH, D_FF), F32),
                                     jax.ShapeDtypeStruct((1, D_FF), F32)],
                          compiler_params=_cp(("parallel",)))(g, u, cw, cb.reshape(1, -1), dh)


def _lbs_of(logits, layer):
    m = jnp.max(logits, 0, keepdims=True)
    e = jnp.exp(logits - m)
    p = e / jnp.sum(e, 0, keepdims=True)
    lb = jnp.zeros((1, BLK), F32)
    for r in range(1, layer + 1):
        lb = lb + p[r:r + 1, :]
    return lb, p


def _dlogits_of(p, dlb, layer):
    rows = lax.broadcasted_iota(jnp.int32, p.shape, 0)
    dp = jnp.where((rows >= 1) & (rows <= layer), dlb, 0.0)
    return p * (dp - jnp.sum(p * dp, 0, keepdims=True))


SROWS = SLAB * A_CHUNK
N_SLAB = N_CHUNK // SLAB


def _chunk_prefix(x, rowi):
    for s in (1, 2, 4, 8):
        x = x + jnp.where(rowi >= s, pltpu.roll(x, s, axis=0), 0.0)
    return x


def _chunk_suffix(x, rowi):
    for s in (1, 2, 4, 8):
        x = x + jnp.where(rowi < A_CHUNK - s, pltpu.roll(x, SROWS - s, axis=0), 0.0)
    return x


def _c3(x):
    return x.reshape(SLAB, A_CHUNK, BLK)


def _c2(x):
    return x.reshape(SROWS, BLK)


def _lane_sum_b(x2, ones):
    return jnp.dot(x2, ones, preferred_element_type=F32, precision=HI)


def _bmm(eq, a, b):
    return jnp.einsum(eq, a, b, preferred_element_type=F32, precision=HI)


def _slab_rows(s):
    return pl.ds(s * SROWS, SROWS)


def _hgrn_prep(q, f, lb):
    rowi = lax.broadcasted_iota(jnp.int32, (SROWS, BLK), 0) & (A_CHUNK - 1)
    sq = _sigmoid(q)
    qc = q * sq
    sf = _sigmoid(f)
    fg = lb + (1.0 - lb) * sf
    kc = 1.0 - fg
    b = _chunk_prefix(jnp.log(fg), rowi)
    b3 = _c3(b)
    blast = b3[:, A_CHUNK - 1:A_CHUNK, :]
    eb = jnp.exp(b)
    ekb = _c2(jnp.exp(blast - b3))
    dec = jnp.exp(blast.reshape(SLAB, BLK))
    return rowi, sq, qc, sf, fg, kc, b, eb, ekb, dec


def _hgrn_slab_states(s, carry, v, ke, dec, dec_ref, u_ref, st_ref):
    dec_ref[pl.ds(s * SLAB, SLAB), :] = dec
    u_ref[...] = _bmm('ncv,nck->nvk', _c3(v), _c3(ke))

    def step(j, c):
        st_ref[j] = c
        return dec_ref[pl.ds(s * SLAB + j, 1), :] * c + u_ref[j]

    return lax.fori_loop(0, SLAB, step, carry)


def _hgrn_fwd(name, proj, lb_logits, nw, layer):
    def body(q_ref, f_ref, i_ref, g_ref, lg_ref, nw_ref, out_ref, raw_ref, dec_ref, u_ref, st_ref):
        lb, _ = _lbs_of(lg_ref[...], layer)
        ones = jnp.ones((BLK, BLK), F32)
        carry = jnp.zeros((BLK, BLK), F32)
        for s in range(N_SLAB):
            rows = _slab_rows(s)
            v = i_ref[rows, :]
            rowi, sq, qc, sf, fg, kc, b, eb, ekb, dec = _hgrn_prep(q_ref[rows, :], f_ref[rows, :], lb)
            carry = _hgrn_slab_states(s, carry, v, kc * ekb, dec, dec_ref, u_ref, st_ref)
            o = _c2(_bmm('nck,nvk->ncv', _c3(qc * eb), st_ref[...]))
            qc3, kc3, b3, v3, row3 = _c3(qc), _c3(kc), _c3(b), _c3(v), _c3(rowi)
            for j in range(A_CHUNK):
                dj = jnp.exp(jnp.where(row3 >= j, b3 - b3[:, j:j + 1, :], -jnp.inf))
                a = _lane_sum_b(_c2(qc3 * dj * kc3[:, j:j + 1, :]), ones)
                o = o + a * _c2(jnp.broadcast_to(v3[:, j:j + 1, :], v3.shape))
            raw_ref[rows, :] = o
            r = lax.rsqrt(jnp.mean(o * o, -1, keepdims=True) + LN_EPS)
            gg = g_ref[rows, :]
            out_ref[rows, :] = o * r * nw_ref[...] * (gg * _sigmoid(gg))

    def colblk(c0):
        return pl.BlockSpec((SEQ, BLK), lambda h: (0, c0 + h))

    big = jax.ShapeDtypeStruct((SEQ, A_HEADS * BLK), F32)
    return pl.pallas_call(
        body, name=name, grid=(A_HEADS,),
        in_specs=[colblk(QA0), colblk(FA0), colblk(IA0), colblk(GA0),
                  pl.BlockSpec((DEPTH, BLK), lambda h: (0, h)), pl.BlockSpec((1, BLK), lambda h: (0, 0))],
        out_specs=[colblk(0), colblk(0)], out_shape=[big, big],
        scratch_shapes=[pltpu.VMEM((N_CHUNK, BLK), F32), pltpu.VMEM((SLAB, BLK, BLK), F32),
                        pltpu.VMEM((SLAB, BLK, BLK), F32)],
        compiler_params=_cp(("parallel",)))(proj, proj, proj, proj, lb_logits, nw.reshape(1, -1))


def _hgrn_bwd(name, proj, raw, dmix, lb_logits, nw, layer):
    def body(q_ref, f_ref, i_ref, g_ref, raw_ref, do_ref, lg_ref, nw_ref,
             dq_ref, df_ref, di_ref, dg_ref, dnw_ref, dlg_ref,
             dec_ref, u_ref, st_ref, h_ref, dbs_ref, dkc_ref, tot_ref):
        lb, p = _lbs_of(lg_ref[...], layer)
        ones = jnp.ones((BLK, BLK), F32)
        nwv = nw_ref[...]

        carry = jnp.zeros((BLK, BLK), F32)
        for s in range(N_SLAB):
            rows = _slab_rows(s)
            rowi, sq, qc, sf, fg, kc, b, eb, ekb, dec = _hgrn_prep(q_ref[rows, :], f_ref[rows, :], lb)
            carry = _hgrn_slab_states(s, carry, i_ref[rows, :], kc * ekb, dec, dec_ref, u_ref,
                                      st_ref.at[pl.ds(s * SLAB, SLAB)])

        carry = jnp.zeros((BLK, BLK), F32)
        dnw = jnp.zeros((1, BLK), F32)
        for s in reversed(range(N_SLAB)):
            rows = _slab_rows(s)
            q, v = q_ref[rows, :], i_ref[rows, :]
            rowi, sq, qc, sf, fg, kc, b, eb, ekb, dec = _hgrn_prep(q, f_ref[rows, :], lb)
            ke = kc * ekb
            qe = qc * eb

            o = raw_ref[rows, :]
            gg = g_ref[rows, :]
            sgg = _sigmoid(gg)
            dout = do_ref[rows, :]
            r = lax.rsqrt(jnp.mean(o * o, -1, keepdims=True) + LN_EPS)
            oh = o * r
            dg_ref[rows, :] = dout * oh * nwv * (sgg * (1.0 + gg * (1.0 - sgg)))
            dn = dout * (gg * sgg)
            dnw = dnw + jnp.sum(dn * oh, 0, keepdims=True)
            doh = dn * nwv
            do = r * (doh - oh * jnp.mean(doh * oh, -1, keepdims=True))
            do3, qe3, v3, ke3 = _c3(do), _c3(qe), _c3(v), _c3(ke)

            u_ref[...] = _bmm('ncv,nck->nvk', do3, qe3)

            def step(jj, c, s=s):
                j = SLAB - 1 - jj
                h_ref[j] = c
                return u_ref[j] + dec_ref[pl.ds(s * SLAB + j, 1), :] * c

            carry = lax.fori_loop(0, SLAB, step, carry)

            hh = h_ref[...]
            dqc = _c2(_bmm('ncv,nvk->nck', do3, st_ref[pl.ds(s * SLAB, SLAB)])) * eb
            dkc = _c2(_bmm('ncv,nvk->nck', v3, hh)) * ekb
            dv = _c2(_bmm('nck,nvk->ncv', ke3, hh))

            qc3, kc3, b3, row3 = _c3(qc), _c3(kc), _c3(b), _c3(rowi)
            for j in range(A_CHUNK):
                dj = jnp.exp(jnp.where(row3 >= j, b3 - b3[:, j:j + 1, :], -jnp.inf))
                kj = kc3[:, j:j + 1, :]
                vj = jnp.broadcast_to(v3[:, j:j + 1, :], v3.shape)
                att = _c3(_lane_sum_b(_c2(qc3 * dj * kj), ones))
                datt = _c3(_lane_sum_b(_c2(do3 * vj), ones))
                md = dj * datt
                dqc = dqc + _c2(md * kj)
                sel = row3 == j
                dkc = dkc + _c2(jnp.where(sel, jnp.sum(md * qc3, 1, keepdims=True), 0.0))
                dv = dv + _c2(jnp.where(sel, jnp.sum(att * do3, 1, keepdims=True), 0.0))
            di_ref[rows, :] = dv
            dq_ref[rows, :] = dqc * (sq * (1.0 + q * (1.0 - sq)))

            dbs = _chunk_suffix(qc * dqc - kc * dkc, rowi)
            dbs_ref[rows, :] = dbs
            dkc_ref[rows, :] = dkc
            tot_ref[pl.ds(s * SLAB, SLAB), :] = _c3(dbs)[:, 0:1, :].reshape(SLAB, BLK)
        dnw_ref[...] = jnp.broadcast_to(dnw, (8, BLK))

        rn = lax.broadcasted_iota(jnp.int32, (N_CHUNK, N_CHUNK), 0)
        cn = lax.broadcasted_iota(jnp.int32, (N_CHUNK, N_CHUNK), 1)
        tot_ref[...] = jnp.dot((cn > rn).astype(F32), tot_ref[...], preferred_element_type=F32, precision=HI)
        dlb = jnp.zeros((1, BLK), F32)
        for s in range(N_SLAB):
            rows = _slab_rows(s)
            sf = _sigmoid(f_ref[rows, :])
            fg = lb + (1.0 - lb) * sf
            later = tot_ref[pl.ds(s * SLAB, SLAB), :]
            dlg = _c2(_c3(dbs_ref[rows, :]) + later[:, None, :])
            dfg = dlg / fg - dkc_ref[rows, :]
            df_ref[rows, :] = dfg * (1.0 - lb) * sf * (1.0 - sf)
            dlb = dlb + jnp.sum(dfg * (1.0 - sf), 0, keepdims=True)
        dlg_ref[...] = _dlogits_of(p, dlb, layer)

    def colblk(c0):
        return pl.BlockSpec((SEQ, BLK), lambda h: (0, c0 + h))

    big = jax.ShapeDtypeStruct((SEQ, A_HEADS * BLK), F32)
    return pl.pallas_call(
        body, name=name, grid=(A_HEADS,),
        in_specs=[colblk(QA0), colblk(FA0), colblk(IA0), colblk(GA0), colblk(0), colblk(0),
                  pl.BlockSpec((DEPTH, BLK), lambda h: (0, h)), pl.BlockSpec((1, BLK), lambda h: (0, 0))],
        out_specs=[colblk(0), colblk(0), colblk(0), colblk(0),
                   pl.BlockSpec((8, BLK), lambda h: (h, 0)), pl.BlockSpec((DEPTH, BLK), lambda h: (0, h))],
        out_shape=[big, big, big, big, jax.ShapeDtypeStruct((A_HEADS * 8, BLK), F32),
                   jax.ShapeDtypeStruct((DEPTH, A_HEADS * BLK), F32)],
        scratch_shapes=[pltpu.VMEM((N_CHUNK, BLK), F32), pltpu.VMEM((SLAB, BLK, BLK), F32),
                        pltpu.VMEM((N_CHUNK, BLK, BLK), F32), pltpu.VMEM((SLAB, BLK, BLK), F32),
                        pltpu.VMEM((SEQ, BLK), F32), pltpu.VMEM((SEQ, BLK), F32), pltpu.VMEM((N_CHUNK, BLK), F32)],
        compiler_params=_cp(("parallel",)))(proj, proj, proj, proj, raw, dmix, lb_logits, nw.reshape(1, -1))


SCALE = HEAD_DIM ** -0.5


def _rope_tables():
    half = ROPE_DIM // 2
    inv = ROPE_THETA ** (-jnp.arange(0, ROPE_DIM, 2, dtype=F32) / ROPE_DIM)
    ang = jnp.arange(SEQ, dtype=F32)[:, None] * inv[None, :]
    cos, sin = jnp.cos(ang), jnp.sin(ang)
    pad = jnp.zeros((SEQ, HEAD_DIM - ROPE_DIM), F32)
    zero = jnp.zeros((SEQ, half), F32)
    c = jnp.concatenate([cos, cos, pad + 1.0], 1)
    s_lo = jnp.concatenate([zero, sin, pad], 1)
    s_hi = jnp.concatenate([-sin, zero, pad], 1)
    return c, s_lo, s_hi


def _rope(x, c, s_lo, s_hi):
    half = ROPE_DIM // 2
    return x * c + pltpu.roll(x, half, axis=1) * s_lo + pltpu.roll(x, HEAD_DIM - half, axis=1) * s_hi


def _unrope(dy, c, s_lo, s_hi):
    half = ROPE_DIM // 2
    return dy * c + pltpu.roll(dy * s_lo, HEAD_DIM - half, axis=1) + pltpu.roll(dy * s_hi, half, axis=1)


def _rows(start, size, stride):
    return pl.ds(start, size) if stride == 1 else pl.ds(start, size, stride=stride)


def _band_blocks(patterns):
    out = []
    for p, (max_lag, dil) in enumerate(patterns):
        nb = SEQ // dil // BLK
        for r in range(dil):
            for n in range(nb):
                lo = max(n - 1, 0)
                kn = (n - lo + 1) * BLK
                out.append((p, _rows(r + n * BLK * dil, BLK, dil), _rows(r + lo * BLK * dil, kn, dil), kn,
                            (n - lo) * BLK, max_lag))
    return out


def _band_valid(kn, off, max_lag):
    lag = off + lax.broadcasted_iota(jnp.int32, (BLK, kn), 0) - lax.broadcasted_iota(jnp.int32, (BLK, kn), 1)
    return (lag >= 0) & (lag <= max_lag)


def _attn_fwd(name, proj, tables, sink_b, *, n_heads, rep, q0, k0, v0, patterns):
    n_pat = len(patterns)
    blocks = _band_blocks(patterns)
    has_sink = sink_b is not None

    def body(*refs):
        if has_sink:
            q_ref, k_ref, v_ref, c_ref, sl_ref, sh_ref, sink_ref, o_ref, lse_ref, qr, kr, op = refs
            sk = sink_ref[0:1, 0:1]
        else:
            q_ref, k_ref, v_ref, c_ref, sl_ref, sh_ref, o_ref, lse_ref, qr, kr, op = refs
        c, s_lo, s_hi = c_ref[...], sl_ref[...], sh_ref[...]
        qr[...] = _rope(q_ref[...], c, s_lo, s_hi)
        kr[...] = _rope(k_ref[...], c, s_lo, s_hi)
        for p, qrows, krows, kn, off, max_lag in blocks:
            qb = qr[qrows, :].astype(BF16)
            kb = kr[krows, :].astype(BF16)
            vb = v_ref[krows, :].astype(BF16)
            s = lax.dot_general(qb, kb, NT, preferred_element_type=F32) * SCALE
            s = jnp.where(_band_valid(kn, off, max_lag), s, -jnp.inf)
            m = jnp.max(s, -1, keepdims=True)
            if has_sink:
                m = jnp.maximum(m, sk)
            e = jnp.exp(s - m)
            den = jnp.sum(e, -1, keepdims=True)
            if has_sink:
                den = den + jnp.exp(sk - m)
            o = jnp.dot(e.astype(BF16), vb, preferred_element_type=F32) / den
            op.at[p][qrows, :] = o
            lse_ref.at[p][qrows, :] = jnp.broadcast_to(m + jnp.log(den), (BLK, BLK))
        if n_pat == 1:
            o_ref[...] = op[0]
        else:
            ls = [lse_ref[p] for p in range(n_pat)]
            m = functools.reduce(jnp.maximum, ls)
            es = [jnp.exp(l - m) for l in ls]
            tot = functools.reduce(jnp.add, es)
            acc = None
            for p in range(n_pat):
                t = (es[p] / tot) * op[p]
                acc = t if acc is None else acc + t
            o_ref[...] = acc

    def colblk(fn):
        return pl.BlockSpec((SEQ, BLK), fn)

    tab = pl.BlockSpec((SEQ, BLK), lambda h: (0, 0))
    in_specs = [colblk(lambda h: (0, q0 + h)), colblk(lambda h: (0, k0 + h // rep)), colblk(lambda h: (0, v0 + h // rep)),
                tab, tab, tab]
    args = [proj, proj, proj, *tables]
    if has_sink:
        in_specs.append(pl.BlockSpec((None, 8, BLK), lambda h: (h, 0, 0)))
        args.append(sink_b)
    return pl.pallas_call(
        body, name=name, grid=(n_heads,), in_specs=in_specs,
        out_specs=[colblk(lambda h: (0, h)), pl.BlockSpec((None, n_pat, SEQ, BLK), lambda h: (h, 0, 0, 0))],
        out_shape=[jax.ShapeDtypeStruct((SEQ, n_heads * BLK), F32),
                   jax.ShapeDtypeStruct((n_heads, n_pat, SEQ, BLK), F32)],
        scratch_shapes=[pltpu.VMEM((SEQ, BLK), F32), pltpu.VMEM((SEQ, BLK), F32), pltpu.VMEM((n_pat, SEQ, BLK), F32)],
        compiler_params=_cp(("parallel",)))(*args)


def _attn_bwd(name, proj, mixed, dmix, lse, tables, sink_b, *, n_kv, rep, q0, k0, v0, m0, patterns):
    n_pat = len(patterns)
    n_heads = n_kv * rep
    blocks = _band_blocks(patterns)
    has_sink = sink_b is not None

    def body(*refs):
        if has_sink:
            (q_ref, k_ref, v_ref, o_ref, do_ref, lse_ref, c_ref, sl_ref, sh_ref, sink_ref,
             dq_ref, dk_ref, dv_ref, dsk_ref, qr, kr, dqa, dka, dva, dd, ww) = refs
        else:
            (q_ref, k_ref, v_ref, o_ref, do_ref, lse_ref, c_ref, sl_ref, sh_ref,
             dq_ref, dk_ref, dv_ref, dsk_ref, qr, kr, dqa, dka, dva, dd, ww) = refs
        j = pl.program_id(1)
        c, s_lo, s_hi = c_ref[...], sl_ref[...], sh_ref[...]
        qr[...] = _rope(q_ref[...], c, s_lo, s_hi)
        kr[...] = _rope(k_ref[...], c, s_lo, s_hi)
        dcol = jnp.sum(do_ref[...] * o_ref[...], -1, keepdims=True)
        dd[...] = jnp.broadcast_to(dcol, (SEQ, BLK))
        if n_pat == 1:
            ww[0] = jnp.ones((SEQ, BLK), F32)
        else:
            ls = [lse_ref[p] for p in range(n_pat)]
            m = functools.reduce(jnp.maximum, ls)
            es = [jnp.exp(l - m) for l in ls]
            tot = functools.reduce(jnp.add, es)
            for p in range(n_pat):
                ww[p] = es[p] / tot
        dqa[...] = jnp.zeros((SEQ, BLK), F32)

        @pl.when(j == 0)
        def _():
            dka[...] = jnp.zeros((SEQ, BLK), F32)
            dva[...] = jnp.zeros((SEQ, BLK), F32)

        for p, qrows, krows, kn, off, max_lag in blocks:
            qb = qr[qrows, :].astype(BF16)
            kb = kr[krows, :].astype(BF16)
            vb = v_ref[krows, :].astype(BF16)
            dob = do_ref[qrows, :].astype(BF16)
            lcol = lse_ref.at[p][qrows, :][:, 0:1]
            wcol = ww.at[p][qrows, :][:, 0:1]
            dcb = dd[qrows, :][:, 0:1]
            s = lax.dot_general(qb, kb, NT, preferred_element_type=F32) * SCALE
            a = jnp.where(_band_valid(kn, off, max_lag), jnp.exp(s - lcol), 0.0) * wcol
            dp = lax.dot_general(dob, vb, NT, preferred_element_type=F32)
            ds = (a * (dp - dcb) * SCALE).astype(BF16)
            dqa[qrows, :] += jnp.dot(ds, kb, preferred_element_type=F32)
            dka[krows, :] += lax.dot_general(ds, qb, TN, preferred_element_type=F32)
            dva[krows, :] += lax.dot_general(a.astype(BF16), dob, TN, preferred_element_type=F32)

        if has_sink:
            sk = sink_ref[0:1, 0:1]
            ps = jnp.exp(sk - lse_ref[0][:, 0:1])
            dsk_ref[...] = jnp.full((8, BLK), -jnp.sum(ps * dcol), F32)
        else:
            dsk_ref[...] = jnp.zeros((8, BLK), F32)
        dq_ref[...] = _unrope(dqa[...], c, s_lo, s_hi)

        @pl.when(j == rep - 1)
        def _():
            dk_ref[...] = _unrope(dka[...], c, s_lo, s_hi)
            dv_ref[...] = dva[...]

    def colblk(fn):
        return pl.BlockSpec((SEQ, BLK), fn)

    tab = pl.BlockSpec((SEQ, BLK), lambda g, j: (0, 0))
    in_specs = [colblk(lambda g, j: (0, q0 + g * rep + j)), colblk(lambda g, j: (0, k0 + g)), colblk(lambda g, j: (0, v0 + g)),
                colblk(lambda g, j: (0, m0 + g * rep + j)), colblk(lambda g, j: (0, m0 + g * rep + j)),
                pl.BlockSpec((None, n_pat, SEQ, BLK), lambda g, j: (g * rep + j, 0, 0, 0)), tab, tab, tab]
    args = [proj, proj, proj, mixed, dmix, lse, *tables]
    if has_sink:
        in_specs.append(pl.BlockSpec((None, 8, BLK), lambda g, j: (g * rep + j, 0, 0)))
        args.append(sink_b)
    acc = pltpu.VMEM((SEQ, BLK), F32)
    return pl.pallas_call(
        body, name=name, grid=(n_kv, rep), in_specs=in_specs,
        out_specs=[colblk(lambda g, j: (0, g * rep + j)), colblk(lambda g, j: (0, g)), colblk(lambda g, j: (0, g)),
                   pl.BlockSpec((None, 8, BLK), lambda g, j: (g * rep + j, 0, 0))],
        out_shape=[jax.ShapeDtypeStruct((SEQ, n_heads * BLK), F32), jax.ShapeDtypeStruct((SEQ, n_kv * BLK), F32),
                   jax.ShapeDtypeStruct((SEQ, n_kv * BLK), F32), jax.ShapeDtypeStruct((n_heads, 8, BLK), F32)],
        scratch_shapes=[acc, acc, acc, acc, acc, acc, pltpu.VMEM((n_pat, SEQ, BLK), F32)],
        compiler_params=_cp(("parallel", "arbitrary")))(*args)


B_PATTERNS = tuple((w // d, d) for w, d in DILATED_PATTERNS)
C_PATTERNS = ((C_WINDOW - 1, 1),)


ANY = pl.BlockSpec(memory_space=pl.ANY)
CHIP_MASKS = ((1, 0), (0, 1), (1, 1))


def _coords():
    return lax.axis_index("x"), lax.axis_index("y"), lax.axis_index("c")


def _flip(v, m):
    return 1 - v if m else v


def _into_slot(name, w, layer, k_idx, dtype):
    _, rows, cols = w.shape
    tr = rows // 8 if rows % 64 == 0 else rows

    def body(k_ref, w_ref, o_ref):
        o_ref[...] = w_ref[...].astype(dtype)

    return pl.pallas_call(
        body, name=name,
        grid_spec=pltpu.PrefetchScalarGridSpec(
            num_scalar_prefetch=1, grid=(rows // tr,),
            in_specs=[pl.BlockSpec((None, tr, cols), lambda i, k: (layer, i, 0))],
            out_specs=pl.BlockSpec((None, tr, cols), lambda i, k: (k[0], i, 0))),
        out_shape=jax.ShapeDtypeStruct((N_CHIPS, rows, cols), dtype),
        compiler_params=_cp(("parallel",)))(k_idx, w)


HBM_SPEC = pl.BlockSpec(memory_space=pltpu.HBM)
SEM_SPEC = pl.BlockSpec(memory_space=pltpu.SEMAPHORE)
TOKEN_SPEC = pl.BlockSpec(memory_space=pltpu.VMEM)
TOKEN_SHAPE = jax.ShapeDtypeStruct((8, BLK), F32)
DATAFLOW = pltpu.SideEffectType.DATAFLOW_SIDE_EFFECTING


def _hbm(a):
    return pltpu.with_memory_space_constraint(a, pltpu.HBM)


def _hbm_like(bufs):
    return [pltpu.HBM(b.shape, b.dtype) for b in bufs]


def _gather_start(name, stages):
    flat = [b for st in stages for b in st]
    n, ns = len(flat), len(stages)

    def body(*refs):
        ins = refs[:n]
        sems = refs[n:n + 2 * ns]
        token = refs[-1]
        x, y, c = _coords()
        k_me = 2 * x + y
        a = 0
        for s, st in enumerate(stages):
            for i in range(len(st)):
                mine = ins[a].at[k_me, c]
                for m, (mx, my) in enumerate(CHIP_MASKS):
                    pltpu.make_async_remote_copy(src_ref=mine, dst_ref=mine, send_sem=sems[2 * s].at[i * 3 + m],
                                                 recv_sem=sems[2 * s + 1].at[i * 3 + m],
                                                 device_id=(_flip(x, mx), _flip(y, my), c), device_id_type=MESH).start()
                a += 1
        token[...] = jnp.zeros_like(token)

    sem_shapes = []
    for st in stages:
        sem_shapes += [pltpu.SemaphoreType.DMA((3 * len(st),))] * 2
    out = pl.pallas_call(
        body, name=name, in_specs=[HBM_SPEC] * n,
        out_specs=tuple([SEM_SPEC] * (2 * ns) + [HBM_SPEC] * n + [TOKEN_SPEC]),
        out_shape=tuple(sem_shapes + _hbm_like(flat) + [TOKEN_SHAPE]),
        input_output_aliases={i: 2 * ns + i for i in range(n)},
        compiler_params=pltpu.CompilerParams(has_side_effects=DATAFLOW),
    )(*[_hbm(b) for b in flat])
    sems, bufs, token = out[:2 * ns], out[2 * ns:2 * ns + n], out[-1]
    res, a = [], 0
    for s, st in enumerate(stages):
        res.append((sems[2 * s], sems[2 * s + 1], list(bufs[a:a + len(st)])))
        a += len(st)
    return res, token


def _gather_forward(name, stage, after):
    ssem_in, rsem_in, bufs = stage
    n = len(bufs)

    def body(*refs):
        ins = refs[:n]
        s_in, r_in, _ = refs[n:n + 3]
        s_out, r_out = refs[n + 3:n + 5]
        token = refs[-1]
        x, y, c = _coords()
        for i in range(n):
            for m, (mx, my) in enumerate(CHIP_MASKS):
                kp = 2 * _flip(x, mx) + _flip(y, my)
                blk = ins[i].at[kp, c]
                got = pltpu.make_async_remote_copy(src_ref=blk, dst_ref=blk, send_sem=s_in.at[i * 3 + m],
                                                   recv_sem=r_in.at[i * 3 + m], device_id=(x, y, 1 - c), device_id_type=MESH)
                got.wait_send()
                got.wait_recv()
                pltpu.make_async_remote_copy(src_ref=blk, dst_ref=blk, send_sem=s_out.at[i * 3 + m],
                                             recv_sem=r_out.at[i * 3 + m], device_id=(x, y, 1 - c), device_id_type=MESH).start()
        token[...] = jnp.zeros_like(token)

    sem = pltpu.SemaphoreType.DMA((3 * n,))
    out = pl.pallas_call(
        body, name=name, in_specs=[HBM_SPEC] * n + [SEM_SPEC, SEM_SPEC, ANY],
        out_specs=tuple([SEM_SPEC, SEM_SPEC] + [HBM_SPEC] * n + [TOKEN_SPEC]),
        out_shape=tuple([sem, sem] + _hbm_like(bufs) + [TOKEN_SHAPE]),
        input_output_aliases={i: 2 + i for i in range(n)},
        compiler_params=pltpu.CompilerParams(has_side_effects=DATAFLOW),
    )(*bufs, ssem_in, rsem_in, after)
    return (out[0], out[1], list(out[2:2 + n])), out[-1]


def _gather_wait(name, stage, after):
    ssem, rsem, bufs = stage
    n = len(bufs)

    def body(*refs):
        ins = refs[:n]
        s_in, r_in, _ = refs[n:n + 3]
        x, y, c = _coords()
        for i in range(n):
            for m, (mx, my) in enumerate(CHIP_MASKS):
                kp = 2 * _flip(x, mx) + _flip(y, my)
                sent, got = ins[i].at[kp, c], ins[i].at[kp, 1 - c]
                cp = pltpu.make_async_remote_copy(src_ref=sent, dst_ref=got, send_sem=s_in.at[i * 3 + m],
                                                  recv_sem=r_in.at[i * 3 + m], device_id=(x, y, 1 - c), device_id_type=MESH)
                cp.wait_send()
                cp.wait_recv()

    out = pl.pallas_call(
        body, name=name, in_specs=[HBM_SPEC] * n + [SEM_SPEC, SEM_SPEC, ANY],
        out_specs=tuple([HBM_SPEC] * n), out_shape=tuple(_hbm_like(bufs)),
        input_output_aliases={i: i for i in range(n)},
        compiler_params=pltpu.CompilerParams(has_side_effects=DATAFLOW),
    )(*bufs, ssem, rsem, after)
    return list(out)


def _pair_swap(name, grads):
    n = len(grads)

    def body(*refs):
        ins, outs = refs[:n], refs[n:2 * n]
        ssem, rsem = refs[2 * n:]
        x, y, c = _coords()
        cps = []
        for a in range(n):
            for j in range(N_CHIPS):
                cp = pltpu.make_async_remote_copy(src_ref=ins[a].at[j, 1 - c], dst_ref=outs[a].at[j],
                                                  send_sem=ssem.at[a * N_CHIPS + j], recv_sem=rsem.at[a * N_CHIPS + j],
                                                  device_id=(x, y, 1 - c), device_id_type=MESH)
                cp.start()
                cps.append(cp)
        for cp in cps:
            cp.wait()

    return pl.pallas_call(
        body, name=name, in_specs=[ANY] * n, out_specs=[ANY] * n,
        out_shape=[jax.ShapeDtypeStruct((N_CHIPS,) + g.shape[2:], g.dtype) for g in grads],
        scratch_shapes=[pltpu.SemaphoreType.DMA((n * N_CHIPS,)), pltpu.SemaphoreType.DMA((n * N_CHIPS,))],
        compiler_params=pltpu.CompilerParams(has_side_effects=True),
    )(*grads)


def _scatter_start(name, parts):
    n = len(parts)

    def body(*refs):
        ins, lands = refs[:n], refs[n:2 * n]
        ssem, rsem = refs[2 * n:2 * n + 2]
        x, y, c = _coords()
        k_me = 2 * x + y
        for a in range(n):
            for m, (mx, my) in enumerate(CHIP_MASKS):
                px, py = _flip(x, mx), _flip(y, my)
                pltpu.make_async_remote_copy(src_ref=ins[a].at[2 * px + py], dst_ref=lands[a].at[k_me],
                                             send_sem=ssem.at[a * 3 + m], recv_sem=rsem.at[a * 3 + m],
                                             device_id=(px, py, c), device_id_type=MESH).start()

    sem = pltpu.SemaphoreType.DMA((3 * n,))
    out = pl.pallas_call(
        body, name=name, in_specs=[HBM_SPEC] * (2 * n),
        out_specs=tuple([SEM_SPEC, SEM_SPEC] + [HBM_SPEC] * (2 * n)),
        out_shape=tuple([sem, sem] + _hbm_like(parts) + _hbm_like(parts)),
        input_output_aliases={i: 2 + i for i in range(2 * n)},
        compiler_params=pltpu.CompilerParams(has_side_effects=DATAFLOW),
    )(*[_hbm(p) for p in parts], *[_hbm(lax.empty(p.shape, p.dtype)) for p in parts])
    return out[0], out[1], list(out[2:2 + n]), list(out[2 + n:])


def _scatter_wait(name, started, after):
    ssem, rsem, parts, lands = started
    n = len(parts)

    def body(*refs):
        ins, lnd = refs[:n], refs[n:2 * n]
        s_in, r_in, _ = refs[2 * n:2 * n + 3]
        x, y, c = _coords()
        k_me = 2 * x + y
        for a in range(n):
            for m, (mx, my) in enumerate(CHIP_MASKS):
                px, py = _flip(x, mx), _flip(y, my)
                cp = pltpu.make_async_remote_copy(src_ref=ins[a].at[2 * px + py], dst_ref=lnd[a].at[k_me],
                                                  send_sem=s_in.at[a * 3 + m], recv_sem=r_in.at[a * 3 + m],
                                                  device_id=(px, py, c), device_id_type=MESH)
                cp.wait_send()
                cp.wait_recv()

    out = pl.pallas_call(
        body, name=name, in_specs=[HBM_SPEC] * (2 * n) + [SEM_SPEC, SEM_SPEC, ANY],
        out_specs=tuple([HBM_SPEC] * (2 * n)), out_shape=tuple(_hbm_like(parts) + _hbm_like(lands)),
        input_output_aliases={i: i for i in range(2 * n)},
        compiler_params=pltpu.CompilerParams(has_side_effects=DATAFLOW),
    )(*parts, *lands, ssem, rsem, after)
    return list(out[:n]), list(out[n:])


def _pair_gather(name, bufs):
    n = len(bufs)

    def body(*refs):
        outs = refs[n:2 * n]
        ssem, rsem = refs[2 * n:]
        x, y, c = _coords()
        cps = []
        for a in range(n):
            mine = outs[a].at[c]
            cp = pltpu.make_async_remote_copy(src_ref=mine, dst_ref=mine, send_sem=ssem.at[a],
                                              recv_sem=rsem.at[a], device_id=(x, y, 1 - c), device_id_type=MESH)
            cp.start()
            cps.append(cp)
        for cp in cps:
            cp.wait()

    return pl.pallas_call(
        body, name=name, in_specs=[ANY] * n, out_specs=[ANY] * n,
        out_shape=[jax.ShapeDtypeStruct(b.shape, b.dtype) for b in bufs],
        input_output_aliases={a: a for a in range(n)},
        scratch_shapes=[pltpu.SemaphoreType.DMA((n,)), pltpu.SemaphoreType.DMA((n,))],
        compiler_params=pltpu.CompilerParams(has_side_effects=True),
    )(*bufs)


DEV_MASKS = tuple((mx, my, mc) for mx in (0, 1) for my in (0, 1) for mc in (0, 1) if (mx, my, mc) != (0, 0, 0))


def _gather_small(buf):
    def body(in_ref, out_ref, ssem, rsem, lsem):
        x, y, c = _coords()
        me = 4 * x + 2 * y + c
        cps = [pltpu.make_async_copy(in_ref, out_ref.at[me], lsem)]
        cps[0].start()
        for t, (mx, my, mc) in enumerate(DEV_MASKS):
            cp = pltpu.make_async_remote_copy(src_ref=in_ref, dst_ref=out_ref.at[me], send_sem=ssem.at[t],
                                              recv_sem=rsem.at[t], device_id=(_flip(x, mx), _flip(y, my), _flip(c, mc)),
                                              device_id_type=MESH)
            cp.start()
            cps.append(cp)
        for cp in cps:
            cp.wait()

    return pl.pallas_call(
        body, name="gather_small", in_specs=[ANY], out_specs=ANY,
        out_shape=jax.ShapeDtypeStruct((N_DEV,) + buf.shape, buf.dtype),
        scratch_shapes=[pltpu.SemaphoreType.DMA((N_DEV - 1,)), pltpu.SemaphoreType.DMA((N_DEV - 1,)),
                        pltpu.SemaphoreType.DMA(())],
        compiler_params=pltpu.CompilerParams(has_side_effects=True),
    )(buf)


def _row_tile(rows):
    return rows // 2 if rows % 16 == 0 else rows


def _pair_add(name, grad, got, c_idx):
    _, _, r2, cols = grad.shape
    tr = _row_tile(r2)

    def body(c_ref, a_ref, b_ref, o_ref):
        o_ref[...] = (a_ref[...] + b_ref[...]).astype(BF16)

    return pl.pallas_call(
        body, name=name,
        grid_spec=pltpu.PrefetchScalarGridSpec(
            num_scalar_prefetch=1, grid=(N_CHIPS, r2 // tr),
            in_specs=[pl.BlockSpec((None, None, tr, cols), lambda j, i, c: (j, c[0], i, 0)),
                      pl.BlockSpec((None, tr, cols), lambda j, i, c: (j, i, 0))],
            out_specs=pl.BlockSpec((None, tr, cols), lambda j, i, c: (j, i, 0))),
        out_shape=jax.ShapeDtypeStruct((N_CHIPS, r2, cols), BF16),
        compiler_params=_cp(("parallel", "parallel")))(c_idx, grad, got)


def _chip_add(name, part, got, kc_idx):
    _, r2, cols = got.shape
    tr = _row_tile(r2)

    def body(k_ref, p_ref, g1_ref, g2_ref, g3_ref, o_ref):
        acc = p_ref[...].astype(F32)
        for g_ref in (g1_ref, g2_ref, g3_ref):
            acc = acc + g_ref[...].astype(F32)
        o_ref[...] = acc

    def slot(d):
        return pl.BlockSpec((None, tr, cols), lambda i, k: ((k[0] + d) % N_CHIPS, i, 0))

    return pl.pallas_call(
        body, name=name,
        grid_spec=pltpu.PrefetchScalarGridSpec(
            num_scalar_prefetch=1, grid=(r2 // tr,),
            in_specs=[slot(0), slot(1), slot(2), slot(3)],
            out_specs=pl.BlockSpec((None, tr, cols), lambda i, k: (k[1], i, 0))),
        out_shape=jax.ShapeDtypeStruct((2, r2, cols), F32),
        compiler_params=_cp(("parallel",)))(kc_idx, part, got, got, got)


def _adam_math(w, g, m, v):
    m2 = ADAM_B1 * m + (1.0 - ADAM_B1) * g
    v2 = ADAM_B2 * v + (1.0 - ADAM_B2) * (g * g)
    m_hat = m2 / (1.0 - ADAM_B1 ** ADAM_STEP)
    v_hat = v2 / (1.0 - ADAM_B2 ** ADAM_STEP)
    delta = -ADAM_LR * (m_hat / (jnp.sqrt(v_hat) + ADAM_EPS) + ADAM_WD * w)
    return delta, m2, v2


def _adamw_matrix(name, w, g_layers, m, v):
    _, rows, cols = w.shape
    tr = rows // 16 if rows % 128 == 0 else rows // 8

    def body(w_ref, g0_ref, g1_ref, m_ref, v_ref, go_ref, d_ref, mo_ref, vo_ref):
        g = jnp.where(pl.program_id(0) == 0, g0_ref[...], g1_ref[...])
        go_ref[...] = g
        d_ref[...], mo_ref[...], vo_ref[...] = _adam_math(w_ref[...], g, m_ref[...], v_ref[...])

    lay = pl.BlockSpec((None, tr, cols), lambda l, i: (l, i, 0))
    flat = pl.BlockSpec((tr, cols), lambda l, i: (i, 0))
    shp = jax.ShapeDtypeStruct(w.shape, F32)
    return pl.pallas_call(body, name=name, grid=(DEPTH, rows // tr), in_specs=[lay, flat, flat, lay, lay],
                          out_specs=[lay, lay, lay, lay], out_shape=[shp, shp, shp, shp],
                          compiler_params=_cp(("parallel", "parallel")))(w, g_layers[0], g_layers[1], m, v)


def _sum_small(gathered):
    def body(g_ref, o_ref):
        acc = g_ref[0]
        for d in range(1, N_DEV):
            acc = acc + g_ref[d]
        o_ref[...] = acc

    return pl.pallas_call(body, name="sum_small", out_shape=jax.ShapeDtypeStruct(gathered.shape[1:], F32),
                          compiler_params=_cp())(gathered)


def _adamw_small(w, g, m, v):
    def body(w_ref, g_ref, m_ref, v_ref, d_ref, mo_ref, vo_ref):
        d_ref[...], mo_ref[...], vo_ref[...] = _adam_math(w_ref[...], g_ref[...], m_ref[...], v_ref[...])

    shp = jax.ShapeDtypeStruct(w.shape, F32)
    return pl.pallas_call(body, name="adamw_small", out_shape=[shp, shp, shp], compiler_params=_cp())(w, g, m, v)


def _pack(arrays, rows):
    flat = jnp.concatenate([a.reshape(-1) for a in arrays])
    return jnp.pad(flat, (0, rows * BLK - flat.shape[0])).reshape(rows, BLK)


def _unpack(buf, shapes):
    flat = buf.reshape(-1)
    out, pos = [], 0
    for s in shapes:
        n = math.prod(s)
        out.append(flat[pos:pos + n].reshape(s))
        pos += n
    return out


def _rows_for(shapes):
    n = sum(math.prod(s) for s in shapes)
    return -(-n // (8 * BLK)) * 8


def _rs_begin(tag, grads, c_idx):
    split = [g.reshape(N_CHIPS, 2, g.shape[1] // 2, g.shape[2]) for g in grads]
    got = _pair_swap(f"rs_pair_swap{tag}", split)
    parts = [_pair_add(f"rs_pair_add{tag}_{i}", s, r, c_idx) for i, (s, r) in enumerate(zip(split, got))]
    return _scatter_start(f"rs_scatter_start{tag}", parts)


def _rs_end(tag, started, after, kc_idx):
    parts, lands = _scatter_wait(f"rs_scatter_wait{tag}", started, after)
    halves = [_chip_add(f"rs_chip_add{tag}_{i}", p, r, kc_idx) for i, (p, r) in enumerate(zip(parts, lands))]
    full = _pair_gather(f"rs_pair_gather{tag}", halves)
    return [f.reshape(2 * f.shape[1], f.shape[2]) for f in full]


def kernel(x, w_in, lb_logits, a_norm_w, c_sinks, w_out, ln1_g, ln1_b, w_gate, w_up, conv_w, conv_b, w_down, ln2_g, ln2_b, loss_target, m_w_in, m_lb_logits, m_a_norm_w, m_c_sinks, m_w_out, m_ln1_g, m_ln1_b, m_w_gate, m_w_up, m_conv_w, m_conv_b, m_w_down, m_ln2_g, m_ln2_b, v_w_in, v_lb_logits, v_a_norm_w, v_c_sinks, v_w_out, v_ln1_g, v_ln1_b, v_w_gate, v_w_up, v_conv_w, v_conv_b, v_w_down, v_ln2_g, v_ln2_b):
    cx, cy, cc = _coords()
    c_idx = jnp.reshape(cc, (1,)).astype(jnp.int32)
    k_me = 2 * cx + cy
    k_idx = jnp.reshape(k_me, (1,)).astype(jnp.int32)
    kc_idx = jnp.stack([k_me, cc]).astype(jnp.int32)

    def slot(nm, w, l):
        b = _into_slot(f"slot_{nm}{l}", w, l, k_idx, BF16)
        return b.reshape(N_CHIPS, 2, b.shape[1] // 2, b.shape[2])

    sl = [{nm: slot(nm, w, l) for nm, w in (("wi", w_in), ("wo", w_out), ("wg", w_gate), ("wu", w_up), ("wd", w_down))}
          for l in range(DEPTH)]
    cw_slot = _into_slot("slot_cw", conv_w.reshape(1, DEPTH * CONV_WIDTH, FF_SHARD), 0, k_idx, F32)
    cw_slot = cw_slot.reshape(N_CHIPS, DEPTH, CONV_WIDTH, FF_SHARD)
    stages, token = _gather_start("gather_start", [
        [sl[0]["wi"], cw_slot],
        [sl[0]["wo"], sl[0]["wg"], sl[0]["wu"], sl[0]["wd"]],
        [sl[1]["wi"], sl[1]["wo"]],
        [sl[1]["wg"], sl[1]["wu"], sl[1]["wd"]]])

    def mat(b):
        return b.reshape(N_CHIPS, 2 * b.shape[2], b.shape[3])

    fwd0, token = _gather_forward("gather_fwd0", stages[0], token)
    wi0, cw_all = _gather_wait("gather_wait0", fwd0, token)
    cw_full = jnp.transpose(cw_all, (1, 2, 0, 3)).reshape(DEPTH, CONV_WIDTH, D_FF)
    tables = _rope_tables()
    weights = [dict(wi=mat(wi0)), dict()]

    h = x[0]
    saved = []
    for l in range(DEPTH):
        w = weights[l]
        proj = _fwd_colsharded(f"proj{l}", h, w["wi"])
        o_a, raw = _hgrn_fwd(f"hgrn_fwd{l}", proj, lb_logits, a_norm_w[l], l)
        o_b, lse_b = _attn_fwd(f"dilated_fwd{l}", proj, tables, None, n_heads=B_HEADS, rep=1, q0=QB0, k0=KB0, v0=VB0,
                               patterns=B_PATTERNS)
        st = stages[1] if l == 0 else stages[3]
        fwd, _ = _gather_forward(f"gather_fwd{2 * l + 1}", st, o_b)
        sink_b = jnp.broadcast_to(c_sinks[l][:, None, None], (C_HEADS, 8, BLK))
        o_c, lse_c = _attn_fwd(f"window_fwd{l}", proj, tables, sink_b, n_heads=C_HEADS, rep=C_HEADS // C_KV_HEADS,
                               q0=QC0, k0=KC0, v0=VC0, patterns=C_PATTERNS)
        got = _gather_wait(f"gather_wait{2 * l + 1}", fwd, o_c)
        if l == 0:
            w["wo"], w["wg"], w["wu"], w["wd"] = (mat(b) for b in got)
        else:
            w["wg"], w["wu"], w["wd"] = (mat(b) for b in got)
        mixed = jnp.concatenate([o_a, o_b, o_c], axis=1)
        y1 = _fwd_rowsharded(f"wout{l}", mixed, w["wo"], OUT_SHARD)
        x1 = _ln_fwd(f"ln1_fwd{l}", h, y1, ln1_g[l], ln1_b[l])
        g = _fwd_colsharded(f"gate{l}", x1, w["wg"])
        u = _fwd_colsharded(f"up{l}", x1, w["wu"])
        if l == 0:
            fwd, _ = _gather_forward("gather_fwd2", stages[2], u)
        hh = _conv_gate_fwd(f"conv_fwd{l}", g, u, cw_full[l], conv_b[l])
        y2 = _fwd_rowsharded(f"down{l}", hh, w["wd"], FF_SHARD)
        x2 = _ln_fwd(f"ln2_fwd{l}", x1, y2, ln2_g[l], ln2_b[l])
        if l == 0:
            weights[1]["wi"], weights[1]["wo"] = (mat(b) for b in _gather_wait("gather_wait2", fwd, x2))
        saved.append((h, proj, raw, lse_b, sink_b, lse_c, mixed, y1, x1, g, u, hh, y2))
        h = x2

    dy, loss_part = _loss_head(h, loss_target[0])

    d_res, d_path = None, dy
    small = [None] * DEPTH
    mat_grads = [None] * DEPTH
    pending = []
    for l in reversed(range(DEPTH)):
        h_in, proj, raw, lse_b, sink_b, lse_c, mixed, y1, x1, g, u, hh, y2 = saved[l]
        wi, wo, wg, wu, wd = (weights[l][k] for k in ("wi", "wo", "wg", "wu", "wd"))
        dz2, d_ln2g, d_ln2b = _ln_bwd(f"ln2_bwd{l}", x1, y2, ln2_g[l], d_res, d_path)
        dhh = _bwd_act_rowsharded(f"down_dx{l}", dz2, wd, FF_SHARD)
        d_wd = _bwd_w_rowsharded(f"down_dw{l}", hh, dz2, FF_SHARD)
        dg, du, d_cw, d_cb = _conv_gate_bwd(f"conv_bwd{l}", g, u, cw_full[l], conv_b[l], dhh)
        dx1 = _bwd_act_colsharded(f"gateup_dx{l}", [(dg, wg), (du, wu)])
        d_wg = _bwd_w_colsharded(f"gate_dw{l}", x1, dg)
        d_wu = _bwd_w_colsharded(f"up_dw{l}", x1, du)
        if pending:
            (s_ffn, s_mix), pending = pending, []
            g_wg, g_wu, g_wd = _rs_end(f"{l + 1}f", s_ffn, d_wu, kc_idx)
            g_wi, g_wo = _rs_end(f"{l + 1}m", s_mix, d_wu, kc_idx)
            mat_grads[l + 1] = [g_wi, g_wo, g_wg, g_wu, g_wd]
        s_ffn = _rs_begin(f"{l}f", [d_wg, d_wu, d_wd], c_idx)
        dz1, d_ln1g, d_ln1b = _ln_bwd(f"ln1_bwd{l}", h_in, y1, ln1_g[l], dz2, dx1)
        dmix = _bwd_act_rowsharded(f"wout_dx{l}", dz1, wo, OUT_SHARD)
        d_wo = _bwd_w_rowsharded(f"wout_dw{l}", mixed, dz1, OUT_SHARD)
        dq_a, df_a, di_a, dg_a, d_nw, d_lb = _hgrn_bwd(f"hgrn_bwd{l}", proj, raw, dmix, lb_logits, a_norm_w[l], l)
        dq_b, dk_b, dv_b, _ = _attn_bwd(f"dilated_bwd{l}", proj, mixed, dmix, lse_b, tables, None, n_kv=B_HEADS, rep=1,
                                        q0=QB0, k0=KB0, v0=VB0, m0=A_HEADS, patterns=B_PATTERNS)
        dq_c, dk_c, dv_c, d_sink = _attn_bwd(f"window_bwd{l}", proj, mixed, dmix, lse_c, tables, sink_b, n_kv=C_KV_HEADS,
                                             rep=C_HEADS // C_KV_HEADS, q0=QC0, k0=KC0, v0=VC0, m0=A_HEADS + B_HEADS,
                                             patterns=C_PATTERNS)
        dproj = jnp.concatenate([dq_a, df_a, di_a, dg_a, dq_b, dk_b, dv_b, dq_c, dk_c, dv_c], axis=1)
        dxp = _bwd_act_colsharded(f"proj_dx{l}", [(dproj, wi)])
        d_wi = _bwd_w_colsharded(f"proj_dw{l}", h_in, dproj)
        d_res, d_path = dz1, dxp
        pending = [s_ffn, _rs_begin(f"{l}m", [d_wi, d_wo], c_idx)]
        small[l] = (d_lb, d_nw.reshape(A_HEADS, 8, BLK)[:, 0].sum(0), d_sink[:, 0, 0], d_ln1g[0], d_ln1b[0],
                    d_cw, d_cb[0], d_ln2g[0], d_ln2b[0])
    grad_x2 = _axpy("grad_x", d_res, d_path)
    grad_x = grad_x2[None]

    g_lb = small[0][0] + small[1][0]
    per_layer = [jnp.stack([small[0][i], small[1][i]]) for i in range(1, 9)]
    small_shapes = [(DEPTH, 4 * BLK), (DEPTH, BLK), (DEPTH, C_HEADS), (DEPTH, D_MODEL), (DEPTH, D_MODEL),
                    (DEPTH, CONV_WIDTH, D_FF), (DEPTH, D_FF), (DEPTH, D_MODEL), (DEPTH, D_MODEL), (BLK,)]
    rows = _rows_for(small_shapes)
    total = _sum_small(_gather_small(_pack([g_lb] + per_layer + [loss_part[0]], rows)))
    g_lb, g_nw, g_sink, g_ln1g, g_ln1b, g_cw_full, g_cb, g_ln2g, g_ln2b, loss_row = _unpack(total, small_shapes)
    loss = loss_row[0]
    g_cw = lax.dynamic_slice_in_dim(g_cw_full, k_me * FF_SHARD, FF_SHARD, axis=2)

    sw = [lb_logits, a_norm_w, c_sinks, ln1_g, ln1_b, conv_w, conv_b, ln2_g, ln2_b]
    sg = [g_lb, g_nw, g_sink, g_ln1g, g_ln1b, g_cw, g_cb, g_ln2g, g_ln2b]
    sm = [m_lb_logits, m_a_norm_w, m_c_sinks, m_ln1_g, m_ln1_b, m_conv_w, m_conv_b, m_ln2_g, m_ln2_b]
    sv = [v_lb_logits, v_a_norm_w, v_c_sinks, v_ln1_g, v_ln1_b, v_conv_w, v_conv_b, v_ln2_g, v_ln2_b]
    shapes = [a.shape for a in sw]
    prow = _rows_for(shapes)
    sd, snm, snv = (_unpack(b, shapes) for b in _adamw_small(_pack(sw, prow), _pack(sg, prow), _pack(sm, prow), _pack(sv, prow)))

    s_ffn, s_mix = pending
    g_wg, g_wu, g_wd = _rs_end("0f", s_ffn, grad_x2, kc_idx)
    g_wi, g_wo = _rs_end("0m", s_mix, sd[0], kc_idx)
    mat_grads[0] = [g_wi, g_wo, g_wg, g_wu, g_wd]
    names = ["w_in", "w_out", "w_gate", "w_up", "w_down"]
    mw = [w_in, w_out, w_gate, w_up, w_down]
    mm = [m_w_in, m_w_out, m_w_gate, m_w_up, m_w_down]
    mv = [v_w_in, v_w_out, v_w_gate, v_w_up, v_w_down]
    mg, md, mnm, mnv = [], [], [], []
    for i in range(5):
        go, d, nm, nv = _adamw_matrix(f"adamw_{names[i]}", mw[i], [mat_grads[0][i], mat_grads[1][i]], mm[i], mv[i])
        mg.append(go), md.append(d), mnm.append(nm), mnv.append(nv)

    def ordered(mat, sm_):
        return [mat[0], sm_[0], sm_[1], sm_[2], mat[1], sm_[3], sm_[4], mat[2], mat[3], sm_[5], sm_[6], mat[4], sm_[7], sm_[8]]

    return (loss, grad_x, *ordered(mg, sg), *ordered(md, sd), *ordered(mnm, snm), *ordered(mnv, snv))
```

```python
import functools
import math

import jax
import jax.numpy as jnp
from jax import lax
from jax.experimental import pallas as pl
from jax.experimental.pallas import tpu as pltpu

F32 = jnp.float32
BF16 = jnp.bfloat16

D_MODEL = 2048
SEQ = 2048
DEPTH = 2
HEAD_DIM = 128
A_HEADS = 4
B_HEADS = 6
C_HEADS = 6
C_KV_HEADS = 2
A_CHUNK = 16
DILATED_PATTERNS = ((128, 1), (512, 4), (2048, 16))
C_WINDOW = 128
ROPE_THETA = 500000.0
ROPE_DIM = HEAD_DIM // 4
D_FF = 5632
CONV_WIDTH = 3
LN_EPS = 1e-5
ALPHA = (2 * DEPTH) ** 0.25
IN_WIDTH = 5632
MIX_WIDTH = 2048
ADAM_LR = 0.001
ADAM_B1 = 0.9
ADAM_B2 = 0.999
ADAM_EPS = 1e-08
ADAM_WD = 0.01
ADAM_STEP = 10

N_CHIPS = 4
N_DEV = 8
FF_SHARD = D_FF // N_CHIPS
OUT_SHARD = MIX_WIDTH // N_CHIPS
BLK = 128
N_CHUNK = SEQ // A_CHUNK
SLAB = 32

QA0, FA0, IA0, GA0 = 0, 4, 8, 12
QB0, KB0, VB0 = 16, 22, 28
QC0, KC0, VC0 = 34, 40, 42

VMEM_LIMIT_V7X = 56 * 1024 * 1024
HI = lax.Precision.HIGHEST
MESH = pl.DeviceIdType.MESH


def _cp(sem=None, vmem=VMEM_LIMIT_V7X, **kw):
    return pltpu.CompilerParams(dimension_semantics=sem, vmem_limit_bytes=vmem, **kw)


def _sigmoid(x):
    return 1.0 / (1.0 + jnp.exp(-x))


def _mm(name, pairs, dims, grid, a_specs, b_specs, out_spec, out_shape, nk=1, acc_shape=None):
    n_pairs = len(pairs)

    def body(*refs):
        o_ref = refs[2 * n_pairs]
        part = None
        for p in range(n_pairs):
            a = refs[2 * p][...].astype(BF16)
            b = refs[2 * p + 1][...].astype(BF16)
            t = lax.dot_general(a, b, dims, preferred_element_type=F32)
            part = t if part is None else part + t
        if nk == 1:
            o_ref[...] = part.astype(o_ref.dtype)
        else:
            acc = refs[2 * n_pairs + 1]
            k = pl.program_id(len(grid) - 1)

            @pl.when(k == 0)
            def _():
                acc[...] = part

            @pl.when(k > 0)
            def _():
                acc[...] += part

            @pl.when(k == nk - 1)
            def _():
                o_ref[...] = acc[...].astype(o_ref.dtype)

    in_specs, args = [], []
    for (a, b), sa, sb in zip(pairs, a_specs, b_specs):
        in_specs += [sa, sb]
        args += [a, b]
    sem = ("parallel",) * (len(grid) - (1 if nk > 1 else 0)) + (("arbitrary",) if nk > 1 else ())
    return pl.pallas_call(
        body, name=name, grid=grid, in_specs=in_specs, out_specs=out_spec, out_shape=out_shape,
        scratch_shapes=[pltpu.VMEM(acc_shape, F32)] if nk > 1 else [],
        compiler_params=_cp(sem),
    )(*args)


NN = (((1,), (0,)), ((), ()))
NT = (((1,), (1,)), ((), ()))
TN = (((0,), (0,)), ((), ()))
TM = 1024


def _fwd_colsharded(name, x, w_stk):
    return _mm(name, [(x, w_stk)], NN, (N_CHIPS, SEQ // TM),
               [pl.BlockSpec((TM, D_MODEL), lambda j, i: (i, 0))],
               [pl.BlockSpec((None, D_MODEL, FF_SHARD), lambda j, i: (j, 0, 0))],
               pl.BlockSpec((TM, FF_SHARD), lambda j, i: (i, j)),
               jax.ShapeDtypeStruct((SEQ, D_FF), F32))


def _fwd_rowsharded(name, a, w_stk, shard):
    tn = 1024
    return _mm(name, [(a, w_stk)], NN, (SEQ // TM, D_MODEL // tn, N_CHIPS),
               [pl.BlockSpec((TM, shard), lambda i, j, k: (i, k))],
               [pl.BlockSpec((None, shard, tn), lambda i, j, k: (k, 0, j))],
               pl.BlockSpec((TM, tn), lambda i, j, k: (i, j)),
               jax.ShapeDtypeStruct((SEQ, D_MODEL), F32), nk=N_CHIPS, acc_shape=(TM, tn))


def _bwd_act_colsharded(name, pairs):
    tn = 1024
    n = len(pairs)
    return _mm(name, pairs, NT, (SEQ // TM, D_MODEL // tn, N_CHIPS),
               [pl.BlockSpec((TM, FF_SHARD), lambda i, j, k: (i, k))] * n,
               [pl.BlockSpec((None, tn, FF_SHARD), lambda i, j, k: (k, j, 0))] * n,
               pl.BlockSpec((TM, tn), lambda i, j, k: (i, j)),
               jax.ShapeDtypeStruct((SEQ, D_MODEL), F32), nk=N_CHIPS, acc_shape=(TM, tn))


def _bwd_act_rowsharded(name, dy, w_stk, shard):
    return _mm(name, [(dy, w_stk)], NT, (N_CHIPS, SEQ // TM),
               [pl.BlockSpec((TM, D_MODEL), lambda j, i: (i, 0))],
               [pl.BlockSpec((None, shard, D_MODEL), lambda j, i: (j, 0, 0))],
               pl.BlockSpec((TM, shard), lambda j, i: (i, j)),
               jax.ShapeDtypeStruct((SEQ, N_CHIPS * shard), F32))


def _bwd_w_colsharded(name, x, dy):
    tm = 512
    return _mm(name, [(x, dy)], TN, (N_CHIPS, D_MODEL // tm),
               [pl.BlockSpec((SEQ, tm), lambda j, i: (0, i))],
               [pl.BlockSpec((SEQ, FF_SHARD), lambda j, i: (0, j))],
               pl.BlockSpec((None, tm, FF_SHARD), lambda j, i: (j, i, 0)),
               jax.ShapeDtypeStruct((N_CHIPS, D_MODEL, FF_SHARD), F32))


def _bwd_w_rowsharded(name, a, dy, shard):
    tn = 1024
    return _mm(name, [(a, dy)], TN, (N_CHIPS, D_MODEL // tn),
               [pl.BlockSpec((SEQ, shard), lambda j, i: (0, j))],
               [pl.BlockSpec((SEQ, tn), lambda j, i: (0, i))],
               pl.BlockSpec((None, shard, tn), lambda j, i: (j, 0, i)),
               jax.ShapeDtypeStruct((N_CHIPS, shard, D_MODEL), F32))


TR = 256


def _ln_fwd(name, x, y, g, b):
    def body(x_ref, y_ref, g_ref, b_ref, o_ref):
        z = ALPHA * x_ref[...] + y_ref[...]
        mu = jnp.mean(z, -1, keepdims=True)
        zc = z - mu
        var = jnp.mean(zc * zc, -1, keepdims=True)
        o_ref[...] = zc * lax.rsqrt(var + LN_EPS) * g_ref[...] + b_ref[...]

    row = pl.BlockSpec((TR, D_MODEL), lambda i: (i, 0))
    vec = pl.BlockSpec((1, D_MODEL), lambda i: (0, 0))
    return pl.pallas_call(body, name=name, grid=(SEQ // TR,), in_specs=[row, row, vec, vec], out_specs=row,
                          out_shape=jax.ShapeDtypeStruct((SEQ, D_MODEL), F32),
                          compiler_params=_cp(("parallel",)))(x, y, g.reshape(1, -1), b.reshape(1, -1))


def _ln_bwd(name, x, y, g, d_res, d_path, run_after=None):
    has_res = d_res is not None
    n_in = 4 + has_res + (run_after is not None)

    def body(*refs):
        dz_ref, dg_ref, db_ref = refs[n_in:]
        if has_res:
            x_ref, y_ref, g_ref, r_ref, p_ref = refs[:5]
            dout = ALPHA * r_ref[...] + p_ref[...]
        else:
            x_ref, y_ref, g_ref, p_ref = refs[:4]
            dout = p_ref[...]
        z = ALPHA * x_ref[...] + y_ref[...]
        mu = jnp.mean(z, -1, keepdims=True)
        zc = z - mu
        rstd = lax.rsqrt(jnp.mean(zc * zc, -1, keepdims=True) + LN_EPS)
        zh = zc * rstd
        dzh = dout * g_ref[...]
        dz_ref[...] = rstd * (dzh - jnp.mean(dzh, -1, keepdims=True) - zh * jnp.mean(dzh * zh, -1, keepdims=True))
        pg = jnp.sum(dout * zh, 0, keepdims=True)
        pb = jnp.sum(dout, 0, keepdims=True)

        @pl.when(pl.program_id(0) == 0)
        def _():
            dg_ref[...] = pg
            db_ref[...] = pb

        @pl.when(pl.program_id(0) > 0)
        def _():
            dg_ref[...] += pg
            db_ref[...] += pb

    row = pl.BlockSpec((TR, D_MODEL), lambda i: (i, 0))
    vec = pl.BlockSpec((1, D_MODEL), lambda i: (0, 0))
    args = [x, y, g.reshape(1, -1)] + ([d_res] if has_res else []) + [d_path]
    in_specs = [row, row, vec] + ([row] if has_res else []) + [row]
    if run_after is not None:
        args.append(run_after)
        in_specs.append(pl.BlockSpec(memory_space=pl.ANY))
    vshape = jax.ShapeDtypeStruct((1, D_MODEL), F32)
    return pl.pallas_call(body, name=name, grid=(SEQ // TR,), in_specs=in_specs, out_specs=[row, vec, vec],
                          out_shape=[jax.ShapeDtypeStruct((SEQ, D_MODEL), F32), vshape, vshape],
                          compiler_params=_cp(("arbitrary",)))(*args)


def _loss_head(y, target):
    def body(y_ref, t_ref, dy_ref, l_ref):
        e = y_ref[...] - t_ref[...]
        dy_ref[...] = e * (1.0 / D_MODEL)
        part = jnp.full((8, BLK), 0.5 / D_MODEL * jnp.sum(e * e), F32)

        @pl.when(pl.program_id(0) == 0)
        def _():
            l_ref[...] = part

        @pl.when(pl.program_id(0) > 0)
        def _():
            l_ref[...] += part

    row = pl.BlockSpec((TR, D_MODEL), lambda i: (i, 0))
    return pl.pallas_call(body, name="loss_head", grid=(SEQ // TR,), in_specs=[row, row],
                          out_specs=[row, pl.BlockSpec((8, BLK), lambda i: (0, 0))],
                          out_shape=[jax.ShapeDtypeStruct((SEQ, D_MODEL), F32), jax.ShapeDtypeStruct((8, BLK), F32)],
                          compiler_params=_cp(("arbitrary",)))(y, target)


def _axpy(name, a, b):
    def body(a_ref, b_ref, o_ref):
        o_ref[...] = ALPHA * a_ref[...] + b_ref[...]

    row = pl.BlockSpec((TR, D_MODEL), lambda i: (i, 0))
    return pl.pallas_call(body, name=name, grid=(SEQ // TR,), in_specs=[row, row], out_specs=row,
                          out_shape=jax.ShapeDtypeStruct((SEQ, D_MODEL), F32),
                          compiler_params=_cp(("parallel",)))(a, b)


TC = 512


def _shift_down(x, s, rows):
    if s == 0:
        return x
    return jnp.where(rows >= s, pltpu.roll(x, s, axis=0), 0.0)


def _shift_up(x, s, rows):
    if s == 0:
        return x
    return jnp.where(rows < SEQ - s, pltpu.roll(x, SEQ - s, axis=0), 0.0)


def _conv_gate_fwd(name, g, u, cw, cb):
    def body(g_ref, u_ref, w_ref, b_ref, h_ref):
        gg = g_ref[...]
        rows = lax.broadcasted_iota(jnp.int32, gg.shape, 0)
        gc = b_ref[...] + w_ref[2:3, :] * gg
        gc = gc + w_ref[1:2, :] * _shift_down(gg, 1, rows)
        gc = gc + w_ref[0:1, :] * _shift_down(gg, 2, rows)
        h_ref[...] = (gc * _sigmoid(gc) * u_ref[...]).astype(BF16)

    col = pl.BlockSpec((SEQ, TC), lambda j: (0, j))
    return pl.pallas_call(body, name=name, grid=(D_FF // TC,),
                          in_specs=[col, col, pl.BlockSpec((CONV_WIDTH, TC), lambda j: (0, j)),
                                    pl.BlockSpec((1, TC), lambda j: (0, j))],
                          out_specs=col, out_shape=jax.ShapeDtypeStruct((SEQ, D_FF), BF16),
                          compiler_params=_cp(("parallel",)))(g, u, cw, cb.reshape(1, -1))


def _conv_gate_bwd(name, g, u, cw, cb, dh):
    def body(g_ref, u_ref, w_ref, b_ref, dh_ref, dg_ref, du_ref, dw_ref, db_ref):
        gg = g_ref[...]
        rows = lax.broadcasted_iota(jnp.int32, gg.shape, 0)
        g1 = _shift_down(gg, 1, rows)
        g2 = _shift_down(gg, 2, rows)
        gc = b_ref[...] + w_ref[2:3, :] * gg + w_ref[1:2, :] * g1 + w_ref[0:1, :] * g2
        sg = _sigmoid(gc)
        act = gc * sg
        dh = dh_ref[...]
        du_ref[...] = dh * act
        dgc = dh * u_ref[...] * (sg * (1.0 + gc * (1.0 - sg)))
        db_ref[...] = jnp.sum(dgc, 0, keepdims=True)
        dw_ref[2:3, :] = jnp.sum(dgc * gg, 0, keepdims=True)
        dw_ref[1:2, :] = jnp.sum(dgc * g1, 0, keepdims=True)
        dw_ref[0:1, :] = jnp.sum(dgc * g2, 0, keepdims=True)
        dg_ref[...] = (w_ref[2:3, :] * dgc + w_ref[1:2, :] * _shift_up(dgc, 1, rows)
                       + w_ref[0:1, :] * _shift_up(dgc, 2, rows))

    col = pl.BlockSpec((SEQ, TC), lambda j: (0, j))
    w3 = pl.BlockSpec((CONV_WIDTH, TC), lambda j: (0, j))
    w1 = pl.BlockSpec((1, TC), lambda j: (0, j))
    big = jax.ShapeDtypeStruct((SEQ, D_FF), F32)
    return pl.pallas_call(body, name=name, grid=(D_FF // TC,), in_specs=[col, col, w3, w1, col],
                          out_specs=[col, col, w3, w1],
                          out_shape=[big, big, jax.ShapeDtypeStruct((CONV_WIDTH, D_FF), F32),
                                     jax.ShapeDtypeStruct((1, D_FF), F32)],
                          compiler_params=_cp(("parallel",)))(g, u, cw, cb.reshape(1, -1), dh)


def _lbs_of(logits, layer):
    m = jnp.max(logits, 0, keepdims=True)
    e = jnp.exp(logits - m)
    p = e / jnp.sum(e, 0, keepdims=True)
    lb = jnp.zeros((1, BLK), F32)
    for r in range(1, layer + 1):
        lb = lb + p[r:r + 1, :]
    return lb, p


def _dlogits_of(p, dlb, layer):
    rows = lax.broadcasted_iota(jnp.int32, p.shape, 0)
    dp = jnp.where((rows >= 1) & (rows <= layer), dlb, 0.0)
    return p * (dp - jnp.sum(p * dp, 0, keepdims=True))


SROWS = SLAB * A_CHUNK
N_SLAB = N_CHUNK // SLAB


def _chunk_prefix(x, rowi):
    for s in (1, 2, 4, 8):
        x = x + jnp.where(rowi >= s, pltpu.roll(x, s, axis=0), 0.0)
    return x


def _chunk_suffix(x, rowi):
    for s in (1, 2, 4, 8):
        x = x + jnp.where(rowi < A_CHUNK - s, pltpu.roll(x, SROWS - s, axis=0), 0.0)
    return x


def _c3(x):
    return x.reshape(SLAB, A_CHUNK, BLK)


def _c2(x):
    return x.reshape(SROWS, BLK)


def _lane_sum_b(x2, ones):
    return jnp.dot(x2, ones, preferred_element_type=F32, precision=HI)


def _bmm(eq, a, b):
    return jnp.einsum(eq, a, b, preferred_element_type=F32, precision=HI)


def _slab_rows(s):
    return pl.ds(s * SROWS, SROWS)


def _hgrn_prep(q, f, lb):
    rowi = lax.broadcasted_iota(jnp.int32, (SROWS, BLK), 0) & (A_CHUNK - 1)
    sq = _sigmoid(q)
    qc = q * sq
    sf = _sigmoid(f)
    fg = lb + (1.0 - lb) * sf
    kc = 1.0 - fg
    b = _chunk_prefix(jnp.log(fg), rowi)
    b3 = _c3(b)
    blast = b3[:, A_CHUNK - 1:A_CHUNK, :]
    eb = jnp.exp(b)
    ekb = _c2(jnp.exp(blast - b3))
    dec = jnp.exp(blast.reshape(SLAB, BLK))
    return rowi, sq, qc, sf, fg, kc, b, eb, ekb, dec


def _hgrn_slab_states(s, carry, v, ke, dec, dec_ref, u_ref, st_ref):
    dec_ref[pl.ds(s * SLAB, SLAB), :] = dec
    u_ref[...] = _bmm('ncv,nck->nvk', _c3(v), _c3(ke))

    def step(j, c):
        st_ref[j] = c
        return dec_ref[pl.ds(s * SLAB + j, 1), :] * c + u_ref[j]

    return lax.fori_loop(0, SLAB, step, carry)


def _hgrn_fwd(name, proj, lb_logits, nw, layer):
    def body(q_ref, f_ref, i_ref, g_ref, lg_ref, nw_ref, out_ref, raw_ref, dec_ref, u_ref, st_ref):
        lb, _ = _lbs_of(lg_ref[...], layer)
        ones = jnp.ones((BLK, BLK), F32)
        carry = jnp.zeros((BLK, BLK), F32)
        for s in range(N_SLAB):
            rows = _slab_rows(s)
            v = i_ref[rows, :]
            rowi, sq, qc, sf, fg, kc, b, eb, ekb, dec = _hgrn_prep(q_ref[rows, :], f_ref[rows, :], lb)
            carry = _hgrn_slab_states(s, carry, v, kc * ekb, dec, dec_ref, u_ref, st_ref)
            o = _c2(_bmm('nck,nvk->ncv', _c3(qc * eb), st_ref[...]))
            qc3, kc3, b3, v3, row3 = _c3(qc), _c3(kc), _c3(b), _c3(v), _c3(rowi)
            for j in range(A_CHUNK):
                dj = jnp.exp(jnp.where(row3 >= j, b3 - b3[:, j:j + 1, :], -jnp.inf))
                a = _lane_sum_b(_c2(qc3 * dj * kc3[:, j:j + 1, :]), ones)
                o = o + a * _c2(jnp.broadcast_to(v3[:, j:j + 1, :], v3.shape))
            raw_ref[rows, :] = o
            r = lax.rsqrt(jnp.mean(o * o, -1, keepdims=True) + LN_EPS)
            gg = g_ref[rows, :]
            out_ref[rows, :] = o * r * nw_ref[...] * (gg * _sigmoid(gg))

    def colblk(c0):
        return pl.BlockSpec((SEQ, BLK), lambda h: (0, c0 + h))

    big = jax.ShapeDtypeStruct((SEQ, A_HEADS * BLK), F32)
    return pl.pallas_call(
        body, name=name, grid=(A_HEADS,),
        in_specs=[colblk(QA0), colblk(FA0), colblk(IA0), colblk(GA0),
                  pl.BlockSpec((DEPTH, BLK), lambda h: (0, h)), pl.BlockSpec((1, BLK), lambda h: (0, 0))],
        out_specs=[colblk(0), colblk(0)], out_shape=[big, big],
        scratch_shapes=[pltpu.VMEM((N_CHUNK, BLK), F32), pltpu.VMEM((SLAB, BLK, BLK), F32),
                        pltpu.VMEM((SLAB, BLK, BLK), F32)],
        compiler_params=_cp(("parallel",)))(proj, proj, proj, proj, lb_logits, nw.reshape(1, -1))


def _hgrn_bwd(name, proj, raw, dmix, lb_logits, nw, layer):
    def body(q_ref, f_ref, i_ref, g_ref, raw_ref, do_ref, lg_ref, nw_ref,
             dq_ref, df_ref, di_ref, dg_ref, dnw_ref, dlg_ref,
             dec_ref, u_ref, st_ref, h_ref, dbs_ref, dkc_ref, tot_ref):
        lb, p = _lbs_of(lg_ref[...], layer)
        ones = jnp.ones((BLK, BLK), F32)
        nwv = nw_ref[...]

        carry = jnp.zeros((BLK, BLK), F32)
        for s in range(N_SLAB):
            rows = _slab_rows(s)
            rowi, sq, qc, sf, fg, kc, b, eb, ekb, dec = _hgrn_prep(q_ref[rows, :], f_ref[rows, :], lb)
            carry = _hgrn_slab_states(s, carry, i_ref[rows, :], kc * ekb, dec, dec_ref, u_ref,
                                      st_ref.at[pl.ds(s * SLAB, SLAB)])

        carry = jnp.zeros((BLK, BLK), F32)
        dnw = jnp.zeros((1, BLK), F32)
        for s in reversed(range(N_SLAB)):
            rows = _slab_rows(s)
            q, v = q_ref[rows, :], i_ref[rows, :]
            rowi, sq, qc, sf, fg, kc, b, eb, ekb, dec = _hgrn_prep(q, f_ref[rows, :], lb)
            ke = kc * ekb
            qe = qc * eb

            o = raw_ref[rows, :]
            gg = g_ref[rows, :]
            sgg = _sigmoid(gg)
            dout = do_ref[rows, :]
            r = lax.rsqrt(jnp.mean(o * o, -1, keepdims=True) + LN_EPS)
            oh = o * r
            dg_ref[rows, :] = dout * oh * nwv * (sgg * (1.0 + gg * (1.0 - sgg)))
            dn = dout * (gg * sgg)
            dnw = dnw + jnp.sum(dn * oh, 0, keepdims=True)
            doh = dn * nwv
            do = r * (doh - oh * jnp.mean(doh * oh, -1, keepdims=True))
            do3, qe3, v3, ke3 = _c3(do), _c3(qe), _c3(v), _c3(ke)

            u_ref[...] = _bmm('ncv,nck->nvk', do3, qe3)

            def step(jj, c, s=s):
                j = SLAB - 1 - jj
                h_ref[j] = c
                return u_ref[j] + dec_ref[pl.ds(s * SLAB + j, 1), :] * c

            carry = lax.fori_loop(0, SLAB, step, carry)

            hh = h_ref[...]
            dqc = _c2(_bmm('ncv,nvk->nck', do3, st_ref[pl.ds(s * SLAB, SLAB)])) * eb
            dkc = _c2(_bmm('ncv,nvk->nck', v3, hh)) * ekb
            dv = _c2(_bmm('nck,nvk->ncv', ke3, hh))

            qc3, kc3, b3, row3 = _c3(qc), _c3(kc), _c3(b), _c3(rowi)
            for j in range(A_CHUNK):
                dj = jnp.exp(jnp.where(row3 >= j, b3 - b3[:, j:j + 1, :], -jnp.inf))
                kj = kc3[:, j:j + 1, :]
                vj = jnp.broadcast_to(v3[:, j:j + 1, :], v3.shape)
                att = _c3(_lane_sum_b(_c2(qc3 * dj * kj), ones))
                datt = _c3(_lane_sum_b(_c2(do3 * vj), ones))
                md = dj * datt
                dqc = dqc + _c2(md * kj)
                sel = row3 == j
                dkc = dkc + _c2(jnp.where(sel, jnp.sum(md * qc3, 1, keepdims=True), 0.0))
                dv = dv + _c2(jnp.where(sel, jnp.sum(att * do3, 1, keepdims=True), 0.0))
            di_ref[rows, :] = dv
            dq_ref[rows, :] = dqc * (sq * (1.0 + q * (1.0 - sq)))

            dbs = _chunk_suffix(qc * dqc - kc * dkc, rowi)
            dbs_ref[rows, :] = dbs
            dkc_ref[rows, :] = dkc
            tot_ref[pl.ds(s * SLAB, SLAB), :] = _c3(dbs)[:, 0:1, :].reshape(SLAB, BLK)
        dnw_ref[...] = jnp.broadcast_to(dnw, (8, BLK))

        rn = lax.broadcasted_iota(jnp.int32, (N_CHUNK, N_CHUNK), 0)
        cn = lax.broadcasted_iota(jnp.int32, (N_CHUNK, N_CHUNK), 1)
        tot_ref[...] = jnp.dot((cn > rn).astype(F32), tot_ref[...], preferred_element_type=F32, precision=HI)
        dlb = jnp.zeros((1, BLK), F32)
        for s in range(N_SLAB):
            rows = _slab_rows(s)
            sf = _sigmoid(f_ref[rows, :])
            fg = lb + (1.0 - lb) * sf
            later = tot_ref[pl.ds(s * SLAB, SLAB), :]
            dlg = _c2(_c3(dbs_ref[rows, :]) + later[:, None, :])
            dfg = dlg / fg - dkc_ref[rows, :]
            df_ref[rows, :] = dfg * (1.0 - lb) * sf * (1.0 - sf)
            dlb = dlb + jnp.sum(dfg * (1.0 - sf), 0, keepdims=True)
        dlg_ref[...] = _dlogits_of(p, dlb, layer)

    def colblk(c0):
        return pl.BlockSpec((SEQ, BLK), lambda h: (0, c0 + h))

    big = jax.ShapeDtypeStruct((SEQ, A_HEADS * BLK), F32)
    return pl.pallas_call(
        body, name=name, grid=(A_HEADS,),
        in_specs=[colblk(QA0), colblk(FA0), colblk(IA0), colblk(GA0), colblk(0), colblk(0),
                  pl.BlockSpec((DEPTH, BLK), lambda h: (0, h)), pl.BlockSpec((1, BLK), lambda h: (0, 0))],
        out_specs=[colblk(0), colblk(0), colblk(0), colblk(0),
                   pl.BlockSpec((8, BLK), lambda h: (h, 0)), pl.BlockSpec((DEPTH, BLK), lambda h: (0, h))],
        out_shape=[big, big, big, big, jax.ShapeDtypeStruct((A_HEADS * 8, BLK), F32),
                   jax.ShapeDtypeStruct((DEPTH, A_HEADS * BLK), F32)],
        scratch_shapes=[pltpu.VMEM((N_CHUNK, BLK), F32), pltpu.VMEM((SLAB, BLK, BLK), F32),
                        pltpu.VMEM((N_CHUNK, BLK, BLK), F32), pltpu.VMEM((SLAB, BLK, BLK), F32),
                        pltpu.VMEM((SEQ, BLK), F32), pltpu.VMEM((SEQ, BLK), F32), pltpu.VMEM((N_CHUNK, BLK), F32)],
        compiler_params=_cp(("parallel",)))(proj, proj, proj, proj, raw, dmix, lb_logits, nw.reshape(1, -1))


SCALE = HEAD_DIM ** -0.5


def _rope_tables():
    half = ROPE_DIM // 2
    inv = ROPE_THETA ** (-jnp.arange(0, ROPE_DIM, 2, dtype=F32) / ROPE_DIM)
    ang = jnp.arange(SEQ, dtype=F32)[:, None] * inv[None, :]
    cos, sin = jnp.cos(ang), jnp.sin(ang)
    pad = jnp.zeros((SEQ, HEAD_DIM - ROPE_DIM), F32)
    zero = jnp.zeros((SEQ, half), F32)
    c = jnp.concatenate([cos, cos, pad + 1.0], 1)
    s_lo = jnp.concatenate([zero, sin, pad], 1)
    s_hi = jnp.concatenate([-sin, zero, pad], 1)
    return c, s_lo, s_hi


def _rope(x, c, s_lo, s_hi):
    half = ROPE_DIM // 2
    return x * c + pltpu.roll(x, half, axis=1) * s_lo + pltpu.roll(x, HEAD_DIM - half, axis=1) * s_hi


def _unrope(dy, c, s_lo, s_hi):
    half = ROPE_DIM // 2
    return dy * c + pltpu.roll(dy * s_lo, HEAD_DIM - half, axis=1) + pltpu.roll(dy * s_hi, half, axis=1)


def _rows(start, size, stride):
    return pl.ds(start, size) if stride == 1 else pl.ds(start, size, stride=stride)


def _band_blocks(patterns):
    out = []
    for p, (max_lag, dil) in enumerate(patterns):
        nb = SEQ // dil // BLK
        for r in range(dil):
            for n in range(nb):
                lo = max(n - 1, 0)
                kn = (n - lo + 1) * BLK
                out.append((p, _rows(r + n * BLK * dil, BLK, dil), _rows(r + lo * BLK * dil, kn, dil), kn,
                            (n - lo) * BLK, max_lag))
    return out


def _band_valid(kn, off, max_lag):
    lag = off + lax.broadcasted_iota(jnp.int32, (BLK, kn), 0) - lax.broadcasted_iota(jnp.int32, (BLK, kn), 1)
    return (lag >= 0) & (lag <= max_lag)


def _attn_fwd(name, proj, tables, sink_b, *, n_heads, rep, q0, k0, v0, patterns):
    n_pat = len(patterns)
    blocks = _band_blocks(patterns)
    has_sink = sink_b is not None

    def body(*refs):
        if has_sink:
            q_ref, k_ref, v_ref, c_ref, sl_ref, sh_ref, sink_ref, o_ref, lse_ref, qr, kr, op = refs
            sk = sink_ref[0:1, 0:1]
        else:
            q_ref, k_ref, v_ref, c_ref, sl_ref, sh_ref, o_ref, lse_ref, qr, kr, op = refs
        c, s_lo, s_hi = c_ref[...], sl_ref[...], sh_ref[...]
        qr[...] = _rope(q_ref[...], c, s_lo, s_hi)
        kr[...] = _rope(k_ref[...], c, s_lo, s_hi)
        for p, qrows, krows, kn, off, max_lag in blocks:
            qb = qr[qrows, :].astype(BF16)
            kb = kr[krows, :].astype(BF16)
            vb = v_ref[krows, :].astype(BF16)
            s = lax.dot_general(qb, kb, NT, preferred_element_type=F32) * SCALE
            s = jnp.where(_band_valid(kn, off, max_lag), s, -jnp.inf)
            m = jnp.max(s, -1, keepdims=True)
            if has_sink:
                m = jnp.maximum(m, sk)
            e = jnp.exp(s - m)
            den = jnp.sum(e, -1, keepdims=True)
            if has_sink:
                den = den + jnp.exp(sk - m)
            o = jnp.dot(e.astype(BF16), vb, preferred_element_type=F32) / den
            op.at[p][qrows, :] = o
            lse_ref.at[p][qrows, :] = jnp.broadcast_to(m + jnp.log(den), (BLK, BLK))
        if n_pat == 1:
            o_ref[...] = op[0]
        else:
            ls = [lse_ref[p] for p in range(n_pat)]
            m = functools.reduce(jnp.maximum, ls)
            es = [jnp.exp(l - m) for l in ls]
            tot = functools.reduce(jnp.add, es)
            acc = None
            for p in range(n_pat):
                t = (es[p] / tot) * op[p]
                acc = t if acc is None else acc + t
            o_ref[...] = acc

    def colblk(fn):
        return pl.BlockSpec((SEQ, BLK), fn)

    tab = pl.BlockSpec((SEQ, BLK), lambda h: (0, 0))
    in_specs = [colblk(lambda h: (0, q0 + h)), colblk(lambda h: (0, k0 + h // rep)), colblk(lambda h: (0, v0 + h // rep)),
                tab, tab, tab]
    args = [proj, proj, proj, *tables]
    if has_sink:
        in_specs.append(pl.BlockSpec((None, 8, BLK), lambda h: (h, 0, 0)))
        args.append(sink_b)
    return pl.pallas_call(
        body, name=name, grid=(n_heads,), in_specs=in_specs,
        out_specs=[colblk(lambda h: (0, h)), pl.BlockSpec((None, n_pat, SEQ, BLK), lambda h: (h, 0, 0, 0))],
        out_shape=[jax.ShapeDtypeStruct((SEQ, n_heads * BLK), F32),
                   jax.ShapeDtypeStruct((n_heads, n_pat, SEQ, BLK), F32)],
        scratch_shapes=[pltpu.VMEM((SEQ, BLK), F32), pltpu.VMEM((SEQ, BLK), F32), pltpu.VMEM((n_pat, SEQ, BLK), F32)],
        compiler_params=_cp(("parallel",)))(*args)


def _attn_bwd(name, proj, mixed, dmix, lse, tables, sink_b, *, n_kv, rep, q0, k0, v0, m0, patterns):
    n_pat = len(patterns)
    n_heads = n_kv * rep
    blocks = _band_blocks(patterns)
    has_sink = sink_b is not None

    def body(*refs):
        if has_sink:
            (q_ref, k_ref, v_ref, o_ref, do_ref, lse_ref, c_ref, sl_ref, sh_ref, sink_ref,
             dq_ref, dk_ref, dv_ref, dsk_ref, qr, kr, dqa, dka, dva, dd, ww) = refs
        else:
            (q_ref, k_ref, v_ref, o_ref, do_ref, lse_ref, c_ref, sl_ref, sh_ref,
             dq_ref, dk_ref, dv_ref, dsk_ref, qr, kr, dqa, dka, dva, dd, ww) = refs
        j = pl.program_id(1)
        c, s_lo, s_hi = c_ref[...], sl_ref[...], sh_ref[...]
        qr[...] = _rope(q_ref[...], c, s_lo, s_hi)
        kr[...] = _rope(k_ref[...], c, s_lo, s_hi)
        dcol = jnp.sum(do_ref[...] * o_ref[...], -1, keepdims=True)
        dd[...] = jnp.broadcast_to(dcol, (SEQ, BLK))
        if n_pat == 1:
            ww[0] = jnp.ones((SEQ, BLK), F32)
        else:
            ls = [lse_ref[p] for p in range(n_pat)]
            m = functools.reduce(jnp.maximum, ls)
            es = [jnp.exp(l - m) for l in ls]
            tot = functools.reduce(jnp.add, es)
            for p in range(n_pat):
                ww[p] = es[p] / tot
        dqa[...] = jnp.zeros((SEQ, BLK), F32)

        @pl.when(j == 0)
        def _():
            dka[...] = jnp.zeros((SEQ, BLK), F32)
            dva[...] = jnp.zeros((SEQ, BLK), F32)

        for p, qrows, krows, kn, off, max_lag in blocks:
            qb = qr[qrows, :].astype(BF16)
            kb = kr[krows, :].astype(BF16)
            vb = v_ref[krows, :].astype(BF16)
            dob = do_ref[qrows, :].astype(BF16)
            lcol = lse_ref.at[p][qrows, :][:, 0:1]
            wcol = ww.at[p][qrows, :][:, 0:1]
            dcb = dd[qrows, :][:, 0:1]
            s = lax.dot_general(qb, kb, NT, preferred_element_type=F32) * SCALE
            a = jnp.where(_band_valid(kn, off, max_lag), jnp.exp(s - lcol), 0.0) * wcol
            dp = lax.dot_general(dob, vb, NT, preferred_element_type=F32)
            ds = (a * (dp - dcb) * SCALE).astype(BF16)
            dqa[qrows, :] += jnp.dot(ds, kb, preferred_element_type=F32)
            dka[krows, :] += lax.dot_general(ds, qb, TN, preferred_element_type=F32)
            dva[krows, :] += lax.dot_general(a.astype(BF16), dob, TN, preferred_element_type=F32)

        if has_sink:
            sk = sink_ref[0:1, 0:1]
            ps = jnp.exp(sk - lse_ref[0][:, 0:1])
            dsk_ref[...] = jnp.full((8, BLK), -jnp.sum(ps * dcol), F32)
        else:
            dsk_ref[...] = jnp.zeros((8, BLK), F32)
        dq_ref[...] = _unrope(dqa[...], c, s_lo, s_hi)

        @pl.when(j == rep - 1)
        def _():
            dk_ref[...] = _unrope(dka[...], c, s_lo, s_hi)
            dv_ref[...] = dva[...]

    def colblk(fn):
        return pl.BlockSpec((SEQ, BLK), fn)

    tab = pl.BlockSpec((SEQ, BLK), lambda g, j: (0, 0))
    in_specs = [colblk(lambda g, j: (0, q0 + g * rep + j)), colblk(lambda g, j: (0, k0 + g)), colblk(lambda g, j: (0, v0 + g)),
                colblk(lambda g, j: (0, m0 + g * rep + j)), colblk(lambda g, j: (0, m0 + g * rep + j)),
                pl.BlockSpec((None, n_pat, SEQ, BLK), lambda g, j: (g * rep + j, 0, 0, 0)), tab, tab, tab]
    args = [proj, proj, proj, mixed, dmix, lse, *tables]
    if has_sink:
        in_specs.append(pl.BlockSpec((None, 8, BLK), lambda g, j: (g * rep + j, 0, 0)))
        args.append(sink_b)
    acc = pltpu.VMEM((SEQ, BLK), F32)
    return pl.pallas_call(
        body, name=name, grid=(n_kv, rep), in_specs=in_specs,
        out_specs=[colblk(lambda g, j: (0, g * rep + j)), colblk(lambda g, j: (0, g)), colblk(lambda g, j: (0, g)),
                   pl.BlockSpec((None, 8, BLK), lambda g, j: (g * rep + j, 0, 0))],
        out_shape=[jax.ShapeDtypeStruct((SEQ, n_heads * BLK), F32), jax.ShapeDtypeStruct((SEQ, n_kv * BLK), F32),
                   jax.ShapeDtypeStruct((SEQ, n_kv * BLK), F32), jax.ShapeDtypeStruct((n_heads, 8, BLK), F32)],
        scratch_shapes=[acc, acc, acc, acc, acc, acc, pltpu.VMEM((n_pat, SEQ, BLK), F32)],
        compiler_params=_cp(("parallel", "arbitrary")))(*args)


B_PATTERNS = tuple((w // d, d) for w, d in DILATED_PATTERNS)
C_PATTERNS = ((C_WINDOW - 1, 1),)


ANY = pl.BlockSpec(memory_space=pl.ANY)
CHIP_MASKS = ((1, 0), (0, 1), (1, 1))


def _coords():
    return lax.axis_index("x"), lax.axis_index("y"), lax.axis_index("c")


def _flip(v, m):
    return 1 - v if m else v


def _into_slot(name, w, layer, k_idx, dtype, run_after=None):
    _, rows, cols = w.shape
    tr = rows // 8 if rows % 64 == 0 else rows

    def body(k_ref, w_ref, *rest):
        rest[-1][...] = w_ref[...].astype(dtype)

    in_specs = [pl.BlockSpec((None, tr, cols), lambda i, k: (layer, i, 0))]
    args = [k_idx, w]
    if run_after is not None:
        in_specs.append(pl.BlockSpec(memory_space=pl.ANY))
        args.append(run_after)
    return pl.pallas_call(
        body, name=name,
        grid_spec=pltpu.PrefetchScalarGridSpec(
            num_scalar_prefetch=1, grid=(rows // tr,), in_specs=in_specs,
            out_specs=pl.BlockSpec((None, tr, cols), lambda i, k: (k[0], i, 0))),
        out_shape=jax.ShapeDtypeStruct((N_CHIPS, rows, cols), dtype),
        compiler_params=_cp(("parallel",)))(*args)


HBM_SPEC = pl.BlockSpec(memory_space=pltpu.HBM)
SEM_SPEC = pl.BlockSpec(memory_space=pltpu.SEMAPHORE)
TOKEN_SPEC = pl.BlockSpec(memory_space=pltpu.VMEM)
TOKEN_SHAPE = jax.ShapeDtypeStruct((8, BLK), F32)
DATAFLOW = pltpu.SideEffectType.DATAFLOW_SIDE_EFFECTING


def _hbm(a):
    return pltpu.with_memory_space_constraint(a, pltpu.HBM)


def _hbm_like(bufs):
    return [pltpu.HBM(b.shape, b.dtype) for b in bufs]


def _gather_start(name, stages):
    flat = [b for st in stages for b in st]
    n, ns = len(flat), len(stages)

    def body(*refs):
        ins = refs[:n]
        sems = refs[n:n + 2 * ns]
        token = refs[-1]
        x, y, c = _coords()
        k_me = 2 * x + y
        a = 0
        for s, st in enumerate(stages):
            for i in range(len(st)):
                mine = ins[a].at[k_me, c]
                for m, (mx, my) in enumerate(CHIP_MASKS):
                    pltpu.make_async_remote_copy(src_ref=mine, dst_ref=mine, send_sem=sems[2 * s].at[i * 3 + m],
                                                 recv_sem=sems[2 * s + 1].at[i * 3 + m],
                                                 device_id=(_flip(x, mx), _flip(y, my), c), device_id_type=MESH).start()
                a += 1
        token[...] = jnp.zeros_like(token)

    sem_shapes = []
    for st in stages:
        sem_shapes += [pltpu.SemaphoreType.DMA((3 * len(st),))] * 2
    out = pl.pallas_call(
        body, name=name, in_specs=[HBM_SPEC] * n,
        out_specs=tuple([SEM_SPEC] * (2 * ns) + [HBM_SPEC] * n + [TOKEN_SPEC]),
        out_shape=tuple(sem_shapes + _hbm_like(flat) + [TOKEN_SHAPE]),
        input_output_aliases={i: 2 * ns + i for i in range(n)},
        compiler_params=pltpu.CompilerParams(has_side_effects=DATAFLOW),
    )(*[_hbm(b) for b in flat])
    sems, bufs, token = out[:2 * ns], out[2 * ns:2 * ns + n], out[-1]
    res, a = [], 0
    for s, st in enumerate(stages):
        res.append((sems[2 * s], sems[2 * s + 1], list(bufs[a:a + len(st)])))
        a += len(st)
    return res, token


def _gather_forward(name, stage, after):
    ssem_in, rsem_in, bufs = stage
    n = len(bufs)

    def body(*refs):
        ins = refs[:n]
        s_in, r_in, _ = refs[n:n + 3]
        s_out, r_out = refs[n + 3:n + 5]
        token = refs[-1]
        x, y, c = _coords()
        for i in range(n):
            for m, (mx, my) in enumerate(CHIP_MASKS):
                kp = 2 * _flip(x, mx) + _flip(y, my)
                blk = ins[i].at[kp, c]
                got = pltpu.make_async_remote_copy(src_ref=blk, dst_ref=blk, send_sem=s_in.at[i * 3 + m],
                                                   recv_sem=r_in.at[i * 3 + m], device_id=(x, y, 1 - c), device_id_type=MESH)
                got.wait_send()
                got.wait_recv()
                pltpu.make_async_remote_copy(src_ref=blk, dst_ref=blk, send_sem=s_out.at[i * 3 + m],
                                             recv_sem=r_out.at[i * 3 + m], device_id=(x, y, 1 - c), device_id_type=MESH).start()
        token[...] = jnp.zeros_like(token)

    sem = pltpu.SemaphoreType.DMA((3 * n,))
    out = pl.pallas_call(
        body, name=name, in_specs=[HBM_SPEC] * n + [SEM_SPEC, SEM_SPEC, ANY],
        out_specs=tuple([SEM_SPEC, SEM_SPEC] + [HBM_SPEC] * n + [TOKEN_SPEC]),
        out_shape=tuple([sem, sem] + _hbm_like(bufs) + [TOKEN_SHAPE]),
        input_output_aliases={i: 2 + i for i in range(n)},
        compiler_params=pltpu.CompilerParams(has_side_effects=DATAFLOW),
    )(*bufs, ssem_in, rsem_in, after)
    return (out[0], out[1], list(out[2:2 + n])), out[-1]


def _gather_wait(name, stage, after):
    ssem, rsem, bufs = stage
    n = len(bufs)

    def body(*refs):
        ins = refs[:n]
        s_in, r_in, _ = refs[n:n + 3]
        x, y, c = _coords()
        for i in range(n):
            for m, (mx, my) in enumerate(CHIP_MASKS):
                kp = 2 * _flip(x, mx) + _flip(y, my)
                sent, got = ins[i].at[kp, c], ins[i].at[kp, 1 - c]
                cp = pltpu.make_async_remote_copy(src_ref=sent, dst_ref=got, send_sem=s_in.at[i * 3 + m],
                                                  recv_sem=r_in.at[i * 3 + m], device_id=(x, y, 1 - c), device_id_type=MESH)
                cp.wait_send()
                cp.wait_recv()

    out = pl.pallas_call(
        body, name=name, in_specs=[HBM_SPEC] * n + [SEM_SPEC, SEM_SPEC, ANY],
        out_specs=tuple([HBM_SPEC] * n), out_shape=tuple(_hbm_like(bufs)),
        input_output_aliases={i: i for i in range(n)},
        compiler_params=pltpu.CompilerParams(has_side_effects=DATAFLOW),
    )(*bufs, ssem, rsem, after)
    return list(out)


def _pair_swap(name, grads):
    n = len(grads)

    def body(*refs):
        ins, outs = refs[:n], refs[n:2 * n]
        ssem, rsem = refs[2 * n:]
        x, y, c = _coords()
        cps = []
        for a in range(n):
            for j in range(N_CHIPS):
                cp = pltpu.make_async_remote_copy(src_ref=ins[a].at[j, 1 - c], dst_ref=outs[a].at[j],
                                                  send_sem=ssem.at[a * N_CHIPS + j], recv_sem=rsem.at[a * N_CHIPS + j],
                                                  device_id=(x, y, 1 - c), device_id_type=MESH)
                cp.start()
                cps.append(cp)
        for cp in cps:
            cp.wait()

    return pl.pallas_call(
        body, name=name, in_specs=[ANY] * n, out_specs=[ANY] * n,
        out_shape=[jax.ShapeDtypeStruct((N_CHIPS,) + g.shape[2:], g.dtype) for g in grads],
        scratch_shapes=[pltpu.SemaphoreType.DMA((n * N_CHIPS,)), pltpu.SemaphoreType.DMA((n * N_CHIPS,))],
        compiler_params=pltpu.CompilerParams(has_side_effects=True),
    )(*grads)


def _scatter_start(name, parts):
    n = len(parts)

    def body(*refs):
        ins, lands = refs[:n], refs[n:2 * n]
        ssem, rsem = refs[2 * n:2 * n + 2]
        x, y, c = _coords()
        k_me = 2 * x + y
        for a in range(n):
            for m, (mx, my) in enumerate(CHIP_MASKS):
                px, py = _flip(x, mx), _flip(y, my)
                pltpu.make_async_remote_copy(src_ref=ins[a].at[2 * px + py], dst_ref=lands[a].at[k_me],
                                             send_sem=ssem.at[a * 3 + m], recv_sem=rsem.at[a * 3 + m],
                                             device_id=(px, py, c), device_id_type=MESH).start()

    sem = pltpu.SemaphoreType.DMA((3 * n,))
    out = pl.pallas_call(
        body, name=name, in_specs=[HBM_SPEC] * (2 * n),
        out_specs=tuple([SEM_SPEC, SEM_SPEC] + [HBM_SPEC] * (2 * n)),
        out_shape=tuple([sem, sem] + _hbm_like(parts) + _hbm_like(parts)),
        input_output_aliases={i: 2 + i for i in range(2 * n)},
        compiler_params=pltpu.CompilerParams(has_side_effects=DATAFLOW),
    )(*[_hbm(p) for p in parts], *[_hbm(lax.empty(p.shape, p.dtype)) for p in parts])
    return out[0], out[1], list(out[2:2 + n]), list(out[2 + n:])


def _scatter_wait(name, started, after):
    ssem, rsem, parts, lands = started
    n = len(parts)

    def body(*refs):
        ins, lnd = refs[:n], refs[n:2 * n]
        s_in, r_in, _ = refs[2 * n:2 * n + 3]
        x, y, c = _coords()
        k_me = 2 * x + y
        for a in range(n):
            for m, (mx, my) in enumerate(CHIP_MASKS):
                px, py = _flip(x, mx), _flip(y, my)
                cp = pltpu.make_async_remote_copy(src_ref=ins[a].at[2 * px + py], dst_ref=lnd[a].at[k_me],
                                                  send_sem=s_in.at[a * 3 + m], recv_sem=r_in.at[a * 3 + m],
                                                  device_id=(px, py, c), device_id_type=MESH)
                cp.wait_send()
                cp.wait_recv()

    out = pl.pallas_call(
        body, name=name, in_specs=[HBM_SPEC] * (2 * n) + [SEM_SPEC, SEM_SPEC, ANY],
        out_specs=tuple([HBM_SPEC] * (2 * n)), out_shape=tuple(_hbm_like(parts) + _hbm_like(lands)),
        input_output_aliases={i: i for i in range(2 * n)},
        compiler_params=pltpu.CompilerParams(has_side_effects=DATAFLOW),
    )(*parts, *lands, ssem, rsem, after)
    return list(out[:n]), list(out[n:])


def _pair_gather(name, bufs):
    n = len(bufs)

    def body(*refs):
        outs = refs[n:2 * n]
        ssem, rsem = refs[2 * n:]
        x, y, c = _coords()
        cps = []
        for a in range(n):
            mine = outs[a].at[c]
            cp = pltpu.make_async_remote_copy(src_ref=mine, dst_ref=mine, send_sem=ssem.at[a],
                                              recv_sem=rsem.at[a], device_id=(x, y, 1 - c), device_id_type=MESH)
            cp.start()
            cps.append(cp)
        for cp in cps:
            cp.wait()

    return pl.pallas_call(
        body, name=name, in_specs=[ANY] * n, out_specs=[ANY] * n,
        out_shape=[jax.ShapeDtypeStruct(b.shape, b.dtype) for b in bufs],
        input_output_aliases={a: a for a in range(n)},
        scratch_shapes=[pltpu.SemaphoreType.DMA((n,)), pltpu.SemaphoreType.DMA((n,))],
        compiler_params=pltpu.CompilerParams(has_side_effects=True),
    )(*bufs)


DEV_MASKS = tuple((mx, my, mc) for mx in (0, 1) for my in (0, 1) for mc in (0, 1) if (mx, my, mc) != (0, 0, 0))


def _gather_small(buf):
    def body(in_ref, out_ref, ssem, rsem, lsem):
        x, y, c = _coords()
        me = 4 * x + 2 * y + c
        cps = [pltpu.make_async_copy(in_ref, out_ref.at[me], lsem)]
        cps[0].start()
        for t, (mx, my, mc) in enumerate(DEV_MASKS):
            cp = pltpu.make_async_remote_copy(src_ref=in_ref, dst_ref=out_ref.at[me], send_sem=ssem.at[t],
                                              recv_sem=rsem.at[t], device_id=(_flip(x, mx), _flip(y, my), _flip(c, mc)),
                                              device_id_type=MESH)
            cp.start()
            cps.append(cp)
        for cp in cps:
            cp.wait()

    return pl.pallas_call(
        body, name="gather_small", in_specs=[ANY], out_specs=ANY,
        out_shape=jax.ShapeDtypeStruct((N_DEV,) + buf.shape, buf.dtype),
        scratch_shapes=[pltpu.SemaphoreType.DMA((N_DEV - 1,)), pltpu.SemaphoreType.DMA((N_DEV - 1,)),
                        pltpu.SemaphoreType.DMA(())],
        compiler_params=pltpu.CompilerParams(has_side_effects=True),
    )(buf)


def _row_tile(rows):
    return rows // 2 if rows % 16 == 0 else rows


def _pair_add(name, grad, got, c_idx):
    _, _, r2, cols = grad.shape
    tr = _row_tile(r2)

    def body(c_ref, a_ref, b_ref, o_ref):
        o_ref[...] = (a_ref[...] + b_ref[...]).astype(BF16)

    return pl.pallas_call(
        body, name=name,
        grid_spec=pltpu.PrefetchScalarGridSpec(
            num_scalar_prefetch=1, grid=(N_CHIPS, r2 // tr),
            in_specs=[pl.BlockSpec((None, None, tr, cols), lambda j, i, c: (j, c[0], i, 0)),
                      pl.BlockSpec((None, tr, cols), lambda j, i, c: (j, i, 0))],
            out_specs=pl.BlockSpec((None, tr, cols), lambda j, i, c: (j, i, 0))),
        out_shape=jax.ShapeDtypeStruct((N_CHIPS, r2, cols), BF16),
        compiler_params=_cp(("parallel", "parallel")))(c_idx, grad, got)


def _chip_add(name, part, got, kc_idx):
    _, r2, cols = got.shape
    tr = _row_tile(r2)

    def body(k_ref, p_ref, g1_ref, g2_ref, g3_ref, o_ref):
        acc = p_ref[...].astype(F32)
        for g_ref in (g1_ref, g2_ref, g3_ref):
            acc = acc + g_ref[...].astype(F32)
        o_ref[...] = acc

    def slot(d):
        return pl.BlockSpec((None, tr, cols), lambda i, k: ((k[0] + d) % N_CHIPS, i, 0))

    return pl.pallas_call(
        body, name=name,
        grid_spec=pltpu.PrefetchScalarGridSpec(
            num_scalar_prefetch=1, grid=(r2 // tr,),
            in_specs=[slot(0), slot(1), slot(2), slot(3)],
            out_specs=pl.BlockSpec((None, tr, cols), lambda i, k: (k[1], i, 0))),
        out_shape=jax.ShapeDtypeStruct((2, r2, cols), F32),
        compiler_params=_cp(("parallel",)))(kc_idx, part, got, got, got)


def _adam_math(w, g, m, v):
    m2 = ADAM_B1 * m + (1.0 - ADAM_B1) * g
    v2 = ADAM_B2 * v + (1.0 - ADAM_B2) * (g * g)
    m_hat = m2 / (1.0 - ADAM_B1 ** ADAM_STEP)
    v_hat = v2 / (1.0 - ADAM_B2 ** ADAM_STEP)
    delta = -ADAM_LR * (m_hat / (jnp.sqrt(v_hat) + ADAM_EPS) + ADAM_WD * w)
    return delta, m2, v2


def _adamw_matrix(name, w, g_layers, m, v):
    _, rows, cols = w.shape
    tr = rows // 16 if rows % 128 == 0 else rows // 8

    def body(w_ref, g0_ref, g1_ref, m_ref, v_ref, go_ref, d_ref, mo_ref, vo_ref):
        g = jnp.where(pl.program_id(0) == 0, g0_ref[...], g1_ref[...])
        go_ref[...] = g
        d_ref[...], mo_ref[...], vo_ref[...] = _adam_math(w_ref[...], g, m_ref[...], v_ref[...])

    lay = pl.BlockSpec((None, tr, cols), lambda l, i: (l, i, 0))
    flat = pl.BlockSpec((tr, cols), lambda l, i: (i, 0))
    shp = jax.ShapeDtypeStruct(w.shape, F32)
    return pl.pallas_call(body, name=name, grid=(DEPTH, rows // tr), in_specs=[lay, flat, flat, lay, lay],
                          out_specs=[lay, lay, lay, lay], out_shape=[shp, shp, shp, shp],
                          compiler_params=_cp(("parallel", "parallel")))(w, g_layers[0], g_layers[1], m, v)


def _sum_small(gathered):
    def body(g_ref, o_ref):
        acc = g_ref[0]
        for d in range(1, N_DEV):
            acc = acc + g_ref[d]
        o_ref[...] = acc

    return pl.pallas_call(body, name="sum_small", out_shape=jax.ShapeDtypeStruct(gathered.shape[1:], F32),
                          compiler_params=_cp())(gathered)


def _adamw_small(w, g, m, v):
    def body(w_ref, g_ref, m_ref, v_ref, d_ref, mo_ref, vo_ref):
        d_ref[...], mo_ref[...], vo_ref[...] = _adam_math(w_ref[...], g_ref[...], m_ref[...], v_ref[...])

    shp = jax.ShapeDtypeStruct(w.shape, F32)
    return pl.pallas_call(body, name="adamw_small", out_shape=[shp, shp, shp], compiler_params=_cp())(w, g, m, v)


def _pack(arrays, rows):
    flat = jnp.concatenate([a.reshape(-1) for a in arrays])
    return jnp.pad(flat, (0, rows * BLK - flat.shape[0])).reshape(rows, BLK)


def _unpack(buf, shapes):
    flat = buf.reshape(-1)
    out, pos = [], 0
    for s in shapes:
        n = math.prod(s)
        out.append(flat[pos:pos + n].reshape(s))
        pos += n
    return out


def _rows_for(shapes):
    n = sum(math.prod(s) for s in shapes)
    return -(-n // (8 * BLK)) * 8


def _rs_begin(tag, grads, c_idx):
    split = [g.reshape(N_CHIPS, 2, g.shape[1] // 2, g.shape[2]) for g in grads]
    got = _pair_swap(f"rs_pair_swap{tag}", split)
    parts = [_pair_add(f"rs_pair_add{tag}_{i}", s, r, c_idx) for i, (s, r) in enumerate(zip(split, got))]
    return _scatter_start(f"rs_scatter_start{tag}", parts)


def _rs_end(tag, started, after, kc_idx):
    parts, lands = _scatter_wait(f"rs_scatter_wait{tag}", started, after)
    halves = [_chip_add(f"rs_chip_add{tag}_{i}", p, r, kc_idx) for i, (p, r) in enumerate(zip(parts, lands))]
    full = _pair_gather(f"rs_pair_gather{tag}", halves)
    return [f.reshape(2 * f.shape[1], f.shape[2]) for f in full]


def kernel(x, w_in, lb_logits, a_norm_w, c_sinks, w_out, ln1_g, ln1_b, w_gate, w_up, conv_w, conv_b, w_down, ln2_g, ln2_b, loss_target, m_w_in, m_lb_logits, m_a_norm_w, m_c_sinks, m_w_out, m_ln1_g, m_ln1_b, m_w_gate, m_w_up, m_conv_w, m_conv_b, m_w_down, m_ln2_g, m_ln2_b, v_w_in, v_lb_logits, v_a_norm_w, v_c_sinks, v_w_out, v_ln1_g, v_ln1_b, v_w_gate, v_w_up, v_conv_w, v_conv_b, v_w_down, v_ln2_g, v_ln2_b):
    cx, cy, cc = _coords()
    c_idx = jnp.reshape(cc, (1,)).astype(jnp.int32)
    k_me = 2 * cx + cy
    k_idx = jnp.reshape(k_me, (1,)).astype(jnp.int32)
    kc_idx = jnp.stack([k_me, cc]).astype(jnp.int32)

    def slot(nm, w, l, run_after=None):
        b = _into_slot(f"slot_{nm}{l}", w, l, k_idx, BF16, run_after)
        return b.reshape(N_CHIPS, 2, b.shape[1] // 2, b.shape[2])

    cw_slot = _into_slot("slot_cw", conv_w.reshape(1, DEPTH * CONV_WIDTH, FF_SHARD), 0, k_idx, F32)
    cw_slot = cw_slot.reshape(N_CHIPS, DEPTH, CONV_WIDTH, FF_SHARD)
    first, token = _gather_start("gather_start0", [[slot("wi", w_in, 0), cw_slot]])
    sl = [{nm: slot(nm, w, l, token) for nm, w in (("wi", w_in), ("wo", w_out), ("wg", w_gate), ("wu", w_up), ("wd", w_down))
           if (nm, l) != ("wi", 0)} for l in range(DEPTH)]
    rest, token = _gather_start("gather_start1", [
        [sl[0]["wo"], sl[0]["wg"], sl[0]["wu"], sl[0]["wd"]],
        [sl[1]["wi"], sl[1]["wo"]],
        [sl[1]["wg"], sl[1]["wu"], sl[1]["wd"]]])
    stages = first + rest

    def mat(b):
        return b.reshape(N_CHIPS, 2 * b.shape[2], b.shape[3])

    fwd0, token = _gather_forward("gather_fwd0", stages[0], token)
    wi0, cw_all = _gather_wait("gather_wait0", fwd0, token)
    cw_full = jnp.transpose(cw_all, (1, 2, 0, 3)).reshape(DEPTH, CONV_WIDTH, D_FF)
    tables = _rope_tables()
    weights = [dict(wi=mat(wi0)), dict()]

    h = x[0]
    saved = []
    for l in range(DEPTH):
        w = weights[l]
        proj = _fwd_colsharded(f"proj{l}", h, w["wi"])
        o_a, raw = _hgrn_fwd(f"hgrn_fwd{l}", proj, lb_logits, a_norm_w[l], l)
        o_b, lse_b = _attn_fwd(f"dilated_fwd{l}", proj, tables, None, n_heads=B_HEADS, rep=1, q0=QB0, k0=KB0, v0=VB0,
                               patterns=B_PATTERNS)
        st = stages[1] if l == 0 else stages[3]
        fwd, _ = _gather_forward(f"gather_fwd{2 * l + 1}", st, o_b)
        sink_b = jnp.broadcast_to(c_sinks[l][:, None, None], (C_HEADS, 8, BLK))
        o_c, lse_c = _attn_fwd(f"window_fwd{l}", proj, tables, sink_b, n_heads=C_HEADS, rep=C_HEADS // C_KV_HEADS,
                               q0=QC0, k0=KC0, v0=VC0, patterns=C_PATTERNS)
        got = _gather_wait(f"gather_wait{2 * l + 1}", fwd, o_c)
        if l == 0:
            w["wo"], w["wg"], w["wu"], w["wd"] = (mat(b) for b in got)
        else:
            w["wg"], w["wu"], w["wd"] = (mat(b) for b in got)
        mixed = jnp.concatenate([o_a, o_b, o_c], axis=1)
        y1 = _fwd_rowsharded(f"wout{l}", mixed, w["wo"], OUT_SHARD)
        x1 = _ln_fwd(f"ln1_fwd{l}", h, y1, ln1_g[l], ln1_b[l])
        g = _fwd_colsharded(f"gate{l}", x1, w["wg"])
        u = _fwd_colsharded(f"up{l}", x1, w["wu"])
        if l == 0:
            fwd, _ = _gather_forward("gather_fwd2", stages[2], u)
        hh = _conv_gate_fwd(f"conv_fwd{l}", g, u, cw_full[l], conv_b[l])
        y2 = _fwd_rowsharded(f"down{l}", hh, w["wd"], FF_SHARD)
        x2 = _ln_fwd(f"ln2_fwd{l}", x1, y2, ln2_g[l], ln2_b[l])
        if l == 0:
            weights[1]["wi"], weights[1]["wo"] = (mat(b) for b in _gather_wait("gather_wait2", fwd, x2))
        saved.append((h, proj, raw, lse_b, sink_b, lse_c, mixed, y1, x1, g, u, hh, y2))
        h = x2

    dy, loss_part = _loss_head(h, loss_target[0])

    d_res, d_path = None, dy
    small = [None] * DEPTH
    mat_grads = [None] * DEPTH
    pending = []
    for l in reversed(range(DEPTH)):
        h_in, proj, raw, lse_b, sink_b, lse_c, mixed, y1, x1, g, u, hh, y2 = saved[l]
        wi, wo, wg, wu, wd = (weights[l][k] for k in ("wi", "wo", "wg", "wu", "wd"))
        dz2, d_ln2g, d_ln2b = _ln_bwd(f"ln2_bwd{l}", x1, y2, ln2_g[l], d_res, d_path,
                                      run_after=pending[1][2][0] if pending else None)
        dhh = _bwd_act_rowsharded(f"down_dx{l}", dz2, wd, FF_SHARD)
        d_wd = _bwd_w_rowsharded(f"down_dw{l}", hh, dz2, FF_SHARD)
        dg, du, d_cw, d_cb = _conv_gate_bwd(f"conv_bwd{l}", g, u, cw_full[l], conv_b[l], dhh)
        dx1 = _bwd_act_colsharded(f"gateup_dx{l}", [(dg, wg), (du, wu)])
        d_wg = _bwd_w_colsharded(f"gate_dw{l}", x1, dg)
        d_wu = _bwd_w_colsharded(f"up_dw{l}", x1, du)
        if pending:
            (s_ffn, s_mix), pending = pending, []
            g_wg, g_wu, g_wd = _rs_end(f"{l + 1}f", s_ffn, d_wu, kc_idx)
            g_wi, g_wo = _rs_end(f"{l + 1}m", s_mix, d_wu, kc_idx)
            mat_grads[l + 1] = [g_wi, g_wo, g_wg, g_wu, g_wd]
        s_ffn = _rs_begin(f"{l}f", [d_wg, d_wu, d_wd], c_idx)
        dz1, d_ln1g, d_ln1b = _ln_bwd(f"ln1_bwd{l}", h_in, y1, ln1_g[l], dz2, dx1, run_after=s_ffn[2][0])
        dmix = _bwd_act_rowsharded(f"wout_dx{l}", dz1, wo, OUT_SHARD)
        d_wo = _bwd_w_rowsharded(f"wout_dw{l}", mixed, dz1, OUT_SHARD)
        dq_a, df_a, di_a, dg_a, d_nw, d_lb = _hgrn_bwd(f"hgrn_bwd{l}", proj, raw, dmix, lb_logits, a_norm_w[l], l)
        dq_b, dk_b, dv_b, _ = _attn_bwd(f"dilated_bwd{l}", proj, mixed, dmix, lse_b, tables, None, n_kv=B_HEADS, rep=1,
                                        q0=QB0, k0=KB0, v0=VB0, m0=A_HEADS, patterns=B_PATTERNS)
        dq_c, dk_c, dv_c, d_sink = _attn_bwd(f"window_bwd{l}", proj, mixed, dmix, lse_c, tables, sink_b, n_kv=C_KV_HEADS,
                                             rep=C_HEADS // C_KV_HEADS, q0=QC0, k0=KC0, v0=VC0, m0=A_HEADS + B_HEADS,
                                             patterns=C_PATTERNS)
        dproj = jnp.concatenate([dq_a, df_a, di_a, dg_a, dq_b, dk_b, dv_b, dq_c, dk_c, dv_c], axis=1)
        dxp = _bwd_act_colsharded(f"proj_dx{l}", [(dproj, wi)])
        d_wi = _bwd_w_colsharded(f"proj_dw{l}", h_in, dproj)
        d_res, d_path = dz1, dxp
        pending = [s_ffn, _rs_begin(f"{l}m", [d_wi, d_wo], c_idx)]
        small[l] = (d_lb, d_nw.reshape(A_HEADS, 8, BLK)[:, 0].sum(0), d_sink[:, 0, 0], d_ln1g[0], d_ln1b[0],
                    d_cw, d_cb[0], d_ln2g[0], d_ln2b[0])
    grad_x2 = _axpy("grad_x", d_res, d_path)
    grad_x = grad_x2[None]

    g_lb = small[0][0] + small[1][0]
    per_layer = [jnp.stack([small[0][i], small[1][i]]) for i in range(1, 9)]
    small_shapes = [(DEPTH, 4 * BLK), (DEPTH, BLK), (DEPTH, C_HEADS), (DEPTH, D_MODEL), (DEPTH, D_MODEL),
                    (DEPTH, CONV_WIDTH, D_FF), (DEPTH, D_FF), (DEPTH, D_MODEL), (DEPTH, D_MODEL), (BLK,)]
    rows = _rows_for(small_shapes)
    total = _sum_small(_gather_small(_pack([g_lb] + per_layer + [loss_part[0]], rows)))
    g_lb, g_nw, g_sink, g_ln1g, g_ln1b, g_cw_full, g_cb, g_ln2g, g_ln2b, loss_row = _unpack(total, small_shapes)
    loss = loss_row[0]
    g_cw = lax.dynamic_slice_in_dim(g_cw_full, k_me * FF_SHARD, FF_SHARD, axis=2)

    sw = [lb_logits, a_norm_w, c_sinks, ln1_g, ln1_b, conv_w, conv_b, ln2_g, ln2_b]
    sg = [g_lb, g_nw, g_sink, g_ln1g, g_ln1b, g_cw, g_cb, g_ln2g, g_ln2b]
    sm = [m_lb_logits, m_a_norm_w, m_c_sinks, m_ln1_g, m_ln1_b, m_conv_w, m_conv_b, m_ln2_g, m_ln2_b]
    sv = [v_lb_logits, v_a_norm_w, v_c_sinks, v_ln1_g, v_ln1_b, v_conv_w, v_conv_b, v_ln2_g, v_ln2_b]
    shapes = [a.shape for a in sw]
    prow = _rows_for(shapes)
    sd, snm, snv = (_unpack(b, shapes) for b in _adamw_small(_pack(sw, prow), _pack(sg, prow), _pack(sm, prow), _pack(sv, prow)))

    s_ffn, s_mix = pending
    g_wg, g_wu, g_wd = _rs_end("0f", s_ffn, grad_x2, kc_idx)
    g_wi, g_wo = _rs_end("0m", s_mix, sd[0], kc_idx)
    mat_grads[0] = [g_wi, g_wo, g_wg, g_wu, g_wd]
    names = ["w_in", "w_out", "w_gate", "w_up", "w_down"]
    mw = [w_in, w_out, w_gate, w_up, w_down]
    mm = [m_w_in, m_w_out, m_w_gate, m_w_up, m_w_down]
    mv = [v_w_in, v_w_out, v_w_gate, v_w_up, v_w_down]
    mg, md, mnm, mnv = [], [], [], []
    for i in range(5):
        go, d, nm, nv = _adamw_matrix(f"adamw_{names[i]}", mw[i], [mat_grads[0][i], mat_grads[1][i]], mm[i], mv[i])
        mg.append(go), md.append(d), mnm.append(nm), mnv.append(nv)

    def ordered(mat, sm_):
        return [mat[0], sm_[0], sm_[1], sm_[2], mat[1], sm_[3], sm_[4], mat[2], mat[3], sm_[5], sm_[6], mat[4], sm_[7], sm_[8]]

    return (loss, grad_x, *ordered(mg, sg), *ordered(md, sd), *ordered(mnm, snm), *ordered(mnv, snv))
```

```python
import functools
import math

import jax
import jax.numpy as jnp
from jax import lax
from jax.experimental import pallas as pl
from jax.experimental.pallas import tpu as pltpu

F32 = jnp.float32
BF16 = jnp.bfloat16

D_MODEL = 2048
SEQ = 2048
DEPTH = 2
HEAD_DIM = 128
A_HEADS = 4
B_HEADS = 6
C_HEADS = 6
C_KV_HEADS = 2
A_CHUNK = 16
DILATED_PATTERNS = ((128, 1), (512, 4), (2048, 16))
C_WINDOW = 128
ROPE_THETA = 500000.0
ROPE_DIM = HEAD_DIM // 4
D_FF = 5632
CONV_WIDTH = 3
LN_EPS = 1e-5
ALPHA = (2 * DEPTH) ** 0.25
IN_WIDTH = 5632
MIX_WIDTH = 2048
ADAM_LR = 0.001
ADAM_B1 = 0.9
ADAM_B2 = 0.999
ADAM_EPS = 1e-08
ADAM_WD = 0.01
ADAM_STEP = 10

N_CHIPS = 4
N_DEV = 8
FF_SHARD = D_FF // N_CHIPS
OUT_SHARD = MIX_WIDTH // N_CHIPS
BLK = 128
N_CHUNK = SEQ // A_CHUNK
SLAB = 32

QA0, FA0, IA0, GA0 = 0, 4, 8, 12
QB0, KB0, VB0 = 16, 22, 28
QC0, KC0, VC0 = 34, 40, 42

VMEM_LIMIT_V7X = 56 * 1024 * 1024
HI = lax.Precision.HIGHEST
MESH = pl.DeviceIdType.MESH


def _cp(sem=None, vmem=VMEM_LIMIT_V7X, **kw):
    return pltpu.CompilerParams(dimension_semantics=sem, vmem_limit_bytes=vmem, **kw)


def _sigmoid(x):
    return 1.0 / (1.0 + jnp.exp(-x))


def _mm(name, pairs, dims, grid, a_specs, b_specs, out_spec, out_shape, nk=1, acc_shape=None):
    n_pairs = len(pairs)

    def body(*refs):
        o_ref = refs[2 * n_pairs]
        part = None
        for p in range(n_pairs):
            a = refs[2 * p][...].astype(BF16)
            b = refs[2 * p + 1][...].astype(BF16)
            t = lax.dot_general(a, b, dims, preferred_element_type=F32)
            part = t if part is None else part + t
        if nk == 1:
            o_ref[...] = part.astype(o_ref.dtype)
        else:
            acc = refs[2 * n_pairs + 1]
            k = pl.program_id(len(grid) - 1)

            @pl.when(k == 0)
            def _():
                acc[...] = part

            @pl.when(k > 0)
            def _():
                acc[...] += part

            @pl.when(k == nk - 1)
            def _():
                o_ref[...] = acc[...].astype(o_ref.dtype)

    in_specs, args = [], []
    for (a, b), sa, sb in zip(pairs, a_specs, b_specs):
        in_specs += [sa, sb]
        args += [a, b]
    sem = ("parallel",) * (len(grid) - (1 if nk > 1 else 0)) + (("arbitrary",) if nk > 1 else ())
    return pl.pallas_call(
        body, name=name, grid=grid, in_specs=in_specs, out_specs=out_spec, out_shape=out_shape,
        scratch_shapes=[pltpu.VMEM(acc_shape, F32)] if nk > 1 else [],
        compiler_params=_cp(sem),
    )(*args)


NN = (((1,), (0,)), ((), ()))
NT = (((1,), (1,)), ((), ()))
TN = (((0,), (0,)), ((), ()))
TM = 1024


def _fwd_colsharded(name, x, w_stk):
    return _mm(name, [(x, w_stk)], NN, (N_CHIPS, SEQ // TM),
               [pl.BlockSpec((TM, D_MODEL), lambda j, i: (i, 0))],
               [pl.BlockSpec((None, D_MODEL, FF_SHARD), lambda j, i: (j, 0, 0))],
               pl.BlockSpec((TM, FF_SHARD), lambda j, i: (i, j)),
               jax.ShapeDtypeStruct((SEQ, D_FF), F32))


def _fwd_rowsharded(name, a, w_stk, shard):
    tn = 1024
    return _mm(name, [(a, w_stk)], NN, (SEQ // TM, D_MODEL // tn, N_CHIPS),
               [pl.BlockSpec((TM, shard), lambda i, j, k: (i, k))],
               [pl.BlockSpec((None, shard, tn), lambda i, j, k: (k, 0, j))],
               pl.BlockSpec((TM, tn), lambda i, j, k: (i, j)),
               jax.ShapeDtypeStruct((SEQ, D_MODEL), F32), nk=N_CHIPS, acc_shape=(TM, tn))


def _bwd_act_colsharded(name, pairs):
    tn = 1024
    n = len(pairs)
    return _mm(name, pairs, NT, (SEQ // TM, D_MODEL // tn, N_CHIPS),
               [pl.BlockSpec((TM, FF_SHARD), lambda i, j, k: (i, k))] * n,
               [pl.BlockSpec((None, tn, FF_SHARD), lambda i, j, k: (k, j, 0))] * n,
               pl.BlockSpec((TM, tn), lambda i, j, k: (i, j)),
               jax.ShapeDtypeStruct((SEQ, D_MODEL), F32), nk=N_CHIPS, acc_shape=(TM, tn))


def _bwd_act_rowsharded(name, dy, w_stk, shard):
    return _mm(name, [(dy, w_stk)], NT, (N_CHIPS, SEQ // TM),
               [pl.BlockSpec((TM, D_MODEL), lambda j, i: (i, 0))],
               [pl.BlockSpec((None, shard, D_MODEL), lambda j, i: (j, 0, 0))],
               pl.BlockSpec((TM, shard), lambda j, i: (i, j)),
               jax.ShapeDtypeStruct((SEQ, N_CHIPS * shard), F32))


def _bwd_w_colsharded(name, x, dy):
    tm = 512
    return _mm(name, [(x, dy)], TN, (N_CHIPS, D_MODEL // tm),
               [pl.BlockSpec((SEQ, tm), lambda j, i: (0, i))],
               [pl.BlockSpec((SEQ, FF_SHARD), lambda j, i: (0, j))],
               pl.BlockSpec((None, tm, FF_SHARD), lambda j, i: (j, i, 0)),
               jax.ShapeDtypeStruct((N_CHIPS, D_MODEL, FF_SHARD), F32))


def _bwd_w_rowsharded(name, a, dy, shard):
    tn = 1024
    return _mm(name, [(a, dy)], TN, (N_CHIPS, D_MODEL // tn),
               [pl.BlockSpec((SEQ, shard), lambda j, i: (0, j))],
               [pl.BlockSpec((SEQ, tn), lambda j, i: (0, i))],
               pl.BlockSpec((None, shard, tn), lambda j, i: (j, 0, i)),
               jax.ShapeDtypeStruct((N_CHIPS, shard, D_MODEL), F32))


TR = 256


def _ln_fwd(name, x, y, g, b):
    def body(x_ref, y_ref, g_ref, b_ref, o_ref):
        z = ALPHA * x_ref[...] + y_ref[...]
        mu = jnp.mean(z, -1, keepdims=True)
        zc = z - mu
        var = jnp.mean(zc * zc, -1, keepdims=True)
        o_ref[...] = zc * lax.rsqrt(var + LN_EPS) * g_ref[...] + b_ref[...]

    row = pl.BlockSpec((TR, D_MODEL), lambda i: (i, 0))
    vec = pl.BlockSpec((1, D_MODEL), lambda i: (0, 0))
    return pl.pallas_call(body, name=name, grid=(SEQ // TR,), in_specs=[row, row, vec, vec], out_specs=row,
                          out_shape=jax.ShapeDtypeStruct((SEQ, D_MODEL), F32),
                          compiler_params=_cp(("parallel",)))(x, y, g.reshape(1, -1), b.reshape(1, -1))


def _ln_bwd(name, x, y, g, d_res, d_path, run_after=None):
    has_res = d_res is not None
    n_in = 4 + has_res + (run_after is not None)

    def body(*refs):
        dz_ref, dg_ref, db_ref = refs[n_in:]
        if has_res:
            x_ref, y_ref, g_ref, r_ref, p_ref = refs[:5]
            dout = ALPHA * r_ref[...] + p_ref[...]
        else:
            x_ref, y_ref, g_ref, p_ref = refs[:4]
            dout = p_ref[...]
        z = ALPHA * x_ref[...] + y_ref[...]
        mu = jnp.mean(z, -1, keepdims=True)
        zc = z - mu
        rstd = lax.rsqrt(jnp.mean(zc * zc, -1, keepdims=True) + LN_EPS)
        zh = zc * rstd
        dzh = dout * g_ref[...]
        dz_ref[...] = rstd * (dzh - jnp.mean(dzh, -1, keepdims=True) - zh * jnp.mean(dzh * zh, -1, keepdims=True))
        pg = jnp.sum(dout * zh, 0, keepdims=True)
        pb = jnp.sum(dout, 0, keepdims=True)

        @pl.when(pl.program_id(0) == 0)
        def _():
            dg_ref[...] = pg
            db_ref[...] = pb

        @pl.when(pl.program_id(0) > 0)
        def _():
            dg_ref[...] += pg
            db_ref[...] += pb

    row = pl.BlockSpec((TR, D_MODEL), lambda i: (i, 0))
    vec = pl.BlockSpec((1, D_MODEL), lambda i: (0, 0))
    args = [x, y, g.reshape(1, -1)] + ([d_res] if has_res else []) + [d_path]
    in_specs = [row, row, vec] + ([row] if has_res else []) + [row]
    if run_after is not None:
        args.append(run_after)
        in_specs.append(pl.BlockSpec(memory_space=pl.ANY))
    vshape = jax.ShapeDtypeStruct((1, D_MODEL), F32)
    return pl.pallas_call(body, name=name, grid=(SEQ // TR,), in_specs=in_specs, out_specs=[row, vec, vec],
                          out_shape=[jax.ShapeDtypeStruct((SEQ, D_MODEL), F32), vshape, vshape],
                          compiler_params=_cp(("arbitrary",)))(*args)


def _loss_head(y, target):
    def body(y_ref, t_ref, dy_ref, l_ref):
        e = y_ref[...] - t_ref[...]
        dy_ref[...] = e * (1.0 / D_MODEL)
        part = jnp.full((8, BLK), 0.5 / D_MODEL * jnp.sum(e * e), F32)

        @pl.when(pl.program_id(0) == 0)
        def _():
            l_ref[...] = part

        @pl.when(pl.program_id(0) > 0)
        def _():
            l_ref[...] += part

    row = pl.BlockSpec((TR, D_MODEL), lambda i: (i, 0))
    return pl.pallas_call(body, name="loss_head", grid=(SEQ // TR,), in_specs=[row, row],
                          out_specs=[row, pl.BlockSpec((8, BLK), lambda i: (0, 0))],
                          out_shape=[jax.ShapeDtypeStruct((SEQ, D_MODEL), F32), jax.ShapeDtypeStruct((8, BLK), F32)],
                          compiler_params=_cp(("arbitrary",)))(y, target)


def _axpy(name, a, b):
    def body(a_ref, b_ref, o_ref):
        o_ref[...] = ALPHA * a_ref[...] + b_ref[...]

    row = pl.BlockSpec((TR, D_MODEL), lambda i: (i, 0))
    return pl.pallas_call(body, name=name, grid=(SEQ // TR,), in_specs=[row, row], out_specs=row,
                          out_shape=jax.ShapeDtypeStruct((SEQ, D_MODEL), F32),
                          compiler_params=_cp(("parallel",)))(a, b)


TC = 512


def _shift_down(x, s, rows):
    if s == 0:
        return x
    return jnp.where(rows >= s, pltpu.roll(x, s, axis=0), 0.0)


def _shift_up(x, s, rows):
    if s == 0:
        return x
    return jnp.where(rows < SEQ - s, pltpu.roll(x, SEQ - s, axis=0), 0.0)


def _conv_gate_fwd(name, g, u, cw, cb):
    def body(g_ref, u_ref, w_ref, b_ref, h_ref):
        gg = g_ref[...]
        rows = lax.broadcasted_iota(jnp.int32, gg.shape, 0)
        gc = b_ref[...] + w_ref[2:3, :] * gg
        gc = gc + w_ref[1:2, :] * _shift_down(gg, 1, rows)
        gc = gc + w_ref[0:1, :] * _shift_down(gg, 2, rows)
        h_ref[...] = (gc * _sigmoid(gc) * u_ref[...]).astype(BF16)

    col = pl.BlockSpec((SEQ, TC), lambda j: (0, j))
    return pl.pallas_call(body, name=name, grid=(D_FF // TC,),
                          in_specs=[col, col, pl.BlockSpec((CONV_WIDTH, TC), lambda j: (0, j)),
                                    pl.BlockSpec((1, TC), lambda j: (0, j))],
                          out_specs=col, out_shape=jax.ShapeDtypeStruct((SEQ, D_FF), BF16),
                          compiler_params=_cp(("parallel",)))(g, u, cw, cb.reshape(1, -1))


def _conv_gate_bwd(name, g, u, cw, cb, dh, run_after=None):
    def body(g_ref, u_ref, w_ref, b_ref, dh_ref, *rest):
        dg_ref, du_ref, dw_ref, db_ref = rest[-4:]
        gg = g_ref[...]
        rows = lax.broadcasted_iota(jnp.int32, gg.shape, 0)
        g1 = _shift_down(gg, 1, rows)
        g2 = _shift_down(gg, 2, rows)
        gc = b_ref[...] + w_ref[2:3, :] * gg + w_ref[1:2, :] * g1 + w_ref[0:1, :] * g2
        sg = _sigmoid(gc)
        act = gc * sg
        dh = dh_ref[...]
        du_ref[...] = dh * act
        dgc = dh * u_ref[...] * (sg * (1.0 + gc * (1.0 - sg)))
        db_ref[...] = jnp.sum(dgc, 0, keepdims=True)
        dw_ref[2:3, :] = jnp.sum(dgc * gg, 0, keepdims=True)
        dw_ref[1:2, :] = jnp.sum(dgc * g1, 0, keepdims=True)
        dw_ref[0:1, :] = jnp.sum(dgc * g2, 0, keepdims=True)
        dg_ref[...] = (w_ref[2:3, :] * dgc + w_ref[1:2, :] * _shift_up(dgc, 1, rows)
                       + w_ref[0:1, :] * _shift_up(dgc, 2, rows))

    col = pl.BlockSpec((SEQ, TC), lambda j: (0, j))
    w3 = pl.BlockSpec((CONV_WIDTH, TC), lambda j: (0, j))
    w1 = pl.BlockSpec((1, TC), lambda j: (0, j))
    big = jax.ShapeDtypeStruct((SEQ, D_FF), F32)
    extra = [] if run_after is None else [run_after]
    return pl.pallas_call(body, name=name, grid=(D_FF // TC,),
                          in_specs=[col, col, w3, w1, col] + [pl.BlockSpec(memory_space=pl.ANY)] * len(extra),
                          out_specs=[col, col, w3, w1],
                          out_shape=[big, big, jax.ShapeDtypeStruct((CONV_WIDTH, D_FF), F32),
                                     jax.ShapeDtypeStruct((1, D_FF), F32)],
                          compiler_params=_cp(("parallel",)))(g, u, cw, cb.reshape(1, -1), dh, *extra)


def _lbs_of(logits, layer):
    m = jnp.max(logits, 0, keepdims=True)
    e = jnp.exp(logits - m)
    p = e / jnp.sum(e, 0, keepdims=True)
    lb = jnp.zeros((1, BLK), F32)
    for r in range(1, layer + 1):
        lb = lb + p[r:r + 1, :]
    return lb, p


def _dlogits_of(p, dlb, layer):
    rows = lax.broadcasted_iota(jnp.int32, p.shape, 0)
    dp = jnp.where((rows >= 1) & (rows <= layer), dlb, 0.0)
    return p * (dp - jnp.sum(p * dp, 0, keepdims=True))


SROWS = SLAB * A_CHUNK
N_SLAB = N_CHUNK // SLAB


def _chunk_prefix(x, rowi):
    for s in (1, 2, 4, 8):
        x = x + jnp.where(rowi >= s, pltpu.roll(x, s, axis=0), 0.0)
    return x


def _chunk_suffix(x, rowi):
    for s in (1, 2, 4, 8):
        x = x + jnp.where(rowi < A_CHUNK - s, pltpu.roll(x, SROWS - s, axis=0), 0.0)
    return x


def _c3(x):
    return x.reshape(SLAB, A_CHUNK, BLK)


def _c2(x):
    return x.reshape(SROWS, BLK)


def _split(x):
    top = lax.bitcast_convert_type(lax.bitcast_convert_type(x, jnp.uint32) & jnp.uint32(0xFFFF0000), F32)
    return top.astype(BF16), (x - top).astype(BF16)


def _lane_sum_b(x2, ones):
    hi, lo = _split(x2)
    return jnp.dot(hi, ones, preferred_element_type=F32) + jnp.dot(lo, ones, preferred_element_type=F32)


def _bmm(eq, a, b):
    ah, al = _split(a)
    bh, bl = _split(b)

    def mm(u, v):
        return jnp.einsum(eq, u, v, preferred_element_type=F32)

    return mm(ah, bh) + (mm(ah, bl) + mm(al, bh))


def _slab_rows(s):
    return pl.ds(s * SROWS, SROWS)


def _hgrn_prep(q, f, lb):
    rowi = lax.broadcasted_iota(jnp.int32, (SROWS, BLK), 0) & (A_CHUNK - 1)
    sq = _sigmoid(q)
    qc = q * sq
    sf = _sigmoid(f)
    fg = lb + (1.0 - lb) * sf
    kc = 1.0 - fg
    b = _chunk_prefix(jnp.log(fg), rowi)
    b3 = _c3(b)
    blast = b3[:, A_CHUNK - 1:A_CHUNK, :]
    eb = jnp.exp(b)
    ekb = _c2(jnp.exp(blast - b3))
    dec = jnp.exp(blast.reshape(SLAB, BLK))
    return rowi, sq, qc, sf, fg, kc, b, eb, ekb, dec


def _hgrn_slab_states(s, carry, v, ke, dec, dec_ref, u_ref, st_ref):
    dec_ref[pl.ds(s * SLAB, SLAB), :] = dec
    u_ref[...] = _bmm('ncv,nck->nvk', _c3(v), _c3(ke))

    def step(j, c):
        st_ref[j] = c
        return dec_ref[pl.ds(s * SLAB + j, 1), :] * c + u_ref[j]

    return lax.fori_loop(0, SLAB, step, carry)


def _hgrn_fwd(name, proj, lb_logits, nw, layer):
    def body(q_ref, f_ref, i_ref, g_ref, lg_ref, nw_ref, out_ref, raw_ref, dec_ref, u_ref, st_ref):
        lb, _ = _lbs_of(lg_ref[...], layer)
        ones = jnp.ones((BLK, BLK), BF16)
        carry = jnp.zeros((BLK, BLK), F32)
        for s in range(N_SLAB):
            rows = _slab_rows(s)
            v = i_ref[rows, :]
            rowi, sq, qc, sf, fg, kc, b, eb, ekb, dec = _hgrn_prep(q_ref[rows, :], f_ref[rows, :], lb)
            carry = _hgrn_slab_states(s, carry, v, kc * ekb, dec, dec_ref, u_ref, st_ref)
            o = _c2(_bmm('nck,nvk->ncv', _c3(qc * eb), st_ref[...]))
            qc3, kc3, b3, v3, row3 = _c3(qc), _c3(kc), _c3(b), _c3(v), _c3(rowi)
            for j in range(A_CHUNK):
                dj = jnp.exp(jnp.where(row3 >= j, b3 - b3[:, j:j + 1, :], -jnp.inf))
                a = _lane_sum_b(_c2(qc3 * dj * kc3[:, j:j + 1, :]), ones)
                o = o + a * _c2(jnp.broadcast_to(v3[:, j:j + 1, :], v3.shape))
            raw_ref[rows, :] = o
            r = lax.rsqrt(jnp.mean(o * o, -1, keepdims=True) + LN_EPS)
            gg = g_ref[rows, :]
            out_ref[rows, :] = o * r * nw_ref[...] * (gg * _sigmoid(gg))

    def colblk(c0):
        return pl.BlockSpec((SEQ, BLK), lambda h: (0, c0 + h))

    big = jax.ShapeDtypeStruct((SEQ, A_HEADS * BLK), F32)
    return pl.pallas_call(
        body, name=name, grid=(A_HEADS,),
        in_specs=[colblk(QA0), colblk(FA0), colblk(IA0), colblk(GA0),
                  pl.BlockSpec((DEPTH, BLK), lambda h: (0, h)), pl.BlockSpec((1, BLK), lambda h: (0, 0))],
        out_specs=[colblk(0), colblk(0)], out_shape=[big, big],
        scratch_shapes=[pltpu.VMEM((N_CHUNK, BLK), F32), pltpu.VMEM((SLAB, BLK, BLK), F32),
                        pltpu.VMEM((SLAB, BLK, BLK), F32)],
        compiler_params=_cp(("parallel",)))(proj, proj, proj, proj, lb_logits, nw.reshape(1, -1))


def _hgrn_bwd(name, proj, raw, dmix, lb_logits, nw, layer, run_after=None):
    extra = [] if run_after is None else [run_after]

    def body(q_ref, f_ref, i_ref, g_ref, raw_ref, do_ref, lg_ref, nw_ref, *rest):
        (dq_ref, df_ref, di_ref, dg_ref, dnw_ref, dlg_ref,
         dec_ref, u_ref, st_ref, h_ref, dbs_ref, dkc_ref, tot_ref) = rest[-13:]
        lb, p = _lbs_of(lg_ref[...], layer)
        ones = jnp.ones((BLK, BLK), BF16)
        nwv = nw_ref[...]

        carry = jnp.zeros((BLK, BLK), F32)
        for s in range(N_SLAB):
            rows = _slab_rows(s)
            rowi, sq, qc, sf, fg, kc, b, eb, ekb, dec = _hgrn_prep(q_ref[rows, :], f_ref[rows, :], lb)
            carry = _hgrn_slab_states(s, carry, i_ref[rows, :], kc * ekb, dec, dec_ref, u_ref,
                                      st_ref.at[pl.ds(s * SLAB, SLAB)])

        carry = jnp.zeros((BLK, BLK), F32)
        dnw = jnp.zeros((1, BLK), F32)
        for s in reversed(range(N_SLAB)):
            rows = _slab_rows(s)
            q, v = q_ref[rows, :], i_ref[rows, :]
            rowi, sq, qc, sf, fg, kc, b, eb, ekb, dec = _hgrn_prep(q, f_ref[rows, :], lb)
            ke = kc * ekb
            qe = qc * eb

            o = raw_ref[rows, :]
            gg = g_ref[rows, :]
            sgg = _sigmoid(gg)
            dout = do_ref[rows, :]
            r = lax.rsqrt(jnp.mean(o * o, -1, keepdims=True) + LN_EPS)
            oh = o * r
            dg_ref[rows, :] = dout * oh * nwv * (sgg * (1.0 + gg * (1.0 - sgg)))
            dn = dout * (gg * sgg)
            dnw = dnw + jnp.sum(dn * oh, 0, keepdims=True)
            doh = dn * nwv
            do = r * (doh - oh * jnp.mean(doh * oh, -1, keepdims=True))
            do3, qe3, v3, ke3 = _c3(do), _c3(qe), _c3(v), _c3(ke)

            u_ref[...] = _bmm('ncv,nck->nvk', do3, qe3)

            def step(jj, c, s=s):
                j = SLAB - 1 - jj
                h_ref[j] = c
                return u_ref[j] + dec_ref[pl.ds(s * SLAB + j, 1), :] * c

            carry = lax.fori_loop(0, SLAB, step, carry)

            hh = h_ref[...]
            dqc = _c2(_bmm('ncv,nvk->nck', do3, st_ref[pl.ds(s * SLAB, SLAB)])) * eb
            dkc = _c2(_bmm('ncv,nvk->nck', v3, hh)) * ekb
            dv = _c2(_bmm('nck,nvk->ncv', ke3, hh))

            qc3, kc3, b3, row3 = _c3(qc), _c3(kc), _c3(b), _c3(rowi)
            for j in range(A_CHUNK):
                dj = jnp.exp(jnp.where(row3 >= j, b3 - b3[:, j:j + 1, :], -jnp.inf))
                kj = kc3[:, j:j + 1, :]
                vj = jnp.broadcast_to(v3[:, j:j + 1, :], v3.shape)
                att = _c3(_lane_sum_b(_c2(qc3 * dj * kj), ones))
                datt = _c3(_lane_sum_b(_c2(do3 * vj), ones))
                md = dj * datt
                dqc = dqc + _c2(md * kj)
                sel = row3 == j
                dkc = dkc + _c2(jnp.where(sel, jnp.sum(md * qc3, 1, keepdims=True), 0.0))
                dv = dv + _c2(jnp.where(sel, jnp.sum(att * do3, 1, keepdims=True), 0.0))
            di_ref[rows, :] = dv
            dq_ref[rows, :] = dqc * (sq * (1.0 + q * (1.0 - sq)))

            dbs = _chunk_suffix(qc * dqc - kc * dkc, rowi)
            dbs_ref[rows, :] = dbs
            dkc_ref[rows, :] = dkc
            tot_ref[pl.ds(s * SLAB, SLAB), :] = _c3(dbs)[:, 0:1, :].reshape(SLAB, BLK)
        dnw_ref[...] = jnp.broadcast_to(dnw, (8, BLK))

        rn = lax.broadcasted_iota(jnp.int32, (N_CHUNK, N_CHUNK), 0)
        cn = lax.broadcasted_iota(jnp.int32, (N_CHUNK, N_CHUNK), 1)
        tot_ref[...] = jnp.dot((cn > rn).astype(F32), tot_ref[...], preferred_element_type=F32, precision=HI)
        dlb = jnp.zeros((1, BLK), F32)
        for s in range(N_SLAB):
            rows = _slab_rows(s)
            sf = _sigmoid(f_ref[rows, :])
            fg = lb + (1.0 - lb) * sf
            later = tot_ref[pl.ds(s * SLAB, SLAB), :]
            dlg = _c2(_c3(dbs_ref[rows, :]) + later[:, None, :])
            dfg = dlg / fg - dkc_ref[rows, :]
            df_ref[rows, :] = dfg * (1.0 - lb) * sf * (1.0 - sf)
            dlb = dlb + jnp.sum(dfg * (1.0 - sf), 0, keepdims=True)
        dlg_ref[...] = _dlogits_of(p, dlb, layer)

    def colblk(c0):
        return pl.BlockSpec((SEQ, BLK), lambda h: (0, c0 + h))

    big = jax.ShapeDtypeStruct((SEQ, A_HEADS * BLK), F32)
    return pl.pallas_call(
        body, name=name, grid=(A_HEADS,),
        in_specs=[colblk(QA0), colblk(FA0), colblk(IA0), colblk(GA0), colblk(0), colblk(0),
                  pl.BlockSpec((DEPTH, BLK), lambda h: (0, h)), pl.BlockSpec((1, BLK), lambda h: (0, 0))]
        + [pl.BlockSpec(memory_space=pl.ANY)] * len(extra),
        out_specs=[colblk(0), colblk(0), colblk(0), colblk(0),
                   pl.BlockSpec((8, BLK), lambda h: (h, 0)), pl.BlockSpec((DEPTH, BLK), lambda h: (0, h))],
        out_shape=[big, big, big, big, jax.ShapeDtypeStruct((A_HEADS * 8, BLK), F32),
                   jax.ShapeDtypeStruct((DEPTH, A_HEADS * BLK), F32)],
        scratch_shapes=[pltpu.VMEM((N_CHUNK, BLK), F32), pltpu.VMEM((SLAB, BLK, BLK), F32),
                        pltpu.VMEM((N_CHUNK, BLK, BLK), F32), pltpu.VMEM((SLAB, BLK, BLK), F32),
                        pltpu.VMEM((SEQ, BLK), F32), pltpu.VMEM((SEQ, BLK), F32), pltpu.VMEM((N_CHUNK, BLK), F32)],
        compiler_params=_cp(("parallel",)))(proj, proj, proj, proj, raw, dmix, lb_logits, nw.reshape(1, -1), *extra)


SCALE = HEAD_DIM ** -0.5


def _rope_tables():
    half = ROPE_DIM // 2
    inv = ROPE_THETA ** (-jnp.arange(0, ROPE_DIM, 2, dtype=F32) / ROPE_DIM)
    ang = jnp.arange(SEQ, dtype=F32)[:, None] * inv[None, :]
    cos, sin = jnp.cos(ang), jnp.sin(ang)
    pad = jnp.zeros((SEQ, HEAD_DIM - ROPE_DIM), F32)
    zero = jnp.zeros((SEQ, half), F32)
    c = jnp.concatenate([cos, cos, pad + 1.0], 1)
    s_lo = jnp.concatenate([zero, sin, pad], 1)
    s_hi = jnp.concatenate([-sin, zero, pad], 1)
    return c, s_lo, s_hi


def _rope(x, c, s_lo, s_hi):
    half = ROPE_DIM // 2
    return x * c + pltpu.roll(x, half, axis=1) * s_lo + pltpu.roll(x, HEAD_DIM - half, axis=1) * s_hi


def _unrope(dy, c, s_lo, s_hi):
    half = ROPE_DIM // 2
    return dy * c + pltpu.roll(dy * s_lo, HEAD_DIM - half, axis=1) + pltpu.roll(dy * s_hi, half, axis=1)


def _rows(start, size, stride):
    return pl.ds(start, size) if stride == 1 else pl.ds(start, size, stride=stride)


def _band_blocks(patterns):
    out = []
    for p, (max_lag, dil) in enumerate(patterns):
        nb = SEQ // dil // BLK
        for r in range(dil):
            for n in range(nb):
                lo = max(n - 1, 0)
                kn = (n - lo + 1) * BLK
                out.append((p, _rows(r + n * BLK * dil, BLK, dil), _rows(r + lo * BLK * dil, kn, dil), kn,
                            (n - lo) * BLK, max_lag))
    return out


def _band_valid(kn, off, max_lag):
    lag = off + lax.broadcasted_iota(jnp.int32, (BLK, kn), 0) - lax.broadcasted_iota(jnp.int32, (BLK, kn), 1)
    return (lag >= 0) & (lag <= max_lag)


def _attn_fwd(name, proj, tables, sink_b, *, n_heads, rep, q0, k0, v0, patterns):
    n_pat = len(patterns)
    blocks = _band_blocks(patterns)
    has_sink = sink_b is not None

    def body(*refs):
        if has_sink:
            q_ref, k_ref, v_ref, c_ref, sl_ref, sh_ref, sink_ref, o_ref, lse_ref, qr, kr, op = refs
            sk = sink_ref[0:1, 0:1]
        else:
            q_ref, k_ref, v_ref, c_ref, sl_ref, sh_ref, o_ref, lse_ref, qr, kr, op = refs
        c, s_lo, s_hi = c_ref[...], sl_ref[...], sh_ref[...]
        qr[...] = _rope(q_ref[...], c, s_lo, s_hi)
        kr[...] = _rope(k_ref[...], c, s_lo, s_hi)
        for p, qrows, krows, kn, off, max_lag in blocks:
            qb = qr[qrows, :].astype(BF16)
            kb = kr[krows, :].astype(BF16)
            vb = v_ref[krows, :].astype(BF16)
            s = lax.dot_general(qb, kb, NT, preferred_element_type=F32) * SCALE
            s = jnp.where(_band_valid(kn, off, max_lag), s, -jnp.inf)
            m = jnp.max(s, -1, keepdims=True)
            if has_sink:
                m = jnp.maximum(m, sk)
            e = jnp.exp(s - m)
            den = jnp.sum(e, -1, keepdims=True)
            if has_sink:
                den = den + jnp.exp(sk - m)
            o = jnp.dot(e.astype(BF16), vb, preferred_element_type=F32) / den
            op.at[p][qrows, :] = o
            lse_ref.at[p][qrows, :] = jnp.broadcast_to(m + jnp.log(den), (BLK, BLK))
        if n_pat == 1:
            o_ref[...] = op[0]
        else:
            ls = [lse_ref[p] for p in range(n_pat)]
            m = functools.reduce(jnp.maximum, ls)
            es = [jnp.exp(l - m) for l in ls]
            tot = functools.reduce(jnp.add, es)
            acc = None
            for p in range(n_pat):
                t = (es[p] / tot) * op[p]
                acc = t if acc is None else acc + t
            o_ref[...] = acc

    def colblk(fn):
        return pl.BlockSpec((SEQ, BLK), fn)

    tab = pl.BlockSpec((SEQ, BLK), lambda h: (0, 0))
    in_specs = [colblk(lambda h: (0, q0 + h)), colblk(lambda h: (0, k0 + h // rep)), colblk(lambda h: (0, v0 + h // rep)),
                tab, tab, tab]
    args = [proj, proj, proj, *tables]
    if has_sink:
        in_specs.append(pl.BlockSpec((None, 8, BLK), lambda h: (h, 0, 0)))
        args.append(sink_b)
    return pl.pallas_call(
        body, name=name, grid=(n_heads,), in_specs=in_specs,
        out_specs=[colblk(lambda h: (0, h)), pl.BlockSpec((None, n_pat, SEQ, BLK), lambda h: (h, 0, 0, 0))],
        out_shape=[jax.ShapeDtypeStruct((SEQ, n_heads * BLK), F32),
                   jax.ShapeDtypeStruct((n_heads, n_pat, SEQ, BLK), F32)],
        scratch_shapes=[pltpu.VMEM((SEQ, BLK), F32), pltpu.VMEM((SEQ, BLK), F32), pltpu.VMEM((n_pat, SEQ, BLK), F32)],
        compiler_params=_cp(("parallel",)))(*args)


def _attn_bwd(name, proj, mixed, dmix, lse, tables, sink_b, *, n_kv, rep, q0, k0, v0, m0, patterns):
    n_pat = len(patterns)
    n_heads = n_kv * rep
    blocks = _band_blocks(patterns)
    has_sink = sink_b is not None

    def body(*refs):
        if has_sink:
            (q_ref, k_ref, v_ref, o_ref, do_ref, lse_ref, c_ref, sl_ref, sh_ref, sink_ref,
             dq_ref, dk_ref, dv_ref, dsk_ref, qr, kr, dqa, dka, dva, dd, ww) = refs
        else:
            (q_ref, k_ref, v_ref, o_ref, do_ref, lse_ref, c_ref, sl_ref, sh_ref,
             dq_ref, dk_ref, dv_ref, dsk_ref, qr, kr, dqa, dka, dva, dd, ww) = refs
        j = pl.program_id(1)
        c, s_lo, s_hi = c_ref[...], sl_ref[...], sh_ref[...]
        qr[...] = _rope(q_ref[...], c, s_lo, s_hi)
        kr[...] = _rope(k_ref[...], c, s_lo, s_hi)
        dcol = jnp.sum(do_ref[...] * o_ref[...], -1, keepdims=True)
        dd[...] = jnp.broadcast_to(dcol, (SEQ, BLK))
        if n_pat == 1:
            ww[0] = jnp.ones((SEQ, BLK), F32)
        else:
            ls = [lse_ref[p] for p in range(n_pat)]
            m = functools.reduce(jnp.maximum, ls)
            es = [jnp.exp(l - m) for l in ls]
            tot = functools.reduce(jnp.add, es)
            for p in range(n_pat):
                ww[p] = es[p] / tot
        dqa[...] = jnp.zeros((SEQ, BLK), F32)

        @pl.when(j == 0)
        def _():
            dka[...] = jnp.zeros((SEQ, BLK), F32)
            dva[...] = jnp.zeros((SEQ, BLK), F32)

        for p, qrows, krows, kn, off, max_lag in blocks:
            qb = qr[qrows, :].astype(BF16)
            kb = kr[krows, :].astype(BF16)
            vb = v_ref[krows, :].astype(BF16)
            dob = do_ref[qrows, :].astype(BF16)
            lcol = lse_ref.at[p][qrows, :][:, 0:1]
            wcol = ww.at[p][qrows, :][:, 0:1]
            dcb = dd[qrows, :][:, 0:1]
            s = lax.dot_general(qb, kb, NT, preferred_element_type=F32) * SCALE
            a = jnp.where(_band_valid(kn, off, max_lag), jnp.exp(s - lcol), 0.0) * wcol
            dp = lax.dot_general(dob, vb, NT, preferred_element_type=F32)
            ds = (a * (dp - dcb) * SCALE).astype(BF16)
            dqa[qrows, :] += jnp.dot(ds, kb, preferred_element_type=F32)
            dka[krows, :] += lax.dot_general(ds, qb, TN, preferred_element_type=F32)
            dva[krows, :] += lax.dot_general(a.astype(BF16), dob, TN, preferred_element_type=F32)

        if has_sink:
            sk = sink_ref[0:1, 0:1]
            ps = jnp.exp(sk - lse_ref[0][:, 0:1])
            dsk_ref[...] = jnp.full((8, BLK), -jnp.sum(ps * dcol), F32)
        else:
            dsk_ref[...] = jnp.zeros((8, BLK), F32)
        dq_ref[...] = _unrope(dqa[...], c, s_lo, s_hi)

        @pl.when(j == rep - 1)
        def _():
            dk_ref[...] = _unrope(dka[...], c, s_lo, s_hi)
            dv_ref[...] = dva[...]

    def colblk(fn):
        return pl.BlockSpec((SEQ, BLK), fn)

    tab = pl.BlockSpec((SEQ, BLK), lambda g, j: (0, 0))
    in_specs = [colblk(lambda g, j: (0, q0 + g * rep + j)), colblk(lambda g, j: (0, k0 + g)), colblk(lambda g, j: (0, v0 + g)),
                colblk(lambda g, j: (0, m0 + g * rep + j)), colblk(lambda g, j: (0, m0 + g * rep + j)),
                pl.BlockSpec((None, n_pat, SEQ, BLK), lambda g, j: (g * rep + j, 0, 0, 0)), tab, tab, tab]
    args = [proj, proj, proj, mixed, dmix, lse, *tables]
    if has_sink:
        in_specs.append(pl.BlockSpec((None, 8, BLK), lambda g, j: (g * rep + j, 0, 0)))
        args.append(sink_b)
    acc = pltpu.VMEM((SEQ, BLK), F32)
    return pl.pallas_call(
        body, name=name, grid=(n_kv, rep), in_specs=in_specs,
        out_specs=[colblk(lambda g, j: (0, g * rep + j)), colblk(lambda g, j: (0, g)), colblk(lambda g, j: (0, g)),
                   pl.BlockSpec((None, 8, BLK), lambda g, j: (g * rep + j, 0, 0))],
        out_shape=[jax.ShapeDtypeStruct((SEQ, n_heads * BLK), F32), jax.ShapeDtypeStruct((SEQ, n_kv * BLK), F32),
                   jax.ShapeDtypeStruct((SEQ, n_kv * BLK), F32), jax.ShapeDtypeStruct((n_heads, 8, BLK), F32)],
        scratch_shapes=[acc, acc, acc, acc, acc, acc, pltpu.VMEM((n_pat, SEQ, BLK), F32)],
        compiler_params=_cp(("parallel", "arbitrary")))(*args)


B_PATTERNS = tuple((w // d, d) for w, d in DILATED_PATTERNS)
C_PATTERNS = ((C_WINDOW - 1, 1),)


ANY = pl.BlockSpec(memory_space=pl.ANY)
CHIP_MASKS = ((1, 0), (0, 1), (1, 1))


def _coords():
    return lax.axis_index("x"), lax.axis_index("y"), lax.axis_index("c")


def _flip(v, m):
    return 1 - v if m else v


def _into_slot(name, w, layer, k_idx, dtype, run_after=None):
    _, rows, cols = w.shape
    tr = rows // 8 if rows % 64 == 0 else rows

    def body(k_ref, w_ref, *rest):
        rest[-1][...] = w_ref[...].astype(dtype)

    in_specs = [pl.BlockSpec((None, tr, cols), lambda i, k: (layer, i, 0))]
    args = [k_idx, w]
    if run_after is not None:
        in_specs.append(pl.BlockSpec(memory_space=pl.ANY))
        args.append(run_after)
    return pl.pallas_call(
        body, name=name,
        grid_spec=pltpu.PrefetchScalarGridSpec(
            num_scalar_prefetch=1, grid=(rows // tr,), in_specs=in_specs,
            out_specs=pl.BlockSpec((None, tr, cols), lambda i, k: (k[0], i, 0))),
        out_shape=jax.ShapeDtypeStruct((N_CHIPS, rows, cols), dtype),
        compiler_params=_cp(("parallel",)))(*args)


HBM_SPEC = pl.BlockSpec(memory_space=pltpu.HBM)
SEM_SPEC = pl.BlockSpec(memory_space=pltpu.SEMAPHORE)
TOKEN_SPEC = pl.BlockSpec(memory_space=pltpu.VMEM)
TOKEN_SHAPE = jax.ShapeDtypeStruct((8, BLK), F32)
DATAFLOW = pltpu.SideEffectType.DATAFLOW_SIDE_EFFECTING


def _hbm(a):
    return pltpu.with_memory_space_constraint(a, pltpu.HBM)


def _hbm_like(bufs):
    return [pltpu.HBM(b.shape, b.dtype) for b in bufs]


def _gather_start(name, stages):
    flat = [b for st in stages for b in st]
    n, ns = len(flat), len(stages)

    def body(*refs):
        ins = refs[:n]
        sems = refs[n:n + 2 * ns]
        token = refs[-1]
        x, y, c = _coords()
        k_me = 2 * x + y
        a = 0
        for s, st in enumerate(stages):
            for i in range(len(st)):
                mine = ins[a].at[k_me, c]
                for m, (mx, my) in enumerate(CHIP_MASKS):
                    pltpu.make_async_remote_copy(src_ref=mine, dst_ref=mine, send_sem=sems[2 * s].at[i * 3 + m],
                                                 recv_sem=sems[2 * s + 1].at[i * 3 + m],
                                                 device_id=(_flip(x, mx), _flip(y, my), c), device_id_type=MESH).start()
                a += 1
        token[...] = jnp.zeros_like(token)

    sem_shapes = []
    for st in stages:
        sem_shapes += [pltpu.SemaphoreType.DMA((3 * len(st),))] * 2
    out = pl.pallas_call(
        body, name=name, in_specs=[HBM_SPEC] * n,
        out_specs=tuple([SEM_SPEC] * (2 * ns) + [HBM_SPEC] * n + [TOKEN_SPEC]),
        out_shape=tuple(sem_shapes + _hbm_like(flat) + [TOKEN_SHAPE]),
        input_output_aliases={i: 2 * ns + i for i in range(n)},
        compiler_params=pltpu.CompilerParams(has_side_effects=DATAFLOW),
    )(*[_hbm(b) for b in flat])
    sems, bufs, token = out[:2 * ns], out[2 * ns:2 * ns + n], out[-1]
    res, a = [], 0
    for s, st in enumerate(stages):
        res.append((sems[2 * s], sems[2 * s + 1], list(bufs[a:a + len(st)])))
        a += len(st)
    return res, token


def _gather_forward(name, stage, after):
    ssem_in, rsem_in, bufs = stage
    n = len(bufs)

    def body(*refs):
        ins = refs[:n]
        s_in, r_in, _ = refs[n:n + 3]
        s_out, r_out = refs[n + 3:n + 5]
        token = refs[-1]
        x, y, c = _coords()
        for i in range(n):
            for m, (mx, my) in enumerate(CHIP_MASKS):
                kp = 2 * _flip(x, mx) + _flip(y, my)
                blk = ins[i].at[kp, c]
                got = pltpu.make_async_remote_copy(src_ref=blk, dst_ref=blk, send_sem=s_in.at[i * 3 + m],
                                                   recv_sem=r_in.at[i * 3 + m], device_id=(x, y, 1 - c), device_id_type=MESH)
                got.wait_send()
                got.wait_recv()
                pltpu.make_async_remote_copy(src_ref=blk, dst_ref=blk, send_sem=s_out.at[i * 3 + m],
                                             recv_sem=r_out.at[i * 3 + m], device_id=(x, y, 1 - c), device_id_type=MESH).start()
        token[...] = jnp.zeros_like(token)

    sem = pltpu.SemaphoreType.DMA((3 * n,))
    out = pl.pallas_call(
        body, name=name, in_specs=[HBM_SPEC] * n + [SEM_SPEC, SEM_SPEC, ANY],
        out_specs=tuple([SEM_SPEC, SEM_SPEC] + [HBM_SPEC] * n + [TOKEN_SPEC]),
        out_shape=tuple([sem, sem] + _hbm_like(bufs) + [TOKEN_SHAPE]),
        input_output_aliases={i: 2 + i for i in range(n)},
        compiler_params=pltpu.CompilerParams(has_side_effects=DATAFLOW),
    )(*bufs, ssem_in, rsem_in, after)
    return (out[0], out[1], list(out[2:2 + n])), out[-1]


def _gather_wait(name, stage, after):
    ssem, rsem, bufs = stage
    n = len(bufs)

    def body(*refs):
        ins = refs[:n]
        s_in, r_in, _ = refs[n:n + 3]
        x, y, c = _coords()
        for i in range(n):
            for m, (mx, my) in enumerate(CHIP_MASKS):
                kp = 2 * _flip(x, mx) + _flip(y, my)
                sent, got = ins[i].at[kp, c], ins[i].at[kp, 1 - c]
                cp = pltpu.make_async_remote_copy(src_ref=sent, dst_ref=got, send_sem=s_in.at[i * 3 + m],
                                                  recv_sem=r_in.at[i * 3 + m], device_id=(x, y, 1 - c), device_id_type=MESH)
                cp.wait_send()
                cp.wait_recv()

    out = pl.pallas_call(
        body, name=name, in_specs=[HBM_SPEC] * n + [SEM_SPEC, SEM_SPEC, ANY],
        out_specs=tuple([HBM_SPEC] * n), out_shape=tuple(_hbm_like(bufs)),
        input_output_aliases={i: i for i in range(n)},
        compiler_params=pltpu.CompilerParams(has_side_effects=DATAFLOW),
    )(*bufs, ssem, rsem, after)
    return list(out)


def _swap_start(name, grads):
    n = len(grads)

    def body(*refs):
        ins, lands = refs[:n], refs[n:2 * n]
        ssem, rsem = refs[2 * n:2 * n + 2]
        x, y, c = _coords()
        for a in range(n):
            for j in range(N_CHIPS):
                pltpu.make_async_remote_copy(src_ref=ins[a].at[j, 1 - c], dst_ref=lands[a].at[j],
                                             send_sem=ssem.at[a * N_CHIPS + j], recv_sem=rsem.at[a * N_CHIPS + j],
                                             device_id=(x, y, 1 - c), device_id_type=MESH).start()

    sem = pltpu.SemaphoreType.DMA((N_CHIPS * n,))
    land_shapes = [pltpu.HBM((N_CHIPS,) + g.shape[2:], g.dtype) for g in grads]
    out = pl.pallas_call(
        body, name=name, in_specs=[HBM_SPEC] * (2 * n),
        out_specs=tuple([SEM_SPEC, SEM_SPEC] + [HBM_SPEC] * (2 * n)),
        out_shape=tuple([sem, sem] + _hbm_like(grads) + land_shapes),
        input_output_aliases={i: 2 + i for i in range(2 * n)},
        compiler_params=pltpu.CompilerParams(has_side_effects=DATAFLOW),
    )(*[_hbm(g) for g in grads], *[_hbm(lax.empty((N_CHIPS,) + g.shape[2:], g.dtype)) for g in grads])
    return out[0], out[1], list(out[2:2 + n]), list(out[2 + n:])


def _swap_wait(name, started, after):
    ssem, rsem, grads, lands = started
    n = len(grads)

    def body(*refs):
        ins, lnd = refs[:n], refs[n:2 * n]
        s_in, r_in, _ = refs[2 * n:2 * n + 3]
        x, y, c = _coords()
        for a in range(n):
            for j in range(N_CHIPS):
                cp = pltpu.make_async_remote_copy(src_ref=ins[a].at[j, 1 - c], dst_ref=lnd[a].at[j],
                                                  send_sem=s_in.at[a * N_CHIPS + j], recv_sem=r_in.at[a * N_CHIPS + j],
                                                  device_id=(x, y, 1 - c), device_id_type=MESH)
                cp.wait_send()
                cp.wait_recv()

    out = pl.pallas_call(
        body, name=name, in_specs=[HBM_SPEC] * (2 * n) + [SEM_SPEC, SEM_SPEC, ANY],
        out_specs=tuple([HBM_SPEC] * (2 * n)), out_shape=tuple(_hbm_like(grads) + _hbm_like(lands)),
        input_output_aliases={i: i for i in range(2 * n)},
        compiler_params=pltpu.CompilerParams(has_side_effects=DATAFLOW),
    )(*grads, *lands, ssem, rsem, after)
    return list(out[:n]), list(out[n:])


def _scatter_start(name, parts):
    n = len(parts)

    def body(*refs):
        ins, lands = refs[:n], refs[n:2 * n]
        ssem, rsem = refs[2 * n:2 * n + 2]
        x, y, c = _coords()
        k_me = 2 * x + y
        for a in range(n):
            for m, (mx, my) in enumerate(CHIP_MASKS):
                px, py = _flip(x, mx), _flip(y, my)
                pltpu.make_async_remote_copy(src_ref=ins[a].at[2 * px + py], dst_ref=lands[a].at[k_me],
                                             send_sem=ssem.at[a * 3 + m], recv_sem=rsem.at[a * 3 + m],
                                             device_id=(px, py, c), device_id_type=MESH).start()

    sem = pltpu.SemaphoreType.DMA((3 * n,))
    out = pl.pallas_call(
        body, name=name, in_specs=[HBM_SPEC] * (2 * n),
        out_specs=tuple([SEM_SPEC, SEM_SPEC] + [HBM_SPEC] * (2 * n)),
        out_shape=tuple([sem, sem] + _hbm_like(parts) + _hbm_like(parts)),
        input_output_aliases={i: 2 + i for i in range(2 * n)},
        compiler_params=pltpu.CompilerParams(has_side_effects=DATAFLOW),
    )(*[_hbm(p) for p in parts], *[_hbm(lax.empty(p.shape, p.dtype)) for p in parts])
    return out[0], out[1], list(out[2:2 + n]), list(out[2 + n:])


def _scatter_wait(name, started, after):
    ssem, rsem, parts, lands = started
    n = len(parts)

    def body(*refs):
        ins, lnd = refs[:n], refs[n:2 * n]
        s_in, r_in, _ = refs[2 * n:2 * n + 3]
        x, y, c = _coords()
        k_me = 2 * x + y
        for a in range(n):
            for m, (mx, my) in enumerate(CHIP_MASKS):
                px, py = _flip(x, mx), _flip(y, my)
                cp = pltpu.make_async_remote_copy(src_ref=ins[a].at[2 * px + py], dst_ref=lnd[a].at[k_me],
                                                  send_sem=s_in.at[a * 3 + m], recv_sem=r_in.at[a * 3 + m],
                                                  device_id=(px, py, c), device_id_type=MESH)
                cp.wait_send()
                cp.wait_recv()

    out = pl.pallas_call(
        body, name=name, in_specs=[HBM_SPEC] * (2 * n) + [SEM_SPEC, SEM_SPEC, ANY],
        out_specs=tuple([HBM_SPEC] * (2 * n)), out_shape=tuple(_hbm_like(parts) + _hbm_like(lands)),
        input_output_aliases={i: i for i in range(2 * n)},
        compiler_params=pltpu.CompilerParams(has_side_effects=DATAFLOW),
    )(*parts, *lands, ssem, rsem, after)
    return list(out[:n]), list(out[n:])


def _pair_gather(name, bufs):
    n = len(bufs)

    def body(*refs):
        outs = refs[n:2 * n]
        ssem, rsem = refs[2 * n:]
        x, y, c = _coords()
        cps = []
        for a in range(n):
            mine = outs[a].at[c]
            cp = pltpu.make_async_remote_copy(src_ref=mine, dst_ref=mine, send_sem=ssem.at[a],
                                              recv_sem=rsem.at[a], device_id=(x, y, 1 - c), device_id_type=MESH)
            cp.start()
            cps.append(cp)
        for cp in cps:
            cp.wait()

    return pl.pallas_call(
        body, name=name, in_specs=[ANY] * n, out_specs=[ANY] * n,
        out_shape=[jax.ShapeDtypeStruct(b.shape, b.dtype) for b in bufs],
        input_output_aliases={a: a for a in range(n)},
        scratch_shapes=[pltpu.SemaphoreType.DMA((n,)), pltpu.SemaphoreType.DMA((n,))],
        compiler_params=pltpu.CompilerParams(has_side_effects=True),
    )(*bufs)


DEV_MASKS = tuple((mx, my, mc) for mx in (0, 1) for my in (0, 1) for mc in (0, 1) if (mx, my, mc) != (0, 0, 0))


def _gather_small(buf):
    def body(in_ref, out_ref, ssem, rsem, lsem):
        x, y, c = _coords()
        me = 4 * x + 2 * y + c
        cps = [pltpu.make_async_copy(in_ref, out_ref.at[me], lsem)]
        cps[0].start()
        for t, (mx, my, mc) in enumerate(DEV_MASKS):
            cp = pltpu.make_async_remote_copy(src_ref=in_ref, dst_ref=out_ref.at[me], send_sem=ssem.at[t],
                                              recv_sem=rsem.at[t], device_id=(_flip(x, mx), _flip(y, my), _flip(c, mc)),
                                              device_id_type=MESH)
            cp.start()
            cps.append(cp)
        for cp in cps:
            cp.wait()

    return pl.pallas_call(
        body, name="gather_small", in_specs=[ANY], out_specs=ANY,
        out_shape=jax.ShapeDtypeStruct((N_DEV,) + buf.shape, buf.dtype),
        scratch_shapes=[pltpu.SemaphoreType.DMA((N_DEV - 1,)), pltpu.SemaphoreType.DMA((N_DEV - 1,)),
                        pltpu.SemaphoreType.DMA(())],
        compiler_params=pltpu.CompilerParams(has_side_effects=True),
    )(buf)


def _row_tile(rows):
    return rows // 2 if rows % 16 == 0 else rows


def _pair_add(name, grad, got, c_idx):
    _, _, r2, cols = grad.shape
    tr = _row_tile(r2)

    def body(c_ref, a_ref, b_ref, o_ref):
        o_ref[...] = (a_ref[...] + b_ref[...]).astype(BF16)

    return pl.pallas_call(
        body, name=name,
        grid_spec=pltpu.PrefetchScalarGridSpec(
            num_scalar_prefetch=1, grid=(N_CHIPS, r2 // tr),
            in_specs=[pl.BlockSpec((None, None, tr, cols), lambda j, i, c: (j, c[0], i, 0)),
                      pl.BlockSpec((None, tr, cols), lambda j, i, c: (j, i, 0))],
            out_specs=pl.BlockSpec((None, tr, cols), lambda j, i, c: (j, i, 0))),
        out_shape=jax.ShapeDtypeStruct((N_CHIPS, r2, cols), BF16),
        compiler_params=_cp(("parallel", "parallel")))(c_idx, grad, got)


def _chip_add(name, part, got, kc_idx):
    _, r2, cols = got.shape
    tr = _row_tile(r2)

    def body(k_ref, p_ref, g1_ref, g2_ref, g3_ref, o_ref):
        acc = p_ref[...].astype(F32)
        for g_ref in (g1_ref, g2_ref, g3_ref):
            acc = acc + g_ref[...].astype(F32)
        o_ref[...] = acc

    def slot(d):
        return pl.BlockSpec((None, tr, cols), lambda i, k: ((k[0] + d) % N_CHIPS, i, 0))

    return pl.pallas_call(
        body, name=name,
        grid_spec=pltpu.PrefetchScalarGridSpec(
            num_scalar_prefetch=1, grid=(r2 // tr,),
            in_specs=[slot(0), slot(1), slot(2), slot(3)],
            out_specs=pl.BlockSpec((None, tr, cols), lambda i, k: (k[1], i, 0))),
        out_shape=jax.ShapeDtypeStruct((2, r2, cols), F32),
        compiler_params=_cp(("parallel",)))(kc_idx, part, got, got, got)


def _adam_math(w, g, m, v):
    m2 = ADAM_B1 * m + (1.0 - ADAM_B1) * g
    v2 = ADAM_B2 * v + (1.0 - ADAM_B2) * (g * g)
    m_hat = m2 / (1.0 - ADAM_B1 ** ADAM_STEP)
    v_hat = v2 / (1.0 - ADAM_B2 ** ADAM_STEP)
    delta = -ADAM_LR * (m_hat / (jnp.sqrt(v_hat) + ADAM_EPS) + ADAM_WD * w)
    return delta, m2, v2


def _adamw_matrix(name, w, g_layers, m, v):
    _, rows, cols = w.shape
    tr = rows // 16 if rows % 128 == 0 else rows // 8

    def body(w_ref, g0_ref, g1_ref, m_ref, v_ref, go_ref, d_ref, mo_ref, vo_ref):
        g = jnp.where(pl.program_id(0) == 0, g0_ref[...], g1_ref[...])
        go_ref[...] = g
        d_ref[...], mo_ref[...], vo_ref[...] = _adam_math(w_ref[...], g, m_ref[...], v_ref[...])

    lay = pl.BlockSpec((None, tr, cols), lambda l, i: (l, i, 0))
    flat = pl.BlockSpec((tr, cols), lambda l, i: (i, 0))
    shp = jax.ShapeDtypeStruct(w.shape, F32)
    return pl.pallas_call(body, name=name, grid=(DEPTH, rows // tr), in_specs=[lay, flat, flat, lay, lay],
                          out_specs=[lay, lay, lay, lay], out_shape=[shp, shp, shp, shp],
                          compiler_params=_cp(("parallel", "parallel")))(w, g_layers[0], g_layers[1], m, v)


def _sum_small(gathered):
    def body(g_ref, o_ref):
        acc = g_ref[0]
        for d in range(1, N_DEV):
            acc = acc + g_ref[d]
        o_ref[...] = acc

    return pl.pallas_call(body, name="sum_small", out_shape=jax.ShapeDtypeStruct(gathered.shape[1:], F32),
                          compiler_params=_cp())(gathered)


def _adamw_small(w, g, m, v):
    def body(w_ref, g_ref, m_ref, v_ref, d_ref, mo_ref, vo_ref):
        d_ref[...], mo_ref[...], vo_ref[...] = _adam_math(w_ref[...], g_ref[...], m_ref[...], v_ref[...])

    shp = jax.ShapeDtypeStruct(w.shape, F32)
    return pl.pallas_call(body, name="adamw_small", out_shape=[shp, shp, shp], compiler_params=_cp())(w, g, m, v)


def _pack(arrays, rows):
    flat = jnp.concatenate([a.reshape(-1) for a in arrays])
    return jnp.pad(flat, (0, rows * BLK - flat.shape[0])).reshape(rows, BLK)


def _unpack(buf, shapes):
    flat = buf.reshape(-1)
    out, pos = [], 0
    for s in shapes:
        n = math.prod(s)
        out.append(flat[pos:pos + n].reshape(s))
        pos += n
    return out


def _rows_for(shapes):
    n = sum(math.prod(s) for s in shapes)
    return -(-n // (8 * BLK)) * 8


def _rs_swap(tag, grads):
    return _swap_start(f"rs_swap_start{tag}", [g.reshape(N_CHIPS, 2, g.shape[1] // 2, g.shape[2]) for g in grads])


def _rs_scatter(tag, swapping, after, c_idx):
    split, got = _swap_wait(f"rs_swap_wait{tag}", swapping, after)
    parts = [_pair_add(f"rs_pair_add{tag}_{i}", s, r, c_idx) for i, (s, r) in enumerate(zip(split, got))]
    return _scatter_start(f"rs_scatter_start{tag}", parts)


def _rs_end(tag, started, after, kc_idx):
    parts, lands = _scatter_wait(f"rs_scatter_wait{tag}", started, after)
    halves = [_chip_add(f"rs_chip_add{tag}_{i}", p, r, kc_idx) for i, (p, r) in enumerate(zip(parts, lands))]
    full = _pair_gather(f"rs_pair_gather{tag}", halves)
    return [f.reshape(2 * f.shape[1], f.shape[2]) for f in full]


def kernel(x, w_in, lb_logits, a_norm_w, c_sinks, w_out, ln1_g, ln1_b, w_gate, w_up, conv_w, conv_b, w_down, ln2_g, ln2_b, loss_target, m_w_in, m_lb_logits, m_a_norm_w, m_c_sinks, m_w_out, m_ln1_g, m_ln1_b, m_w_gate, m_w_up, m_conv_w, m_conv_b, m_w_down, m_ln2_g, m_ln2_b, v_w_in, v_lb_logits, v_a_norm_w, v_c_sinks, v_w_out, v_ln1_g, v_ln1_b, v_w_gate, v_w_up, v_conv_w, v_conv_b, v_w_down, v_ln2_g, v_ln2_b):
    cx, cy, cc = _coords()
    c_idx = jnp.reshape(cc, (1,)).astype(jnp.int32)
    k_me = 2 * cx + cy
    k_idx = jnp.reshape(k_me, (1,)).astype(jnp.int32)
    kc_idx = jnp.stack([k_me, cc]).astype(jnp.int32)

    def slot(nm, w, l, run_after=None):
        b = _into_slot(f"slot_{nm}{l}", w, l, k_idx, BF16, run_after)
        return b.reshape(N_CHIPS, 2, b.shape[1] // 2, b.shape[2])

    cw_slot = _into_slot("slot_cw", conv_w.reshape(1, DEPTH * CONV_WIDTH, FF_SHARD), 0, k_idx, F32)
    cw_slot = cw_slot.reshape(N_CHIPS, DEPTH, CONV_WIDTH, FF_SHARD)
    first, token = _gather_start("gather_start0", [[slot("wi", w_in, 0), cw_slot]])
    sl = [{nm: slot(nm, w, l, token) for nm, w in (("wi", w_in), ("wo", w_out), ("wg", w_gate), ("wu", w_up), ("wd", w_down))
           if (nm, l) != ("wi", 0)} for l in range(DEPTH)]
    rest, token = _gather_start("gather_start1", [
        [sl[0]["wo"], sl[0]["wg"], sl[0]["wu"], sl[0]["wd"]],
        [sl[1]["wi"], sl[1]["wo"]],
        [sl[1]["wg"], sl[1]["wu"], sl[1]["wd"]]])
    stages = first + rest

    def mat(b):
        return b.reshape(N_CHIPS, 2 * b.shape[2], b.shape[3])

    fwd0, token = _gather_forward("gather_fwd0", stages[0], token)
    wi0, cw_all = _gather_wait("gather_wait0", fwd0, token)
    cw_full = jnp.transpose(cw_all, (1, 2, 0, 3)).reshape(DEPTH, CONV_WIDTH, D_FF)
    tables = _rope_tables()
    weights = [dict(wi=mat(wi0)), dict()]

    h = x[0]
    saved = []
    for l in range(DEPTH):
        w = weights[l]
        proj = _fwd_colsharded(f"proj{l}", h, w["wi"])
        o_a, raw = _hgrn_fwd(f"hgrn_fwd{l}", proj, lb_logits, a_norm_w[l], l)
        o_b, lse_b = _attn_fwd(f"dilated_fwd{l}", proj, tables, None, n_heads=B_HEADS, rep=1, q0=QB0, k0=KB0, v0=VB0,
                               patterns=B_PATTERNS)
        st = stages[1] if l == 0 else stages[3]
        fwd, _ = _gather_forward(f"gather_fwd{2 * l + 1}", st, o_b)
        sink_b = jnp.broadcast_to(c_sinks[l][:, None, None], (C_HEADS, 8, BLK))
        o_c, lse_c = _attn_fwd(f"window_fwd{l}", proj, tables, sink_b, n_heads=C_HEADS, rep=C_HEADS // C_KV_HEADS,
                               q0=QC0, k0=KC0, v0=VC0, patterns=C_PATTERNS)
        got = _gather_wait(f"gather_wait{2 * l + 1}", fwd, o_c)
        if l == 0:
            w["wo"], w["wg"], w["wu"], w["wd"] = (mat(b) for b in got)
        else:
            w["wg"], w["wu"], w["wd"] = (mat(b) for b in got)
        mixed = jnp.concatenate([o_a, o_b, o_c], axis=1)
        y1 = _fwd_rowsharded(f"wout{l}", mixed, w["wo"], OUT_SHARD)
        x1 = _ln_fwd(f"ln1_fwd{l}", h, y1, ln1_g[l], ln1_b[l])
        g = _fwd_colsharded(f"gate{l}", x1, w["wg"])
        u = _fwd_colsharded(f"up{l}", x1, w["wu"])
        if l == 0:
            fwd, _ = _gather_forward("gather_fwd2", stages[2], u)
        hh = _conv_gate_fwd(f"conv_fwd{l}", g, u, cw_full[l], conv_b[l])
        y2 = _fwd_rowsharded(f"down{l}", hh, w["wd"], FF_SHARD)
        x2 = _ln_fwd(f"ln2_fwd{l}", x1, y2, ln2_g[l], ln2_b[l])
        if l == 0:
            weights[1]["wi"], weights[1]["wo"] = (mat(b) for b in _gather_wait("gather_wait2", fwd, x2))
        saved.append((h, proj, raw, lse_b, sink_b, lse_c, mixed, y1, x1, g, u, hh, y2))
        h = x2

    dy, loss_part = _loss_head(h, loss_target[0])

    d_res, d_path = None, dy
    small = [None] * DEPTH
    mat_grads = [None] * DEPTH
    prev_ffn = prev_mix_swap = None
    for l in reversed(range(DEPTH)):
        h_in, proj, raw, lse_b, sink_b, lse_c, mixed, y1, x1, g, u, hh, y2 = saved[l]
        wi, wo, wg, wu, wd = (weights[l][k] for k in ("wi", "wo", "wg", "wu", "wd"))
        dz2, d_ln2g, d_ln2b = _ln_bwd(f"ln2_bwd{l}", x1, y2, ln2_g[l], d_res, d_path,
                                      run_after=prev_mix_swap[2][0] if prev_mix_swap else None)
        dhh = _bwd_act_rowsharded(f"down_dx{l}", dz2, wd, FF_SHARD)
        prev_mix = _rs_scatter(f"{l + 1}m", prev_mix_swap, dhh, c_idx) if prev_mix_swap else None
        d_wd = _bwd_w_rowsharded(f"down_dw{l}", hh, dz2, FF_SHARD)
        dg, du, d_cw, d_cb = _conv_gate_bwd(f"conv_bwd{l}", g, u, cw_full[l], conv_b[l], dhh,
                                            run_after=prev_mix[2][0] if prev_mix else None)
        dx1 = _bwd_act_colsharded(f"gateup_dx{l}", [(dg, wg), (du, wu)])
        d_wg = _bwd_w_colsharded(f"gate_dw{l}", x1, dg)
        d_wu = _bwd_w_colsharded(f"up_dw{l}", x1, du)
        if prev_ffn:
            g_wg, g_wu, g_wd = _rs_end(f"{l + 1}f", prev_ffn, d_wu, kc_idx)
        ffn_swap = _rs_swap(f"{l}f", [d_wg, d_wu, d_wd])
        dz1, d_ln1g, d_ln1b = _ln_bwd(f"ln1_bwd{l}", h_in, y1, ln1_g[l], dz2, dx1, run_after=ffn_swap[2][0])
        dmix = _bwd_act_rowsharded(f"wout_dx{l}", dz1, wo, OUT_SHARD)
        d_wo = _bwd_w_rowsharded(f"wout_dw{l}", mixed, dz1, OUT_SHARD)
        if prev_mix:
            g_wi, g_wo = _rs_end(f"{l + 1}m", prev_mix, d_wo, kc_idx)
            mat_grads[l + 1] = [g_wi, g_wo, g_wg, g_wu, g_wd]
        s_ffn = _rs_scatter(f"{l}f", ffn_swap, d_wo, c_idx)
        dq_a, df_a, di_a, dg_a, d_nw, d_lb = _hgrn_bwd(f"hgrn_bwd{l}", proj, raw, dmix, lb_logits, a_norm_w[l], l,
                                                       run_after=s_ffn[2][0])
        dq_b, dk_b, dv_b, _ = _attn_bwd(f"dilated_bwd{l}", proj, mixed, dmix, lse_b, tables, None, n_kv=B_HEADS, rep=1,
                                        q0=QB0, k0=KB0, v0=VB0, m0=A_HEADS, patterns=B_PATTERNS)
        dq_c, dk_c, dv_c, d_sink = _attn_bwd(f"window_bwd{l}", proj, mixed, dmix, lse_c, tables, sink_b, n_kv=C_KV_HEADS,
                                             rep=C_HEADS // C_KV_HEADS, q0=QC0, k0=KC0, v0=VC0, m0=A_HEADS + B_HEADS,
                                             patterns=C_PATTERNS)
        dproj = jnp.concatenate([dq_a, df_a, di_a, dg_a, dq_b, dk_b, dv_b, dq_c, dk_c, dv_c], axis=1)
        dxp = _bwd_act_colsharded(f"proj_dx{l}", [(dproj, wi)])
        d_wi = _bwd_w_colsharded(f"proj_dw{l}", h_in, dproj)
        d_res, d_path = dz1, dxp
        prev_ffn, prev_mix_swap = s_ffn, _rs_swap(f"{l}m", [d_wi, d_wo])
        small[l] = (d_lb, d_nw.reshape(A_HEADS, 8, BLK)[:, 0].sum(0), d_sink[:, 0, 0], d_ln1g[0], d_ln1b[0],
                    d_cw, d_cb[0], d_ln2g[0], d_ln2b[0])
    grad_x2 = _axpy("grad_x", d_res, d_path)
    grad_x = grad_x2[None]
    s_mix = _rs_scatter("0m", prev_mix_swap, grad_x2, c_idx)

    g_lb = small[0][0] + small[1][0]
    per_layer = [jnp.stack([small[0][i], small[1][i]]) for i in range(1, 9)]
    small_shapes = [(DEPTH, 4 * BLK), (DEPTH, BLK), (DEPTH, C_HEADS), (DEPTH, D_MODEL), (DEPTH, D_MODEL),
                    (DEPTH, CONV_WIDTH, D_FF), (DEPTH, D_FF), (DEPTH, D_MODEL), (DEPTH, D_MODEL), (BLK,)]
    rows = _rows_for(small_shapes)
    total = _sum_small(_gather_small(_pack([g_lb] + per_layer + [loss_part[0]], rows)))
    g_lb, g_nw, g_sink, g_ln1g, g_ln1b, g_cw_full, g_cb, g_ln2g, g_ln2b, loss_row = _unpack(total, small_shapes)
    loss = loss_row[0]
    g_cw = lax.dynamic_slice_in_dim(g_cw_full, k_me * FF_SHARD, FF_SHARD, axis=2)

    sw = [lb_logits, a_norm_w, c_sinks, ln1_g, ln1_b, conv_w, conv_b, ln2_g, ln2_b]
    sg = [g_lb, g_nw, g_sink, g_ln1g, g_ln1b, g_cw, g_cb, g_ln2g, g_ln2b]
    sm = [m_lb_logits, m_a_norm_w, m_c_sinks, m_ln1_g, m_ln1_b, m_conv_w, m_conv_b, m_ln2_g, m_ln2_b]
    sv = [v_lb_logits, v_a_norm_w, v_c_sinks, v_ln1_g, v_ln1_b, v_conv_w, v_conv_b, v_ln2_g, v_ln2_b]
    shapes = [a.shape for a in sw]
    prow = _rows_for(shapes)
    sd, snm, snv = (_unpack(b, shapes) for b in _adamw_small(_pack(sw, prow), _pack(sg, prow), _pack(sm, prow), _pack(sv, prow)))

    names = ["w_in", "w_out", "w_gate", "w_up", "w_down"]
    mw = [w_in, w_out, w_gate, w_up, w_down]
    mm = [m_w_in, m_w_out, m_w_gate, m_w_up, m_w_down]
    mv = [v_w_in, v_w_out, v_w_gate, v_w_up, v_w_down]
    res = [None] * 5
    ffn0 = _rs_end("0f", prev_ffn, s_mix[2][0], kc_idx)
    for i, g0 in zip((2, 3, 4), ffn0):
        res[i] = _adamw_matrix(f"adamw_{names[i]}", mw[i], [g0, mat_grads[1][i]], mm[i], mv[i])
    mix0 = _rs_end("0m", s_mix, res[4][1], kc_idx)
    for i, g0 in zip((0, 1), mix0):
        res[i] = _adamw_matrix(f"adamw_{names[i]}", mw[i], [g0, mat_grads[1][i]], mm[i], mv[i])
    mg, md, mnm, mnv = ([r[j] for r in res] for j in range(4))

    def ordered(mat, sm_):
        return [mat[0], sm_[0], sm_[1], sm_[2], mat[1], sm_[3], sm_[4], mat[2], mat[3], sm_[5], sm_[6], mat[4], sm_[7], sm_[8]]

    return (loss, grad_x, *ordered(mg, sg), *ordered(md, sd), *ordered(mnm, snm), *ordered(mnv, snv))
```

```python
import functools
import math

import jax
import jax.numpy as jnp
from jax import lax
from jax.experimental import pallas as pl
from jax.experimental.pallas import tpu as pltpu

F32 = jnp.float32
BF16 = jnp.bfloat16

D_MODEL = 2048
SEQ = 2048
DEPTH = 2
HEAD_DIM = 128
A_HEADS = 4
B_HEADS = 6
C_HEADS = 6
C_KV_HEADS = 2
A_CHUNK = 16
DILATED_PATTERNS = ((128, 1), (512, 4), (2048, 16))
C_WINDOW = 128
ROPE_THETA = 500000.0
ROPE_DIM = HEAD_DIM // 4
D_FF = 5632
CONV_WIDTH = 3
LN_EPS = 1e-5
ALPHA = (2 * DEPTH) ** 0.25
IN_WIDTH = 5632
MIX_WIDTH = 2048
ADAM_LR = 0.001
ADAM_B1 = 0.9
ADAM_B2 = 0.999
ADAM_EPS = 1e-08
ADAM_WD = 0.01
ADAM_STEP = 10

N_CHIPS = 4
N_DEV = 8
FF_SHARD = D_FF // N_CHIPS
OUT_SHARD = MIX_WIDTH // N_CHIPS
BLK = 128
N_CHUNK = SEQ // A_CHUNK
SLAB = 32

QA0, FA0, IA0, GA0 = 0, 4, 8, 12
QB0, KB0, VB0 = 16, 22, 28
QC0, KC0, VC0 = 34, 40, 42

VMEM_LIMIT_V7X = 56 * 1024 * 1024
HI = lax.Precision.HIGHEST
MESH = pl.DeviceIdType.MESH


def _cp(sem=None, vmem=VMEM_LIMIT_V7X, **kw):
    return pltpu.CompilerParams(dimension_semantics=sem, vmem_limit_bytes=vmem, **kw)


def _sigmoid(x):
    return 1.0 / (1.0 + jnp.exp(-x))


def _mm(name, pairs, dims, grid, a_specs, b_specs, out_spec, out_shape, nk=1, acc_shape=None):
    n_pairs = len(pairs)

    def body(*refs):
        o_ref = refs[2 * n_pairs]
        part = None
        for p in range(n_pairs):
            a = refs[2 * p][...].astype(BF16)
            b = refs[2 * p + 1][...].astype(BF16)
            t = lax.dot_general(a, b, dims, preferred_element_type=F32)
            part = t if part is None else part + t
        if nk == 1:
            o_ref[...] = part.astype(o_ref.dtype)
        else:
            acc = refs[2 * n_pairs + 1]
            k = pl.program_id(len(grid) - 1)

            @pl.when(k == 0)
            def _():
                acc[...] = part

            @pl.when(k > 0)
            def _():
                acc[...] += part

            @pl.when(k == nk - 1)
            def _():
                o_ref[...] = acc[...].astype(o_ref.dtype)

    in_specs, args = [], []
    for (a, b), sa, sb in zip(pairs, a_specs, b_specs):
        in_specs += [sa, sb]
        args += [a, b]
    sem = ("parallel",) * (len(grid) - (1 if nk > 1 else 0)) + (("arbitrary",) if nk > 1 else ())
    return pl.pallas_call(
        body, name=name, grid=grid, in_specs=in_specs, out_specs=out_spec, out_shape=out_shape,
        scratch_shapes=[pltpu.VMEM(acc_shape, F32)] if nk > 1 else [],
        compiler_params=_cp(sem),
    )(*args)


NN = (((1,), (0,)), ((), ()))
NT = (((1,), (1,)), ((), ()))
TN = (((0,), (0,)), ((), ()))
TM = 1024


def _fwd_colsharded(name, x, w_stk):
    return _mm(name, [(x, w_stk)], NN, (N_CHIPS, SEQ // TM),
               [pl.BlockSpec((TM, D_MODEL), lambda j, i: (i, 0))],
               [pl.BlockSpec((None, D_MODEL, FF_SHARD), lambda j, i: (j, 0, 0))],
               pl.BlockSpec((TM, FF_SHARD), lambda j, i: (i, j)),
               jax.ShapeDtypeStruct((SEQ, D_FF), F32))


def _fwd_rowsharded(name, a, w_stk, shard):
    tn = D_MODEL
    return _mm(name, [(a, w_stk)], NN, (SEQ // TM, D_MODEL // tn, N_CHIPS),
               [pl.BlockSpec((TM, shard), lambda i, j, k: (i, k))],
               [pl.BlockSpec((None, shard, tn), lambda i, j, k: (k, 0, j))],
               pl.BlockSpec((TM, tn), lambda i, j, k: (i, j)),
               jax.ShapeDtypeStruct((SEQ, D_MODEL), F32), nk=N_CHIPS, acc_shape=(TM, tn))


def _bwd_act_colsharded(name, pairs):
    tn = 1024
    n = len(pairs)
    return _mm(name, pairs, NT, (SEQ // TM, D_MODEL // tn, N_CHIPS),
               [pl.BlockSpec((TM, FF_SHARD), lambda i, j, k: (i, k))] * n,
               [pl.BlockSpec((None, tn, FF_SHARD), lambda i, j, k: (k, j, 0))] * n,
               pl.BlockSpec((TM, tn), lambda i, j, k: (i, j)),
               jax.ShapeDtypeStruct((SEQ, D_MODEL), F32), nk=N_CHIPS, acc_shape=(TM, tn))


def _bwd_act_rowsharded(name, dy, w_stk, shard):
    return _mm(name, [(dy, w_stk)], NT, (N_CHIPS, SEQ // TM),
               [pl.BlockSpec((TM, D_MODEL), lambda j, i: (i, 0))],
               [pl.BlockSpec((None, shard, D_MODEL), lambda j, i: (j, 0, 0))],
               pl.BlockSpec((TM, shard), lambda j, i: (i, j)),
               jax.ShapeDtypeStruct((SEQ, N_CHIPS * shard), F32))


def _bwd_w_colsharded(name, x, dy):
    tm = 1024
    return _mm(name, [(x, dy)], TN, (N_CHIPS, D_MODEL // tm),
               [pl.BlockSpec((SEQ, tm), lambda j, i: (0, i))],
               [pl.BlockSpec((SEQ, FF_SHARD), lambda j, i: (0, j))],
               pl.BlockSpec((None, tm, FF_SHARD), lambda j, i: (j, i, 0)),
               jax.ShapeDtypeStruct((N_CHIPS, D_MODEL, FF_SHARD), BF16))


def _bwd_w_rowsharded(name, a, dy, shard):
    tn = 1024
    return _mm(name, [(a, dy)], TN, (N_CHIPS, D_MODEL // tn),
               [pl.BlockSpec((SEQ, shard), lambda j, i: (0, j))],
               [pl.BlockSpec((SEQ, tn), lambda j, i: (0, i))],
               pl.BlockSpec((None, shard, tn), lambda j, i: (j, 0, i)),
               jax.ShapeDtypeStruct((N_CHIPS, shard, D_MODEL), BF16))


TR = 256


def _ln_fwd(name, x, y, g, b):
    def body(x_ref, y_ref, g_ref, b_ref, o_ref, ob_ref):
        z = ALPHA * x_ref[...] + y_ref[...]
        mu = jnp.mean(z, -1, keepdims=True)
        zc = z - mu
        var = jnp.mean(zc * zc, -1, keepdims=True)
        o = zc * lax.rsqrt(var + LN_EPS) * g_ref[...] + b_ref[...]
        o_ref[...] = o
        ob_ref[...] = o.astype(BF16)

    row = pl.BlockSpec((TR, D_MODEL), lambda i: (i, 0))
    vec = pl.BlockSpec((1, D_MODEL), lambda i: (0, 0))
    return pl.pallas_call(body, name=name, grid=(SEQ // TR,), in_specs=[row, row, vec, vec], out_specs=[row, row],
                          out_shape=[jax.ShapeDtypeStruct((SEQ, D_MODEL), F32), jax.ShapeDtypeStruct((SEQ, D_MODEL), BF16)],
                          compiler_params=_cp(("parallel",)))(x, y, g.reshape(1, -1), b.reshape(1, -1))


def _to_bf16(name, x):
    def body(x_ref, o_ref):
        o_ref[...] = x_ref[...].astype(BF16)

    row = pl.BlockSpec((TR, D_MODEL), lambda i: (i, 0))
    return pl.pallas_call(body, name=name, grid=(SEQ // TR,), in_specs=[row], out_specs=row,
                          out_shape=jax.ShapeDtypeStruct((SEQ, D_MODEL), BF16),
                          compiler_params=_cp(("parallel",)))(x)


def _ln_bwd(name, x, y, g, d_res, d_path, run_after=None):
    has_res = d_res is not None
    n_in = 4 + has_res + (run_after is not None)

    def body(*refs):
        dz_ref, dzb_ref, dg_ref, db_ref = refs[n_in:]
        if has_res:
            x_ref, y_ref, g_ref, r_ref, p_ref = refs[:5]
            dout = ALPHA * r_ref[...] + p_ref[...]
        else:
            x_ref, y_ref, g_ref, p_ref = refs[:4]
            dout = p_ref[...]
        z = ALPHA * x_ref[...] + y_ref[...]
        mu = jnp.mean(z, -1, keepdims=True)
        zc = z - mu
        rstd = lax.rsqrt(jnp.mean(zc * zc, -1, keepdims=True) + LN_EPS)
        zh = zc * rstd
        dzh = dout * g_ref[...]
        dz = rstd * (dzh - jnp.mean(dzh, -1, keepdims=True) - zh * jnp.mean(dzh * zh, -1, keepdims=True))
        dz_ref[...] = dz
        dzb_ref[...] = dz.astype(BF16)
        pg = jnp.sum(dout * zh, 0, keepdims=True)
        pb = jnp.sum(dout, 0, keepdims=True)

        @pl.when(pl.program_id(0) == 0)
        def _():
            dg_ref[...] = pg
            db_ref[...] = pb

        @pl.when(pl.program_id(0) > 0)
        def _():
            dg_ref[...] += pg
            db_ref[...] += pb

    row = pl.BlockSpec((TR, D_MODEL), lambda i: (i, 0))
    vec = pl.BlockSpec((1, D_MODEL), lambda i: (0, 0))
    args = [x, y, g.reshape(1, -1)] + ([d_res] if has_res else []) + [d_path]
    in_specs = [row, row, vec] + ([row] if has_res else []) + [row]
    if run_after is not None:
        args.append(run_after)
        in_specs.append(pl.BlockSpec(memory_space=pl.ANY))
    vshape = jax.ShapeDtypeStruct((1, D_MODEL), F32)
    return pl.pallas_call(body, name=name, grid=(SEQ // TR,), in_specs=in_specs, out_specs=[row, row, vec, vec],
                          out_shape=[jax.ShapeDtypeStruct((SEQ, D_MODEL), F32), jax.ShapeDtypeStruct((SEQ, D_MODEL), BF16),
                                     vshape, vshape],
                          compiler_params=_cp(("arbitrary",)))(*args)


def _loss_head(y, target):
    def body(y_ref, t_ref, dy_ref, l_ref):
        e = y_ref[...] - t_ref[...]
        dy_ref[...] = e * (1.0 / D_MODEL)
        part = jnp.full((8, BLK), 0.5 / D_MODEL * jnp.sum(e * e), F32)

        @pl.when(pl.program_id(0) == 0)
        def _():
            l_ref[...] = part

        @pl.when(pl.program_id(0) > 0)
        def _():
            l_ref[...] += part

    row = pl.BlockSpec((TR, D_MODEL), lambda i: (i, 0))
    return pl.pallas_call(body, name="loss_head", grid=(SEQ // TR,), in_specs=[row, row],
                          out_specs=[row, pl.BlockSpec((8, BLK), lambda i: (0, 0))],
                          out_shape=[jax.ShapeDtypeStruct((SEQ, D_MODEL), F32), jax.ShapeDtypeStruct((8, BLK), F32)],
                          compiler_params=_cp(("arbitrary",)))(y, target)


def _axpy(name, a, b):
    def body(a_ref, b_ref, o_ref):
        o_ref[...] = ALPHA * a_ref[...] + b_ref[...]

    row = pl.BlockSpec((TR, D_MODEL), lambda i: (i, 0))
    return pl.pallas_call(body, name=name, grid=(SEQ // TR,), in_specs=[row, row], out_specs=row,
                          out_shape=jax.ShapeDtypeStruct((SEQ, D_MODEL), F32),
                          compiler_params=_cp(("parallel",)))(a, b)


TC = 512


def _shift_down(x, s, rows):
    if s == 0:
        return x
    return jnp.where(rows >= s, pltpu.roll(x, s, axis=0), 0.0)


def _shift_up(x, s, rows):
    if s == 0:
        return x
    return jnp.where(rows < SEQ - s, pltpu.roll(x, SEQ - s, axis=0), 0.0)


def _conv_gate_fwd(name, g, u, cw, cb):
    def body(g_ref, u_ref, w_ref, b_ref, h_ref):
        gg = g_ref[...]
        rows = lax.broadcasted_iota(jnp.int32, gg.shape, 0)
        gc = b_ref[...] + w_ref[2:3, :] * gg
        gc = gc + w_ref[1:2, :] * _shift_down(gg, 1, rows)
        gc = gc + w_ref[0:1, :] * _shift_down(gg, 2, rows)
        h_ref[...] = (gc * _sigmoid(gc) * u_ref[...]).astype(BF16)

    col = pl.BlockSpec((SEQ, TC), lambda j: (0, j))
    return pl.pallas_call(body, name=name, grid=(D_FF // TC,),
                          in_specs=[col, col, pl.BlockSpec((CONV_WIDTH, TC), lambda j: (0, j)),
                                    pl.BlockSpec((1, TC), lambda j: (0, j))],
                          out_specs=col, out_shape=jax.ShapeDtypeStruct((SEQ, D_FF), BF16),
                          compiler_params=_cp(("parallel",)))(g, u, cw, cb.reshape(1, -1))


def _conv_gate_bwd(name, g, u, cw, cb, dh, run_after=None):
    def body(g_ref, u_ref, w_ref, b_ref, dh_ref, *rest):
        dg_ref, du_ref, dw_ref, db_ref = rest[-4:]
        gg = g_ref[...]
        rows = lax.broadcasted_iota(jnp.int32, gg.shape, 0)
        g1 = _shift_down(gg, 1, rows)
        g2 = _shift_down(gg, 2, rows)
        gc = b_ref[...] + w_ref[2:3, :] * gg + w_ref[1:2, :] * g1 + w_ref[0:1, :] * g2
        sg = _sigmoid(gc)
        act = gc * sg
        dh = dh_ref[...]
        du_ref[...] = (dh * act).astype(BF16)
        dgc = dh * u_ref[...] * (sg * (1.0 + gc * (1.0 - sg)))
        db_ref[...] = jnp.sum(dgc, 0, keepdims=True)
        dw_ref[2:3, :] = jnp.sum(dgc * gg, 0, keepdims=True)
        dw_ref[1:2, :] = jnp.sum(dgc * g1, 0, keepdims=True)
        dw_ref[0:1, :] = jnp.sum(dgc * g2, 0, keepdims=True)
        dg_ref[...] = (w_ref[2:3, :] * dgc + w_ref[1:2, :] * _shift_up(dgc, 1, rows)
                       + w_ref[0:1, :] * _shift_up(dgc, 2, rows)).astype(BF16)

    col = pl.BlockSpec((SEQ, TC), lambda j: (0, j))
    w3 = pl.BlockSpec((CONV_WIDTH, TC), lambda j: (0, j))
    w1 = pl.BlockSpec((1, TC), lambda j: (0, j))
    big = jax.ShapeDtypeStruct((SEQ, D_FF), BF16)
    extra = [] if run_after is None else [run_after]
    return pl.pallas_call(body, name=name, grid=(D_FF // TC,),
                          in_specs=[col, col, w3, w1, col] + [pl.BlockSpec(memory_space=pl.ANY)] * len(extra),
                          out_specs=[col, col, w3, w1],
                          out_shape=[big, big, jax.ShapeDtypeStruct((CONV_WIDTH, D_FF), F32),
                                     jax.ShapeDtypeStruct((1, D_FF), F32)],
                          compiler_params=_cp(("parallel",)))(g, u, cw, cb.reshape(1, -1), dh, *extra)


def _lbs_of(logits, layer):
    m = jnp.max(logits, 0, keepdims=True)
    e = jnp.exp(logits - m)
    p = e / jnp.sum(e, 0, keepdims=True)
    lb = jnp.zeros((1, BLK), F32)
    for r in range(1, layer + 1):
        lb = lb + p[r:r + 1, :]
    return lb, p


def _dlogits_of(p, dlb, layer):
    rows = lax.broadcasted_iota(jnp.int32, p.shape, 0)
    dp = jnp.where((rows >= 1) & (rows <= layer), dlb, 0.0)
    return p * (dp - jnp.sum(p * dp, 0, keepdims=True))


SROWS = SLAB * A_CHUNK
N_SLAB = N_CHUNK // SLAB


def _chunk_prefix(x, rowi):
    for s in (1, 2, 4, 8):
        x = x + jnp.where(rowi >= s, pltpu.roll(x, s, axis=0), 0.0)
    return x


def _chunk_suffix(x, rowi):
    for s in (1, 2, 4, 8):
        x = x + jnp.where(rowi < A_CHUNK - s, pltpu.roll(x, SROWS - s, axis=0), 0.0)
    return x


def _c3(x):
    return x.reshape(SLAB, A_CHUNK, BLK)


def _c2(x):
    return x.reshape(SROWS, BLK)


def _split(x):
    top = lax.bitcast_convert_type(lax.bitcast_convert_type(x, jnp.uint32) & jnp.uint32(0xFFFF0000), F32)
    return top.astype(BF16), (x - top).astype(BF16)


def _lane_sum_b(x2, ones):
    hi, lo = _split(x2)
    return jnp.dot(hi, ones, preferred_element_type=F32) + jnp.dot(lo, ones, preferred_element_type=F32)


def _bmm(eq, a, b):
    ah, al = _split(a)
    bh, bl = _split(b)

    def mm(u, v):
        return jnp.einsum(eq, u, v, preferred_element_type=F32)

    return mm(ah, bh) + (mm(ah, bl) + mm(al, bh))


def _slab_rows(s):
    return pl.ds(s * SROWS, SROWS)


def _hgrn_prep(q, f, lb):
    rowi = lax.broadcasted_iota(jnp.int32, (SROWS, BLK), 0) & (A_CHUNK - 1)
    sq = _sigmoid(q)
    qc = q * sq
    sf = _sigmoid(f)
    fg = lb + (1.0 - lb) * sf
    kc = 1.0 - fg
    b = _chunk_prefix(jnp.log(fg), rowi)
    b3 = _c3(b)
    blast = b3[:, A_CHUNK - 1:A_CHUNK, :]
    eb = jnp.exp(b)
    ekb = _c2(jnp.exp(blast - b3))
    dec = jnp.exp(blast.reshape(SLAB, BLK))
    return rowi, sq, qc, sf, fg, kc, b, eb, ekb, dec


def _hgrn_slab_states(s, carry, v, ke, dec, dec_ref, u_ref, st_ref):
    dec_ref[pl.ds(s * SLAB, SLAB), :] = dec
    u_ref[...] = _bmm('ncv,nck->nvk', _c3(v), _c3(ke))

    def step(j, c):
        st_ref[j] = c
        return dec_ref[pl.ds(s * SLAB + j, 1), :] * c + u_ref[j]

    return lax.fori_loop(0, SLAB, step, carry)


def _hgrn_fwd(name, proj, lb_logits, nw, layer):
    def body(q_ref, f_ref, i_ref, g_ref, lg_ref, nw_ref, out_ref, raw_ref, dec_ref, u_ref, st_ref):
        lb, _ = _lbs_of(lg_ref[...], layer)
        ones = jnp.ones((BLK, BLK), BF16)
        carry = jnp.zeros((BLK, BLK), F32)
        for s in range(N_SLAB):
            rows = _slab_rows(s)
            v = i_ref[rows, :]
            rowi, sq, qc, sf, fg, kc, b, eb, ekb, dec = _hgrn_prep(q_ref[rows, :], f_ref[rows, :], lb)
            carry = _hgrn_slab_states(s, carry, v, kc * ekb, dec, dec_ref, u_ref, st_ref)
            o = _c2(_bmm('nck,nvk->ncv', _c3(qc * eb), st_ref[...]))
            qc3, kc3, b3, v3, row3 = _c3(qc), _c3(kc), _c3(b), _c3(v), _c3(rowi)
            for j in range(A_CHUNK):
                dj = jnp.exp(jnp.where(row3 >= j, b3 - b3[:, j:j + 1, :], -jnp.inf))
                a = _lane_sum_b(_c2(qc3 * dj * kc3[:, j:j + 1, :]), ones)
                o = o + a * _c2(jnp.broadcast_to(v3[:, j:j + 1, :], v3.shape))
            raw_ref[rows, :] = o
            r = lax.rsqrt(jnp.mean(o * o, -1, keepdims=True) + LN_EPS)
            gg = g_ref[rows, :]
            out_ref[rows, :] = o * r * nw_ref[...] * (gg * _sigmoid(gg))

    def colblk(c0):
        return pl.BlockSpec((SEQ, BLK), lambda h: (0, c0 + h))

    big = jax.ShapeDtypeStruct((SEQ, A_HEADS * BLK), F32)
    return pl.pallas_call(
        body, name=name, grid=(A_HEADS,),
        in_specs=[colblk(QA0), colblk(FA0), colblk(IA0), colblk(GA0),
                  pl.BlockSpec((DEPTH, BLK), lambda h: (0, h)), pl.BlockSpec((1, BLK), lambda h: (0, 0))],
        out_specs=[colblk(0), colblk(0)], out_shape=[big, big],
        scratch_shapes=[pltpu.VMEM((N_CHUNK, BLK), F32), pltpu.VMEM((SLAB, BLK, BLK), F32),
                        pltpu.VMEM((SLAB, BLK, BLK), F32)],
        compiler_params=_cp(("parallel",)))(proj, proj, proj, proj, lb_logits, nw.reshape(1, -1))


def _hgrn_bwd(name, proj, raw, dmix, lb_logits, nw, layer, run_after=None):
    extra = [] if run_after is None else [run_after]

    def body(q_ref, f_ref, i_ref, g_ref, raw_ref, do_ref, lg_ref, nw_ref, *rest):
        (dq_ref, df_ref, di_ref, dg_ref, dnw_ref, dlg_ref,
         dec_ref, u_ref, st_ref, h_ref, dbs_ref, dkc_ref, tot_ref) = rest[-13:]
        lb, p = _lbs_of(lg_ref[...], layer)
        ones = jnp.ones((BLK, BLK), BF16)
        nwv = nw_ref[...]

        carry = jnp.zeros((BLK, BLK), F32)
        for s in range(N_SLAB):
            rows = _slab_rows(s)
            rowi, sq, qc, sf, fg, kc, b, eb, ekb, dec = _hgrn_prep(q_ref[rows, :], f_ref[rows, :], lb)
            carry = _hgrn_slab_states(s, carry, i_ref[rows, :], kc * ekb, dec, dec_ref, u_ref,
                                      st_ref.at[pl.ds(s * SLAB, SLAB)])

        carry = jnp.zeros((BLK, BLK), F32)
        dnw = jnp.zeros((1, BLK), F32)
        for s in reversed(range(N_SLAB)):
            rows = _slab_rows(s)
            q, v = q_ref[rows, :], i_ref[rows, :]
            rowi, sq, qc, sf, fg, kc, b, eb, ekb, dec = _hgrn_prep(q, f_ref[rows, :], lb)
            ke = kc * ekb
            qe = qc * eb

            o = raw_ref[rows, :]
            gg = g_ref[rows, :]
            sgg = _sigmoid(gg)
            dout = do_ref[rows, :]
            r = lax.rsqrt(jnp.mean(o * o, -1, keepdims=True) + LN_EPS)
            oh = o * r
            dg_ref[rows, :] = (dout * oh * nwv * (sgg * (1.0 + gg * (1.0 - sgg)))).astype(BF16)
            dn = dout * (gg * sgg)
            dnw = dnw + jnp.sum(dn * oh, 0, keepdims=True)
            doh = dn * nwv
            do = r * (doh - oh * jnp.mean(doh * oh, -1, keepdims=True))
            do3, qe3, v3, ke3 = _c3(do), _c3(qe), _c3(v), _c3(ke)

            u_ref[...] = _bmm('ncv,nck->nvk', do3, qe3)

            def step(jj, c, s=s):
                j = SLAB - 1 - jj
                h_ref[j] = c
                return u_ref[j] + dec_ref[pl.ds(s * SLAB + j, 1), :] * c

            carry = lax.fori_loop(0, SLAB, step, carry)

            hh = h_ref[...]
            dqc = _c2(_bmm('ncv,nvk->nck', do3, st_ref[pl.ds(s * SLAB, SLAB)])) * eb
            dkc = _c2(_bmm('ncv,nvk->nck', v3, hh)) * ekb
            dv = _c2(_bmm('nck,nvk->ncv', ke3, hh))

            qc3, kc3, b3, row3 = _c3(qc), _c3(kc), _c3(b), _c3(rowi)
            for j in range(A_CHUNK):
                dj = jnp.exp(jnp.where(row3 >= j, b3 - b3[:, j:j + 1, :], -jnp.inf))
                kj = kc3[:, j:j + 1, :]
                vj = jnp.broadcast_to(v3[:, j:j + 1, :], v3.shape)
                att = _c3(_lane_sum_b(_c2(qc3 * dj * kj), ones))
                datt = _c3(_lane_sum_b(_c2(do3 * vj), ones))
                md = dj * datt
                dqc = dqc + _c2(md * kj)
                sel = row3 == j
                dkc = dkc + _c2(jnp.where(sel, jnp.sum(md * qc3, 1, keepdims=True), 0.0))
                dv = dv + _c2(jnp.where(sel, jnp.sum(att * do3, 1, keepdims=True), 0.0))
            di_ref[rows, :] = dv.astype(BF16)
            dq_ref[rows, :] = (dqc * (sq * (1.0 + q * (1.0 - sq)))).astype(BF16)

            dbs = _chunk_suffix(qc * dqc - kc * dkc, rowi)
            dbs_ref[rows, :] = dbs
            dkc_ref[rows, :] = dkc
            tot_ref[pl.ds(s * SLAB, SLAB), :] = _c3(dbs)[:, 0:1, :].reshape(SLAB, BLK)
        dnw_ref[...] = jnp.broadcast_to(dnw, (8, BLK))

        rn = lax.broadcasted_iota(jnp.int32, (N_CHUNK, N_CHUNK), 0)
        cn = lax.broadcasted_iota(jnp.int32, (N_CHUNK, N_CHUNK), 1)
        tot_ref[...] = jnp.dot((cn > rn).astype(F32), tot_ref[...], preferred_element_type=F32, precision=HI)
        dlb = jnp.zeros((1, BLK), F32)
        for s in range(N_SLAB):
            rows = _slab_rows(s)
            sf = _sigmoid(f_ref[rows, :])
            fg = lb + (1.0 - lb) * sf
            later = tot_ref[pl.ds(s * SLAB, SLAB), :]
            dlg = _c2(_c3(dbs_ref[rows, :]) + later[:, None, :])
            dfg = dlg / fg - dkc_ref[rows, :]
            df_ref[rows, :] = (dfg * (1.0 - lb) * sf * (1.0 - sf)).astype(BF16)
            dlb = dlb + jnp.sum(dfg * (1.0 - sf), 0, keepdims=True)
        dlg_ref[...] = _dlogits_of(p, dlb, layer)

    def colblk(c0):
        return pl.BlockSpec((SEQ, BLK), lambda h: (0, c0 + h))

    bigb = jax.ShapeDtypeStruct((SEQ, A_HEADS * BLK), BF16)
    return pl.pallas_call(
        body, name=name, grid=(A_HEADS,),
        in_specs=[colblk(QA0), colblk(FA0), colblk(IA0), colblk(GA0), colblk(0), colblk(0),
                  pl.BlockSpec((DEPTH, BLK), lambda h: (0, h)), pl.BlockSpec((1, BLK), lambda h: (0, 0))]
        + [pl.BlockSpec(memory_space=pl.ANY)] * len(extra),
        out_specs=[colblk(0), colblk(0), colblk(0), colblk(0),
                   pl.BlockSpec((8, BLK), lambda h: (h, 0)), pl.BlockSpec((DEPTH, BLK), lambda h: (0, h))],
        out_shape=[bigb, bigb, bigb, bigb, jax.ShapeDtypeStruct((A_HEADS * 8, BLK), F32),
                   jax.ShapeDtypeStruct((DEPTH, A_HEADS * BLK), F32)],
        scratch_shapes=[pltpu.VMEM((N_CHUNK, BLK), F32), pltpu.VMEM((SLAB, BLK, BLK), F32),
                        pltpu.VMEM((N_CHUNK, BLK, BLK), F32), pltpu.VMEM((SLAB, BLK, BLK), F32),
                        pltpu.VMEM((SEQ, BLK), F32), pltpu.VMEM((SEQ, BLK), F32), pltpu.VMEM((N_CHUNK, BLK), F32)],
        compiler_params=_cp(("parallel",)))(proj, proj, proj, proj, raw, dmix, lb_logits, nw.reshape(1, -1), *extra)


SCALE = HEAD_DIM ** -0.5


def _rope_tables():
    half = ROPE_DIM // 2
    inv = ROPE_THETA ** (-jnp.arange(0, ROPE_DIM, 2, dtype=F32) / ROPE_DIM)
    ang = jnp.arange(SEQ, dtype=F32)[:, None] * inv[None, :]
    cos, sin = jnp.cos(ang), jnp.sin(ang)
    pad = jnp.zeros((SEQ, HEAD_DIM - ROPE_DIM), F32)
    zero = jnp.zeros((SEQ, half), F32)
    c = jnp.concatenate([cos, cos, pad + 1.0], 1)
    s_lo = jnp.concatenate([zero, sin, pad], 1)
    s_hi = jnp.concatenate([-sin, zero, pad], 1)
    return c, s_lo, s_hi


def _rope(x, c, s_lo, s_hi):
    half = ROPE_DIM // 2
    return x * c + pltpu.roll(x, half, axis=1) * s_lo + pltpu.roll(x, HEAD_DIM - half, axis=1) * s_hi


def _unrope(dy, c, s_lo, s_hi):
    half = ROPE_DIM // 2
    return dy * c + pltpu.roll(dy * s_lo, HEAD_DIM - half, axis=1) + pltpu.roll(dy * s_hi, half, axis=1)


def _rows(start, size, stride):
    return pl.ds(start, size) if stride == 1 else pl.ds(start, size, stride=stride)


def _band_blocks(patterns):
    out = []
    for p, (max_lag, dil) in enumerate(patterns):
        nb = SEQ // dil // BLK
        for r in range(dil):
            for n in range(nb):
                lo = max(n - 1, 0)
                kn = (n - lo + 1) * BLK
                out.append((p, _rows(r + n * BLK * dil, BLK, dil), _rows(r + lo * BLK * dil, kn, dil), kn,
                            (n - lo) * BLK, max_lag))
    return out


def _band_valid(kn, off, max_lag):
    lag = off + lax.broadcasted_iota(jnp.int32, (BLK, kn), 0) - lax.broadcasted_iota(jnp.int32, (BLK, kn), 1)
    return (lag >= 0) & (lag <= max_lag)


def _attn_fwd(name, proj, tables, sink_b, *, n_heads, rep, q0, k0, v0, patterns):
    n_pat = len(patterns)
    blocks = _band_blocks(patterns)
    has_sink = sink_b is not None

    def body(*refs):
        if has_sink:
            q_ref, k_ref, v_ref, c_ref, sl_ref, sh_ref, sink_ref, o_ref, lse_ref, qr, kr, op = refs
            sk = sink_ref[0:1, 0:1]
        else:
            q_ref, k_ref, v_ref, c_ref, sl_ref, sh_ref, o_ref, lse_ref, qr, kr, op = refs
        c, s_lo, s_hi = c_ref[...], sl_ref[...], sh_ref[...]
        qr[...] = _rope(q_ref[...], c, s_lo, s_hi)
        kr[...] = _rope(k_ref[...], c, s_lo, s_hi)
        for p, qrows, krows, kn, off, max_lag in blocks:
            qb = qr[qrows, :].astype(BF16)
            kb = kr[krows, :].astype(BF16)
            vb = v_ref[krows, :].astype(BF16)
            s = lax.dot_general(qb, kb, NT, preferred_element_type=F32) * SCALE
            s = jnp.where(_band_valid(kn, off, max_lag), s, -jnp.inf)
            m = jnp.max(s, -1, keepdims=True)
            if has_sink:
                m = jnp.maximum(m, sk)
            e = jnp.exp(s - m)
            den = jnp.sum(e, -1, keepdims=True)
            if has_sink:
                den = den + jnp.exp(sk - m)
            o = jnp.dot(e.astype(BF16), vb, preferred_element_type=F32) / den
            op.at[p][qrows, :] = o
            lse_ref.at[p][qrows, :] = jnp.broadcast_to(m + jnp.log(den), (BLK, BLK))
        if n_pat == 1:
            o_ref[...] = op[0]
        else:
            ls = [lse_ref[p] for p in range(n_pat)]
            m = functools.reduce(jnp.maximum, ls)
            es = [jnp.exp(l - m) for l in ls]
            tot = functools.reduce(jnp.add, es)
            acc = None
            for p in range(n_pat):
                t = (es[p] / tot) * op[p]
                acc = t if acc is None else acc + t
            o_ref[...] = acc

    def colblk(fn):
        return pl.BlockSpec((SEQ, BLK), fn)

    tab = pl.BlockSpec((SEQ, BLK), lambda h: (0, 0))
    in_specs = [colblk(lambda h: (0, q0 + h)), colblk(lambda h: (0, k0 + h // rep)), colblk(lambda h: (0, v0 + h // rep)),
                tab, tab, tab]
    args = [proj, proj, proj, *tables]
    if has_sink:
        in_specs.append(pl.BlockSpec((None, 8, BLK), lambda h: (h, 0, 0)))
        args.append(sink_b)
    return pl.pallas_call(
        body, name=name, grid=(n_heads,), in_specs=in_specs,
        out_specs=[colblk(lambda h: (0, h)), pl.BlockSpec((None, n_pat, SEQ, BLK), lambda h: (h, 0, 0, 0))],
        out_shape=[jax.ShapeDtypeStruct((SEQ, n_heads * BLK), F32),
                   jax.ShapeDtypeStruct((n_heads, n_pat, SEQ, BLK), F32)],
        scratch_shapes=[pltpu.VMEM((SEQ, BLK), F32), pltpu.VMEM((SEQ, BLK), F32), pltpu.VMEM((n_pat, SEQ, BLK), F32)],
        compiler_params=_cp(("parallel",)))(*args)


def _attn_bwd(name, proj, mixed, dmix, lse, tables, sink_b, *, n_kv, rep, q0, k0, v0, m0, patterns):
    n_pat = len(patterns)
    n_heads = n_kv * rep
    blocks = _band_blocks(patterns)
    has_sink = sink_b is not None

    def body(*refs):
        if has_sink:
            (q_ref, k_ref, v_ref, o_ref, do_ref, lse_ref, c_ref, sl_ref, sh_ref, sink_ref,
             dq_ref, dk_ref, dv_ref, dsk_ref, qr, kr, dqa, dka, dva, dd, ww) = refs
        else:
            (q_ref, k_ref, v_ref, o_ref, do_ref, lse_ref, c_ref, sl_ref, sh_ref,
             dq_ref, dk_ref, dv_ref, dsk_ref, qr, kr, dqa, dka, dva, dd, ww) = refs
        j = pl.program_id(1)
        c, s_lo, s_hi = c_ref[...], sl_ref[...], sh_ref[...]
        qr[...] = _rope(q_ref[...], c, s_lo, s_hi)
        kr[...] = _rope(k_ref[...], c, s_lo, s_hi)
        dcol = jnp.sum(do_ref[...] * o_ref[...], -1, keepdims=True)
        dd[...] = jnp.broadcast_to(dcol, (SEQ, BLK))
        if n_pat == 1:
            ww[0] = jnp.ones((SEQ, BLK), F32)
        else:
            ls = [lse_ref[p] for p in range(n_pat)]
            m = functools.reduce(jnp.maximum, ls)
            es = [jnp.exp(l - m) for l in ls]
            tot = functools.reduce(jnp.add, es)
            for p in range(n_pat):
                ww[p] = es[p] / tot
        dqa[...] = jnp.zeros((SEQ, BLK), F32)

        @pl.when(j == 0)
        def _():
            dka[...] = jnp.zeros((SEQ, BLK), F32)
            dva[...] = jnp.zeros((SEQ, BLK), F32)

        for p, qrows, krows, kn, off, max_lag in blocks:
            qb = qr[qrows, :].astype(BF16)
            kb = kr[krows, :].astype(BF16)
            vb = v_ref[krows, :].astype(BF16)
            dob = do_ref[qrows, :].astype(BF16)
            lcol = lse_ref.at[p][qrows, :][:, 0:1]
            wcol = ww.at[p][qrows, :][:, 0:1]
            dcb = dd[qrows, :][:, 0:1]
            s = lax.dot_general(qb, kb, NT, preferred_element_type=F32) * SCALE
            a = jnp.where(_band_valid(kn, off, max_lag), jnp.exp(s - lcol), 0.0) * wcol
            dp = lax.dot_general(dob, vb, NT, preferred_element_type=F32)
            ds = (a * (dp - dcb) * SCALE).astype(BF16)
            dqa[qrows, :] += jnp.dot(ds, kb, preferred_element_type=F32)
            dka[krows, :] += lax.dot_general(ds, qb, TN, preferred_element_type=F32)
            dva[krows, :] += lax.dot_general(a.astype(BF16), dob, TN, preferred_element_type=F32)

        if has_sink:
            sk = sink_ref[0:1, 0:1]
            ps = jnp.exp(sk - lse_ref[0][:, 0:1])
            dsk_ref[...] = jnp.full((8, BLK), -jnp.sum(ps * dcol), F32)
        else:
            dsk_ref[...] = jnp.zeros((8, BLK), F32)
        dq_ref[...] = _unrope(dqa[...], c, s_lo, s_hi).astype(BF16)

        @pl.when(j == rep - 1)
        def _():
            dk_ref[...] = _unrope(dka[...], c, s_lo, s_hi).astype(BF16)
            dv_ref[...] = dva[...].astype(BF16)

    def colblk(fn):
        return pl.BlockSpec((SEQ, BLK), fn)

    tab = pl.BlockSpec((SEQ, BLK), lambda g, j: (0, 0))
    in_specs = [colblk(lambda g, j: (0, q0 + g * rep + j)), colblk(lambda g, j: (0, k0 + g)), colblk(lambda g, j: (0, v0 + g)),
                colblk(lambda g, j: (0, m0 + g * rep + j)), colblk(lambda g, j: (0, m0 + g * rep + j)),
                pl.BlockSpec((None, n_pat, SEQ, BLK), lambda g, j: (g * rep + j, 0, 0, 0)), tab, tab, tab]
    args = [proj, proj, proj, mixed, dmix, lse, *tables]
    if has_sink:
        in_specs.append(pl.BlockSpec((None, 8, BLK), lambda g, j: (g * rep + j, 0, 0)))
        args.append(sink_b)
    acc = pltpu.VMEM((SEQ, BLK), F32)
    return pl.pallas_call(
        body, name=name, grid=(n_kv, rep), in_specs=in_specs,
        out_specs=[colblk(lambda g, j: (0, g * rep + j)), colblk(lambda g, j: (0, g)), colblk(lambda g, j: (0, g)),
                   pl.BlockSpec((None, 8, BLK), lambda g, j: (g * rep + j, 0, 0))],
        out_shape=[jax.ShapeDtypeStruct((SEQ, n_heads * BLK), BF16), jax.ShapeDtypeStruct((SEQ, n_kv * BLK), BF16),
                   jax.ShapeDtypeStruct((SEQ, n_kv * BLK), BF16), jax.ShapeDtypeStruct((n_heads, 8, BLK), F32)],
        scratch_shapes=[acc, acc, acc, acc, acc, acc, pltpu.VMEM((n_pat, SEQ, BLK), F32)],
        compiler_params=_cp(("parallel", "arbitrary")))(*args)


B_PATTERNS = tuple((w // d, d) for w, d in DILATED_PATTERNS)
C_PATTERNS = ((C_WINDOW - 1, 1),)


ANY = pl.BlockSpec(memory_space=pl.ANY)
CHIP_MASKS = ((1, 0), (0, 1), (1, 1))


def _coords():
    return lax.axis_index("x"), lax.axis_index("y"), lax.axis_index("c")


def _flip(v, m):
    return 1 - v if m else v


def _into_slot(name, w, layer, k_idx, dtype, run_after=None):
    _, rows, cols = w.shape
    tr = rows // 8 if rows % 64 == 0 else rows

    def body(k_ref, w_ref, *rest):
        rest[-1][...] = w_ref[...].astype(dtype)

    in_specs = [pl.BlockSpec((None, tr, cols), lambda i, k: (layer, i, 0))]
    args = [k_idx, w]
    if run_after is not None:
        in_specs.append(pl.BlockSpec(memory_space=pl.ANY))
        args.append(run_after)
    return pl.pallas_call(
        body, name=name,
        grid_spec=pltpu.PrefetchScalarGridSpec(
            num_scalar_prefetch=1, grid=(rows // tr,), in_specs=in_specs,
            out_specs=pl.BlockSpec((None, tr, cols), lambda i, k: (k[0], i, 0))),
        out_shape=jax.ShapeDtypeStruct((N_CHIPS, rows, cols), dtype),
        compiler_params=_cp(("parallel",)))(*args)


HBM_SPEC = pl.BlockSpec(memory_space=pltpu.HBM)
SEM_SPEC = pl.BlockSpec(memory_space=pltpu.SEMAPHORE)
TOKEN_SPEC = pl.BlockSpec(memory_space=pltpu.VMEM)
TOKEN_SHAPE = jax.ShapeDtypeStruct((8, BLK), F32)
DATAFLOW = pltpu.SideEffectType.DATAFLOW_SIDE_EFFECTING


def _hbm(a):
    return pltpu.with_memory_space_constraint(a, pltpu.HBM)


def _hbm_like(bufs):
    return [pltpu.HBM(b.shape, b.dtype) for b in bufs]


def _gather_start(name, stages):
    flat = [b for st in stages for b in st]
    n, ns = len(flat), len(stages)

    def body(*refs):
        ins = refs[:n]
        sems = refs[n:n + 2 * ns]
        token = refs[-1]
        x, y, c = _coords()
        k_me = 2 * x + y
        a = 0
        for s, st in enumerate(stages):
            for i in range(len(st)):
                mine = ins[a].at[k_me, c]
                for m, (mx, my) in enumerate(CHIP_MASKS):
                    pltpu.make_async_remote_copy(src_ref=mine, dst_ref=mine, send_sem=sems[2 * s].at[i * 3 + m],
                                                 recv_sem=sems[2 * s + 1].at[i * 3 + m],
                                                 device_id=(_flip(x, mx), _flip(y, my), c), device_id_type=MESH).start()
                a += 1
        token[...] = jnp.zeros_like(token)

    sem_shapes = []
    for st in stages:
        sem_shapes += [pltpu.SemaphoreType.DMA((3 * len(st),))] * 2
    out = pl.pallas_call(
        body, name=name, in_specs=[HBM_SPEC] * n,
        out_specs=tuple([SEM_SPEC] * (2 * ns) + [HBM_SPEC] * n + [TOKEN_SPEC]),
        out_shape=tuple(sem_shapes + _hbm_like(flat) + [TOKEN_SHAPE]),
        input_output_aliases={i: 2 * ns + i for i in range(n)},
        compiler_params=pltpu.CompilerParams(has_side_effects=DATAFLOW),
    )(*[_hbm(b) for b in flat])
    sems, bufs, token = out[:2 * ns], out[2 * ns:2 * ns + n], out[-1]
    res, a = [], 0
    for s, st in enumerate(stages):
        res.append((sems[2 * s], sems[2 * s + 1], list(bufs[a:a + len(st)])))
        a += len(st)
    return res, token


def _gather_forward(name, stage, after):
    ssem_in, rsem_in, bufs = stage
    n = len(bufs)

    def body(*refs):
        ins = refs[:n]
        s_in, r_in, _ = refs[n:n + 3]
        s_out, r_out = refs[n + 3:n + 5]
        token = refs[-1]
        x, y, c = _coords()
        for i in range(n):
            for m, (mx, my) in enumerate(CHIP_MASKS):
                kp = 2 * _flip(x, mx) + _flip(y, my)
                blk = ins[i].at[kp, c]
                got = pltpu.make_async_remote_copy(src_ref=blk, dst_ref=blk, send_sem=s_in.at[i * 3 + m],
                                                   recv_sem=r_in.at[i * 3 + m], device_id=(x, y, 1 - c), device_id_type=MESH)
                got.wait_send()
                got.wait_recv()
                pltpu.make_async_remote_copy(src_ref=blk, dst_ref=blk, send_sem=s_out.at[i * 3 + m],
                                             recv_sem=r_out.at[i * 3 + m], device_id=(x, y, 1 - c), device_id_type=MESH).start()
        token[...] = jnp.zeros_like(token)

    sem = pltpu.SemaphoreType.DMA((3 * n,))
    out = pl.pallas_call(
        body, name=name, in_specs=[HBM_SPEC] * n + [SEM_SPEC, SEM_SPEC, ANY],
        out_specs=tuple([SEM_SPEC, SEM_SPEC] + [HBM_SPEC] * n + [TOKEN_SPEC]),
        out_shape=tuple([sem, sem] + _hbm_like(bufs) + [TOKEN_SHAPE]),
        input_output_aliases={i: 2 + i for i in range(n)},
        compiler_params=pltpu.CompilerParams(has_side_effects=DATAFLOW),
    )(*bufs, ssem_in, rsem_in, after)
    return (out[0], out[1], list(out[2:2 + n])), out[-1]


def _gather_wait(name, stage, after):
    ssem, rsem, bufs = stage
    n = len(bufs)

    def body(*refs):
        ins = refs[:n]
        s_in, r_in, _ = refs[n:n + 3]
        x, y, c = _coords()
        for i in range(n):
            for m, (mx, my) in enumerate(CHIP_MASKS):
                kp = 2 * _flip(x, mx) + _flip(y, my)
                sent, got = ins[i].at[kp, c], ins[i].at[kp, 1 - c]
                cp = pltpu.make_async_remote_copy(src_ref=sent, dst_ref=got, send_sem=s_in.at[i * 3 + m],
                                                  recv_sem=r_in.at[i * 3 + m], device_id=(x, y, 1 - c), device_id_type=MESH)
                cp.wait_send()
                cp.wait_recv()

    out = pl.pallas_call(
        body, name=name, in_specs=[HBM_SPEC] * n + [SEM_SPEC, SEM_SPEC, ANY],
        out_specs=tuple([HBM_SPEC] * n), out_shape=tuple(_hbm_like(bufs)),
        input_output_aliases={i: i for i in range(n)},
        compiler_params=pltpu.CompilerParams(has_side_effects=DATAFLOW),
    )(*bufs, ssem, rsem, after)
    return list(out)


def _swap_start(name, grads):
    n = len(grads)

    def body(*refs):
        ins, lands = refs[:n], refs[n:2 * n]
        ssem, rsem = refs[2 * n:2 * n + 2]
        x, y, c = _coords()
        for a in range(n):
            for j in range(N_CHIPS):
                pltpu.make_async_remote_copy(src_ref=ins[a].at[j, 1 - c], dst_ref=lands[a].at[j],
                                             send_sem=ssem.at[a * N_CHIPS + j], recv_sem=rsem.at[a * N_CHIPS + j],
                                             device_id=(x, y, 1 - c), device_id_type=MESH).start()

    sem = pltpu.SemaphoreType.DMA((N_CHIPS * n,))
    land_shapes = [pltpu.HBM((N_CHIPS,) + g.shape[2:], g.dtype) for g in grads]
    out = pl.pallas_call(
        body, name=name, in_specs=[HBM_SPEC] * (2 * n),
        out_specs=tuple([SEM_SPEC, SEM_SPEC] + [HBM_SPEC] * (2 * n)),
        out_shape=tuple([sem, sem] + _hbm_like(grads) + land_shapes),
        input_output_aliases={i: 2 + i for i in range(2 * n)},
        compiler_params=pltpu.CompilerParams(has_side_effects=DATAFLOW),
    )(*[_hbm(g) for g in grads], *[_hbm(lax.empty((N_CHIPS,) + g.shape[2:], g.dtype)) for g in grads])
    return out[0], out[1], list(out[2:2 + n]), list(out[2 + n:])


def _swap_wait(name, started, after):
    ssem, rsem, grads, lands = started
    n = len(grads)

    def body(*refs):
        ins, lnd = refs[:n], refs[n:2 * n]
        s_in, r_in, _ = refs[2 * n:2 * n + 3]
        x, y, c = _coords()
        for a in range(n):
            for j in range(N_CHIPS):
                cp = pltpu.make_async_remote_copy(src_ref=ins[a].at[j, 1 - c], dst_ref=lnd[a].at[j],
                                                  send_sem=s_in.at[a * N_CHIPS + j], recv_sem=r_in.at[a * N_CHIPS + j],
                                                  device_id=(x, y, 1 - c), device_id_type=MESH)
                cp.wait_send()
                cp.wait_recv()

    out = pl.pallas_call(
        body, name=name, in_specs=[HBM_SPEC] * (2 * n) + [SEM_SPEC, SEM_SPEC, ANY],
        out_specs=tuple([HBM_SPEC] * (2 * n)), out_shape=tuple(_hbm_like(grads) + _hbm_like(lands)),
        input_output_aliases={i: i for i in range(2 * n)},
        compiler_params=pltpu.CompilerParams(has_side_effects=DATAFLOW),
    )(*grads, *lands, ssem, rsem, after)
    return list(out[:n]), list(out[n:])


def _scatter_start(name, parts):
    n = len(parts)

    def body(*refs):
        ins, lands = refs[:n], refs[n:2 * n]
        ssem, rsem = refs[2 * n:2 * n + 2]
        x, y, c = _coords()
        k_me = 2 * x + y
        for a in range(n):
            for m, (mx, my) in enumerate(CHIP_MASKS):
                px, py = _flip(x, mx), _flip(y, my)
                pltpu.make_async_remote_copy(src_ref=ins[a].at[2 * px + py], dst_ref=lands[a].at[k_me],
                                             send_sem=ssem.at[a * 3 + m], recv_sem=rsem.at[a * 3 + m],
                                             device_id=(px, py, c), device_id_type=MESH).start()

    sem = pltpu.SemaphoreType.DMA((3 * n,))
    out = pl.pallas_call(
        body, name=name, in_specs=[HBM_SPEC] * (2 * n),
        out_specs=tuple([SEM_SPEC, SEM_SPEC] + [HBM_SPEC] * (2 * n)),
        out_shape=tuple([sem, sem] + _hbm_like(parts) + _hbm_like(parts)),
        input_output_aliases={i: 2 + i for i in range(2 * n)},
        compiler_params=pltpu.CompilerParams(has_side_effects=DATAFLOW),
    )(*[_hbm(p) for p in parts], *[_hbm(lax.empty(p.shape, p.dtype)) for p in parts])
    return out[0], out[1], list(out[2:2 + n]), list(out[2 + n:])


def _scatter_wait(name, started, after):
    ssem, rsem, parts, lands = started
    n = len(parts)

    def body(*refs):
        ins, lnd = refs[:n], refs[n:2 * n]
        s_in, r_in, _ = refs[2 * n:2 * n + 3]
        x, y, c = _coords()
        k_me = 2 * x + y
        for a in range(n):
            for m, (mx, my) in enumerate(CHIP_MASKS):
                px, py = _flip(x, mx), _flip(y, my)
                cp = pltpu.make_async_remote_copy(src_ref=ins[a].at[2 * px + py], dst_ref=lnd[a].at[k_me],
                                                  send_sem=s_in.at[a * 3 + m], recv_sem=r_in.at[a * 3 + m],
                                                  device_id=(px, py, c), device_id_type=MESH)
                cp.wait_send()
                cp.wait_recv()

    out = pl.pallas_call(
        body, name=name, in_specs=[HBM_SPEC] * (2 * n) + [SEM_SPEC, SEM_SPEC, ANY],
        out_specs=tuple([HBM_SPEC] * (2 * n)), out_shape=tuple(_hbm_like(parts) + _hbm_like(lands)),
        input_output_aliases={i: i for i in range(2 * n)},
        compiler_params=pltpu.CompilerParams(has_side_effects=DATAFLOW),
    )(*parts, *lands, ssem, rsem, after)
    return list(out[:n]), list(out[n:])


def _pair_gather(name, bufs):
    n = len(bufs)

    def body(*refs):
        outs = refs[n:2 * n]
        ssem, rsem = refs[2 * n:]
        x, y, c = _coords()
        cps = []
        for a in range(n):
            mine = outs[a].at[c]
            cp = pltpu.make_async_remote_copy(src_ref=mine, dst_ref=mine, send_sem=ssem.at[a],
                                              recv_sem=rsem.at[a], device_id=(x, y, 1 - c), device_id_type=MESH)
            cp.start()
            cps.append(cp)
        for cp in cps:
            cp.wait()

    return pl.pallas_call(
        body, name=name, in_specs=[ANY] * n, out_specs=[ANY] * n,
        out_shape=[jax.ShapeDtypeStruct(b.shape, b.dtype) for b in bufs],
        input_output_aliases={a: a for a in range(n)},
        scratch_shapes=[pltpu.SemaphoreType.DMA((n,)), pltpu.SemaphoreType.DMA((n,))],
        compiler_params=pltpu.CompilerParams(has_side_effects=True),
    )(*bufs)


DEV_MASKS = tuple((mx, my, mc) for mx in (0, 1) for my in (0, 1) for mc in (0, 1) if (mx, my, mc) != (0, 0, 0))


def _gather_small(buf, run_after):
    def body(in_ref, _, out_ref, ssem, rsem, lsem):
        x, y, c = _coords()
        me = 4 * x + 2 * y + c
        cps = [pltpu.make_async_copy(in_ref, out_ref.at[me], lsem)]
        cps[0].start()
        for t, (mx, my, mc) in enumerate(DEV_MASKS):
            cp = pltpu.make_async_remote_copy(src_ref=in_ref, dst_ref=out_ref.at[me], send_sem=ssem.at[t],
                                              recv_sem=rsem.at[t], device_id=(_flip(x, mx), _flip(y, my), _flip(c, mc)),
                                              device_id_type=MESH)
            cp.start()
            cps.append(cp)
        for cp in cps:
            cp.wait()

    return pl.pallas_call(
        body, name="gather_small", in_specs=[ANY, ANY], out_specs=ANY,
        out_shape=jax.ShapeDtypeStruct((N_DEV,) + buf.shape, buf.dtype),
        scratch_shapes=[pltpu.SemaphoreType.DMA((N_DEV - 1,)), pltpu.SemaphoreType.DMA((N_DEV - 1,)),
                        pltpu.SemaphoreType.DMA(())],
        compiler_params=pltpu.CompilerParams(has_side_effects=True),
    )(buf, run_after)


def _row_tile(rows):
    return rows // 2 if rows % 16 == 0 else rows


def _pair_add(name, grad, got, c_idx):
    _, _, r2, cols = grad.shape
    tr = _row_tile(r2)

    def body(c_ref, a_ref, b_ref, o_ref):
        o_ref[...] = (a_ref[...].astype(F32) + b_ref[...].astype(F32)).astype(BF16)

    return pl.pallas_call(
        body, name=name,
        grid_spec=pltpu.PrefetchScalarGridSpec(
            num_scalar_prefetch=1, grid=(N_CHIPS, r2 // tr),
            in_specs=[pl.BlockSpec((None, None, tr, cols), lambda j, i, c: (j, c[0], i, 0)),
                      pl.BlockSpec((None, tr, cols), lambda j, i, c: (j, i, 0))],
            out_specs=pl.BlockSpec((None, tr, cols), lambda j, i, c: (j, i, 0))),
        out_shape=jax.ShapeDtypeStruct((N_CHIPS, r2, cols), BF16),
        compiler_params=_cp(("parallel", "parallel")))(c_idx, grad, got)


def _chip_add(name, part, got, kc_idx):
    _, r2, cols = got.shape
    tr = _row_tile(r2)

    def body(k_ref, p_ref, g1_ref, g2_ref, g3_ref, o_ref):
        acc = p_ref[...].astype(F32)
        for g_ref in (g1_ref, g2_ref, g3_ref):
            acc = acc + g_ref[...].astype(F32)
        o_ref[...] = acc

    def slot(d):
        return pl.BlockSpec((None, tr, cols), lambda i, k: ((k[0] + d) % N_CHIPS, i, 0))

    return pl.pallas_call(
        body, name=name,
        grid_spec=pltpu.PrefetchScalarGridSpec(
            num_scalar_prefetch=1, grid=(r2 // tr,),
            in_specs=[slot(0), slot(1), slot(2), slot(3)],
            out_specs=pl.BlockSpec((None, tr, cols), lambda i, k: (k[1], i, 0))),
        out_shape=jax.ShapeDtypeStruct((2, r2, cols), F32),
        compiler_params=_cp(("parallel",)))(kc_idx, part, got, got, got)


def _adam_math(w, g, m, v):
    m2 = ADAM_B1 * m + (1.0 - ADAM_B1) * g
    v2 = ADAM_B2 * v + (1.0 - ADAM_B2) * (g * g)
    m_hat = m2 / (1.0 - ADAM_B1 ** ADAM_STEP)
    v_hat = v2 / (1.0 - ADAM_B2 ** ADAM_STEP)
    delta = -ADAM_LR * (m_hat / (jnp.sqrt(v_hat) + ADAM_EPS) + ADAM_WD * w)
    return delta, m2, v2


def _adamw_matrix(name, w, g_layers, m, v):
    _, rows, cols = w.shape
    tr = rows // 16 if rows % 128 == 0 else rows // 8

    def body(w_ref, g0_ref, g1_ref, m_ref, v_ref, go_ref, d_ref, mo_ref, vo_ref):
        g = jnp.where(pl.program_id(0) == 0, g0_ref[...], g1_ref[...])
        go_ref[...] = g
        d_ref[...], mo_ref[...], vo_ref[...] = _adam_math(w_ref[...], g, m_ref[...], v_ref[...])

    lay = pl.BlockSpec((None, tr, cols), lambda l, i: (l, i, 0))
    flat = pl.BlockSpec((tr, cols), lambda l, i: (i, 0))
    shp = jax.ShapeDtypeStruct(w.shape, F32)
    return pl.pallas_call(body, name=name, grid=(DEPTH, rows // tr), in_specs=[lay, flat, flat, lay, lay],
                          out_specs=[lay, lay, lay, lay], out_shape=[shp, shp, shp, shp],
                          compiler_params=_cp(("parallel", "parallel")))(w, g_layers[0], g_layers[1], m, v)


def _sum_small(gathered):
    def body(g_ref, o_ref):
        acc = g_ref[0]
        for d in range(1, N_DEV):
            acc = acc + g_ref[d]
        o_ref[...] = acc

    return pl.pallas_call(body, name="sum_small", out_shape=jax.ShapeDtypeStruct(gathered.shape[1:], F32),
                          compiler_params=_cp())(gathered)


def _adamw_small(w, g, m, v):
    def body(w_ref, g_ref, m_ref, v_ref, d_ref, mo_ref, vo_ref):
        d_ref[...], mo_ref[...], vo_ref[...] = _adam_math(w_ref[...], g_ref[...], m_ref[...], v_ref[...])

    shp = jax.ShapeDtypeStruct(w.shape, F32)
    return pl.pallas_call(body, name="adamw_small", out_shape=[shp, shp, shp], compiler_params=_cp())(w, g, m, v)


def _pack(arrays, rows):
    flat = jnp.concatenate([a.reshape(-1) for a in arrays])
    return jnp.pad(flat, (0, rows * BLK - flat.shape[0])).reshape(rows, BLK)


def _unpack(buf, shapes):
    flat = buf.reshape(-1)
    out, pos = [], 0
    for s in shapes:
        n = math.prod(s)
        out.append(flat[pos:pos + n].reshape(s))
        pos += n
    return out


def _rows_for(shapes):
    n = sum(math.prod(s) for s in shapes)
    return -(-n // (8 * BLK)) * 8


def _rs_swap(tag, grads):
    return _swap_start(f"rs_swap_start{tag}", [g.reshape(N_CHIPS, 2, g.shape[1] // 2, g.shape[2]) for g in grads])


def _rs_scatter(tag, swapping, after, c_idx):
    split, got = _swap_wait(f"rs_swap_wait{tag}", swapping, after)
    parts = [_pair_add(f"rs_pair_add{tag}_{i}", s, r, c_idx) for i, (s, r) in enumerate(zip(split, got))]
    return _scatter_start(f"rs_scatter_start{tag}", parts)


def _rs_end(tag, started, after, kc_idx):
    parts, lands = _scatter_wait(f"rs_scatter_wait{tag}", started, after)
    halves = [_chip_add(f"rs_chip_add{tag}_{i}", p, r, kc_idx) for i, (p, r) in enumerate(zip(parts, lands))]
    full = _pair_gather(f"rs_pair_gather{tag}", halves)
    return [f.reshape(2 * f.shape[1], f.shape[2]) for f in full]


def kernel(x, w_in, lb_logits, a_norm_w, c_sinks, w_out, ln1_g, ln1_b, w_gate, w_up, conv_w, conv_b, w_down, ln2_g, ln2_b, loss_target, m_w_in, m_lb_logits, m_a_norm_w, m_c_sinks, m_w_out, m_ln1_g, m_ln1_b, m_w_gate, m_w_up, m_conv_w, m_conv_b, m_w_down, m_ln2_g, m_ln2_b, v_w_in, v_lb_logits, v_a_norm_w, v_c_sinks, v_w_out, v_ln1_g, v_ln1_b, v_w_gate, v_w_up, v_conv_w, v_conv_b, v_w_down, v_ln2_g, v_ln2_b):
    cx, cy, cc = _coords()
    c_idx = jnp.reshape(cc, (1,)).astype(jnp.int32)
    k_me = 2 * cx + cy
    k_idx = jnp.reshape(k_me, (1,)).astype(jnp.int32)
    kc_idx = jnp.stack([k_me, cc]).astype(jnp.int32)

    def slot(nm, w, l, run_after=None):
        b = _into_slot(f"slot_{nm}{l}", w, l, k_idx, BF16, run_after)
        return b.reshape(N_CHIPS, 2, b.shape[1] // 2, b.shape[2])

    cw_slot = _into_slot("slot_cw", conv_w.reshape(1, DEPTH * CONV_WIDTH, FF_SHARD), 0, k_idx, F32)
    cw_slot = cw_slot.reshape(N_CHIPS, DEPTH, CONV_WIDTH, FF_SHARD)
    first, token = _gather_start("gather_start0", [[slot("wi", w_in, 0), cw_slot]])
    sl = [{nm: slot(nm, w, l, token) for nm, w in (("wi", w_in), ("wo", w_out), ("wg", w_gate), ("wu", w_up), ("wd", w_down))
           if (nm, l) != ("wi", 0)} for l in range(DEPTH)]
    order = [(l, nm) for l in range(DEPTH) for nm in ("wi", "wo", "wg", "wu", "wd")][1:]
    rest, token = _gather_start("gather_start1", [[sl[l][nm]] for l, nm in order])
    stage_of = {key: st for key, st in zip(order, rest)}

    def mat(b):
        return b.reshape(N_CHIPS, 2 * b.shape[2], b.shape[3])

    fwd0, token = _gather_forward("gather_fwd0", first[0], token)
    wi0, cw_all = _gather_wait("gather_wait0", fwd0, token)
    cw_full = jnp.transpose(cw_all, (1, 2, 0, 3)).reshape(DEPTH, CONV_WIDTH, D_FF)
    tables = _rope_tables()

    passing = {}

    def pass_on(l, nm, after):
        passing[(l, nm)] = _gather_forward(f"gather_fwd_{nm}{l}", stage_of[(l, nm)], after)

    def arrived(l, nm, after):
        i = order.index((l, nm))
        if i + 1 < len(order):
            pass_on(*order[i + 1], after)
            after = passing[order[i + 1]][1]
        return mat(_gather_wait(f"gather_wait_{nm}{l}", passing[(l, nm)][0], after)[0])

    h = x[0]
    h_bf = _to_bf16("x_bf16", h)
    saved = []
    weights = []
    for l in range(DEPTH):
        wi = mat(wi0) if l == 0 else arrived(l, "wi", h)
        proj = _fwd_colsharded(f"proj{l}", h_bf, wi)
        o_a, raw = _hgrn_fwd(f"hgrn_fwd{l}", proj, lb_logits, a_norm_w[l], l)
        o_b, lse_b = _attn_fwd(f"dilated_fwd{l}", proj, tables, None, n_heads=B_HEADS, rep=1, q0=QB0, k0=KB0, v0=VB0,
                               patterns=B_PATTERNS)
        if l == 0:
            pass_on(l, "wo", o_b)
        sink_b = jnp.broadcast_to(c_sinks[l][:, None, None], (C_HEADS, 8, BLK))
        o_c, lse_c = _attn_fwd(f"window_fwd{l}", proj, tables, sink_b, n_heads=C_HEADS, rep=C_HEADS // C_KV_HEADS,
                               q0=QC0, k0=KC0, v0=VC0, patterns=C_PATTERNS)
        wo = arrived(l, "wo", o_c)
        mixed = jnp.concatenate([o_a, o_b, o_c], axis=1)
        mixed_bf = mixed.astype(BF16)
        y1 = _fwd_rowsharded(f"wout{l}", mixed_bf, wo, OUT_SHARD)
        x1, x1_bf = _ln_fwd(f"ln1_fwd{l}", h, y1, ln1_g[l], ln1_b[l])
        wg = arrived(l, "wg", x1)
        g = _fwd_colsharded(f"gate{l}", x1_bf, wg)
        wu = arrived(l, "wu", g)
        u = _fwd_colsharded(f"up{l}", x1_bf, wu)
        hh = _conv_gate_fwd(f"conv_fwd{l}", g, u, cw_full[l], conv_b[l])
        wd = arrived(l, "wd", hh)
        y2 = _fwd_rowsharded(f"down{l}", hh, wd, FF_SHARD)
        x2, x2_bf = _ln_fwd(f"ln2_fwd{l}", x1, y2, ln2_g[l], ln2_b[l])
        weights.append(dict(wi=wi, wo=wo, wg=wg, wu=wu, wd=wd))
        saved.append((h, h_bf, proj, raw, lse_b, sink_b, lse_c, mixed, mixed_bf, y1, x1, x1_bf, g, u, hh, y2))
        h, h_bf = x2, x2_bf

    dy, loss_part = _loss_head(h, loss_target[0])

    d_res, d_path = None, dy
    small = [None] * DEPTH
    mat_grads = [None] * DEPTH
    prev_ffn = prev_mix_swap = None
    for l in reversed(range(DEPTH)):
        h_in, h_in_bf, proj, raw, lse_b, sink_b, lse_c, mixed, mixed_bf, y1, x1, x1_bf, g, u, hh, y2 = saved[l]
        wi, wo, wg, wu, wd = (weights[l][k] for k in ("wi", "wo", "wg", "wu", "wd"))
        dz2, dz2_bf, d_ln2g, d_ln2b = _ln_bwd(f"ln2_bwd{l}", x1, y2, ln2_g[l], d_res, d_path,
                                              run_after=prev_mix_swap[2][0] if prev_mix_swap else None)
        dhh = _bwd_act_rowsharded(f"down_dx{l}", dz2_bf, wd, FF_SHARD)
        prev_mix = _rs_scatter(f"{l + 1}m", prev_mix_swap, dhh, c_idx) if prev_mix_swap else None
        d_wd = _bwd_w_rowsharded(f"down_dw{l}", hh, dz2_bf, FF_SHARD)
        dg, du, d_cw, d_cb = _conv_gate_bwd(f"conv_bwd{l}", g, u, cw_full[l], conv_b[l], dhh,
                                            run_after=prev_mix[2][0] if prev_mix else None)
        dx1 = _bwd_act_colsharded(f"gateup_dx{l}", [(dg, wg), (du, wu)])
        d_wg = _bwd_w_colsharded(f"gate_dw{l}", x1_bf, dg)
        d_wu = _bwd_w_colsharded(f"up_dw{l}", x1_bf, du)
        if prev_ffn:
            g_wg, g_wu, g_wd = _rs_end(f"{l + 1}f", prev_ffn, d_wu, kc_idx)
        ffn_swap = _rs_swap(f"{l}f", [d_wg, d_wu, d_wd])
        dz1, dz1_bf, d_ln1g, d_ln1b = _ln_bwd(f"ln1_bwd{l}", h_in, y1, ln1_g[l], dz2, dx1, run_after=ffn_swap[2][0])
        dmix = _bwd_act_rowsharded(f"wout_dx{l}", dz1_bf, wo, OUT_SHARD)
        d_wo = _bwd_w_rowsharded(f"wout_dw{l}", mixed_bf, dz1_bf, OUT_SHARD)
        if prev_mix:
            g_wi, g_wo = _rs_end(f"{l + 1}m", prev_mix, d_wo, kc_idx)
            mat_grads[l + 1] = [g_wi, g_wo, g_wg, g_wu, g_wd]
        s_ffn = _rs_scatter(f"{l}f", ffn_swap, d_wo, c_idx)
        dq_a, df_a, di_a, dg_a, d_nw, d_lb = _hgrn_bwd(f"hgrn_bwd{l}", proj, raw, dmix, lb_logits, a_norm_w[l], l,
                                                       run_after=s_ffn[2][0])
        dq_b, dk_b, dv_b, _ = _attn_bwd(f"dilated_bwd{l}", proj, mixed, dmix, lse_b, tables, None, n_kv=B_HEADS, rep=1,
                                        q0=QB0, k0=KB0, v0=VB0, m0=A_HEADS, patterns=B_PATTERNS)
        dq_c, dk_c, dv_c, d_sink = _attn_bwd(f"window_bwd{l}", proj, mixed, dmix, lse_c, tables, sink_b, n_kv=C_KV_HEADS,
                                             rep=C_HEADS // C_KV_HEADS, q0=QC0, k0=KC0, v0=VC0, m0=A_HEADS + B_HEADS,
                                             patterns=C_PATTERNS)
        dproj = jnp.concatenate([dq_a, df_a, di_a, dg_a, dq_b, dk_b, dv_b, dq_c, dk_c, dv_c], axis=1)
        dxp = _bwd_act_colsharded(f"proj_dx{l}", [(dproj, wi)])
        d_wi = _bwd_w_colsharded(f"proj_dw{l}", h_in_bf, dproj)
        d_res, d_path = dz1, dxp
        prev_ffn, prev_mix_swap = s_ffn, _rs_swap(f"{l}m", [d_wi, d_wo])
        small[l] = (d_lb, d_nw.reshape(A_HEADS, 8, BLK)[:, 0].sum(0), d_sink[:, 0, 0], d_ln1g[0], d_ln1b[0],
                    d_cw, d_cb[0], d_ln2g[0], d_ln2b[0])
    grad_x2 = _axpy("grad_x", d_res, d_path)
    grad_x = grad_x2[None]

    g_lb = small[0][0] + small[1][0]
    per_layer = [jnp.stack([small[0][i], small[1][i]]) for i in range(1, 9)]
    small_shapes = [(DEPTH, 4 * BLK), (DEPTH, BLK), (DEPTH, C_HEADS), (DEPTH, D_MODEL), (DEPTH, D_MODEL),
                    (DEPTH, CONV_WIDTH, D_FF), (DEPTH, D_FF), (DEPTH, D_MODEL), (DEPTH, D_MODEL), (BLK,)]
    rows = _rows_for(small_shapes)
    total = _sum_small(_gather_small(_pack([g_lb] + per_layer + [loss_part[0]], rows), prev_mix_swap[2][0]))
    g_lb, g_nw, g_sink, g_ln1g, g_ln1b, g_cw_full, g_cb, g_ln2g, g_ln2b, loss_row = _unpack(total, small_shapes)
    loss = loss_row[0]
    g_cw = lax.dynamic_slice_in_dim(g_cw_full, k_me * FF_SHARD, FF_SHARD, axis=2)

    sw = [lb_logits, a_norm_w, c_sinks, ln1_g, ln1_b, conv_w, conv_b, ln2_g, ln2_b]
    sg = [g_lb, g_nw, g_sink, g_ln1g, g_ln1b, g_cw, g_cb, g_ln2g, g_ln2b]
    sm = [m_lb_logits, m_a_norm_w, m_c_sinks, m_ln1_g, m_ln1_b, m_conv_w, m_conv_b, m_ln2_g, m_ln2_b]
    sv = [v_lb_logits, v_a_norm_w, v_c_sinks, v_ln1_g, v_ln1_b, v_conv_w, v_conv_b, v_ln2_g, v_ln2_b]
    shapes = [a.shape for a in sw]
    prow = _rows_for(shapes)
    sd, snm, snv = (_unpack(b, shapes) for b in _adamw_small(_pack(sw, prow), _pack(sg, prow), _pack(sm, prow), _pack(sv, prow)))

    names = ["w_in", "w_out", "w_gate", "w_up", "w_down"]
    mw = [w_in, w_out, w_gate, w_up, w_down]
    mm = [m_w_in, m_w_out, m_w_gate, m_w_up, m_w_down]
    mv = [v_w_in, v_w_out, v_w_gate, v_w_up, v_w_down]
    res = [None] * 5
    s_mix = _rs_scatter("0m", prev_mix_swap, total, c_idx)
    ffn0 = _rs_end("0f", prev_ffn, s_mix[2][0], kc_idx)
    for i, g0 in zip((2, 3, 4), ffn0):
        res[i] = _adamw_matrix(f"adamw_{names[i]}", mw[i], [g0, mat_grads[1][i]], mm[i], mv[i])
    mix0 = _rs_end("0m", s_mix, res[4][1], kc_idx)
    for i, g0 in zip((0, 1), mix0):
        res[i] = _adamw_matrix(f"adamw_{names[i]}", mw[i], [g0, mat_grads[1][i]], mm[i], mv[i])
    mg, md, mnm, mnv = ([r[j] for r in res] for j in range(4))

    def ordered(mat, sm_):
        return [mat[0], sm_[0], sm_[1], sm_[2], mat[1], sm_[3], sm_[4], mat[2], mat[3], sm_[5], sm_[6], mat[4], sm_[7], sm_[8]]

    return (loss, grad_x, *ordered(mg, sg), *ordered(md, sd), *ordered(mnm, snm), *ordered(mnv, snv))
```

```python
import functools
import math

import jax
import jax.numpy as jnp
from jax import lax
from jax.experimental import pallas as pl
from jax.experimental.pallas import tpu as pltpu

F32 = jnp.float32
BF16 = jnp.bfloat16

D_MODEL = 2048
SEQ = 2048
DEPTH = 2
HEAD_DIM = 128
A_HEADS = 4
B_HEADS = 6
C_HEADS = 6
C_KV_HEADS = 2
A_CHUNK = 16
DILATED_PATTERNS = ((128, 1), (512, 4), (2048, 16))
C_WINDOW = 128
ROPE_THETA = 500000.0
ROPE_DIM = HEAD_DIM // 4
D_FF = 5632
CONV_WIDTH = 3
LN_EPS = 1e-5
ALPHA = (2 * DEPTH) ** 0.25
IN_WIDTH = 5632
MIX_WIDTH = 2048
ADAM_LR = 0.001
ADAM_B1 = 0.9
ADAM_B2 = 0.999
ADAM_EPS = 1e-08
ADAM_WD = 0.01
ADAM_STEP = 10

N_CHIPS = 4
N_DEV = 8
FF_SHARD = D_FF // N_CHIPS
OUT_SHARD = MIX_WIDTH // N_CHIPS
BLK = 128
N_CHUNK = SEQ // A_CHUNK
SLAB = 32

QA0, FA0, IA0, GA0 = 0, 4, 8, 12
QB0, KB0, VB0 = 16, 22, 28
QC0, KC0, VC0 = 34, 40, 42

VMEM_LIMIT_V7X = 56 * 1024 * 1024
HI = lax.Precision.HIGHEST
MESH = pl.DeviceIdType.MESH


def _cp(sem=None, vmem=VMEM_LIMIT_V7X, **kw):
    return pltpu.CompilerParams(dimension_semantics=sem, vmem_limit_bytes=vmem, **kw)


def _sigmoid(x):
    return 1.0 / (1.0 + jnp.exp(-x))


def _mm(name, pairs, dims, grid, a_specs, b_specs, out_spec, out_shape, nk=1, acc_shape=None):
    n_pairs = len(pairs)

    def body(*refs):
        o_ref = refs[2 * n_pairs]
        part = None
        for p in range(n_pairs):
            a = refs[2 * p][...].astype(BF16)
            b = refs[2 * p + 1][...].astype(BF16)
            t = lax.dot_general(a, b, dims, preferred_element_type=F32)
            part = t if part is None else part + t
        if nk == 1:
            o_ref[...] = part.astype(o_ref.dtype)
        else:
            acc = refs[2 * n_pairs + 1]
            k = pl.program_id(len(grid) - 1)

            @pl.when(k == 0)
            def _():
                acc[...] = part

            @pl.when(k > 0)
            def _():
                acc[...] += part

            @pl.when(k == nk - 1)
            def _():
                o_ref[...] = acc[...].astype(o_ref.dtype)

    in_specs, args = [], []
    for (a, b), sa, sb in zip(pairs, a_specs, b_specs):
        in_specs += [sa, sb]
        args += [a, b]
    sem = ("parallel",) * (len(grid) - (1 if nk > 1 else 0)) + (("arbitrary",) if nk > 1 else ())
    return pl.pallas_call(
        body, name=name, grid=grid, in_specs=in_specs, out_specs=out_spec, out_shape=out_shape,
        scratch_shapes=[pltpu.VMEM(acc_shape, F32)] if nk > 1 else [],
        compiler_params=_cp(sem),
    )(*args)


NN = (((1,), (0,)), ((), ()))
NT = (((1,), (1,)), ((), ()))
TN = (((0,), (0,)), ((), ()))
TM = 1024


def _fwd_colsharded(name, x, w_stk, out_dtype=F32):
    return _mm(name, [(x, w_stk)], NN, (N_CHIPS, SEQ // TM),
               [pl.BlockSpec((TM, D_MODEL), lambda j, i: (i, 0))],
               [pl.BlockSpec((None, D_MODEL, FF_SHARD), lambda j, i: (j, 0, 0))],
               pl.BlockSpec((TM, FF_SHARD), lambda j, i: (i, j)),
               jax.ShapeDtypeStruct((SEQ, D_FF), out_dtype))


def _fwd_rowsharded(name, a, w_stk, shard):
    tn = D_MODEL
    return _mm(name, [(a, w_stk)], NN, (SEQ // TM, D_MODEL // tn, N_CHIPS),
               [pl.BlockSpec((TM, shard), lambda i, j, k: (i, k))],
               [pl.BlockSpec((None, shard, tn), lambda i, j, k: (k, 0, j))],
               pl.BlockSpec((TM, tn), lambda i, j, k: (i, j)),
               jax.ShapeDtypeStruct((SEQ, D_MODEL), F32), nk=N_CHIPS, acc_shape=(TM, tn))


def _bwd_act_colsharded(name, pairs):
    tn = 1024
    n = len(pairs)
    return _mm(name, pairs, NT, (SEQ // TM, D_MODEL // tn, N_CHIPS),
               [pl.BlockSpec((TM, FF_SHARD), lambda i, j, k: (i, k))] * n,
               [pl.BlockSpec((None, tn, FF_SHARD), lambda i, j, k: (k, j, 0))] * n,
               pl.BlockSpec((TM, tn), lambda i, j, k: (i, j)),
               jax.ShapeDtypeStruct((SEQ, D_MODEL), F32), nk=N_CHIPS, acc_shape=(TM, tn))


def _bwd_act_rowsharded(name, dy, w_stk, shard, out_dtype=F32):
    return _mm(name, [(dy, w_stk)], NT, (N_CHIPS, SEQ // TM),
               [pl.BlockSpec((TM, D_MODEL), lambda j, i: (i, 0))],
               [pl.BlockSpec((None, shard, D_MODEL), lambda j, i: (j, 0, 0))],
               pl.BlockSpec((TM, shard), lambda j, i: (i, j)),
               jax.ShapeDtypeStruct((SEQ, N_CHIPS * shard), out_dtype))


def _bwd_w_colsharded(name, x, dy):
    tm = 1024
    return _mm(name, [(x, dy)], TN, (N_CHIPS, D_MODEL // tm),
               [pl.BlockSpec((SEQ, tm), lambda j, i: (0, i))],
               [pl.BlockSpec((SEQ, FF_SHARD), lambda j, i: (0, j))],
               pl.BlockSpec((None, tm, FF_SHARD), lambda j, i: (j, i, 0)),
               jax.ShapeDtypeStruct((N_CHIPS, D_MODEL, FF_SHARD), BF16))


def _bwd_w_rowsharded(name, a, dy, shard):
    tn = 1024
    return _mm(name, [(a, dy)], TN, (N_CHIPS, D_MODEL // tn),
               [pl.BlockSpec((SEQ, shard), lambda j, i: (0, j))],
               [pl.BlockSpec((SEQ, tn), lambda j, i: (0, i))],
               pl.BlockSpec((None, shard, tn), lambda j, i: (j, 0, i)),
               jax.ShapeDtypeStruct((N_CHIPS, shard, D_MODEL), BF16))


TR = 256


def _ln_fwd(name, x, y, g, b):
    def body(x_ref, y_ref, g_ref, b_ref, o_ref, ob_ref):
        z = ALPHA * x_ref[...] + y_ref[...]
        mu = jnp.mean(z, -1, keepdims=True)
        zc = z - mu
        var = jnp.mean(zc * zc, -1, keepdims=True)
        o = zc * lax.rsqrt(var + LN_EPS) * g_ref[...] + b_ref[...]
        o_ref[...] = o
        ob_ref[...] = o.astype(BF16)

    row = pl.BlockSpec((TR, D_MODEL), lambda i: (i, 0))
    vec = pl.BlockSpec((1, D_MODEL), lambda i: (0, 0))
    return pl.pallas_call(body, name=name, grid=(SEQ // TR,), in_specs=[row, row, vec, vec], out_specs=[row, row],
                          out_shape=[jax.ShapeDtypeStruct((SEQ, D_MODEL), F32), jax.ShapeDtypeStruct((SEQ, D_MODEL), BF16)],
                          compiler_params=_cp(("parallel",)))(x, y, g.reshape(1, -1), b.reshape(1, -1))


def _to_bf16(name, x):
    def body(x_ref, o_ref):
        o_ref[...] = x_ref[...].astype(BF16)

    row = pl.BlockSpec((TR, D_MODEL), lambda i: (i, 0))
    return pl.pallas_call(body, name=name, grid=(SEQ // TR,), in_specs=[row], out_specs=row,
                          out_shape=jax.ShapeDtypeStruct((SEQ, D_MODEL), BF16),
                          compiler_params=_cp(("parallel",)))(x)


def _ln_bwd(name, x, y, g, d_res, d_path, run_after=None):
    has_res = d_res is not None
    n_in = 4 + has_res + (run_after is not None)

    def body(*refs):
        dz_ref, dzb_ref, dg_ref, db_ref = refs[n_in:]
        if has_res:
            x_ref, y_ref, g_ref, r_ref, p_ref = refs[:5]
            dout = ALPHA * r_ref[...] + p_ref[...]
        else:
            x_ref, y_ref, g_ref, p_ref = refs[:4]
            dout = p_ref[...]
        z = ALPHA * x_ref[...] + y_ref[...]
        mu = jnp.mean(z, -1, keepdims=True)
        zc = z - mu
        rstd = lax.rsqrt(jnp.mean(zc * zc, -1, keepdims=True) + LN_EPS)
        zh = zc * rstd
        dzh = dout * g_ref[...]
        dz = rstd * (dzh - jnp.mean(dzh, -1, keepdims=True) - zh * jnp.mean(dzh * zh, -1, keepdims=True))
        dz_ref[...] = dz
        dzb_ref[...] = dz.astype(BF16)
        pg = jnp.sum(dout * zh, 0, keepdims=True)
        pb = jnp.sum(dout, 0, keepdims=True)

        @pl.when(pl.program_id(0) == 0)
        def _():
            dg_ref[...] = pg
            db_ref[...] = pb

        @pl.when(pl.program_id(0) > 0)
        def _():
            dg_ref[...] += pg
            db_ref[...] += pb

    row = pl.BlockSpec((TR, D_MODEL), lambda i: (i, 0))
    vec = pl.BlockSpec((1, D_MODEL), lambda i: (0, 0))
    args = [x, y, g.reshape(1, -1)] + ([d_res] if has_res else []) + [d_path]
    in_specs = [row, row, vec] + ([row] if has_res else []) + [row]
    if run_after is not None:
        args.append(run_after)
        in_specs.append(pl.BlockSpec(memory_space=pl.ANY))
    vshape = jax.ShapeDtypeStruct((1, D_MODEL), F32)
    return pl.pallas_call(body, name=name, grid=(SEQ // TR,), in_specs=in_specs, out_specs=[row, row, vec, vec],
                          out_shape=[jax.ShapeDtypeStruct((SEQ, D_MODEL), F32), jax.ShapeDtypeStruct((SEQ, D_MODEL), BF16),
                                     vshape, vshape],
                          compiler_params=_cp(("arbitrary",)))(*args)


def _loss_head(y, target):
    def body(y_ref, t_ref, dy_ref, l_ref):
        e = y_ref[...] - t_ref[...]
        dy_ref[...] = e * (1.0 / D_MODEL)
        part = jnp.full((8, BLK), 0.5 / D_MODEL * jnp.sum(e * e), F32)

        @pl.when(pl.program_id(0) == 0)
        def _():
            l_ref[...] = part

        @pl.when(pl.program_id(0) > 0)
        def _():
            l_ref[...] += part

    row = pl.BlockSpec((TR, D_MODEL), lambda i: (i, 0))
    return pl.pallas_call(body, name="loss_head", grid=(SEQ // TR,), in_specs=[row, row],
                          out_specs=[row, pl.BlockSpec((8, BLK), lambda i: (0, 0))],
                          out_shape=[jax.ShapeDtypeStruct((SEQ, D_MODEL), F32), jax.ShapeDtypeStruct((8, BLK), F32)],
                          compiler_params=_cp(("arbitrary",)))(y, target)


def _axpy(name, a, b):
    def body(a_ref, b_ref, o_ref):
        o_ref[...] = ALPHA * a_ref[...] + b_ref[...]

    row = pl.BlockSpec((TR, D_MODEL), lambda i: (i, 0))
    return pl.pallas_call(body, name=name, grid=(SEQ // TR,), in_specs=[row, row], out_specs=row,
                          out_shape=jax.ShapeDtypeStruct((SEQ, D_MODEL), F32),
                          compiler_params=_cp(("parallel",)))(a, b)


TC = 512


def _shift_down(x, s, rows):
    if s == 0:
        return x
    return jnp.where(rows >= s, pltpu.roll(x, s, axis=0), 0.0)


def _shift_up(x, s, rows):
    if s == 0:
        return x
    return jnp.where(rows < SEQ - s, pltpu.roll(x, SEQ - s, axis=0), 0.0)


def _conv_gate_fwd(name, g, u, cw, cb):
    def body(g_ref, u_ref, w_ref, b_ref, h_ref):
        gg = g_ref[...].astype(F32)
        rows = lax.broadcasted_iota(jnp.int32, gg.shape, 0)
        gc = b_ref[...] + w_ref[2:3, :] * gg
        gc = gc + w_ref[1:2, :] * _shift_down(gg, 1, rows)
        gc = gc + w_ref[0:1, :] * _shift_down(gg, 2, rows)
        h_ref[...] = (gc * _sigmoid(gc) * u_ref[...].astype(F32)).astype(BF16)

    col = pl.BlockSpec((SEQ, TC), lambda j: (0, j))
    return pl.pallas_call(body, name=name, grid=(D_FF // TC,),
                          in_specs=[col, col, pl.BlockSpec((CONV_WIDTH, TC), lambda j: (0, j)),
                                    pl.BlockSpec((1, TC), lambda j: (0, j))],
                          out_specs=col, out_shape=jax.ShapeDtypeStruct((SEQ, D_FF), BF16),
                          compiler_params=_cp(("parallel",)))(g, u, cw, cb.reshape(1, -1))


def _conv_gate_bwd(name, g, u, cw, cb, dh, run_after=None):
    def body(g_ref, u_ref, w_ref, b_ref, dh_ref, *rest):
        dg_ref, du_ref, dw_ref, db_ref = rest[-4:]
        gg = g_ref[...].astype(F32)
        rows = lax.broadcasted_iota(jnp.int32, gg.shape, 0)
        g1 = _shift_down(gg, 1, rows)
        g2 = _shift_down(gg, 2, rows)
        gc = b_ref[...] + w_ref[2:3, :] * gg + w_ref[1:2, :] * g1 + w_ref[0:1, :] * g2
        sg = _sigmoid(gc)
        act = gc * sg
        dh = dh_ref[...].astype(F32)
        du_ref[...] = (dh * act).astype(BF16)
        dgc = dh * u_ref[...].astype(F32) * (sg * (1.0 + gc * (1.0 - sg)))
        db_ref[...] = jnp.sum(dgc, 0, keepdims=True)
        dw_ref[2:3, :] = jnp.sum(dgc * gg, 0, keepdims=True)
        dw_ref[1:2, :] = jnp.sum(dgc * g1, 0, keepdims=True)
        dw_ref[0:1, :] = jnp.sum(dgc * g2, 0, keepdims=True)
        dg_ref[...] = (w_ref[2:3, :] * dgc + w_ref[1:2, :] * _shift_up(dgc, 1, rows)
                       + w_ref[0:1, :] * _shift_up(dgc, 2, rows)).astype(BF16)

    col = pl.BlockSpec((SEQ, TC), lambda j: (0, j))
    w3 = pl.BlockSpec((CONV_WIDTH, TC), lambda j: (0, j))
    w1 = pl.BlockSpec((1, TC), lambda j: (0, j))
    big = jax.ShapeDtypeStruct((SEQ, D_FF), BF16)
    extra = [] if run_after is None else [run_after]
    return pl.pallas_call(body, name=name, grid=(D_FF // TC,),
                          in_specs=[col, col, w3, w1, col] + [pl.BlockSpec(memory_space=pl.ANY)] * len(extra),
                          out_specs=[col, col, w3, w1],
                          out_shape=[big, big, jax.ShapeDtypeStruct((CONV_WIDTH, D_FF), F32),
                                     jax.ShapeDtypeStruct((1, D_FF), F32)],
                          compiler_params=_cp(("parallel",)))(g, u, cw, cb.reshape(1, -1), dh, *extra)


def _lbs_of(logits, layer):
    m = jnp.max(logits, 0, keepdims=True)
    e = jnp.exp(logits - m)
    p = e / jnp.sum(e, 0, keepdims=True)
    lb = jnp.zeros((1, BLK), F32)
    for r in range(1, layer + 1):
        lb = lb + p[r:r + 1, :]
    return lb, p


def _dlogits_of(p, dlb, layer):
    rows = lax.broadcasted_iota(jnp.int32, p.shape, 0)
    dp = jnp.where((rows >= 1) & (rows <= layer), dlb, 0.0)
    return p * (dp - jnp.sum(p * dp, 0, keepdims=True))


SROWS = SLAB * A_CHUNK
N_SLAB = N_CHUNK // SLAB


def _chunk_prefix(x, rowi):
    for s in (1, 2, 4, 8):
        x = x + jnp.where(rowi >= s, pltpu.roll(x, s, axis=0), 0.0)
    return x


def _chunk_suffix(x, rowi):
    for s in (1, 2, 4, 8):
        x = x + jnp.where(rowi < A_CHUNK - s, pltpu.roll(x, SROWS - s, axis=0), 0.0)
    return x


def _c3(x):
    return x.reshape(SLAB, A_CHUNK, BLK)


def _c2(x):
    return x.reshape(SROWS, BLK)


def _split(x):
    top = lax.bitcast_convert_type(lax.bitcast_convert_type(x, jnp.uint32) & jnp.uint32(0xFFFF0000), F32)
    return top.astype(BF16), (x - top).astype(BF16)


def _lane_sum_b(x2, ones):
    hi, lo = _split(x2)
    return jnp.dot(hi, ones, preferred_element_type=F32) + jnp.dot(lo, ones, preferred_element_type=F32)


def _bmm(eq, a, b):
    ah, al = _split(a)
    bh, bl = _split(b)

    def mm(u, v):
        return jnp.einsum(eq, u, v, preferred_element_type=F32)

    return mm(ah, bh) + (mm(ah, bl) + mm(al, bh))


def _slab_rows(s):
    return pl.ds(s * SROWS, SROWS)


def _hgrn_prep(q, f, lb):
    rowi = lax.broadcasted_iota(jnp.int32, (SROWS, BLK), 0) & (A_CHUNK - 1)
    sq = _sigmoid(q)
    qc = q * sq
    sf = _sigmoid(f)
    fg = lb + (1.0 - lb) * sf
    kc = 1.0 - fg
    b = _chunk_prefix(jnp.log(fg), rowi)
    b3 = _c3(b)
    blast = b3[:, A_CHUNK - 1:A_CHUNK, :]
    eb = jnp.exp(b)
    ekb = _c2(jnp.exp(blast - b3))
    dec = jnp.exp(blast.reshape(SLAB, BLK))
    return rowi, sq, qc, sf, fg, kc, b, eb, ekb, dec


def _hgrn_slab_states(s, carry, v, ke, dec, dec_ref, u_ref, st_ref):
    dec_ref[pl.ds(s * SLAB, SLAB), :] = dec
    u_ref[...] = _bmm('ncv,nck->nvk', _c3(v), _c3(ke))

    def step(j, c):
        st_ref[j] = c
        return dec_ref[pl.ds(s * SLAB + j, 1), :] * c + u_ref[j]

    return lax.fori_loop(0, SLAB, step, carry)


def _hgrn_fwd(name, proj, lb_logits, nw, layer):
    def body(q_ref, f_ref, i_ref, g_ref, lg_ref, nw_ref, out_ref, outb_ref, raw_ref, dec_ref, u_ref, st_ref):
        lb, _ = _lbs_of(lg_ref[...], layer)
        ones = jnp.ones((BLK, BLK), BF16)
        carry = jnp.zeros((BLK, BLK), F32)
        for s in range(N_SLAB):
            rows = _slab_rows(s)
            v = i_ref[rows, :]
            rowi, sq, qc, sf, fg, kc, b, eb, ekb, dec = _hgrn_prep(q_ref[rows, :], f_ref[rows, :], lb)
            carry = _hgrn_slab_states(s, carry, v, kc * ekb, dec, dec_ref, u_ref, st_ref)
            o = _c2(_bmm('nck,nvk->ncv', _c3(qc * eb), st_ref[...]))
            qc3, kc3, b3, v3, row3 = _c3(qc), _c3(kc), _c3(b), _c3(v), _c3(rowi)
            for j in range(A_CHUNK):
                dj = jnp.exp(jnp.where(row3 >= j, b3 - b3[:, j:j + 1, :], -jnp.inf))
                a = _lane_sum_b(_c2(qc3 * dj * kc3[:, j:j + 1, :]), ones)
                o = o + a * _c2(jnp.broadcast_to(v3[:, j:j + 1, :], v3.shape))
            raw_ref[rows, :] = o
            r = lax.rsqrt(jnp.mean(o * o, -1, keepdims=True) + LN_EPS)
            gg = g_ref[rows, :]
            gated = o * r * nw_ref[...] * (gg * _sigmoid(gg))
            out_ref[rows, :] = gated
            outb_ref[rows, :] = gated.astype(BF16)

    def colblk(c0):
        return pl.BlockSpec((SEQ, BLK), lambda h: (0, c0 + h))

    return pl.pallas_call(
        body, name=name, grid=(A_HEADS,),
        in_specs=[colblk(QA0), colblk(FA0), colblk(IA0), colblk(GA0),
                  pl.BlockSpec((DEPTH, BLK), lambda h: (0, h)), pl.BlockSpec((1, BLK), lambda h: (0, 0))],
        out_specs=[colblk(0), colblk(0), colblk(0)],
        out_shape=[jax.ShapeDtypeStruct((SEQ, MIX_WIDTH), F32), jax.ShapeDtypeStruct((SEQ, MIX_WIDTH), BF16),
                   jax.ShapeDtypeStruct((SEQ, A_HEADS * BLK), F32)],
        scratch_shapes=[pltpu.VMEM((N_CHUNK, BLK), F32), pltpu.VMEM((SLAB, BLK, BLK), F32),
                        pltpu.VMEM((SLAB, BLK, BLK), F32)],
        compiler_params=_cp(("parallel",)))(proj, proj, proj, proj, lb_logits, nw.reshape(1, -1))


def _hgrn_bwd(name, proj, raw, dmix, lb_logits, nw, layer, run_after=None):
    extra = [] if run_after is None else [run_after]

    def body(q_ref, f_ref, i_ref, g_ref, raw_ref, do_ref, lg_ref, nw_ref, *rest):
        (dq_ref, df_ref, di_ref, dg_ref, dnw_ref, dlg_ref,
         dec_ref, u_ref, st_ref, h_ref, dbs_ref, dkc_ref, tot_ref) = rest[-13:]
        lb, p = _lbs_of(lg_ref[...], layer)
        ones = jnp.ones((BLK, BLK), BF16)
        nwv = nw_ref[...]

        carry = jnp.zeros((BLK, BLK), F32)
        for s in range(N_SLAB):
            rows = _slab_rows(s)
            rowi, sq, qc, sf, fg, kc, b, eb, ekb, dec = _hgrn_prep(q_ref[rows, :], f_ref[rows, :], lb)
            carry = _hgrn_slab_states(s, carry, i_ref[rows, :], kc * ekb, dec, dec_ref, u_ref,
                                      st_ref.at[pl.ds(s * SLAB, SLAB)])

        carry = jnp.zeros((BLK, BLK), F32)
        dnw = jnp.zeros((1, BLK), F32)
        for s in reversed(range(N_SLAB)):
            rows = _slab_rows(s)
            q, v = q_ref[rows, :], i_ref[rows, :]
            rowi, sq, qc, sf, fg, kc, b, eb, ekb, dec = _hgrn_prep(q, f_ref[rows, :], lb)
            ke = kc * ekb
            qe = qc * eb

            o = raw_ref[rows, :]
            gg = g_ref[rows, :]
            sgg = _sigmoid(gg)
            dout = do_ref[rows, :]
            r = lax.rsqrt(jnp.mean(o * o, -1, keepdims=True) + LN_EPS)
            oh = o * r
            dg_ref[rows, :] = (dout * oh * nwv * (sgg * (1.0 + gg * (1.0 - sgg)))).astype(BF16)
            dn = dout * (gg * sgg)
            dnw = dnw + jnp.sum(dn * oh, 0, keepdims=True)
            doh = dn * nwv
            do = r * (doh - oh * jnp.mean(doh * oh, -1, keepdims=True))
            do3, qe3, v3, ke3 = _c3(do), _c3(qe), _c3(v), _c3(ke)

            u_ref[...] = _bmm('ncv,nck->nvk', do3, qe3)

            def step(jj, c, s=s):
                j = SLAB - 1 - jj
                h_ref[j] = c
                return u_ref[j] + dec_ref[pl.ds(s * SLAB + j, 1), :] * c

            carry = lax.fori_loop(0, SLAB, step, carry)

            hh = h_ref[...]
            dqc = _c2(_bmm('ncv,nvk->nck', do3, st_ref[pl.ds(s * SLAB, SLAB)])) * eb
            dkc = _c2(_bmm('ncv,nvk->nck', v3, hh)) * ekb
            dv = _c2(_bmm('nck,nvk->ncv', ke3, hh))

            qc3, kc3, b3, row3 = _c3(qc), _c3(kc), _c3(b), _c3(rowi)
            datt_all = _bmm('niv,njv->nij', do3, v3)
            col = lax.broadcasted_iota(jnp.int32, datt_all.shape, 2)
            att_all = jnp.zeros_like(datt_all)
            for j in range(A_CHUNK):
                dj = jnp.exp(jnp.where(row3 >= j, b3 - b3[:, j:j + 1, :], -jnp.inf))
                kj = kc3[:, j:j + 1, :]
                att = _c3(_lane_sum_b(_c2(qc3 * dj * kj), ones))
                att_all = jnp.where(col == j, att[:, :, :A_CHUNK], att_all)
                md = dj * datt_all[:, :, j:j + 1]
                dqc = dqc + _c2(md * kj)
                dkc = dkc + _c2(jnp.where(row3 == j, jnp.sum(md * qc3, 1, keepdims=True), 0.0))
            dv = dv + _c2(_bmm('nij,niv->njv', att_all, do3))
            di_ref[rows, :] = dv.astype(BF16)
            dq_ref[rows, :] = (dqc * (sq * (1.0 + q * (1.0 - sq)))).astype(BF16)

            dbs = _chunk_suffix(qc * dqc - kc * dkc, rowi)
            dbs_ref[rows, :] = dbs
            dkc_ref[rows, :] = dkc
            tot_ref[pl.ds(s * SLAB, SLAB), :] = _c3(dbs)[:, 0:1, :].reshape(SLAB, BLK)
        dnw_ref[...] = jnp.broadcast_to(dnw, (8, BLK))

        rn = lax.broadcasted_iota(jnp.int32, (N_CHUNK, N_CHUNK), 0)
        cn = lax.broadcasted_iota(jnp.int32, (N_CHUNK, N_CHUNK), 1)
        tot_ref[...] = jnp.dot((cn > rn).astype(F32), tot_ref[...], preferred_element_type=F32, precision=HI)
        dlb = jnp.zeros((1, BLK), F32)
        for s in range(N_SLAB):
            rows = _slab_rows(s)
            sf = _sigmoid(f_ref[rows, :])
            fg = lb + (1.0 - lb) * sf
            later = tot_ref[pl.ds(s * SLAB, SLAB), :]
            dlg = _c2(_c3(dbs_ref[rows, :]) + later[:, None, :])
            dfg = dlg / fg - dkc_ref[rows, :]
            df_ref[rows, :] = (dfg * (1.0 - lb) * sf * (1.0 - sf)).astype(BF16)
            dlb = dlb + jnp.sum(dfg * (1.0 - sf), 0, keepdims=True)
        dlg_ref[...] = _dlogits_of(p, dlb, layer)

    def colblk(c0):
        return pl.BlockSpec((SEQ, BLK), lambda h: (0, c0 + h))

    bigb = jax.ShapeDtypeStruct((SEQ, A_HEADS * BLK), BF16)
    return pl.pallas_call(
        body, name=name, grid=(A_HEADS,),
        in_specs=[colblk(QA0), colblk(FA0), colblk(IA0), colblk(GA0), colblk(0), colblk(0),
                  pl.BlockSpec((DEPTH, BLK), lambda h: (0, h)), pl.BlockSpec((1, BLK), lambda h: (0, 0))]
        + [pl.BlockSpec(memory_space=pl.ANY)] * len(extra),
        out_specs=[colblk(0), colblk(0), colblk(0), colblk(0),
                   pl.BlockSpec((8, BLK), lambda h: (h, 0)), pl.BlockSpec((DEPTH, BLK), lambda h: (0, h))],
        out_shape=[bigb, bigb, bigb, bigb, jax.ShapeDtypeStruct((A_HEADS * 8, BLK), F32),
                   jax.ShapeDtypeStruct((DEPTH, A_HEADS * BLK), F32)],
        scratch_shapes=[pltpu.VMEM((N_CHUNK, BLK), F32), pltpu.VMEM((SLAB, BLK, BLK), F32),
                        pltpu.VMEM((N_CHUNK, BLK, BLK), F32), pltpu.VMEM((SLAB, BLK, BLK), F32),
                        pltpu.VMEM((SEQ, BLK), F32), pltpu.VMEM((SEQ, BLK), F32), pltpu.VMEM((N_CHUNK, BLK), F32)],
        compiler_params=_cp(("parallel",)))(proj, proj, proj, proj, raw, dmix, lb_logits, nw.reshape(1, -1), *extra)


SCALE = HEAD_DIM ** -0.5


def _rope_tables():
    half = ROPE_DIM // 2
    inv = ROPE_THETA ** (-jnp.arange(0, ROPE_DIM, 2, dtype=F32) / ROPE_DIM)
    ang = jnp.arange(SEQ, dtype=F32)[:, None] * inv[None, :]
    cos, sin = jnp.cos(ang), jnp.sin(ang)
    pad = jnp.zeros((SEQ, HEAD_DIM - ROPE_DIM), F32)
    zero = jnp.zeros((SEQ, half), F32)
    c = jnp.concatenate([cos, cos, pad + 1.0], 1)
    s_lo = jnp.concatenate([zero, sin, pad], 1)
    s_hi = jnp.concatenate([-sin, zero, pad], 1)
    return c, s_lo, s_hi


def _rope(x, c, s_lo, s_hi):
    half = ROPE_DIM // 2
    return x * c + pltpu.roll(x, half, axis=1) * s_lo + pltpu.roll(x, HEAD_DIM - half, axis=1) * s_hi


def _unrope(dy, c, s_lo, s_hi):
    half = ROPE_DIM // 2
    return dy * c + pltpu.roll(dy * s_lo, HEAD_DIM - half, axis=1) + pltpu.roll(dy * s_hi, half, axis=1)


def _rows(start, size, stride):
    return pl.ds(start, size) if stride == 1 else pl.ds(start, size, stride=stride)


def _band_blocks(patterns):
    out = []
    for p, (max_lag, dil) in enumerate(patterns):
        nb = SEQ // dil // BLK
        for r in range(dil):
            for n in range(nb):
                lo = max(n - 1, 0)
                kn = (n - lo + 1) * BLK
                out.append((p, _rows(r + n * BLK * dil, BLK, dil), _rows(r + lo * BLK * dil, kn, dil), kn,
                            (n - lo) * BLK, max_lag))
    return out


def _band_valid(kn, off, max_lag):
    lag = off + lax.broadcasted_iota(jnp.int32, (BLK, kn), 0) - lax.broadcasted_iota(jnp.int32, (BLK, kn), 1)
    return (lag >= 0) & (lag <= max_lag)


def _attn_fwd(name, proj, tables, sink_b, mixed, mixed_bf, *, n_heads, rep, q0, k0, v0, m0, patterns):
    n_pat = len(patterns)
    blocks = _band_blocks(patterns)
    has_sink = sink_b is not None

    def body(*refs):
        o_ref, ob_ref, l_ref, qr, kr, op, lse_ref = refs[-7:]
        q_ref, k_ref, v_ref, c_ref, sl_ref, sh_ref = refs[:6]
        if has_sink:
            sk = refs[6][0:1, 0:1]
        c, s_lo, s_hi = c_ref[...], sl_ref[...], sh_ref[...]
        qr[...] = _rope(q_ref[...], c, s_lo, s_hi)
        kr[...] = _rope(k_ref[...], c, s_lo, s_hi)
        for p, qrows, krows, kn, off, max_lag in blocks:
            qb = qr[qrows, :].astype(BF16)
            kb = kr[krows, :].astype(BF16)
            vb = v_ref[krows, :].astype(BF16)
            s = lax.dot_general(qb, kb, NT, preferred_element_type=F32) * SCALE
            s = jnp.where(_band_valid(kn, off, max_lag), s, -jnp.inf)
            m = jnp.max(s, -1, keepdims=True)
            if has_sink:
                m = jnp.maximum(m, sk)
            e = jnp.exp(s - m)
            den = jnp.sum(e, -1, keepdims=True)
            if has_sink:
                den = den + jnp.exp(sk - m)
            o = jnp.dot(e.astype(BF16), vb, preferred_element_type=F32) / den
            op.at[p][qrows, :] = o
            lse_ref.at[p][qrows, :] = jnp.broadcast_to(m + jnp.log(den), (BLK, BLK))
        if n_pat == 1:
            acc = op[0]
            l_ref[...] = lse_ref[0]
        else:
            ls = [lse_ref[p] for p in range(n_pat)]
            m = functools.reduce(jnp.maximum, ls)
            es = [jnp.exp(l - m) for l in ls]
            tot = functools.reduce(jnp.add, es)
            acc = None
            for p in range(n_pat):
                t = (es[p] / tot) * op[p]
                acc = t if acc is None else acc + t
            l_ref[...] = m + jnp.log(tot)
        o_ref[...] = acc
        ob_ref[...] = acc.astype(BF16)

    def colblk(fn):
        return pl.BlockSpec((SEQ, BLK), fn)

    tab = pl.BlockSpec((SEQ, BLK), lambda h: (0, 0))
    in_specs = [colblk(lambda h: (0, q0 + h)), colblk(lambda h: (0, k0 + h // rep)), colblk(lambda h: (0, v0 + h // rep)),
                tab, tab, tab]
    args = [proj, proj, proj, *tables]
    if has_sink:
        in_specs.append(pl.BlockSpec((None, 8, BLK), lambda h: (h, 0, 0)))
        args.append(sink_b)
    n_in = len(args)
    in_specs += [pl.BlockSpec(memory_space=pl.ANY)] * 2
    args += [mixed, mixed_bf]
    pat = pltpu.VMEM((n_pat, SEQ, BLK), F32)
    return pl.pallas_call(
        body, name=name, grid=(n_heads,), in_specs=in_specs,
        out_specs=[colblk(lambda h: (0, m0 + h)), colblk(lambda h: (0, m0 + h)),
                   pl.BlockSpec((None, SEQ, BLK), lambda h: (h, 0, 0))],
        out_shape=[jax.ShapeDtypeStruct(mixed.shape, F32), jax.ShapeDtypeStruct(mixed.shape, BF16),
                   jax.ShapeDtypeStruct((n_heads, SEQ, BLK), F32)],
        input_output_aliases={n_in: 0, n_in + 1: 1},
        scratch_shapes=[pltpu.VMEM((SEQ, BLK), F32), pltpu.VMEM((SEQ, BLK), F32), pat, pat],
        compiler_params=_cp(("parallel",)))(*args)


def _attn_bwd(name, proj, mixed, dmix, lse, tables, sink_b, *, n_kv, rep, q0, k0, v0, m0, patterns):
    n_pat = len(patterns)
    n_heads = n_kv * rep
    blocks = _band_blocks(patterns)
    has_sink = sink_b is not None

    def body(*refs):
        if has_sink:
            (q_ref, k_ref, v_ref, o_ref, do_ref, lse_ref, c_ref, sl_ref, sh_ref, sink_ref,
             dq_ref, dk_ref, dv_ref, dsk_ref, qr, kr, dqa, dka, dva, dd) = refs
        else:
            (q_ref, k_ref, v_ref, o_ref, do_ref, lse_ref, c_ref, sl_ref, sh_ref,
             dq_ref, dk_ref, dv_ref, dsk_ref, qr, kr, dqa, dka, dva, dd) = refs
        j = pl.program_id(1)
        c, s_lo, s_hi = c_ref[...], sl_ref[...], sh_ref[...]
        qr[...] = _rope(q_ref[...], c, s_lo, s_hi)
        kr[...] = _rope(k_ref[...], c, s_lo, s_hi)
        dcol = jnp.sum(do_ref[...] * o_ref[...], -1, keepdims=True)
        dd[...] = jnp.broadcast_to(dcol, (SEQ, BLK))
        dqa[...] = jnp.zeros((SEQ, BLK), F32)

        @pl.when(j == 0)
        def _():
            dka[...] = jnp.zeros((SEQ, BLK), F32)
            dva[...] = jnp.zeros((SEQ, BLK), F32)

        for p, qrows, krows, kn, off, max_lag in blocks:
            qb = qr[qrows, :].astype(BF16)
            kb = kr[krows, :].astype(BF16)
            vb = v_ref[krows, :].astype(BF16)
            dob = do_ref[qrows, :].astype(BF16)
            lcol = lse_ref[qrows, :][:, 0:1]
            dcb = dd[qrows, :][:, 0:1]
            s = lax.dot_general(qb, kb, NT, preferred_element_type=F32) * SCALE
            a = jnp.where(_band_valid(kn, off, max_lag), jnp.exp(s - lcol), 0.0)
            dp = lax.dot_general(dob, vb, NT, preferred_element_type=F32)
            ds = (a * (dp - dcb) * SCALE).astype(BF16)
            dqa[qrows, :] += jnp.dot(ds, kb, preferred_element_type=F32)
            dka[krows, :] += lax.dot_general(ds, qb, TN, preferred_element_type=F32)
            dva[krows, :] += lax.dot_general(a.astype(BF16), dob, TN, preferred_element_type=F32)

        if has_sink:
            sk = sink_ref[0:1, 0:1]
            ps = jnp.exp(sk - lse_ref[...][:, 0:1])
            dsk_ref[...] = jnp.full((8, BLK), -jnp.sum(ps * dcol), F32)
        else:
            dsk_ref[...] = jnp.zeros((8, BLK), F32)
        dq_ref[...] = _unrope(dqa[...], c, s_lo, s_hi).astype(BF16)

        @pl.when(j == rep - 1)
        def _():
            dk_ref[...] = _unrope(dka[...], c, s_lo, s_hi).astype(BF16)
            dv_ref[...] = dva[...].astype(BF16)

    def colblk(fn):
        return pl.BlockSpec((SEQ, BLK), fn)

    tab = pl.BlockSpec((SEQ, BLK), lambda g, j: (0, 0))
    in_specs = [colblk(lambda g, j: (0, q0 + g * rep + j)), colblk(lambda g, j: (0, k0 + g)), colblk(lambda g, j: (0, v0 + g)),
                colblk(lambda g, j: (0, m0 + g * rep + j)), colblk(lambda g, j: (0, m0 + g * rep + j)),
                pl.BlockSpec((None, SEQ, BLK), lambda g, j: (g * rep + j, 0, 0)), tab, tab, tab]
    args = [proj, proj, proj, mixed, dmix, lse, *tables]
    if has_sink:
        in_specs.append(pl.BlockSpec((None, 8, BLK), lambda g, j: (g * rep + j, 0, 0)))
        args.append(sink_b)
    acc = pltpu.VMEM((SEQ, BLK), F32)
    return pl.pallas_call(
        body, name=name, grid=(n_kv, rep), in_specs=in_specs,
        out_specs=[colblk(lambda g, j: (0, g * rep + j)), colblk(lambda g, j: (0, g)), colblk(lambda g, j: (0, g)),
                   pl.BlockSpec((None, 8, BLK), lambda g, j: (g * rep + j, 0, 0))],
        out_shape=[jax.ShapeDtypeStruct((SEQ, n_heads * BLK), BF16), jax.ShapeDtypeStruct((SEQ, n_kv * BLK), BF16),
                   jax.ShapeDtypeStruct((SEQ, n_kv * BLK), BF16), jax.ShapeDtypeStruct((n_heads, 8, BLK), F32)],
        scratch_shapes=[acc, acc, acc, acc, acc, acc],
        compiler_params=_cp(("parallel", "arbitrary")))(*args)


B_PATTERNS = tuple((w // d, d) for w, d in DILATED_PATTERNS)
C_PATTERNS = ((C_WINDOW - 1, 1),)


ANY = pl.BlockSpec(memory_space=pl.ANY)
CHIP_MASKS = ((1, 0), (0, 1), (1, 1))


def _coords():
    return lax.axis_index("x"), lax.axis_index("y"), lax.axis_index("c")


def _flip(v, m):
    return 1 - v if m else v


def _into_slot(name, w, layer, k_idx, dtype, run_after=None):
    _, rows, cols = w.shape
    tr = rows // 8 if rows % 64 == 0 else rows

    def body(k_ref, w_ref, *rest):
        rest[-1][...] = w_ref[...].astype(dtype)

    in_specs = [pl.BlockSpec((None, tr, cols), lambda i, k: (layer, i, 0))]
    args = [k_idx, w]
    if run_after is not None:
        in_specs.append(pl.BlockSpec(memory_space=pl.ANY))
        args.append(run_after)
    return pl.pallas_call(
        body, name=name,
        grid_spec=pltpu.PrefetchScalarGridSpec(
            num_scalar_prefetch=1, grid=(rows // tr,), in_specs=in_specs,
            out_specs=pl.BlockSpec((None, tr, cols), lambda i, k: (k[0], i, 0))),
        out_shape=jax.ShapeDtypeStruct((N_CHIPS, rows, cols), dtype),
        compiler_params=_cp(("parallel",)))(*args)


HBM_SPEC = pl.BlockSpec(memory_space=pltpu.HBM)
SEM_SPEC = pl.BlockSpec(memory_space=pltpu.SEMAPHORE)
TOKEN_SPEC = pl.BlockSpec(memory_space=pltpu.VMEM)
TOKEN_SHAPE = jax.ShapeDtypeStruct((8, BLK), F32)
DATAFLOW = pltpu.SideEffectType.DATAFLOW_SIDE_EFFECTING


def _hbm(a):
    return pltpu.with_memory_space_constraint(a, pltpu.HBM)


def _hbm_like(bufs):
    return [pltpu.HBM(b.shape, b.dtype) for b in bufs]


def _gather_start(name, stages):
    flat = [b for st in stages for b in st]
    n, ns = len(flat), len(stages)

    def body(*refs):
        ins = refs[:n]
        sems = refs[n:n + 2 * ns]
        token = refs[-1]
        x, y, c = _coords()
        k_me = 2 * x + y
        a = 0
        for s, st in enumerate(stages):
            for i in range(len(st)):
                mine = ins[a].at[k_me, c]
                for m, (mx, my) in enumerate(CHIP_MASKS):
                    pltpu.make_async_remote_copy(src_ref=mine, dst_ref=mine, send_sem=sems[2 * s].at[i * 3 + m],
                                                 recv_sem=sems[2 * s + 1].at[i * 3 + m],
                                                 device_id=(_flip(x, mx), _flip(y, my), c), device_id_type=MESH).start()
                a += 1
        token[...] = jnp.zeros_like(token)

    sem_shapes = []
    for st in stages:
        sem_shapes += [pltpu.SemaphoreType.DMA((3 * len(st),))] * 2
    out = pl.pallas_call(
        body, name=name, in_specs=[HBM_SPEC] * n,
        out_specs=tuple([SEM_SPEC] * (2 * ns) + [HBM_SPEC] * n + [TOKEN_SPEC]),
        out_shape=tuple(sem_shapes + _hbm_like(flat) + [TOKEN_SHAPE]),
        input_output_aliases={i: 2 * ns + i for i in range(n)},
        compiler_params=pltpu.CompilerParams(has_side_effects=DATAFLOW),
    )(*[_hbm(b) for b in flat])
    sems, bufs, token = out[:2 * ns], out[2 * ns:2 * ns + n], out[-1]
    res, a = [], 0
    for s, st in enumerate(stages):
        res.append((sems[2 * s], sems[2 * s + 1], list(bufs[a:a + len(st)])))
        a += len(st)
    return res, token


def _gather_forward(name, stage, after):
    ssem_in, rsem_in, bufs = stage
    n = len(bufs)

    def body(*refs):
        ins = refs[:n]
        s_in, r_in, _ = refs[n:n + 3]
        s_out, r_out = refs[n + 3:n + 5]
        token = refs[-1]
        x, y, c = _coords()
        for i in range(n):
            for m, (mx, my) in enumerate(CHIP_MASKS):
                kp = 2 * _flip(x, mx) + _flip(y, my)
                blk = ins[i].at[kp, c]
                got = pltpu.make_async_remote_copy(src_ref=blk, dst_ref=blk, send_sem=s_in.at[i * 3 + m],
                                                   recv_sem=r_in.at[i * 3 + m], device_id=(x, y, 1 - c), device_id_type=MESH)
                got.wait_send()
                got.wait_recv()
                pltpu.make_async_remote_copy(src_ref=blk, dst_ref=blk, send_sem=s_out.at[i * 3 + m],
                                             recv_sem=r_out.at[i * 3 + m], device_id=(x, y, 1 - c), device_id_type=MESH).start()
        token[...] = jnp.zeros_like(token)

    sem = pltpu.SemaphoreType.DMA((3 * n,))
    out = pl.pallas_call(
        body, name=name, in_specs=[HBM_SPEC] * n + [SEM_SPEC, SEM_SPEC, ANY],
        out_specs=tuple([SEM_SPEC, SEM_SPEC] + [HBM_SPEC] * n + [TOKEN_SPEC]),
        out_shape=tuple([sem, sem] + _hbm_like(bufs) + [TOKEN_SHAPE]),
        input_output_aliases={i: 2 + i for i in range(n)},
        compiler_params=pltpu.CompilerParams(has_side_effects=DATAFLOW),
    )(*bufs, ssem_in, rsem_in, after)
    return (out[0], out[1], list(out[2:2 + n])), out[-1]


def _gather_wait(name, stage, after):
    ssem, rsem, bufs = stage
    n = len(bufs)

    def body(*refs):
        ins = refs[:n]
        s_in, r_in, _ = refs[n:n + 3]
        x, y, c = _coords()
        for i in range(n):
            for m, (mx, my) in enumerate(CHIP_MASKS):
                kp = 2 * _flip(x, mx) + _flip(y, my)
                sent, got = ins[i].at[kp, c], ins[i].at[kp, 1 - c]
                cp = pltpu.make_async_remote_copy(src_ref=sent, dst_ref=got, send_sem=s_in.at[i * 3 + m],
                                                  recv_sem=r_in.at[i * 3 + m], device_id=(x, y, 1 - c), device_id_type=MESH)
                cp.wait_send()
                cp.wait_recv()

    out = pl.pallas_call(
        body, name=name, in_specs=[HBM_SPEC] * n + [SEM_SPEC, SEM_SPEC, ANY],
        out_specs=tuple([HBM_SPEC] * n), out_shape=tuple(_hbm_like(bufs)),
        input_output_aliases={i: i for i in range(n)},
        compiler_params=pltpu.CompilerParams(has_side_effects=DATAFLOW),
    )(*bufs, ssem, rsem, after)
    return list(out)


def _swap_start(name, grads):
    n = len(grads)

    def body(*refs):
        ins, lands = refs[:n], refs[n:2 * n]
        ssem, rsem = refs[2 * n:2 * n + 2]
        x, y, c = _coords()
        for a in range(n):
            for j in range(N_CHIPS):
                pltpu.make_async_remote_copy(src_ref=ins[a].at[j, 1 - c], dst_ref=lands[a].at[j],
                                             send_sem=ssem.at[a * N_CHIPS + j], recv_sem=rsem.at[a * N_CHIPS + j],
                                             device_id=(x, y, 1 - c), device_id_type=MESH).start()

    sem = pltpu.SemaphoreType.DMA((N_CHIPS * n,))
    land_shapes = [pltpu.HBM((N_CHIPS,) + g.shape[2:], g.dtype) for g in grads]
    out = pl.pallas_call(
        body, name=name, in_specs=[HBM_SPEC] * (2 * n),
        out_specs=tuple([SEM_SPEC, SEM_SPEC] + [HBM_SPEC] * (2 * n)),
        out_shape=tuple([sem, sem] + _hbm_like(grads) + land_shapes),
        input_output_aliases={i: 2 + i for i in range(2 * n)},
        compiler_params=pltpu.CompilerParams(has_side_effects=DATAFLOW),
    )(*[_hbm(g) for g in grads], *[_hbm(lax.empty((N_CHIPS,) + g.shape[2:], g.dtype)) for g in grads])
    return out[0], out[1], list(out[2:2 + n]), list(out[2 + n:])


def _swap_wait(name, started, after):
    ssem, rsem, grads, lands = started
    n = len(grads)

    def body(*refs):
        ins, lnd = refs[:n], refs[n:2 * n]
        s_in, r_in, _ = refs[2 * n:2 * n + 3]
        x, y, c = _coords()
        for a in range(n):
            for j in range(N_CHIPS):
                cp = pltpu.make_async_remote_copy(src_ref=ins[a].at[j, 1 - c], dst_ref=lnd[a].at[j],
                                                  send_sem=s_in.at[a * N_CHIPS + j], recv_sem=r_in.at[a * N_CHIPS + j],
                                                  device_id=(x, y, 1 - c), device_id_type=MESH)
                cp.wait_send()
                cp.wait_recv()

    out = pl.pallas_call(
        body, name=name, in_specs=[HBM_SPEC] * (2 * n) + [SEM_SPEC, SEM_SPEC, ANY],
        out_specs=tuple([HBM_SPEC] * (2 * n)), out_shape=tuple(_hbm_like(grads) + _hbm_like(lands)),
        input_output_aliases={i: i for i in range(2 * n)},
        compiler_params=pltpu.CompilerParams(has_side_effects=DATAFLOW),
    )(*grads, *lands, ssem, rsem, after)
    return list(out[:n]), list(out[n:])


def _scatter_start(name, parts):
    n = len(parts)

    def body(*refs):
        ins, lands = refs[:n], refs[n:2 * n]
        ssem, rsem = refs[2 * n:2 * n + 2]
        x, y, c = _coords()
        k_me = 2 * x + y
        for a in range(n):
            for m, (mx, my) in enumerate(CHIP_MASKS):
                px, py = _flip(x, mx), _flip(y, my)
                pltpu.make_async_remote_copy(src_ref=ins[a].at[2 * px + py], dst_ref=lands[a].at[k_me],
                                             send_sem=ssem.at[a * 3 + m], recv_sem=rsem.at[a * 3 + m],
                                             device_id=(px, py, c), device_id_type=MESH).start()

    sem = pltpu.SemaphoreType.DMA((3 * n,))
    out = pl.pallas_call(
        body, name=name, in_specs=[HBM_SPEC] * (2 * n),
        out_specs=tuple([SEM_SPEC, SEM_SPEC] + [HBM_SPEC] * (2 * n)),
        out_shape=tuple([sem, sem] + _hbm_like(parts) + _hbm_like(parts)),
        input_output_aliases={i: 2 + i for i in range(2 * n)},
        compiler_params=pltpu.CompilerParams(has_side_effects=DATAFLOW),
    )(*[_hbm(p) for p in parts], *[_hbm(lax.empty(p.shape, p.dtype)) for p in parts])
    return out[0], out[1], list(out[2:2 + n]), list(out[2 + n:])


def _scatter_wait(name, started, after):
    ssem, rsem, parts, lands = started
    n = len(parts)

    def body(*refs):
        ins, lnd = refs[:n], refs[n:2 * n]
        s_in, r_in, _ = refs[2 * n:2 * n + 3]
        x, y, c = _coords()
        k_me = 2 * x + y
        for a in range(n):
            for m, (mx, my) in enumerate(CHIP_MASKS):
                px, py = _flip(x, mx), _flip(y, my)
                cp = pltpu.make_async_remote_copy(src_ref=ins[a].at[2 * px + py], dst_ref=lnd[a].at[k_me],
                                                  send_sem=s_in.at[a * 3 + m], recv_sem=r_in.at[a * 3 + m],
                                                  device_id=(px, py, c), device_id_type=MESH)
                cp.wait_send()
                cp.wait_recv()

    out = pl.pallas_call(
        body, name=name, in_specs=[HBM_SPEC] * (2 * n) + [SEM_SPEC, SEM_SPEC, ANY],
        out_specs=tuple([HBM_SPEC] * (2 * n)), out_shape=tuple(_hbm_like(parts) + _hbm_like(lands)),
        input_output_aliases={i: i for i in range(2 * n)},
        compiler_params=pltpu.CompilerParams(has_side_effects=DATAFLOW),
    )(*parts, *lands, ssem, rsem, after)
    return list(out[:n]), list(out[n:])


def _pair_gather(name, bufs):
    n = len(bufs)

    def body(*refs):
        outs = refs[n:2 * n]
        ssem, rsem = refs[2 * n:]
        x, y, c = _coords()
        cps = []
        for a in range(n):
            mine = outs[a].at[c]
            cp = pltpu.make_async_remote_copy(src_ref=mine, dst_ref=mine, send_sem=ssem.at[a],
                                              recv_sem=rsem.at[a], device_id=(x, y, 1 - c), device_id_type=MESH)
            cp.start()
            cps.append(cp)
        for cp in cps:
            cp.wait()

    return pl.pallas_call(
        body, name=name, in_specs=[ANY] * n, out_specs=[ANY] * n,
        out_shape=[jax.ShapeDtypeStruct(b.shape, b.dtype) for b in bufs],
        input_output_aliases={a: a for a in range(n)},
        scratch_shapes=[pltpu.SemaphoreType.DMA((n,)), pltpu.SemaphoreType.DMA((n,))],
        compiler_params=pltpu.CompilerParams(has_side_effects=True),
    )(*bufs)


DEV_MASKS = tuple((mx, my, mc) for mx in (0, 1) for my in (0, 1) for mc in (0, 1) if (mx, my, mc) != (0, 0, 0))


def _gather_small(buf, run_after):
    def body(in_ref, _, out_ref, ssem, rsem, lsem):
        x, y, c = _coords()
        me = 4 * x + 2 * y + c
        cps = [pltpu.make_async_copy(in_ref, out_ref.at[me], lsem)]
        cps[0].start()
        for t, (mx, my, mc) in enumerate(DEV_MASKS):
            cp = pltpu.make_async_remote_copy(src_ref=in_ref, dst_ref=out_ref.at[me], send_sem=ssem.at[t],
                                              recv_sem=rsem.at[t], device_id=(_flip(x, mx), _flip(y, my), _flip(c, mc)),
                                              device_id_type=MESH)
            cp.start()
            cps.append(cp)
        for cp in cps:
            cp.wait()

    return pl.pallas_call(
        body, name="gather_small", in_specs=[ANY, ANY], out_specs=ANY,
        out_shape=jax.ShapeDtypeStruct((N_DEV,) + buf.shape, buf.dtype),
        scratch_shapes=[pltpu.SemaphoreType.DMA((N_DEV - 1,)), pltpu.SemaphoreType.DMA((N_DEV - 1,)),
                        pltpu.SemaphoreType.DMA(())],
        compiler_params=pltpu.CompilerParams(has_side_effects=True),
    )(buf, run_after)


def _row_tile(rows):
    return rows // 2 if rows % 16 == 0 else rows


def _pair_add(name, grad, got, c_idx):
    _, _, r2, cols = grad.shape
    tr = _row_tile(r2)

    def body(c_ref, a_ref, b_ref, o_ref):
        o_ref[...] = (a_ref[...].astype(F32) + b_ref[...].astype(F32)).astype(BF16)

    return pl.pallas_call(
        body, name=name,
        grid_spec=pltpu.PrefetchScalarGridSpec(
            num_scalar_prefetch=1, grid=(N_CHIPS, r2 // tr),
            in_specs=[pl.BlockSpec((None, None, tr, cols), lambda j, i, c: (j, c[0], i, 0)),
                      pl.BlockSpec((None, tr, cols), lambda j, i, c: (j, i, 0))],
            out_specs=pl.BlockSpec((None, tr, cols), lambda j, i, c: (j, i, 0))),
        out_shape=jax.ShapeDtypeStruct((N_CHIPS, r2, cols), BF16),
        compiler_params=_cp(("parallel", "parallel")))(c_idx, grad, got)


def _chip_add(name, part, got, kc_idx):
    _, r2, cols = got.shape
    tr = _row_tile(r2)

    def body(k_ref, p_ref, g1_ref, g2_ref, g3_ref, o_ref):
        acc = p_ref[...].astype(F32)
        for g_ref in (g1_ref, g2_ref, g3_ref):
            acc = acc + g_ref[...].astype(F32)
        o_ref[...] = acc

    def slot(d):
        return pl.BlockSpec((None, tr, cols), lambda i, k: ((k[0] + d) % N_CHIPS, i, 0))

    return pl.pallas_call(
        body, name=name,
        grid_spec=pltpu.PrefetchScalarGridSpec(
            num_scalar_prefetch=1, grid=(r2 // tr,),
            in_specs=[slot(0), slot(1), slot(2), slot(3)],
            out_specs=pl.BlockSpec((None, tr, cols), lambda i, k: (k[1], i, 0))),
        out_shape=jax.ShapeDtypeStruct((2, r2, cols), F32),
        compiler_params=_cp(("parallel",)))(kc_idx, part, got, got, got)


def _adam_math(w, g, m, v):
    m2 = ADAM_B1 * m + (1.0 - ADAM_B1) * g
    v2 = ADAM_B2 * v + (1.0 - ADAM_B2) * (g * g)
    m_hat = m2 / (1.0 - ADAM_B1 ** ADAM_STEP)
    v_hat = v2 / (1.0 - ADAM_B2 ** ADAM_STEP)
    delta = -ADAM_LR * (m_hat / (jnp.sqrt(v_hat) + ADAM_EPS) + ADAM_WD * w)
    return delta, m2, v2


def _adamw_matrix(name, w, g_layers, m, v):
    _, rows, cols = w.shape
    tr = rows // 16 if rows % 128 == 0 else rows // 8

    def body(w_ref, g0_ref, g1_ref, m_ref, v_ref, go_ref, d_ref, mo_ref, vo_ref):
        g = jnp.where(pl.program_id(0) == 0, g0_ref[...], g1_ref[...])
        go_ref[...] = g
        d_ref[...], mo_ref[...], vo_ref[...] = _adam_math(w_ref[...], g, m_ref[...], v_ref[...])

    lay = pl.BlockSpec((None, tr, cols), lambda l, i: (l, i, 0))
    flat = pl.BlockSpec((tr, cols), lambda l, i: (i, 0))
    shp = jax.ShapeDtypeStruct(w.shape, F32)
    return pl.pallas_call(body, name=name, grid=(DEPTH, rows // tr), in_specs=[lay, flat, flat, lay, lay],
                          out_specs=[lay, lay, lay, lay], out_shape=[shp, shp, shp, shp],
                          compiler_params=_cp(("parallel", "parallel")))(w, g_layers[0], g_layers[1], m, v)


def _sum_small(gathered):
    def body(g_ref, o_ref):
        acc = g_ref[0]
        for d in range(1, N_DEV):
            acc = acc + g_ref[d]
        o_ref[...] = acc

    return pl.pallas_call(body, name="sum_small", out_shape=jax.ShapeDtypeStruct(gathered.shape[1:], F32),
                          compiler_params=_cp())(gathered)


def _adamw_small(w, g, m, v):
    def body(w_ref, g_ref, m_ref, v_ref, d_ref, mo_ref, vo_ref):
        d_ref[...], mo_ref[...], vo_ref[...] = _adam_math(w_ref[...], g_ref[...], m_ref[...], v_ref[...])

    shp = jax.ShapeDtypeStruct(w.shape, F32)
    return pl.pallas_call(body, name="adamw_small", out_shape=[shp, shp, shp], compiler_params=_cp())(w, g, m, v)


def _pack(arrays, rows):
    flat = jnp.concatenate([a.reshape(-1) for a in arrays])
    return jnp.pad(flat, (0, rows * BLK - flat.shape[0])).reshape(rows, BLK)


def _unpack(buf, shapes):
    flat = buf.reshape(-1)
    out, pos = [], 0
    for s in shapes:
        n = math.prod(s)
        out.append(flat[pos:pos + n].reshape(s))
        pos += n
    return out


def _rows_for(shapes):
    n = sum(math.prod(s) for s in shapes)
    return -(-n // (8 * BLK)) * 8


def _rs_swap(tag, grads):
    return _swap_start(f"rs_swap_start{tag}", [g.reshape(N_CHIPS, 2, g.shape[1] // 2, g.shape[2]) for g in grads])


def _rs_scatter(tag, swapping, after, c_idx):
    split, got = _swap_wait(f"rs_swap_wait{tag}", swapping, after)
    parts = [_pair_add(f"rs_pair_add{tag}_{i}", s, r, c_idx) for i, (s, r) in enumerate(zip(split, got))]
    return _scatter_start(f"rs_scatter_start{tag}", parts)


def _rs_end(tag, started, after, kc_idx):
    parts, lands = _scatter_wait(f"rs_scatter_wait{tag}", started, after)
    halves = [_chip_add(f"rs_chip_add{tag}_{i}", p, r, kc_idx) for i, (p, r) in enumerate(zip(parts, lands))]
    full = _pair_gather(f"rs_pair_gather{tag}", halves)
    return [f.reshape(2 * f.shape[1], f.shape[2]) for f in full]


def kernel(x, w_in, lb_logits, a_norm_w, c_sinks, w_out, ln1_g, ln1_b, w_gate, w_up, conv_w, conv_b, w_down, ln2_g, ln2_b, loss_target, m_w_in, m_lb_logits, m_a_norm_w, m_c_sinks, m_w_out, m_ln1_g, m_ln1_b, m_w_gate, m_w_up, m_conv_w, m_conv_b, m_w_down, m_ln2_g, m_ln2_b, v_w_in, v_lb_logits, v_a_norm_w, v_c_sinks, v_w_out, v_ln1_g, v_ln1_b, v_w_gate, v_w_up, v_conv_w, v_conv_b, v_w_down, v_ln2_g, v_ln2_b):
    cx, cy, cc = _coords()
    c_idx = jnp.reshape(cc, (1,)).astype(jnp.int32)
    k_me = 2 * cx + cy
    k_idx = jnp.reshape(k_me, (1,)).astype(jnp.int32)
    kc_idx = jnp.stack([k_me, cc]).astype(jnp.int32)

    def slot(nm, w, l, run_after=None):
        b = _into_slot(f"slot_{nm}{l}", w, l, k_idx, BF16, run_after)
        return b.reshape(N_CHIPS, 2, b.shape[1] // 2, b.shape[2])

    cw_slot = _into_slot("slot_cw", conv_w.reshape(1, DEPTH * CONV_WIDTH, FF_SHARD), 0, k_idx, F32)
    cw_slot = cw_slot.reshape(N_CHIPS, DEPTH, CONV_WIDTH, FF_SHARD)
    first, token = _gather_start("gather_start0", [[slot("wi", w_in, 0), cw_slot]])
    sl = [{nm: slot(nm, w, l, token) for nm, w in (("wi", w_in), ("wo", w_out), ("wg", w_gate), ("wu", w_up), ("wd", w_down))
           if (nm, l) != ("wi", 0)} for l in range(DEPTH)]
    order = [(l, nm) for l in range(DEPTH) for nm in ("wi", "wo", "wg", "wu", "wd")][1:]
    rest, token = _gather_start("gather_start1", [[sl[l][nm]] for l, nm in order])
    stage_of = {key: st for key, st in zip(order, rest)}

    def mat(b):
        return b.reshape(N_CHIPS, 2 * b.shape[2], b.shape[3])

    fwd0, token = _gather_forward("gather_fwd0", first[0], token)
    wi0, cw_all = _gather_wait("gather_wait0", fwd0, token)
    cw_full = jnp.transpose(cw_all, (1, 2, 0, 3)).reshape(DEPTH, CONV_WIDTH, D_FF)
    tables = _rope_tables()

    passing = {}

    def pass_on(l, nm, after):
        passing[(l, nm)] = _gather_forward(f"gather_fwd_{nm}{l}", stage_of[(l, nm)], after)

    def arrived(l, nm, after):
        i = order.index((l, nm))
        if i + 1 < len(order):
            pass_on(*order[i + 1], after)
            after = passing[order[i + 1]][1]
        return mat(_gather_wait(f"gather_wait_{nm}{l}", passing[(l, nm)][0], after)[0])

    h = x[0]
    h_bf = _to_bf16("x_bf16", h)
    saved = []
    weights = []
    for l in range(DEPTH):
        wi = mat(wi0) if l == 0 else arrived(l, "wi", h)
        proj = _fwd_colsharded(f"proj{l}", h_bf, wi)
        mixed, mixed_bf, raw = _hgrn_fwd(f"hgrn_fwd{l}", proj, lb_logits, a_norm_w[l], l)
        mixed, mixed_bf, lse_b = _attn_fwd(f"dilated_fwd{l}", proj, tables, None, mixed, mixed_bf, n_heads=B_HEADS, rep=1,
                                           q0=QB0, k0=KB0, v0=VB0, m0=A_HEADS, patterns=B_PATTERNS)
        if l == 0:
            pass_on(l, "wo", lse_b)
        sink_b = jnp.broadcast_to(c_sinks[l][:, None, None], (C_HEADS, 8, BLK))
        mixed, mixed_bf, lse_c = _attn_fwd(f"window_fwd{l}", proj, tables, sink_b, mixed, mixed_bf, n_heads=C_HEADS,
                                           rep=C_HEADS // C_KV_HEADS, q0=QC0, k0=KC0, v0=VC0, m0=A_HEADS + B_HEADS,
                                           patterns=C_PATTERNS)
        wo = arrived(l, "wo", lse_c)
        y1 = _fwd_rowsharded(f"wout{l}", mixed_bf, wo, OUT_SHARD)
        x1, x1_bf = _ln_fwd(f"ln1_fwd{l}", h, y1, ln1_g[l], ln1_b[l])
        wg = arrived(l, "wg", x1)
        g = _fwd_colsharded(f"gate{l}", x1_bf, wg, BF16)
        wu = arrived(l, "wu", g)
        u = _fwd_colsharded(f"up{l}", x1_bf, wu, BF16)
        hh = _conv_gate_fwd(f"conv_fwd{l}", g, u, cw_full[l], conv_b[l])
        wd = arrived(l, "wd", hh)
        y2 = _fwd_rowsharded(f"down{l}", hh, wd, FF_SHARD)
        x2, x2_bf = _ln_fwd(f"ln2_fwd{l}", x1, y2, ln2_g[l], ln2_b[l])
        weights.append(dict(wi=wi, wo=wo, wg=wg, wu=wu, wd=wd))
        saved.append((h, h_bf, proj, raw, lse_b, sink_b, lse_c, mixed, mixed_bf, y1, x1, x1_bf, g, u, hh, y2))
        h, h_bf = x2, x2_bf

    dy, loss_part = _loss_head(h, loss_target[0])

    d_res, d_path = None, dy
    small = [None] * DEPTH
    mat_grads = [None] * DEPTH
    prev_ffn = prev_mix_swap = None
    for l in reversed(range(DEPTH)):
        h_in, h_in_bf, proj, raw, lse_b, sink_b, lse_c, mixed, mixed_bf, y1, x1, x1_bf, g, u, hh, y2 = saved[l]
        wi, wo, wg, wu, wd = (weights[l][k] for k in ("wi", "wo", "wg", "wu", "wd"))
        dz2, dz2_bf, d_ln2g, d_ln2b = _ln_bwd(f"ln2_bwd{l}", x1, y2, ln2_g[l], d_res, d_path,
                                              run_after=prev_mix_swap[2][0] if prev_mix_swap else None)
        dhh = _bwd_act_rowsharded(f"down_dx{l}", dz2_bf, wd, FF_SHARD, BF16)
        prev_mix = _rs_scatter(f"{l + 1}m", prev_mix_swap, dhh, c_idx) if prev_mix_swap else None
        d_wd = _bwd_w_rowsharded(f"down_dw{l}", hh, dz2_bf, FF_SHARD)
        dg, du, d_cw, d_cb = _conv_gate_bwd(f"conv_bwd{l}", g, u, cw_full[l], conv_b[l], dhh,
                                            run_after=prev_mix[2][0] if prev_mix else None)
        dx1 = _bwd_act_colsharded(f"gateup_dx{l}", [(dg, wg), (du, wu)])
        d_wg = _bwd_w_colsharded(f"gate_dw{l}", x1_bf, dg)
        d_wu = _bwd_w_colsharded(f"up_dw{l}", x1_bf, du)
        if prev_ffn:
            g_wg, g_wu, g_wd = _rs_end(f"{l + 1}f", prev_ffn, d_wu, kc_idx)
        ffn_swap = _rs_swap(f"{l}f", [d_wg, d_wu, d_wd])
        dz1, dz1_bf, d_ln1g, d_ln1b = _ln_bwd(f"ln1_bwd{l}", h_in, y1, ln1_g[l], dz2, dx1, run_after=ffn_swap[2][0])
        dmix = _bwd_act_rowsharded(f"wout_dx{l}", dz1_bf, wo, OUT_SHARD)
        d_wo = _bwd_w_rowsharded(f"wout_dw{l}", mixed_bf, dz1_bf, OUT_SHARD)
        if prev_mix:
            g_wi, g_wo = _rs_end(f"{l + 1}m", prev_mix, d_wo, kc_idx)
            mat_grads[l + 1] = [g_wi, g_wo, g_wg, g_wu, g_wd]
        s_ffn = _rs_scatter(f"{l}f", ffn_swap, d_wo, c_idx)
        dq_a, df_a, di_a, dg_a, d_nw, d_lb = _hgrn_bwd(f"hgrn_bwd{l}", proj, raw, dmix, lb_logits, a_norm_w[l], l,
                                                       run_after=s_ffn[2][0])
        dq_b, dk_b, dv_b, _ = _attn_bwd(f"dilated_bwd{l}", proj, mixed, dmix, lse_b, tables, None, n_kv=B_HEADS, rep=1,
                                        q0=QB0, k0=KB0, v0=VB0, m0=A_HEADS, patterns=B_PATTERNS)
        dq_c, dk_c, dv_c, d_sink = _attn_bwd(f"window_bwd{l}", proj, mixed, dmix, lse_c, tables, sink_b, n_kv=C_KV_HEADS,
                                             rep=C_HEADS // C_KV_HEADS, q0=QC0, k0=KC0, v0=VC0, m0=A_HEADS + B_HEADS,
                                             patterns=C_PATTERNS)
        dproj = jnp.concatenate([dq_a, df_a, di_a, dg_a, dq_b, dk_b, dv_b, dq_c, dk_c, dv_c], axis=1)
        dxp = _bwd_act_colsharded(f"proj_dx{l}", [(dproj, wi)])
        d_wi = _bwd_w_colsharded(f"proj_dw{l}", h_in_bf, dproj)
        d_res, d_path = dz1, dxp
        prev_ffn, prev_mix_swap = s_ffn, _rs_swap(f"{l}m", [d_wi, d_wo])
        small[l] = (d_lb, d_nw.reshape(A_HEADS, 8, BLK)[:, 0].sum(0), d_sink[:, 0, 0], d_ln1g[0], d_ln1b[0],
                    d_cw, d_cb[0], d_ln2g[0], d_ln2b[0])
    grad_x2 = _axpy("grad_x", d_res, d_path)
    grad_x = grad_x2[None]

    g_lb = small[0][0] + small[1][0]
    per_layer = [jnp.stack([small[0][i], small[1][i]]) for i in range(1, 9)]
    small_shapes = [(DEPTH, 4 * BLK), (DEPTH, BLK), (DEPTH, C_HEADS), (DEPTH, D_MODEL), (DEPTH, D_MODEL),
                    (DEPTH, CONV_WIDTH, D_FF), (DEPTH, D_FF), (DEPTH, D_MODEL), (DEPTH, D_MODEL), (BLK,)]
    rows = _rows_for(small_shapes)
    total = _sum_small(_gather_small(_pack([g_lb] + per_layer + [loss_part[0]], rows), prev_mix_swap[2][0]))
    g_lb, g_nw, g_sink, g_ln1g, g_ln1b, g_cw_full, g_cb, g_ln2g, g_ln2b, loss_row = _unpack(total, small_shapes)
    loss = loss_row[0]
    g_cw = lax.dynamic_slice_in_dim(g_cw_full, k_me * FF_SHARD, FF_SHARD, axis=2)

    sw = [lb_logits, a_norm_w, c_sinks, ln1_g, ln1_b, conv_w, conv_b, ln2_g, ln2_b]
    sg = [g_lb, g_nw, g_sink, g_ln1g, g_ln1b, g_cw, g_cb, g_ln2g, g_ln2b]
    sm = [m_lb_logits, m_a_norm_w, m_c_sinks, m_ln1_g, m_ln1_b, m_conv_w, m_conv_b, m_ln2_g, m_ln2_b]
    sv = [v_lb_logits, v_a_norm_w, v_c_sinks, v_ln1_g, v_ln1_b, v_conv_w, v_conv_b, v_ln2_g, v_ln2_b]
    shapes = [a.shape for a in sw]
    prow = _rows_for(shapes)
    sd, snm, snv = (_unpack(b, shapes) for b in _adamw_small(_pack(sw, prow), _pack(sg, prow), _pack(sm, prow), _pack(sv, prow)))

    names = ["w_in", "w_out", "w_gate", "w_up", "w_down"]
    mw = [w_in, w_out, w_gate, w_up, w_down]
    mm = [m_w_in, m_w_out, m_w_gate, m_w_up, m_w_down]
    mv = [v_w_in, v_w_out, v_w_gate, v_w_up, v_w_down]
    res = [None] * 5
    s_mix = _rs_scatter("0m", prev_mix_swap, total, c_idx)
    ffn0 = _rs_end("0f", prev_ffn, s_mix[2][0], kc_idx)
    for i, g0 in zip((2, 3, 4), ffn0):
        res[i] = _adamw_matrix(f"adamw_{names[i]}", mw[i], [g0, mat_grads[1][i]], mm[i], mv[i])
    mix0 = _rs_end("0m", s_mix, res[4][1], kc_idx)
    for i, g0 in zip((0, 1), mix0):
        res[i] = _adamw_matrix(f"adamw_{names[i]}", mw[i], [g0, mat_grads[1][i]], mm[i], mv[i])
    mg, md, mnm, mnv = ([r[j] for r in res] for j in range(4))

    def ordered(mat, sm_):
        return [mat[0], sm_[0], sm_[1], sm_[2], mat[1], sm_[3], sm_[4], mat[2], mat[3], sm_[5], sm_[6], mat[4], sm_[7], sm_[8]]

    return (loss, grad_x, *ordered(mg, sg), *ordered(md, sd), *ordered(mnm, snm), *ordered(mnv, snv))
```

```python
import functools
import math

import jax
import jax.numpy as jnp
from jax import lax
from jax.experimental import pallas as pl
from jax.experimental.pallas import tpu as pltpu

F32 = jnp.float32
BF16 = jnp.bfloat16

D_MODEL = 2048
SEQ = 2048
DEPTH = 2
HEAD_DIM = 128
A_HEADS = 4
B_HEADS = 6
C_HEADS = 6
C_KV_HEADS = 2
A_CHUNK = 16
DILATED_PATTERNS = ((128, 1), (512, 4), (2048, 16))
C_WINDOW = 128
ROPE_THETA = 500000.0
ROPE_DIM = HEAD_DIM // 4
D_FF = 5632
CONV_WIDTH = 3
LN_EPS = 1e-5
ALPHA = (2 * DEPTH) ** 0.25
IN_WIDTH = 5632
MIX_WIDTH = 2048
ADAM_LR = 0.001
ADAM_B1 = 0.9
ADAM_B2 = 0.999
ADAM_EPS = 1e-08
ADAM_WD = 0.01
ADAM_STEP = 10

N_CHIPS = 4
N_DEV = 8
FF_SHARD = D_FF // N_CHIPS
OUT_SHARD = MIX_WIDTH // N_CHIPS
BLK = 128
N_CHUNK = SEQ // A_CHUNK
SLAB = 32

QA0, FA0, IA0, GA0 = 0, 4, 8, 12
QB0, KB0, VB0 = 16, 22, 28
QC0, KC0, VC0 = 34, 40, 42

VMEM_LIMIT_V7X = 56 * 1024 * 1024
HI = lax.Precision.HIGHEST
MESH = pl.DeviceIdType.MESH


def _cp(sem=None, vmem=VMEM_LIMIT_V7X, **kw):
    return pltpu.CompilerParams(dimension_semantics=sem, vmem_limit_bytes=vmem, **kw)


def _sigmoid(x):
    return 1.0 / (1.0 + jnp.exp(-x))


def _mm(name, pairs, dims, grid, a_specs, b_specs, out_spec, out_shape, nk=1, acc_shape=None):
    n_pairs = len(pairs)

    def body(*refs):
        o_ref = refs[2 * n_pairs]
        part = None
        for p in range(n_pairs):
            a = refs[2 * p][...].astype(BF16)
            b = refs[2 * p + 1][...].astype(BF16)
            t = lax.dot_general(a, b, dims, preferred_element_type=F32)
            part = t if part is None else part + t
        if nk == 1:
            o_ref[...] = part.astype(o_ref.dtype)
        else:
            acc = refs[2 * n_pairs + 1]
            k = pl.program_id(len(grid) - 1)

            @pl.when(k == 0)
            def _():
                acc[...] = part

            @pl.when(k > 0)
            def _():
                acc[...] += part

            @pl.when(k == nk - 1)
            def _():
                o_ref[...] = acc[...].astype(o_ref.dtype)

    in_specs, args = [], []
    for (a, b), sa, sb in zip(pairs, a_specs, b_specs):
        in_specs += [sa, sb]
        args += [a, b]
    sem = ("parallel",) * (len(grid) - (1 if nk > 1 else 0)) + (("arbitrary",) if nk > 1 else ())
    return pl.pallas_call(
        body, name=name, grid=grid, in_specs=in_specs, out_specs=out_spec, out_shape=out_shape,
        scratch_shapes=[pltpu.VMEM(acc_shape, F32)] if nk > 1 else [],
        compiler_params=_cp(sem),
    )(*args)


NN = (((1,), (0,)), ((), ()))
NT = (((1,), (1,)), ((), ()))
TN = (((0,), (0,)), ((), ()))
TM = 1024


def _fwd_colsharded(name, x, w_stk, out_dtype=F32):
    return _mm(name, [(x, w_stk)], NN, (N_CHIPS, SEQ // TM),
               [pl.BlockSpec((TM, D_MODEL), lambda j, i: (i, 0))],
               [pl.BlockSpec((None, D_MODEL, FF_SHARD), lambda j, i: (j, 0, 0))],
               pl.BlockSpec((TM, FF_SHARD), lambda j, i: (i, j)),
               jax.ShapeDtypeStruct((SEQ, D_FF), out_dtype))


def _fwd_rowsharded(name, a, w_stk, shard):
    tn = D_MODEL
    return _mm(name, [(a, w_stk)], NN, (SEQ // TM, D_MODEL // tn, N_CHIPS),
               [pl.BlockSpec((TM, shard), lambda i, j, k: (i, k))],
               [pl.BlockSpec((None, shard, tn), lambda i, j, k: (k, 0, j))],
               pl.BlockSpec((TM, tn), lambda i, j, k: (i, j)),
               jax.ShapeDtypeStruct((SEQ, D_MODEL), F32), nk=N_CHIPS, acc_shape=(TM, tn))


def _bwd_act_colsharded(name, pairs):
    tn = 1024
    n = len(pairs)
    return _mm(name, pairs, NT, (SEQ // TM, D_MODEL // tn, N_CHIPS),
               [pl.BlockSpec((TM, FF_SHARD), lambda i, j, k: (i, k))] * n,
               [pl.BlockSpec((None, tn, FF_SHARD), lambda i, j, k: (k, j, 0))] * n,
               pl.BlockSpec((TM, tn), lambda i, j, k: (i, j)),
               jax.ShapeDtypeStruct((SEQ, D_MODEL), F32), nk=N_CHIPS, acc_shape=(TM, tn))


def _bwd_act_rowsharded(name, dy, w_stk, shard, out_dtype=F32):
    return _mm(name, [(dy, w_stk)], NT, (N_CHIPS, SEQ // TM),
               [pl.BlockSpec((TM, D_MODEL), lambda j, i: (i, 0))],
               [pl.BlockSpec((None, shard, D_MODEL), lambda j, i: (j, 0, 0))],
               pl.BlockSpec((TM, shard), lambda j, i: (i, j)),
               jax.ShapeDtypeStruct((SEQ, N_CHIPS * shard), out_dtype))


def _bwd_w_colsharded(name, x, dy):
    tm = 1024
    return _mm(name, [(x, dy)], TN, (N_CHIPS, D_MODEL // tm),
               [pl.BlockSpec((SEQ, tm), lambda j, i: (0, i))],
               [pl.BlockSpec((SEQ, FF_SHARD), lambda j, i: (0, j))],
               pl.BlockSpec((None, tm, FF_SHARD), lambda j, i: (j, i, 0)),
               jax.ShapeDtypeStruct((N_CHIPS, D_MODEL, FF_SHARD), BF16))


def _bwd_w_rowsharded(name, a, dy, shard):
    tn = 1024
    return _mm(name, [(a, dy)], TN, (N_CHIPS, D_MODEL // tn),
               [pl.BlockSpec((SEQ, shard), lambda j, i: (0, j))],
               [pl.BlockSpec((SEQ, tn), lambda j, i: (0, i))],
               pl.BlockSpec((None, shard, tn), lambda j, i: (j, 0, i)),
               jax.ShapeDtypeStruct((N_CHIPS, shard, D_MODEL), BF16))


TR = 256


def _ln_fwd(name, x, y, g, b):
    def body(x_ref, y_ref, g_ref, b_ref, o_ref, ob_ref):
        z = ALPHA * x_ref[...] + y_ref[...]
        mu = jnp.mean(z, -1, keepdims=True)
        zc = z - mu
        var = jnp.mean(zc * zc, -1, keepdims=True)
        o = zc * lax.rsqrt(var + LN_EPS) * g_ref[...] + b_ref[...]
        o_ref[...] = o
        ob_ref[...] = o.astype(BF16)

    row = pl.BlockSpec((TR, D_MODEL), lambda i: (i, 0))
    vec = pl.BlockSpec((1, D_MODEL), lambda i: (0, 0))
    return pl.pallas_call(body, name=name, grid=(SEQ // TR,), in_specs=[row, row, vec, vec], out_specs=[row, row],
                          out_shape=[jax.ShapeDtypeStruct((SEQ, D_MODEL), F32), jax.ShapeDtypeStruct((SEQ, D_MODEL), BF16)],
                          compiler_params=_cp(("parallel",)))(x, y, g.reshape(1, -1), b.reshape(1, -1))


def _to_bf16(name, x):
    def body(x_ref, o_ref):
        o_ref[...] = x_ref[...].astype(BF16)

    row = pl.BlockSpec((TR, D_MODEL), lambda i: (i, 0))
    return pl.pallas_call(body, name=name, grid=(SEQ // TR,), in_specs=[row], out_specs=row,
                          out_shape=jax.ShapeDtypeStruct((SEQ, D_MODEL), BF16),
                          compiler_params=_cp(("parallel",)))(x)


def _ln_bwd(name, x, y, g, d_res, d_path, run_after=None):
    has_res = d_res is not None
    n_in = 4 + has_res + (run_after is not None)

    def body(*refs):
        dz_ref, dzb_ref, dg_ref, db_ref = refs[n_in:]
        if has_res:
            x_ref, y_ref, g_ref, r_ref, p_ref = refs[:5]
            dout = ALPHA * r_ref[...] + p_ref[...]
        else:
            x_ref, y_ref, g_ref, p_ref = refs[:4]
            dout = p_ref[...]
        z = ALPHA * x_ref[...] + y_ref[...]
        mu = jnp.mean(z, -1, keepdims=True)
        zc = z - mu
        rstd = lax.rsqrt(jnp.mean(zc * zc, -1, keepdims=True) + LN_EPS)
        zh = zc * rstd
        dzh = dout * g_ref[...]
        dz = rstd * (dzh - jnp.mean(dzh, -1, keepdims=True) - zh * jnp.mean(dzh * zh, -1, keepdims=True))
        dz_ref[...] = dz
        dzb_ref[...] = dz.astype(BF16)
        pg = jnp.sum(dout * zh, 0, keepdims=True)
        pb = jnp.sum(dout, 0, keepdims=True)

        @pl.when(pl.program_id(0) == 0)
        def _():
            dg_ref[...] = pg
            db_ref[...] = pb

        @pl.when(pl.program_id(0) > 0)
        def _():
            dg_ref[...] += pg
            db_ref[...] += pb

    row = pl.BlockSpec((TR, D_MODEL), lambda i: (i, 0))
    vec = pl.BlockSpec((1, D_MODEL), lambda i: (0, 0))
    args = [x, y, g.reshape(1, -1)] + ([d_res] if has_res else []) + [d_path]
    in_specs = [row, row, vec] + ([row] if has_res else []) + [row]
    if run_after is not None:
        args.append(run_after)
        in_specs.append(pl.BlockSpec(memory_space=pl.ANY))
    vshape = jax.ShapeDtypeStruct((1, D_MODEL), F32)
    return pl.pallas_call(body, name=name, grid=(SEQ // TR,), in_specs=in_specs, out_specs=[row, row, vec, vec],
                          out_shape=[jax.ShapeDtypeStruct((SEQ, D_MODEL), F32), jax.ShapeDtypeStruct((SEQ, D_MODEL), BF16),
                                     vshape, vshape],
                          compiler_params=_cp(("arbitrary",)))(*args)


def _loss_head(y, target):
    def body(y_ref, t_ref, dy_ref, l_ref):
        e = y_ref[...] - t_ref[...]
        dy_ref[...] = e * (1.0 / D_MODEL)
        part = jnp.full((8, BLK), 0.5 / D_MODEL * jnp.sum(e * e), F32)

        @pl.when(pl.program_id(0) == 0)
        def _():
            l_ref[...] = part

        @pl.when(pl.program_id(0) > 0)
        def _():
            l_ref[...] += part

    row = pl.BlockSpec((TR, D_MODEL), lambda i: (i, 0))
    return pl.pallas_call(body, name="loss_head", grid=(SEQ // TR,), in_specs=[row, row],
                          out_specs=[row, pl.BlockSpec((8, BLK), lambda i: (0, 0))],
                          out_shape=[jax.ShapeDtypeStruct((SEQ, D_MODEL), F32), jax.ShapeDtypeStruct((8, BLK), F32)],
                          compiler_params=_cp(("arbitrary",)))(y, target)


def _axpy(name, a, b):
    def body(a_ref, b_ref, o_ref):
        o_ref[...] = ALPHA * a_ref[...] + b_ref[...]

    row = pl.BlockSpec((TR, D_MODEL), lambda i: (i, 0))
    return pl.pallas_call(body, name=name, grid=(SEQ // TR,), in_specs=[row, row], out_specs=row,
                          out_shape=jax.ShapeDtypeStruct((SEQ, D_MODEL), F32),
                          compiler_params=_cp(("parallel",)))(a, b)


TC = 512


def _shift_down(x, s, rows):
    if s == 0:
        return x
    return jnp.where(rows >= s, pltpu.roll(x, s, axis=0), 0.0)


def _shift_up(x, s, rows):
    if s == 0:
        return x
    return jnp.where(rows < SEQ - s, pltpu.roll(x, SEQ - s, axis=0), 0.0)


def _conv_gate_fwd(name, g, u, cw, cb):
    def body(g_ref, u_ref, w_ref, b_ref, h_ref):
        gg = g_ref[...].astype(F32)
        rows = lax.broadcasted_iota(jnp.int32, gg.shape, 0)
        gc = b_ref[...] + w_ref[2:3, :] * gg
        gc = gc + w_ref[1:2, :] * _shift_down(gg, 1, rows)
        gc = gc + w_ref[0:1, :] * _shift_down(gg, 2, rows)
        h_ref[...] = (gc * _sigmoid(gc) * u_ref[...].astype(F32)).astype(BF16)

    col = pl.BlockSpec((SEQ, TC), lambda j: (0, j))
    return pl.pallas_call(body, name=name, grid=(D_FF // TC,),
                          in_specs=[col, col, pl.BlockSpec((CONV_WIDTH, TC), lambda j: (0, j)),
                                    pl.BlockSpec((1, TC), lambda j: (0, j))],
                          out_specs=col, out_shape=jax.ShapeDtypeStruct((SEQ, D_FF), BF16),
                          compiler_params=_cp(("parallel",)))(g, u, cw, cb.reshape(1, -1))


def _conv_gate_bwd(name, g, u, cw, cb, dh, run_after=None):
    def body(g_ref, u_ref, w_ref, b_ref, dh_ref, *rest):
        dg_ref, du_ref, dw_ref, db_ref = rest[-4:]
        gg = g_ref[...].astype(F32)
        rows = lax.broadcasted_iota(jnp.int32, gg.shape, 0)
        g1 = _shift_down(gg, 1, rows)
        g2 = _shift_down(gg, 2, rows)
        gc = b_ref[...] + w_ref[2:3, :] * gg + w_ref[1:2, :] * g1 + w_ref[0:1, :] * g2
        sg = _sigmoid(gc)
        act = gc * sg
        dh = dh_ref[...].astype(F32)
        du_ref[...] = (dh * act).astype(BF16)
        dgc = dh * u_ref[...].astype(F32) * (sg * (1.0 + gc * (1.0 - sg)))
        db_ref[...] = jnp.sum(dgc, 0, keepdims=True)
        dw_ref[2:3, :] = jnp.sum(dgc * gg, 0, keepdims=True)
        dw_ref[1:2, :] = jnp.sum(dgc * g1, 0, keepdims=True)
        dw_ref[0:1, :] = jnp.sum(dgc * g2, 0, keepdims=True)
        dg_ref[...] = (w_ref[2:3, :] * dgc + w_ref[1:2, :] * _shift_up(dgc, 1, rows)
                       + w_ref[0:1, :] * _shift_up(dgc, 2, rows)).astype(BF16)

    col = pl.BlockSpec((SEQ, TC), lambda j: (0, j))
    w3 = pl.BlockSpec((CONV_WIDTH, TC), lambda j: (0, j))
    w1 = pl.BlockSpec((1, TC), lambda j: (0, j))
    big = jax.ShapeDtypeStruct((SEQ, D_FF), BF16)
    extra = [] if run_after is None else [run_after]
    return pl.pallas_call(body, name=name, grid=(D_FF // TC,),
                          in_specs=[col, col, w3, w1, col] + [pl.BlockSpec(memory_space=pl.ANY)] * len(extra),
                          out_specs=[col, col, w3, w1],
                          out_shape=[big, big, jax.ShapeDtypeStruct((CONV_WIDTH, D_FF), F32),
                                     jax.ShapeDtypeStruct((1, D_FF), F32)],
                          compiler_params=_cp(("parallel",)))(g, u, cw, cb.reshape(1, -1), dh, *extra)


def _lbs_of(logits, layer):
    m = jnp.max(logits, 0, keepdims=True)
    e = jnp.exp(logits - m)
    p = e / jnp.sum(e, 0, keepdims=True)
    lb = jnp.zeros((1, BLK), F32)
    for r in range(1, layer + 1):
        lb = lb + p[r:r + 1, :]
    return lb, p


def _dlogits_of(p, dlb, layer):
    rows = lax.broadcasted_iota(jnp.int32, p.shape, 0)
    dp = jnp.where((rows >= 1) & (rows <= layer), dlb, 0.0)
    return p * (dp - jnp.sum(p * dp, 0, keepdims=True))


SROWS = SLAB * A_CHUNK
N_SLAB = N_CHUNK // SLAB


def _chunk_prefix(x, rowi):
    for s in (1, 2, 4, 8):
        x = x + jnp.where(rowi >= s, pltpu.roll(x, s, axis=0), 0.0)
    return x


def _chunk_suffix(x, rowi):
    for s in (1, 2, 4, 8):
        x = x + jnp.where(rowi < A_CHUNK - s, pltpu.roll(x, SROWS - s, axis=0), 0.0)
    return x


def _c3(x):
    return x.reshape(SLAB, A_CHUNK, BLK)


def _c2(x):
    return x.reshape(SROWS, BLK)


def _split(x):
    top = lax.bitcast_convert_type(lax.bitcast_convert_type(x, jnp.uint32) & jnp.uint32(0xFFFF0000), F32)
    return top.astype(BF16), (x - top).astype(BF16)


def _lane_sum_b(x2, ones):
    hi, lo = _split(x2)
    return jnp.dot(hi, ones, preferred_element_type=F32) + jnp.dot(lo, ones, preferred_element_type=F32)


def _bmm(eq, a, b):
    ah, al = _split(a)
    bh, bl = _split(b)

    def mm(u, v):
        return jnp.einsum(eq, u, v, preferred_element_type=F32)

    return mm(ah, bh) + (mm(ah, bl) + mm(al, bh))


def _slab_rows(s):
    return pl.ds(s * SROWS, SROWS)


def _hgrn_prep(q, f, lb):
    rowi = lax.broadcasted_iota(jnp.int32, (SROWS, BLK), 0) & (A_CHUNK - 1)
    sq = _sigmoid(q)
    qc = q * sq
    sf = _sigmoid(f)
    fg = lb + (1.0 - lb) * sf
    kc = 1.0 - fg
    b = _chunk_prefix(jnp.log(fg), rowi)
    b3 = _c3(b)
    blast = b3[:, A_CHUNK - 1:A_CHUNK, :]
    eb = jnp.exp(b)
    ekb = _c2(jnp.exp(blast - b3))
    dec = jnp.exp(blast.reshape(SLAB, BLK))
    return rowi, sq, qc, sf, fg, kc, b, eb, ekb, dec


def _hgrn_slab_states(s, carry, v, ke, dec, dec_ref, u_ref, st_ref):
    dec_ref[pl.ds(s * SLAB, SLAB), :] = dec
    u_ref[...] = _bmm('ncv,nck->nvk', _c3(v), _c3(ke))

    def step(j, c):
        st_ref[j] = c
        return dec_ref[pl.ds(s * SLAB + j, 1), :] * c + u_ref[j]

    return lax.fori_loop(0, SLAB, step, carry)


def _hgrn_fwd(name, proj, lb_logits, nw, layer):
    def body(q_ref, f_ref, i_ref, g_ref, lg_ref, nw_ref, out_ref, outb_ref, raw_ref, dec_ref, u_ref, st_ref):
        lb, _ = _lbs_of(lg_ref[...], layer)
        ones = jnp.ones((BLK, BLK), BF16)
        carry = jnp.zeros((BLK, BLK), F32)
        for s in range(N_SLAB):
            rows = _slab_rows(s)
            v = i_ref[rows, :]
            rowi, sq, qc, sf, fg, kc, b, eb, ekb, dec = _hgrn_prep(q_ref[rows, :], f_ref[rows, :], lb)
            carry = _hgrn_slab_states(s, carry, v, kc * ekb, dec, dec_ref, u_ref, st_ref)
            o = _c2(_bmm('nck,nvk->ncv', _c3(qc * eb), st_ref[...]))
            qc3, kc3, b3, v3, row3 = _c3(qc), _c3(kc), _c3(b), _c3(v), _c3(rowi)
            for j in range(A_CHUNK):
                dj = jnp.exp(jnp.where(row3 >= j, b3 - b3[:, j:j + 1, :], -jnp.inf))
                a = _lane_sum_b(_c2(qc3 * dj * kc3[:, j:j + 1, :]), ones)
                o = o + a * _c2(jnp.broadcast_to(v3[:, j:j + 1, :], v3.shape))
            raw_ref[rows, :] = o
            r = lax.rsqrt(jnp.mean(o * o, -1, keepdims=True) + LN_EPS)
            gg = g_ref[rows, :]
            gated = o * r * nw_ref[...] * (gg * _sigmoid(gg))
            out_ref[rows, :] = gated
            outb_ref[rows, :] = gated.astype(BF16)

    def colblk(c0):
        return pl.BlockSpec((SEQ, BLK), lambda h: (0, c0 + h))

    return pl.pallas_call(
        body, name=name, grid=(A_HEADS,),
        in_specs=[colblk(QA0), colblk(FA0), colblk(IA0), colblk(GA0),
                  pl.BlockSpec((DEPTH, BLK), lambda h: (0, h)), pl.BlockSpec((1, BLK), lambda h: (0, 0))],
        out_specs=[colblk(0), colblk(0), colblk(0)],
        out_shape=[jax.ShapeDtypeStruct((SEQ, MIX_WIDTH), F32), jax.ShapeDtypeStruct((SEQ, MIX_WIDTH), BF16),
                   jax.ShapeDtypeStruct((SEQ, A_HEADS * BLK), F32)],
        scratch_shapes=[pltpu.VMEM((N_CHUNK, BLK), F32), pltpu.VMEM((SLAB, BLK, BLK), F32),
                        pltpu.VMEM((SLAB, BLK, BLK), F32)],
        compiler_params=_cp(("parallel",)))(proj, proj, proj, proj, lb_logits, nw.reshape(1, -1))


def _hgrn_bwd(name, proj, raw, dmix, lb_logits, nw, layer, run_after=None):
    extra = [] if run_after is None else [run_after]

    def body(q_ref, f_ref, i_ref, g_ref, raw_ref, do_ref, lg_ref, nw_ref, *rest):
        (dq_ref, df_ref, di_ref, dg_ref, dnw_ref, dlg_ref,
         dec_ref, u_ref, st_ref, h_ref, dbs_ref, dkc_ref, tot_ref) = rest[-13:]
        lb, p = _lbs_of(lg_ref[...], layer)
        ones = jnp.ones((BLK, BLK), BF16)
        nwv = nw_ref[...]

        carry = jnp.zeros((BLK, BLK), F32)
        for s in range(N_SLAB):
            rows = _slab_rows(s)
            rowi, sq, qc, sf, fg, kc, b, eb, ekb, dec = _hgrn_prep(q_ref[rows, :], f_ref[rows, :], lb)
            carry = _hgrn_slab_states(s, carry, i_ref[rows, :], kc * ekb, dec, dec_ref, u_ref,
                                      st_ref.at[pl.ds(s * SLAB, SLAB)])

        carry = jnp.zeros((BLK, BLK), F32)
        dnw = jnp.zeros((1, BLK), F32)
        for s in reversed(range(N_SLAB)):
            rows = _slab_rows(s)
            q, v = q_ref[rows, :], i_ref[rows, :]
            rowi, sq, qc, sf, fg, kc, b, eb, ekb, dec = _hgrn_prep(q, f_ref[rows, :], lb)
            ke = kc * ekb
            qe = qc * eb

            o = raw_ref[rows, :]
            gg = g_ref[rows, :]
            sgg = _sigmoid(gg)
            dout = do_ref[rows, :]
            r = lax.rsqrt(jnp.mean(o * o, -1, keepdims=True) + LN_EPS)
            oh = o * r
            dg_ref[rows, :] = (dout * oh * nwv * (sgg * (1.0 + gg * (1.0 - sgg)))).astype(BF16)
            dn = dout * (gg * sgg)
            dnw = dnw + jnp.sum(dn * oh, 0, keepdims=True)
            doh = dn * nwv
            do = r * (doh - oh * jnp.mean(doh * oh, -1, keepdims=True))
            do3, qe3, v3, ke3 = _c3(do), _c3(qe), _c3(v), _c3(ke)

            u_ref[...] = _bmm('ncv,nck->nvk', do3, qe3)

            def step(jj, c, s=s):
                j = SLAB - 1 - jj
                h_ref[j] = c
                return u_ref[j] + dec_ref[pl.ds(s * SLAB + j, 1), :] * c

            carry = lax.fori_loop(0, SLAB, step, carry)

            hh = h_ref[...]
            dqc = _c2(_bmm('ncv,nvk->nck', do3, st_ref[pl.ds(s * SLAB, SLAB)])) * eb
            dkc = _c2(_bmm('ncv,nvk->nck', v3, hh)) * ekb
            dv = _c2(_bmm('nck,nvk->ncv', ke3, hh))

            qc3, kc3, b3, row3 = _c3(qc), _c3(kc), _c3(b), _c3(rowi)
            datt_all = _bmm('niv,njv->nij', do3, v3)
            col = lax.broadcasted_iota(jnp.int32, datt_all.shape, 2)
            att_all = jnp.zeros_like(datt_all)
            for j in range(A_CHUNK):
                dj = jnp.exp(jnp.where(row3 >= j, b3 - b3[:, j:j + 1, :], -jnp.inf))
                kj = kc3[:, j:j + 1, :]
                att = _c3(_lane_sum_b(_c2(qc3 * dj * kj), ones))
                att_all = jnp.where(col == j, att[:, :, :A_CHUNK], att_all)
                md = dj * datt_all[:, :, j:j + 1]
                dqc = dqc + _c2(md * kj)
                dkc = dkc + _c2(jnp.where(row3 == j, jnp.sum(md * qc3, 1, keepdims=True), 0.0))
            dv = dv + _c2(_bmm('nij,niv->njv', att_all, do3))
            di_ref[rows, :] = dv.astype(BF16)
            dq_ref[rows, :] = (dqc * (sq * (1.0 + q * (1.0 - sq)))).astype(BF16)

            dbs = _chunk_suffix(qc * dqc - kc * dkc, rowi)
            dbs_ref[rows, :] = dbs
            dkc_ref[rows, :] = dkc
            tot_ref[pl.ds(s * SLAB, SLAB), :] = _c3(dbs)[:, 0:1, :].reshape(SLAB, BLK)
        dnw_ref[...] = jnp.broadcast_to(dnw, (8, BLK))

        rn = lax.broadcasted_iota(jnp.int32, (N_CHUNK, N_CHUNK), 0)
        cn = lax.broadcasted_iota(jnp.int32, (N_CHUNK, N_CHUNK), 1)
        tot_ref[...] = jnp.dot((cn > rn).astype(F32), tot_ref[...], preferred_element_type=F32, precision=HI)
        dlb = jnp.zeros((1, BLK), F32)
        for s in range(N_SLAB):
            rows = _slab_rows(s)
            sf = _sigmoid(f_ref[rows, :])
            fg = lb + (1.0 - lb) * sf
            later = tot_ref[pl.ds(s * SLAB, SLAB), :]
            dlg = _c2(_c3(dbs_ref[rows, :]) + later[:, None, :])
            dfg = dlg / fg - dkc_ref[rows, :]
            df_ref[rows, :] = (dfg * (1.0 - lb) * sf * (1.0 - sf)).astype(BF16)
            dlb = dlb + jnp.sum(dfg * (1.0 - sf), 0, keepdims=True)
        dlg_ref[...] = _dlogits_of(p, dlb, layer)

    def colblk(c0):
        return pl.BlockSpec((SEQ, BLK), lambda h: (0, c0 + h))

    bigb = jax.ShapeDtypeStruct((SEQ, A_HEADS * BLK), BF16)
    return pl.pallas_call(
        body, name=name, grid=(A_HEADS,),
        in_specs=[colblk(QA0), colblk(FA0), colblk(IA0), colblk(GA0), colblk(0), colblk(0),
                  pl.BlockSpec((DEPTH, BLK), lambda h: (0, h)), pl.BlockSpec((1, BLK), lambda h: (0, 0))]
        + [pl.BlockSpec(memory_space=pl.ANY)] * len(extra),
        out_specs=[colblk(0), colblk(0), colblk(0), colblk(0),
                   pl.BlockSpec((8, BLK), lambda h: (h, 0)), pl.BlockSpec((DEPTH, BLK), lambda h: (0, h))],
        out_shape=[bigb, bigb, bigb, bigb, jax.ShapeDtypeStruct((A_HEADS * 8, BLK), F32),
                   jax.ShapeDtypeStruct((DEPTH, A_HEADS * BLK), F32)],
        scratch_shapes=[pltpu.VMEM((N_CHUNK, BLK), F32), pltpu.VMEM((SLAB, BLK, BLK), F32),
                        pltpu.VMEM((N_CHUNK, BLK, BLK), F32), pltpu.VMEM((SLAB, BLK, BLK), F32),
                        pltpu.VMEM((SEQ, BLK), F32), pltpu.VMEM((SEQ, BLK), F32), pltpu.VMEM((N_CHUNK, BLK), F32)],
        compiler_params=_cp(("parallel",)))(proj, proj, proj, proj, raw, dmix, lb_logits, nw.reshape(1, -1), *extra)


SCALE = HEAD_DIM ** -0.5


def _rope_tables():
    half = ROPE_DIM // 2
    inv = ROPE_THETA ** (-jnp.arange(0, ROPE_DIM, 2, dtype=F32) / ROPE_DIM)
    ang = jnp.arange(SEQ, dtype=F32)[:, None] * inv[None, :]
    cos, sin = jnp.cos(ang), jnp.sin(ang)
    pad = jnp.zeros((SEQ, HEAD_DIM - ROPE_DIM), F32)
    zero = jnp.zeros((SEQ, half), F32)
    c = jnp.concatenate([cos, cos, pad + 1.0], 1)
    s_lo = jnp.concatenate([zero, sin, pad], 1)
    s_hi = jnp.concatenate([-sin, zero, pad], 1)
    return c, s_lo, s_hi


def _rope(x, c, s_lo, s_hi):
    half = ROPE_DIM // 2
    return x * c + pltpu.roll(x, half, axis=1) * s_lo + pltpu.roll(x, HEAD_DIM - half, axis=1) * s_hi


def _unrope(dy, c, s_lo, s_hi):
    half = ROPE_DIM // 2
    return dy * c + pltpu.roll(dy * s_lo, HEAD_DIM - half, axis=1) + pltpu.roll(dy * s_hi, half, axis=1)


N_BLK = SEQ // BLK


def _block_rows(dil):
    nb = N_BLK // dil
    return [pl.ds(r + n * BLK * dil, BLK, stride=dil) for r in range(dil) for n in range(nb)]


def _to_blocks(ref, dil):
    if dil == 1:
        return ref[...].reshape(N_BLK, BLK, BLK)
    return jnp.stack([ref[rows, :] for rows in _block_rows(dil)], 0)


def _from_blocks(ref, val, dil, add=False):
    if dil == 1:
        flat = val.reshape(SEQ, BLK)
        ref[...] = ref[...] + flat if add else flat
        return
    for b, rows in enumerate(_block_rows(dil)):
        ref[rows, :] = ref[rows, :] + val[b] if add else val[b]


def _prev_block(x):
    return jnp.concatenate([x[:1], x[:-1]], axis=0)


def _to_next_block(x):
    return jnp.concatenate([x[1:], jnp.zeros_like(x[:1])], axis=0)


def _band_masks(max_lag, dil):
    r = lax.broadcasted_iota(jnp.int32, (N_BLK, BLK, BLK), 1)
    c = lax.broadcasted_iota(jnp.int32, (N_BLK, BLK, BLK), 2)
    b = lax.broadcasted_iota(jnp.int32, (N_BLK, BLK, BLK), 0)
    has_prev = (b % (N_BLK // dil)) != 0
    return r >= c, has_prev & (BLK + r - c <= max_lag)


def _bdot(eq, a, b):
    return jnp.einsum(eq, a, b, preferred_element_type=F32)


def _attn_fwd(name, proj, tables, sink_b, mixed, mixed_bf, *, n_heads, rep, q0, k0, v0, m0, patterns):
    n_pat = len(patterns)
    has_sink = sink_b is not None

    def body(*refs):
        o_ref, ob_ref, l_ref, qr, kr, op, lse_ref = refs[-7:]
        q_ref, k_ref, v_ref, c_ref, sl_ref, sh_ref = refs[:6]
        if has_sink:
            sk = refs[6][0:1, 0:1]
        c, s_lo, s_hi = c_ref[...], sl_ref[...], sh_ref[...]
        qr[...] = _rope(q_ref[...], c, s_lo, s_hi)
        kr[...] = _rope(k_ref[...], c, s_lo, s_hi)
        for p, (max_lag, dil) in enumerate(patterns):
            qa = _to_blocks(qr, dil).astype(BF16)
            ka = _to_blocks(kr, dil).astype(BF16)
            va = _to_blocks(v_ref, dil).astype(BF16)
            own, before = _band_masks(max_lag, dil)
            s1 = jnp.where(own, _bdot('nqd,nkd->nqk', qa, ka) * SCALE, -jnp.inf)
            m = jnp.max(s1, -1, keepdims=True)
            with_prev = dil < N_BLK
            if with_prev:
                kp, vp = _prev_block(ka), _prev_block(va)
                s0 = jnp.where(before, _bdot('nqd,nkd->nqk', qa, kp) * SCALE, -jnp.inf)
                m = jnp.maximum(m, jnp.max(s0, -1, keepdims=True))
            if has_sink:
                m = jnp.maximum(m, sk)
            e1 = jnp.exp(s1 - m)
            den = jnp.sum(e1, -1, keepdims=True)
            o = _bdot('nqk,nkd->nqd', e1.astype(BF16), va)
            if with_prev:
                e0 = jnp.exp(s0 - m)
                den = den + jnp.sum(e0, -1, keepdims=True)
                o = o + _bdot('nqk,nkd->nqd', e0.astype(BF16), vp)
            if has_sink:
                den = den + jnp.exp(sk - m)
            _from_blocks(op.at[p], o / den, dil)
            _from_blocks(lse_ref.at[p], jnp.broadcast_to(m + jnp.log(den), (N_BLK, BLK, BLK)), dil)
        if n_pat == 1:
            acc = op[0]
            l_ref[...] = lse_ref[0]
        else:
            ls = [lse_ref[p] for p in range(n_pat)]
            m = functools.reduce(jnp.maximum, ls)
            es = [jnp.exp(l - m) for l in ls]
            tot = functools.reduce(jnp.add, es)
            acc = None
            for p in range(n_pat):
                t = (es[p] / tot) * op[p]
                acc = t if acc is None else acc + t
            l_ref[...] = m + jnp.log(tot)
        o_ref[...] = acc
        ob_ref[...] = acc.astype(BF16)

    def colblk(fn):
        return pl.BlockSpec((SEQ, BLK), fn)

    tab = pl.BlockSpec((SEQ, BLK), lambda h: (0, 0))
    in_specs = [colblk(lambda h: (0, q0 + h)), colblk(lambda h: (0, k0 + h // rep)), colblk(lambda h: (0, v0 + h // rep)),
                tab, tab, tab]
    args = [proj, proj, proj, *tables]
    if has_sink:
        in_specs.append(pl.BlockSpec((None, 8, BLK), lambda h: (h, 0, 0)))
        args.append(sink_b)
    n_in = len(args)
    in_specs += [pl.BlockSpec(memory_space=pl.ANY)] * 2
    args += [mixed, mixed_bf]
    pat = pltpu.VMEM((n_pat, SEQ, BLK), F32)
    return pl.pallas_call(
        body, name=name, grid=(n_heads,), in_specs=in_specs,
        out_specs=[colblk(lambda h: (0, m0 + h)), colblk(lambda h: (0, m0 + h)),
                   pl.BlockSpec((None, SEQ, BLK), lambda h: (h, 0, 0))],
        out_shape=[jax.ShapeDtypeStruct(mixed.shape, F32), jax.ShapeDtypeStruct(mixed.shape, BF16),
                   jax.ShapeDtypeStruct((n_heads, SEQ, BLK), F32)],
        input_output_aliases={n_in: 0, n_in + 1: 1},
        scratch_shapes=[pltpu.VMEM((SEQ, BLK), F32), pltpu.VMEM((SEQ, BLK), F32), pat, pat],
        compiler_params=_cp(("parallel",)))(*args)


def _attn_bwd(name, proj, mixed, dmix, lse, tables, sink_b, *, n_kv, rep, q0, k0, v0, m0, patterns):
    n_heads = n_kv * rep
    has_sink = sink_b is not None

    def body(*refs):
        if has_sink:
            (q_ref, k_ref, v_ref, o_ref, do_ref, lse_ref, c_ref, sl_ref, sh_ref, sink_ref,
             dq_ref, dk_ref, dv_ref, dsk_ref, qr, kr, dqa, dka, dva, dd) = refs
        else:
            (q_ref, k_ref, v_ref, o_ref, do_ref, lse_ref, c_ref, sl_ref, sh_ref,
             dq_ref, dk_ref, dv_ref, dsk_ref, qr, kr, dqa, dka, dva, dd) = refs
        j = pl.program_id(1)
        c, s_lo, s_hi = c_ref[...], sl_ref[...], sh_ref[...]
        qr[...] = _rope(q_ref[...], c, s_lo, s_hi)
        kr[...] = _rope(k_ref[...], c, s_lo, s_hi)
        dcol = jnp.sum(do_ref[...] * o_ref[...], -1, keepdims=True)
        dd[...] = jnp.broadcast_to(dcol, (SEQ, BLK))

        @pl.when(j == 0)
        def _():
            dka[...] = jnp.zeros((SEQ, BLK), F32)
            dva[...] = jnp.zeros((SEQ, BLK), F32)

        for p, (max_lag, dil) in enumerate(patterns):
            qa = _to_blocks(qr, dil).astype(BF16)
            ka = _to_blocks(kr, dil).astype(BF16)
            va = _to_blocks(v_ref, dil).astype(BF16)
            doa = _to_blocks(do_ref, dil).astype(BF16)
            lcol = _to_blocks(lse_ref, dil)[:, :, 0:1]
            dcb = _to_blocks(dd, dil)[:, :, 0:1]
            own, before = _band_masks(max_lag, dil)

            def probs_and_ds(kk, vv, valid):
                s = _bdot('nqd,nkd->nqk', qa, kk) * SCALE
                a = jnp.where(valid, jnp.exp(s - lcol), 0.0)
                ds = a * (_bdot('nqd,nkd->nqk', doa, vv) - dcb) * SCALE
                return a.astype(BF16), ds.astype(BF16)

            a1, ds1 = probs_and_ds(ka, va, own)
            dq = _bdot('nqk,nkd->nqd', ds1, ka)
            dk = _bdot('nqk,nqd->nkd', ds1, qa)
            dv = _bdot('nqk,nqd->nkd', a1, doa)
            if dil < N_BLK:
                kp, vp = _prev_block(ka), _prev_block(va)
                a0, ds0 = probs_and_ds(kp, vp, before)
                dq = dq + _bdot('nqk,nkd->nqd', ds0, kp)
                dk = dk + _to_next_block(_bdot('nqk,nqd->nkd', ds0, qa))
                dv = dv + _to_next_block(_bdot('nqk,nqd->nkd', a0, doa))
            _from_blocks(dqa, dq, dil, add=p > 0)
            _from_blocks(dka, dk, dil, add=True)
            _from_blocks(dva, dv, dil, add=True)

        if has_sink:
            sk = sink_ref[0:1, 0:1]
            ps = jnp.exp(sk - lse_ref[...][:, 0:1])
            dsk_ref[...] = jnp.full((8, BLK), -jnp.sum(ps * dcol), F32)
        else:
            dsk_ref[...] = jnp.zeros((8, BLK), F32)
        dq_ref[...] = _unrope(dqa[...], c, s_lo, s_hi).astype(BF16)

        @pl.when(j == rep - 1)
        def _():
            dk_ref[...] = _unrope(dka[...], c, s_lo, s_hi).astype(BF16)
            dv_ref[...] = dva[...].astype(BF16)

    def colblk(fn):
        return pl.BlockSpec((SEQ, BLK), fn)

    tab = pl.BlockSpec((SEQ, BLK), lambda g, j: (0, 0))
    in_specs = [colblk(lambda g, j: (0, q0 + g * rep + j)), colblk(lambda g, j: (0, k0 + g)), colblk(lambda g, j: (0, v0 + g)),
                colblk(lambda g, j: (0, m0 + g * rep + j)), colblk(lambda g, j: (0, m0 + g * rep + j)),
                pl.BlockSpec((None, SEQ, BLK), lambda g, j: (g * rep + j, 0, 0)), tab, tab, tab]
    args = [proj, proj, proj, mixed, dmix, lse, *tables]
    if has_sink:
        in_specs.append(pl.BlockSpec((None, 8, BLK), lambda g, j: (g * rep + j, 0, 0)))
        args.append(sink_b)
    acc = pltpu.VMEM((SEQ, BLK), F32)
    return pl.pallas_call(
        body, name=name, grid=(n_kv, rep), in_specs=in_specs,
        out_specs=[colblk(lambda g, j: (0, g * rep + j)), colblk(lambda g, j: (0, g)), colblk(lambda g, j: (0, g)),
                   pl.BlockSpec((None, 8, BLK), lambda g, j: (g * rep + j, 0, 0))],
        out_shape=[jax.ShapeDtypeStruct((SEQ, n_heads * BLK), BF16), jax.ShapeDtypeStruct((SEQ, n_kv * BLK), BF16),
                   jax.ShapeDtypeStruct((SEQ, n_kv * BLK), BF16), jax.ShapeDtypeStruct((n_heads, 8, BLK), F32)],
        scratch_shapes=[acc, acc, acc, acc, acc, acc],
        compiler_params=_cp(("parallel", "arbitrary")))(*args)


B_PATTERNS = tuple((w // d, d) for w, d in DILATED_PATTERNS)
C_PATTERNS = ((C_WINDOW - 1, 1),)


ANY = pl.BlockSpec(memory_space=pl.ANY)
CHIP_MASKS = ((1, 0), (0, 1), (1, 1))


def _coords():
    return lax.axis_index("x"), lax.axis_index("y"), lax.axis_index("c")


def _flip(v, m):
    return 1 - v if m else v


def _into_slot(name, w, layer, k_idx, dtype, run_after=None):
    _, rows, cols = w.shape
    tr = rows // 8 if rows % 64 == 0 else rows

    def body(k_ref, w_ref, *rest):
        rest[-1][...] = w_ref[...].astype(dtype)

    in_specs = [pl.BlockSpec((None, tr, cols), lambda i, k: (layer, i, 0))]
    args = [k_idx, w]
    if run_after is not None:
        in_specs.append(pl.BlockSpec(memory_space=pl.ANY))
        args.append(run_after)
    return pl.pallas_call(
        body, name=name,
        grid_spec=pltpu.PrefetchScalarGridSpec(
            num_scalar_prefetch=1, grid=(rows // tr,), in_specs=in_specs,
            out_specs=pl.BlockSpec((None, tr, cols), lambda i, k: (k[0], i, 0))),
        out_shape=jax.ShapeDtypeStruct((N_CHIPS, rows, cols), dtype),
        compiler_params=_cp(("parallel",)))(*args)


HBM_SPEC = pl.BlockSpec(memory_space=pltpu.HBM)
SEM_SPEC = pl.BlockSpec(memory_space=pltpu.SEMAPHORE)
TOKEN_SPEC = pl.BlockSpec(memory_space=pltpu.VMEM)
TOKEN_SHAPE = jax.ShapeDtypeStruct((8, BLK), F32)
DATAFLOW = pltpu.SideEffectType.DATAFLOW_SIDE_EFFECTING


def _hbm(a):
    return pltpu.with_memory_space_constraint(a, pltpu.HBM)


def _hbm_like(bufs):
    return [pltpu.HBM(b.shape, b.dtype) for b in bufs]


def _gather_start(name, stages):
    flat = [b for st in stages for b in st]
    n, ns = len(flat), len(stages)

    def body(*refs):
        ins = refs[:n]
        sems = refs[n:n + 2 * ns]
        token = refs[-1]
        x, y, c = _coords()
        k_me = 2 * x + y
        a = 0
        for s, st in enumerate(stages):
            for i in range(len(st)):
                mine = ins[a].at[k_me, c]
                for m, (mx, my) in enumerate(CHIP_MASKS):
                    pltpu.make_async_remote_copy(src_ref=mine, dst_ref=mine, send_sem=sems[2 * s].at[i * 3 + m],
                                                 recv_sem=sems[2 * s + 1].at[i * 3 + m],
                                                 device_id=(_flip(x, mx), _flip(y, my), c), device_id_type=MESH).start()
                a += 1
        token[...] = jnp.zeros_like(token)

    sem_shapes = []
    for st in stages:
        sem_shapes += [pltpu.SemaphoreType.DMA((3 * len(st),))] * 2
    out = pl.pallas_call(
        body, name=name, in_specs=[HBM_SPEC] * n,
        out_specs=tuple([SEM_SPEC] * (2 * ns) + [HBM_SPEC] * n + [TOKEN_SPEC]),
        out_shape=tuple(sem_shapes + _hbm_like(flat) + [TOKEN_SHAPE]),
        input_output_aliases={i: 2 * ns + i for i in range(n)},
        compiler_params=pltpu.CompilerParams(has_side_effects=DATAFLOW),
    )(*[_hbm(b) for b in flat])
    sems, bufs, token = out[:2 * ns], out[2 * ns:2 * ns + n], out[-1]
    res, a = [], 0
    for s, st in enumerate(stages):
        res.append((sems[2 * s], sems[2 * s + 1], list(bufs[a:a + len(st)])))
        a += len(st)
    return res, token


def _gather_forward(name, stage, after):
    ssem_in, rsem_in, bufs = stage
    n = len(bufs)

    def body(*refs):
        ins = refs[:n]
        s_in, r_in, _ = refs[n:n + 3]
        s_out, r_out = refs[n + 3:n + 5]
        token = refs[-1]
        x, y, c = _coords()
        for i in range(n):
            for m, (mx, my) in enumerate(CHIP_MASKS):
                kp = 2 * _flip(x, mx) + _flip(y, my)
                blk = ins[i].at[kp, c]
                got = pltpu.make_async_remote_copy(src_ref=blk, dst_ref=blk, send_sem=s_in.at[i * 3 + m],
                                                   recv_sem=r_in.at[i * 3 + m], device_id=(x, y, 1 - c), device_id_type=MESH)
                got.wait_send()
                got.wait_recv()
                pltpu.make_async_remote_copy(src_ref=blk, dst_ref=blk, send_sem=s_out.at[i * 3 + m],
                                             recv_sem=r_out.at[i * 3 + m], device_id=(x, y, 1 - c), device_id_type=MESH).start()
        token[...] = jnp.zeros_like(token)

    sem = pltpu.SemaphoreType.DMA((3 * n,))
    out = pl.pallas_call(
        body, name=name, in_specs=[HBM_SPEC] * n + [SEM_SPEC, SEM_SPEC, ANY],
        out_specs=tuple([SEM_SPEC, SEM_SPEC] + [HBM_SPEC] * n + [TOKEN_SPEC]),
        out_shape=tuple([sem, sem] + _hbm_like(bufs) + [TOKEN_SHAPE]),
        input_output_aliases={i: 2 + i for i in range(n)},
        compiler_params=pltpu.CompilerParams(has_side_effects=DATAFLOW),
    )(*bufs, ssem_in, rsem_in, after)
    return (out[0], out[1], list(out[2:2 + n])), out[-1]


def _gather_wait(name, stage, after):
    ssem, rsem, bufs = stage
    n = len(bufs)

    def body(*refs):
        ins = refs[:n]
        s_in, r_in, _ = refs[n:n + 3]
        x, y, c = _coords()
        for i in range(n):
            for m, (mx, my) in enumerate(CHIP_MASKS):
                kp = 2 * _flip(x, mx) + _flip(y, my)
                sent, got = ins[i].at[kp, c], ins[i].at[kp, 1 - c]
                cp = pltpu.make_async_remote_copy(src_ref=sent, dst_ref=got, send_sem=s_in.at[i * 3 + m],
                                                  recv_sem=r_in.at[i * 3 + m], device_id=(x, y, 1 - c), device_id_type=MESH)
                cp.wait_send()
                cp.wait_recv()

    out = pl.pallas_call(
        body, name=name, in_specs=[HBM_SPEC] * n + [SEM_SPEC, SEM_SPEC, ANY],
        out_specs=tuple([HBM_SPEC] * n), out_shape=tuple(_hbm_like(bufs)),
        input_output_aliases={i: i for i in range(n)},
        compiler_params=pltpu.CompilerParams(has_side_effects=DATAFLOW),
    )(*bufs, ssem, rsem, after)
    return list(out)


def _swap_start(name, grads):
    n = len(grads)

    def body(*refs):
        ins, lands = refs[:n], refs[n:2 * n]
        ssem, rsem = refs[2 * n:2 * n + 2]
        x, y, c = _coords()
        for a in range(n):
            for j in range(N_CHIPS):
                pltpu.make_async_remote_copy(src_ref=ins[a].at[j, 1 - c], dst_ref=lands[a].at[j],
                                             send_sem=ssem.at[a * N_CHIPS + j], recv_sem=rsem.at[a * N_CHIPS + j],
                                             device_id=(x, y, 1 - c), device_id_type=MESH).start()

    sem = pltpu.SemaphoreType.DMA((N_CHIPS * n,))
    land_shapes = [pltpu.HBM((N_CHIPS,) + g.shape[2:], g.dtype) for g in grads]
    out = pl.pallas_call(
        body, name=name, in_specs=[HBM_SPEC] * (2 * n),
        out_specs=tuple([SEM_SPEC, SEM_SPEC] + [HBM_SPEC] * (2 * n)),
        out_shape=tuple([sem, sem] + _hbm_like(grads) + land_shapes),
        input_output_aliases={i: 2 + i for i in range(2 * n)},
        compiler_params=pltpu.CompilerParams(has_side_effects=DATAFLOW),
    )(*[_hbm(g) for g in grads], *[_hbm(lax.empty((N_CHIPS,) + g.shape[2:], g.dtype)) for g in grads])
    return out[0], out[1], list(out[2:2 + n]), list(out[2 + n:])


def _swap_wait(name, started, after):
    ssem, rsem, grads, lands = started
    n = len(grads)

    def body(*refs):
        ins, lnd = refs[:n], refs[n:2 * n]
        s_in, r_in, _ = refs[2 * n:2 * n + 3]
        x, y, c = _coords()
        for a in range(n):
            for j in range(N_CHIPS):
                cp = pltpu.make_async_remote_copy(src_ref=ins[a].at[j, 1 - c], dst_ref=lnd[a].at[j],
                                                  send_sem=s_in.at[a * N_CHIPS + j], recv_sem=r_in.at[a * N_CHIPS + j],
                                                  device_id=(x, y, 1 - c), device_id_type=MESH)
                cp.wait_send()
                cp.wait_recv()

    out = pl.pallas_call(
        body, name=name, in_specs=[HBM_SPEC] * (2 * n) + [SEM_SPEC, SEM_SPEC, ANY],
        out_specs=tuple([HBM_SPEC] * (2 * n)), out_shape=tuple(_hbm_like(grads) + _hbm_like(lands)),
        input_output_aliases={i: i for i in range(2 * n)},
        compiler_params=pltpu.CompilerParams(has_side_effects=DATAFLOW),
    )(*grads, *lands, ssem, rsem, after)
    return list(out[:n]), list(out[n:])


def _scatter_start(name, parts):
    n = len(parts)

    def body(*refs):
        ins, lands = refs[:n], refs[n:2 * n]
        ssem, rsem = refs[2 * n:2 * n + 2]
        x, y, c = _coords()
        k_me = 2 * x + y
        for a in range(n):
            for m, (mx, my) in enumerate(CHIP_MASKS):
                px, py = _flip(x, mx), _flip(y, my)
                pltpu.make_async_remote_copy(src_ref=ins[a].at[2 * px + py], dst_ref=lands[a].at[k_me],
                                             send_sem=ssem.at[a * 3 + m], recv_sem=rsem.at[a * 3 + m],
                                             device_id=(px, py, c), device_id_type=MESH).start()

    sem = pltpu.SemaphoreType.DMA((3 * n,))
    out = pl.pallas_call(
        body, name=name, in_specs=[HBM_SPEC] * (2 * n),
        out_specs=tuple([SEM_SPEC, SEM_SPEC] + [HBM_SPEC] * (2 * n)),
        out_shape=tuple([sem, sem] + _hbm_like(parts) + _hbm_like(parts)),
        input_output_aliases={i: 2 + i for i in range(2 * n)},
        compiler_params=pltpu.CompilerParams(has_side_effects=DATAFLOW),
    )(*[_hbm(p) for p in parts], *[_hbm(lax.empty(p.shape, p.dtype)) for p in parts])
    return out[0], out[1], list(out[2:2 + n]), list(out[2 + n:])


def _scatter_wait(name, started, after):
    ssem, rsem, parts, lands = started
    n = len(parts)

    def body(*refs):
        ins, lnd = refs[:n], refs[n:2 * n]
        s_in, r_in, _ = refs[2 * n:2 * n + 3]
        x, y, c = _coords()
        k_me = 2 * x + y
        for a in range(n):
            for m, (mx, my) in enumerate(CHIP_MASKS):
                px, py = _flip(x, mx), _flip(y, my)
                cp = pltpu.make_async_remote_copy(src_ref=ins[a].at[2 * px + py], dst_ref=lnd[a].at[k_me],
                                                  send_sem=s_in.at[a * 3 + m], recv_sem=r_in.at[a * 3 + m],
                                                  device_id=(px, py, c), device_id_type=MESH)
                cp.wait_send()
                cp.wait_recv()

    out = pl.pallas_call(
        body, name=name, in_specs=[HBM_SPEC] * (2 * n) + [SEM_SPEC, SEM_SPEC, ANY],
        out_specs=tuple([HBM_SPEC] * (2 * n)), out_shape=tuple(_hbm_like(parts) + _hbm_like(lands)),
        input_output_aliases={i: i for i in range(2 * n)},
        compiler_params=pltpu.CompilerParams(has_side_effects=DATAFLOW),
    )(*parts, *lands, ssem, rsem, after)
    return list(out[:n]), list(out[n:])


def _pair_gather(name, bufs):
    n = len(bufs)

    def body(*refs):
        outs = refs[n:2 * n]
        ssem, rsem = refs[2 * n:]
        x, y, c = _coords()
        cps = []
        for a in range(n):
            mine = outs[a].at[c]
            cp = pltpu.make_async_remote_copy(src_ref=mine, dst_ref=mine, send_sem=ssem.at[a],
                                              recv_sem=rsem.at[a], device_id=(x, y, 1 - c), device_id_type=MESH)
            cp.start()
            cps.append(cp)
        for cp in cps:
            cp.wait()

    return pl.pallas_call(
        body, name=name, in_specs=[ANY] * n, out_specs=[ANY] * n,
        out_shape=[jax.ShapeDtypeStruct(b.shape, b.dtype) for b in bufs],
        input_output_aliases={a: a for a in range(n)},
        scratch_shapes=[pltpu.SemaphoreType.DMA((n,)), pltpu.SemaphoreType.DMA((n,))],
        compiler_params=pltpu.CompilerParams(has_side_effects=True),
    )(*bufs)


DEV_MASKS = tuple((mx, my, mc) for mx in (0, 1) for my in (0, 1) for mc in (0, 1) if (mx, my, mc) != (0, 0, 0))


def _gather_small(buf, run_after):
    def body(in_ref, _, out_ref, ssem, rsem, lsem):
        x, y, c = _coords()
        me = 4 * x + 2 * y + c
        cps = [pltpu.make_async_copy(in_ref, out_ref.at[me], lsem)]
        cps[0].start()
        for t, (mx, my, mc) in enumerate(DEV_MASKS):
            cp = pltpu.make_async_remote_copy(src_ref=in_ref, dst_ref=out_ref.at[me], send_sem=ssem.at[t],
                                              recv_sem=rsem.at[t], device_id=(_flip(x, mx), _flip(y, my), _flip(c, mc)),
                                              device_id_type=MESH)
            cp.start()
            cps.append(cp)
        for cp in cps:
            cp.wait()

    return pl.pallas_call(
        body, name="gather_small", in_specs=[ANY, ANY], out_specs=ANY,
        out_shape=jax.ShapeDtypeStruct((N_DEV,) + buf.shape, buf.dtype),
        scratch_shapes=[pltpu.SemaphoreType.DMA((N_DEV - 1,)), pltpu.SemaphoreType.DMA((N_DEV - 1,)),
                        pltpu.SemaphoreType.DMA(())],
        compiler_params=pltpu.CompilerParams(has_side_effects=True),
    )(buf, run_after)


def _row_tile(rows):
    return rows // 2 if rows % 16 == 0 else rows


def _pair_add(name, grad, got, c_idx):
    _, _, r2, cols = grad.shape
    tr = _row_tile(r2)

    def body(c_ref, a_ref, b_ref, o_ref):
        o_ref[...] = (a_ref[...].astype(F32) + b_ref[...].astype(F32)).astype(BF16)

    return pl.pallas_call(
        body, name=name,
        grid_spec=pltpu.PrefetchScalarGridSpec(
            num_scalar_prefetch=1, grid=(N_CHIPS, r2 // tr),
            in_specs=[pl.BlockSpec((None, None, tr, cols), lambda j, i, c: (j, c[0], i, 0)),
                      pl.BlockSpec((None, tr, cols), lambda j, i, c: (j, i, 0))],
            out_specs=pl.BlockSpec((None, tr, cols), lambda j, i, c: (j, i, 0))),
        out_shape=jax.ShapeDtypeStruct((N_CHIPS, r2, cols), BF16),
        compiler_params=_cp(("parallel", "parallel")))(c_idx, grad, got)


def _chip_add(name, part, got, kc_idx):
    _, r2, cols = got.shape
    tr = _row_tile(r2)

    def body(k_ref, p_ref, g1_ref, g2_ref, g3_ref, o_ref):
        acc = p_ref[...].astype(F32)
        for g_ref in (g1_ref, g2_ref, g3_ref):
            acc = acc + g_ref[...].astype(F32)
        o_ref[...] = acc

    def slot(d):
        return pl.BlockSpec((None, tr, cols), lambda i, k: ((k[0] + d) % N_CHIPS, i, 0))

    return pl.pallas_call(
        body, name=name,
        grid_spec=pltpu.PrefetchScalarGridSpec(
            num_scalar_prefetch=1, grid=(r2 // tr,),
            in_specs=[slot(0), slot(1), slot(2), slot(3)],
            out_specs=pl.BlockSpec((None, tr, cols), lambda i, k: (k[1], i, 0))),
        out_shape=jax.ShapeDtypeStruct((2, r2, cols), F32),
        compiler_params=_cp(("parallel",)))(kc_idx, part, got, got, got)


def _adam_math(w, g, m, v):
    m2 = ADAM_B1 * m + (1.0 - ADAM_B1) * g
    v2 = ADAM_B2 * v + (1.0 - ADAM_B2) * (g * g)
    m_hat = m2 / (1.0 - ADAM_B1 ** ADAM_STEP)
    v_hat = v2 / (1.0 - ADAM_B2 ** ADAM_STEP)
    delta = -ADAM_LR * (m_hat / (jnp.sqrt(v_hat) + ADAM_EPS) + ADAM_WD * w)
    return delta, m2, v2


def _adamw_matrix(name, w, g_layers, m, v):
    _, rows, cols = w.shape
    tr = rows // 16 if rows % 128 == 0 else rows // 8

    def body(w_ref, g0_ref, g1_ref, m_ref, v_ref, go_ref, d_ref, mo_ref, vo_ref):
        g = jnp.where(pl.program_id(0) == 0, g0_ref[...], g1_ref[...])
        go_ref[...] = g
        d_ref[...], mo_ref[...], vo_ref[...] = _adam_math(w_ref[...], g, m_ref[...], v_ref[...])

    lay = pl.BlockSpec((None, tr, cols), lambda l, i: (l, i, 0))
    flat = pl.BlockSpec((tr, cols), lambda l, i: (i, 0))
    shp = jax.ShapeDtypeStruct(w.shape, F32)
    return pl.pallas_call(body, name=name, grid=(DEPTH, rows // tr), in_specs=[lay, flat, flat, lay, lay],
                          out_specs=[lay, lay, lay, lay], out_shape=[shp, shp, shp, shp],
                          compiler_params=_cp(("parallel", "parallel")))(w, g_layers[0], g_layers[1], m, v)


def _sum_small(gathered):
    def body(g_ref, o_ref):
        acc = g_ref[0]
        for d in range(1, N_DEV):
            acc = acc + g_ref[d]
        o_ref[...] = acc

    return pl.pallas_call(body, name="sum_small", out_shape=jax.ShapeDtypeStruct(gathered.shape[1:], F32),
                          compiler_params=_cp())(gathered)


def _adamw_small(w, g, m, v):
    def body(w_ref, g_ref, m_ref, v_ref, d_ref, mo_ref, vo_ref):
        d_ref[...], mo_ref[...], vo_ref[...] = _adam_math(w_ref[...], g_ref[...], m_ref[...], v_ref[...])

    shp = jax.ShapeDtypeStruct(w.shape, F32)
    return pl.pallas_call(body, name="adamw_small", out_shape=[shp, shp, shp], compiler_params=_cp())(w, g, m, v)


def _pack(arrays, rows):
    flat = jnp.concatenate([a.reshape(-1) for a in arrays])
    return jnp.pad(flat, (0, rows * BLK - flat.shape[0])).reshape(rows, BLK)


def _unpack(buf, shapes):
    flat = buf.reshape(-1)
    out, pos = [], 0
    for s in shapes:
        n = math.prod(s)
        out.append(flat[pos:pos + n].reshape(s))
        pos += n
    return out


def _rows_for(shapes):
    n = sum(math.prod(s) for s in shapes)
    return -(-n // (8 * BLK)) * 8


def _rs_swap(tag, grads):
    return _swap_start(f"rs_swap_start{tag}", [g.reshape(N_CHIPS, 2, g.shape[1] // 2, g.shape[2]) for g in grads])


def _rs_scatter(tag, swapping, after, c_idx):
    split, got = _swap_wait(f"rs_swap_wait{tag}", swapping, after)
    parts = [_pair_add(f"rs_pair_add{tag}_{i}", s, r, c_idx) for i, (s, r) in enumerate(zip(split, got))]
    return _scatter_start(f"rs_scatter_start{tag}", parts)


def _rs_end(tag, started, after, kc_idx):
    parts, lands = _scatter_wait(f"rs_scatter_wait{tag}", started, after)
    halves = [_chip_add(f"rs_chip_add{tag}_{i}", p, r, kc_idx) for i, (p, r) in enumerate(zip(parts, lands))]
    full = _pair_gather(f"rs_pair_gather{tag}", halves)
    return [f.reshape(2 * f.shape[1], f.shape[2]) for f in full]


def kernel(x, w_in, lb_logits, a_norm_w, c_sinks, w_out, ln1_g, ln1_b, w_gate, w_up, conv_w, conv_b, w_down, ln2_g, ln2_b, loss_target, m_w_in, m_lb_logits, m_a_norm_w, m_c_sinks, m_w_out, m_ln1_g, m_ln1_b, m_w_gate, m_w_up, m_conv_w, m_conv_b, m_w_down, m_ln2_g, m_ln2_b, v_w_in, v_lb_logits, v_a_norm_w, v_c_sinks, v_w_out, v_ln1_g, v_ln1_b, v_w_gate, v_w_up, v_conv_w, v_conv_b, v_w_down, v_ln2_g, v_ln2_b):
    cx, cy, cc = _coords()
    c_idx = jnp.reshape(cc, (1,)).astype(jnp.int32)
    k_me = 2 * cx + cy
    k_idx = jnp.reshape(k_me, (1,)).astype(jnp.int32)
    kc_idx = jnp.stack([k_me, cc]).astype(jnp.int32)

    def slot(nm, w, l, run_after=None):
        b = _into_slot(f"slot_{nm}{l}", w, l, k_idx, BF16, run_after)
        return b.reshape(N_CHIPS, 2, b.shape[1] // 2, b.shape[2])

    cw_slot = _into_slot("slot_cw", conv_w.reshape(1, DEPTH * CONV_WIDTH, FF_SHARD), 0, k_idx, F32)
    cw_slot = cw_slot.reshape(N_CHIPS, DEPTH, CONV_WIDTH, FF_SHARD)
    first, token = _gather_start("gather_start0", [[slot("wi", w_in, 0), cw_slot]])
    sl = [{nm: slot(nm, w, l, token) for nm, w in (("wi", w_in), ("wo", w_out), ("wg", w_gate), ("wu", w_up), ("wd", w_down))
           if (nm, l) != ("wi", 0)} for l in range(DEPTH)]
    order = [(l, nm) for l in range(DEPTH) for nm in ("wi", "wo", "wg", "wu", "wd")][1:]
    rest, token = _gather_start("gather_start1", [[sl[l][nm]] for l, nm in order])
    stage_of = {key: st for key, st in zip(order, rest)}

    def mat(b):
        return b.reshape(N_CHIPS, 2 * b.shape[2], b.shape[3])

    fwd0, token = _gather_forward("gather_fwd0", first[0], token)
    wi0, cw_all = _gather_wait("gather_wait0", fwd0, token)
    cw_full = jnp.transpose(cw_all, (1, 2, 0, 3)).reshape(DEPTH, CONV_WIDTH, D_FF)
    tables = _rope_tables()

    passing = {}

    def pass_on(l, nm, after):
        passing[(l, nm)] = _gather_forward(f"gather_fwd_{nm}{l}", stage_of[(l, nm)], after)

    def arrived(l, nm, after):
        i = order.index((l, nm))
        if i + 1 < len(order):
            pass_on(*order[i + 1], after)
            after = passing[order[i + 1]][1]
        return mat(_gather_wait(f"gather_wait_{nm}{l}", passing[(l, nm)][0], after)[0])

    h = x[0]
    h_bf = _to_bf16("x_bf16", h)
    saved = []
    weights = []
    for l in range(DEPTH):
        wi = mat(wi0) if l == 0 else arrived(l, "wi", h)
        proj = _fwd_colsharded(f"proj{l}", h_bf, wi)
        mixed, mixed_bf, raw = _hgrn_fwd(f"hgrn_fwd{l}", proj, lb_logits, a_norm_w[l], l)
        mixed, mixed_bf, lse_b = _attn_fwd(f"dilated_fwd{l}", proj, tables, None, mixed, mixed_bf, n_heads=B_HEADS, rep=1,
                                           q0=QB0, k0=KB0, v0=VB0, m0=A_HEADS, patterns=B_PATTERNS)
        if l == 0:
            pass_on(l, "wo", lse_b)
        sink_b = jnp.broadcast_to(c_sinks[l][:, None, None], (C_HEADS, 8, BLK))
        mixed, mixed_bf, lse_c = _attn_fwd(f"window_fwd{l}", proj, tables, sink_b, mixed, mixed_bf, n_heads=C_HEADS,
                                           rep=C_HEADS // C_KV_HEADS, q0=QC0, k0=KC0, v0=VC0, m0=A_HEADS + B_HEADS,
                                           patterns=C_PATTERNS)
        wo = arrived(l, "wo", lse_c)
        y1 = _fwd_rowsharded(f"wout{l}", mixed_bf, wo, OUT_SHARD)
        x1, x1_bf = _ln_fwd(f"ln1_fwd{l}", h, y1, ln1_g[l], ln1_b[l])
        wg = arrived(l, "wg", x1)
        g = _fwd_colsharded(f"gate{l}", x1_bf, wg, BF16)
        wu = arrived(l, "wu", g)
        u = _fwd_colsharded(f"up{l}", x1_bf, wu, BF16)
        hh = _conv_gate_fwd(f"conv_fwd{l}", g, u, cw_full[l], conv_b[l])
        wd = arrived(l, "wd", hh)
        y2 = _fwd_rowsharded(f"down{l}", hh, wd, FF_SHARD)
        x2, x2_bf = _ln_fwd(f"ln2_fwd{l}", x1, y2, ln2_g[l], ln2_b[l])
        weights.append(dict(wi=wi, wo=wo, wg=wg, wu=wu, wd=wd))
        saved.append((h, h_bf, proj, raw, lse_b, sink_b, lse_c, mixed, mixed_bf, y1, x1, x1_bf, g, u, hh, y2))
        h, h_bf = x2, x2_bf

    dy, loss_part = _loss_head(h, loss_target[0])

    d_res, d_path = None, dy
    small = [None] * DEPTH
    mat_grads = [None] * DEPTH
    prev_ffn = prev_mix_swap = None
    for l in reversed(range(DEPTH)):
        h_in, h_in_bf, proj, raw, lse_b, sink_b, lse_c, mixed, mixed_bf, y1, x1, x1_bf, g, u, hh, y2 = saved[l]
        wi, wo, wg, wu, wd = (weights[l][k] for k in ("wi", "wo", "wg", "wu", "wd"))
        dz2, dz2_bf, d_ln2g, d_ln2b = _ln_bwd(f"ln2_bwd{l}", x1, y2, ln2_g[l], d_res, d_path,
                                              run_after=prev_mix_swap[2][0] if prev_mix_swap else None)
        dhh = _bwd_act_rowsharded(f"down_dx{l}", dz2_bf, wd, FF_SHARD, BF16)
        prev_mix = _rs_scatter(f"{l + 1}m", prev_mix_swap, dhh, c_idx) if prev_mix_swap else None
        d_wd = _bwd_w_rowsharded(f"down_dw{l}", hh, dz2_bf, FF_SHARD)
        dg, du, d_cw, d_cb = _conv_gate_bwd(f"conv_bwd{l}", g, u, cw_full[l], conv_b[l], dhh,
                                            run_after=prev_mix[2][0] if prev_mix else None)
        dx1 = _bwd_act_colsharded(f"gateup_dx{l}", [(dg, wg), (du, wu)])
        d_wg = _bwd_w_colsharded(f"gate_dw{l}", x1_bf, dg)
        d_wu = _bwd_w_colsharded(f"up_dw{l}", x1_bf, du)
        if prev_ffn:
            g_wg, g_wu, g_wd = _rs_end(f"{l + 1}f", prev_ffn, d_wu, kc_idx)
        ffn_swap = _rs_swap(f"{l}f", [d_wg, d_wu, d_wd])
        dz1, dz1_bf, d_ln1g, d_ln1b = _ln_bwd(f"ln1_bwd{l}", h_in, y1, ln1_g[l], dz2, dx1, run_after=ffn_swap[2][0])
        dmix = _bwd_act_rowsharded(f"wout_dx{l}", dz1_bf, wo, OUT_SHARD)
        d_wo = _bwd_w_rowsharded(f"wout_dw{l}", mixed_bf, dz1_bf, OUT_SHARD)
        if prev_mix:
            g_wi, g_wo = _rs_end(f"{l + 1}m", prev_mix, d_wo, kc_idx)
            mat_grads[l + 1] = [g_wi, g_wo, g_wg, g_wu, g_wd]
        s_ffn = _rs_scatter(f"{l}f", ffn_swap, d_wo, c_idx)
        dq_a, df_a, di_a, dg_a, d_nw, d_lb = _hgrn_bwd(f"hgrn_bwd{l}", proj, raw, dmix, lb_logits, a_norm_w[l], l,
                                                       run_after=s_ffn[2][0])
        dq_b, dk_b, dv_b, _ = _attn_bwd(f"dilated_bwd{l}", proj, mixed, dmix, lse_b, tables, None, n_kv=B_HEADS, rep=1,
                                        q0=QB0, k0=KB0, v0=VB0, m0=A_HEADS, patterns=B_PATTERNS)
        dq_c, dk_c, dv_c, d_sink = _attn_bwd(f"window_bwd{l}", proj, mixed, dmix, lse_c, tables, sink_b, n_kv=C_KV_HEADS,
                                             rep=C_HEADS // C_KV_HEADS, q0=QC0, k0=KC0, v0=VC0, m0=A_HEADS + B_HEADS,
                                             patterns=C_PATTERNS)
        dproj = jnp.concatenate([dq_a, df_a, di_a, dg_a, dq_b, dk_b, dv_b, dq_c, dk_c, dv_c], axis=1)
        dxp = _bwd_act_colsharded(f"proj_dx{l}", [(dproj, wi)])
        d_wi = _bwd_w_colsharded(f"proj_dw{l}", h_in_bf, dproj)
        d_res, d_path = dz1, dxp
        prev_ffn, prev_mix_swap = s_ffn, _rs_swap(f"{l}m", [d_wi, d_wo])
        small[l] = (d_lb, d_nw.reshape(A_HEADS, 8, BLK)[:, 0].sum(0), d_sink[:, 0, 0], d_ln1g[0], d_ln1b[0],
                    d_cw, d_cb[0], d_ln2g[0], d_ln2b[0])
    grad_x2 = _axpy("grad_x", d_res, d_path)
    grad_x = grad_x2[None]

    g_lb = small[0][0] + small[1][0]
    per_layer = [jnp.stack([small[0][i], small[1][i]]) for i in range(1, 9)]
    small_shapes = [(DEPTH, 4 * BLK), (DEPTH, BLK), (DEPTH, C_HEADS), (DEPTH, D_MODEL), (DEPTH, D_MODEL),
                    (DEPTH, CONV_WIDTH, D_FF), (DEPTH, D_FF), (DEPTH, D_MODEL), (DEPTH, D_MODEL), (BLK,)]
    rows = _rows_for(small_shapes)
    total = _sum_small(_gather_small(_pack([g_lb] + per_layer + [loss_part[0]], rows), prev_mix_swap[2][0]))
    g_lb, g_nw, g_sink, g_ln1g, g_ln1b, g_cw_full, g_cb, g_ln2g, g_ln2b, loss_row = _unpack(total, small_shapes)
    loss = loss_row[0]
    g_cw = lax.dynamic_slice_in_dim(g_cw_full, k_me * FF_SHARD, FF_SHARD, axis=2)

    sw = [lb_logits, a_norm_w, c_sinks, ln1_g, ln1_b, conv_w, conv_b, ln2_g, ln2_b]
    sg = [g_lb, g_nw, g_sink, g_ln1g, g_ln1b, g_cw, g_cb, g_ln2g, g_ln2b]
    sm = [m_lb_logits, m_a_norm_w, m_c_sinks, m_ln1_g, m_ln1_b, m_conv_w, m_conv_b, m_ln2_g, m_ln2_b]
    sv = [v_lb_logits, v_a_norm_w, v_c_sinks, v_ln1_g, v_ln1_b, v_conv_w, v_conv_b, v_ln2_g, v_ln2_b]
    shapes = [a.shape for a in sw]
    prow = _rows_for(shapes)
    sd, snm, snv = (_unpack(b, shapes) for b in _adamw_small(_pack(sw, prow), _pack(sg, prow), _pack(sm, prow), _pack(sv, prow)))

    names = ["w_in", "w_out", "w_gate", "w_up", "w_down"]
    mw = [w_in, w_out, w_gate, w_up, w_down]
    mm = [m_w_in, m_w_out, m_w_gate, m_w_up, m_w_down]
    mv = [v_w_in, v_w_out, v_w_gate, v_w_up, v_w_down]
    res = [None] * 5
    s_mix = _rs_scatter("0m", prev_mix_swap, total, c_idx)
    ffn0 = _rs_end("0f", prev_ffn, s_mix[2][0], kc_idx)
    for i, g0 in zip((2, 3, 4), ffn0):
        res[i] = _adamw_matrix(f"adamw_{names[i]}", mw[i], [g0, mat_grads[1][i]], mm[i], mv[i])
    mix0 = _rs_end("0m", s_mix, res[4][1], kc_idx)
    for i, g0 in zip((0, 1), mix0):
        res[i] = _adamw_matrix(f"adamw_{names[i]}", mw[i], [g0, mat_grads[1][i]], mm[i], mv[i])
    mg, md, mnm, mnv = ([r[j] for r in res] for j in range(4))

    def ordered(mat, sm_):
        return [mat[0], sm_[0], sm_[1], sm_[2], mat[1], sm_[3], sm_[4], mat[2], mat[3], sm_[5], sm_[6], mat[4], sm_[7], sm_[8]]

    return (loss, grad_x, *ordered(mg, sg), *ordered(md, sd), *ordered(mnm, snm), *ordered(mnv, snv))
```

```python
import functools
import math

import jax
import jax.numpy as jnp
from jax import lax
from jax.experimental import pallas as pl
from jax.experimental.pallas import tpu as pltpu

F32 = jnp.float32
BF16 = jnp.bfloat16

D_MODEL = 2048
SEQ = 2048
DEPTH = 2
HEAD_DIM = 128
A_HEADS = 4
B_HEADS = 6
C_HEADS = 6
C_KV_HEADS = 2
A_CHUNK = 16
DILATED_PATTERNS = ((128, 1), (512, 4), (2048, 16))
C_WINDOW = 128
ROPE_THETA = 500000.0
ROPE_DIM = HEAD_DIM // 4
D_FF = 5632
CONV_WIDTH = 3
LN_EPS = 1e-5
ALPHA = (2 * DEPTH) ** 0.25
IN_WIDTH = 5632
MIX_WIDTH = 2048
ADAM_LR = 0.001
ADAM_B1 = 0.9
ADAM_B2 = 0.999
ADAM_EPS = 1e-08
ADAM_WD = 0.01
ADAM_STEP = 10

N_CHIPS = 4
N_DEV = 8
FF_SHARD = D_FF // N_CHIPS
OUT_SHARD = MIX_WIDTH // N_CHIPS
BLK = 128
N_CHUNK = SEQ // A_CHUNK
SLAB = 32

QA0, FA0, IA0, GA0 = 0, 4, 8, 12
QB0, KB0, VB0 = 16, 22, 28
QC0, KC0, VC0 = 34, 40, 42

VMEM_LIMIT_V7X = 56 * 1024 * 1024
HI = lax.Precision.HIGHEST
MESH = pl.DeviceIdType.MESH


def _cp(sem=None, vmem=VMEM_LIMIT_V7X, **kw):
    return pltpu.CompilerParams(dimension_semantics=sem, vmem_limit_bytes=vmem, **kw)


def _sigmoid(x):
    return 1.0 / (1.0 + jnp.exp(-x))


def _gate_sigmoid(x):
    return 0.5 * jnp.tanh(0.5 * x) + 0.5


def _mm(name, pairs, dims, grid, a_specs, b_specs, out_spec, out_shape, nk=1, acc_shape=None):
    n_pairs = len(pairs)

    def body(*refs):
        o_ref = refs[2 * n_pairs]
        part = None
        for p in range(n_pairs):
            a = refs[2 * p][...].astype(BF16)
            b = refs[2 * p + 1][...].astype(BF16)
            t = lax.dot_general(a, b, dims, preferred_element_type=F32)
            part = t if part is None else part + t
        if nk == 1:
            o_ref[...] = part.astype(o_ref.dtype)
        else:
            acc = refs[2 * n_pairs + 1]
            k = pl.program_id(len(grid) - 1)

            @pl.when(k == 0)
            def _():
                acc[...] = part

            @pl.when(k > 0)
            def _():
                acc[...] += part

            @pl.when(k == nk - 1)
            def _():
                o_ref[...] = acc[...].astype(o_ref.dtype)

    in_specs, args = [], []
    for (a, b), sa, sb in zip(pairs, a_specs, b_specs):
        in_specs += [sa, sb]
        args += [a, b]
    sem = ("parallel",) * (len(grid) - (1 if nk > 1 else 0)) + (("arbitrary",) if nk > 1 else ())
    return pl.pallas_call(
        body, name=name, grid=grid, in_specs=in_specs, out_specs=out_spec, out_shape=out_shape,
        scratch_shapes=[pltpu.VMEM(acc_shape, F32)] if nk > 1 else [],
        compiler_params=_cp(sem),
    )(*args)


NN = (((1,), (0,)), ((), ()))
NT = (((1,), (1,)), ((), ()))
TN = (((0,), (0,)), ((), ()))
TM = 1024


def _fwd_colsharded(name, x, w_stk, out_dtype=F32):
    return _mm(name, [(x, w_stk)], NN, (N_CHIPS, SEQ // TM),
               [pl.BlockSpec((TM, D_MODEL), lambda j, i: (i, 0))],
               [pl.BlockSpec((None, D_MODEL, FF_SHARD), lambda j, i: (j, 0, 0))],
               pl.BlockSpec((TM, FF_SHARD), lambda j, i: (i, j)),
               jax.ShapeDtypeStruct((SEQ, D_FF), out_dtype))


def _fwd_rowsharded(name, a, w_stk, shard):
    tn = D_MODEL
    return _mm(name, [(a, w_stk)], NN, (SEQ // TM, D_MODEL // tn, N_CHIPS),
               [pl.BlockSpec((TM, shard), lambda i, j, k: (i, k))],
               [pl.BlockSpec((None, shard, tn), lambda i, j, k: (k, 0, j))],
               pl.BlockSpec((TM, tn), lambda i, j, k: (i, j)),
               jax.ShapeDtypeStruct((SEQ, D_MODEL), F32), nk=N_CHIPS, acc_shape=(TM, tn))


def _bwd_act_colsharded(name, pairs):
    tn = 1024
    n = len(pairs)
    return _mm(name, pairs, NT, (SEQ // TM, D_MODEL // tn, N_CHIPS),
               [pl.BlockSpec((TM, FF_SHARD), lambda i, j, k: (i, k))] * n,
               [pl.BlockSpec((None, tn, FF_SHARD), lambda i, j, k: (k, j, 0))] * n,
               pl.BlockSpec((TM, tn), lambda i, j, k: (i, j)),
               jax.ShapeDtypeStruct((SEQ, D_MODEL), F32), nk=N_CHIPS, acc_shape=(TM, tn))


def _bwd_act_rowsharded(name, dy, w_stk, shard, out_dtype=F32):
    return _mm(name, [(dy, w_stk)], NT, (N_CHIPS, SEQ // TM),
               [pl.BlockSpec((TM, D_MODEL), lambda j, i: (i, 0))],
               [pl.BlockSpec((None, shard, D_MODEL), lambda j, i: (j, 0, 0))],
               pl.BlockSpec((TM, shard), lambda j, i: (i, j)),
               jax.ShapeDtypeStruct((SEQ, N_CHIPS * shard), out_dtype))


def _bwd_w_colsharded(name, x, dy):
    tm = 1024
    return _mm(name, [(x, dy)], TN, (N_CHIPS, D_MODEL // tm),
               [pl.BlockSpec((SEQ, tm), lambda j, i: (0, i))],
               [pl.BlockSpec((SEQ, FF_SHARD), lambda j, i: (0, j))],
               pl.BlockSpec((None, tm, FF_SHARD), lambda j, i: (j, i, 0)),
               jax.ShapeDtypeStruct((N_CHIPS, D_MODEL, FF_SHARD), BF16))


def _bwd_w_rowsharded(name, a, dy, shard):
    tn = 1024
    return _mm(name, [(a, dy)], TN, (N_CHIPS, D_MODEL // tn),
               [pl.BlockSpec((SEQ, shard), lambda j, i: (0, j))],
               [pl.BlockSpec((SEQ, tn), lambda j, i: (0, i))],
               pl.BlockSpec((None, shard, tn), lambda j, i: (j, 0, i)),
               jax.ShapeDtypeStruct((N_CHIPS, shard, D_MODEL), BF16))


TR = 256


def _ln_fwd(name, x, y, g, b):
    def body(x_ref, y_ref, g_ref, b_ref, o_ref, ob_ref):
        z = ALPHA * x_ref[...] + y_ref[...]
        mu = jnp.mean(z, -1, keepdims=True)
        zc = z - mu
        var = jnp.mean(zc * zc, -1, keepdims=True)
        o = zc * lax.rsqrt(var + LN_EPS) * g_ref[...] + b_ref[...]
        o_ref[...] = o
        ob_ref[...] = o.astype(BF16)

    row = pl.BlockSpec((TR, D_MODEL), lambda i: (i, 0))
    vec = pl.BlockSpec((1, D_MODEL), lambda i: (0, 0))
    return pl.pallas_call(body, name=name, grid=(SEQ // TR,), in_specs=[row, row, vec, vec], out_specs=[row, row],
                          out_shape=[jax.ShapeDtypeStruct((SEQ, D_MODEL), F32), jax.ShapeDtypeStruct((SEQ, D_MODEL), BF16)],
                          compiler_params=_cp(("parallel",)))(x, y, g.reshape(1, -1), b.reshape(1, -1))


def _to_bf16(name, x):
    def body(x_ref, o_ref):
        o_ref[...] = x_ref[...].astype(BF16)

    row = pl.BlockSpec((TR, D_MODEL), lambda i: (i, 0))
    return pl.pallas_call(body, name=name, grid=(SEQ // TR,), in_specs=[row], out_specs=row,
                          out_shape=jax.ShapeDtypeStruct((SEQ, D_MODEL), BF16),
                          compiler_params=_cp(("parallel",)))(x)


def _ln_bwd(name, x, y, g, d_res, d_path, run_after=None):
    has_res = d_res is not None
    n_in = 4 + has_res + (run_after is not None)

    def body(*refs):
        dz_ref, dzb_ref, dg_ref, db_ref = refs[n_in:]
        if has_res:
            x_ref, y_ref, g_ref, r_ref, p_ref = refs[:5]
            dout = ALPHA * r_ref[...] + p_ref[...]
        else:
            x_ref, y_ref, g_ref, p_ref = refs[:4]
            dout = p_ref[...]
        z = ALPHA * x_ref[...] + y_ref[...]
        mu = jnp.mean(z, -1, keepdims=True)
        zc = z - mu
        rstd = lax.rsqrt(jnp.mean(zc * zc, -1, keepdims=True) + LN_EPS)
        zh = zc * rstd
        dzh = dout * g_ref[...]
        dz = rstd * (dzh - jnp.mean(dzh, -1, keepdims=True) - zh * jnp.mean(dzh * zh, -1, keepdims=True))
        dz_ref[...] = dz
        dzb_ref[...] = dz.astype(BF16)
        pg = jnp.sum(dout * zh, 0, keepdims=True)
        pb = jnp.sum(dout, 0, keepdims=True)

        @pl.when(pl.program_id(0) == 0)
        def _():
            dg_ref[...] = pg
            db_ref[...] = pb

        @pl.when(pl.program_id(0) > 0)
        def _():
            dg_ref[...] += pg
            db_ref[...] += pb

    row = pl.BlockSpec((TR, D_MODEL), lambda i: (i, 0))
    vec = pl.BlockSpec((1, D_MODEL), lambda i: (0, 0))
    args = [x, y, g.reshape(1, -1)] + ([d_res] if has_res else []) + [d_path]
    in_specs = [row, row, vec] + ([row] if has_res else []) + [row]
    if run_after is not None:
        args.append(run_after)
        in_specs.append(pl.BlockSpec(memory_space=pl.ANY))
    vshape = jax.ShapeDtypeStruct((1, D_MODEL), F32)
    return pl.pallas_call(body, name=name, grid=(SEQ // TR,), in_specs=in_specs, out_specs=[row, row, vec, vec],
                          out_shape=[jax.ShapeDtypeStruct((SEQ, D_MODEL), F32), jax.ShapeDtypeStruct((SEQ, D_MODEL), BF16),
                                     vshape, vshape],
                          compiler_params=_cp(("arbitrary",)))(*args)


def _loss_head(y, target):
    def body(y_ref, t_ref, dy_ref, l_ref):
        e = y_ref[...] - t_ref[...]
        dy_ref[...] = e * (1.0 / D_MODEL)
        part = jnp.full((8, BLK), 0.5 / D_MODEL * jnp.sum(e * e), F32)

        @pl.when(pl.program_id(0) == 0)
        def _():
            l_ref[...] = part

        @pl.when(pl.program_id(0) > 0)
        def _():
            l_ref[...] += part

    row = pl.BlockSpec((TR, D_MODEL), lambda i: (i, 0))
    return pl.pallas_call(body, name="loss_head", grid=(SEQ // TR,), in_specs=[row, row],
                          out_specs=[row, pl.BlockSpec((8, BLK), lambda i: (0, 0))],
                          out_shape=[jax.ShapeDtypeStruct((SEQ, D_MODEL), F32), jax.ShapeDtypeStruct((8, BLK), F32)],
                          compiler_params=_cp(("arbitrary",)))(y, target)


def _axpy(name, a, b):
    def body(a_ref, b_ref, o_ref):
        o_ref[...] = ALPHA * a_ref[...] + b_ref[...]

    row = pl.BlockSpec((TR, D_MODEL), lambda i: (i, 0))
    return pl.pallas_call(body, name=name, grid=(SEQ // TR,), in_specs=[row, row], out_specs=row,
                          out_shape=jax.ShapeDtypeStruct((SEQ, D_MODEL), F32),
                          compiler_params=_cp(("parallel",)))(a, b)


TC = 512


def _shift_down(x, s, rows):
    if s == 0:
        return x
    return jnp.where(rows >= s, pltpu.roll(x, s, axis=0), 0.0)


def _shift_up(x, s, rows):
    if s == 0:
        return x
    return jnp.where(rows < SEQ - s, pltpu.roll(x, SEQ - s, axis=0), 0.0)


def _conv_gate_fwd(name, g, u, cw, cb):
    def body(g_ref, u_ref, w_ref, b_ref, h_ref):
        gg = g_ref[...].astype(F32)
        rows = lax.broadcasted_iota(jnp.int32, gg.shape, 0)
        gc = b_ref[...] + w_ref[2:3, :] * gg
        gc = gc + w_ref[1:2, :] * _shift_down(gg, 1, rows)
        gc = gc + w_ref[0:1, :] * _shift_down(gg, 2, rows)
        h_ref[...] = (gc * _gate_sigmoid(gc) * u_ref[...].astype(F32)).astype(BF16)

    col = pl.BlockSpec((SEQ, TC), lambda j: (0, j))
    return pl.pallas_call(body, name=name, grid=(D_FF // TC,),
                          in_specs=[col, col, pl.BlockSpec((CONV_WIDTH, TC), lambda j: (0, j)),
                                    pl.BlockSpec((1, TC), lambda j: (0, j))],
                          out_specs=col, out_shape=jax.ShapeDtypeStruct((SEQ, D_FF), BF16),
                          compiler_params=_cp(("parallel",)))(g, u, cw, cb.reshape(1, -1))


def _conv_gate_bwd(name, g, u, cw, cb, dh, run_after=None):
    def body(g_ref, u_ref, w_ref, b_ref, dh_ref, *rest):
        dg_ref, du_ref, dw_ref, db_ref = rest[-4:]
        gg = g_ref[...].astype(F32)
        rows = lax.broadcasted_iota(jnp.int32, gg.shape, 0)
        g1 = _shift_down(gg, 1, rows)
        g2 = _shift_down(gg, 2, rows)
        gc = b_ref[...] + w_ref[2:3, :] * gg + w_ref[1:2, :] * g1 + w_ref[0:1, :] * g2
        sg = _gate_sigmoid(gc)
        act = gc * sg
        dh = dh_ref[...].astype(F32)
        du_ref[...] = (dh * act).astype(BF16)
        dgc = dh * u_ref[...].astype(F32) * (sg * (1.0 + gc * (1.0 - sg)))
        db_ref[...] = jnp.sum(dgc, 0, keepdims=True)
        dw_ref[2:3, :] = jnp.sum(dgc * gg, 0, keepdims=True)
        dw_ref[1:2, :] = jnp.sum(dgc * g1, 0, keepdims=True)
        dw_ref[0:1, :] = jnp.sum(dgc * g2, 0, keepdims=True)
        dg_ref[...] = (w_ref[2:3, :] * dgc + w_ref[1:2, :] * _shift_up(dgc, 1, rows)
                       + w_ref[0:1, :] * _shift_up(dgc, 2, rows)).astype(BF16)

    col = pl.BlockSpec((SEQ, TC), lambda j: (0, j))
    w3 = pl.BlockSpec((CONV_WIDTH, TC), lambda j: (0, j))
    w1 = pl.BlockSpec((1, TC), lambda j: (0, j))
    big = jax.ShapeDtypeStruct((SEQ, D_FF), BF16)
    extra = [] if run_after is None else [run_after]
    return pl.pallas_call(body, name=name, grid=(D_FF // TC,),
                          in_specs=[col, col, w3, w1, col] + [pl.BlockSpec(memory_space=pl.ANY)] * len(extra),
                          out_specs=[col, col, w3, w1],
                          out_shape=[big, big, jax.ShapeDtypeStruct((CONV_WIDTH, D_FF), F32),
                                     jax.ShapeDtypeStruct((1, D_FF), F32)],
                          compiler_params=_cp(("parallel",)))(g, u, cw, cb.reshape(1, -1), dh, *extra)


def _lbs_of(logits, layer):
    m = jnp.max(logits, 0, keepdims=True)
    e = jnp.exp(logits - m)
    p = e / jnp.sum(e, 0, keepdims=True)
    lb = jnp.zeros((1, BLK), F32)
    for r in range(1, layer + 1):
        lb = lb + p[r:r + 1, :]
    return lb, p


def _dlogits_of(p, dlb, layer):
    rows = lax.broadcasted_iota(jnp.int32, p.shape, 0)
    dp = jnp.where((rows >= 1) & (rows <= layer), dlb, 0.0)
    return p * (dp - jnp.sum(p * dp, 0, keepdims=True))


SROWS = SLAB * A_CHUNK
N_SLAB = N_CHUNK // SLAB


def _chunk_prefix(x, rowi):
    for s in (1, 2, 4, 8):
        x = x + jnp.where(rowi >= s, pltpu.roll(x, s, axis=0), 0.0)
    return x


def _chunk_suffix(x, rowi):
    for s in (1, 2, 4, 8):
        x = x + jnp.where(rowi < A_CHUNK - s, pltpu.roll(x, SROWS - s, axis=0), 0.0)
    return x


def _c3(x):
    return x.reshape(SLAB, A_CHUNK, BLK)


def _c2(x):
    return x.reshape(SROWS, BLK)


def _split(x):
    top = lax.bitcast_convert_type(lax.bitcast_convert_type(x, jnp.uint32) & jnp.uint32(0xFFFF0000), F32)
    return top.astype(BF16), (x - top).astype(BF16)


def _lane_sum_b(x2, ones):
    hi, lo = _split(x2)
    return jnp.dot(hi, ones, preferred_element_type=F32) + jnp.dot(lo, ones, preferred_element_type=F32)


def _bmm(eq, a, b):
    ah, al = _split(a)
    bh, bl = _split(b)

    def mm(u, v):
        return jnp.einsum(eq, u, v, preferred_element_type=F32)

    return mm(ah, bh) + (mm(ah, bl) + mm(al, bh))


def _bmm_1pass(eq, a, b):
    return jnp.einsum(eq, a.astype(BF16), b.astype(BF16), preferred_element_type=F32)


def _slab_rows(s):
    return pl.ds(s * SROWS, SROWS)


def _hgrn_prep(q, f, lb):
    rowi = lax.broadcasted_iota(jnp.int32, (SROWS, BLK), 0) & (A_CHUNK - 1)
    sq = _gate_sigmoid(q)
    qc = q * sq
    sf = _sigmoid(f)
    fg = lb + (1.0 - lb) * sf
    kc = 1.0 - fg
    b = _chunk_prefix(jnp.log(fg), rowi)
    b3 = _c3(b)
    blast = b3[:, A_CHUNK - 1:A_CHUNK, :]
    eb = jnp.exp(b)
    ekb = _c2(jnp.exp(blast - b3))
    dec = jnp.exp(blast.reshape(SLAB, BLK))
    return rowi, sq, qc, sf, fg, kc, b, eb, ekb, dec


def _hgrn_slab_states(s, carry, v, ke, dec, dec_ref, u_ref, st_ref):
    dec_ref[pl.ds(s * SLAB, SLAB), :] = dec
    u_ref[...] = _bmm('ncv,nck->nvk', _c3(v), _c3(ke))

    def step(j, c):
        st_ref[j] = c
        return dec_ref[pl.ds(s * SLAB + j, 1), :] * c + u_ref[j]

    return lax.fori_loop(0, SLAB, step, carry)


def _hgrn_fwd(name, proj, lb_logits, nw, layer):
    def body(q_ref, f_ref, i_ref, g_ref, lg_ref, nw_ref, out_ref, outb_ref, raw_ref, dec_ref, u_ref, st_ref):
        lb, _ = _lbs_of(lg_ref[...], layer)
        ones = jnp.ones((BLK, BLK), BF16)
        carry = jnp.zeros((BLK, BLK), F32)
        for s in range(N_SLAB):
            rows = _slab_rows(s)
            v = i_ref[rows, :]
            rowi, sq, qc, sf, fg, kc, b, eb, ekb, dec = _hgrn_prep(q_ref[rows, :], f_ref[rows, :], lb)
            carry = _hgrn_slab_states(s, carry, v, kc * ekb, dec, dec_ref, u_ref, st_ref)
            o = _c2(_bmm('nck,nvk->ncv', _c3(qc * eb), st_ref[...]))
            qc3, kc3, b3, v3, row3 = _c3(qc), _c3(kc), _c3(b), _c3(v), _c3(rowi)
            for j in range(A_CHUNK):
                dj = jnp.exp(jnp.where(row3 >= j, b3 - b3[:, j:j + 1, :], -jnp.inf))
                a = _lane_sum_b(_c2(qc3 * dj * kc3[:, j:j + 1, :]), ones)
                o = o + a * _c2(jnp.broadcast_to(v3[:, j:j + 1, :], v3.shape))
            raw_ref[rows, :] = o
            r = lax.rsqrt(jnp.mean(o * o, -1, keepdims=True) + LN_EPS)
            gg = g_ref[rows, :]
            gated = o * r * nw_ref[...] * (gg * _gate_sigmoid(gg))
            out_ref[rows, :] = gated
            outb_ref[rows, :] = gated.astype(BF16)

    def colblk(c0):
        return pl.BlockSpec((SEQ, BLK), lambda h: (0, c0 + h))

    return pl.pallas_call(
        body, name=name, grid=(A_HEADS,),
        in_specs=[colblk(QA0), colblk(FA0), colblk(IA0), colblk(GA0),
                  pl.BlockSpec((DEPTH, BLK), lambda h: (0, h)), pl.BlockSpec((1, BLK), lambda h: (0, 0))],
        out_specs=[colblk(0), colblk(0), colblk(0)],
        out_shape=[jax.ShapeDtypeStruct((SEQ, MIX_WIDTH), F32), jax.ShapeDtypeStruct((SEQ, MIX_WIDTH), BF16),
                   jax.ShapeDtypeStruct((SEQ, A_HEADS * BLK), F32)],
        scratch_shapes=[pltpu.VMEM((N_CHUNK, BLK), F32), pltpu.VMEM((SLAB, BLK, BLK), F32),
                        pltpu.VMEM((SLAB, BLK, BLK), F32)],
        compiler_params=_cp(("parallel",)))(proj, proj, proj, proj, lb_logits, nw.reshape(1, -1))


def _hgrn_bwd(name, proj, raw, dmix, lb_logits, nw, layer, run_after=None):
    extra = [] if run_after is None else [run_after]

    def body(q_ref, f_ref, i_ref, g_ref, raw_ref, do_ref, lg_ref, nw_ref, *rest):
        (dq_ref, df_ref, di_ref, dg_ref, dnw_ref, dlg_ref,
         dec_ref, u_ref, st_ref, h_ref, dbs_ref, dkc_ref, tot_ref) = rest[-13:]
        lb, p = _lbs_of(lg_ref[...], layer)
        ones = jnp.ones((BLK, BLK), BF16)
        nwv = nw_ref[...]

        carry = jnp.zeros((BLK, BLK), F32)
        for s in range(N_SLAB):
            rows = _slab_rows(s)
            rowi, sq, qc, sf, fg, kc, b, eb, ekb, dec = _hgrn_prep(q_ref[rows, :], f_ref[rows, :], lb)
            carry = _hgrn_slab_states(s, carry, i_ref[rows, :], kc * ekb, dec, dec_ref, u_ref,
                                      st_ref.at[pl.ds(s * SLAB, SLAB)])

        carry = jnp.zeros((BLK, BLK), F32)
        dnw = jnp.zeros((1, BLK), F32)
        for s in reversed(range(N_SLAB)):
            rows = _slab_rows(s)
            q, v = q_ref[rows, :], i_ref[rows, :]
            rowi, sq, qc, sf, fg, kc, b, eb, ekb, dec = _hgrn_prep(q, f_ref[rows, :], lb)
            ke = kc * ekb
            qe = qc * eb

            o = raw_ref[rows, :]
            gg = g_ref[rows, :]
            sgg = _gate_sigmoid(gg)
            dout = do_ref[rows, :]
            r = lax.rsqrt(jnp.mean(o * o, -1, keepdims=True) + LN_EPS)
            oh = o * r
            dg_ref[rows, :] = (dout * oh * nwv * (sgg * (1.0 + gg * (1.0 - sgg)))).astype(BF16)
            dn = dout * (gg * sgg)
            dnw = dnw + jnp.sum(dn * oh, 0, keepdims=True)
            doh = dn * nwv
            do = r * (doh - oh * jnp.mean(doh * oh, -1, keepdims=True))
            do3, qe3, v3, ke3 = _c3(do), _c3(qe), _c3(v), _c3(ke)

            u_ref[...] = _bmm('ncv,nck->nvk', do3, qe3)

            def step(jj, c, s=s):
                j = SLAB - 1 - jj
                h_ref[j] = c
                return u_ref[j] + dec_ref[pl.ds(s * SLAB + j, 1), :] * c

            carry = lax.fori_loop(0, SLAB, step, carry)

            hh = h_ref[...]
            dqc = _c2(_bmm('ncv,nvk->nck', do3, st_ref[pl.ds(s * SLAB, SLAB)])) * eb
            dkc = _c2(_bmm('ncv,nvk->nck', v3, hh)) * ekb
            dv = _c2(_bmm_1pass('nck,nvk->ncv', ke3, hh))

            qc3, kc3, b3, row3 = _c3(qc), _c3(kc), _c3(b), _c3(rowi)
            datt_all = _bmm('niv,njv->nij', do3, v3)
            col = lax.broadcasted_iota(jnp.int32, datt_all.shape, 2)
            att_all = jnp.zeros_like(datt_all)
            for j in range(A_CHUNK):
                dj = jnp.exp(jnp.where(row3 >= j, b3 - b3[:, j:j + 1, :], -jnp.inf))
                kj = kc3[:, j:j + 1, :]
                att = _c3(jnp.dot(_c2(qc3 * dj * kj).astype(BF16), ones, preferred_element_type=F32))
                att_all = jnp.where(col == j, att[:, :, :A_CHUNK], att_all)
                md = dj * datt_all[:, :, j:j + 1]
                dqc = dqc + _c2(md * kj)
                dkc = dkc + _c2(jnp.where(row3 == j, jnp.sum(md * qc3, 1, keepdims=True), 0.0))
            dv = dv + _c2(_bmm_1pass('nij,niv->njv', att_all, do3))
            di_ref[rows, :] = dv.astype(BF16)
            dq_ref[rows, :] = (dqc * (sq * (1.0 + q * (1.0 - sq)))).astype(BF16)

            dbs = _chunk_suffix(qc * dqc - kc * dkc, rowi)
            dbs_ref[rows, :] = dbs
            dkc_ref[rows, :] = dkc
            tot_ref[pl.ds(s * SLAB, SLAB), :] = _c3(dbs)[:, 0:1, :].reshape(SLAB, BLK)
        dnw_ref[...] = jnp.broadcast_to(dnw, (8, BLK))

        rn = lax.broadcasted_iota(jnp.int32, (N_CHUNK, N_CHUNK), 0)
        cn = lax.broadcasted_iota(jnp.int32, (N_CHUNK, N_CHUNK), 1)
        tot_ref[...] = jnp.dot((cn > rn).astype(F32), tot_ref[...], preferred_element_type=F32, precision=HI)
        dlb = jnp.zeros((1, BLK), F32)
        for s in range(N_SLAB):
            rows = _slab_rows(s)
            sf = _sigmoid(f_ref[rows, :])
            fg = lb + (1.0 - lb) * sf
            later = tot_ref[pl.ds(s * SLAB, SLAB), :]
            dlg = _c2(_c3(dbs_ref[rows, :]) + later[:, None, :])
            dfg = dlg / fg - dkc_ref[rows, :]
            df_ref[rows, :] = (dfg * (1.0 - lb) * sf * (1.0 - sf)).astype(BF16)
            dlb = dlb + jnp.sum(dfg * (1.0 - sf), 0, keepdims=True)
        dlg_ref[...] = _dlogits_of(p, dlb, layer)

    def colblk(c0):
        return pl.BlockSpec((SEQ, BLK), lambda h: (0, c0 + h))

    bigb = jax.ShapeDtypeStruct((SEQ, A_HEADS * BLK), BF16)
    return pl.pallas_call(
        body, name=name, grid=(A_HEADS,),
        in_specs=[colblk(QA0), colblk(FA0), colblk(IA0), colblk(GA0), colblk(0), colblk(0),
                  pl.BlockSpec((DEPTH, BLK), lambda h: (0, h)), pl.BlockSpec((1, BLK), lambda h: (0, 0))]
        + [pl.BlockSpec(memory_space=pl.ANY)] * len(extra),
        out_specs=[colblk(0), colblk(0), colblk(0), colblk(0),
                   pl.BlockSpec((8, BLK), lambda h: (h, 0)), pl.BlockSpec((DEPTH, BLK), lambda h: (0, h))],
        out_shape=[bigb, bigb, bigb, bigb, jax.ShapeDtypeStruct((A_HEADS * 8, BLK), F32),
                   jax.ShapeDtypeStruct((DEPTH, A_HEADS * BLK), F32)],
        scratch_shapes=[pltpu.VMEM((N_CHUNK, BLK), F32), pltpu.VMEM((SLAB, BLK, BLK), F32),
                        pltpu.VMEM((N_CHUNK, BLK, BLK), F32), pltpu.VMEM((SLAB, BLK, BLK), F32),
                        pltpu.VMEM((SEQ, BLK), F32), pltpu.VMEM((SEQ, BLK), F32), pltpu.VMEM((N_CHUNK, BLK), F32)],
        compiler_params=_cp(("parallel",)))(proj, proj, proj, proj, raw, dmix, lb_logits, nw.reshape(1, -1), *extra)


SCALE = HEAD_DIM ** -0.5


def _rope_tables():
    half = ROPE_DIM // 2
    inv = ROPE_THETA ** (-jnp.arange(0, ROPE_DIM, 2, dtype=F32) / ROPE_DIM)
    ang = jnp.arange(SEQ, dtype=F32)[:, None] * inv[None, :]
    cos, sin = jnp.cos(ang), jnp.sin(ang)
    pad = jnp.zeros((SEQ, HEAD_DIM - ROPE_DIM), F32)
    zero = jnp.zeros((SEQ, half), F32)
    c = jnp.concatenate([cos, cos, pad + 1.0], 1)
    s_lo = jnp.concatenate([zero, sin, pad], 1)
    s_hi = jnp.concatenate([-sin, zero, pad], 1)
    return c, s_lo, s_hi


def _rope(x, c, s_lo, s_hi):
    half = ROPE_DIM // 2
    return x * c + pltpu.roll(x, half, axis=1) * s_lo + pltpu.roll(x, HEAD_DIM - half, axis=1) * s_hi


def _unrope(dy, c, s_lo, s_hi):
    half = ROPE_DIM // 2
    return dy * c + pltpu.roll(dy * s_lo, HEAD_DIM - half, axis=1) + pltpu.roll(dy * s_hi, half, axis=1)


N_BLK = SEQ // BLK


def _block_rows(dil):
    nb = N_BLK // dil
    return [pl.ds(r + n * BLK * dil, BLK, stride=dil) for r in range(dil) for n in range(nb)]


def _to_blocks(ref, dil):
    if dil == 1:
        return ref[...].reshape(N_BLK, BLK, BLK)
    return jnp.stack([ref[rows, :] for rows in _block_rows(dil)], 0)


def _from_blocks(ref, val, dil, add=False):
    if dil == 1:
        flat = val.reshape(SEQ, BLK)
        ref[...] = ref[...] + flat if add else flat
        return
    for b, rows in enumerate(_block_rows(dil)):
        ref[rows, :] = ref[rows, :] + val[b] if add else val[b]


def _prev_block(x):
    return jnp.concatenate([x[:1], x[:-1]], axis=0)


def _to_next_block(x):
    return jnp.concatenate([x[1:], jnp.zeros_like(x[:1])], axis=0)


def _band_masks(max_lag, dil):
    r = lax.broadcasted_iota(jnp.int32, (N_BLK, BLK, BLK), 1)
    c = lax.broadcasted_iota(jnp.int32, (N_BLK, BLK, BLK), 2)
    b = lax.broadcasted_iota(jnp.int32, (N_BLK, BLK, BLK), 0)
    has_prev = (b % (N_BLK // dil)) != 0
    return r >= c, has_prev & (BLK + r - c <= max_lag)


def _bdot(eq, a, b):
    return jnp.einsum(eq, a, b, preferred_element_type=F32)


def _attn_fwd(name, proj, tables, sink_b, mixed, mixed_bf, *, n_heads, rep, q0, k0, v0, m0, patterns):
    n_pat = len(patterns)
    has_sink = sink_b is not None

    def body(*refs):
        o_ref, ob_ref, l_ref, qr, kr, op, lse_ref = refs[-7:]
        q_ref, k_ref, v_ref, c_ref, sl_ref, sh_ref = refs[:6]
        if has_sink:
            sk = refs[6][0:1, 0:1]
        c, s_lo, s_hi = c_ref[...], sl_ref[...], sh_ref[...]
        qr[...] = _rope(q_ref[...], c, s_lo, s_hi)
        kr[...] = _rope(k_ref[...], c, s_lo, s_hi)
        for p, (max_lag, dil) in enumerate(patterns):
            qa = _to_blocks(qr, dil).astype(BF16)
            ka = _to_blocks(kr, dil).astype(BF16)
            va = _to_blocks(v_ref, dil).astype(BF16)
            own, before = _band_masks(max_lag, dil)
            s1 = jnp.where(own, _bdot('nqd,nkd->nqk', qa, ka) * SCALE, -jnp.inf)
            m = jnp.max(s1, -1, keepdims=True)
            with_prev = dil < N_BLK
            if with_prev:
                kp, vp = _prev_block(ka), _prev_block(va)
                s0 = jnp.where(before, _bdot('nqd,nkd->nqk', qa, kp) * SCALE, -jnp.inf)
                m = jnp.maximum(m, jnp.max(s0, -1, keepdims=True))
            if has_sink:
                m = jnp.maximum(m, sk)
            e1 = jnp.exp(s1 - m)
            den = jnp.sum(e1, -1, keepdims=True)
            o = _bdot('nqk,nkd->nqd', e1.astype(BF16), va)
            if with_prev:
                e0 = jnp.exp(s0 - m)
                den = den + jnp.sum(e0, -1, keepdims=True)
                o = o + _bdot('nqk,nkd->nqd', e0.astype(BF16), vp)
            if has_sink:
                den = den + jnp.exp(sk - m)
            _from_blocks(op.at[p], o / den, dil)
            _from_blocks(lse_ref.at[p], jnp.broadcast_to(m + jnp.log(den), (N_BLK, BLK, BLK)), dil)
        if n_pat == 1:
            acc = op[0]
            l_ref[...] = lse_ref[0]
        else:
            ls = [lse_ref[p] for p in range(n_pat)]
            m = functools.reduce(jnp.maximum, ls)
            es = [jnp.exp(l - m) for l in ls]
            tot = functools.reduce(jnp.add, es)
            acc = None
            for p in range(n_pat):
                t = (es[p] / tot) * op[p]
                acc = t if acc is None else acc + t
            l_ref[...] = m + jnp.log(tot)
        o_ref[...] = acc
        ob_ref[...] = acc.astype(BF16)

    def colblk(fn):
        return pl.BlockSpec((SEQ, BLK), fn)

    tab = pl.BlockSpec((SEQ, BLK), lambda h: (0, 0))
    in_specs = [colblk(lambda h: (0, q0 + h)), colblk(lambda h: (0, k0 + h // rep)), colblk(lambda h: (0, v0 + h // rep)),
                tab, tab, tab]
    args = [proj, proj, proj, *tables]
    if has_sink:
        in_specs.append(pl.BlockSpec((None, 8, BLK), lambda h: (h, 0, 0)))
        args.append(sink_b)
    n_in = len(args)
    in_specs += [pl.BlockSpec(memory_space=pl.ANY)] * 2
    args += [mixed, mixed_bf]
    pat = pltpu.VMEM((n_pat, SEQ, BLK), F32)
    return pl.pallas_call(
        body, name=name, grid=(n_heads,), in_specs=in_specs,
        out_specs=[colblk(lambda h: (0, m0 + h)), colblk(lambda h: (0, m0 + h)),
                   pl.BlockSpec((None, SEQ, BLK), lambda h: (h, 0, 0))],
        out_shape=[jax.ShapeDtypeStruct(mixed.shape, F32), jax.ShapeDtypeStruct(mixed.shape, BF16),
                   jax.ShapeDtypeStruct((n_heads, SEQ, BLK), F32)],
        input_output_aliases={n_in: 0, n_in + 1: 1},
        scratch_shapes=[pltpu.VMEM((SEQ, BLK), F32), pltpu.VMEM((SEQ, BLK), F32), pat, pat],
        compiler_params=_cp(("parallel",)))(*args)


def _attn_bwd(name, proj, mixed, dmix, lse, tables, sink_b, *, n_kv, rep, q0, k0, v0, m0, patterns):
    n_heads = n_kv * rep
    has_sink = sink_b is not None

    def body(*refs):
        if has_sink:
            (q_ref, k_ref, v_ref, o_ref, do_ref, lse_ref, c_ref, sl_ref, sh_ref, sink_ref,
             dq_ref, dk_ref, dv_ref, dsk_ref, qr, kr, dqa, dka, dva, dd) = refs
        else:
            (q_ref, k_ref, v_ref, o_ref, do_ref, lse_ref, c_ref, sl_ref, sh_ref,
             dq_ref, dk_ref, dv_ref, dsk_ref, qr, kr, dqa, dka, dva, dd) = refs
        j = pl.program_id(1)
        c, s_lo, s_hi = c_ref[...], sl_ref[...], sh_ref[...]
        qr[...] = _rope(q_ref[...], c, s_lo, s_hi)
        kr[...] = _rope(k_ref[...], c, s_lo, s_hi)
        dcol = jnp.sum(do_ref[...] * o_ref[...], -1, keepdims=True)
        dd[...] = jnp.broadcast_to(dcol, (SEQ, BLK))

        @pl.when(j == 0)
        def _():
            dka[...] = jnp.zeros((SEQ, BLK), F32)
            dva[...] = jnp.zeros((SEQ, BLK), F32)

        for p, (max_lag, dil) in enumerate(patterns):
            qa = _to_blocks(qr, dil).astype(BF16)
            ka = _to_blocks(kr, dil).astype(BF16)
            va = _to_blocks(v_ref, dil).astype(BF16)
            doa = _to_blocks(do_ref, dil).astype(BF16)
            lcol = _to_blocks(lse_ref, dil)[:, :, 0:1]
            dcb = _to_blocks(dd, dil)[:, :, 0:1]
            own, before = _band_masks(max_lag, dil)

            def probs_and_ds(kk, vv, valid):
                s = _bdot('nqd,nkd->nqk', qa, kk) * SCALE
                a = jnp.where(valid, jnp.exp(s - lcol), 0.0)
                ds = a * (_bdot('nqd,nkd->nqk', doa, vv) - dcb) * SCALE
                return a.astype(BF16), ds.astype(BF16)

            a1, ds1 = probs_and_ds(ka, va, own)
            dq = _bdot('nqk,nkd->nqd', ds1, ka)
            dk = _bdot('nqk,nqd->nkd', ds1, qa)
            dv = _bdot('nqk,nqd->nkd', a1, doa)
            if dil < N_BLK:
                kp, vp = _prev_block(ka), _prev_block(va)
                a0, ds0 = probs_and_ds(kp, vp, before)
                dq = dq + _bdot('nqk,nkd->nqd', ds0, kp)
                dk = dk + _to_next_block(_bdot('nqk,nqd->nkd', ds0, qa))
                dv = dv + _to_next_block(_bdot('nqk,nqd->nkd', a0, doa))
            _from_blocks(dqa, dq, dil, add=p > 0)
            _from_blocks(dka, dk, dil, add=True)
            _from_blocks(dva, dv, dil, add=True)

        if has_sink:
            sk = sink_ref[0:1, 0:1]
            ps = jnp.exp(sk - lse_ref[...][:, 0:1])
            dsk_ref[...] = jnp.full((8, BLK), -jnp.sum(ps * dcol), F32)
        else:
            dsk_ref[...] = jnp.zeros((8, BLK), F32)
        dq_ref[...] = _unrope(dqa[...], c, s_lo, s_hi).astype(BF16)

        @pl.when(j == rep - 1)
        def _():
            dk_ref[...] = _unrope(dka[...], c, s_lo, s_hi).astype(BF16)
            dv_ref[...] = dva[...].astype(BF16)

    def colblk(fn):
        return pl.BlockSpec((SEQ, BLK), fn)

    tab = pl.BlockSpec((SEQ, BLK), lambda g, j: (0, 0))
    in_specs = [colblk(lambda g, j: (0, q0 + g * rep + j)), colblk(lambda g, j: (0, k0 + g)), colblk(lambda g, j: (0, v0 + g)),
                colblk(lambda g, j: (0, m0 + g * rep + j)), colblk(lambda g, j: (0, m0 + g * rep + j)),
                pl.BlockSpec((None, SEQ, BLK), lambda g, j: (g * rep + j, 0, 0)), tab, tab, tab]
    args = [proj, proj, proj, mixed, dmix, lse, *tables]
    if has_sink:
        in_specs.append(pl.BlockSpec((None, 8, BLK), lambda g, j: (g * rep + j, 0, 0)))
        args.append(sink_b)
    acc = pltpu.VMEM((SEQ, BLK), F32)
    return pl.pallas_call(
        body, name=name, grid=(n_kv, rep), in_specs=in_specs,
        out_specs=[colblk(lambda g, j: (0, g * rep + j)), colblk(lambda g, j: (0, g)), colblk(lambda g, j: (0, g)),
                   pl.BlockSpec((None, 8, BLK), lambda g, j: (g * rep + j, 0, 0))],
        out_shape=[jax.ShapeDtypeStruct((SEQ, n_heads * BLK), BF16), jax.ShapeDtypeStruct((SEQ, n_kv * BLK), BF16),
                   jax.ShapeDtypeStruct((SEQ, n_kv * BLK), BF16), jax.ShapeDtypeStruct((n_heads, 8, BLK), F32)],
        scratch_shapes=[acc, acc, acc, acc, acc, acc],
        compiler_params=_cp(("parallel", "arbitrary")))(*args)


B_PATTERNS = tuple((w // d, d) for w, d in DILATED_PATTERNS)
C_PATTERNS = ((C_WINDOW - 1, 1),)


ANY = pl.BlockSpec(memory_space=pl.ANY)
CHIP_MASKS = ((1, 0), (0, 1), (1, 1))


def _coords():
    return lax.axis_index("x"), lax.axis_index("y"), lax.axis_index("c")


def _flip(v, m):
    return 1 - v if m else v


def _into_slot(name, w, layer, k_idx, dtype, run_after=None):
    _, rows, cols = w.shape
    tr = rows // 8 if rows % 64 == 0 else rows

    def body(k_ref, w_ref, *rest):
        rest[-1][...] = w_ref[...].astype(dtype)

    in_specs = [pl.BlockSpec((None, tr, cols), lambda i, k: (layer, i, 0))]
    args = [k_idx, w]
    if run_after is not None:
        in_specs.append(pl.BlockSpec(memory_space=pl.ANY))
        args.append(run_after)
    return pl.pallas_call(
        body, name=name,
        grid_spec=pltpu.PrefetchScalarGridSpec(
            num_scalar_prefetch=1, grid=(rows // tr,), in_specs=in_specs,
            out_specs=pl.BlockSpec((None, tr, cols), lambda i, k: (k[0], i, 0))),
        out_shape=jax.ShapeDtypeStruct((N_CHIPS, rows, cols), dtype),
        compiler_params=_cp(("parallel",)))(*args)


HBM_SPEC = pl.BlockSpec(memory_space=pltpu.HBM)
SEM_SPEC = pl.BlockSpec(memory_space=pltpu.SEMAPHORE)
TOKEN_SPEC = pl.BlockSpec(memory_space=pltpu.VMEM)
TOKEN_SHAPE = jax.ShapeDtypeStruct((8, BLK), F32)
DATAFLOW = pltpu.SideEffectType.DATAFLOW_SIDE_EFFECTING


def _hbm(a):
    return pltpu.with_memory_space_constraint(a, pltpu.HBM)


def _hbm_like(bufs):
    return [pltpu.HBM(b.shape, b.dtype) for b in bufs]


def _gather_start(name, stages):
    flat = [b for st in stages for b in st]
    n, ns = len(flat), len(stages)

    def body(*refs):
        ins = refs[:n]
        sems = refs[n:n + 2 * ns]
        token = refs[-1]
        x, y, c = _coords()
        k_me = 2 * x + y
        a = 0
        for s, st in enumerate(stages):
            for i in range(len(st)):
                mine = ins[a].at[k_me, c]
                for m, (mx, my) in enumerate(CHIP_MASKS):
                    pltpu.make_async_remote_copy(src_ref=mine, dst_ref=mine, send_sem=sems[2 * s].at[i * 3 + m],
                                                 recv_sem=sems[2 * s + 1].at[i * 3 + m],
                                                 device_id=(_flip(x, mx), _flip(y, my), c), device_id_type=MESH).start()
                a += 1
        token[...] = jnp.zeros_like(token)

    sem_shapes = []
    for st in stages:
        sem_shapes += [pltpu.SemaphoreType.DMA((3 * len(st),))] * 2
    out = pl.pallas_call(
        body, name=name, in_specs=[HBM_SPEC] * n,
        out_specs=tuple([SEM_SPEC] * (2 * ns) + [HBM_SPEC] * n + [TOKEN_SPEC]),
        out_shape=tuple(sem_shapes + _hbm_like(flat) + [TOKEN_SHAPE]),
        input_output_aliases={i: 2 * ns + i for i in range(n)},
        compiler_params=pltpu.CompilerParams(has_side_effects=DATAFLOW),
    )(*[_hbm(b) for b in flat])
    sems, bufs, token = out[:2 * ns], out[2 * ns:2 * ns + n], out[-1]
    res, a = [], 0
    for s, st in enumerate(stages):
        res.append((sems[2 * s], sems[2 * s + 1], list(bufs[a:a + len(st)])))
        a += len(st)
    return res, token


def _gather_forward(name, stage, after):
    ssem_in, rsem_in, bufs = stage
    n = len(bufs)

    def body(*refs):
        ins = refs[:n]
        s_in, r_in, _ = refs[n:n + 3]
        s_out, r_out = refs[n + 3:n + 5]
        token = refs[-1]
        x, y, c = _coords()
        for i in range(n):
            for m, (mx, my) in enumerate(CHIP_MASKS):
                kp = 2 * _flip(x, mx) + _flip(y, my)
                blk = ins[i].at[kp, c]
                got = pltpu.make_async_remote_copy(src_ref=blk, dst_ref=blk, send_sem=s_in.at[i * 3 + m],
                                                   recv_sem=r_in.at[i * 3 + m], device_id=(x, y, 1 - c), device_id_type=MESH)
                got.wait_send()
                got.wait_recv()
                pltpu.make_async_remote_copy(src_ref=blk, dst_ref=blk, send_sem=s_out.at[i * 3 + m],
                                             recv_sem=r_out.at[i * 3 + m], device_id=(x, y, 1 - c), device_id_type=MESH).start()
        token[...] = jnp.zeros_like(token)

    sem = pltpu.SemaphoreType.DMA((3 * n,))
    out = pl.pallas_call(
        body, name=name, in_specs=[HBM_SPEC] * n + [SEM_SPEC, SEM_SPEC, ANY],
        out_specs=tuple([SEM_SPEC, SEM_SPEC] + [HBM_SPEC] * n + [TOKEN_SPEC]),
        out_shape=tuple([sem, sem] + _hbm_like(bufs) + [TOKEN_SHAPE]),
        input_output_aliases={i: 2 + i for i in range(n)},
        compiler_params=pltpu.CompilerParams(has_side_effects=DATAFLOW),
    )(*bufs, ssem_in, rsem_in, after)
    return (out[0], out[1], list(out[2:2 + n])), out[-1]


def _gather_wait(name, stage, after):
    ssem, rsem, bufs = stage
    n = len(bufs)

    def body(*refs):
        ins = refs[:n]
        s_in, r_in, _ = refs[n:n + 3]
        x, y, c = _coords()
        for i in range(n):
            for m, (mx, my) in enumerate(CHIP_MASKS):
                kp = 2 * _flip(x, mx) + _flip(y, my)
                sent, got = ins[i].at[kp, c], ins[i].at[kp, 1 - c]
                cp = pltpu.make_async_remote_copy(src_ref=sent, dst_ref=got, send_sem=s_in.at[i * 3 + m],
                                                  recv_sem=r_in.at[i * 3 + m], device_id=(x, y, 1 - c), device_id_type=MESH)
                cp.wait_send()
                cp.wait_recv()

    out = pl.pallas_call(
        body, name=name, in_specs=[HBM_SPEC] * n + [SEM_SPEC, SEM_SPEC, ANY],
        out_specs=tuple([HBM_SPEC] * n), out_shape=tuple(_hbm_like(bufs)),
        input_output_aliases={i: i for i in range(n)},
        compiler_params=pltpu.CompilerParams(has_side_effects=DATAFLOW),
    )(*bufs, ssem, rsem, after)
    return list(out)


def _swap_start(name, grads):
    n = len(grads)

    def body(*refs):
        ins, lands = refs[:n], refs[n:2 * n]
        ssem, rsem = refs[2 * n:2 * n + 2]
        x, y, c = _coords()
        for a in range(n):
            for j in range(N_CHIPS):
                pltpu.make_async_remote_copy(src_ref=ins[a].at[j, 1 - c], dst_ref=lands[a].at[j],
                                             send_sem=ssem.at[a * N_CHIPS + j], recv_sem=rsem.at[a * N_CHIPS + j],
                                             device_id=(x, y, 1 - c), device_id_type=MESH).start()

    sem = pltpu.SemaphoreType.DMA((N_CHIPS * n,))
    land_shapes = [pltpu.HBM((N_CHIPS,) + g.shape[2:], g.dtype) for g in grads]
    out = pl.pallas_call(
        body, name=name, in_specs=[HBM_SPEC] * (2 * n),
        out_specs=tuple([SEM_SPEC, SEM_SPEC] + [HBM_SPEC] * (2 * n)),
        out_shape=tuple([sem, sem] + _hbm_like(grads) + land_shapes),
        input_output_aliases={i: 2 + i for i in range(2 * n)},
        compiler_params=pltpu.CompilerParams(has_side_effects=DATAFLOW),
    )(*[_hbm(g) for g in grads], *[_hbm(lax.empty((N_CHIPS,) + g.shape[2:], g.dtype)) for g in grads])
    return out[0], out[1], list(out[2:2 + n]), list(out[2 + n:])


def _swap_wait(name, started, after):
    ssem, rsem, grads, lands = started
    n = len(grads)

    def body(*refs):
        ins, lnd = refs[:n], refs[n:2 * n]
        s_in, r_in, _ = refs[2 * n:2 * n + 3]
        x, y, c = _coords()
        for a in range(n):
            for j in range(N_CHIPS):
                cp = pltpu.make_async_remote_copy(src_ref=ins[a].at[j, 1 - c], dst_ref=lnd[a].at[j],
                                                  send_sem=s_in.at[a * N_CHIPS + j], recv_sem=r_in.at[a * N_CHIPS + j],
                                                  device_id=(x, y, 1 - c), device_id_type=MESH)
                cp.wait_send()
                cp.wait_recv()

    out = pl.pallas_call(
        body, name=name, in_specs=[HBM_SPEC] * (2 * n) + [SEM_SPEC, SEM_SPEC, ANY],
        out_specs=tuple([HBM_SPEC] * (2 * n)), out_shape=tuple(_hbm_like(grads) + _hbm_like(lands)),
        input_output_aliases={i: i for i in range(2 * n)},
        compiler_params=pltpu.CompilerParams(has_side_effects=DATAFLOW),
    )(*grads, *lands, ssem, rsem, after)
    return list(out[:n]), list(out[n:])


def _scatter_start(name, parts):
    n = len(parts)

    def body(*refs):
        ins, lands = refs[:n], refs[n:2 * n]
        ssem, rsem = refs[2 * n:2 * n + 2]
        x, y, c = _coords()
        k_me = 2 * x + y
        for a in range(n):
            for m, (mx, my) in enumerate(CHIP_MASKS):
                px, py = _flip(x, mx), _flip(y, my)
                pltpu.make_async_remote_copy(src_ref=ins[a].at[2 * px + py], dst_ref=lands[a].at[k_me],
                                             send_sem=ssem.at[a * 3 + m], recv_sem=rsem.at[a * 3 + m],
                                             device_id=(px, py, c), device_id_type=MESH).start()

    sem = pltpu.SemaphoreType.DMA((3 * n,))
    out = pl.pallas_call(
        body, name=name, in_specs=[HBM_SPEC] * (2 * n),
        out_specs=tuple([SEM_SPEC, SEM_SPEC] + [HBM_SPEC] * (2 * n)),
        out_shape=tuple([sem, sem] + _hbm_like(parts) + _hbm_like(parts)),
        input_output_aliases={i: 2 + i for i in range(2 * n)},
        compiler_params=pltpu.CompilerParams(has_side_effects=DATAFLOW),
    )(*[_hbm(p) for p in parts], *[_hbm(lax.empty(p.shape, p.dtype)) for p in parts])
    return out[0], out[1], list(out[2:2 + n]), list(out[2 + n:])


def _scatter_wait(name, started, after):
    ssem, rsem, parts, lands = started
    n = len(parts)

    def body(*refs):
        ins, lnd = refs[:n], refs[n:2 * n]
        s_in, r_in, _ = refs[2 * n:2 * n + 3]
        x, y, c = _coords()
        k_me = 2 * x + y
        for a in range(n):
            for m, (mx, my) in enumerate(CHIP_MASKS):
                px, py = _flip(x, mx), _flip(y, my)
                cp = pltpu.make_async_remote_copy(src_ref=ins[a].at[2 * px + py], dst_ref=lnd[a].at[k_me],
                                                  send_sem=s_in.at[a * 3 + m], recv_sem=r_in.at[a * 3 + m],
                                                  device_id=(px, py, c), device_id_type=MESH)
                cp.wait_send()
                cp.wait_recv()

    out = pl.pallas_call(
        body, name=name, in_specs=[HBM_SPEC] * (2 * n) + [SEM_SPEC, SEM_SPEC, ANY],
        out_specs=tuple([HBM_SPEC] * (2 * n)), out_shape=tuple(_hbm_like(parts) + _hbm_like(lands)),
        input_output_aliases={i: i for i in range(2 * n)},
        compiler_params=pltpu.CompilerParams(has_side_effects=DATAFLOW),
    )(*parts, *lands, ssem, rsem, after)
    return list(out[:n]), list(out[n:])


def _pair_gather(name, bufs):
    n = len(bufs)

    def body(*refs):
        outs = refs[n:2 * n]
        ssem, rsem = refs[2 * n:]
        x, y, c = _coords()
        cps = []
        for a in range(n):
            mine = outs[a].at[c]
            cp = pltpu.make_async_remote_copy(src_ref=mine, dst_ref=mine, send_sem=ssem.at[a],
                                              recv_sem=rsem.at[a], device_id=(x, y, 1 - c), device_id_type=MESH)
            cp.start()
            cps.append(cp)
        for cp in cps:
            cp.wait()

    return pl.pallas_call(
        body, name=name, in_specs=[ANY] * n, out_specs=[ANY] * n,
        out_shape=[jax.ShapeDtypeStruct(b.shape, b.dtype) for b in bufs],
        input_output_aliases={a: a for a in range(n)},
        scratch_shapes=[pltpu.SemaphoreType.DMA((n,)), pltpu.SemaphoreType.DMA((n,))],
        compiler_params=pltpu.CompilerParams(has_side_effects=True),
    )(*bufs)


DEV_MASKS = tuple((mx, my, mc) for mx in (0, 1) for my in (0, 1) for mc in (0, 1) if (mx, my, mc) != (0, 0, 0))


def _gather_small(buf, run_after):
    def body(in_ref, _, out_ref, ssem, rsem, lsem):
        x, y, c = _coords()
        me = 4 * x + 2 * y + c
        cps = [pltpu.make_async_copy(in_ref, out_ref.at[me], lsem)]
        cps[0].start()
        for t, (mx, my, mc) in enumerate(DEV_MASKS):
            cp = pltpu.make_async_remote_copy(src_ref=in_ref, dst_ref=out_ref.at[me], send_sem=ssem.at[t],
                                              recv_sem=rsem.at[t], device_id=(_flip(x, mx), _flip(y, my), _flip(c, mc)),
                                              device_id_type=MESH)
            cp.start()
            cps.append(cp)
        for cp in cps:
            cp.wait()

    return pl.pallas_call(
        body, name="gather_small", in_specs=[ANY, ANY], out_specs=ANY,
        out_shape=jax.ShapeDtypeStruct((N_DEV,) + buf.shape, buf.dtype),
        scratch_shapes=[pltpu.SemaphoreType.DMA((N_DEV - 1,)), pltpu.SemaphoreType.DMA((N_DEV - 1,)),
                        pltpu.SemaphoreType.DMA(())],
        compiler_params=pltpu.CompilerParams(has_side_effects=True),
    )(buf, run_after)


def _row_tile(rows):
    return rows // 2 if rows % 16 == 0 else rows


def _pair_add(name, grad, got, c_idx):
    _, _, r2, cols = grad.shape
    tr = _row_tile(r2)

    def body(c_ref, a_ref, b_ref, o_ref):
        o_ref[...] = (a_ref[...].astype(F32) + b_ref[...].astype(F32)).astype(BF16)

    return pl.pallas_call(
        body, name=name,
        grid_spec=pltpu.PrefetchScalarGridSpec(
            num_scalar_prefetch=1, grid=(N_CHIPS, r2 // tr),
            in_specs=[pl.BlockSpec((None, None, tr, cols), lambda j, i, c: (j, c[0], i, 0)),
                      pl.BlockSpec((None, tr, cols), lambda j, i, c: (j, i, 0))],
            out_specs=pl.BlockSpec((None, tr, cols), lambda j, i, c: (j, i, 0))),
        out_shape=jax.ShapeDtypeStruct((N_CHIPS, r2, cols), BF16),
        compiler_params=_cp(("parallel", "parallel")))(c_idx, grad, got)


def _chip_add(name, part, got, kc_idx):
    _, r2, cols = got.shape
    tr = _row_tile(r2)

    def body(k_ref, p_ref, g1_ref, g2_ref, g3_ref, o_ref):
        acc = p_ref[...].astype(F32)
        for g_ref in (g1_ref, g2_ref, g3_ref):
            acc = acc + g_ref[...].astype(F32)
        o_ref[...] = acc

    def slot(d):
        return pl.BlockSpec((None, tr, cols), lambda i, k: ((k[0] + d) % N_CHIPS, i, 0))

    return pl.pallas_call(
        body, name=name,
        grid_spec=pltpu.PrefetchScalarGridSpec(
            num_scalar_prefetch=1, grid=(r2 // tr,),
            in_specs=[slot(0), slot(1), slot(2), slot(3)],
            out_specs=pl.BlockSpec((None, tr, cols), lambda i, k: (k[1], i, 0))),
        out_shape=jax.ShapeDtypeStruct((2, r2, cols), F32),
        compiler_params=_cp(("parallel",)))(kc_idx, part, got, got, got)


def _adam_math(w, g, m, v):
    m2 = ADAM_B1 * m + (1.0 - ADAM_B1) * g
    v2 = ADAM_B2 * v + (1.0 - ADAM_B2) * (g * g)
    m_hat = m2 / (1.0 - ADAM_B1 ** ADAM_STEP)
    v_hat = v2 / (1.0 - ADAM_B2 ** ADAM_STEP)
    delta = -ADAM_LR * (m_hat / (jnp.sqrt(v_hat) + ADAM_EPS) + ADAM_WD * w)
    return delta, m2, v2


def _adamw_matrix(name, w, g_layers, m, v):
    _, rows, cols = w.shape
    tr = rows // 16 if rows % 128 == 0 else rows // 8

    def body(w_ref, g0_ref, g1_ref, m_ref, v_ref, go_ref, d_ref, mo_ref, vo_ref):
        g = jnp.where(pl.program_id(0) == 0, g0_ref[...], g1_ref[...])
        go_ref[...] = g
        d_ref[...], mo_ref[...], vo_ref[...] = _adam_math(w_ref[...], g, m_ref[...], v_ref[...])

    lay = pl.BlockSpec((None, tr, cols), lambda l, i: (l, i, 0))
    flat = pl.BlockSpec((tr, cols), lambda l, i: (i, 0))
    shp = jax.ShapeDtypeStruct(w.shape, F32)
    return pl.pallas_call(body, name=name, grid=(DEPTH, rows // tr), in_specs=[lay, flat, flat, lay, lay],
                          out_specs=[lay, lay, lay, lay], out_shape=[shp, shp, shp, shp],
                          compiler_params=_cp(("parallel", "parallel")))(w, g_layers[0], g_layers[1], m, v)


def _sum_small(gathered):
    def body(g_ref, o_ref):
        acc = g_ref[0]
        for d in range(1, N_DEV):
            acc = acc + g_ref[d]
        o_ref[...] = acc

    return pl.pallas_call(body, name="sum_small", out_shape=jax.ShapeDtypeStruct(gathered.shape[1:], F32),
                          compiler_params=_cp())(gathered)


def _adamw_small(w, g, m, v):
    def body(w_ref, g_ref, m_ref, v_ref, d_ref, mo_ref, vo_ref):
        d_ref[...], mo_ref[...], vo_ref[...] = _adam_math(w_ref[...], g_ref[...], m_ref[...], v_ref[...])

    shp = jax.ShapeDtypeStruct(w.shape, F32)
    return pl.pallas_call(body, name="adamw_small", out_shape=[shp, shp, shp], compiler_params=_cp())(w, g, m, v)


def _pack(arrays, rows):
    flat = jnp.concatenate([a.reshape(-1) for a in arrays])
    return jnp.pad(flat, (0, rows * BLK - flat.shape[0])).reshape(rows, BLK)


def _unpack(buf, shapes):
    flat = buf.reshape(-1)
    out, pos = [], 0
    for s in shapes:
        n = math.prod(s)
        out.append(flat[pos:pos + n].reshape(s))
        pos += n
    return out


def _rows_for(shapes):
    n = sum(math.prod(s) for s in shapes)
    return -(-n // (8 * BLK)) * 8


def _rs_swap(tag, grads):
    return _swap_start(f"rs_swap_start{tag}", [g.reshape(N_CHIPS, 2, g.shape[1] // 2, g.shape[2]) for g in grads])


def _rs_scatter(tag, swapping, after, c_idx):
    split, got = _swap_wait(f"rs_swap_wait{tag}", swapping, after)
    parts = [_pair_add(f"rs_pair_add{tag}_{i}", s, r, c_idx) for i, (s, r) in enumerate(zip(split, got))]
    return _scatter_start(f"rs_scatter_start{tag}", parts)


def _rs_end(tag, started, after, kc_idx):
    parts, lands = _scatter_wait(f"rs_scatter_wait{tag}", started, after)
    halves = [_chip_add(f"rs_chip_add{tag}_{i}", p, r, kc_idx) for i, (p, r) in enumerate(zip(parts, lands))]
    full = _pair_gather(f"rs_pair_gather{tag}", halves)
    return [f.reshape(2 * f.shape[1], f.shape[2]) for f in full]


def kernel(x, w_in, lb_logits, a_norm_w, c_sinks, w_out, ln1_g, ln1_b, w_gate, w_up, conv_w, conv_b, w_down, ln2_g, ln2_b, loss_target, m_w_in, m_lb_logits, m_a_norm_w, m_c_sinks, m_w_out, m_ln1_g, m_ln1_b, m_w_gate, m_w_up, m_conv_w, m_conv_b, m_w_down, m_ln2_g, m_ln2_b, v_w_in, v_lb_logits, v_a_norm_w, v_c_sinks, v_w_out, v_ln1_g, v_ln1_b, v_w_gate, v_w_up, v_conv_w, v_conv_b, v_w_down, v_ln2_g, v_ln2_b):
    cx, cy, cc = _coords()
    c_idx = jnp.reshape(cc, (1,)).astype(jnp.int32)
    k_me = 2 * cx + cy
    k_idx = jnp.reshape(k_me, (1,)).astype(jnp.int32)
    kc_idx = jnp.stack([k_me, cc]).astype(jnp.int32)

    def slot(nm, w, l, run_after=None):
        b = _into_slot(f"slot_{nm}{l}", w, l, k_idx, BF16, run_after)
        return b.reshape(N_CHIPS, 2, b.shape[1] // 2, b.shape[2])

    cw_slot = _into_slot("slot_cw", conv_w.reshape(1, DEPTH * CONV_WIDTH, FF_SHARD), 0, k_idx, F32)
    cw_slot = cw_slot.reshape(N_CHIPS, DEPTH, CONV_WIDTH, FF_SHARD)
    first, token = _gather_start("gather_start0", [[slot("wi", w_in, 0), cw_slot]])
    sl = [{nm: slot(nm, w, l, token) for nm, w in (("wi", w_in), ("wo", w_out), ("wg", w_gate), ("wu", w_up), ("wd", w_down))
           if (nm, l) != ("wi", 0)} for l in range(DEPTH)]
    order = [(l, nm) for l in range(DEPTH) for nm in ("wi", "wo", "wg", "wu", "wd")][1:]
    rest, token = _gather_start("gather_start1", [[sl[l][nm]] for l, nm in order])
    stage_of = {key: st for key, st in zip(order, rest)}

    def mat(b):
        return b.reshape(N_CHIPS, 2 * b.shape[2], b.shape[3])

    fwd0, token = _gather_forward("gather_fwd0", first[0], token)
    wi0, cw_all = _gather_wait("gather_wait0", fwd0, token)
    cw_full = jnp.transpose(cw_all, (1, 2, 0, 3)).reshape(DEPTH, CONV_WIDTH, D_FF)
    tables = _rope_tables()

    passing = {}

    def pass_on(l, nm, after):
        passing[(l, nm)] = _gather_forward(f"gather_fwd_{nm}{l}", stage_of[(l, nm)], after)

    def arrived(l, nm, after):
        i = order.index((l, nm))
        if i + 1 < len(order):
            pass_on(*order[i + 1], after)
            after = passing[order[i + 1]][1]
        return mat(_gather_wait(f"gather_wait_{nm}{l}", passing[(l, nm)][0], after)[0])

    h = x[0]
    h_bf = _to_bf16("x_bf16", h)
    saved = []
    weights = []
    for l in range(DEPTH):
        wi = mat(wi0) if l == 0 else arrived(l, "wi", h)
        proj = _fwd_colsharded(f"proj{l}", h_bf, wi)
        mixed, mixed_bf, raw = _hgrn_fwd(f"hgrn_fwd{l}", proj, lb_logits, a_norm_w[l], l)
        mixed, mixed_bf, lse_b = _attn_fwd(f"dilated_fwd{l}", proj, tables, None, mixed, mixed_bf, n_heads=B_HEADS, rep=1,
                                           q0=QB0, k0=KB0, v0=VB0, m0=A_HEADS, patterns=B_PATTERNS)
        if l == 0:
            pass_on(l, "wo", lse_b)
        sink_b = jnp.broadcast_to(c_sinks[l][:, None, None], (C_HEADS, 8, BLK))
        mixed, mixed_bf, lse_c = _attn_fwd(f"window_fwd{l}", proj, tables, sink_b, mixed, mixed_bf, n_heads=C_HEADS,
                                           rep=C_HEADS // C_KV_HEADS, q0=QC0, k0=KC0, v0=VC0, m0=A_HEADS + B_HEADS,
                                           patterns=C_PATTERNS)
        wo = arrived(l, "wo", lse_c)
        y1 = _fwd_rowsharded(f"wout{l}", mixed_bf, wo, OUT_SHARD)
        x1, x1_bf = _ln_fwd(f"ln1_fwd{l}", h, y1, ln1_g[l], ln1_b[l])
        wg = arrived(l, "wg", x1)
        g = _fwd_colsharded(f"gate{l}", x1_bf, wg, BF16)
        wu = arrived(l, "wu", g)
        u = _fwd_colsharded(f"up{l}", x1_bf, wu, BF16)
        hh = _conv_gate_fwd(f"conv_fwd{l}", g, u, cw_full[l], conv_b[l])
        wd = arrived(l, "wd", hh)
        y2 = _fwd_rowsharded(f"down{l}", hh, wd, FF_SHARD)
        x2, x2_bf = _ln_fwd(f"ln2_fwd{l}", x1, y2, ln2_g[l], ln2_b[l])
        weights.append(dict(wi=wi, wo=wo, wg=wg, wu=wu, wd=wd))
        saved.append((h, h_bf, proj, raw, lse_b, sink_b, lse_c, mixed, mixed_bf, y1, x1, x1_bf, g, u, hh, y2))
        h, h_bf = x2, x2_bf

    dy, loss_part = _loss_head(h, loss_target[0])

    d_res, d_path = None, dy
    small = [None] * DEPTH
    mat_grads = [None] * DEPTH
    prev_ffn = prev_mix_swap = None
    for l in reversed(range(DEPTH)):
        h_in, h_in_bf, proj, raw, lse_b, sink_b, lse_c, mixed, mixed_bf, y1, x1, x1_bf, g, u, hh, y2 = saved[l]
        wi, wo, wg, wu, wd = (weights[l][k] for k in ("wi", "wo", "wg", "wu", "wd"))
        dz2, dz2_bf, d_ln2g, d_ln2b = _ln_bwd(f"ln2_bwd{l}", x1, y2, ln2_g[l], d_res, d_path,
                                              run_after=prev_mix_swap[2][0] if prev_mix_swap else None)
        dhh = _bwd_act_rowsharded(f"down_dx{l}", dz2_bf, wd, FF_SHARD, BF16)
        prev_mix = _rs_scatter(f"{l + 1}m", prev_mix_swap, dhh, c_idx) if prev_mix_swap else None
        d_wd = _bwd_w_rowsharded(f"down_dw{l}", hh, dz2_bf, FF_SHARD)
        dg, du, d_cw, d_cb = _conv_gate_bwd(f"conv_bwd{l}", g, u, cw_full[l], conv_b[l], dhh,
                                            run_after=prev_mix[2][0] if prev_mix else None)
        dx1 = _bwd_act_colsharded(f"gateup_dx{l}", [(dg, wg), (du, wu)])
        d_wg = _bwd_w_colsharded(f"gate_dw{l}", x1_bf, dg)
        d_wu = _bwd_w_colsharded(f"up_dw{l}", x1_bf, du)
        if prev_ffn:
            g_wg, g_wu, g_wd = _rs_end(f"{l + 1}f", prev_ffn, d_wu, kc_idx)
        ffn_swap = _rs_swap(f"{l}f", [d_wg, d_wu, d_wd])
        dz1, dz1_bf, d_ln1g, d_ln1b = _ln_bwd(f"ln1_bwd{l}", h_in, y1, ln1_g[l], dz2, dx1, run_after=ffn_swap[2][0])
        dmix = _bwd_act_rowsharded(f"wout_dx{l}", dz1_bf, wo, OUT_SHARD)
        d_wo = _bwd_w_rowsharded(f"wout_dw{l}", mixed_bf, dz1_bf, OUT_SHARD)
        if prev_mix:
            g_wi, g_wo = _rs_end(f"{l + 1}m", prev_mix, d_wo, kc_idx)
            mat_grads[l + 1] = [g_wi, g_wo, g_wg, g_wu, g_wd]
        s_ffn = _rs_scatter(f"{l}f", ffn_swap, d_wo, c_idx)
        dq_a, df_a, di_a, dg_a, d_nw, d_lb = _hgrn_bwd(f"hgrn_bwd{l}", proj, raw, dmix, lb_logits, a_norm_w[l], l,
                                                       run_after=s_ffn[2][0])
        dq_b, dk_b, dv_b, _ = _attn_bwd(f"dilated_bwd{l}", proj, mixed, dmix, lse_b, tables, None, n_kv=B_HEADS, rep=1,
                                        q0=QB0, k0=KB0, v0=VB0, m0=A_HEADS, patterns=B_PATTERNS)
        dq_c, dk_c, dv_c, d_sink = _attn_bwd(f"window_bwd{l}", proj, mixed, dmix, lse_c, tables, sink_b, n_kv=C_KV_HEADS,
                                             rep=C_HEADS // C_KV_HEADS, q0=QC0, k0=KC0, v0=VC0, m0=A_HEADS + B_HEADS,
                                             patterns=C_PATTERNS)
        dproj = jnp.concatenate([dq_a, df_a, di_a, dg_a, dq_b, dk_b, dv_b, dq_c, dk_c, dv_c], axis=1)
        dxp = _bwd_act_colsharded(f"proj_dx{l}", [(dproj, wi)])
        d_wi = _bwd_w_colsharded(f"proj_dw{l}", h_in_bf, dproj)
        d_res, d_path = dz1, dxp
        prev_ffn, prev_mix_swap = s_ffn, _rs_swap(f"{l}m", [d_wi, d_wo])
        small[l] = (d_lb, d_nw.reshape(A_HEADS, 8, BLK)[:, 0].sum(0), d_sink[:, 0, 0], d_ln1g[0], d_ln1b[0],
                    d_cw, d_cb[0], d_ln2g[0], d_ln2b[0])
    grad_x2 = _axpy("grad_x", d_res, d_path)
    grad_x = grad_x2[None]

    g_lb = small[0][0] + small[1][0]
    per_layer = [jnp.stack([small[0][i], small[1][i]]) for i in range(1, 9)]
    small_shapes = [(DEPTH, 4 * BLK), (DEPTH, BLK), (DEPTH, C_HEADS), (DEPTH, D_MODEL), (DEPTH, D_MODEL),
                    (DEPTH, CONV_WIDTH, D_FF), (DEPTH, D_FF), (DEPTH, D_MODEL), (DEPTH, D_MODEL), (BLK,)]
    rows = _rows_for(small_shapes)
    total = _sum_small(_gather_small(_pack([g_lb] + per_layer + [loss_part[0]], rows), prev_mix_swap[2][0]))
    g_lb, g_nw, g_sink, g_ln1g, g_ln1b, g_cw_full, g_cb, g_ln2g, g_ln2b, loss_row = _unpack(total, small_shapes)
    loss = loss_row[0]
    g_cw = lax.dynamic_slice_in_dim(g_cw_full, k_me * FF_SHARD, FF_SHARD, axis=2)

    sw = [lb_logits, a_norm_w, c_sinks, ln1_g, ln1_b, conv_w, conv_b, ln2_g, ln2_b]
    sg = [g_lb, g_nw, g_sink, g_ln1g, g_ln1b, g_cw, g_cb, g_ln2g, g_ln2b]
    sm = [m_lb_logits, m_a_norm_w, m_c_sinks, m_ln1_g, m_ln1_b, m_conv_w, m_conv_b, m_ln2_g, m_ln2_b]
    sv = [v_lb_logits, v_a_norm_w, v_c_sinks, v_ln1_g, v_ln1_b, v_conv_w, v_conv_b, v_ln2_g, v_ln2_b]
    shapes = [a.shape for a in sw]
    prow = _rows_for(shapes)
    sd, snm, snv = (_unpack(b, shapes) for b in _adamw_small(_pack(sw, prow), _pack(sg, prow), _pack(sm, prow), _pack(sv, prow)))

    names = ["w_in", "w_out", "w_gate", "w_up", "w_down"]
    mw = [w_in, w_out, w_gate, w_up, w_down]
    mm = [m_w_in, m_w_out, m_w_gate, m_w_up, m_w_down]
    mv = [v_w_in, v_w_out, v_w_gate, v_w_up, v_w_down]
    res = [None] * 5
    s_mix = _rs_scatter("0m", prev_mix_swap, total, c_idx)
    ffn0 = _rs_end("0f", prev_ffn, s_mix[2][0], kc_idx)
    for i, g0 in zip((2, 3, 4), ffn0):
        res[i] = _adamw_matrix(f"adamw_{names[i]}", mw[i], [g0, mat_grads[1][i]], mm[i], mv[i])
    mix0 = _rs_end("0m", s_mix, res[4][1], kc_idx)
    for i, g0 in zip((0, 1), mix0):
        res[i] = _adamw_matrix(f"adamw_{names[i]}", mw[i], [g0, mat_grads[1][i]], mm[i], mv[i])
    mg, md, mnm, mnv = ([r[j] for r in res] for j in range(4))

    def ordered(mat, sm_):
        return [mat[0], sm_[0], sm_[1], sm_[2], mat[1], sm_[3], sm_[4], mat[2], mat[3], sm_[5], sm_[6], mat[4], sm_[7], sm_[8]]

    return (loss, grad_x, *ordered(mg, sg), *ordered(md, sd), *ordered(mnm, snm), *ordered(mnv, snv))
```

```python
import functools
import math

import jax
import jax.numpy as jnp
from jax import lax
from jax.experimental import pallas as pl
from jax.experimental.pallas import tpu as pltpu

F32 = jnp.float32
BF16 = jnp.bfloat16

D_MODEL = 2048
SEQ = 2048
DEPTH = 2
HEAD_DIM = 128
A_HEADS = 4
B_HEADS = 6
C_HEADS = 6
C_KV_HEADS = 2
A_CHUNK = 16
DILATED_PATTERNS = ((128, 1), (512, 4), (2048, 16))
C_WINDOW = 128
ROPE_THETA = 500000.0
ROPE_DIM = HEAD_DIM // 4
D_FF = 5632
CONV_WIDTH = 3
LN_EPS = 1e-5
ALPHA = (2 * DEPTH) ** 0.25
IN_WIDTH = 5632
MIX_WIDTH = 2048
ADAM_LR = 0.001
ADAM_B1 = 0.9
ADAM_B2 = 0.999
ADAM_EPS = 1e-08
ADAM_WD = 0.01
ADAM_STEP = 10

N_CHIPS = 4
N_DEV = 8
FF_SHARD = D_FF // N_CHIPS
OUT_SHARD = MIX_WIDTH // N_CHIPS
BLK = 128
N_CHUNK = SEQ // A_CHUNK
SLAB = 32

QA0, FA0, IA0, GA0 = 0, 4, 8, 12
QB0, KB0, VB0 = 16, 22, 28
QC0, KC0, VC0 = 34, 40, 42

VMEM_LIMIT_V7X = 56 * 1024 * 1024
HI = lax.Precision.HIGHEST
MESH = pl.DeviceIdType.MESH


def _cp(sem=None, vmem=VMEM_LIMIT_V7X, **kw):
    return pltpu.CompilerParams(dimension_semantics=sem, vmem_limit_bytes=vmem, **kw)


def _sigmoid(x):
    return 1.0 / (1.0 + jnp.exp(-x))


def _gate_sigmoid(x):
    return 0.5 * jnp.tanh(0.5 * x) + 0.5


def _mm(name, pairs, dims, grid, a_specs, b_specs, out_spec, out_shape, nk=1, acc_shape=None):
    n_pairs = len(pairs)

    def body(*refs):
        o_ref = refs[2 * n_pairs]
        part = None
        for p in range(n_pairs):
            a = refs[2 * p][...].astype(BF16)
            b = refs[2 * p + 1][...].astype(BF16)
            t = lax.dot_general(a, b, dims, preferred_element_type=F32)
            part = t if part is None else part + t
        if nk == 1:
            o_ref[...] = part.astype(o_ref.dtype)
        else:
            acc = refs[2 * n_pairs + 1]
            k = pl.program_id(len(grid) - 1)

            @pl.when(k == 0)
            def _():
                acc[...] = part

            @pl.when(k > 0)
            def _():
                acc[...] += part

            @pl.when(k == nk - 1)
            def _():
                o_ref[...] = acc[...].astype(o_ref.dtype)

    in_specs, args = [], []
    for (a, b), sa, sb in zip(pairs, a_specs, b_specs):
        in_specs += [sa, sb]
        args += [a, b]
    sem = ("parallel",) * (len(grid) - (1 if nk > 1 else 0)) + (("arbitrary",) if nk > 1 else ())
    return pl.pallas_call(
        body, name=name, grid=grid, in_specs=in_specs, out_specs=out_spec, out_shape=out_shape,
        scratch_shapes=[pltpu.VMEM(acc_shape, F32)] if nk > 1 else [],
        compiler_params=_cp(sem),
    )(*args)


NN = (((1,), (0,)), ((), ()))
NT = (((1,), (1,)), ((), ()))
TN = (((0,), (0,)), ((), ()))
TM = 1024


def _fwd_colsharded(name, x, w_stk, out_dtype=F32):
    return _mm(name, [(x, w_stk)], NN, (N_CHIPS, SEQ // TM),
               [pl.BlockSpec((TM, D_MODEL), lambda j, i: (i, 0))],
               [pl.BlockSpec((None, D_MODEL, FF_SHARD), lambda j, i: (j, 0, 0))],
               pl.BlockSpec((TM, FF_SHARD), lambda j, i: (i, j)),
               jax.ShapeDtypeStruct((SEQ, D_FF), out_dtype))


def _fwd_rowsharded(name, a, w_stk, shard):
    tn = D_MODEL
    return _mm(name, [(a, w_stk)], NN, (SEQ // TM, D_MODEL // tn, N_CHIPS),
               [pl.BlockSpec((TM, shard), lambda i, j, k: (i, k))],
               [pl.BlockSpec((None, shard, tn), lambda i, j, k: (k, 0, j))],
               pl.BlockSpec((TM, tn), lambda i, j, k: (i, j)),
               jax.ShapeDtypeStruct((SEQ, D_MODEL), F32), nk=N_CHIPS, acc_shape=(TM, tn))


def _bwd_act_colsharded(name, pairs):
    tn = 1024
    n = len(pairs)
    return _mm(name, pairs, NT, (SEQ // TM, D_MODEL // tn, N_CHIPS),
               [pl.BlockSpec((TM, FF_SHARD), lambda i, j, k: (i, k))] * n,
               [pl.BlockSpec((None, tn, FF_SHARD), lambda i, j, k: (k, j, 0))] * n,
               pl.BlockSpec((TM, tn), lambda i, j, k: (i, j)),
               jax.ShapeDtypeStruct((SEQ, D_MODEL), F32), nk=N_CHIPS, acc_shape=(TM, tn))


def _bwd_act_rowsharded(name, dy, w_stk, shard, out_dtype=F32):
    return _mm(name, [(dy, w_stk)], NT, (N_CHIPS, SEQ // TM),
               [pl.BlockSpec((TM, D_MODEL), lambda j, i: (i, 0))],
               [pl.BlockSpec((None, shard, D_MODEL), lambda j, i: (j, 0, 0))],
               pl.BlockSpec((TM, shard), lambda j, i: (i, j)),
               jax.ShapeDtypeStruct((SEQ, N_CHIPS * shard), out_dtype))


def _bwd_w_colsharded(name, x, dy):
    tm = 1024
    return _mm(name, [(x, dy)], TN, (N_CHIPS, D_MODEL // tm),
               [pl.BlockSpec((SEQ, tm), lambda j, i: (0, i))],
               [pl.BlockSpec((SEQ, FF_SHARD), lambda j, i: (0, j))],
               pl.BlockSpec((None, tm, FF_SHARD), lambda j, i: (j, i, 0)),
               jax.ShapeDtypeStruct((N_CHIPS, D_MODEL, FF_SHARD), BF16))


def _bwd_w_rowsharded(name, a, dy, shard):
    tn = 1024
    return _mm(name, [(a, dy)], TN, (N_CHIPS, D_MODEL // tn),
               [pl.BlockSpec((SEQ, shard), lambda j, i: (0, j))],
               [pl.BlockSpec((SEQ, tn), lambda j, i: (0, i))],
               pl.BlockSpec((None, shard, tn), lambda j, i: (j, 0, i)),
               jax.ShapeDtypeStruct((N_CHIPS, shard, D_MODEL), BF16))


TR = 256


def _ln_fwd(name, x, y, g, b):
    def body(x_ref, y_ref, g_ref, b_ref, o_ref, ob_ref):
        z = ALPHA * x_ref[...] + y_ref[...]
        mu = jnp.mean(z, -1, keepdims=True)
        zc = z - mu
        var = jnp.mean(zc * zc, -1, keepdims=True)
        o = zc * lax.rsqrt(var + LN_EPS) * g_ref[...] + b_ref[...]
        o_ref[...] = o
        ob_ref[...] = o.astype(BF16)

    row = pl.BlockSpec((TR, D_MODEL), lambda i: (i, 0))
    vec = pl.BlockSpec((1, D_MODEL), lambda i: (0, 0))
    return pl.pallas_call(body, name=name, grid=(SEQ // TR,), in_specs=[row, row, vec, vec], out_specs=[row, row],
                          out_shape=[jax.ShapeDtypeStruct((SEQ, D_MODEL), F32), jax.ShapeDtypeStruct((SEQ, D_MODEL), BF16)],
                          compiler_params=_cp(("parallel",)))(x, y, g.reshape(1, -1), b.reshape(1, -1))


def _to_bf16(name, x):
    def body(x_ref, o_ref):
        o_ref[...] = x_ref[...].astype(BF16)

    row = pl.BlockSpec((TR, D_MODEL), lambda i: (i, 0))
    return pl.pallas_call(body, name=name, grid=(SEQ // TR,), in_specs=[row], out_specs=row,
                          out_shape=jax.ShapeDtypeStruct((SEQ, D_MODEL), BF16),
                          compiler_params=_cp(("parallel",)))(x)


def _ln_bwd(name, x, y, g, d_res, d_path, run_after=None):
    has_res = d_res is not None
    n_in = 4 + has_res + (run_after is not None)

    def body(*refs):
        dz_ref, dzb_ref, dg_ref, db_ref = refs[n_in:]
        if has_res:
            x_ref, y_ref, g_ref, r_ref, p_ref = refs[:5]
            dout = ALPHA * r_ref[...] + p_ref[...]
        else:
            x_ref, y_ref, g_ref, p_ref = refs[:4]
            dout = p_ref[...]
        z = ALPHA * x_ref[...] + y_ref[...]
        mu = jnp.mean(z, -1, keepdims=True)
        zc = z - mu
        rstd = lax.rsqrt(jnp.mean(zc * zc, -1, keepdims=True) + LN_EPS)
        zh = zc * rstd
        dzh = dout * g_ref[...]
        dz = rstd * (dzh - jnp.mean(dzh, -1, keepdims=True) - zh * jnp.mean(dzh * zh, -1, keepdims=True))
        dz_ref[...] = dz
        dzb_ref[...] = dz.astype(BF16)
        pg = jnp.sum(dout * zh, 0, keepdims=True)
        pb = jnp.sum(dout, 0, keepdims=True)

        @pl.when(pl.program_id(0) == 0)
        def _():
            dg_ref[...] = pg
            db_ref[...] = pb

        @pl.when(pl.program_id(0) > 0)
        def _():
            dg_ref[...] += pg
            db_ref[...] += pb

    row = pl.BlockSpec((TR, D_MODEL), lambda i: (i, 0))
    vec = pl.BlockSpec((1, D_MODEL), lambda i: (0, 0))
    args = [x, y, g.reshape(1, -1)] + ([d_res] if has_res else []) + [d_path]
    in_specs = [row, row, vec] + ([row] if has_res else []) + [row]
    if run_after is not None:
        args.append(run_after)
        in_specs.append(pl.BlockSpec(memory_space=pl.ANY))
    vshape = jax.ShapeDtypeStruct((1, D_MODEL), F32)
    return pl.pallas_call(body, name=name, grid=(SEQ // TR,), in_specs=in_specs, out_specs=[row, row, vec, vec],
                          out_shape=[jax.ShapeDtypeStruct((SEQ, D_MODEL), F32), jax.ShapeDtypeStruct((SEQ, D_MODEL), BF16),
                                     vshape, vshape],
                          compiler_params=_cp(("arbitrary",)))(*args)


def _loss_head(y, target):
    def body(y_ref, t_ref, dy_ref, l_ref):
        e = y_ref[...] - t_ref[...]
        dy_ref[...] = e * (1.0 / D_MODEL)
        part = jnp.full((8, BLK), 0.5 / D_MODEL * jnp.sum(e * e), F32)

        @pl.when(pl.program_id(0) == 0)
        def _():
            l_ref[...] = part

        @pl.when(pl.program_id(0) > 0)
        def _():
            l_ref[...] += part

    row = pl.BlockSpec((TR, D_MODEL), lambda i: (i, 0))
    return pl.pallas_call(body, name="loss_head", grid=(SEQ // TR,), in_specs=[row, row],
                          out_specs=[row, pl.BlockSpec((8, BLK), lambda i: (0, 0))],
                          out_shape=[jax.ShapeDtypeStruct((SEQ, D_MODEL), F32), jax.ShapeDtypeStruct((8, BLK), F32)],
                          compiler_params=_cp(("arbitrary",)))(y, target)


def _axpy(name, a, b):
    def body(a_ref, b_ref, o_ref):
        o_ref[...] = ALPHA * a_ref[...] + b_ref[...]

    row = pl.BlockSpec((TR, D_MODEL), lambda i: (i, 0))
    return pl.pallas_call(body, name=name, grid=(SEQ // TR,), in_specs=[row, row], out_specs=row,
                          out_shape=jax.ShapeDtypeStruct((SEQ, D_MODEL), F32),
                          compiler_params=_cp(("parallel",)))(a, b)


TC = 512


def _shift_down(x, s, rows):
    if s == 0:
        return x
    return jnp.where(rows >= s, pltpu.roll(x, s, axis=0), 0.0)


def _shift_up(x, s, rows):
    if s == 0:
        return x
    return jnp.where(rows < SEQ - s, pltpu.roll(x, SEQ - s, axis=0), 0.0)


def _conv_gate_fwd(name, g, u, cw, cb):
    def body(g_ref, u_ref, w_ref, b_ref, h_ref):
        gg = g_ref[...].astype(F32)
        rows = lax.broadcasted_iota(jnp.int32, gg.shape, 0)
        gc = b_ref[...] + w_ref[2:3, :] * gg
        gc = gc + w_ref[1:2, :] * _shift_down(gg, 1, rows)
        gc = gc + w_ref[0:1, :] * _shift_down(gg, 2, rows)
        h_ref[...] = (gc * _gate_sigmoid(gc) * u_ref[...].astype(F32)).astype(BF16)

    col = pl.BlockSpec((SEQ, TC), lambda j: (0, j))
    return pl.pallas_call(body, name=name, grid=(D_FF // TC,),
                          in_specs=[col, col, pl.BlockSpec((CONV_WIDTH, TC), lambda j: (0, j)),
                                    pl.BlockSpec((1, TC), lambda j: (0, j))],
                          out_specs=col, out_shape=jax.ShapeDtypeStruct((SEQ, D_FF), BF16),
                          compiler_params=_cp(("parallel",)))(g, u, cw, cb.reshape(1, -1))


def _conv_gate_bwd(name, g, u, cw, cb, dh, run_after=None):
    def body(g_ref, u_ref, w_ref, b_ref, dh_ref, *rest):
        dg_ref, du_ref, dw_ref, db_ref = rest[-4:]
        gg = g_ref[...].astype(F32)
        rows = lax.broadcasted_iota(jnp.int32, gg.shape, 0)
        g1 = _shift_down(gg, 1, rows)
        g2 = _shift_down(gg, 2, rows)
        gc = b_ref[...] + w_ref[2:3, :] * gg + w_ref[1:2, :] * g1 + w_ref[0:1, :] * g2
        sg = _gate_sigmoid(gc)
        act = gc * sg
        dh = dh_ref[...].astype(F32)
        du_ref[...] = (dh * act).astype(BF16)
        dgc = dh * u_ref[...].astype(F32) * (sg * (1.0 + gc * (1.0 - sg)))
        db_ref[...] = jnp.sum(dgc, 0, keepdims=True)
        dw_ref[2:3, :] = jnp.sum(dgc * gg, 0, keepdims=True)
        dw_ref[1:2, :] = jnp.sum(dgc * g1, 0, keepdims=True)
        dw_ref[0:1, :] = jnp.sum(dgc * g2, 0, keepdims=True)
        dg_ref[...] = (w_ref[2:3, :] * dgc + w_ref[1:2, :] * _shift_up(dgc, 1, rows)
                       + w_ref[0:1, :] * _shift_up(dgc, 2, rows)).astype(BF16)

    col = pl.BlockSpec((SEQ, TC), lambda j: (0, j))
    w3 = pl.BlockSpec((CONV_WIDTH, TC), lambda j: (0, j))
    w1 = pl.BlockSpec((1, TC), lambda j: (0, j))
    big = jax.ShapeDtypeStruct((SEQ, D_FF), BF16)
    extra = [] if run_after is None else [run_after]
    return pl.pallas_call(body, name=name, grid=(D_FF // TC,),
                          in_specs=[col, col, w3, w1, col] + [pl.BlockSpec(memory_space=pl.ANY)] * len(extra),
                          out_specs=[col, col, w3, w1],
                          out_shape=[big, big, jax.ShapeDtypeStruct((CONV_WIDTH, D_FF), F32),
                                     jax.ShapeDtypeStruct((1, D_FF), F32)],
                          compiler_params=_cp(("parallel",)))(g, u, cw, cb.reshape(1, -1), dh, *extra)


def _lbs_of(logits, layer):
    m = jnp.max(logits, 0, keepdims=True)
    e = jnp.exp(logits - m)
    p = e / jnp.sum(e, 0, keepdims=True)
    lb = jnp.zeros((1, BLK), F32)
    for r in range(1, layer + 1):
        lb = lb + p[r:r + 1, :]
    return lb, p


def _dlogits_of(p, dlb, layer):
    rows = lax.broadcasted_iota(jnp.int32, p.shape, 0)
    dp = jnp.where((rows >= 1) & (rows <= layer), dlb, 0.0)
    return p * (dp - jnp.sum(p * dp, 0, keepdims=True))


SROWS = SLAB * A_CHUNK
N_SLAB = N_CHUNK // SLAB


def _chunk_prefix(x, rowi):
    for s in (1, 2, 4, 8):
        x = x + jnp.where(rowi >= s, pltpu.roll(x, s, axis=0), 0.0)
    return x


def _chunk_suffix(x, rowi):
    for s in (1, 2, 4, 8):
        x = x + jnp.where(rowi < A_CHUNK - s, pltpu.roll(x, SROWS - s, axis=0), 0.0)
    return x


def _c3(x):
    return x.reshape(SLAB, A_CHUNK, BLK)


def _c2(x):
    return x.reshape(SROWS, BLK)


def _split(x):
    top = lax.bitcast_convert_type(lax.bitcast_convert_type(x, jnp.uint32) & jnp.uint32(0xFFFF0000), F32)
    return top.astype(BF16), (x - top).astype(BF16)


def _bmm(eq, a, b):
    ah, al = _split(a)
    bh, bl = _split(b)

    def mm(u, v):
        return jnp.einsum(eq, u, v, preferred_element_type=F32)

    return mm(ah, bh) + (mm(ah, bl) + mm(al, bh))


def _bmm_1pass(eq, a, b):
    return jnp.einsum(eq, a.astype(BF16), b.astype(BF16), preferred_element_type=F32)


def _slab_rows(s):
    return pl.ds(s * SROWS, SROWS)


def _hgrn_prep(q, f, lb):
    rowi = lax.broadcasted_iota(jnp.int32, (SROWS, BLK), 0) & (A_CHUNK - 1)
    sq = _gate_sigmoid(q)
    qc = q * sq
    sf = _sigmoid(f)
    fg = lb + (1.0 - lb) * sf
    kc = 1.0 - fg
    b = _chunk_prefix(jnp.log(fg), rowi)
    b3 = _c3(b)
    blast = b3[:, A_CHUNK - 1:A_CHUNK, :]
    eb = jnp.exp(b)
    ekb = _c2(jnp.exp(blast - b3))
    dec = jnp.exp(blast.reshape(SLAB, BLK))
    return rowi, sq, qc, sf, fg, kc, b, eb, ekb, dec


def _hgrn_slab_states(s, carry, v, ke, dec, dec_ref, u_ref, st_ref):
    dec_ref[pl.ds(s * SLAB, SLAB), :] = dec
    u_ref[...] = _bmm('ncv,nck->nvk', _c3(v), _c3(ke))

    def step(j, c):
        st_ref[j] = c
        return dec_ref[pl.ds(s * SLAB + j, 1), :] * c + u_ref[j]

    return lax.fori_loop(0, SLAB, step, carry)


def _hgrn_fwd(name, proj, lb_logits, nw, layer):
    def body(q_ref, f_ref, i_ref, g_ref, lg_ref, nw_ref, out_ref, outb_ref, raw_ref, dec_ref, u_ref, st_ref):
        lb, _ = _lbs_of(lg_ref[...], layer)
        ones = jnp.ones((BLK, BLK), BF16)
        carry = jnp.zeros((BLK, BLK), F32)
        for s in range(N_SLAB):
            rows = _slab_rows(s)
            v = i_ref[rows, :]
            rowi, sq, qc, sf, fg, kc, b, eb, ekb, dec = _hgrn_prep(q_ref[rows, :], f_ref[rows, :], lb)
            carry = _hgrn_slab_states(s, carry, v, kc * ekb, dec, dec_ref, u_ref, st_ref)
            o = _c2(_bmm('nck,nvk->ncv', _c3(qc * eb), st_ref[...]))
            qc3, kc3, b3, v3, row3 = _c3(qc), _c3(kc), _c3(b), _c3(v), _c3(rowi)
            for j in range(A_CHUNK):
                dj = jnp.exp(jnp.where(row3 >= j, b3 - b3[:, j:j + 1, :], -jnp.inf))
                a = jnp.dot(_c2(qc3 * dj * kc3[:, j:j + 1, :]).astype(BF16), ones, preferred_element_type=F32)
                o = o + a * _c2(jnp.broadcast_to(v3[:, j:j + 1, :], v3.shape))
            raw_ref[rows, :] = o
            r = lax.rsqrt(jnp.mean(o * o, -1, keepdims=True) + LN_EPS)
            gg = g_ref[rows, :]
            gated = o * r * nw_ref[...] * (gg * _gate_sigmoid(gg))
            out_ref[rows, :] = gated
            outb_ref[rows, :] = gated.astype(BF16)

    def colblk(c0):
        return pl.BlockSpec((SEQ, BLK), lambda h: (0, c0 + h))

    return pl.pallas_call(
        body, name=name, grid=(A_HEADS,),
        in_specs=[colblk(QA0), colblk(FA0), colblk(IA0), colblk(GA0),
                  pl.BlockSpec((DEPTH, BLK), lambda h: (0, h)), pl.BlockSpec((1, BLK), lambda h: (0, 0))],
        out_specs=[colblk(0), colblk(0), colblk(0)],
        out_shape=[jax.ShapeDtypeStruct((SEQ, MIX_WIDTH), F32), jax.ShapeDtypeStruct((SEQ, MIX_WIDTH), BF16),
                   jax.ShapeDtypeStruct((SEQ, A_HEADS * BLK), F32)],
        scratch_shapes=[pltpu.VMEM((N_CHUNK, BLK), F32), pltpu.VMEM((SLAB, BLK, BLK), F32),
                        pltpu.VMEM((SLAB, BLK, BLK), F32)],
        compiler_params=_cp(("parallel",)))(proj, proj, proj, proj, lb_logits, nw.reshape(1, -1))


def _col_block_copies(stage, sems, dst, col_blocks, first):
    return [pltpu.make_async_copy(stage.at[first + t], dst.at[:, pl.ds(pl.multiple_of(cb * BLK, BLK), BLK)],
                                  sems.at[first + t]) for t, cb in enumerate(col_blocks)]


def _start_col_blocks(stage, sems, dst, col_blocks, first=0):
    for cp in _col_block_copies(stage, sems, dst, col_blocks, first):
        cp.start()


def _wait_col_blocks(stage, sems, dst, count, first=0):
    for cp in _col_block_copies(stage, sems, dst, [0] * count, first):
        cp.wait()


def _hgrn_bwd(name, proj, raw, dmix, lb_logits, nw, layer, run_after=None):
    extra = [] if run_after is None else [run_after]

    def body(q_ref, f_ref, i_ref, g_ref, raw_ref, do_ref, lg_ref, nw_ref, *rest):
        (dproj_ref, dnw_ref, dlg_ref,
         dec_ref, u_ref, st_ref, h_ref, dbs_ref, dkc_ref, tot_ref, stage, stage_sem) = rest[-12:]
        dq_ref, df_ref, di_ref, dg_ref = (stage.at[t] for t in range(4))
        lb, p = _lbs_of(lg_ref[...], layer)
        ones = jnp.ones((BLK, BLK), BF16)
        nwv = nw_ref[...]

        carry = jnp.zeros((BLK, BLK), F32)
        for s in range(N_SLAB):
            rows = _slab_rows(s)
            rowi, sq, qc, sf, fg, kc, b, eb, ekb, dec = _hgrn_prep(q_ref[rows, :], f_ref[rows, :], lb)
            carry = _hgrn_slab_states(s, carry, i_ref[rows, :], kc * ekb, dec, dec_ref, u_ref,
                                      st_ref.at[pl.ds(s * SLAB, SLAB)])

        @pl.when(pl.program_id(0) > 0)
        def _():
            _wait_col_blocks(stage, stage_sem, dproj_ref, 4)

        carry = jnp.zeros((BLK, BLK), F32)
        dnw = jnp.zeros((1, BLK), F32)
        for s in reversed(range(N_SLAB)):
            rows = _slab_rows(s)
            q, v = q_ref[rows, :], i_ref[rows, :]
            rowi, sq, qc, sf, fg, kc, b, eb, ekb, dec = _hgrn_prep(q, f_ref[rows, :], lb)
            ke = kc * ekb
            qe = qc * eb

            o = raw_ref[rows, :]
            gg = g_ref[rows, :]
            sgg = _gate_sigmoid(gg)
            dout = do_ref[rows, :]
            r = lax.rsqrt(jnp.mean(o * o, -1, keepdims=True) + LN_EPS)
            oh = o * r
            dg_ref[rows, :] = (dout * oh * nwv * (sgg * (1.0 + gg * (1.0 - sgg)))).astype(BF16)
            dn = dout * (gg * sgg)
            dnw = dnw + jnp.sum(dn * oh, 0, keepdims=True)
            doh = dn * nwv
            do = r * (doh - oh * jnp.mean(doh * oh, -1, keepdims=True))
            do3, qe3, v3, ke3 = _c3(do), _c3(qe), _c3(v), _c3(ke)

            u_ref[...] = _bmm('ncv,nck->nvk', do3, qe3)

            def step(jj, c, s=s):
                j = SLAB - 1 - jj
                h_ref[j] = c
                return u_ref[j] + dec_ref[pl.ds(s * SLAB + j, 1), :] * c

            carry = lax.fori_loop(0, SLAB, step, carry)

            hh = h_ref[...]
            dqc = _c2(_bmm('ncv,nvk->nck', do3, st_ref[pl.ds(s * SLAB, SLAB)])) * eb
            dkc = _c2(_bmm('ncv,nvk->nck', v3, hh)) * ekb
            dv = _c2(_bmm_1pass('nck,nvk->ncv', ke3, hh))

            qc3, kc3, b3, row3 = _c3(qc), _c3(kc), _c3(b), _c3(rowi)
            datt_all = _bmm('niv,njv->nij', do3, v3)
            col = lax.broadcasted_iota(jnp.int32, datt_all.shape, 2)
            att_all = jnp.zeros_like(datt_all)
            for j in range(A_CHUNK):
                dj = jnp.exp(jnp.where(row3 >= j, b3 - b3[:, j:j + 1, :], -jnp.inf))
                kj = kc3[:, j:j + 1, :]
                att = _c3(jnp.dot(_c2(qc3 * dj * kj).astype(BF16), ones, preferred_element_type=F32))
                att_all = jnp.where(col == j, att[:, :, :A_CHUNK], att_all)
                md = dj * datt_all[:, :, j:j + 1]
                dqc = dqc + _c2(md * kj)
                dkc = dkc + _c2(jnp.where(row3 == j, jnp.sum(md * qc3, 1, keepdims=True), 0.0))
            dv = dv + _c2(_bmm_1pass('nij,niv->njv', att_all, do3))
            di_ref[rows, :] = dv.astype(BF16)
            dq_ref[rows, :] = (dqc * (sq * (1.0 + q * (1.0 - sq)))).astype(BF16)

            dbs = _chunk_suffix(qc * dqc - kc * dkc, rowi)
            dbs_ref[rows, :] = dbs
            dkc_ref[rows, :] = dkc
            tot_ref[pl.ds(s * SLAB, SLAB), :] = _c3(dbs)[:, 0:1, :].reshape(SLAB, BLK)
        dnw_ref[...] = jnp.broadcast_to(dnw, (8, BLK))

        rn = lax.broadcasted_iota(jnp.int32, (N_CHUNK, N_CHUNK), 0)
        cn = lax.broadcasted_iota(jnp.int32, (N_CHUNK, N_CHUNK), 1)
        tot_ref[...] = jnp.dot((cn > rn).astype(F32), tot_ref[...], preferred_element_type=F32, precision=HI)
        dlb = jnp.zeros((1, BLK), F32)
        for s in range(N_SLAB):
            rows = _slab_rows(s)
            sf = _sigmoid(f_ref[rows, :])
            fg = lb + (1.0 - lb) * sf
            later = tot_ref[pl.ds(s * SLAB, SLAB), :]
            dlg = _c2(_c3(dbs_ref[rows, :]) + later[:, None, :])
            dfg = dlg / fg - dkc_ref[rows, :]
            df_ref[rows, :] = (dfg * (1.0 - lb) * sf * (1.0 - sf)).astype(BF16)
            dlb = dlb + jnp.sum(dfg * (1.0 - sf), 0, keepdims=True)
        dlg_ref[...] = _dlogits_of(p, dlb, layer)
        _start_col_blocks(stage, stage_sem, dproj_ref, [c0 + pl.program_id(0) for c0 in (QA0, FA0, IA0, GA0)])

        @pl.when(pl.program_id(0) == A_HEADS - 1)
        def _():
            _wait_col_blocks(stage, stage_sem, dproj_ref, 4)

    def colblk(c0):
        return pl.BlockSpec((SEQ, BLK), lambda h: (0, c0 + h))

    return pl.pallas_call(
        body, name=name, grid=(A_HEADS,),
        in_specs=[colblk(QA0), colblk(FA0), colblk(IA0), colblk(GA0), colblk(0), colblk(0),
                  pl.BlockSpec((DEPTH, BLK), lambda h: (0, h)), pl.BlockSpec((1, BLK), lambda h: (0, 0))]
        + [pl.BlockSpec(memory_space=pl.ANY)] * len(extra),
        out_specs=[pl.BlockSpec(memory_space=pl.ANY),
                   pl.BlockSpec((8, BLK), lambda h: (h, 0)), pl.BlockSpec((DEPTH, BLK), lambda h: (0, h))],
        out_shape=[jax.ShapeDtypeStruct((SEQ, IN_WIDTH), BF16), jax.ShapeDtypeStruct((A_HEADS * 8, BLK), F32),
                   jax.ShapeDtypeStruct((DEPTH, A_HEADS * BLK), F32)],
        scratch_shapes=[pltpu.VMEM((N_CHUNK, BLK), F32), pltpu.VMEM((SLAB, BLK, BLK), F32),
                        pltpu.VMEM((N_CHUNK, BLK, BLK), F32), pltpu.VMEM((SLAB, BLK, BLK), F32),
                        pltpu.VMEM((SEQ, BLK), F32), pltpu.VMEM((SEQ, BLK), F32), pltpu.VMEM((N_CHUNK, BLK), F32),
                        pltpu.VMEM((4, SEQ, BLK), BF16), pltpu.SemaphoreType.DMA((4,))],
        compiler_params=_cp(("arbitrary",)))(proj, proj, proj, proj, raw, dmix, lb_logits, nw.reshape(1, -1), *extra)


SCALE = HEAD_DIM ** -0.5


def _rope_tables():
    half = ROPE_DIM // 2
    inv = ROPE_THETA ** (-jnp.arange(0, ROPE_DIM, 2, dtype=F32) / ROPE_DIM)
    ang = jnp.arange(SEQ, dtype=F32)[:, None] * inv[None, :]
    cos, sin = jnp.cos(ang), jnp.sin(ang)
    pad = jnp.zeros((SEQ, HEAD_DIM - ROPE_DIM), F32)
    zero = jnp.zeros((SEQ, half), F32)
    c = jnp.concatenate([cos, cos, pad + 1.0], 1)
    s_lo = jnp.concatenate([zero, sin, pad], 1)
    s_hi = jnp.concatenate([-sin, zero, pad], 1)
    return c, s_lo, s_hi


def _rope(x, c, s_lo, s_hi):
    half = ROPE_DIM // 2
    return x * c + pltpu.roll(x, half, axis=1) * s_lo + pltpu.roll(x, HEAD_DIM - half, axis=1) * s_hi


def _unrope(dy, c, s_lo, s_hi):
    half = ROPE_DIM // 2
    return dy * c + pltpu.roll(dy * s_lo, HEAD_DIM - half, axis=1) + pltpu.roll(dy * s_hi, half, axis=1)


N_BLK = SEQ // BLK


def _block_rows(dil):
    nb = N_BLK // dil
    return [pl.ds(r + n * BLK * dil, BLK, stride=dil) for r in range(dil) for n in range(nb)]


def _to_blocks(ref, dil):
    if dil == 1:
        return ref[...].reshape(N_BLK, BLK, BLK)
    return jnp.stack([ref[rows, :] for rows in _block_rows(dil)], 0)


def _from_blocks(ref, val, dil, add=False):
    if dil == 1:
        flat = val.reshape(SEQ, BLK)
        ref[...] = ref[...] + flat if add else flat
        return
    for b, rows in enumerate(_block_rows(dil)):
        ref[rows, :] = ref[rows, :] + val[b] if add else val[b]


def _prev_block(x):
    return jnp.concatenate([x[:1], x[:-1]], axis=0)


def _to_next_block(x):
    return jnp.concatenate([x[1:], jnp.zeros_like(x[:1])], axis=0)


def _band_masks(max_lag, dil):
    r = lax.broadcasted_iota(jnp.int32, (N_BLK, BLK, BLK), 1)
    c = lax.broadcasted_iota(jnp.int32, (N_BLK, BLK, BLK), 2)
    b = lax.broadcasted_iota(jnp.int32, (N_BLK, BLK, BLK), 0)
    has_prev = (b % (N_BLK // dil)) != 0
    return r >= c, has_prev & (BLK + r - c <= max_lag)


def _bdot(eq, a, b):
    return jnp.einsum(eq, a, b, preferred_element_type=F32)


def _attn_fwd(name, proj, tables, sink_b, mixed, mixed_bf, *, n_heads, rep, q0, k0, v0, m0, patterns):
    n_pat = len(patterns)
    has_sink = sink_b is not None

    def body(*refs):
        o_ref, ob_ref, l_ref, qr, kr, op, lse_ref = refs[-7:]
        q_ref, k_ref, v_ref, c_ref, sl_ref, sh_ref = refs[:6]
        if has_sink:
            sk = refs[6][0:1, 0:1]
        c, s_lo, s_hi = c_ref[...], sl_ref[...], sh_ref[...]
        qr[...] = _rope(q_ref[...], c, s_lo, s_hi)
        kr[...] = _rope(k_ref[...], c, s_lo, s_hi)
        for p, (max_lag, dil) in enumerate(patterns):
            qa = _to_blocks(qr, dil).astype(BF16)
            ka = _to_blocks(kr, dil).astype(BF16)
            va = _to_blocks(v_ref, dil).astype(BF16)
            own, before = _band_masks(max_lag, dil)
            s1 = jnp.where(own, _bdot('nqd,nkd->nqk', qa, ka) * SCALE, -jnp.inf)
            m = jnp.max(s1, -1, keepdims=True)
            with_prev = dil < N_BLK
            if with_prev:
                kp, vp = _prev_block(ka), _prev_block(va)
                s0 = jnp.where(before, _bdot('nqd,nkd->nqk', qa, kp) * SCALE, -jnp.inf)
                m = jnp.maximum(m, jnp.max(s0, -1, keepdims=True))
            if has_sink:
                m = jnp.maximum(m, sk)
            e1 = jnp.exp(s1 - m)
            den = jnp.sum(e1, -1, keepdims=True)
            o = _bdot('nqk,nkd->nqd', e1.astype(BF16), va)
            if with_prev:
                e0 = jnp.exp(s0 - m)
                den = den + jnp.sum(e0, -1, keepdims=True)
                o = o + _bdot('nqk,nkd->nqd', e0.astype(BF16), vp)
            if has_sink:
                den = den + jnp.exp(sk - m)
            _from_blocks(op.at[p], o / den, dil)
            _from_blocks(lse_ref.at[p], jnp.broadcast_to(m + jnp.log(den), (N_BLK, BLK, BLK)), dil)
        if n_pat == 1:
            acc = op[0]
            l_ref[...] = lse_ref[0]
        else:
            ls = [lse_ref[p] for p in range(n_pat)]
            m = functools.reduce(jnp.maximum, ls)
            es = [jnp.exp(l - m) for l in ls]
            tot = functools.reduce(jnp.add, es)
            acc = None
            for p in range(n_pat):
                t = (es[p] / tot) * op[p]
                acc = t if acc is None else acc + t
            l_ref[...] = m + jnp.log(tot)
        o_ref[...] = acc
        ob_ref[...] = acc.astype(BF16)

    def colblk(fn):
        return pl.BlockSpec((SEQ, BLK), fn)

    tab = pl.BlockSpec((SEQ, BLK), lambda h: (0, 0))
    in_specs = [colblk(lambda h: (0, q0 + h)), colblk(lambda h: (0, k0 + h // rep)), colblk(lambda h: (0, v0 + h // rep)),
                tab, tab, tab]
    args = [proj, proj, proj, *tables]
    if has_sink:
        in_specs.append(pl.BlockSpec((None, 8, BLK), lambda h: (h, 0, 0)))
        args.append(sink_b)
    n_in = len(args)
    in_specs += [pl.BlockSpec(memory_space=pl.ANY)] * 2
    args += [mixed, mixed_bf]
    pat = pltpu.VMEM((n_pat, SEQ, BLK), F32)
    return pl.pallas_call(
        body, name=name, grid=(n_heads,), in_specs=in_specs,
        out_specs=[colblk(lambda h: (0, m0 + h)), colblk(lambda h: (0, m0 + h)),
                   pl.BlockSpec((None, SEQ, BLK), lambda h: (h, 0, 0))],
        out_shape=[jax.ShapeDtypeStruct(mixed.shape, F32), jax.ShapeDtypeStruct(mixed.shape, BF16),
                   jax.ShapeDtypeStruct((n_heads, SEQ, BLK), F32)],
        input_output_aliases={n_in: 0, n_in + 1: 1},
        scratch_shapes=[pltpu.VMEM((SEQ, BLK), F32), pltpu.VMEM((SEQ, BLK), F32), pat, pat],
        compiler_params=_cp(("parallel",)))(*args)


def _attn_bwd(name, proj, mixed, dmix, lse, tables, sink_b, dproj, *, n_kv, rep, q0, k0, v0, m0, patterns):
    n_heads = n_kv * rep
    has_sink = sink_b is not None

    def body(*refs):
        q_ref, k_ref, v_ref, o_ref, do_ref, lse_ref, c_ref, sl_ref, sh_ref = refs[:9]
        sink_ref = refs[9] if has_sink else None
        dproj_ref, dsk_ref, qr, kr, dqa, dka, dva, dd, stage, stage_sem = refs[-10:]
        g, j = pl.program_id(0), pl.program_id(1)
        c, s_lo, s_hi = c_ref[...], sl_ref[...], sh_ref[...]
        qr[...] = _rope(q_ref[...], c, s_lo, s_hi)
        kr[...] = _rope(k_ref[...], c, s_lo, s_hi)
        dcol = jnp.sum(do_ref[...] * o_ref[...], -1, keepdims=True)
        dd[...] = jnp.broadcast_to(dcol, (SEQ, BLK))

        @pl.when(j == 0)
        def _():
            dka[...] = jnp.zeros((SEQ, BLK), F32)
            dva[...] = jnp.zeros((SEQ, BLK), F32)

        for p, (max_lag, dil) in enumerate(patterns):
            qa = _to_blocks(qr, dil).astype(BF16)
            ka = _to_blocks(kr, dil).astype(BF16)
            va = _to_blocks(v_ref, dil).astype(BF16)
            doa = _to_blocks(do_ref, dil).astype(BF16)
            lcol = _to_blocks(lse_ref, dil)[:, :, 0:1]
            dcb = _to_blocks(dd, dil)[:, :, 0:1]
            own, before = _band_masks(max_lag, dil)

            def probs_and_ds(kk, vv, valid):
                s = _bdot('nqd,nkd->nqk', qa, kk) * SCALE
                a = jnp.where(valid, jnp.exp(s - lcol), 0.0)
                ds = a * (_bdot('nqd,nkd->nqk', doa, vv) - dcb) * SCALE
                return a.astype(BF16), ds.astype(BF16)

            a1, ds1 = probs_and_ds(ka, va, own)
            dq = _bdot('nqk,nkd->nqd', ds1, ka)
            dk = _bdot('nqk,nqd->nkd', ds1, qa)
            dv = _bdot('nqk,nqd->nkd', a1, doa)
            if dil < N_BLK:
                kp, vp = _prev_block(ka), _prev_block(va)
                a0, ds0 = probs_and_ds(kp, vp, before)
                dq = dq + _bdot('nqk,nkd->nqd', ds0, kp)
                dk = dk + _to_next_block(_bdot('nqk,nqd->nkd', ds0, qa))
                dv = dv + _to_next_block(_bdot('nqk,nqd->nkd', a0, doa))
            _from_blocks(dqa, dq, dil, add=p > 0)
            _from_blocks(dka, dk, dil, add=True)
            _from_blocks(dva, dv, dil, add=True)

        if has_sink:
            sk = sink_ref[0:1, 0:1]
            ps = jnp.exp(sk - lse_ref[...][:, 0:1])
            dsk_ref[...] = jnp.full((8, BLK), -jnp.sum(ps * dcol), F32)
        else:
            dsk_ref[...] = jnp.zeros((8, BLK), F32)
        @pl.when(g * rep + j > 0)
        def _():
            _wait_col_blocks(stage, stage_sem, dproj_ref, 1)

        stage[0] = _unrope(dqa[...], c, s_lo, s_hi).astype(BF16)
        _start_col_blocks(stage, stage_sem, dproj_ref, [q0 + g * rep + j])

        @pl.when(j == rep - 1)
        def _():
            @pl.when(g > 0)
            def _():
                _wait_col_blocks(stage, stage_sem, dproj_ref, 2, first=1)

            stage[1] = _unrope(dka[...], c, s_lo, s_hi).astype(BF16)
            stage[2] = dva[...].astype(BF16)
            _start_col_blocks(stage, stage_sem, dproj_ref, [k0 + g, v0 + g], first=1)

        @pl.when((g == n_kv - 1) & (j == rep - 1))
        def _():
            _wait_col_blocks(stage, stage_sem, dproj_ref, 3)

    def colblk(fn):
        return pl.BlockSpec((SEQ, BLK), fn)

    tab = pl.BlockSpec((SEQ, BLK), lambda g, j: (0, 0))
    in_specs = [colblk(lambda g, j: (0, q0 + g * rep + j)), colblk(lambda g, j: (0, k0 + g)), colblk(lambda g, j: (0, v0 + g)),
                colblk(lambda g, j: (0, m0 + g * rep + j)), colblk(lambda g, j: (0, m0 + g * rep + j)),
                pl.BlockSpec((None, SEQ, BLK), lambda g, j: (g * rep + j, 0, 0)), tab, tab, tab]
    args = [proj, proj, proj, mixed, dmix, lse, *tables]
    if has_sink:
        in_specs.append(pl.BlockSpec((None, 8, BLK), lambda g, j: (g * rep + j, 0, 0)))
        args.append(sink_b)
    n_in = len(args)
    in_specs.append(pl.BlockSpec(memory_space=pl.ANY))
    args.append(dproj)
    acc = pltpu.VMEM((SEQ, BLK), F32)
    return pl.pallas_call(
        body, name=name, grid=(n_kv, rep), in_specs=in_specs,
        out_specs=[pl.BlockSpec(memory_space=pl.ANY), pl.BlockSpec((None, 8, BLK), lambda g, j: (g * rep + j, 0, 0))],
        out_shape=[jax.ShapeDtypeStruct(dproj.shape, BF16), jax.ShapeDtypeStruct((n_heads, 8, BLK), F32)],
        input_output_aliases={n_in: 0},
        scratch_shapes=[acc, acc, acc, acc, acc, acc, pltpu.VMEM((3, SEQ, BLK), BF16), pltpu.SemaphoreType.DMA((3,))],
        compiler_params=_cp(("arbitrary", "arbitrary")))(*args)


B_PATTERNS = tuple((w // d, d) for w, d in DILATED_PATTERNS)
C_PATTERNS = ((C_WINDOW - 1, 1),)


ANY = pl.BlockSpec(memory_space=pl.ANY)
CHIP_MASKS = ((1, 0), (0, 1), (1, 1))


def _coords():
    return lax.axis_index("x"), lax.axis_index("y"), lax.axis_index("c")


def _flip(v, m):
    return 1 - v if m else v


def _into_slot(name, w, layer, k_idx, dtype, run_after=None):
    _, rows, cols = w.shape
    tr = rows // 8 if rows % 64 == 0 else rows

    def body(k_ref, w_ref, *rest):
        rest[-1][...] = w_ref[...].astype(dtype)

    in_specs = [pl.BlockSpec((None, tr, cols), lambda i, k: (layer, i, 0))]
    args = [k_idx, w]
    if run_after is not None:
        in_specs.append(pl.BlockSpec(memory_space=pl.ANY))
        args.append(run_after)
    return pl.pallas_call(
        body, name=name,
        grid_spec=pltpu.PrefetchScalarGridSpec(
            num_scalar_prefetch=1, grid=(rows // tr,), in_specs=in_specs,
            out_specs=pl.BlockSpec((None, tr, cols), lambda i, k: (k[0], i, 0))),
        out_shape=jax.ShapeDtypeStruct((N_CHIPS, rows, cols), dtype),
        compiler_params=_cp(("parallel",)))(*args)


HBM_SPEC = pl.BlockSpec(memory_space=pltpu.HBM)
SEM_SPEC = pl.BlockSpec(memory_space=pltpu.SEMAPHORE)
TOKEN_SPEC = pl.BlockSpec(memory_space=pltpu.VMEM)
TOKEN_SHAPE = jax.ShapeDtypeStruct((8, BLK), F32)
DATAFLOW = pltpu.SideEffectType.DATAFLOW_SIDE_EFFECTING


def _hbm(a):
    return pltpu.with_memory_space_constraint(a, pltpu.HBM)


def _hbm_like(bufs):
    return [pltpu.HBM(b.shape, b.dtype) for b in bufs]


def _gather_start(name, stages):
    flat = [b for st in stages for b in st]
    n, ns = len(flat), len(stages)

    def body(*refs):
        ins = refs[:n]
        sems = refs[n:n + 2 * ns]
        token = refs[-1]
        x, y, c = _coords()
        k_me = 2 * x + y
        a = 0
        for s, st in enumerate(stages):
            for i in range(len(st)):
                mine = ins[a].at[k_me, c]
                for m, (mx, my) in enumerate(CHIP_MASKS):
                    pltpu.make_async_remote_copy(src_ref=mine, dst_ref=mine, send_sem=sems[2 * s].at[i * 3 + m],
                                                 recv_sem=sems[2 * s + 1].at[i * 3 + m],
                                                 device_id=(_flip(x, mx), _flip(y, my), c), device_id_type=MESH).start()
                a += 1
        token[...] = jnp.zeros_like(token)

    sem_shapes = []
    for st in stages:
        sem_shapes += [pltpu.SemaphoreType.DMA((3 * len(st),))] * 2
    out = pl.pallas_call(
        body, name=name, in_specs=[HBM_SPEC] * n,
        out_specs=tuple([SEM_SPEC] * (2 * ns) + [HBM_SPEC] * n + [TOKEN_SPEC]),
        out_shape=tuple(sem_shapes + _hbm_like(flat) + [TOKEN_SHAPE]),
        input_output_aliases={i: 2 * ns + i for i in range(n)},
        compiler_params=pltpu.CompilerParams(has_side_effects=DATAFLOW),
    )(*[_hbm(b) for b in flat])
    sems, bufs, token = out[:2 * ns], out[2 * ns:2 * ns + n], out[-1]
    res, a = [], 0
    for s, st in enumerate(stages):
        res.append((sems[2 * s], sems[2 * s + 1], list(bufs[a:a + len(st)])))
        a += len(st)
    return res, token


def _gather_forward(name, stage, after):
    ssem_in, rsem_in, bufs = stage
    n = len(bufs)

    def body(*refs):
        ins = refs[:n]
        s_in, r_in, _ = refs[n:n + 3]
        s_out, r_out = refs[n + 3:n + 5]
        token = refs[-1]
        x, y, c = _coords()
        for i in range(n):
            for m, (mx, my) in enumerate(CHIP_MASKS):
                kp = 2 * _flip(x, mx) + _flip(y, my)
                blk = ins[i].at[kp, c]
                got = pltpu.make_async_remote_copy(src_ref=blk, dst_ref=blk, send_sem=s_in.at[i * 3 + m],
                                                   recv_sem=r_in.at[i * 3 + m], device_id=(x, y, 1 - c), device_id_type=MESH)
                got.wait_send()
                got.wait_recv()
                pltpu.make_async_remote_copy(src_ref=blk, dst_ref=blk, send_sem=s_out.at[i * 3 + m],
                                             recv_sem=r_out.at[i * 3 + m], device_id=(x, y, 1 - c), device_id_type=MESH).start()
        token[...] = jnp.zeros_like(token)

    sem = pltpu.SemaphoreType.DMA((3 * n,))
    out = pl.pallas_call(
        body, name=name, in_specs=[HBM_SPEC] * n + [SEM_SPEC, SEM_SPEC, ANY],
        out_specs=tuple([SEM_SPEC, SEM_SPEC] + [HBM_SPEC] * n + [TOKEN_SPEC]),
        out_shape=tuple([sem, sem] + _hbm_like(bufs) + [TOKEN_SHAPE]),
        input_output_aliases={i: 2 + i for i in range(n)},
        compiler_params=pltpu.CompilerParams(has_side_effects=DATAFLOW),
    )(*bufs, ssem_in, rsem_in, after)
    return (out[0], out[1], list(out[2:2 + n])), out[-1]


def _gather_wait(name, stage, after):
    ssem, rsem, bufs = stage
    n = len(bufs)

    def body(*refs):
        ins = refs[:n]
        s_in, r_in, _ = refs[n:n + 3]
        x, y, c = _coords()
        for i in range(n):
            for m, (mx, my) in enumerate(CHIP_MASKS):
                kp = 2 * _flip(x, mx) + _flip(y, my)
                sent, got = ins[i].at[kp, c], ins[i].at[kp, 1 - c]
                cp = pltpu.make_async_remote_copy(src_ref=sent, dst_ref=got, send_sem=s_in.at[i * 3 + m],
                                                  recv_sem=r_in.at[i * 3 + m], device_id=(x, y, 1 - c), device_id_type=MESH)
                cp.wait_send()
                cp.wait_recv()

    out = pl.pallas_call(
        body, name=name, in_specs=[HBM_SPEC] * n + [SEM_SPEC, SEM_SPEC, ANY],
        out_specs=tuple([HBM_SPEC] * n), out_shape=tuple(_hbm_like(bufs)),
        input_output_aliases={i: i for i in range(n)},
        compiler_params=pltpu.CompilerParams(has_side_effects=DATAFLOW),
    )(*bufs, ssem, rsem, after)
    return list(out)


def _swap_start(name, grads):
    n = len(grads)

    def body(*refs):
        ins, lands = refs[:n], refs[n:2 * n]
        ssem, rsem = refs[2 * n:2 * n + 2]
        x, y, c = _coords()
        for a in range(n):
            for j in range(N_CHIPS):
                pltpu.make_async_remote_copy(src_ref=ins[a].at[j, 1 - c], dst_ref=lands[a].at[j],
                                             send_sem=ssem.at[a * N_CHIPS + j], recv_sem=rsem.at[a * N_CHIPS + j],
                                             device_id=(x, y, 1 - c), device_id_type=MESH).start()

    sem = pltpu.SemaphoreType.DMA((N_CHIPS * n,))
    land_shapes = [pltpu.HBM((N_CHIPS,) + g.shape[2:], g.dtype) for g in grads]
    out = pl.pallas_call(
        body, name=name, in_specs=[HBM_SPEC] * (2 * n),
        out_specs=tuple([SEM_SPEC, SEM_SPEC] + [HBM_SPEC] * (2 * n)),
        out_shape=tuple([sem, sem] + _hbm_like(grads) + land_shapes),
        input_output_aliases={i: 2 + i for i in range(2 * n)},
        compiler_params=pltpu.CompilerParams(has_side_effects=DATAFLOW),
    )(*[_hbm(g) for g in grads], *[_hbm(lax.empty((N_CHIPS,) + g.shape[2:], g.dtype)) for g in grads])
    return out[0], out[1], list(out[2:2 + n]), list(out[2 + n:])


def _swap_wait(name, started, after):
    ssem, rsem, grads, lands = started
    n = len(grads)

    def body(*refs):
        ins, lnd = refs[:n], refs[n:2 * n]
        s_in, r_in, _ = refs[2 * n:2 * n + 3]
        x, y, c = _coords()
        for a in range(n):
            for j in range(N_CHIPS):
                cp = pltpu.make_async_remote_copy(src_ref=ins[a].at[j, 1 - c], dst_ref=lnd[a].at[j],
                                                  send_sem=s_in.at[a * N_CHIPS + j], recv_sem=r_in.at[a * N_CHIPS + j],
                                                  device_id=(x, y, 1 - c), device_id_type=MESH)
                cp.wait_send()
                cp.wait_recv()

    out = pl.pallas_call(
        body, name=name, in_specs=[HBM_SPEC] * (2 * n) + [SEM_SPEC, SEM_SPEC, ANY],
        out_specs=tuple([HBM_SPEC] * (2 * n)), out_shape=tuple(_hbm_like(grads) + _hbm_like(lands)),
        input_output_aliases={i: i for i in range(2 * n)},
        compiler_params=pltpu.CompilerParams(has_side_effects=DATAFLOW),
    )(*grads, *lands, ssem, rsem, after)
    return list(out[:n]), list(out[n:])


def _scatter_start(name, parts):
    n = len(parts)

    def body(*refs):
        ins, lands = refs[:n], refs[n:2 * n]
        ssem, rsem = refs[2 * n:2 * n + 2]
        x, y, c = _coords()
        k_me = 2 * x + y
        for a in range(n):
            for m, (mx, my) in enumerate(CHIP_MASKS):
                px, py = _flip(x, mx), _flip(y, my)
                pltpu.make_async_remote_copy(src_ref=ins[a].at[2 * px + py], dst_ref=lands[a].at[k_me],
                                             send_sem=ssem.at[a * 3 + m], recv_sem=rsem.at[a * 3 + m],
                                             device_id=(px, py, c), device_id_type=MESH).start()

    sem = pltpu.SemaphoreType.DMA((3 * n,))
    out = pl.pallas_call(
        body, name=name, in_specs=[HBM_SPEC] * (2 * n),
        out_specs=tuple([SEM_SPEC, SEM_SPEC] + [HBM_SPEC] * (2 * n)),
        out_shape=tuple([sem, sem] + _hbm_like(parts) + _hbm_like(parts)),
        input_output_aliases={i: 2 + i for i in range(2 * n)},
        compiler_params=pltpu.CompilerParams(has_side_effects=DATAFLOW),
    )(*[_hbm(p) for p in parts], *[_hbm(lax.empty(p.shape, p.dtype)) for p in parts])
    return out[0], out[1], list(out[2:2 + n]), list(out[2 + n:])


def _scatter_wait(name, started, after):
    ssem, rsem, parts, lands = started
    n = len(parts)

    def body(*refs):
        ins, lnd = refs[:n], refs[n:2 * n]
        s_in, r_in, _ = refs[2 * n:2 * n + 3]
        x, y, c = _coords()
        k_me = 2 * x + y
        for a in range(n):
            for m, (mx, my) in enumerate(CHIP_MASKS):
                px, py = _flip(x, mx), _flip(y, my)
                cp = pltpu.make_async_remote_copy(src_ref=ins[a].at[2 * px + py], dst_ref=lnd[a].at[k_me],
                                                  send_sem=s_in.at[a * 3 + m], recv_sem=r_in.at[a * 3 + m],
                                                  device_id=(px, py, c), device_id_type=MESH)
                cp.wait_send()
                cp.wait_recv()

    out = pl.pallas_call(
        body, name=name, in_specs=[HBM_SPEC] * (2 * n) + [SEM_SPEC, SEM_SPEC, ANY],
        out_specs=tuple([HBM_SPEC] * (2 * n)), out_shape=tuple(_hbm_like(parts) + _hbm_like(lands)),
        input_output_aliases={i: i for i in range(2 * n)},
        compiler_params=pltpu.CompilerParams(has_side_effects=DATAFLOW),
    )(*parts, *lands, ssem, rsem, after)
    return list(out[:n]), list(out[n:])


def _pair_gather(name, bufs):
    n = len(bufs)

    def body(*refs):
        outs = refs[n:2 * n]
        ssem, rsem = refs[2 * n:]
        x, y, c = _coords()
        cps = []
        for a in range(n):
            mine = outs[a].at[c]
            cp = pltpu.make_async_remote_copy(src_ref=mine, dst_ref=mine, send_sem=ssem.at[a],
                                              recv_sem=rsem.at[a], device_id=(x, y, 1 - c), device_id_type=MESH)
            cp.start()
            cps.append(cp)
        for cp in cps:
            cp.wait()

    return pl.pallas_call(
        body, name=name, in_specs=[ANY] * n, out_specs=[ANY] * n,
        out_shape=[jax.ShapeDtypeStruct(b.shape, b.dtype) for b in bufs],
        input_output_aliases={a: a for a in range(n)},
        scratch_shapes=[pltpu.SemaphoreType.DMA((n,)), pltpu.SemaphoreType.DMA((n,))],
        compiler_params=pltpu.CompilerParams(has_side_effects=True),
    )(*bufs)


DEV_MASKS = tuple((mx, my, mc) for mx in (0, 1) for my in (0, 1) for mc in (0, 1) if (mx, my, mc) != (0, 0, 0))


def _gather_small(buf, run_after):
    def body(in_ref, _, out_ref, ssem, rsem, lsem):
        x, y, c = _coords()
        me = 4 * x + 2 * y + c
        cps = [pltpu.make_async_copy(in_ref, out_ref.at[me], lsem)]
        cps[0].start()
        for t, (mx, my, mc) in enumerate(DEV_MASKS):
            cp = pltpu.make_async_remote_copy(src_ref=in_ref, dst_ref=out_ref.at[me], send_sem=ssem.at[t],
                                              recv_sem=rsem.at[t], device_id=(_flip(x, mx), _flip(y, my), _flip(c, mc)),
                                              device_id_type=MESH)
            cp.start()
            cps.append(cp)
        for cp in cps:
            cp.wait()

    return pl.pallas_call(
        body, name="gather_small", in_specs=[ANY, ANY], out_specs=ANY,
        out_shape=jax.ShapeDtypeStruct((N_DEV,) + buf.shape, buf.dtype),
        scratch_shapes=[pltpu.SemaphoreType.DMA((N_DEV - 1,)), pltpu.SemaphoreType.DMA((N_DEV - 1,)),
                        pltpu.SemaphoreType.DMA(())],
        compiler_params=pltpu.CompilerParams(has_side_effects=True),
    )(buf, run_after)


def _row_tile(rows):
    return rows // 2 if rows % 16 == 0 else rows


def _pair_add(name, grad, got, c_idx):
    _, _, r2, cols = grad.shape
    tr = _row_tile(r2)

    def body(c_ref, a_ref, b_ref, o_ref):
        o_ref[...] = (a_ref[...].astype(F32) + b_ref[...].astype(F32)).astype(BF16)

    return pl.pallas_call(
        body, name=name,
        grid_spec=pltpu.PrefetchScalarGridSpec(
            num_scalar_prefetch=1, grid=(N_CHIPS, r2 // tr),
            in_specs=[pl.BlockSpec((None, None, tr, cols), lambda j, i, c: (j, c[0], i, 0)),
                      pl.BlockSpec((None, tr, cols), lambda j, i, c: (j, i, 0))],
            out_specs=pl.BlockSpec((None, tr, cols), lambda j, i, c: (j, i, 0))),
        out_shape=jax.ShapeDtypeStruct((N_CHIPS, r2, cols), BF16),
        compiler_params=_cp(("parallel", "parallel")))(c_idx, grad, got)


def _chip_add(name, part, got, kc_idx):
    _, r2, cols = got.shape
    tr = _row_tile(r2)

    def body(k_ref, p_ref, g1_ref, g2_ref, g3_ref, o_ref):
        acc = p_ref[...].astype(F32)
        for g_ref in (g1_ref, g2_ref, g3_ref):
            acc = acc + g_ref[...].astype(F32)
        o_ref[...] = acc

    def slot(d):
        return pl.BlockSpec((None, tr, cols), lambda i, k: ((k[0] + d) % N_CHIPS, i, 0))

    return pl.pallas_call(
        body, name=name,
        grid_spec=pltpu.PrefetchScalarGridSpec(
            num_scalar_prefetch=1, grid=(r2 // tr,),
            in_specs=[slot(0), slot(1), slot(2), slot(3)],
            out_specs=pl.BlockSpec((None, tr, cols), lambda i, k: (k[1], i, 0))),
        out_shape=jax.ShapeDtypeStruct((2, r2, cols), F32),
        compiler_params=_cp(("parallel",)))(kc_idx, part, got, got, got)


def _adam_math(w, g, m, v):
    m2 = ADAM_B1 * m + (1.0 - ADAM_B1) * g
    v2 = ADAM_B2 * v + (1.0 - ADAM_B2) * (g * g)
    m_hat = m2 / (1.0 - ADAM_B1 ** ADAM_STEP)
    v_hat = v2 / (1.0 - ADAM_B2 ** ADAM_STEP)
    delta = -ADAM_LR * (m_hat / (jnp.sqrt(v_hat) + ADAM_EPS) + ADAM_WD * w)
    return delta, m2, v2


def _adamw_matrix(name, w, g_layers, m, v):
    _, rows, cols = w.shape
    tr = rows // 16 if rows % 128 == 0 else rows // 8

    def body(w_ref, g0_ref, g1_ref, m_ref, v_ref, go_ref, d_ref, mo_ref, vo_ref):
        g = jnp.where(pl.program_id(0) == 0, g0_ref[...], g1_ref[...])
        go_ref[...] = g
        d_ref[...], mo_ref[...], vo_ref[...] = _adam_math(w_ref[...], g, m_ref[...], v_ref[...])

    lay = pl.BlockSpec((None, tr, cols), lambda l, i: (l, i, 0))
    flat = pl.BlockSpec((tr, cols), lambda l, i: (i, 0))
    shp = jax.ShapeDtypeStruct(w.shape, F32)
    return pl.pallas_call(body, name=name, grid=(DEPTH, rows // tr), in_specs=[lay, flat, flat, lay, lay],
                          out_specs=[lay, lay, lay, lay], out_shape=[shp, shp, shp, shp],
                          compiler_params=_cp(("parallel", "parallel")))(w, g_layers[0], g_layers[1], m, v)


def _sum_small(gathered):
    def body(g_ref, o_ref):
        acc = g_ref[0]
        for d in range(1, N_DEV):
            acc = acc + g_ref[d]
        o_ref[...] = acc

    return pl.pallas_call(body, name="sum_small", out_shape=jax.ShapeDtypeStruct(gathered.shape[1:], F32),
                          compiler_params=_cp())(gathered)


def _adamw_small(w, g, m, v):
    def body(w_ref, g_ref, m_ref, v_ref, d_ref, mo_ref, vo_ref):
        d_ref[...], mo_ref[...], vo_ref[...] = _adam_math(w_ref[...], g_ref[...], m_ref[...], v_ref[...])

    shp = jax.ShapeDtypeStruct(w.shape, F32)
    return pl.pallas_call(body, name="adamw_small", out_shape=[shp, shp, shp], compiler_params=_cp())(w, g, m, v)


def _pack(arrays, rows):
    flat = jnp.concatenate([a.reshape(-1) for a in arrays])
    return jnp.pad(flat, (0, rows * BLK - flat.shape[0])).reshape(rows, BLK)


def _unpack(buf, shapes):
    flat = buf.reshape(-1)
    out, pos = [], 0
    for s in shapes:
        n = math.prod(s)
        out.append(flat[pos:pos + n].reshape(s))
        pos += n
    return out


def _rows_for(shapes):
    n = sum(math.prod(s) for s in shapes)
    return -(-n // (8 * BLK)) * 8


def _rs_swap(tag, grads):
    return _swap_start(f"rs_swap_start{tag}", [g.reshape(N_CHIPS, 2, g.shape[1] // 2, g.shape[2]) for g in grads])


def _rs_scatter(tag, swapping, after, c_idx):
    split, got = _swap_wait(f"rs_swap_wait{tag}", swapping, after)
    parts = [_pair_add(f"rs_pair_add{tag}_{i}", s, r, c_idx) for i, (s, r) in enumerate(zip(split, got))]
    return _scatter_start(f"rs_scatter_start{tag}", parts)


def _rs_end(tag, started, after, kc_idx):
    parts, lands = _scatter_wait(f"rs_scatter_wait{tag}", started, after)
    halves = [_chip_add(f"rs_chip_add{tag}_{i}", p, r, kc_idx) for i, (p, r) in enumerate(zip(parts, lands))]
    full = _pair_gather(f"rs_pair_gather{tag}", halves)
    return [f.reshape(2 * f.shape[1], f.shape[2]) for f in full]


def kernel(x, w_in, lb_logits, a_norm_w, c_sinks, w_out, ln1_g, ln1_b, w_gate, w_up, conv_w, conv_b, w_down, ln2_g, ln2_b, loss_target, m_w_in, m_lb_logits, m_a_norm_w, m_c_sinks, m_w_out, m_ln1_g, m_ln1_b, m_w_gate, m_w_up, m_conv_w, m_conv_b, m_w_down, m_ln2_g, m_ln2_b, v_w_in, v_lb_logits, v_a_norm_w, v_c_sinks, v_w_out, v_ln1_g, v_ln1_b, v_w_gate, v_w_up, v_conv_w, v_conv_b, v_w_down, v_ln2_g, v_ln2_b):
    cx, cy, cc = _coords()
    c_idx = jnp.reshape(cc, (1,)).astype(jnp.int32)
    k_me = 2 * cx + cy
    k_idx = jnp.reshape(k_me, (1,)).astype(jnp.int32)
    kc_idx = jnp.stack([k_me, cc]).astype(jnp.int32)

    def slot(nm, w, l, run_after=None):
        b = _into_slot(f"slot_{nm}{l}", w, l, k_idx, BF16, run_after)
        return b.reshape(N_CHIPS, 2, b.shape[1] // 2, b.shape[2])

    cw_slot = _into_slot("slot_cw", conv_w.reshape(1, DEPTH * CONV_WIDTH, FF_SHARD), 0, k_idx, F32)
    cw_slot = cw_slot.reshape(N_CHIPS, DEPTH, CONV_WIDTH, FF_SHARD)
    first, token = _gather_start("gather_start0", [[slot("wi", w_in, 0), cw_slot]])
    sl = [{nm: slot(nm, w, l, token) for nm, w in (("wi", w_in), ("wo", w_out), ("wg", w_gate), ("wu", w_up), ("wd", w_down))
           if (nm, l) != ("wi", 0)} for l in range(DEPTH)]
    order = [(l, nm) for l in range(DEPTH) for nm in ("wi", "wo", "wg", "wu", "wd")][1:]
    rest, token = _gather_start("gather_start1", [[sl[l][nm]] for l, nm in order])
    stage_of = {key: st for key, st in zip(order, rest)}

    def mat(b):
        return b.reshape(N_CHIPS, 2 * b.shape[2], b.shape[3])

    fwd0, token = _gather_forward("gather_fwd0", first[0], token)
    wi0, cw_all = _gather_wait("gather_wait0", fwd0, token)
    cw_full = jnp.transpose(cw_all, (1, 2, 0, 3)).reshape(DEPTH, CONV_WIDTH, D_FF)
    tables = _rope_tables()

    passing = {}

    def pass_on(l, nm, after):
        passing[(l, nm)] = _gather_forward(f"gather_fwd_{nm}{l}", stage_of[(l, nm)], after)

    def arrived(l, nm, after):
        i = order.index((l, nm))
        if i + 1 < len(order):
            pass_on(*order[i + 1], after)
            after = passing[order[i + 1]][1]
        return mat(_gather_wait(f"gather_wait_{nm}{l}", passing[(l, nm)][0], after)[0])

    h = x[0]
    h_bf = _to_bf16("x_bf16", h)
    saved = []
    weights = []
    for l in range(DEPTH):
        wi = mat(wi0) if l == 0 else arrived(l, "wi", h)
        proj = _fwd_colsharded(f"proj{l}", h_bf, wi)
        mixed, mixed_bf, raw = _hgrn_fwd(f"hgrn_fwd{l}", proj, lb_logits, a_norm_w[l], l)
        mixed, mixed_bf, lse_b = _attn_fwd(f"dilated_fwd{l}", proj, tables, None, mixed, mixed_bf, n_heads=B_HEADS, rep=1,
                                           q0=QB0, k0=KB0, v0=VB0, m0=A_HEADS, patterns=B_PATTERNS)
        if l == 0:
            pass_on(l, "wo", lse_b)
        sink_b = jnp.broadcast_to(c_sinks[l][:, None, None], (C_HEADS, 8, BLK))
        mixed, mixed_bf, lse_c = _attn_fwd(f"window_fwd{l}", proj, tables, sink_b, mixed, mixed_bf, n_heads=C_HEADS,
                                           rep=C_HEADS // C_KV_HEADS, q0=QC0, k0=KC0, v0=VC0, m0=A_HEADS + B_HEADS,
                                           patterns=C_PATTERNS)
        wo = arrived(l, "wo", lse_c)
        y1 = _fwd_rowsharded(f"wout{l}", mixed_bf, wo, OUT_SHARD)
        x1, x1_bf = _ln_fwd(f"ln1_fwd{l}", h, y1, ln1_g[l], ln1_b[l])
        wg = arrived(l, "wg", x1)
        g = _fwd_colsharded(f"gate{l}", x1_bf, wg, BF16)
        wu = arrived(l, "wu", g)
        u = _fwd_colsharded(f"up{l}", x1_bf, wu, BF16)
        hh = _conv_gate_fwd(f"conv_fwd{l}", g, u, cw_full[l], conv_b[l])
        wd = arrived(l, "wd", hh)
        y2 = _fwd_rowsharded(f"down{l}", hh, wd, FF_SHARD)
        x2, x2_bf = _ln_fwd(f"ln2_fwd{l}", x1, y2, ln2_g[l], ln2_b[l])
        weights.append(dict(wi=wi, wo=wo, wg=wg, wu=wu, wd=wd))
        saved.append((h, h_bf, proj, raw, lse_b, sink_b, lse_c, mixed, mixed_bf, y1, x1, x1_bf, g, u, hh, y2))
        h, h_bf = x2, x2_bf

    dy, loss_part = _loss_head(h, loss_target[0])

    d_res, d_path = None, dy
    small = [None] * DEPTH
    mat_grads = [None] * DEPTH
    prev_ffn = prev_mix_swap = None
    for l in reversed(range(DEPTH)):
        h_in, h_in_bf, proj, raw, lse_b, sink_b, lse_c, mixed, mixed_bf, y1, x1, x1_bf, g, u, hh, y2 = saved[l]
        wi, wo, wg, wu, wd = (weights[l][k] for k in ("wi", "wo", "wg", "wu", "wd"))
        dz2, dz2_bf, d_ln2g, d_ln2b = _ln_bwd(f"ln2_bwd{l}", x1, y2, ln2_g[l], d_res, d_path,
                                              run_after=prev_mix_swap[2][0] if prev_mix_swap else None)
        dhh = _bwd_act_rowsharded(f"down_dx{l}", dz2_bf, wd, FF_SHARD, BF16)
        prev_mix = _rs_scatter(f"{l + 1}m", prev_mix_swap, dhh, c_idx) if prev_mix_swap else None
        d_wd = _bwd_w_rowsharded(f"down_dw{l}", hh, dz2_bf, FF_SHARD)
        dg, du, d_cw, d_cb = _conv_gate_bwd(f"conv_bwd{l}", g, u, cw_full[l], conv_b[l], dhh,
                                            run_after=prev_mix[2][0] if prev_mix else None)
        dx1 = _bwd_act_colsharded(f"gateup_dx{l}", [(dg, wg), (du, wu)])
        d_wg = _bwd_w_colsharded(f"gate_dw{l}", x1_bf, dg)
        d_wu = _bwd_w_colsharded(f"up_dw{l}", x1_bf, du)
        if prev_ffn:
            g_wg, g_wu, g_wd = _rs_end(f"{l + 1}f", prev_ffn, d_wu, kc_idx)
        ffn_swap = _rs_swap(f"{l}f", [d_wg, d_wu, d_wd])
        dz1, dz1_bf, d_ln1g, d_ln1b = _ln_bwd(f"ln1_bwd{l}", h_in, y1, ln1_g[l], dz2, dx1, run_after=ffn_swap[2][0])
        dmix = _bwd_act_rowsharded(f"wout_dx{l}", dz1_bf, wo, OUT_SHARD)
        d_wo = _bwd_w_rowsharded(f"wout_dw{l}", mixed_bf, dz1_bf, OUT_SHARD)
        if prev_mix:
            g_wi, g_wo = _rs_end(f"{l + 1}m", prev_mix, d_wo, kc_idx)
            mat_grads[l + 1] = [g_wi, g_wo, g_wg, g_wu, g_wd]
        s_ffn = _rs_scatter(f"{l}f", ffn_swap, d_wo, c_idx)
        dproj, d_nw, d_lb = _hgrn_bwd(f"hgrn_bwd{l}", proj, raw, dmix, lb_logits, a_norm_w[l], l, run_after=s_ffn[2][0])
        dproj, _ = _attn_bwd(f"dilated_bwd{l}", proj, mixed, dmix, lse_b, tables, None, dproj, n_kv=B_HEADS, rep=1,
                             q0=QB0, k0=KB0, v0=VB0, m0=A_HEADS, patterns=B_PATTERNS)
        dproj, d_sink = _attn_bwd(f"window_bwd{l}", proj, mixed, dmix, lse_c, tables, sink_b, dproj, n_kv=C_KV_HEADS,
                                  rep=C_HEADS // C_KV_HEADS, q0=QC0, k0=KC0, v0=VC0, m0=A_HEADS + B_HEADS,
                                  patterns=C_PATTERNS)
        dxp = _bwd_act_colsharded(f"proj_dx{l}", [(dproj, wi)])
        d_wi = _bwd_w_colsharded(f"proj_dw{l}", h_in_bf, dproj)
        d_res, d_path = dz1, dxp
        prev_ffn, prev_mix_swap = s_ffn, _rs_swap(f"{l}m", [d_wi, d_wo])
        small[l] = (d_lb, d_nw.reshape(A_HEADS, 8, BLK)[:, 0].sum(0), d_sink[:, 0, 0], d_ln1g[0], d_ln1b[0],
                    d_cw, d_cb[0], d_ln2g[0], d_ln2b[0])
    grad_x2 = _axpy("grad_x", d_res, d_path)
    grad_x = grad_x2[None]

    g_lb = small[0][0] + small[1][0]
    per_layer = [jnp.stack([small[0][i], small[1][i]]) for i in range(1, 9)]
    small_shapes = [(DEPTH, 4 * BLK), (DEPTH, BLK), (DEPTH, C_HEADS), (DEPTH, D_MODEL), (DEPTH, D_MODEL),
                    (DEPTH, CONV_WIDTH, D_FF), (DEPTH, D_FF), (DEPTH, D_MODEL), (DEPTH, D_MODEL), (BLK,)]
    rows = _rows_for(small_shapes)
    total = _sum_small(_gather_small(_pack([g_lb] + per_layer + [loss_part[0]], rows), prev_mix_swap[2][0]))
    g_lb, g_nw, g_sink, g_ln1g, g_ln1b, g_cw_full, g_cb, g_ln2g, g_ln2b, loss_row = _unpack(total, small_shapes)
    loss = loss_row[0]
    g_cw = lax.dynamic_slice_in_dim(g_cw_full, k_me * FF_SHARD, FF_SHARD, axis=2)

    sw = [lb_logits, a_norm_w, c_sinks, ln1_g, ln1_b, conv_w, conv_b, ln2_g, ln2_b]
    sg = [g_lb, g_nw, g_sink, g_ln1g, g_ln1b, g_cw, g_cb, g_ln2g, g_ln2b]
    sm = [m_lb_logits, m_a_norm_w, m_c_sinks, m_ln1_g, m_ln1_b, m_conv_w, m_conv_b, m_ln2_g, m_ln2_b]
    sv = [v_lb_logits, v_a_norm_w, v_c_sinks, v_ln1_g, v_ln1_b, v_conv_w, v_conv_b, v_ln2_g, v_ln2_b]
    shapes = [a.shape for a in sw]
    prow = _rows_for(shapes)
    sd, snm, snv = (_unpack(b, shapes) for b in _adamw_small(_pack(sw, prow), _pack(sg, prow), _pack(sm, prow), _pack(sv, prow)))

    names = ["w_in", "w_out", "w_gate", "w_up", "w_down"]
    mw = [w_in, w_out, w_gate, w_up, w_down]
    mm = [m_w_in, m_w_out, m_w_gate, m_w_up, m_w_down]
    mv = [v_w_in, v_w_out, v_w_gate, v_w_up, v_w_down]
    res = [None] * 5
    s_mix = _rs_scatter("0m", prev_mix_swap, total, c_idx)
    ffn0 = _rs_end("0f", prev_ffn, s_mix[2][0], kc_idx)
    for i, g0 in zip((2, 3, 4), ffn0):
        res[i] = _adamw_matrix(f"adamw_{names[i]}", mw[i], [g0, mat_grads[1][i]], mm[i], mv[i])
    mix0 = _rs_end("0m", s_mix, res[4][1], kc_idx)
    for i, g0 in zip((0, 1), mix0):
        res[i] = _adamw_matrix(f"adamw_{names[i]}", mw[i], [g0, mat_grads[1][i]], mm[i], mv[i])
    mg, md, mnm, mnv = ([r[j] for r in res] for j in range(4))

    def ordered(mat, sm_):
        return [mat[0], sm_[0], sm_[1], sm_[2], mat[1], sm_[3], sm_[4], mat[2], mat[3], sm_[5], sm_[6], mat[4], sm_[7], sm_[8]]

    return (loss, grad_x, *ordered(mg, sg), *ordered(md, sd), *ordered(mnm, snm), *ordered(mnv, snv))
```

```python
import functools
import math

import jax
import jax.numpy as jnp
from jax import lax
from jax.experimental import pallas as pl
from jax.experimental.pallas import tpu as pltpu

F32 = jnp.float32
BF16 = jnp.bfloat16

D_MODEL = 2048
SEQ = 2048
DEPTH = 2
HEAD_DIM = 128
A_HEADS = 4
B_HEADS = 6
C_HEADS = 6
C_KV_HEADS = 2
A_CHUNK = 16
DILATED_PATTERNS = ((128, 1), (512, 4), (2048, 16))
C_WINDOW = 128
ROPE_THETA = 500000.0
ROPE_DIM = HEAD_DIM // 4
D_FF = 5632
CONV_WIDTH = 3
LN_EPS = 1e-5
ALPHA = (2 * DEPTH) ** 0.25
IN_WIDTH = 5632
MIX_WIDTH = 2048
ADAM_LR = 0.001
ADAM_B1 = 0.9
ADAM_B2 = 0.999
ADAM_EPS = 1e-08
ADAM_WD = 0.01
ADAM_STEP = 10

N_CHIPS = 4
N_DEV = 8
FF_SHARD = D_FF // N_CHIPS
OUT_SHARD = MIX_WIDTH // N_CHIPS
BLK = 128
N_CHUNK = SEQ // A_CHUNK
SLAB = 32

QA0, FA0, IA0, GA0 = 0, 4, 8, 12
QB0, KB0, VB0 = 16, 22, 28
QC0, KC0, VC0 = 34, 40, 42

VMEM_LIMIT_V7X = 56 * 1024 * 1024
HI = lax.Precision.HIGHEST
MESH = pl.DeviceIdType.MESH


def _cp(sem=None, vmem=VMEM_LIMIT_V7X, **kw):
    return pltpu.CompilerParams(dimension_semantics=sem, vmem_limit_bytes=vmem, **kw)


def _sigmoid(x):
    return 1.0 / (1.0 + jnp.exp(-x))


def _gate_sigmoid(x):
    return 0.5 * jnp.tanh(0.5 * x) + 0.5


def _mm(name, pairs, dims, grid, a_specs, b_specs, out_spec, out_shape, nk=1, acc_shape=None):
    n_pairs = len(pairs)

    def body(*refs):
        o_ref = refs[2 * n_pairs]
        part = None
        for p in range(n_pairs):
            a = refs[2 * p][...].astype(BF16)
            b = refs[2 * p + 1][...].astype(BF16)
            t = lax.dot_general(a, b, dims, preferred_element_type=F32)
            part = t if part is None else part + t
        if nk == 1:
            o_ref[...] = part.astype(o_ref.dtype)
        else:
            acc = refs[2 * n_pairs + 1]
            k = pl.program_id(len(grid) - 1)

            @pl.when(k == 0)
            def _():
                acc[...] = part

            @pl.when(k > 0)
            def _():
                acc[...] += part

            @pl.when(k == nk - 1)
            def _():
                o_ref[...] = acc[...].astype(o_ref.dtype)

    in_specs, args = [], []
    for (a, b), sa, sb in zip(pairs, a_specs, b_specs):
        in_specs += [sa, sb]
        args += [a, b]
    sem = ("parallel",) * (len(grid) - (1 if nk > 1 else 0)) + (("arbitrary",) if nk > 1 else ())
    return pl.pallas_call(
        body, name=name, grid=grid, in_specs=in_specs, out_specs=out_spec, out_shape=out_shape,
        scratch_shapes=[pltpu.VMEM(acc_shape, F32)] if nk > 1 else [],
        compiler_params=_cp(sem),
    )(*args)


NN = (((1,), (0,)), ((), ()))
NT = (((1,), (1,)), ((), ()))
TN = (((0,), (0,)), ((), ()))
TM = 1024


def _fwd_colsharded(name, x, w_stk, out_dtype=F32):
    return _mm(name, [(x, w_stk)], NN, (N_CHIPS, SEQ // TM),
               [pl.BlockSpec((TM, D_MODEL), lambda j, i: (i, 0))],
               [pl.BlockSpec((None, D_MODEL, FF_SHARD), lambda j, i: (j, 0, 0))],
               pl.BlockSpec((TM, FF_SHARD), lambda j, i: (i, j)),
               jax.ShapeDtypeStruct((SEQ, D_FF), out_dtype))


def _fwd_rowsharded(name, a, w_stk, shard):
    tn = 512
    rows = N_CHIPS * shard
    return _mm(name, [(a, w_stk.reshape(rows, D_MODEL))], NN, (SEQ // TM, D_MODEL // tn),
               [pl.BlockSpec((TM, rows), lambda i, j: (i, 0))],
               [pl.BlockSpec((rows, tn), lambda i, j: (0, j))],
               pl.BlockSpec((TM, tn), lambda i, j: (i, j)),
               jax.ShapeDtypeStruct((SEQ, D_MODEL), F32))


def _bwd_act_colsharded(name, pairs):
    tn = 1024
    n = len(pairs)
    return _mm(name, pairs, NT, (SEQ // TM, D_MODEL // tn, N_CHIPS),
               [pl.BlockSpec((TM, FF_SHARD), lambda i, j, k: (i, k))] * n,
               [pl.BlockSpec((None, tn, FF_SHARD), lambda i, j, k: (k, j, 0))] * n,
               pl.BlockSpec((TM, tn), lambda i, j, k: (i, j)),
               jax.ShapeDtypeStruct((SEQ, D_MODEL), F32), nk=N_CHIPS, acc_shape=(TM, tn))


def _bwd_act_rowsharded(name, dy, w_stk, shard, out_dtype=F32):
    return _mm(name, [(dy, w_stk)], NT, (N_CHIPS, SEQ // TM),
               [pl.BlockSpec((TM, D_MODEL), lambda j, i: (i, 0))],
               [pl.BlockSpec((None, shard, D_MODEL), lambda j, i: (j, 0, 0))],
               pl.BlockSpec((TM, shard), lambda j, i: (i, j)),
               jax.ShapeDtypeStruct((SEQ, N_CHIPS * shard), out_dtype))


def _bwd_w_colsharded(name, x, dy):
    tm = 1024
    return _mm(name, [(x, dy)], TN, (N_CHIPS, D_MODEL // tm),
               [pl.BlockSpec((SEQ, tm), lambda j, i: (0, i))],
               [pl.BlockSpec((SEQ, FF_SHARD), lambda j, i: (0, j))],
               pl.BlockSpec((None, tm, FF_SHARD), lambda j, i: (j, i, 0)),
               jax.ShapeDtypeStruct((N_CHIPS, D_MODEL, FF_SHARD), BF16))


def _bwd_w_rowsharded(name, a, dy, shard):
    tn = 1024
    return _mm(name, [(a, dy)], TN, (N_CHIPS, D_MODEL // tn),
               [pl.BlockSpec((SEQ, shard), lambda j, i: (0, j))],
               [pl.BlockSpec((SEQ, tn), lambda j, i: (0, i))],
               pl.BlockSpec((None, shard, tn), lambda j, i: (j, 0, i)),
               jax.ShapeDtypeStruct((N_CHIPS, shard, D_MODEL), BF16))


TR = 256


def _ln_fwd(name, x, y, g, b):
    def body(x_ref, y_ref, g_ref, b_ref, o_ref, ob_ref):
        z = ALPHA * x_ref[...] + y_ref[...]
        mu = jnp.mean(z, -1, keepdims=True)
        zc = z - mu
        var = jnp.mean(zc * zc, -1, keepdims=True)
        o = zc * lax.rsqrt(var + LN_EPS) * g_ref[...] + b_ref[...]
        o_ref[...] = o
        ob_ref[...] = o.astype(BF16)

    row = pl.BlockSpec((TR, D_MODEL), lambda i: (i, 0))
    vec = pl.BlockSpec((1, D_MODEL), lambda i: (0, 0))
    return pl.pallas_call(body, name=name, grid=(SEQ // TR,), in_specs=[row, row, vec, vec], out_specs=[row, row],
                          out_shape=[jax.ShapeDtypeStruct((SEQ, D_MODEL), F32), jax.ShapeDtypeStruct((SEQ, D_MODEL), BF16)],
                          compiler_params=_cp(("parallel",)))(x, y, g.reshape(1, -1), b.reshape(1, -1))


def _to_bf16(name, x):
    def body(x_ref, o_ref):
        o_ref[...] = x_ref[...].astype(BF16)

    row = pl.BlockSpec((TR, D_MODEL), lambda i: (i, 0))
    return pl.pallas_call(body, name=name, grid=(SEQ // TR,), in_specs=[row], out_specs=row,
                          out_shape=jax.ShapeDtypeStruct((SEQ, D_MODEL), BF16),
                          compiler_params=_cp(("parallel",)))(x)


def _ln_bwd(name, x, y, g, d_res, d_path, run_after=None):
    has_res = d_res is not None
    n_in = 4 + has_res + (run_after is not None)

    def body(*refs):
        dz_ref, dzb_ref, dg_ref, db_ref = refs[n_in:]
        if has_res:
            x_ref, y_ref, g_ref, r_ref, p_ref = refs[:5]
            dout = ALPHA * r_ref[...] + p_ref[...]
        else:
            x_ref, y_ref, g_ref, p_ref = refs[:4]
            dout = p_ref[...]
        z = ALPHA * x_ref[...] + y_ref[...]
        mu = jnp.mean(z, -1, keepdims=True)
        zc = z - mu
        rstd = lax.rsqrt(jnp.mean(zc * zc, -1, keepdims=True) + LN_EPS)
        zh = zc * rstd
        dzh = dout * g_ref[...]
        dz = rstd * (dzh - jnp.mean(dzh, -1, keepdims=True) - zh * jnp.mean(dzh * zh, -1, keepdims=True))
        dz_ref[...] = dz
        dzb_ref[...] = dz.astype(BF16)
        pg = jnp.sum(dout * zh, 0, keepdims=True)
        pb = jnp.sum(dout, 0, keepdims=True)

        @pl.when(pl.program_id(0) == 0)
        def _():
            dg_ref[...] = pg
            db_ref[...] = pb

        @pl.when(pl.program_id(0) > 0)
        def _():
            dg_ref[...] += pg
            db_ref[...] += pb

    row = pl.BlockSpec((TR, D_MODEL), lambda i: (i, 0))
    vec = pl.BlockSpec((1, D_MODEL), lambda i: (0, 0))
    args = [x, y, g.reshape(1, -1)] + ([d_res] if has_res else []) + [d_path]
    in_specs = [row, row, vec] + ([row] if has_res else []) + [row]
    if run_after is not None:
        args.append(run_after)
        in_specs.append(pl.BlockSpec(memory_space=pl.ANY))
    vshape = jax.ShapeDtypeStruct((1, D_MODEL), F32)
    return pl.pallas_call(body, name=name, grid=(SEQ // TR,), in_specs=in_specs, out_specs=[row, row, vec, vec],
                          out_shape=[jax.ShapeDtypeStruct((SEQ, D_MODEL), F32), jax.ShapeDtypeStruct((SEQ, D_MODEL), BF16),
                                     vshape, vshape],
                          compiler_params=_cp(("arbitrary",)))(*args)


def _loss_head(y, target):
    def body(y_ref, t_ref, dy_ref, l_ref):
        e = y_ref[...] - t_ref[...]
        dy_ref[...] = e * (1.0 / D_MODEL)
        part = jnp.full((8, BLK), 0.5 / D_MODEL * jnp.sum(e * e), F32)

        @pl.when(pl.program_id(0) == 0)
        def _():
            l_ref[...] = part

        @pl.when(pl.program_id(0) > 0)
        def _():
            l_ref[...] += part

    row = pl.BlockSpec((TR, D_MODEL), lambda i: (i, 0))
    return pl.pallas_call(body, name="loss_head", grid=(SEQ // TR,), in_specs=[row, row],
                          out_specs=[row, pl.BlockSpec((8, BLK), lambda i: (0, 0))],
                          out_shape=[jax.ShapeDtypeStruct((SEQ, D_MODEL), F32), jax.ShapeDtypeStruct((8, BLK), F32)],
                          compiler_params=_cp(("arbitrary",)))(y, target)


def _axpy(name, a, b):
    def body(a_ref, b_ref, o_ref):
        o_ref[...] = ALPHA * a_ref[...] + b_ref[...]

    row = pl.BlockSpec((TR, D_MODEL), lambda i: (i, 0))
    return pl.pallas_call(body, name=name, grid=(SEQ // TR,), in_specs=[row, row], out_specs=row,
                          out_shape=jax.ShapeDtypeStruct((SEQ, D_MODEL), F32),
                          compiler_params=_cp(("parallel",)))(a, b)


TC = 512


def _shift_down(x, s, rows):
    if s == 0:
        return x
    return jnp.where(rows >= s, pltpu.roll(x, s, axis=0), 0.0)


def _shift_up(x, s, rows):
    if s == 0:
        return x
    return jnp.where(rows < SEQ - s, pltpu.roll(x, SEQ - s, axis=0), 0.0)


def _conv_gate_fwd(name, g, u, cw, cb):
    def body(g_ref, u_ref, w_ref, b_ref, h_ref):
        gg = g_ref[...].astype(F32)
        rows = lax.broadcasted_iota(jnp.int32, gg.shape, 0)
        gc = b_ref[...] + w_ref[2:3, :] * gg
        gc = gc + w_ref[1:2, :] * _shift_down(gg, 1, rows)
        gc = gc + w_ref[0:1, :] * _shift_down(gg, 2, rows)
        h_ref[...] = (gc * _gate_sigmoid(gc) * u_ref[...].astype(F32)).astype(BF16)

    col = pl.BlockSpec((SEQ, TC), lambda j: (0, j))
    return pl.pallas_call(body, name=name, grid=(D_FF // TC,),
                          in_specs=[col, col, pl.BlockSpec((CONV_WIDTH, TC), lambda j: (0, j)),
                                    pl.BlockSpec((1, TC), lambda j: (0, j))],
                          out_specs=col, out_shape=jax.ShapeDtypeStruct((SEQ, D_FF), BF16),
                          compiler_params=_cp(("parallel",)))(g, u, cw, cb.reshape(1, -1))


def _conv_gate_bwd(name, g, u, cw, cb, dh, run_after=None):
    def body(g_ref, u_ref, w_ref, b_ref, dh_ref, *rest):
        dg_ref, du_ref, dw_ref, db_ref = rest[-4:]
        gg = g_ref[...].astype(F32)
        rows = lax.broadcasted_iota(jnp.int32, gg.shape, 0)
        g1 = _shift_down(gg, 1, rows)
        g2 = _shift_down(gg, 2, rows)
        gc = b_ref[...] + w_ref[2:3, :] * gg + w_ref[1:2, :] * g1 + w_ref[0:1, :] * g2
        sg = _gate_sigmoid(gc)
        act = gc * sg
        dh = dh_ref[...].astype(F32)
        du_ref[...] = (dh * act).astype(BF16)
        dgc = dh * u_ref[...].astype(F32) * (sg * (1.0 + gc * (1.0 - sg)))
        db_ref[...] = jnp.sum(dgc, 0, keepdims=True)
        dw_ref[2:3, :] = jnp.sum(dgc * gg, 0, keepdims=True)
        dw_ref[1:2, :] = jnp.sum(dgc * g1, 0, keepdims=True)
        dw_ref[0:1, :] = jnp.sum(dgc * g2, 0, keepdims=True)
        dg_ref[...] = (w_ref[2:3, :] * dgc + w_ref[1:2, :] * _shift_up(dgc, 1, rows)
                       + w_ref[0:1, :] * _shift_up(dgc, 2, rows)).astype(BF16)

    col = pl.BlockSpec((SEQ, TC), lambda j: (0, j))
    w3 = pl.BlockSpec((CONV_WIDTH, TC), lambda j: (0, j))
    w1 = pl.BlockSpec((1, TC), lambda j: (0, j))
    big = jax.ShapeDtypeStruct((SEQ, D_FF), BF16)
    extra = [] if run_after is None else [run_after]
    return pl.pallas_call(body, name=name, grid=(D_FF // TC,),
                          in_specs=[col, col, w3, w1, col] + [pl.BlockSpec(memory_space=pl.ANY)] * len(extra),
                          out_specs=[col, col, w3, w1],
                          out_shape=[big, big, jax.ShapeDtypeStruct((CONV_WIDTH, D_FF), F32),
                                     jax.ShapeDtypeStruct((1, D_FF), F32)],
                          compiler_params=_cp(("parallel",)))(g, u, cw, cb.reshape(1, -1), dh, *extra)


def _lbs_of(logits, layer):
    m = jnp.max(logits, 0, keepdims=True)
    e = jnp.exp(logits - m)
    p = e / jnp.sum(e, 0, keepdims=True)
    lb = jnp.zeros((1, BLK), F32)
    for r in range(1, layer + 1):
        lb = lb + p[r:r + 1, :]
    return lb, p


def _dlogits_of(p, dlb, layer):
    rows = lax.broadcasted_iota(jnp.int32, p.shape, 0)
    dp = jnp.where((rows >= 1) & (rows <= layer), dlb, 0.0)
    return p * (dp - jnp.sum(p * dp, 0, keepdims=True))


SROWS = SLAB * A_CHUNK
N_SLAB = N_CHUNK // SLAB


def _chunk_prefix(x, rowi):
    for s in (1, 2, 4, 8):
        x = x + jnp.where(rowi >= s, pltpu.roll(x, s, axis=0), 0.0)
    return x


def _chunk_suffix(x, rowi):
    for s in (1, 2, 4, 8):
        x = x + jnp.where(rowi < A_CHUNK - s, pltpu.roll(x, SROWS - s, axis=0), 0.0)
    return x


def _c3(x):
    return x.reshape(SLAB, A_CHUNK, BLK)


def _c2(x):
    return x.reshape(SROWS, BLK)


def _split(x):
    top = lax.bitcast_convert_type(lax.bitcast_convert_type(x, jnp.uint32) & jnp.uint32(0xFFFF0000), F32)
    return top.astype(BF16), (x - top).astype(BF16)


def _bmm(eq, a, b):
    ah, al = _split(a)
    bh, bl = _split(b)

    def mm(u, v):
        return jnp.einsum(eq, u, v, preferred_element_type=F32)

    return mm(ah, bh) + (mm(ah, bl) + mm(al, bh))


def _bmm_1pass(eq, a, b):
    return jnp.einsum(eq, a.astype(BF16), b.astype(BF16), preferred_element_type=F32)


def _slab_rows(s):
    return pl.ds(s * SROWS, SROWS)


def _hgrn_prep(q, f, lb):
    rowi = lax.broadcasted_iota(jnp.int32, (SROWS, BLK), 0) & (A_CHUNK - 1)
    sq = _gate_sigmoid(q)
    qc = q * sq
    sf = _sigmoid(f)
    fg = lb + (1.0 - lb) * sf
    kc = 1.0 - fg
    b = _chunk_prefix(jnp.log(fg), rowi)
    b3 = _c3(b)
    blast = b3[:, A_CHUNK - 1:A_CHUNK, :]
    eb = jnp.exp(b)
    ekb = _c2(jnp.exp(blast - b3))
    dec = jnp.exp(blast.reshape(SLAB, BLK))
    return rowi, sq, qc, sf, fg, kc, b, eb, ekb, dec


def _hgrn_slab_states(s, carry, v, ke, dec, dec_ref, u_ref, st_ref):
    dec_ref[pl.ds(s * SLAB, SLAB), :] = dec
    u_ref[...] = _bmm('ncv,nck->nvk', _c3(v), _c3(ke))

    def step(j, c):
        st_ref[j] = c
        return dec_ref[pl.ds(s * SLAB + j, 1), :] * c + u_ref[j]

    return lax.fori_loop(0, SLAB, step, carry)


def _hgrn_fwd(name, proj, lb_logits, nw, layer):
    def body(q_ref, f_ref, i_ref, g_ref, lg_ref, nw_ref, out_ref, outb_ref, raw_ref, dec_ref, u_ref, st_ref):
        lb, _ = _lbs_of(lg_ref[...], layer)
        ones = jnp.ones((BLK, BLK), BF16)
        carry = jnp.zeros((BLK, BLK), F32)
        for s in range(N_SLAB):
            rows = _slab_rows(s)
            v = i_ref[rows, :]
            rowi, sq, qc, sf, fg, kc, b, eb, ekb, dec = _hgrn_prep(q_ref[rows, :], f_ref[rows, :], lb)
            carry = _hgrn_slab_states(s, carry, v, kc * ekb, dec, dec_ref, u_ref, st_ref)
            o = _c2(_bmm('nck,nvk->ncv', _c3(qc * eb), st_ref[...]))
            qc3, kc3, b3, v3, row3 = _c3(qc), _c3(kc), _c3(b), _c3(v), _c3(rowi)
            for j in range(A_CHUNK):
                dj = jnp.exp(jnp.where(row3 >= j, b3 - b3[:, j:j + 1, :], -jnp.inf))
                a = jnp.dot(_c2(qc3 * dj * kc3[:, j:j + 1, :]).astype(BF16), ones, preferred_element_type=F32)
                o = o + a * _c2(jnp.broadcast_to(v3[:, j:j + 1, :], v3.shape))
            raw_ref[rows, :] = o
            r = lax.rsqrt(jnp.mean(o * o, -1, keepdims=True) + LN_EPS)
            gg = g_ref[rows, :]
            gated = o * r * nw_ref[...] * (gg * _gate_sigmoid(gg))
            out_ref[rows, :] = gated
            outb_ref[rows, :] = gated.astype(BF16)

    def colblk(c0):
        return pl.BlockSpec((SEQ, BLK), lambda h: (0, c0 + h))

    return pl.pallas_call(
        body, name=name, grid=(A_HEADS,),
        in_specs=[colblk(QA0), colblk(FA0), colblk(IA0), colblk(GA0),
                  pl.BlockSpec((DEPTH, BLK), lambda h: (0, h)), pl.BlockSpec((1, BLK), lambda h: (0, 0))],
        out_specs=[colblk(0), colblk(0), colblk(0)],
        out_shape=[jax.ShapeDtypeStruct((SEQ, MIX_WIDTH), F32), jax.ShapeDtypeStruct((SEQ, MIX_WIDTH), BF16),
                   jax.ShapeDtypeStruct((SEQ, A_HEADS * BLK), F32)],
        scratch_shapes=[pltpu.VMEM((N_CHUNK, BLK), F32), pltpu.VMEM((SLAB, BLK, BLK), F32),
                        pltpu.VMEM((SLAB, BLK, BLK), F32)],
        compiler_params=_cp(("parallel",)))(proj, proj, proj, proj, lb_logits, nw.reshape(1, -1))


def _col_block_copies(stage, sems, dst, col_blocks, first):
    return [pltpu.make_async_copy(stage.at[first + t], dst.at[:, pl.ds(pl.multiple_of(cb * BLK, BLK), BLK)],
                                  sems.at[first + t]) for t, cb in enumerate(col_blocks)]


def _start_col_blocks(stage, sems, dst, col_blocks, first=0):
    for cp in _col_block_copies(stage, sems, dst, col_blocks, first):
        cp.start()


def _wait_col_blocks(stage, sems, dst, count, first=0):
    for cp in _col_block_copies(stage, sems, dst, [0] * count, first):
        cp.wait()


def _hgrn_bwd(name, proj, raw, dmix, lb_logits, nw, layer, run_after=None):
    extra = [] if run_after is None else [run_after]

    def body(q_ref, f_ref, i_ref, g_ref, raw_ref, do_ref, lg_ref, nw_ref, *rest):
        (dproj_ref, dnw_ref, dlg_ref,
         dec_ref, u_ref, st_ref, h_ref, dbs_ref, dkc_ref, tot_ref, stage, stage_sem) = rest[-12:]
        dq_ref, df_ref, di_ref, dg_ref = (stage.at[t] for t in range(4))
        lb, p = _lbs_of(lg_ref[...], layer)
        ones = jnp.ones((BLK, BLK), BF16)
        nwv = nw_ref[...]

        carry = jnp.zeros((BLK, BLK), F32)
        for s in range(N_SLAB):
            rows = _slab_rows(s)
            rowi, sq, qc, sf, fg, kc, b, eb, ekb, dec = _hgrn_prep(q_ref[rows, :], f_ref[rows, :], lb)
            carry = _hgrn_slab_states(s, carry, i_ref[rows, :], kc * ekb, dec, dec_ref, u_ref,
                                      st_ref.at[pl.ds(s * SLAB, SLAB)])

        @pl.when(pl.program_id(0) > 0)
        def _():
            _wait_col_blocks(stage, stage_sem, dproj_ref, 4)

        carry = jnp.zeros((BLK, BLK), F32)
        dnw = jnp.zeros((1, BLK), F32)
        for s in reversed(range(N_SLAB)):
            rows = _slab_rows(s)
            q, v = q_ref[rows, :], i_ref[rows, :]
            rowi, sq, qc, sf, fg, kc, b, eb, ekb, dec = _hgrn_prep(q, f_ref[rows, :], lb)
            ke = kc * ekb
            qe = qc * eb

            o = raw_ref[rows, :]
            gg = g_ref[rows, :]
            sgg = _gate_sigmoid(gg)
            dout = do_ref[rows, :]
            r = lax.rsqrt(jnp.mean(o * o, -1, keepdims=True) + LN_EPS)
            oh = o * r
            dg_ref[rows, :] = (dout * oh * nwv * (sgg * (1.0 + gg * (1.0 - sgg)))).astype(BF16)
            dn = dout * (gg * sgg)
            dnw = dnw + jnp.sum(dn * oh, 0, keepdims=True)
            doh = dn * nwv
            do = r * (doh - oh * jnp.mean(doh * oh, -1, keepdims=True))
            do3, qe3, v3, ke3 = _c3(do), _c3(qe), _c3(v), _c3(ke)

            u_ref[...] = _bmm('ncv,nck->nvk', do3, qe3)

            def step(jj, c, s=s):
                j = SLAB - 1 - jj
                h_ref[j] = c
                return u_ref[j] + dec_ref[pl.ds(s * SLAB + j, 1), :] * c

            carry = lax.fori_loop(0, SLAB, step, carry)

            hh = h_ref[...]
            dqc = _c2(_bmm('ncv,nvk->nck', do3, st_ref[pl.ds(s * SLAB, SLAB)])) * eb
            dkc = _c2(_bmm('ncv,nvk->nck', v3, hh)) * ekb
            dv = _c2(_bmm_1pass('nck,nvk->ncv', ke3, hh))

            qc3, kc3, b3, row3 = _c3(qc), _c3(kc), _c3(b), _c3(rowi)
            datt_all = _bmm('niv,njv->nij', do3, v3)
            col = lax.broadcasted_iota(jnp.int32, datt_all.shape, 2)
            att_all = jnp.zeros_like(datt_all)
            for j in range(A_CHUNK):
                dj = jnp.exp(jnp.where(row3 >= j, b3 - b3[:, j:j + 1, :], -jnp.inf))
                kj = kc3[:, j:j + 1, :]
                att = _c3(jnp.dot(_c2(qc3 * dj * kj).astype(BF16), ones, preferred_element_type=F32))
                att_all = jnp.where(col == j, att[:, :, :A_CHUNK], att_all)
                md = dj * datt_all[:, :, j:j + 1]
                dqc = dqc + _c2(md * kj)
                dkc = dkc + _c2(jnp.where(row3 == j, jnp.sum(md * qc3, 1, keepdims=True), 0.0))
            dv = dv + _c2(_bmm_1pass('nij,niv->njv', att_all, do3))
            di_ref[rows, :] = dv.astype(BF16)
            dq_ref[rows, :] = (dqc * (sq * (1.0 + q * (1.0 - sq)))).astype(BF16)

            dbs = _chunk_suffix(qc * dqc - kc * dkc, rowi)
            dbs_ref[rows, :] = dbs
            dkc_ref[rows, :] = dkc
            tot_ref[pl.ds(s * SLAB, SLAB), :] = _c3(dbs)[:, 0:1, :].reshape(SLAB, BLK)
        dnw_ref[...] = jnp.broadcast_to(dnw, (8, BLK))

        rn = lax.broadcasted_iota(jnp.int32, (N_CHUNK, N_CHUNK), 0)
        cn = lax.broadcasted_iota(jnp.int32, (N_CHUNK, N_CHUNK), 1)
        tot_ref[...] = jnp.dot((cn > rn).astype(F32), tot_ref[...], preferred_element_type=F32, precision=HI)
        dlb = jnp.zeros((1, BLK), F32)
        for s in range(N_SLAB):
            rows = _slab_rows(s)
            sf = _sigmoid(f_ref[rows, :])
            fg = lb + (1.0 - lb) * sf
            later = tot_ref[pl.ds(s * SLAB, SLAB), :]
            dlg = _c2(_c3(dbs_ref[rows, :]) + later[:, None, :])
            dfg = dlg / fg - dkc_ref[rows, :]
            df_ref[rows, :] = (dfg * (1.0 - lb) * sf * (1.0 - sf)).astype(BF16)
            dlb = dlb + jnp.sum(dfg * (1.0 - sf), 0, keepdims=True)
        dlg_ref[...] = _dlogits_of(p, dlb, layer)
        _start_col_blocks(stage, stage_sem, dproj_ref, [c0 + pl.program_id(0) for c0 in (QA0, FA0, IA0, GA0)])

        @pl.when(pl.program_id(0) == A_HEADS - 1)
        def _():
            _wait_col_blocks(stage, stage_sem, dproj_ref, 4)

    def colblk(c0):
        return pl.BlockSpec((SEQ, BLK), lambda h: (0, c0 + h))

    return pl.pallas_call(
        body, name=name, grid=(A_HEADS,),
        in_specs=[colblk(QA0), colblk(FA0), colblk(IA0), colblk(GA0), colblk(0), colblk(0),
                  pl.BlockSpec((DEPTH, BLK), lambda h: (0, h)), pl.BlockSpec((1, BLK), lambda h: (0, 0))]
        + [pl.BlockSpec(memory_space=pl.ANY)] * len(extra),
        out_specs=[pl.BlockSpec(memory_space=pl.ANY),
                   pl.BlockSpec((8, BLK), lambda h: (h, 0)), pl.BlockSpec((DEPTH, BLK), lambda h: (0, h))],
        out_shape=[jax.ShapeDtypeStruct((SEQ, IN_WIDTH), BF16), jax.ShapeDtypeStruct((A_HEADS * 8, BLK), F32),
                   jax.ShapeDtypeStruct((DEPTH, A_HEADS * BLK), F32)],
        scratch_shapes=[pltpu.VMEM((N_CHUNK, BLK), F32), pltpu.VMEM((SLAB, BLK, BLK), F32),
                        pltpu.VMEM((N_CHUNK, BLK, BLK), F32), pltpu.VMEM((SLAB, BLK, BLK), F32),
                        pltpu.VMEM((SEQ, BLK), F32), pltpu.VMEM((SEQ, BLK), F32), pltpu.VMEM((N_CHUNK, BLK), F32),
                        pltpu.VMEM((4, SEQ, BLK), BF16), pltpu.SemaphoreType.DMA((4,))],
        compiler_params=_cp(("arbitrary",)))(proj, proj, proj, proj, raw, dmix, lb_logits, nw.reshape(1, -1), *extra)


SCALE = HEAD_DIM ** -0.5


def _rope_tables():
    half = ROPE_DIM // 2
    inv = ROPE_THETA ** (-jnp.arange(0, ROPE_DIM, 2, dtype=F32) / ROPE_DIM)
    ang = jnp.arange(SEQ, dtype=F32)[:, None] * inv[None, :]
    cos, sin = jnp.cos(ang), jnp.sin(ang)
    pad = jnp.zeros((SEQ, HEAD_DIM - ROPE_DIM), F32)
    zero = jnp.zeros((SEQ, half), F32)
    c = jnp.concatenate([cos, cos, pad + 1.0], 1)
    s_lo = jnp.concatenate([zero, sin, pad], 1)
    s_hi = jnp.concatenate([-sin, zero, pad], 1)
    return c, s_lo, s_hi


def _rope(x, c, s_lo, s_hi):
    half = ROPE_DIM // 2
    return x * c + pltpu.roll(x, half, axis=1) * s_lo + pltpu.roll(x, HEAD_DIM - half, axis=1) * s_hi


def _unrope(dy, c, s_lo, s_hi):
    half = ROPE_DIM // 2
    return dy * c + pltpu.roll(dy * s_lo, HEAD_DIM - half, axis=1) + pltpu.roll(dy * s_hi, half, axis=1)


N_BLK = SEQ // BLK


def _block_rows(dil):
    nb = N_BLK // dil
    return [pl.ds(r + n * BLK * dil, BLK, stride=dil) for r in range(dil) for n in range(nb)]


def _to_blocks(ref, dil):
    if dil == 1:
        return ref[...].reshape(N_BLK, BLK, BLK)
    return jnp.stack([ref[rows, :] for rows in _block_rows(dil)], 0)


def _from_blocks(ref, val, dil, add=False):
    if dil == 1:
        flat = val.reshape(SEQ, BLK)
        ref[...] = ref[...] + flat if add else flat
        return
    for b, rows in enumerate(_block_rows(dil)):
        ref[rows, :] = ref[rows, :] + val[b] if add else val[b]


def _prev_block(x):
    return jnp.concatenate([x[:1], x[:-1]], axis=0)


def _to_next_block(x):
    return jnp.concatenate([x[1:], jnp.zeros_like(x[:1])], axis=0)


def _band_masks(max_lag, dil):
    r = lax.broadcasted_iota(jnp.int32, (N_BLK, BLK, BLK), 1)
    c = lax.broadcasted_iota(jnp.int32, (N_BLK, BLK, BLK), 2)
    b = lax.broadcasted_iota(jnp.int32, (N_BLK, BLK, BLK), 0)
    has_prev = (b % (N_BLK // dil)) != 0
    return r >= c, has_prev & (BLK + r - c <= max_lag)


def _bdot(eq, a, b):
    return jnp.einsum(eq, a, b, preferred_element_type=F32)


def _attn_fwd(name, proj, tables, sink_b, mixed, mixed_bf, *, n_heads, rep, q0, k0, v0, m0, patterns):
    n_pat = len(patterns)
    has_sink = sink_b is not None

    def body(*refs):
        o_ref, ob_ref, l_ref, qr, kr, op, lse_ref = refs[-7:]
        q_ref, k_ref, v_ref, c_ref, sl_ref, sh_ref = refs[:6]
        if has_sink:
            sk = refs[6][0:1, 0:1]
        c, s_lo, s_hi = c_ref[...], sl_ref[...], sh_ref[...]
        qr[...] = _rope(q_ref[...], c, s_lo, s_hi)
        kr[...] = _rope(k_ref[...], c, s_lo, s_hi)
        for p, (max_lag, dil) in enumerate(patterns):
            qa = _to_blocks(qr, dil).astype(BF16)
            ka = _to_blocks(kr, dil).astype(BF16)
            va = _to_blocks(v_ref, dil).astype(BF16)
            own, before = _band_masks(max_lag, dil)
            s1 = jnp.where(own, _bdot('nqd,nkd->nqk', qa, ka) * SCALE, -jnp.inf)
            m = jnp.max(s1, -1, keepdims=True)
            with_prev = dil < N_BLK
            if with_prev:
                kp, vp = _prev_block(ka), _prev_block(va)
                s0 = jnp.where(before, _bdot('nqd,nkd->nqk', qa, kp) * SCALE, -jnp.inf)
                m = jnp.maximum(m, jnp.max(s0, -1, keepdims=True))
            if has_sink:
                m = jnp.maximum(m, sk)
            e1 = jnp.exp(s1 - m)
            den = jnp.sum(e1, -1, keepdims=True)
            o = _bdot('nqk,nkd->nqd', e1.astype(BF16), va)
            if with_prev:
                e0 = jnp.exp(s0 - m)
                den = den + jnp.sum(e0, -1, keepdims=True)
                o = o + _bdot('nqk,nkd->nqd', e0.astype(BF16), vp)
            if has_sink:
                den = den + jnp.exp(sk - m)
            _from_blocks(op.at[p], o / den, dil)
            _from_blocks(lse_ref.at[p], jnp.broadcast_to(m + jnp.log(den), (N_BLK, BLK, BLK)), dil)
        if n_pat == 1:
            acc = op[0]
            l_ref[...] = lse_ref[0]
        else:
            ls = [lse_ref[p] for p in range(n_pat)]
            m = functools.reduce(jnp.maximum, ls)
            es = [jnp.exp(l - m) for l in ls]
            tot = functools.reduce(jnp.add, es)
            acc = None
            for p in range(n_pat):
                t = (es[p] / tot) * op[p]
                acc = t if acc is None else acc + t
            l_ref[...] = m + jnp.log(tot)
        o_ref[...] = acc
        ob_ref[...] = acc.astype(BF16)

    def colblk(fn):
        return pl.BlockSpec((SEQ, BLK), fn)

    tab = pl.BlockSpec((SEQ, BLK), lambda h: (0, 0))
    in_specs = [colblk(lambda h: (0, q0 + h)), colblk(lambda h: (0, k0 + h // rep)), colblk(lambda h: (0, v0 + h // rep)),
                tab, tab, tab]
    args = [proj, proj, proj, *tables]
    if has_sink:
        in_specs.append(pl.BlockSpec((None, 8, BLK), lambda h: (h, 0, 0)))
        args.append(sink_b)
    n_in = len(args)
    in_specs += [pl.BlockSpec(memory_space=pl.ANY)] * 2
    args += [mixed, mixed_bf]
    pat = pltpu.VMEM((n_pat, SEQ, BLK), F32)
    return pl.pallas_call(
        body, name=name, grid=(n_heads,), in_specs=in_specs,
        out_specs=[colblk(lambda h: (0, m0 + h)), colblk(lambda h: (0, m0 + h)),
                   pl.BlockSpec((None, SEQ, BLK), lambda h: (h, 0, 0))],
        out_shape=[jax.ShapeDtypeStruct(mixed.shape, F32), jax.ShapeDtypeStruct(mixed.shape, BF16),
                   jax.ShapeDtypeStruct((n_heads, SEQ, BLK), F32)],
        input_output_aliases={n_in: 0, n_in + 1: 1},
        scratch_shapes=[pltpu.VMEM((SEQ, BLK), F32), pltpu.VMEM((SEQ, BLK), F32), pat, pat],
        compiler_params=_cp(("parallel",)))(*args)


def _attn_bwd(name, proj, mixed, dmix, lse, tables, sink_b, dproj, *, n_kv, rep, q0, k0, v0, m0, patterns):
    n_heads = n_kv * rep
    has_sink = sink_b is not None

    def body(*refs):
        q_ref, k_ref, v_ref, o_ref, do_ref, lse_ref, c_ref, sl_ref, sh_ref = refs[:9]
        sink_ref = refs[9] if has_sink else None
        dproj_ref, dsk_ref, qr, kr, dqa, dka, dva, dd, stage, stage_sem = refs[-10:]
        g, j = pl.program_id(0), pl.program_id(1)
        c, s_lo, s_hi = c_ref[...], sl_ref[...], sh_ref[...]
        qr[...] = _rope(q_ref[...], c, s_lo, s_hi)
        kr[...] = _rope(k_ref[...], c, s_lo, s_hi)
        dcol = jnp.sum(do_ref[...] * o_ref[...], -1, keepdims=True)
        dd[...] = jnp.broadcast_to(dcol, (SEQ, BLK))

        @pl.when(j == 0)
        def _():
            dka[...] = jnp.zeros((SEQ, BLK), F32)
            dva[...] = jnp.zeros((SEQ, BLK), F32)

        for p, (max_lag, dil) in enumerate(patterns):
            qa = _to_blocks(qr, dil).astype(BF16)
            ka = _to_blocks(kr, dil).astype(BF16)
            va = _to_blocks(v_ref, dil).astype(BF16)
            doa = _to_blocks(do_ref, dil).astype(BF16)
            lcol = _to_blocks(lse_ref, dil)[:, :, 0:1]
            dcb = _to_blocks(dd, dil)[:, :, 0:1]
            own, before = _band_masks(max_lag, dil)

            def probs_and_ds(kk, vv, valid):
                s = _bdot('nqd,nkd->nqk', qa, kk) * SCALE
                a = jnp.where(valid, jnp.exp(s - lcol), 0.0)
                ds = a * (_bdot('nqd,nkd->nqk', doa, vv) - dcb) * SCALE
                return a.astype(BF16), ds.astype(BF16)

            a1, ds1 = probs_and_ds(ka, va, own)
            dq = _bdot('nqk,nkd->nqd', ds1, ka)
            dk = _bdot('nqk,nqd->nkd', ds1, qa)
            dv = _bdot('nqk,nqd->nkd', a1, doa)
            if dil < N_BLK:
                kp, vp = _prev_block(ka), _prev_block(va)
                a0, ds0 = probs_and_ds(kp, vp, before)
                dq = dq + _bdot('nqk,nkd->nqd', ds0, kp)
                dk = dk + _to_next_block(_bdot('nqk,nqd->nkd', ds0, qa))
                dv = dv + _to_next_block(_bdot('nqk,nqd->nkd', a0, doa))
            _from_blocks(dqa, dq, dil, add=p > 0)
            _from_blocks(dka, dk, dil, add=True)
            _from_blocks(dva, dv, dil, add=True)

        if has_sink:
            sk = sink_ref[0:1, 0:1]
            ps = jnp.exp(sk - lse_ref[...][:, 0:1])
            dsk_ref[...] = jnp.full((8, BLK), -jnp.sum(ps * dcol), F32)
        else:
            dsk_ref[...] = jnp.zeros((8, BLK), F32)
        @pl.when(g * rep + j > 0)
        def _():
            _wait_col_blocks(stage, stage_sem, dproj_ref, 1)

        stage[0] = _unrope(dqa[...], c, s_lo, s_hi).astype(BF16)
        _start_col_blocks(stage, stage_sem, dproj_ref, [q0 + g * rep + j])

        @pl.when(j == rep - 1)
        def _():
            @pl.when(g > 0)
            def _():
                _wait_col_blocks(stage, stage_sem, dproj_ref, 2, first=1)

            stage[1] = _unrope(dka[...], c, s_lo, s_hi).astype(BF16)
            stage[2] = dva[...].astype(BF16)
            _start_col_blocks(stage, stage_sem, dproj_ref, [k0 + g, v0 + g], first=1)

        @pl.when((g == n_kv - 1) & (j == rep - 1))
        def _():
            _wait_col_blocks(stage, stage_sem, dproj_ref, 3)

    def colblk(fn):
        return pl.BlockSpec((SEQ, BLK), fn)

    tab = pl.BlockSpec((SEQ, BLK), lambda g, j: (0, 0))
    in_specs = [colblk(lambda g, j: (0, q0 + g * rep + j)), colblk(lambda g, j: (0, k0 + g)), colblk(lambda g, j: (0, v0 + g)),
                colblk(lambda g, j: (0, m0 + g * rep + j)), colblk(lambda g, j: (0, m0 + g * rep + j)),
                pl.BlockSpec((None, SEQ, BLK), lambda g, j: (g * rep + j, 0, 0)), tab, tab, tab]
    args = [proj, proj, proj, mixed, dmix, lse, *tables]
    if has_sink:
        in_specs.append(pl.BlockSpec((None, 8, BLK), lambda g, j: (g * rep + j, 0, 0)))
        args.append(sink_b)
    n_in = len(args)
    in_specs.append(pl.BlockSpec(memory_space=pl.ANY))
    args.append(dproj)
    acc = pltpu.VMEM((SEQ, BLK), F32)
    return pl.pallas_call(
        body, name=name, grid=(n_kv, rep), in_specs=in_specs,
        out_specs=[pl.BlockSpec(memory_space=pl.ANY), pl.BlockSpec((None, 8, BLK), lambda g, j: (g * rep + j, 0, 0))],
        out_shape=[jax.ShapeDtypeStruct(dproj.shape, BF16), jax.ShapeDtypeStruct((n_heads, 8, BLK), F32)],
        input_output_aliases={n_in: 0},
        scratch_shapes=[acc, acc, acc, acc, acc, acc, pltpu.VMEM((3, SEQ, BLK), BF16), pltpu.SemaphoreType.DMA((3,))],
        compiler_params=_cp(("arbitrary", "arbitrary")))(*args)


B_PATTERNS = tuple((w // d, d) for w, d in DILATED_PATTERNS)
C_PATTERNS = ((C_WINDOW - 1, 1),)


ANY = pl.BlockSpec(memory_space=pl.ANY)
CHIP_MASKS = ((1, 0), (0, 1), (1, 1))


def _coords():
    return lax.axis_index("x"), lax.axis_index("y"), lax.axis_index("c")


def _flip(v, m):
    return 1 - v if m else v


def _into_slot(name, w, layer, k_idx, dtype, run_after=None):
    _, rows, cols = w.shape
    tr = rows // 8 if rows % 64 == 0 else rows

    def body(k_ref, w_ref, *rest):
        rest[-1][...] = w_ref[...].astype(dtype)

    in_specs = [pl.BlockSpec((None, tr, cols), lambda i, k: (layer, i, 0))]
    args = [k_idx, w]
    if run_after is not None:
        in_specs.append(pl.BlockSpec(memory_space=pl.ANY))
        args.append(run_after)
    return pl.pallas_call(
        body, name=name,
        grid_spec=pltpu.PrefetchScalarGridSpec(
            num_scalar_prefetch=1, grid=(rows // tr,), in_specs=in_specs,
            out_specs=pl.BlockSpec((None, tr, cols), lambda i, k: (k[0], i, 0))),
        out_shape=jax.ShapeDtypeStruct((N_CHIPS, rows, cols), dtype),
        compiler_params=_cp(("parallel",)))(*args)


HBM_SPEC = pl.BlockSpec(memory_space=pltpu.HBM)
SEM_SPEC = pl.BlockSpec(memory_space=pltpu.SEMAPHORE)
TOKEN_SPEC = pl.BlockSpec(memory_space=pltpu.VMEM)
TOKEN_SHAPE = jax.ShapeDtypeStruct((8, BLK), F32)
DATAFLOW = pltpu.SideEffectType.DATAFLOW_SIDE_EFFECTING


def _hbm(a):
    return pltpu.with_memory_space_constraint(a, pltpu.HBM)


def _hbm_like(bufs):
    return [pltpu.HBM(b.shape, b.dtype) for b in bufs]


def _gather_start(name, stages):
    flat = [b for st in stages for b in st]
    n, ns = len(flat), len(stages)

    def body(*refs):
        ins = refs[:n]
        sems = refs[n:n + 2 * ns]
        token = refs[-1]
        x, y, c = _coords()
        k_me = 2 * x + y
        a = 0
        for s, st in enumerate(stages):
            for i in range(len(st)):
                mine = ins[a].at[k_me, c]
                for m, (mx, my) in enumerate(CHIP_MASKS):
                    pltpu.make_async_remote_copy(src_ref=mine, dst_ref=mine, send_sem=sems[2 * s].at[i * 3 + m],
                                                 recv_sem=sems[2 * s + 1].at[i * 3 + m],
                                                 device_id=(_flip(x, mx), _flip(y, my), c), device_id_type=MESH).start()
                a += 1
        token[...] = jnp.zeros_like(token)

    sem_shapes = []
    for st in stages:
        sem_shapes += [pltpu.SemaphoreType.DMA((3 * len(st),))] * 2
    out = pl.pallas_call(
        body, name=name, in_specs=[HBM_SPEC] * n,
        out_specs=tuple([SEM_SPEC] * (2 * ns) + [HBM_SPEC] * n + [TOKEN_SPEC]),
        out_shape=tuple(sem_shapes + _hbm_like(flat) + [TOKEN_SHAPE]),
        input_output_aliases={i: 2 * ns + i for i in range(n)},
        compiler_params=pltpu.CompilerParams(has_side_effects=DATAFLOW),
    )(*[_hbm(b) for b in flat])
    sems, bufs, token = out[:2 * ns], out[2 * ns:2 * ns + n], out[-1]
    res, a = [], 0
    for s, st in enumerate(stages):
        res.append((sems[2 * s], sems[2 * s + 1], list(bufs[a:a + len(st)])))
        a += len(st)
    return res, token


def _gather_forward(name, stage, after):
    ssem_in, rsem_in, bufs = stage
    n = len(bufs)

    def body(*refs):
        ins = refs[:n]
        s_in, r_in, _ = refs[n:n + 3]
        s_out, r_out = refs[n + 3:n + 5]
        token = refs[-1]
        x, y, c = _coords()
        for i in range(n):
            for m, (mx, my) in enumerate(CHIP_MASKS):
                kp = 2 * _flip(x, mx) + _flip(y, my)
                blk = ins[i].at[kp, c]
                got = pltpu.make_async_remote_copy(src_ref=blk, dst_ref=blk, send_sem=s_in.at[i * 3 + m],
                                                   recv_sem=r_in.at[i * 3 + m], device_id=(x, y, 1 - c), device_id_type=MESH)
                got.wait_send()
                got.wait_recv()
                pltpu.make_async_remote_copy(src_ref=blk, dst_ref=blk, send_sem=s_out.at[i * 3 + m],
                                             recv_sem=r_out.at[i * 3 + m], device_id=(x, y, 1 - c), device_id_type=MESH).start()
        token[...] = jnp.zeros_like(token)

    sem = pltpu.SemaphoreType.DMA((3 * n,))
    out = pl.pallas_call(
        body, name=name, in_specs=[HBM_SPEC] * n + [SEM_SPEC, SEM_SPEC, ANY],
        out_specs=tuple([SEM_SPEC, SEM_SPEC] + [HBM_SPEC] * n + [TOKEN_SPEC]),
        out_shape=tuple([sem, sem] + _hbm_like(bufs) + [TOKEN_SHAPE]),
        input_output_aliases={i: 2 + i for i in range(n)},
        compiler_params=pltpu.CompilerParams(has_side_effects=DATAFLOW),
    )(*bufs, ssem_in, rsem_in, after)
    return (out[0], out[1], list(out[2:2 + n])), out[-1]


def _gather_wait(name, stage, after):
    ssem, rsem, bufs = stage
    n = len(bufs)

    def body(*refs):
        ins = refs[:n]
        s_in, r_in, _ = refs[n:n + 3]
        x, y, c = _coords()
        for i in range(n):
            for m, (mx, my) in enumerate(CHIP_MASKS):
                kp = 2 * _flip(x, mx) + _flip(y, my)
                sent, got = ins[i].at[kp, c], ins[i].at[kp, 1 - c]
                cp = pltpu.make_async_remote_copy(src_ref=sent, dst_ref=got, send_sem=s_in.at[i * 3 + m],
                                                  recv_sem=r_in.at[i * 3 + m], device_id=(x, y, 1 - c), device_id_type=MESH)
                cp.wait_send()
                cp.wait_recv()

    out = pl.pallas_call(
        body, name=name, in_specs=[HBM_SPEC] * n + [SEM_SPEC, SEM_SPEC, ANY],
        out_specs=tuple([HBM_SPEC] * n), out_shape=tuple(_hbm_like(bufs)),
        input_output_aliases={i: i for i in range(n)},
        compiler_params=pltpu.CompilerParams(has_side_effects=DATAFLOW),
    )(*bufs, ssem, rsem, after)
    return list(out)


def _swap_start(name, grads):
    n = len(grads)

    def body(*refs):
        ins, lands = refs[:n], refs[n:2 * n]
        ssem, rsem = refs[2 * n:2 * n + 2]
        x, y, c = _coords()
        for a in range(n):
            for j in range(N_CHIPS):
                pltpu.make_async_remote_copy(src_ref=ins[a].at[j, 1 - c], dst_ref=lands[a].at[j],
                                             send_sem=ssem.at[a * N_CHIPS + j], recv_sem=rsem.at[a * N_CHIPS + j],
                                             device_id=(x, y, 1 - c), device_id_type=MESH).start()

    sem = pltpu.SemaphoreType.DMA((N_CHIPS * n,))
    land_shapes = [pltpu.HBM((N_CHIPS,) + g.shape[2:], g.dtype) for g in grads]
    out = pl.pallas_call(
        body, name=name, in_specs=[HBM_SPEC] * (2 * n),
        out_specs=tuple([SEM_SPEC, SEM_SPEC] + [HBM_SPEC] * (2 * n)),
        out_shape=tuple([sem, sem] + _hbm_like(grads) + land_shapes),
        input_output_aliases={i: 2 + i for i in range(2 * n)},
        compiler_params=pltpu.CompilerParams(has_side_effects=DATAFLOW),
    )(*[_hbm(g) for g in grads], *[_hbm(lax.empty((N_CHIPS,) + g.shape[2:], g.dtype)) for g in grads])
    return out[0], out[1], list(out[2:2 + n]), list(out[2 + n:])


def _swap_wait(name, started, after):
    ssem, rsem, grads, lands = started
    n = len(grads)
    after = after if isinstance(after, tuple) else (after,)

    def body(*refs):
        ins, lnd = refs[:n], refs[n:2 * n]
        s_in, r_in = refs[2 * n:2 * n + 2]
        x, y, c = _coords()
        for a in range(n):
            for j in range(N_CHIPS):
                cp = pltpu.make_async_remote_copy(src_ref=ins[a].at[j, 1 - c], dst_ref=lnd[a].at[j],
                                                  send_sem=s_in.at[a * N_CHIPS + j], recv_sem=r_in.at[a * N_CHIPS + j],
                                                  device_id=(x, y, 1 - c), device_id_type=MESH)
                cp.wait_send()
                cp.wait_recv()

    out = pl.pallas_call(
        body, name=name, in_specs=[HBM_SPEC] * (2 * n) + [SEM_SPEC, SEM_SPEC] + [ANY] * len(after),
        out_specs=tuple([HBM_SPEC] * (2 * n)), out_shape=tuple(_hbm_like(grads) + _hbm_like(lands)),
        input_output_aliases={i: i for i in range(2 * n)},
        compiler_params=pltpu.CompilerParams(has_side_effects=DATAFLOW),
    )(*grads, *lands, ssem, rsem, *after)
    return list(out[:n]), list(out[n:])


def _scatter_start(name, parts):
    n = len(parts)

    def body(*refs):
        ins, lands = refs[:n], refs[n:2 * n]
        ssem, rsem = refs[2 * n:2 * n + 2]
        x, y, c = _coords()
        k_me = 2 * x + y
        for a in range(n):
            for m, (mx, my) in enumerate(CHIP_MASKS):
                px, py = _flip(x, mx), _flip(y, my)
                pltpu.make_async_remote_copy(src_ref=ins[a].at[2 * px + py], dst_ref=lands[a].at[k_me],
                                             send_sem=ssem.at[a * 3 + m], recv_sem=rsem.at[a * 3 + m],
                                             device_id=(px, py, c), device_id_type=MESH).start()

    sem = pltpu.SemaphoreType.DMA((3 * n,))
    out = pl.pallas_call(
        body, name=name, in_specs=[HBM_SPEC] * (2 * n),
        out_specs=tuple([SEM_SPEC, SEM_SPEC] + [HBM_SPEC] * (2 * n)),
        out_shape=tuple([sem, sem] + _hbm_like(parts) + _hbm_like(parts)),
        input_output_aliases={i: 2 + i for i in range(2 * n)},
        compiler_params=pltpu.CompilerParams(has_side_effects=DATAFLOW),
    )(*[_hbm(p) for p in parts], *[_hbm(lax.empty(p.shape, p.dtype)) for p in parts])
    return out[0], out[1], list(out[2:2 + n]), list(out[2 + n:])


def _scatter_wait(name, started, after):
    ssem, rsem, parts, lands = started
    n = len(parts)

    def body(*refs):
        ins, lnd = refs[:n], refs[n:2 * n]
        s_in, r_in, _ = refs[2 * n:2 * n + 3]
        x, y, c = _coords()
        k_me = 2 * x + y
        for a in range(n):
            for m, (mx, my) in enumerate(CHIP_MASKS):
                px, py = _flip(x, mx), _flip(y, my)
                cp = pltpu.make_async_remote_copy(src_ref=ins[a].at[2 * px + py], dst_ref=lnd[a].at[k_me],
                                                  send_sem=s_in.at[a * 3 + m], recv_sem=r_in.at[a * 3 + m],
                                                  device_id=(px, py, c), device_id_type=MESH)
                cp.wait_send()
                cp.wait_recv()

    out = pl.pallas_call(
        body, name=name, in_specs=[HBM_SPEC] * (2 * n) + [SEM_SPEC, SEM_SPEC, ANY],
        out_specs=tuple([HBM_SPEC] * (2 * n)), out_shape=tuple(_hbm_like(parts) + _hbm_like(lands)),
        input_output_aliases={i: i for i in range(2 * n)},
        compiler_params=pltpu.CompilerParams(has_side_effects=DATAFLOW),
    )(*parts, *lands, ssem, rsem, after)
    return list(out[:n]), list(out[n:])


def _pair_gather_start(name, bufs):
    n = len(bufs)

    def body(*refs):
        ins = refs[:n]
        ssem, rsem = refs[n:n + 2]
        x, y, c = _coords()
        for a in range(n):
            mine = ins[a].at[c]
            pltpu.make_async_remote_copy(src_ref=mine, dst_ref=mine, send_sem=ssem.at[a], recv_sem=rsem.at[a],
                                         device_id=(x, y, 1 - c), device_id_type=MESH).start()

    sem = pltpu.SemaphoreType.DMA((n,))
    out = pl.pallas_call(
        body, name=name, in_specs=[HBM_SPEC] * n, out_specs=tuple([SEM_SPEC, SEM_SPEC] + [HBM_SPEC] * n),
        out_shape=tuple([sem, sem] + _hbm_like(bufs)),
        input_output_aliases={i: 2 + i for i in range(n)},
        compiler_params=pltpu.CompilerParams(has_side_effects=DATAFLOW),
    )(*[_hbm(b) for b in bufs])
    return out[0], out[1], list(out[2:])


def _pair_gather_wait(name, started, after):
    ssem, rsem, bufs = started
    n = len(bufs)

    def body(*refs):
        ins = refs[:n]
        s_in, r_in, _ = refs[n:n + 3]
        x, y, c = _coords()
        for a in range(n):
            cp = pltpu.make_async_remote_copy(src_ref=ins[a].at[c], dst_ref=ins[a].at[1 - c], send_sem=s_in.at[a],
                                              recv_sem=r_in.at[a], device_id=(x, y, 1 - c), device_id_type=MESH)
            cp.wait_send()
            cp.wait_recv()

    out = pl.pallas_call(
        body, name=name, in_specs=[HBM_SPEC] * n + [SEM_SPEC, SEM_SPEC, ANY],
        out_specs=tuple([HBM_SPEC] * n), out_shape=tuple(_hbm_like(bufs)),
        input_output_aliases={i: i for i in range(n)},
        compiler_params=pltpu.CompilerParams(has_side_effects=DATAFLOW),
    )(*bufs, ssem, rsem, after)
    return list(out)


DEV_MASKS = tuple((mx, my, mc) for mx in (0, 1) for my in (0, 1) for mc in (0, 1) if (mx, my, mc) != (0, 0, 0))


def _gather_small(buf, run_after):
    def body(in_ref, _, out_ref, ssem, rsem, lsem):
        x, y, c = _coords()
        me = 4 * x + 2 * y + c
        cps = [pltpu.make_async_copy(in_ref, out_ref.at[me], lsem)]
        cps[0].start()
        for t, (mx, my, mc) in enumerate(DEV_MASKS):
            cp = pltpu.make_async_remote_copy(src_ref=in_ref, dst_ref=out_ref.at[me], send_sem=ssem.at[t],
                                              recv_sem=rsem.at[t], device_id=(_flip(x, mx), _flip(y, my), _flip(c, mc)),
                                              device_id_type=MESH)
            cp.start()
            cps.append(cp)
        for cp in cps:
            cp.wait()

    return pl.pallas_call(
        body, name="gather_small", in_specs=[ANY, ANY], out_specs=ANY,
        out_shape=jax.ShapeDtypeStruct((N_DEV,) + buf.shape, buf.dtype),
        scratch_shapes=[pltpu.SemaphoreType.DMA((N_DEV - 1,)), pltpu.SemaphoreType.DMA((N_DEV - 1,)),
                        pltpu.SemaphoreType.DMA(())],
        compiler_params=pltpu.CompilerParams(has_side_effects=True),
    )(buf, run_after)


def _row_tile(rows):
    return rows // 2 if rows % 16 == 0 else rows


def _pair_add(name, grad, got, c_idx):
    _, _, r2, cols = grad.shape
    tr = _row_tile(r2)

    def body(c_ref, a_ref, b_ref, o_ref):
        o_ref[...] = (a_ref[...].astype(F32) + b_ref[...].astype(F32)).astype(BF16)

    return pl.pallas_call(
        body, name=name,
        grid_spec=pltpu.PrefetchScalarGridSpec(
            num_scalar_prefetch=1, grid=(N_CHIPS, r2 // tr),
            in_specs=[pl.BlockSpec((None, None, tr, cols), lambda j, i, c: (j, c[0], i, 0)),
                      pl.BlockSpec((None, tr, cols), lambda j, i, c: (j, i, 0))],
            out_specs=pl.BlockSpec((None, tr, cols), lambda j, i, c: (j, i, 0))),
        out_shape=jax.ShapeDtypeStruct((N_CHIPS, r2, cols), BF16),
        compiler_params=_cp(("parallel", "parallel")))(c_idx, grad, got)


def _chip_add(name, part, got, kc_idx):
    _, r2, cols = got.shape
    tr = _row_tile(r2)

    def body(k_ref, p_ref, g1_ref, g2_ref, g3_ref, o_ref):
        acc = p_ref[...].astype(F32)
        for g_ref in (g1_ref, g2_ref, g3_ref):
            acc = acc + g_ref[...].astype(F32)
        o_ref[...] = acc

    def slot(d):
        return pl.BlockSpec((None, tr, cols), lambda i, k: ((k[0] + d) % N_CHIPS, i, 0))

    return pl.pallas_call(
        body, name=name,
        grid_spec=pltpu.PrefetchScalarGridSpec(
            num_scalar_prefetch=1, grid=(r2 // tr,),
            in_specs=[slot(0), slot(1), slot(2), slot(3)],
            out_specs=pl.BlockSpec((None, tr, cols), lambda i, k: (k[1], i, 0))),
        out_shape=jax.ShapeDtypeStruct((2, r2, cols), F32),
        compiler_params=_cp(("parallel",)))(kc_idx, part, got, got, got)


def _adam_math(w, g, m, v):
    m2 = ADAM_B1 * m + (1.0 - ADAM_B1) * g
    v2 = ADAM_B2 * v + (1.0 - ADAM_B2) * (g * g)
    m_hat = m2 / (1.0 - ADAM_B1 ** ADAM_STEP)
    v_hat = v2 / (1.0 - ADAM_B2 ** ADAM_STEP)
    delta = -ADAM_LR * (m_hat / (jnp.sqrt(v_hat) + ADAM_EPS) + ADAM_WD * w)
    return delta, m2, v2


def _adamw_matrix(name, w, g_layers, m, v):
    _, rows, cols = w.shape
    tr = rows // 16 if rows % 128 == 0 else rows // 8

    def body(w_ref, g0_ref, g1_ref, m_ref, v_ref, go_ref, d_ref, mo_ref, vo_ref):
        g = jnp.where(pl.program_id(0) == 0, g0_ref[...], g1_ref[...])
        go_ref[...] = g
        d_ref[...], mo_ref[...], vo_ref[...] = _adam_math(w_ref[...], g, m_ref[...], v_ref[...])

    lay = pl.BlockSpec((None, tr, cols), lambda l, i: (l, i, 0))
    flat = pl.BlockSpec((tr, cols), lambda l, i: (i, 0))
    shp = jax.ShapeDtypeStruct(w.shape, F32)
    return pl.pallas_call(body, name=name, grid=(DEPTH, rows // tr), in_specs=[lay, flat, flat, lay, lay],
                          out_specs=[lay, lay, lay, lay], out_shape=[shp, shp, shp, shp],
                          compiler_params=_cp(("parallel", "parallel")))(w, g_layers[0], g_layers[1], m, v)


def _sum_small(gathered):
    def body(g_ref, o_ref):
        acc = g_ref[0]
        for d in range(1, N_DEV):
            acc = acc + g_ref[d]
        o_ref[...] = acc

    return pl.pallas_call(body, name="sum_small", out_shape=jax.ShapeDtypeStruct(gathered.shape[1:], F32),
                          compiler_params=_cp())(gathered)


def _adamw_small(w, g, m, v):
    def body(w_ref, g_ref, m_ref, v_ref, d_ref, mo_ref, vo_ref):
        d_ref[...], mo_ref[...], vo_ref[...] = _adam_math(w_ref[...], g_ref[...], m_ref[...], v_ref[...])

    shp = jax.ShapeDtypeStruct(w.shape, F32)
    return pl.pallas_call(body, name="adamw_small", out_shape=[shp, shp, shp], compiler_params=_cp())(w, g, m, v)


def _pack(arrays, rows):
    flat = jnp.concatenate([a.reshape(-1) for a in arrays])
    return jnp.pad(flat, (0, rows * BLK - flat.shape[0])).reshape(rows, BLK)


def _unpack(buf, shapes):
    flat = buf.reshape(-1)
    out, pos = [], 0
    for s in shapes:
        n = math.prod(s)
        out.append(flat[pos:pos + n].reshape(s))
        pos += n
    return out


def _rows_for(shapes):
    n = sum(math.prod(s) for s in shapes)
    return -(-n // (8 * BLK)) * 8


def _rs_swap(tag, grads):
    return _swap_start(f"rs_swap_start{tag}", [g.reshape(N_CHIPS, 2, g.shape[1] // 2, g.shape[2]) for g in grads])


def _rs_scatter(tag, swapping, after, c_idx):
    split, got = _swap_wait(f"rs_swap_wait{tag}", swapping, after)
    parts = [_pair_add(f"rs_pair_add{tag}_{i}", s, r, c_idx) for i, (s, r) in enumerate(zip(split, got))]
    return _scatter_start(f"rs_scatter_start{tag}", parts)


def _rs_reduce(tag, started, after, kc_idx):
    parts, lands = _scatter_wait(f"rs_scatter_wait{tag}", started, after)
    halves = [_chip_add(f"rs_chip_add{tag}_{i}", p, r, kc_idx) for i, (p, r) in enumerate(zip(parts, lands))]
    return _pair_gather_start(f"rs_pair_gather_start{tag}", halves)


def _rs_finish(tag, gathering, after):
    full = _pair_gather_wait(f"rs_pair_gather_wait{tag}", gathering, after)
    return [f.reshape(2 * f.shape[1], f.shape[2]) for f in full]


def kernel(x, w_in, lb_logits, a_norm_w, c_sinks, w_out, ln1_g, ln1_b, w_gate, w_up, conv_w, conv_b, w_down, ln2_g, ln2_b, loss_target, m_w_in, m_lb_logits, m_a_norm_w, m_c_sinks, m_w_out, m_ln1_g, m_ln1_b, m_w_gate, m_w_up, m_conv_w, m_conv_b, m_w_down, m_ln2_g, m_ln2_b, v_w_in, v_lb_logits, v_a_norm_w, v_c_sinks, v_w_out, v_ln1_g, v_ln1_b, v_w_gate, v_w_up, v_conv_w, v_conv_b, v_w_down, v_ln2_g, v_ln2_b):
    cx, cy, cc = _coords()
    c_idx = jnp.reshape(cc, (1,)).astype(jnp.int32)
    k_me = 2 * cx + cy
    k_idx = jnp.reshape(k_me, (1,)).astype(jnp.int32)
    kc_idx = jnp.stack([k_me, cc]).astype(jnp.int32)

    def slot(nm, w, l, run_after=None):
        b = _into_slot(f"slot_{nm}{l}", w, l, k_idx, BF16, run_after)
        return b.reshape(N_CHIPS, 2, b.shape[1] // 2, b.shape[2])

    cw_slot = _into_slot("slot_cw", conv_w.reshape(1, DEPTH * CONV_WIDTH, FF_SHARD), 0, k_idx, F32)
    cw_slot = cw_slot.reshape(N_CHIPS, DEPTH, CONV_WIDTH, FF_SHARD)
    first, token = _gather_start("gather_start0", [[slot("wi", w_in, 0), cw_slot]])
    sl = [{nm: slot(nm, w, l, token) for nm, w in (("wi", w_in), ("wo", w_out), ("wg", w_gate), ("wu", w_up), ("wd", w_down))
           if (nm, l) != ("wi", 0)} for l in range(DEPTH)]
    order = [(l, nm) for l in range(DEPTH) for nm in ("wi", "wo", "wg", "wu", "wd")][1:]
    rest, token = _gather_start("gather_start1", [[sl[l][nm]] for l, nm in order])
    stage_of = {key: st for key, st in zip(order, rest)}

    def mat(b):
        return b.reshape(N_CHIPS, 2 * b.shape[2], b.shape[3])

    fwd0, token = _gather_forward("gather_fwd0", first[0], token)
    wi0, cw_all = _gather_wait("gather_wait0", fwd0, token)
    cw_full = jnp.transpose(cw_all, (1, 2, 0, 3)).reshape(DEPTH, CONV_WIDTH, D_FF)
    tables = _rope_tables()

    passing = {}

    def pass_on(l, nm, after):
        passing[(l, nm)] = _gather_forward(f"gather_fwd_{nm}{l}", stage_of[(l, nm)], after)

    def arrived(l, nm, after):
        i = order.index((l, nm))
        if i + 1 < len(order):
            pass_on(*order[i + 1], after)
            after = passing[order[i + 1]][1]
        return mat(_gather_wait(f"gather_wait_{nm}{l}", passing[(l, nm)][0], after)[0])

    h = x[0]
    h_bf = _to_bf16("x_bf16", h)
    saved = []
    weights = []
    for l in range(DEPTH):
        wi = mat(wi0) if l == 0 else arrived(l, "wi", h)
        proj = _fwd_colsharded(f"proj{l}", h_bf, wi)
        mixed, mixed_bf, raw = _hgrn_fwd(f"hgrn_fwd{l}", proj, lb_logits, a_norm_w[l], l)
        mixed, mixed_bf, lse_b = _attn_fwd(f"dilated_fwd{l}", proj, tables, None, mixed, mixed_bf, n_heads=B_HEADS, rep=1,
                                           q0=QB0, k0=KB0, v0=VB0, m0=A_HEADS, patterns=B_PATTERNS)
        if l == 0:
            pass_on(l, "wo", lse_b)
        sink_b = jnp.broadcast_to(c_sinks[l][:, None, None], (C_HEADS, 8, BLK))
        mixed, mixed_bf, lse_c = _attn_fwd(f"window_fwd{l}", proj, tables, sink_b, mixed, mixed_bf, n_heads=C_HEADS,
                                           rep=C_HEADS // C_KV_HEADS, q0=QC0, k0=KC0, v0=VC0, m0=A_HEADS + B_HEADS,
                                           patterns=C_PATTERNS)
        wo = arrived(l, "wo", lse_c)
        y1 = _fwd_rowsharded(f"wout{l}", mixed_bf, wo, OUT_SHARD)
        x1, x1_bf = _ln_fwd(f"ln1_fwd{l}", h, y1, ln1_g[l], ln1_b[l])
        wg = arrived(l, "wg", x1)
        g = _fwd_colsharded(f"gate{l}", x1_bf, wg, BF16)
        wu = arrived(l, "wu", g)
        u = _fwd_colsharded(f"up{l}", x1_bf, wu, BF16)
        hh = _conv_gate_fwd(f"conv_fwd{l}", g, u, cw_full[l], conv_b[l])
        wd = arrived(l, "wd", hh)
        y2 = _fwd_rowsharded(f"down{l}", hh, wd, FF_SHARD)
        x2, x2_bf = _ln_fwd(f"ln2_fwd{l}", x1, y2, ln2_g[l], ln2_b[l])
        weights.append(dict(wi=wi, wo=wo, wg=wg, wu=wu, wd=wd))
        saved.append((h, h_bf, proj, raw, lse_b, sink_b, lse_c, mixed, mixed_bf, y1, x1, x1_bf, g, u, hh, y2))
        h, h_bf = x2, x2_bf

    dy, loss_part = _loss_head(h, loss_target[0])

    d_res, d_path = None, dy
    small = [None] * DEPTH
    mat_grads = [None] * DEPTH
    late = {}
    prev_ffn = prev_mix_swap = None
    for l in reversed(range(DEPTH)):
        h_in, h_in_bf, proj, raw, lse_b, sink_b, lse_c, mixed, mixed_bf, y1, x1, x1_bf, g, u, hh, y2 = saved[l]
        wi, wo, wg, wu, wd = (weights[l][k] for k in ("wi", "wo", "wg", "wu", "wd"))
        dz2, dz2_bf, d_ln2g, d_ln2b = _ln_bwd(f"ln2_bwd{l}", x1, y2, ln2_g[l], d_res, d_path,
                                              run_after=prev_mix_swap[2][0] if prev_mix_swap else None)
        dhh = _bwd_act_rowsharded(f"down_dx{l}", dz2_bf, wd, FF_SHARD, BF16)
        prev_mix = _rs_scatter(f"{l + 1}m", prev_mix_swap, dhh, c_idx) if prev_mix_swap else None
        d_wd = _bwd_w_rowsharded(f"down_dw{l}", hh, dz2_bf, FF_SHARD)
        dg, du, d_cw, d_cb = _conv_gate_bwd(f"conv_bwd{l}", g, u, cw_full[l], conv_b[l], dhh,
                                            run_after=prev_mix[2][0] if prev_mix else None)
        dx1 = _bwd_act_colsharded(f"gateup_dx{l}", [(dg, wg), (du, wu)])
        d_wg = _bwd_w_colsharded(f"gate_dw{l}", x1_bf, dg)
        d_wu = _bwd_w_colsharded(f"up_dw{l}", x1_bf, du)
        pins = ()
        if prev_ffn:
            late[l + 1] = [_rs_reduce(f"{l + 1}f", prev_ffn, d_wu, kc_idx)]
        ffn_swap = _rs_swap(f"{l}f", [d_wg, d_wu, d_wd])
        dz1, dz1_bf, d_ln1g, d_ln1b = _ln_bwd(f"ln1_bwd{l}", h_in, y1, ln1_g[l], dz2, dx1, run_after=ffn_swap[2][0])
        dmix = _bwd_act_rowsharded(f"wout_dx{l}", dz1_bf, wo, OUT_SHARD)
        d_wo = _bwd_w_rowsharded(f"wout_dw{l}", mixed_bf, dz1_bf, OUT_SHARD)
        if prev_mix:
            late[l + 1].append(_rs_reduce(f"{l + 1}m", prev_mix, d_wo, kc_idx))
            pins = tuple(g[2][0] for g in late[l + 1])
        s_ffn = _rs_scatter(f"{l}f", ffn_swap, (d_wo,) + pins, c_idx)
        dproj, d_nw, d_lb = _hgrn_bwd(f"hgrn_bwd{l}", proj, raw, dmix, lb_logits, a_norm_w[l], l, run_after=s_ffn[2][0])
        dproj, _ = _attn_bwd(f"dilated_bwd{l}", proj, mixed, dmix, lse_b, tables, None, dproj, n_kv=B_HEADS, rep=1,
                             q0=QB0, k0=KB0, v0=VB0, m0=A_HEADS, patterns=B_PATTERNS)
        dproj, d_sink = _attn_bwd(f"window_bwd{l}", proj, mixed, dmix, lse_c, tables, sink_b, dproj, n_kv=C_KV_HEADS,
                                  rep=C_HEADS // C_KV_HEADS, q0=QC0, k0=KC0, v0=VC0, m0=A_HEADS + B_HEADS,
                                  patterns=C_PATTERNS)
        dxp = _bwd_act_colsharded(f"proj_dx{l}", [(dproj, wi)])
        d_wi = _bwd_w_colsharded(f"proj_dw{l}", h_in_bf, dproj)
        d_res, d_path = dz1, dxp
        prev_ffn, prev_mix_swap = s_ffn, _rs_swap(f"{l}m", [d_wi, d_wo])
        small[l] = (d_lb, d_nw.reshape(A_HEADS, 8, BLK)[:, 0].sum(0), d_sink[:, 0, 0], d_ln1g[0], d_ln1b[0],
                    d_cw, d_cb[0], d_ln2g[0], d_ln2b[0])
    grad_x2 = _axpy("grad_x", d_res, d_path)
    grad_x = grad_x2[None]

    g_lb = small[0][0] + small[1][0]
    per_layer = [jnp.stack([small[0][i], small[1][i]]) for i in range(1, 9)]
    small_shapes = [(DEPTH, 4 * BLK), (DEPTH, BLK), (DEPTH, C_HEADS), (DEPTH, D_MODEL), (DEPTH, D_MODEL),
                    (DEPTH, CONV_WIDTH, D_FF), (DEPTH, D_FF), (DEPTH, D_MODEL), (DEPTH, D_MODEL), (BLK,)]
    rows = _rows_for(small_shapes)
    total = _sum_small(_gather_small(_pack([g_lb] + per_layer + [loss_part[0]], rows), prev_mix_swap[2][0]))
    g_lb, g_nw, g_sink, g_ln1g, g_ln1b, g_cw_full, g_cb, g_ln2g, g_ln2b, loss_row = _unpack(total, small_shapes)
    loss = loss_row[0]
    g_cw = lax.dynamic_slice_in_dim(g_cw_full, k_me * FF_SHARD, FF_SHARD, axis=2)

    sw = [lb_logits, a_norm_w, c_sinks, ln1_g, ln1_b, conv_w, conv_b, ln2_g, ln2_b]
    sg = [g_lb, g_nw, g_sink, g_ln1g, g_ln1b, g_cw, g_cb, g_ln2g, g_ln2b]
    sm = [m_lb_logits, m_a_norm_w, m_c_sinks, m_ln1_g, m_ln1_b, m_conv_w, m_conv_b, m_ln2_g, m_ln2_b]
    sv = [v_lb_logits, v_a_norm_w, v_c_sinks, v_ln1_g, v_ln1_b, v_conv_w, v_conv_b, v_ln2_g, v_ln2_b]
    shapes = [a.shape for a in sw]
    prow = _rows_for(shapes)
    sd, snm, snv = (_unpack(b, shapes) for b in _adamw_small(_pack(sw, prow), _pack(sg, prow), _pack(sm, prow), _pack(sv, prow)))

    names = ["w_in", "w_out", "w_gate", "w_up", "w_down"]
    mw = [w_in, w_out, w_gate, w_up, w_down]
    mm = [m_w_in, m_w_out, m_w_gate, m_w_up, m_w_down]
    mv = [v_w_in, v_w_out, v_w_gate, v_w_up, v_w_down]
    res = [None] * 5
    s_mix = _rs_scatter("0m", prev_mix_swap, total, c_idx)
    for l, (g_ffn, g_mix) in late.items():
        g_wg, g_wu, g_wd = _rs_finish(f"{l}f", g_ffn, s_mix[2][0])
        g_wi, g_wo = _rs_finish(f"{l}m", g_mix, s_mix[2][0])
        mat_grads[l] = [g_wi, g_wo, g_wg, g_wu, g_wd]
    ffn0 = _rs_finish("0f", _rs_reduce("0f", prev_ffn, s_mix[2][0], kc_idx), s_mix[2][0])
    for i, g0 in zip((2, 3, 4), ffn0):
        res[i] = _adamw_matrix(f"adamw_{names[i]}", mw[i], [g0, mat_grads[1][i]], mm[i], mv[i])
    mix0 = _rs_finish("0m", _rs_reduce("0m", s_mix, res[4][1], kc_idx), res[4][1])
    for i, g0 in zip((0, 1), mix0):
        res[i] = _adamw_matrix(f"adamw_{names[i]}", mw[i], [g0, mat_grads[1][i]], mm[i], mv[i])
    mg, md, mnm, mnv = ([r[j] for r in res] for j in range(4))

    def ordered(mat, sm_):
        return [mat[0], sm_[0], sm_[1], sm_[2], mat[1], sm_[3], sm_[4], mat[2], mat[3], sm_[5], sm_[6], mat[4], sm_[7], sm_[8]]

    return (loss, grad_x, *ordered(mg, sg), *ordered(md, sd), *ordered(mnm, snm), *ordered(mnv, snv))
```

```python
import functools
import math

import jax
import jax.numpy as jnp
from jax import lax
from jax.experimental import pallas as pl
from jax.experimental.pallas import tpu as pltpu

F32 = jnp.float32
BF16 = jnp.bfloat16

D_MODEL = 2048
SEQ = 2048
DEPTH = 2
HEAD_DIM = 128
A_HEADS = 4
B_HEADS = 6
C_HEADS = 6
C_KV_HEADS = 2
A_CHUNK = 16
DILATED_PATTERNS = ((128, 1), (512, 4), (2048, 16))
C_WINDOW = 128
ROPE_THETA = 500000.0
ROPE_DIM = HEAD_DIM // 4
D_FF = 5632
CONV_WIDTH = 3
LN_EPS = 1e-5
ALPHA = (2 * DEPTH) ** 0.25
IN_WIDTH = 5632
MIX_WIDTH = 2048
ADAM_LR = 0.001
ADAM_B1 = 0.9
ADAM_B2 = 0.999
ADAM_EPS = 1e-08
ADAM_WD = 0.01
ADAM_STEP = 10

N_CHIPS = 4
N_DEV = 8
FF_SHARD = D_FF // N_CHIPS
OUT_SHARD = MIX_WIDTH // N_CHIPS
BLK = 128
N_CHUNK = SEQ // A_CHUNK
SLAB = 32

QA0, FA0, IA0, GA0 = 0, 4, 8, 12
QB0, KB0, VB0 = 16, 22, 28
QC0, KC0, VC0 = 34, 40, 42

VMEM_LIMIT_V7X = 56 * 1024 * 1024
HI = lax.Precision.HIGHEST
MESH = pl.DeviceIdType.MESH


def _cp(sem=None, vmem=VMEM_LIMIT_V7X, **kw):
    return pltpu.CompilerParams(dimension_semantics=sem, vmem_limit_bytes=vmem, **kw)


def _sigmoid(x):
    return 1.0 / (1.0 + jnp.exp(-x))


def _gate_sigmoid(x):
    return 0.5 * jnp.tanh(0.5 * x) + 0.5


def _mm(name, pairs, dims, grid, a_specs, b_specs, out_spec, out_shape, nk=1, acc_shape=None):
    n_pairs = len(pairs)

    def body(*refs):
        o_ref = refs[2 * n_pairs]
        part = None
        for p in range(n_pairs):
            a = refs[2 * p][...].astype(BF16)
            b = refs[2 * p + 1][...].astype(BF16)
            t = lax.dot_general(a, b, dims, preferred_element_type=F32)
            part = t if part is None else part + t
        if nk == 1:
            o_ref[...] = part.astype(o_ref.dtype)
        else:
            acc = refs[2 * n_pairs + 1]
            k = pl.program_id(len(grid) - 1)

            @pl.when(k == 0)
            def _():
                acc[...] = part

            @pl.when(k > 0)
            def _():
                acc[...] += part

            @pl.when(k == nk - 1)
            def _():
                o_ref[...] = acc[...].astype(o_ref.dtype)

    in_specs, args = [], []
    for (a, b), sa, sb in zip(pairs, a_specs, b_specs):
        in_specs += [sa, sb]
        args += [a, b]
    sem = ("parallel",) * (len(grid) - (1 if nk > 1 else 0)) + (("arbitrary",) if nk > 1 else ())
    return pl.pallas_call(
        body, name=name, grid=grid, in_specs=in_specs, out_specs=out_spec, out_shape=out_shape,
        scratch_shapes=[pltpu.VMEM(acc_shape, F32)] if nk > 1 else [],
        compiler_params=_cp(sem),
    )(*args)


NN = (((1,), (0,)), ((), ()))
NT = (((1,), (1,)), ((), ()))
TN = (((0,), (0,)), ((), ()))
TM = 1024


def _fwd_colsharded(name, x, w_stk, out_dtype=F32):
    return _mm(name, [(x, w_stk)], NN, (N_CHIPS, SEQ // TM),
               [pl.BlockSpec((TM, D_MODEL), lambda j, i: (i, 0))],
               [pl.BlockSpec((None, D_MODEL, FF_SHARD), lambda j, i: (j, 0, 0))],
               pl.BlockSpec((TM, FF_SHARD), lambda j, i: (i, j)),
               jax.ShapeDtypeStruct((SEQ, D_FF), out_dtype))


def _fwd_rowsharded(name, a, w_stk, shard):
    tn = 512
    rows = N_CHIPS * shard
    return _mm(name, [(a, w_stk.reshape(rows, D_MODEL))], NN, (SEQ // TM, D_MODEL // tn),
               [pl.BlockSpec((TM, rows), lambda i, j: (i, 0))],
               [pl.BlockSpec((rows, tn), lambda i, j: (0, j))],
               pl.BlockSpec((TM, tn), lambda i, j: (i, j)),
               jax.ShapeDtypeStruct((SEQ, D_MODEL), F32))


def _bwd_act_colsharded(name, pairs):
    tn = 1024
    n = len(pairs)
    return _mm(name, pairs, NT, (SEQ // TM, D_MODEL // tn, N_CHIPS),
               [pl.BlockSpec((TM, FF_SHARD), lambda i, j, k: (i, k))] * n,
               [pl.BlockSpec((None, tn, FF_SHARD), lambda i, j, k: (k, j, 0))] * n,
               pl.BlockSpec((TM, tn), lambda i, j, k: (i, j)),
               jax.ShapeDtypeStruct((SEQ, D_MODEL), F32), nk=N_CHIPS, acc_shape=(TM, tn))


def _bwd_act_colsharded_full(name, dy, w_stk):
    tn = 512

    def body(a_ref, w_ref, o_ref):
        acc = None
        for k in range(N_CHIPS):
            t = lax.dot_general(a_ref[:, k * FF_SHARD:(k + 1) * FF_SHARD], w_ref[k], NT, preferred_element_type=F32)
            acc = t if acc is None else acc + t
        o_ref[...] = acc

    return pl.pallas_call(
        body, name=name, grid=(SEQ // TM, D_MODEL // tn),
        in_specs=[pl.BlockSpec((TM, D_FF), lambda i, j: (i, 0)),
                  pl.BlockSpec((N_CHIPS, tn, FF_SHARD), lambda i, j: (0, j, 0))],
        out_specs=pl.BlockSpec((TM, tn), lambda i, j: (i, j)),
        out_shape=jax.ShapeDtypeStruct((SEQ, D_MODEL), F32),
        compiler_params=_cp(("parallel", "parallel")))(dy, w_stk)


def _bwd_act_rowsharded(name, dy, w_stk, shard, out_dtype=F32):
    return _mm(name, [(dy, w_stk)], NT, (N_CHIPS, SEQ // TM),
               [pl.BlockSpec((TM, D_MODEL), lambda j, i: (i, 0))],
               [pl.BlockSpec((None, shard, D_MODEL), lambda j, i: (j, 0, 0))],
               pl.BlockSpec((TM, shard), lambda j, i: (i, j)),
               jax.ShapeDtypeStruct((SEQ, N_CHIPS * shard), out_dtype))


def _bwd_w_colsharded(name, x, dy):
    tm = 1024
    return _mm(name, [(x, dy)], TN, (N_CHIPS, D_MODEL // tm),
               [pl.BlockSpec((SEQ, tm), lambda j, i: (0, i))],
               [pl.BlockSpec((SEQ, FF_SHARD), lambda j, i: (0, j))],
               pl.BlockSpec((None, tm, FF_SHARD), lambda j, i: (j, i, 0)),
               jax.ShapeDtypeStruct((N_CHIPS, D_MODEL, FF_SHARD), BF16))


def _bwd_w_rowsharded(name, a, dy, shard):
    tn = 1024
    return _mm(name, [(a, dy)], TN, (N_CHIPS, D_MODEL // tn),
               [pl.BlockSpec((SEQ, shard), lambda j, i: (0, j))],
               [pl.BlockSpec((SEQ, tn), lambda j, i: (0, i))],
               pl.BlockSpec((None, shard, tn), lambda j, i: (j, 0, i)),
               jax.ShapeDtypeStruct((N_CHIPS, shard, D_MODEL), BF16))


TR = 256


def _ln_fwd(name, x, y, g, b):
    def body(x_ref, y_ref, g_ref, b_ref, o_ref, ob_ref):
        z = ALPHA * x_ref[...] + y_ref[...]
        mu = jnp.mean(z, -1, keepdims=True)
        zc = z - mu
        var = jnp.mean(zc * zc, -1, keepdims=True)
        o = zc * lax.rsqrt(var + LN_EPS) * g_ref[...] + b_ref[...]
        o_ref[...] = o
        ob_ref[...] = o.astype(BF16)

    row = pl.BlockSpec((TR, D_MODEL), lambda i: (i, 0))
    vec = pl.BlockSpec((1, D_MODEL), lambda i: (0, 0))
    return pl.pallas_call(body, name=name, grid=(SEQ // TR,), in_specs=[row, row, vec, vec], out_specs=[row, row],
                          out_shape=[jax.ShapeDtypeStruct((SEQ, D_MODEL), F32), jax.ShapeDtypeStruct((SEQ, D_MODEL), BF16)],
                          compiler_params=_cp(("parallel",)))(x, y, g.reshape(1, -1), b.reshape(1, -1))


def _to_bf16(name, x):
    def body(x_ref, o_ref):
        o_ref[...] = x_ref[...].astype(BF16)

    row = pl.BlockSpec((TR, D_MODEL), lambda i: (i, 0))
    return pl.pallas_call(body, name=name, grid=(SEQ // TR,), in_specs=[row], out_specs=row,
                          out_shape=jax.ShapeDtypeStruct((SEQ, D_MODEL), BF16),
                          compiler_params=_cp(("parallel",)))(x)


def _ln_bwd(name, x, y, g, d_res, d_path, run_after=None):
    has_res = d_res is not None
    n_in = 4 + has_res + (run_after is not None)

    def body(*refs):
        dz_ref, dzb_ref, dg_ref, db_ref = refs[n_in:]
        if has_res:
            x_ref, y_ref, g_ref, r_ref, p_ref = refs[:5]
            dout = ALPHA * r_ref[...] + p_ref[...]
        else:
            x_ref, y_ref, g_ref, p_ref = refs[:4]
            dout = p_ref[...]
        z = ALPHA * x_ref[...] + y_ref[...]
        mu = jnp.mean(z, -1, keepdims=True)
        zc = z - mu
        rstd = lax.rsqrt(jnp.mean(zc * zc, -1, keepdims=True) + LN_EPS)
        zh = zc * rstd
        dzh = dout * g_ref[...]
        dz = rstd * (dzh - jnp.mean(dzh, -1, keepdims=True) - zh * jnp.mean(dzh * zh, -1, keepdims=True))
        dz_ref[...] = dz
        dzb_ref[...] = dz.astype(BF16)
        pg = jnp.sum(dout * zh, 0, keepdims=True)
        pb = jnp.sum(dout, 0, keepdims=True)

        @pl.when(pl.program_id(0) == 0)
        def _():
            dg_ref[...] = pg
            db_ref[...] = pb

        @pl.when(pl.program_id(0) > 0)
        def _():
            dg_ref[...] += pg
            db_ref[...] += pb

    row = pl.BlockSpec((TR, D_MODEL), lambda i: (i, 0))
    vec = pl.BlockSpec((1, D_MODEL), lambda i: (0, 0))
    args = [x, y, g.reshape(1, -1)] + ([d_res] if has_res else []) + [d_path]
    in_specs = [row, row, vec] + ([row] if has_res else []) + [row]
    if run_after is not None:
        args.append(run_after)
        in_specs.append(pl.BlockSpec(memory_space=pl.ANY))
    vshape = jax.ShapeDtypeStruct((1, D_MODEL), F32)
    return pl.pallas_call(body, name=name, grid=(SEQ // TR,), in_specs=in_specs, out_specs=[row, row, vec, vec],
                          out_shape=[jax.ShapeDtypeStruct((SEQ, D_MODEL), F32), jax.ShapeDtypeStruct((SEQ, D_MODEL), BF16),
                                     vshape, vshape],
                          compiler_params=_cp(("arbitrary",)))(*args)


def _loss_head(y, target):
    def body(y_ref, t_ref, dy_ref, l_ref):
        e = y_ref[...] - t_ref[...]
        dy_ref[...] = e * (1.0 / D_MODEL)
        part = jnp.full((8, BLK), 0.5 / D_MODEL * jnp.sum(e * e), F32)

        @pl.when(pl.program_id(0) == 0)
        def _():
            l_ref[...] = part

        @pl.when(pl.program_id(0) > 0)
        def _():
            l_ref[...] += part

    row = pl.BlockSpec((TR, D_MODEL), lambda i: (i, 0))
    return pl.pallas_call(body, name="loss_head", grid=(SEQ // TR,), in_specs=[row, row],
                          out_specs=[row, pl.BlockSpec((8, BLK), lambda i: (0, 0))],
                          out_shape=[jax.ShapeDtypeStruct((SEQ, D_MODEL), F32), jax.ShapeDtypeStruct((8, BLK), F32)],
                          compiler_params=_cp(("arbitrary",)))(y, target)


def _axpy(name, a, b):
    def body(a_ref, b_ref, o_ref):
        o_ref[...] = ALPHA * a_ref[...] + b_ref[...]

    row = pl.BlockSpec((TR, D_MODEL), lambda i: (i, 0))
    return pl.pallas_call(body, name=name, grid=(SEQ // TR,), in_specs=[row, row], out_specs=row,
                          out_shape=jax.ShapeDtypeStruct((SEQ, D_MODEL), F32),
                          compiler_params=_cp(("parallel",)))(a, b)


TC = 512


def _shift_down(x, s, rows):
    if s == 0:
        return x
    return jnp.where(rows >= s, pltpu.roll(x, s, axis=0), 0.0)


def _shift_up(x, s, rows):
    if s == 0:
        return x
    return jnp.where(rows < SEQ - s, pltpu.roll(x, SEQ - s, axis=0), 0.0)


def _conv_gate_fwd(name, g, u, cw, cb):
    def body(g_ref, u_ref, w_ref, b_ref, h_ref):
        gg = g_ref[...].astype(F32)
        rows = lax.broadcasted_iota(jnp.int32, gg.shape, 0)
        gc = b_ref[...] + w_ref[2:3, :] * gg
        gc = gc + w_ref[1:2, :] * _shift_down(gg, 1, rows)
        gc = gc + w_ref[0:1, :] * _shift_down(gg, 2, rows)
        h_ref[...] = (gc * _gate_sigmoid(gc) * u_ref[...].astype(F32)).astype(BF16)

    col = pl.BlockSpec((SEQ, TC), lambda j: (0, j))
    return pl.pallas_call(body, name=name, grid=(D_FF // TC,),
                          in_specs=[col, col, pl.BlockSpec((CONV_WIDTH, TC), lambda j: (0, j)),
                                    pl.BlockSpec((1, TC), lambda j: (0, j))],
                          out_specs=col, out_shape=jax.ShapeDtypeStruct((SEQ, D_FF), BF16),
                          compiler_params=_cp(("parallel",)))(g, u, cw, cb.reshape(1, -1))


def _conv_gate_bwd(name, g, u, cw, cb, dh, run_after=None):
    def body(g_ref, u_ref, w_ref, b_ref, dh_ref, *rest):
        dg_ref, du_ref, dw_ref, db_ref = rest[-4:]
        gg = g_ref[...].astype(F32)
        rows = lax.broadcasted_iota(jnp.int32, gg.shape, 0)
        g1 = _shift_down(gg, 1, rows)
        g2 = _shift_down(gg, 2, rows)
        gc = b_ref[...] + w_ref[2:3, :] * gg + w_ref[1:2, :] * g1 + w_ref[0:1, :] * g2
        sg = _gate_sigmoid(gc)
        act = gc * sg
        dh = dh_ref[...].astype(F32)
        du_ref[...] = (dh * act).astype(BF16)
        dgc = dh * u_ref[...].astype(F32) * (sg * (1.0 + gc * (1.0 - sg)))
        db_ref[...] = jnp.sum(dgc, 0, keepdims=True)
        dw_ref[2:3, :] = jnp.sum(dgc * gg, 0, keepdims=True)
        dw_ref[1:2, :] = jnp.sum(dgc * g1, 0, keepdims=True)
        dw_ref[0:1, :] = jnp.sum(dgc * g2, 0, keepdims=True)
        dg_ref[...] = (w_ref[2:3, :] * dgc + w_ref[1:2, :] * _shift_up(dgc, 1, rows)
                       + w_ref[0:1, :] * _shift_up(dgc, 2, rows)).astype(BF16)

    col = pl.BlockSpec((SEQ, TC), lambda j: (0, j))
    w3 = pl.BlockSpec((CONV_WIDTH, TC), lambda j: (0, j))
    w1 = pl.BlockSpec((1, TC), lambda j: (0, j))
    big = jax.ShapeDtypeStruct((SEQ, D_FF), BF16)
    extra = [] if run_after is None else [run_after]
    return pl.pallas_call(body, name=name, grid=(D_FF // TC,),
                          in_specs=[col, col, w3, w1, col] + [pl.BlockSpec(memory_space=pl.ANY)] * len(extra),
                          out_specs=[col, col, w3, w1],
                          out_shape=[big, big, jax.ShapeDtypeStruct((CONV_WIDTH, D_FF), F32),
                                     jax.ShapeDtypeStruct((1, D_FF), F32)],
                          compiler_params=_cp(("parallel",)))(g, u, cw, cb.reshape(1, -1), dh, *extra)


def _lbs_of(logits, layer):
    m = jnp.max(logits, 0, keepdims=True)
    e = jnp.exp(logits - m)
    p = e / jnp.sum(e, 0, keepdims=True)
    lb = jnp.zeros((1, BLK), F32)
    for r in range(1, layer + 1):
        lb = lb + p[r:r + 1, :]
    return lb, p


def _dlogits_of(p, dlb, layer):
    rows = lax.broadcasted_iota(jnp.int32, p.shape, 0)
    dp = jnp.where((rows >= 1) & (rows <= layer), dlb, 0.0)
    return p * (dp - jnp.sum(p * dp, 0, keepdims=True))


SROWS = SLAB * A_CHUNK
N_SLAB = N_CHUNK // SLAB


def _chunk_prefix(x, rowi):
    for s in (1, 2, 4, 8):
        x = x + jnp.where(rowi >= s, pltpu.roll(x, s, axis=0), 0.0)
    return x


def _chunk_suffix(x, rowi):
    for s in (1, 2, 4, 8):
        x = x + jnp.where(rowi < A_CHUNK - s, pltpu.roll(x, SROWS - s, axis=0), 0.0)
    return x


def _c3(x):
    return x.reshape(SLAB, A_CHUNK, BLK)


def _c2(x):
    return x.reshape(SROWS, BLK)


def _split(x):
    top = lax.bitcast_convert_type(lax.bitcast_convert_type(x, jnp.uint32) & jnp.uint32(0xFFFF0000), F32)
    return top.astype(BF16), (x - top).astype(BF16)


def _bmm(eq, a, b):
    ah, al = _split(a)
    bh, bl = _split(b)

    def mm(u, v):
        return jnp.einsum(eq, u, v, preferred_element_type=F32)

    return mm(ah, bh) + (mm(ah, bl) + mm(al, bh))


def _bmm_1pass(eq, a, b):
    return jnp.einsum(eq, a.astype(BF16), b.astype(BF16), preferred_element_type=F32)


def _slab_rows(s):
    return pl.ds(s * SROWS, SROWS)


def _hgrn_prep(q, f, lb):
    rowi = lax.broadcasted_iota(jnp.int32, (SROWS, BLK), 0) & (A_CHUNK - 1)
    sq = _gate_sigmoid(q)
    qc = q * sq
    sf = _sigmoid(f)
    fg = lb + (1.0 - lb) * sf
    kc = 1.0 - fg
    b = _chunk_prefix(jnp.log(fg), rowi)
    b3 = _c3(b)
    blast = b3[:, A_CHUNK - 1:A_CHUNK, :]
    eb = jnp.exp(b)
    ekb = _c2(jnp.exp(blast - b3))
    dec = jnp.exp(blast.reshape(SLAB, BLK))
    return rowi, sq, qc, sf, fg, kc, b, eb, ekb, dec


def _hgrn_slab_states(s, carry, v, ke, dec, dec_ref, u_ref, st_ref):
    dec_ref[pl.ds(s * SLAB, SLAB), :] = dec
    u_ref[...] = _bmm('ncv,nck->nvk', _c3(v), _c3(ke))

    def step(j, c):
        st_ref[j] = c
        return dec_ref[pl.ds(s * SLAB + j, 1), :] * c + u_ref[j]

    return lax.fori_loop(0, SLAB, step, carry)


def _hgrn_fwd(name, proj, lb_logits, nw, layer):
    def body(q_ref, f_ref, i_ref, g_ref, lg_ref, nw_ref, out_ref, outb_ref, raw_ref, dec_ref, u_ref, st_ref):
        lb, _ = _lbs_of(lg_ref[...], layer)
        ones = jnp.ones((BLK, BLK), BF16)
        carry = jnp.zeros((BLK, BLK), F32)
        for s in range(N_SLAB):
            rows = _slab_rows(s)
            v = i_ref[rows, :]
            rowi, sq, qc, sf, fg, kc, b, eb, ekb, dec = _hgrn_prep(q_ref[rows, :], f_ref[rows, :], lb)
            carry = _hgrn_slab_states(s, carry, v, kc * ekb, dec, dec_ref, u_ref, st_ref)
            o = _c2(_bmm('nck,nvk->ncv', _c3(qc * eb), st_ref[...]))
            qc3, kc3, b3, v3, row3 = _c3(qc), _c3(kc), _c3(b), _c3(v), _c3(rowi)
            col = lax.broadcasted_iota(jnp.int32, (SLAB, A_CHUNK, A_CHUNK), 2)
            att_all = jnp.zeros((SLAB, A_CHUNK, A_CHUNK), F32)
            for j in range(A_CHUNK):
                dj = jnp.exp(jnp.where(row3 >= j, b3 - b3[:, j:j + 1, :], -jnp.inf))
                a = jnp.dot(_c2(qc3 * dj * kc3[:, j:j + 1, :]).astype(BF16), ones, preferred_element_type=F32)
                att_all = jnp.where(col == j, _c3(a)[:, :, :A_CHUNK], att_all)
            o = o + _c2(_bmm_1pass('nij,njv->niv', att_all, v3))
            raw_ref[rows, :] = o
            r = lax.rsqrt(jnp.mean(o * o, -1, keepdims=True) + LN_EPS)
            gg = g_ref[rows, :]
            gated = o * r * nw_ref[...] * (gg * _gate_sigmoid(gg))
            out_ref[rows, :] = gated
            outb_ref[rows, :] = gated.astype(BF16)

    def colblk(c0):
        return pl.BlockSpec((SEQ, BLK), lambda h: (0, c0 + h))

    return pl.pallas_call(
        body, name=name, grid=(A_HEADS,),
        in_specs=[colblk(QA0), colblk(FA0), colblk(IA0), colblk(GA0),
                  pl.BlockSpec((DEPTH, BLK), lambda h: (0, h)), pl.BlockSpec((1, BLK), lambda h: (0, 0))],
        out_specs=[colblk(0), colblk(0), colblk(0)],
        out_shape=[jax.ShapeDtypeStruct((SEQ, MIX_WIDTH), F32), jax.ShapeDtypeStruct((SEQ, MIX_WIDTH), BF16),
                   jax.ShapeDtypeStruct((SEQ, A_HEADS * BLK), F32)],
        scratch_shapes=[pltpu.VMEM((N_CHUNK, BLK), F32), pltpu.VMEM((SLAB, BLK, BLK), F32),
                        pltpu.VMEM((SLAB, BLK, BLK), F32)],
        compiler_params=_cp(("parallel",)))(proj, proj, proj, proj, lb_logits, nw.reshape(1, -1))


def _col_block_copies(stage, sems, dst, col_blocks, first):
    return [pltpu.make_async_copy(stage.at[first + t], dst.at[:, pl.ds(pl.multiple_of(cb * BLK, BLK), BLK)],
                                  sems.at[first + t]) for t, cb in enumerate(col_blocks)]


def _start_col_blocks(stage, sems, dst, col_blocks, first=0):
    for cp in _col_block_copies(stage, sems, dst, col_blocks, first):
        cp.start()


def _wait_col_blocks(stage, sems, dst, count, first=0):
    for cp in _col_block_copies(stage, sems, dst, [0] * count, first):
        cp.wait()


def _hgrn_bwd(name, proj, raw, dmix, lb_logits, nw, layer, run_after=None):
    extra = [] if run_after is None else [run_after]

    def body(q_ref, f_ref, i_ref, g_ref, raw_ref, do_ref, lg_ref, nw_ref, *rest):
        (dproj_ref, dnw_ref, dlg_ref,
         dec_ref, u_ref, st_ref, h_ref, dbs_ref, dkc_ref, tot_ref, stage, stage_sem) = rest[-12:]
        dq_ref, df_ref, di_ref, dg_ref = (stage.at[t] for t in range(4))
        lb, p = _lbs_of(lg_ref[...], layer)
        ones = jnp.ones((BLK, BLK), BF16)
        nwv = nw_ref[...]

        carry = jnp.zeros((BLK, BLK), F32)
        for s in range(N_SLAB):
            rows = _slab_rows(s)
            rowi, sq, qc, sf, fg, kc, b, eb, ekb, dec = _hgrn_prep(q_ref[rows, :], f_ref[rows, :], lb)
            carry = _hgrn_slab_states(s, carry, i_ref[rows, :], kc * ekb, dec, dec_ref, u_ref,
                                      st_ref.at[pl.ds(s * SLAB, SLAB)])

        @pl.when(pl.program_id(0) > 0)
        def _():
            _wait_col_blocks(stage, stage_sem, dproj_ref, 4)

        carry = jnp.zeros((BLK, BLK), F32)
        dnw = jnp.zeros((1, BLK), F32)
        for s in reversed(range(N_SLAB)):
            rows = _slab_rows(s)
            q, v = q_ref[rows, :], i_ref[rows, :]
            rowi, sq, qc, sf, fg, kc, b, eb, ekb, dec = _hgrn_prep(q, f_ref[rows, :], lb)
            ke = kc * ekb
            qe = qc * eb

            o = raw_ref[rows, :]
            gg = g_ref[rows, :]
            sgg = _gate_sigmoid(gg)
            dout = do_ref[rows, :]
            r = lax.rsqrt(jnp.mean(o * o, -1, keepdims=True) + LN_EPS)
            oh = o * r
            dg_ref[rows, :] = (dout * oh * nwv * (sgg * (1.0 + gg * (1.0 - sgg)))).astype(BF16)
            dn = dout * (gg * sgg)
            dnw = dnw + jnp.sum(dn * oh, 0, keepdims=True)
            doh = dn * nwv
            do = r * (doh - oh * jnp.mean(doh * oh, -1, keepdims=True))
            do3, qe3, v3, ke3 = _c3(do), _c3(qe), _c3(v), _c3(ke)

            u_ref[...] = _bmm('ncv,nck->nvk', do3, qe3)

            def step(jj, c, s=s):
                j = SLAB - 1 - jj
                h_ref[j] = c
                return u_ref[j] + dec_ref[pl.ds(s * SLAB + j, 1), :] * c

            carry = lax.fori_loop(0, SLAB, step, carry)

            hh = h_ref[...]
            dqc = _c2(_bmm('ncv,nvk->nck', do3, st_ref[pl.ds(s * SLAB, SLAB)])) * eb
            dkc = _c2(_bmm('ncv,nvk->nck', v3, hh)) * ekb
            dv = _c2(_bmm_1pass('nck,nvk->ncv', ke3, hh))

            qc3, kc3, b3, row3 = _c3(qc), _c3(kc), _c3(b), _c3(rowi)
            datt_all = _bmm('niv,njv->nij', do3, v3)
            col = lax.broadcasted_iota(jnp.int32, datt_all.shape, 2)
            att_all = jnp.zeros_like(datt_all)
            for j in range(A_CHUNK):
                dj = jnp.exp(jnp.where(row3 >= j, b3 - b3[:, j:j + 1, :], -jnp.inf))
                kj = kc3[:, j:j + 1, :]
                att = _c3(jnp.dot(_c2(qc3 * dj * kj).astype(BF16), ones, preferred_element_type=F32))
                att_all = jnp.where(col == j, att[:, :, :A_CHUNK], att_all)
                md = dj * datt_all[:, :, j:j + 1]
                dqc = dqc + _c2(md * kj)
                dkc = dkc + _c2(jnp.where(row3 == j, jnp.sum(md * qc3, 1, keepdims=True), 0.0))
            dv = dv + _c2(_bmm_1pass('nij,niv->njv', att_all, do3))
            di_ref[rows, :] = dv.astype(BF16)
            dq_ref[rows, :] = (dqc * (sq * (1.0 + q * (1.0 - sq)))).astype(BF16)

            dbs = _chunk_suffix(qc * dqc - kc * dkc, rowi)
            dbs_ref[rows, :] = dbs
            dkc_ref[rows, :] = dkc
            tot_ref[pl.ds(s * SLAB, SLAB), :] = _c3(dbs)[:, 0:1, :].reshape(SLAB, BLK)
        dnw_ref[...] = jnp.broadcast_to(dnw, (8, BLK))

        rn = lax.broadcasted_iota(jnp.int32, (N_CHUNK, N_CHUNK), 0)
        cn = lax.broadcasted_iota(jnp.int32, (N_CHUNK, N_CHUNK), 1)
        tot_ref[...] = jnp.dot((cn > rn).astype(F32), tot_ref[...], preferred_element_type=F32, precision=HI)
        dlb = jnp.zeros((1, BLK), F32)
        for s in range(N_SLAB):
            rows = _slab_rows(s)
            sf = _sigmoid(f_ref[rows, :])
            fg = lb + (1.0 - lb) * sf
            later = tot_ref[pl.ds(s * SLAB, SLAB), :]
            dlg = _c2(_c3(dbs_ref[rows, :]) + later[:, None, :])
            dfg = dlg / fg - dkc_ref[rows, :]
            df_ref[rows, :] = (dfg * (1.0 - lb) * sf * (1.0 - sf)).astype(BF16)
            dlb = dlb + jnp.sum(dfg * (1.0 - sf), 0, keepdims=True)
        dlg_ref[...] = _dlogits_of(p, dlb, layer)
        _start_col_blocks(stage, stage_sem, dproj_ref, [c0 + pl.program_id(0) for c0 in (QA0, FA0, IA0, GA0)])

        @pl.when(pl.program_id(0) == A_HEADS - 1)
        def _():
            _wait_col_blocks(stage, stage_sem, dproj_ref, 4)

    def colblk(c0):
        return pl.BlockSpec((SEQ, BLK), lambda h: (0, c0 + h))

    return pl.pallas_call(
        body, name=name, grid=(A_HEADS,),
        in_specs=[colblk(QA0), colblk(FA0), colblk(IA0), colblk(GA0), colblk(0), colblk(0),
                  pl.BlockSpec((DEPTH, BLK), lambda h: (0, h)), pl.BlockSpec((1, BLK), lambda h: (0, 0))]
        + [pl.BlockSpec(memory_space=pl.ANY)] * len(extra),
        out_specs=[pl.BlockSpec(memory_space=pl.ANY),
                   pl.BlockSpec((8, BLK), lambda h: (h, 0)), pl.BlockSpec((DEPTH, BLK), lambda h: (0, h))],
        out_shape=[jax.ShapeDtypeStruct((SEQ, IN_WIDTH), BF16), jax.ShapeDtypeStruct((A_HEADS * 8, BLK), F32),
                   jax.ShapeDtypeStruct((DEPTH, A_HEADS * BLK), F32)],
        scratch_shapes=[pltpu.VMEM((N_CHUNK, BLK), F32), pltpu.VMEM((SLAB, BLK, BLK), F32),
                        pltpu.VMEM((N_CHUNK, BLK, BLK), F32), pltpu.VMEM((SLAB, BLK, BLK), F32),
                        pltpu.VMEM((SEQ, BLK), F32), pltpu.VMEM((SEQ, BLK), F32), pltpu.VMEM((N_CHUNK, BLK), F32),
                        pltpu.VMEM((4, SEQ, BLK), BF16), pltpu.SemaphoreType.DMA((4,))],
        compiler_params=_cp(("arbitrary",)))(proj, proj, proj, proj, raw, dmix, lb_logits, nw.reshape(1, -1), *extra)


SCALE = HEAD_DIM ** -0.5


def _rope_tables():
    half = ROPE_DIM // 2
    inv = ROPE_THETA ** (-jnp.arange(0, ROPE_DIM, 2, dtype=F32) / ROPE_DIM)
    ang = jnp.arange(SEQ, dtype=F32)[:, None] * inv[None, :]
    cos, sin = jnp.cos(ang), jnp.sin(ang)
    pad = jnp.zeros((SEQ, HEAD_DIM - ROPE_DIM), F32)
    zero = jnp.zeros((SEQ, half), F32)
    c = jnp.concatenate([cos, cos, pad + 1.0], 1)
    s_lo = jnp.concatenate([zero, sin, pad], 1)
    s_hi = jnp.concatenate([-sin, zero, pad], 1)
    return c, s_lo, s_hi


def _rope(x, c, s_lo, s_hi):
    half = ROPE_DIM // 2
    return x * c + pltpu.roll(x, half, axis=1) * s_lo + pltpu.roll(x, HEAD_DIM - half, axis=1) * s_hi


def _unrope(dy, c, s_lo, s_hi):
    half = ROPE_DIM // 2
    return dy * c + pltpu.roll(dy * s_lo, HEAD_DIM - half, axis=1) + pltpu.roll(dy * s_hi, half, axis=1)


N_BLK = SEQ // BLK


def _block_rows(dil):
    nb = N_BLK // dil
    return [pl.ds(r + n * BLK * dil, BLK, stride=dil) for r in range(dil) for n in range(nb)]


def _to_blocks(ref, dil):
    if dil == 1:
        return ref[...].reshape(N_BLK, BLK, BLK)
    return jnp.stack([ref[rows, :] for rows in _block_rows(dil)], 0)


def _from_blocks(ref, val, dil, add=False):
    if dil == 1:
        flat = val.reshape(SEQ, BLK)
        ref[...] = ref[...] + flat if add else flat
        return
    for b, rows in enumerate(_block_rows(dil)):
        ref[rows, :] = ref[rows, :] + val[b] if add else val[b]


def _prev_block(x):
    return jnp.concatenate([x[:1], x[:-1]], axis=0)


def _to_next_block(x):
    return jnp.concatenate([x[1:], jnp.zeros_like(x[:1])], axis=0)


def _band_masks(max_lag, dil):
    r = lax.broadcasted_iota(jnp.int32, (N_BLK, BLK, BLK), 1)
    c = lax.broadcasted_iota(jnp.int32, (N_BLK, BLK, BLK), 2)
    b = lax.broadcasted_iota(jnp.int32, (N_BLK, BLK, BLK), 0)
    has_prev = (b % (N_BLK // dil)) != 0
    return r >= c, has_prev & (BLK + r - c <= max_lag)


def _bdot(eq, a, b):
    return jnp.einsum(eq, a, b, preferred_element_type=F32)


def _attn_fwd(name, proj, tables, sink_b, mixed, mixed_bf, *, n_heads, rep, q0, k0, v0, m0, patterns):
    n_pat = len(patterns)
    has_sink = sink_b is not None

    def body(*refs):
        o_ref, ob_ref, l_ref, qr, kr, op, lse_ref = refs[-7:]
        q_ref, k_ref, v_ref, c_ref, sl_ref, sh_ref = refs[:6]
        if has_sink:
            sk = refs[6][0:1, 0:1]
        c, s_lo, s_hi = c_ref[...], sl_ref[...], sh_ref[...]
        qr[...] = _rope(q_ref[...], c, s_lo, s_hi)
        kr[...] = _rope(k_ref[...], c, s_lo, s_hi)
        for p, (max_lag, dil) in enumerate(patterns):
            qa = _to_blocks(qr, dil).astype(BF16)
            ka = _to_blocks(kr, dil).astype(BF16)
            va = _to_blocks(v_ref, dil).astype(BF16)
            own, before = _band_masks(max_lag, dil)
            s1 = jnp.where(own, _bdot('nqd,nkd->nqk', qa, ka) * SCALE, -jnp.inf)
            m = jnp.max(s1, -1, keepdims=True)
            with_prev = dil < N_BLK
            if with_prev:
                kp, vp = _prev_block(ka), _prev_block(va)
                s0 = jnp.where(before, _bdot('nqd,nkd->nqk', qa, kp) * SCALE, -jnp.inf)
                m = jnp.maximum(m, jnp.max(s0, -1, keepdims=True))
            if has_sink:
                m = jnp.maximum(m, sk)
            e1 = jnp.exp(s1 - m)
            den = jnp.sum(e1, -1, keepdims=True)
            o = _bdot('nqk,nkd->nqd', e1.astype(BF16), va)
            if with_prev:
                e0 = jnp.exp(s0 - m)
                den = den + jnp.sum(e0, -1, keepdims=True)
                o = o + _bdot('nqk,nkd->nqd', e0.astype(BF16), vp)
            if has_sink:
                den = den + jnp.exp(sk - m)
            _from_blocks(op.at[p], o / den, dil)
            _from_blocks(lse_ref.at[p], jnp.broadcast_to(m + jnp.log(den), (N_BLK, BLK, BLK)), dil)
        if n_pat == 1:
            acc = op[0]
            l_ref[...] = lse_ref[0]
        else:
            ls = [lse_ref[p] for p in range(n_pat)]
            m = functools.reduce(jnp.maximum, ls)
            es = [jnp.exp(l - m) for l in ls]
            tot = functools.reduce(jnp.add, es)
            acc = None
            for p in range(n_pat):
                t = (es[p] / tot) * op[p]
                acc = t if acc is None else acc + t
            l_ref[...] = m + jnp.log(tot)
        o_ref[...] = acc
        ob_ref[...] = acc.astype(BF16)

    def colblk(fn):
        return pl.BlockSpec((SEQ, BLK), fn)

    tab = pl.BlockSpec((SEQ, BLK), lambda h: (0, 0))
    in_specs = [colblk(lambda h: (0, q0 + h)), colblk(lambda h: (0, k0 + h // rep)), colblk(lambda h: (0, v0 + h // rep)),
                tab, tab, tab]
    args = [proj, proj, proj, *tables]
    if has_sink:
        in_specs.append(pl.BlockSpec((None, 8, BLK), lambda h: (h, 0, 0)))
        args.append(sink_b)
    n_in = len(args)
    in_specs += [pl.BlockSpec(memory_space=pl.ANY)] * 2
    args += [mixed, mixed_bf]
    pat = pltpu.VMEM((n_pat, SEQ, BLK), F32)
    return pl.pallas_call(
        body, name=name, grid=(n_heads,), in_specs=in_specs,
        out_specs=[colblk(lambda h: (0, m0 + h)), colblk(lambda h: (0, m0 + h)),
                   pl.BlockSpec((None, SEQ, BLK), lambda h: (h, 0, 0))],
        out_shape=[jax.ShapeDtypeStruct(mixed.shape, F32), jax.ShapeDtypeStruct(mixed.shape, BF16),
                   jax.ShapeDtypeStruct((n_heads, SEQ, BLK), F32)],
        input_output_aliases={n_in: 0, n_in + 1: 1},
        scratch_shapes=[pltpu.VMEM((SEQ, BLK), F32), pltpu.VMEM((SEQ, BLK), F32), pat, pat],
        compiler_params=_cp(("parallel",)))(*args)


def _attn_bwd(name, proj, mixed, dmix, lse, tables, sink_b, dproj, *, n_kv, rep, q0, k0, v0, m0, patterns):
    n_heads = n_kv * rep
    has_sink = sink_b is not None

    def body(*refs):
        q_ref, k_ref, v_ref, o_ref, do_ref, lse_ref, c_ref, sl_ref, sh_ref = refs[:9]
        sink_ref = refs[9] if has_sink else None
        dproj_ref, dsk_ref, qr, kr, dqa, dka, dva, dd, stage, stage_sem = refs[-10:]
        g, j = pl.program_id(0), pl.program_id(1)
        c, s_lo, s_hi = c_ref[...], sl_ref[...], sh_ref[...]
        qr[...] = _rope(q_ref[...], c, s_lo, s_hi)
        kr[...] = _rope(k_ref[...], c, s_lo, s_hi)
        dcol = jnp.sum(do_ref[...] * o_ref[...], -1, keepdims=True)
        dd[...] = jnp.broadcast_to(dcol, (SEQ, BLK))

        @pl.when(j == 0)
        def _():
            dka[...] = jnp.zeros((SEQ, BLK), F32)
            dva[...] = jnp.zeros((SEQ, BLK), F32)

        for p, (max_lag, dil) in enumerate(patterns):
            qa = _to_blocks(qr, dil).astype(BF16)
            ka = _to_blocks(kr, dil).astype(BF16)
            va = _to_blocks(v_ref, dil).astype(BF16)
            doa = _to_blocks(do_ref, dil).astype(BF16)
            lcol = _to_blocks(lse_ref, dil)[:, :, 0:1]
            dcb = _to_blocks(dd, dil)[:, :, 0:1]
            own, before = _band_masks(max_lag, dil)

            def probs_and_ds(kk, vv, valid):
                s = _bdot('nqd,nkd->nqk', qa, kk) * SCALE
                a = jnp.where(valid, jnp.exp(s - lcol), 0.0)
                ds = a * (_bdot('nqd,nkd->nqk', doa, vv) - dcb) * SCALE
                return a.astype(BF16), ds.astype(BF16)

            a1, ds1 = probs_and_ds(ka, va, own)
            dq = _bdot('nqk,nkd->nqd', ds1, ka)
            dk = _bdot('nqk,nqd->nkd', ds1, qa)
            dv = _bdot('nqk,nqd->nkd', a1, doa)
            if dil < N_BLK:
                kp, vp = _prev_block(ka), _prev_block(va)
                a0, ds0 = probs_and_ds(kp, vp, before)
                dq = dq + _bdot('nqk,nkd->nqd', ds0, kp)
                dk = dk + _to_next_block(_bdot('nqk,nqd->nkd', ds0, qa))
                dv = dv + _to_next_block(_bdot('nqk,nqd->nkd', a0, doa))
            _from_blocks(dqa, dq, dil, add=p > 0)
            _from_blocks(dka, dk, dil, add=True)
            _from_blocks(dva, dv, dil, add=True)

        if has_sink:
            sk = sink_ref[0:1, 0:1]
            ps = jnp.exp(sk - lse_ref[...][:, 0:1])
            dsk_ref[...] = jnp.full((8, BLK), -jnp.sum(ps * dcol), F32)
        else:
            dsk_ref[...] = jnp.zeros((8, BLK), F32)
        @pl.when(g * rep + j > 0)
        def _():
            _wait_col_blocks(stage, stage_sem, dproj_ref, 1)

        stage[0] = _unrope(dqa[...], c, s_lo, s_hi).astype(BF16)
        _start_col_blocks(stage, stage_sem, dproj_ref, [q0 + g * rep + j])

        @pl.when(j == rep - 1)
        def _():
            @pl.when(g > 0)
            def _():
                _wait_col_blocks(stage, stage_sem, dproj_ref, 2, first=1)

            stage[1] = _unrope(dka[...], c, s_lo, s_hi).astype(BF16)
            stage[2] = dva[...].astype(BF16)
            _start_col_blocks(stage, stage_sem, dproj_ref, [k0 + g, v0 + g], first=1)

        @pl.when((g == n_kv - 1) & (j == rep - 1))
        def _():
            _wait_col_blocks(stage, stage_sem, dproj_ref, 3)

    def colblk(fn):
        return pl.BlockSpec((SEQ, BLK), fn)

    tab = pl.BlockSpec((SEQ, BLK), lambda g, j: (0, 0))
    in_specs = [colblk(lambda g, j: (0, q0 + g * rep + j)), colblk(lambda g, j: (0, k0 + g)), colblk(lambda g, j: (0, v0 + g)),
                colblk(lambda g, j: (0, m0 + g * rep + j)), colblk(lambda g, j: (0, m0 + g * rep + j)),
                pl.BlockSpec((None, SEQ, BLK), lambda g, j: (g * rep + j, 0, 0)), tab, tab, tab]
    args = [proj, proj, proj, mixed, dmix, lse, *tables]
    if has_sink:
        in_specs.append(pl.BlockSpec((None, 8, BLK), lambda g, j: (g * rep + j, 0, 0)))
        args.append(sink_b)
    n_in = len(args)
    in_specs.append(pl.BlockSpec(memory_space=pl.ANY))
    args.append(dproj)
    acc = pltpu.VMEM((SEQ, BLK), F32)
    return pl.pallas_call(
        body, name=name, grid=(n_kv, rep), in_specs=in_specs,
        out_specs=[pl.BlockSpec(memory_space=pl.ANY), pl.BlockSpec((None, 8, BLK), lambda g, j: (g * rep + j, 0, 0))],
        out_shape=[jax.ShapeDtypeStruct(dproj.shape, BF16), jax.ShapeDtypeStruct((n_heads, 8, BLK), F32)],
        input_output_aliases={n_in: 0},
        scratch_shapes=[acc, acc, acc, acc, acc, acc, pltpu.VMEM((3, SEQ, BLK), BF16), pltpu.SemaphoreType.DMA((3,))],
        compiler_params=_cp(("arbitrary", "arbitrary")))(*args)


B_PATTERNS = tuple((w // d, d) for w, d in DILATED_PATTERNS)
C_PATTERNS = ((C_WINDOW - 1, 1),)


ANY = pl.BlockSpec(memory_space=pl.ANY)
CHIP_MASKS = ((1, 0), (0, 1), (1, 1))


def _coords():
    return lax.axis_index("x"), lax.axis_index("y"), lax.axis_index("c")


def _flip(v, m):
    return 1 - v if m else v


def _into_slot(name, w, layer, k_idx, dtype, run_after=None):
    _, rows, cols = w.shape
    tr = rows // 8 if rows % 64 == 0 else rows

    def body(k_ref, w_ref, *rest):
        rest[-1][...] = w_ref[...].astype(dtype)

    in_specs = [pl.BlockSpec((None, tr, cols), lambda i, k: (layer, i, 0))]
    args = [k_idx, w]
    if run_after is not None:
        in_specs.append(pl.BlockSpec(memory_space=pl.ANY))
        args.append(run_after)
    return pl.pallas_call(
        body, name=name,
        grid_spec=pltpu.PrefetchScalarGridSpec(
            num_scalar_prefetch=1, grid=(rows // tr,), in_specs=in_specs,
            out_specs=pl.BlockSpec((None, tr, cols), lambda i, k: (k[0], i, 0))),
        out_shape=jax.ShapeDtypeStruct((N_CHIPS, rows, cols), dtype),
        compiler_params=_cp(("parallel",)))(*args)


HBM_SPEC = pl.BlockSpec(memory_space=pltpu.HBM)
SEM_SPEC = pl.BlockSpec(memory_space=pltpu.SEMAPHORE)
TOKEN_SPEC = pl.BlockSpec(memory_space=pltpu.VMEM)
TOKEN_SHAPE = jax.ShapeDtypeStruct((8, BLK), F32)
DATAFLOW = pltpu.SideEffectType.DATAFLOW_SIDE_EFFECTING


def _hbm(a):
    return pltpu.with_memory_space_constraint(a, pltpu.HBM)


def _hbm_like(bufs):
    return [pltpu.HBM(b.shape, b.dtype) for b in bufs]


def _gather_start(name, stages):
    flat = [b for st in stages for b in st]
    n, ns = len(flat), len(stages)

    def body(*refs):
        ins = refs[:n]
        sems = refs[n:n + 2 * ns]
        token = refs[-1]
        x, y, c = _coords()
        k_me = 2 * x + y
        a = 0
        for s, st in enumerate(stages):
            for i in range(len(st)):
                mine = ins[a].at[k_me, c]
                for m, (mx, my) in enumerate(CHIP_MASKS):
                    pltpu.make_async_remote_copy(src_ref=mine, dst_ref=mine, send_sem=sems[2 * s].at[i * 3 + m],
                                                 recv_sem=sems[2 * s + 1].at[i * 3 + m],
                                                 device_id=(_flip(x, mx), _flip(y, my), c), device_id_type=MESH).start()
                a += 1
        token[...] = jnp.zeros_like(token)

    sem_shapes = []
    for st in stages:
        sem_shapes += [pltpu.SemaphoreType.DMA((3 * len(st),))] * 2
    out = pl.pallas_call(
        body, name=name, in_specs=[HBM_SPEC] * n,
        out_specs=tuple([SEM_SPEC] * (2 * ns) + [HBM_SPEC] * n + [TOKEN_SPEC]),
        out_shape=tuple(sem_shapes + _hbm_like(flat) + [TOKEN_SHAPE]),
        input_output_aliases={i: 2 * ns + i for i in range(n)},
        compiler_params=pltpu.CompilerParams(has_side_effects=DATAFLOW),
    )(*[_hbm(b) for b in flat])
    sems, bufs, token = out[:2 * ns], out[2 * ns:2 * ns + n], out[-1]
    res, a = [], 0
    for s, st in enumerate(stages):
        res.append((sems[2 * s], sems[2 * s + 1], list(bufs[a:a + len(st)])))
        a += len(st)
    return res, token


def _gather_forward(name, stage, after):
    ssem_in, rsem_in, bufs = stage
    n = len(bufs)

    def body(*refs):
        ins = refs[:n]
        s_in, r_in, _ = refs[n:n + 3]
        s_out, r_out = refs[n + 3:n + 5]
        token = refs[-1]
        x, y, c = _coords()
        for i in range(n):
            for m, (mx, my) in enumerate(CHIP_MASKS):
                kp = 2 * _flip(x, mx) + _flip(y, my)
                blk = ins[i].at[kp, c]
                got = pltpu.make_async_remote_copy(src_ref=blk, dst_ref=blk, send_sem=s_in.at[i * 3 + m],
                                                   recv_sem=r_in.at[i * 3 + m], device_id=(x, y, 1 - c), device_id_type=MESH)
                got.wait_send()
                got.wait_recv()
                pltpu.make_async_remote_copy(src_ref=blk, dst_ref=blk, send_sem=s_out.at[i * 3 + m],
                                             recv_sem=r_out.at[i * 3 + m], device_id=(x, y, 1 - c), device_id_type=MESH).start()
        token[...] = jnp.zeros_like(token)

    sem = pltpu.SemaphoreType.DMA((3 * n,))
    out = pl.pallas_call(
        body, name=name, in_specs=[HBM_SPEC] * n + [SEM_SPEC, SEM_SPEC, ANY],
        out_specs=tuple([SEM_SPEC, SEM_SPEC] + [HBM_SPEC] * n + [TOKEN_SPEC]),
        out_shape=tuple([sem, sem] + _hbm_like(bufs) + [TOKEN_SHAPE]),
        input_output_aliases={i: 2 + i for i in range(n)},
        compiler_params=pltpu.CompilerParams(has_side_effects=DATAFLOW),
    )(*bufs, ssem_in, rsem_in, after)
    return (out[0], out[1], list(out[2:2 + n])), out[-1]


def _gather_wait(name, stage, after):
    ssem, rsem, bufs = stage
    n = len(bufs)

    def body(*refs):
        ins = refs[:n]
        s_in, r_in, _ = refs[n:n + 3]
        x, y, c = _coords()
        for i in range(n):
            for m, (mx, my) in enumerate(CHIP_MASKS):
                kp = 2 * _flip(x, mx) + _flip(y, my)
                sent, got = ins[i].at[kp, c], ins[i].at[kp, 1 - c]
                cp = pltpu.make_async_remote_copy(src_ref=sent, dst_ref=got, send_sem=s_in.at[i * 3 + m],
                                                  recv_sem=r_in.at[i * 3 + m], device_id=(x, y, 1 - c), device_id_type=MESH)
                cp.wait_send()
                cp.wait_recv()

    out = pl.pallas_call(
        body, name=name, in_specs=[HBM_SPEC] * n + [SEM_SPEC, SEM_SPEC, ANY],
        out_specs=tuple([HBM_SPEC] * n), out_shape=tuple(_hbm_like(bufs)),
        input_output_aliases={i: i for i in range(n)},
        compiler_params=pltpu.CompilerParams(has_side_effects=DATAFLOW),
    )(*bufs, ssem, rsem, after)
    return list(out)


def _swap_start(name, grads):
    n = len(grads)

    def body(*refs):
        ins, lands = refs[:n], refs[n:2 * n]
        ssem, rsem = refs[2 * n:2 * n + 2]
        x, y, c = _coords()
        for a in range(n):
            for j in range(N_CHIPS):
                pltpu.make_async_remote_copy(src_ref=ins[a].at[j, 1 - c], dst_ref=lands[a].at[j],
                                             send_sem=ssem.at[a * N_CHIPS + j], recv_sem=rsem.at[a * N_CHIPS + j],
                                             device_id=(x, y, 1 - c), device_id_type=MESH).start()

    sem = pltpu.SemaphoreType.DMA((N_CHIPS * n,))
    land_shapes = [pltpu.HBM((N_CHIPS,) + g.shape[2:], g.dtype) for g in grads]
    out = pl.pallas_call(
        body, name=name, in_specs=[HBM_SPEC] * (2 * n),
        out_specs=tuple([SEM_SPEC, SEM_SPEC] + [HBM_SPEC] * (2 * n)),
        out_shape=tuple([sem, sem] + _hbm_like(grads) + land_shapes),
        input_output_aliases={i: 2 + i for i in range(2 * n)},
        compiler_params=pltpu.CompilerParams(has_side_effects=DATAFLOW),
    )(*[_hbm(g) for g in grads], *[_hbm(lax.empty((N_CHIPS,) + g.shape[2:], g.dtype)) for g in grads])
    return out[0], out[1], list(out[2:2 + n]), list(out[2 + n:])


def _swap_wait(name, started, after):
    ssem, rsem, grads, lands = started
    n = len(grads)
    after = after if isinstance(after, tuple) else (after,)

    def body(*refs):
        ins, lnd = refs[:n], refs[n:2 * n]
        s_in, r_in = refs[2 * n:2 * n + 2]
        x, y, c = _coords()
        for a in range(n):
            for j in range(N_CHIPS):
                cp = pltpu.make_async_remote_copy(src_ref=ins[a].at[j, 1 - c], dst_ref=lnd[a].at[j],
                                                  send_sem=s_in.at[a * N_CHIPS + j], recv_sem=r_in.at[a * N_CHIPS + j],
                                                  device_id=(x, y, 1 - c), device_id_type=MESH)
                cp.wait_send()
                cp.wait_recv()

    out = pl.pallas_call(
        body, name=name, in_specs=[HBM_SPEC] * (2 * n) + [SEM_SPEC, SEM_SPEC] + [ANY] * len(after),
        out_specs=tuple([HBM_SPEC] * (2 * n)), out_shape=tuple(_hbm_like(grads) + _hbm_like(lands)),
        input_output_aliases={i: i for i in range(2 * n)},
        compiler_params=pltpu.CompilerParams(has_side_effects=DATAFLOW),
    )(*grads, *lands, ssem, rsem, *after)
    return list(out[:n]), list(out[n:])


def _scatter_start(name, parts):
    n = len(parts)

    def body(*refs):
        ins, lands = refs[:n], refs[n:2 * n]
        ssem, rsem = refs[2 * n:2 * n + 2]
        x, y, c = _coords()
        k_me = 2 * x + y
        for a in range(n):
            for m, (mx, my) in enumerate(CHIP_MASKS):
                px, py = _flip(x, mx), _flip(y, my)
                pltpu.make_async_remote_copy(src_ref=ins[a].at[2 * px + py], dst_ref=lands[a].at[k_me],
                                             send_sem=ssem.at[a * 3 + m], recv_sem=rsem.at[a * 3 + m],
                                             device_id=(px, py, c), device_id_type=MESH).start()

    sem = pltpu.SemaphoreType.DMA((3 * n,))
    out = pl.pallas_call(
        body, name=name, in_specs=[HBM_SPEC] * (2 * n),
        out_specs=tuple([SEM_SPEC, SEM_SPEC] + [HBM_SPEC] * (2 * n)),
        out_shape=tuple([sem, sem] + _hbm_like(parts) + _hbm_like(parts)),
        input_output_aliases={i: 2 + i for i in range(2 * n)},
        compiler_params=pltpu.CompilerParams(has_side_effects=DATAFLOW),
    )(*[_hbm(p) for p in parts], *[_hbm(lax.empty(p.shape, p.dtype)) for p in parts])
    return out[0], out[1], list(out[2:2 + n]), list(out[2 + n:])


def _scatter_wait(name, started, after):
    ssem, rsem, parts, lands = started
    n = len(parts)

    def body(*refs):
        ins, lnd = refs[:n], refs[n:2 * n]
        s_in, r_in, _ = refs[2 * n:2 * n + 3]
        x, y, c = _coords()
        k_me = 2 * x + y
        for a in range(n):
            for m, (mx, my) in enumerate(CHIP_MASKS):
                px, py = _flip(x, mx), _flip(y, my)
                cp = pltpu.make_async_remote_copy(src_ref=ins[a].at[2 * px + py], dst_ref=lnd[a].at[k_me],
                                                  send_sem=s_in.at[a * 3 + m], recv_sem=r_in.at[a * 3 + m],
                                                  device_id=(px, py, c), device_id_type=MESH)
                cp.wait_send()
                cp.wait_recv()

    out = pl.pallas_call(
        body, name=name, in_specs=[HBM_SPEC] * (2 * n) + [SEM_SPEC, SEM_SPEC, ANY],
        out_specs=tuple([HBM_SPEC] * (2 * n)), out_shape=tuple(_hbm_like(parts) + _hbm_like(lands)),
        input_output_aliases={i: i for i in range(2 * n)},
        compiler_params=pltpu.CompilerParams(has_side_effects=DATAFLOW),
    )(*parts, *lands, ssem, rsem, after)
    return list(out[:n]), list(out[n:])


def _pair_gather_start(name, bufs):
    n = len(bufs)

    def body(*refs):
        ins = refs[:n]
        ssem, rsem = refs[n:n + 2]
        x, y, c = _coords()
        for a in range(n):
            mine = ins[a].at[c]
            pltpu.make_async_remote_copy(src_ref=mine, dst_ref=mine, send_sem=ssem.at[a], recv_sem=rsem.at[a],
                                         device_id=(x, y, 1 - c), device_id_type=MESH).start()

    sem = pltpu.SemaphoreType.DMA((n,))
    out = pl.pallas_call(
        body, name=name, in_specs=[HBM_SPEC] * n, out_specs=tuple([SEM_SPEC, SEM_SPEC] + [HBM_SPEC] * n),
        out_shape=tuple([sem, sem] + _hbm_like(bufs)),
        input_output_aliases={i: 2 + i for i in range(n)},
        compiler_params=pltpu.CompilerParams(has_side_effects=DATAFLOW),
    )(*[_hbm(b) for b in bufs])
    return out[0], out[1], list(out[2:])


def _pair_gather_wait(name, started, after):
    ssem, rsem, bufs = started
    n = len(bufs)

    def body(*refs):
        ins = refs[:n]
        s_in, r_in, _ = refs[n:n + 3]
        x, y, c = _coords()
        for a in range(n):
            cp = pltpu.make_async_remote_copy(src_ref=ins[a].at[c], dst_ref=ins[a].at[1 - c], send_sem=s_in.at[a],
                                              recv_sem=r_in.at[a], device_id=(x, y, 1 - c), device_id_type=MESH)
            cp.wait_send()
            cp.wait_recv()

    out = pl.pallas_call(
        body, name=name, in_specs=[HBM_SPEC] * n + [SEM_SPEC, SEM_SPEC, ANY],
        out_specs=tuple([HBM_SPEC] * n), out_shape=tuple(_hbm_like(bufs)),
        input_output_aliases={i: i for i in range(n)},
        compiler_params=pltpu.CompilerParams(has_side_effects=DATAFLOW),
    )(*bufs, ssem, rsem, after)
    return list(out)


DEV_MASKS = tuple((mx, my, mc) for mx in (0, 1) for my in (0, 1) for mc in (0, 1) if (mx, my, mc) != (0, 0, 0))


def _gather_small(buf, run_after):
    def body(in_ref, _, out_ref, ssem, rsem, lsem):
        x, y, c = _coords()
        me = 4 * x + 2 * y + c
        cps = [pltpu.make_async_copy(in_ref, out_ref.at[me], lsem)]
        cps[0].start()
        for t, (mx, my, mc) in enumerate(DEV_MASKS):
            cp = pltpu.make_async_remote_copy(src_ref=in_ref, dst_ref=out_ref.at[me], send_sem=ssem.at[t],
                                              recv_sem=rsem.at[t], device_id=(_flip(x, mx), _flip(y, my), _flip(c, mc)),
                                              device_id_type=MESH)
            cp.start()
            cps.append(cp)
        for cp in cps:
            cp.wait()

    return pl.pallas_call(
        body, name="gather_small", in_specs=[ANY, ANY], out_specs=ANY,
        out_shape=jax.ShapeDtypeStruct((N_DEV,) + buf.shape, buf.dtype),
        scratch_shapes=[pltpu.SemaphoreType.DMA((N_DEV - 1,)), pltpu.SemaphoreType.DMA((N_DEV - 1,)),
                        pltpu.SemaphoreType.DMA(())],
        compiler_params=pltpu.CompilerParams(has_side_effects=True),
    )(buf, run_after)


def _row_tile(rows):
    return rows // 2 if rows % 16 == 0 else rows


def _pair_add(name, grad, got, c_idx):
    _, _, r2, cols = grad.shape
    tr = _row_tile(r2)

    def body(c_ref, a_ref, b_ref, o_ref):
        o_ref[...] = (a_ref[...].astype(F32) + b_ref[...].astype(F32)).astype(BF16)

    return pl.pallas_call(
        body, name=name,
        grid_spec=pltpu.PrefetchScalarGridSpec(
            num_scalar_prefetch=1, grid=(N_CHIPS, r2 // tr),
            in_specs=[pl.BlockSpec((None, None, tr, cols), lambda j, i, c: (j, c[0], i, 0)),
                      pl.BlockSpec((None, tr, cols), lambda j, i, c: (j, i, 0))],
            out_specs=pl.BlockSpec((None, tr, cols), lambda j, i, c: (j, i, 0))),
        out_shape=jax.ShapeDtypeStruct((N_CHIPS, r2, cols), BF16),
        compiler_params=_cp(("parallel", "parallel")))(c_idx, grad, got)


def _chip_add(name, part, got, kc_idx):
    _, r2, cols = got.shape
    tr = _row_tile(r2)

    def body(k_ref, p_ref, g1_ref, g2_ref, g3_ref, o_ref):
        acc = p_ref[...].astype(F32)
        for g_ref in (g1_ref, g2_ref, g3_ref):
            acc = acc + g_ref[...].astype(F32)
        o_ref[...] = acc

    def slot(d):
        return pl.BlockSpec((None, tr, cols), lambda i, k: ((k[0] + d) % N_CHIPS, i, 0))

    return pl.pallas_call(
        body, name=name,
        grid_spec=pltpu.PrefetchScalarGridSpec(
            num_scalar_prefetch=1, grid=(r2 // tr,),
            in_specs=[slot(0), slot(1), slot(2), slot(3)],
            out_specs=pl.BlockSpec((None, tr, cols), lambda i, k: (k[1], i, 0))),
        out_shape=jax.ShapeDtypeStruct((2, r2, cols), F32),
        compiler_params=_cp(("parallel",)))(kc_idx, part, got, got, got)


def _adam_math(w, g, m, v):
    m2 = ADAM_B1 * m + (1.0 - ADAM_B1) * g
    v2 = ADAM_B2 * v + (1.0 - ADAM_B2) * (g * g)
    m_hat = m2 / (1.0 - ADAM_B1 ** ADAM_STEP)
    v_hat = v2 / (1.0 - ADAM_B2 ** ADAM_STEP)
    delta = -ADAM_LR * (m_hat / (jnp.sqrt(v_hat) + ADAM_EPS) + ADAM_WD * w)
    return delta, m2, v2


def _adamw_matrix(name, w, g_layers, m, v):
    _, rows, cols = w.shape
    tr = rows // 16 if rows % 128 == 0 else rows // 8

    def body(w_ref, g0_ref, g1_ref, m_ref, v_ref, go_ref, d_ref, mo_ref, vo_ref):
        g = jnp.where(pl.program_id(0) == 0, g0_ref[...], g1_ref[...])
        go_ref[...] = g
        d_ref[...], mo_ref[...], vo_ref[...] = _adam_math(w_ref[...], g, m_ref[...], v_ref[...])

    lay = pl.BlockSpec((None, tr, cols), lambda l, i: (l, i, 0))
    flat = pl.BlockSpec((tr, cols), lambda l, i: (i, 0))
    shp = jax.ShapeDtypeStruct(w.shape, F32)
    return pl.pallas_call(body, name=name, grid=(DEPTH, rows // tr), in_specs=[lay, flat, flat, lay, lay],
                          out_specs=[lay, lay, lay, lay], out_shape=[shp, shp, shp, shp],
                          compiler_params=_cp(("parallel", "parallel")))(w, g_layers[0], g_layers[1], m, v)


def _sum_small(gathered):
    def body(g_ref, o_ref):
        acc = g_ref[0]
        for d in range(1, N_DEV):
            acc = acc + g_ref[d]
        o_ref[...] = acc

    return pl.pallas_call(body, name="sum_small", out_shape=jax.ShapeDtypeStruct(gathered.shape[1:], F32),
                          compiler_params=_cp())(gathered)


def _adamw_small(w, g, m, v):
    def body(w_ref, g_ref, m_ref, v_ref, d_ref, mo_ref, vo_ref):
        d_ref[...], mo_ref[...], vo_ref[...] = _adam_math(w_ref[...], g_ref[...], m_ref[...], v_ref[...])

    shp = jax.ShapeDtypeStruct(w.shape, F32)
    return pl.pallas_call(body, name="adamw_small", out_shape=[shp, shp, shp], compiler_params=_cp())(w, g, m, v)


def _pack(arrays, rows):
    flat = jnp.concatenate([a.reshape(-1) for a in arrays])
    return jnp.pad(flat, (0, rows * BLK - flat.shape[0])).reshape(rows, BLK)


def _unpack(buf, shapes):
    flat = buf.reshape(-1)
    out, pos = [], 0
    for s in shapes:
        n = math.prod(s)
        out.append(flat[pos:pos + n].reshape(s))
        pos += n
    return out


def _rows_for(shapes):
    n = sum(math.prod(s) for s in shapes)
    return -(-n // (8 * BLK)) * 8


def _rs_swap(tag, grads):
    return _swap_start(f"rs_swap_start{tag}", [g.reshape(N_CHIPS, 2, g.shape[1] // 2, g.shape[2]) for g in grads])


def _rs_scatter(tag, swapping, after, c_idx):
    split, got = _swap_wait(f"rs_swap_wait{tag}", swapping, after)
    parts = [_pair_add(f"rs_pair_add{tag}_{i}", s, r, c_idx) for i, (s, r) in enumerate(zip(split, got))]
    return _scatter_start(f"rs_scatter_start{tag}", parts)


def _rs_reduce(tag, started, after, kc_idx):
    parts, lands = _scatter_wait(f"rs_scatter_wait{tag}", started, after)
    halves = [_chip_add(f"rs_chip_add{tag}_{i}", p, r, kc_idx) for i, (p, r) in enumerate(zip(parts, lands))]
    return _pair_gather_start(f"rs_pair_gather_start{tag}", halves)


def _rs_finish(tag, gathering, after):
    full = _pair_gather_wait(f"rs_pair_gather_wait{tag}", gathering, after)
    return [f.reshape(2 * f.shape[1], f.shape[2]) for f in full]


def kernel(x, w_in, lb_logits, a_norm_w, c_sinks, w_out, ln1_g, ln1_b, w_gate, w_up, conv_w, conv_b, w_down, ln2_g, ln2_b, loss_target, m_w_in, m_lb_logits, m_a_norm_w, m_c_sinks, m_w_out, m_ln1_g, m_ln1_b, m_w_gate, m_w_up, m_conv_w, m_conv_b, m_w_down, m_ln2_g, m_ln2_b, v_w_in, v_lb_logits, v_a_norm_w, v_c_sinks, v_w_out, v_ln1_g, v_ln1_b, v_w_gate, v_w_up, v_conv_w, v_conv_b, v_w_down, v_ln2_g, v_ln2_b):
    cx, cy, cc = _coords()
    c_idx = jnp.reshape(cc, (1,)).astype(jnp.int32)
    k_me = 2 * cx + cy
    k_idx = jnp.reshape(k_me, (1,)).astype(jnp.int32)
    kc_idx = jnp.stack([k_me, cc]).astype(jnp.int32)

    def slot(nm, w, l, run_after=None):
        b = _into_slot(f"slot_{nm}{l}", w, l, k_idx, BF16, run_after)
        return b.reshape(N_CHIPS, 2, b.shape[1] // 2, b.shape[2])

    cw_slot = _into_slot("slot_cw", conv_w.reshape(1, DEPTH * CONV_WIDTH, FF_SHARD), 0, k_idx, F32)
    cw_slot = cw_slot.reshape(N_CHIPS, DEPTH, CONV_WIDTH, FF_SHARD)
    first, token = _gather_start("gather_start0", [[slot("wi", w_in, 0), cw_slot]])
    sl = [{nm: slot(nm, w, l, token) for nm, w in (("wi", w_in), ("wo", w_out), ("wg", w_gate), ("wu", w_up), ("wd", w_down))
           if (nm, l) != ("wi", 0)} for l in range(DEPTH)]
    order = [(l, nm) for l in range(DEPTH) for nm in ("wi", "wo", "wg", "wu", "wd")][1:]
    rest, token = _gather_start("gather_start1", [[sl[l][nm]] for l, nm in order])
    stage_of = {key: st for key, st in zip(order, rest)}

    def mat(b):
        return b.reshape(N_CHIPS, 2 * b.shape[2], b.shape[3])

    fwd0, token = _gather_forward("gather_fwd0", first[0], token)
    wi0, cw_all = _gather_wait("gather_wait0", fwd0, token)
    cw_full = jnp.transpose(cw_all, (1, 2, 0, 3)).reshape(DEPTH, CONV_WIDTH, D_FF)
    tables = _rope_tables()

    passing = {}

    def pass_on(l, nm, after):
        passing[(l, nm)] = _gather_forward(f"gather_fwd_{nm}{l}", stage_of[(l, nm)], after)

    def arrived(l, nm, after):
        i = order.index((l, nm))
        if i + 1 < len(order):
            pass_on(*order[i + 1], after)
            after = passing[order[i + 1]][1]
        return mat(_gather_wait(f"gather_wait_{nm}{l}", passing[(l, nm)][0], after)[0])

    h = x[0]
    h_bf = _to_bf16("x_bf16", h)
    saved = []
    weights = []
    for l in range(DEPTH):
        wi = mat(wi0) if l == 0 else arrived(l, "wi", h)
        proj = _fwd_colsharded(f"proj{l}", h_bf, wi)
        mixed, mixed_bf, raw = _hgrn_fwd(f"hgrn_fwd{l}", proj, lb_logits, a_norm_w[l], l)
        mixed, mixed_bf, lse_b = _attn_fwd(f"dilated_fwd{l}", proj, tables, None, mixed, mixed_bf, n_heads=B_HEADS, rep=1,
                                           q0=QB0, k0=KB0, v0=VB0, m0=A_HEADS, patterns=B_PATTERNS)
        if l == 0:
            pass_on(l, "wo", lse_b)
        sink_b = jnp.broadcast_to(c_sinks[l][:, None, None], (C_HEADS, 8, BLK))
        mixed, mixed_bf, lse_c = _attn_fwd(f"window_fwd{l}", proj, tables, sink_b, mixed, mixed_bf, n_heads=C_HEADS,
                                           rep=C_HEADS // C_KV_HEADS, q0=QC0, k0=KC0, v0=VC0, m0=A_HEADS + B_HEADS,
                                           patterns=C_PATTERNS)
        wo = arrived(l, "wo", lse_c)
        y1 = _fwd_rowsharded(f"wout{l}", mixed_bf, wo, OUT_SHARD)
        x1, x1_bf = _ln_fwd(f"ln1_fwd{l}", h, y1, ln1_g[l], ln1_b[l])
        wg = arrived(l, "wg", x1)
        g = _fwd_colsharded(f"gate{l}", x1_bf, wg, BF16)
        wu = arrived(l, "wu", g)
        u = _fwd_colsharded(f"up{l}", x1_bf, wu, BF16)
        hh = _conv_gate_fwd(f"conv_fwd{l}", g, u, cw_full[l], conv_b[l])
        wd = arrived(l, "wd", hh)
        y2 = _fwd_rowsharded(f"down{l}", hh, wd, FF_SHARD)
        x2, x2_bf = _ln_fwd(f"ln2_fwd{l}", x1, y2, ln2_g[l], ln2_b[l])
        weights.append(dict(wi=wi, wo=wo, wg=wg, wu=wu, wd=wd))
        saved.append((h, h_bf, proj, raw, lse_b, sink_b, lse_c, mixed, mixed_bf, y1, x1, x1_bf, g, u, hh, y2))
        h, h_bf = x2, x2_bf

    dy, loss_part = _loss_head(h, loss_target[0])

    d_res, d_path = None, dy
    small = [None] * DEPTH
    mat_grads = [None] * DEPTH
    late = {}
    prev_ffn = prev_mix_swap = None
    for l in reversed(range(DEPTH)):
        h_in, h_in_bf, proj, raw, lse_b, sink_b, lse_c, mixed, mixed_bf, y1, x1, x1_bf, g, u, hh, y2 = saved[l]
        wi, wo, wg, wu, wd = (weights[l][k] for k in ("wi", "wo", "wg", "wu", "wd"))
        dz2, dz2_bf, d_ln2g, d_ln2b = _ln_bwd(f"ln2_bwd{l}", x1, y2, ln2_g[l], d_res, d_path,
                                              run_after=prev_mix_swap[2][0] if prev_mix_swap else None)
        dhh = _bwd_act_rowsharded(f"down_dx{l}", dz2_bf, wd, FF_SHARD, BF16)
        prev_mix = _rs_scatter(f"{l + 1}m", prev_mix_swap, dhh, c_idx) if prev_mix_swap else None
        d_wd = _bwd_w_rowsharded(f"down_dw{l}", hh, dz2_bf, FF_SHARD)
        dg, du, d_cw, d_cb = _conv_gate_bwd(f"conv_bwd{l}", g, u, cw_full[l], conv_b[l], dhh,
                                            run_after=prev_mix[2][0] if prev_mix else None)
        dx1 = _bwd_act_colsharded(f"gateup_dx{l}", [(dg, wg), (du, wu)])
        d_wg = _bwd_w_colsharded(f"gate_dw{l}", x1_bf, dg)
        d_wu = _bwd_w_colsharded(f"up_dw{l}", x1_bf, du)
        pins = ()
        if prev_ffn:
            late[l + 1] = [_rs_reduce(f"{l + 1}f", prev_ffn, d_wu, kc_idx)]
        ffn_swap = _rs_swap(f"{l}f", [d_wg, d_wu, d_wd])
        dz1, dz1_bf, d_ln1g, d_ln1b = _ln_bwd(f"ln1_bwd{l}", h_in, y1, ln1_g[l], dz2, dx1, run_after=ffn_swap[2][0])
        dmix = _bwd_act_rowsharded(f"wout_dx{l}", dz1_bf, wo, OUT_SHARD)
        d_wo = _bwd_w_rowsharded(f"wout_dw{l}", mixed_bf, dz1_bf, OUT_SHARD)
        if prev_mix:
            late[l + 1].append(_rs_reduce(f"{l + 1}m", prev_mix, d_wo, kc_idx))
            pins = tuple(g[2][0] for g in late[l + 1])
        s_ffn = _rs_scatter(f"{l}f", ffn_swap, (d_wo,) + pins, c_idx)
        dproj, d_nw, d_lb = _hgrn_bwd(f"hgrn_bwd{l}", proj, raw, dmix, lb_logits, a_norm_w[l], l, run_after=s_ffn[2][0])
        dproj, _ = _attn_bwd(f"dilated_bwd{l}", proj, mixed, dmix, lse_b, tables, None, dproj, n_kv=B_HEADS, rep=1,
                             q0=QB0, k0=KB0, v0=VB0, m0=A_HEADS, patterns=B_PATTERNS)
        dproj, d_sink = _attn_bwd(f"window_bwd{l}", proj, mixed, dmix, lse_c, tables, sink_b, dproj, n_kv=C_KV_HEADS,
                                  rep=C_HEADS // C_KV_HEADS, q0=QC0, k0=KC0, v0=VC0, m0=A_HEADS + B_HEADS,
                                  patterns=C_PATTERNS)
        dxp = _bwd_act_colsharded_full(f"proj_dx{l}", dproj, wi)
        d_wi = _bwd_w_colsharded(f"proj_dw{l}", h_in_bf, dproj)
        d_res, d_path = dz1, dxp
        prev_ffn, prev_mix_swap = s_ffn, _rs_swap(f"{l}m", [d_wi, d_wo])
        small[l] = (d_lb, d_nw.reshape(A_HEADS, 8, BLK)[:, 0].sum(0), d_sink[:, 0, 0], d_ln1g[0], d_ln1b[0],
                    d_cw, d_cb[0], d_ln2g[0], d_ln2b[0])
    grad_x2 = _axpy("grad_x", d_res, d_path)
    grad_x = grad_x2[None]

    g_lb = small[0][0] + small[1][0]
    per_layer = [jnp.stack([small[0][i], small[1][i]]) for i in range(1, 9)]
    small_shapes = [(DEPTH, 4 * BLK), (DEPTH, BLK), (DEPTH, C_HEADS), (DEPTH, D_MODEL), (DEPTH, D_MODEL),
                    (DEPTH, CONV_WIDTH, D_FF), (DEPTH, D_FF), (DEPTH, D_MODEL), (DEPTH, D_MODEL), (BLK,)]
    rows = _rows_for(small_shapes)
    total = _sum_small(_gather_small(_pack([g_lb] + per_layer + [loss_part[0]], rows), prev_mix_swap[2][0]))
    g_lb, g_nw, g_sink, g_ln1g, g_ln1b, g_cw_full, g_cb, g_ln2g, g_ln2b, loss_row = _unpack(total, small_shapes)
    loss = loss_row[0]
    g_cw = lax.dynamic_slice_in_dim(g_cw_full, k_me * FF_SHARD, FF_SHARD, axis=2)

    sw = [lb_logits, a_norm_w, c_sinks, ln1_g, ln1_b, conv_w, conv_b, ln2_g, ln2_b]
    sg = [g_lb, g_nw, g_sink, g_ln1g, g_ln1b, g_cw, g_cb, g_ln2g, g_ln2b]
    sm = [m_lb_logits, m_a_norm_w, m_c_sinks, m_ln1_g, m_ln1_b, m_conv_w, m_conv_b, m_ln2_g, m_ln2_b]
    sv = [v_lb_logits, v_a_norm_w, v_c_sinks, v_ln1_g, v_ln1_b, v_conv_w, v_conv_b, v_ln2_g, v_ln2_b]
    shapes = [a.shape for a in sw]
    prow = _rows_for(shapes)
    sd, snm, snv = (_unpack(b, shapes) for b in _adamw_small(_pack(sw, prow), _pack(sg, prow), _pack(sm, prow), _pack(sv, prow)))

    names = ["w_in", "w_out", "w_gate", "w_up", "w_down"]
    mw = [w_in, w_out, w_gate, w_up, w_down]
    mm = [m_w_in, m_w_out, m_w_gate, m_w_up, m_w_down]
    mv = [v_w_in, v_w_out, v_w_gate, v_w_up, v_w_down]
    res = [None] * 5
    s_mix = _rs_scatter("0m", prev_mix_swap, total, c_idx)
    for l, (g_ffn, g_mix) in late.items():
        g_wg, g_wu, g_wd = _rs_finish(f"{l}f", g_ffn, s_mix[2][0])
        g_wi, g_wo = _rs_finish(f"{l}m", g_mix, s_mix[2][0])
        mat_grads[l] = [g_wi, g_wo, g_wg, g_wu, g_wd]
    ffn0 = _rs_finish("0f", _rs_reduce("0f", prev_ffn, s_mix[2][0], kc_idx), s_mix[2][0])
    for i, g0 in zip((2, 3, 4), ffn0):
        res[i] = _adamw_matrix(f"adamw_{names[i]}", mw[i], [g0, mat_grads[1][i]], mm[i], mv[i])
    mix0 = _rs_finish("0m", _rs_reduce("0m", s_mix, res[4][1], kc_idx), res[4][1])
    for i, g0 in zip((0, 1), mix0):
        res[i] = _adamw_matrix(f"adamw_{names[i]}", mw[i], [g0, mat_grads[1][i]], mm[i], mv[i])
    mg, md, mnm, mnv = ([r[j] for r in res] for j in range(4))

    def ordered(mat, sm_):
        return [mat[0], sm_[0], sm_[1], sm_[2], mat[1], sm_[3], sm_[4], mat[2], mat[3], sm_[5], sm_[6], mat[4], sm_[7], sm_[8]]

    return (loss, grad_x, *ordered(mg, sg), *ordered(md, sd), *ordered(mnm, snm), *ordered(mnv, snv))
```

```python
import functools
import math

import jax
import jax.numpy as jnp
from jax import lax
from jax.experimental import pallas as pl
from jax.experimental.pallas import tpu as pltpu

F32 = jnp.float32
BF16 = jnp.bfloat16

D_MODEL = 2048
SEQ = 2048
DEPTH = 2
HEAD_DIM = 128
A_HEADS = 4
B_HEADS = 6
C_HEADS = 6
C_KV_HEADS = 2
A_CHUNK = 16
DILATED_PATTERNS = ((128, 1), (512, 4), (2048, 16))
C_WINDOW = 128
ROPE_THETA = 500000.0
ROPE_DIM = HEAD_DIM // 4
D_FF = 5632
CONV_WIDTH = 3
LN_EPS = 1e-5
ALPHA = (2 * DEPTH) ** 0.25
IN_WIDTH = 5632
MIX_WIDTH = 2048
ADAM_LR = 0.001
ADAM_B1 = 0.9
ADAM_B2 = 0.999
ADAM_EPS = 1e-08
ADAM_WD = 0.01
ADAM_STEP = 10

N_CHIPS = 4
N_DEV = 8
FF_SHARD = D_FF // N_CHIPS
OUT_SHARD = MIX_WIDTH // N_CHIPS
BLK = 128
N_CHUNK = SEQ // A_CHUNK
SLAB = 32

QA0, FA0, IA0, GA0 = 0, 4, 8, 12
QB0, KB0, VB0 = 16, 22, 28
QC0, KC0, VC0 = 34, 40, 42

VMEM_LIMIT_V7X = 56 * 1024 * 1024
HI = lax.Precision.HIGHEST
MESH = pl.DeviceIdType.MESH


def _cp(sem=None, vmem=VMEM_LIMIT_V7X, **kw):
    return pltpu.CompilerParams(dimension_semantics=sem, vmem_limit_bytes=vmem, **kw)


def _sigmoid(x):
    return 1.0 / (1.0 + jnp.exp(-x))


def _gate_sigmoid(x):
    return 0.5 * jnp.tanh(0.5 * x) + 0.5


def _mm(name, pairs, dims, grid, a_specs, b_specs, out_spec, out_shape, nk=1, acc_shape=None):
    n_pairs = len(pairs)

    def body(*refs):
        o_ref = refs[2 * n_pairs]
        part = None
        for p in range(n_pairs):
            a = refs[2 * p][...].astype(BF16)
            b = refs[2 * p + 1][...].astype(BF16)
            t = lax.dot_general(a, b, dims, preferred_element_type=F32)
            part = t if part is None else part + t
        if nk == 1:
            o_ref[...] = part.astype(o_ref.dtype)
        else:
            acc = refs[2 * n_pairs + 1]
            k = pl.program_id(len(grid) - 1)

            @pl.when(k == 0)
            def _():
                acc[...] = part

            @pl.when(k > 0)
            def _():
                acc[...] += part

            @pl.when(k == nk - 1)
            def _():
                o_ref[...] = acc[...].astype(o_ref.dtype)

    in_specs, args = [], []
    for (a, b), sa, sb in zip(pairs, a_specs, b_specs):
        in_specs += [sa, sb]
        args += [a, b]
    sem = ("parallel",) * (len(grid) - (1 if nk > 1 else 0)) + (("arbitrary",) if nk > 1 else ())
    return pl.pallas_call(
        body, name=name, grid=grid, in_specs=in_specs, out_specs=out_spec, out_shape=out_shape,
        scratch_shapes=[pltpu.VMEM(acc_shape, F32)] if nk > 1 else [],
        compiler_params=_cp(sem),
    )(*args)


NN = (((1,), (0,)), ((), ()))
NT = (((1,), (1,)), ((), ()))
TN = (((0,), (0,)), ((), ()))
TM = 1024


def _fwd_colsharded(name, x, w_stk, out_dtype=F32):
    return _mm(name, [(x, w_stk)], NN, (N_CHIPS, SEQ // TM),
               [pl.BlockSpec((TM, D_MODEL), lambda j, i: (i, 0))],
               [pl.BlockSpec((None, D_MODEL, FF_SHARD), lambda j, i: (j, 0, 0))],
               pl.BlockSpec((TM, FF_SHARD), lambda j, i: (i, j)),
               jax.ShapeDtypeStruct((SEQ, D_FF), out_dtype))


def _fwd_rowsharded(name, a, w_stk, shard):
    tn = 512
    rows = N_CHIPS * shard
    return _mm(name, [(a, w_stk.reshape(rows, D_MODEL))], NN, (SEQ // TM, D_MODEL // tn),
               [pl.BlockSpec((TM, rows), lambda i, j: (i, 0))],
               [pl.BlockSpec((rows, tn), lambda i, j: (0, j))],
               pl.BlockSpec((TM, tn), lambda i, j: (i, j)),
               jax.ShapeDtypeStruct((SEQ, D_MODEL), F32))


def _bwd_act_colsharded(name, pairs):
    tn = 1024
    n = len(pairs)
    return _mm(name, pairs, NT, (SEQ // TM, D_MODEL // tn, N_CHIPS),
               [pl.BlockSpec((TM, FF_SHARD), lambda i, j, k: (i, k))] * n,
               [pl.BlockSpec((None, tn, FF_SHARD), lambda i, j, k: (k, j, 0))] * n,
               pl.BlockSpec((TM, tn), lambda i, j, k: (i, j)),
               jax.ShapeDtypeStruct((SEQ, D_MODEL), F32), nk=N_CHIPS, acc_shape=(TM, tn))


def _bwd_act_colsharded_full(name, dy, w_stk):
    tn = 512

    def body(a_ref, w_ref, o_ref):
        acc = None
        for k in range(N_CHIPS):
            t = lax.dot_general(a_ref[:, k * FF_SHARD:(k + 1) * FF_SHARD], w_ref[k], NT, preferred_element_type=F32)
            acc = t if acc is None else acc + t
        o_ref[...] = acc

    return pl.pallas_call(
        body, name=name, grid=(SEQ // TM, D_MODEL // tn),
        in_specs=[pl.BlockSpec((TM, D_FF), lambda i, j: (i, 0)),
                  pl.BlockSpec((N_CHIPS, tn, FF_SHARD), lambda i, j: (0, j, 0))],
        out_specs=pl.BlockSpec((TM, tn), lambda i, j: (i, j)),
        out_shape=jax.ShapeDtypeStruct((SEQ, D_MODEL), F32),
        compiler_params=_cp(("parallel", "parallel")))(dy, w_stk)


def _bwd_act_rowsharded(name, dy, w_stk, shard, out_dtype=F32):
    return _mm(name, [(dy, w_stk)], NT, (N_CHIPS, SEQ // TM),
               [pl.BlockSpec((TM, D_MODEL), lambda j, i: (i, 0))],
               [pl.BlockSpec((None, shard, D_MODEL), lambda j, i: (j, 0, 0))],
               pl.BlockSpec((TM, shard), lambda j, i: (i, j)),
               jax.ShapeDtypeStruct((SEQ, N_CHIPS * shard), out_dtype))


def _bwd_w_colsharded(name, x, dy):
    tm = 1024
    return _mm(name, [(x, dy)], TN, (N_CHIPS, D_MODEL // tm),
               [pl.BlockSpec((SEQ, tm), lambda j, i: (0, i))],
               [pl.BlockSpec((SEQ, FF_SHARD), lambda j, i: (0, j))],
               pl.BlockSpec((None, tm, FF_SHARD), lambda j, i: (j, i, 0)),
               jax.ShapeDtypeStruct((N_CHIPS, D_MODEL, FF_SHARD), BF16))


def _bwd_w_rowsharded(name, a, dy, shard):
    tn = 1024
    return _mm(name, [(a, dy)], TN, (N_CHIPS, D_MODEL // tn),
               [pl.BlockSpec((SEQ, shard), lambda j, i: (0, j))],
               [pl.BlockSpec((SEQ, tn), lambda j, i: (0, i))],
               pl.BlockSpec((None, shard, tn), lambda j, i: (j, 0, i)),
               jax.ShapeDtypeStruct((N_CHIPS, shard, D_MODEL), BF16))


TR = 256


def _ln_fwd(name, x, y, g, b):
    def body(x_ref, y_ref, g_ref, b_ref, o_ref, ob_ref):
        z = ALPHA * x_ref[...] + y_ref[...]
        mu = jnp.mean(z, -1, keepdims=True)
        zc = z - mu
        var = jnp.mean(zc * zc, -1, keepdims=True)
        o = zc * lax.rsqrt(var + LN_EPS) * g_ref[...] + b_ref[...]
        o_ref[...] = o
        ob_ref[...] = o.astype(BF16)

    row = pl.BlockSpec((TR, D_MODEL), lambda i: (i, 0))
    vec = pl.BlockSpec((1, D_MODEL), lambda i: (0, 0))
    return pl.pallas_call(body, name=name, grid=(SEQ // TR,), in_specs=[row, row, vec, vec], out_specs=[row, row],
                          out_shape=[jax.ShapeDtypeStruct((SEQ, D_MODEL), F32), jax.ShapeDtypeStruct((SEQ, D_MODEL), BF16)],
                          compiler_params=_cp(("parallel",)))(x, y, g.reshape(1, -1), b.reshape(1, -1))


def _to_bf16(name, x, run_after):
    def body(x_ref, _, o_ref):
        o_ref[...] = x_ref[...].astype(BF16)

    row = pl.BlockSpec((TR, D_MODEL), lambda i: (i, 0))
    return pl.pallas_call(body, name=name, grid=(SEQ // TR,), in_specs=[row, pl.BlockSpec(memory_space=pl.ANY)],
                          out_specs=row, out_shape=jax.ShapeDtypeStruct((SEQ, D_MODEL), BF16),
                          compiler_params=_cp(("parallel",)))(x, run_after)


def _ln_bwd(name, x, y, g, d_res, d_path, run_after=None):
    has_res = d_res is not None
    n_in = 4 + has_res + (run_after is not None)

    def body(*refs):
        dz_ref, dzb_ref, dg_ref, db_ref = refs[n_in:]
        if has_res:
            x_ref, y_ref, g_ref, r_ref, p_ref = refs[:5]
            dout = ALPHA * r_ref[...] + p_ref[...]
        else:
            x_ref, y_ref, g_ref, p_ref = refs[:4]
            dout = p_ref[...]
        z = ALPHA * x_ref[...] + y_ref[...]
        mu = jnp.mean(z, -1, keepdims=True)
        zc = z - mu
        rstd = lax.rsqrt(jnp.mean(zc * zc, -1, keepdims=True) + LN_EPS)
        zh = zc * rstd
        dzh = dout * g_ref[...]
        dz = rstd * (dzh - jnp.mean(dzh, -1, keepdims=True) - zh * jnp.mean(dzh * zh, -1, keepdims=True))
        dz_ref[...] = dz
        dzb_ref[...] = dz.astype(BF16)
        pg = jnp.sum(dout * zh, 0, keepdims=True)
        pb = jnp.sum(dout, 0, keepdims=True)

        @pl.when(pl.program_id(0) == 0)
        def _():
            dg_ref[...] = pg
            db_ref[...] = pb

        @pl.when(pl.program_id(0) > 0)
        def _():
            dg_ref[...] += pg
            db_ref[...] += pb

    row = pl.BlockSpec((TR, D_MODEL), lambda i: (i, 0))
    vec = pl.BlockSpec((1, D_MODEL), lambda i: (0, 0))
    args = [x, y, g.reshape(1, -1)] + ([d_res] if has_res else []) + [d_path]
    in_specs = [row, row, vec] + ([row] if has_res else []) + [row]
    if run_after is not None:
        args.append(run_after)
        in_specs.append(pl.BlockSpec(memory_space=pl.ANY))
    vshape = jax.ShapeDtypeStruct((1, D_MODEL), F32)
    return pl.pallas_call(body, name=name, grid=(SEQ // TR,), in_specs=in_specs, out_specs=[row, row, vec, vec],
                          out_shape=[jax.ShapeDtypeStruct((SEQ, D_MODEL), F32), jax.ShapeDtypeStruct((SEQ, D_MODEL), BF16),
                                     vshape, vshape],
                          compiler_params=_cp(("arbitrary",)))(*args)


def _loss_head(y, target):
    def body(y_ref, t_ref, dy_ref, l_ref):
        e = y_ref[...] - t_ref[...]
        dy_ref[...] = e * (1.0 / D_MODEL)
        part = jnp.full((8, BLK), 0.5 / D_MODEL * jnp.sum(e * e), F32)

        @pl.when(pl.program_id(0) == 0)
        def _():
            l_ref[...] = part

        @pl.when(pl.program_id(0) > 0)
        def _():
            l_ref[...] += part

    row = pl.BlockSpec((TR, D_MODEL), lambda i: (i, 0))
    return pl.pallas_call(body, name="loss_head", grid=(SEQ // TR,), in_specs=[row, row],
                          out_specs=[row, pl.BlockSpec((8, BLK), lambda i: (0, 0))],
                          out_shape=[jax.ShapeDtypeStruct((SEQ, D_MODEL), F32), jax.ShapeDtypeStruct((8, BLK), F32)],
                          compiler_params=_cp(("arbitrary",)))(y, target)


def _axpy(name, a, b):
    def body(a_ref, b_ref, o_ref):
        o_ref[...] = ALPHA * a_ref[...] + b_ref[...]

    row = pl.BlockSpec((TR, D_MODEL), lambda i: (i, 0))
    return pl.pallas_call(body, name=name, grid=(SEQ // TR,), in_specs=[row, row], out_specs=row,
                          out_shape=jax.ShapeDtypeStruct((SEQ, D_MODEL), F32),
                          compiler_params=_cp(("parallel",)))(a, b)


TC = 512


def _shift_down(x, s, rows):
    if s == 0:
        return x
    return jnp.where(rows >= s, pltpu.roll(x, s, axis=0), 0.0)


def _shift_up(x, s, rows):
    if s == 0:
        return x
    return jnp.where(rows < SEQ - s, pltpu.roll(x, SEQ - s, axis=0), 0.0)


def _conv_gate_fwd(name, g, u, cw, cb):
    def body(g_ref, u_ref, w_ref, b_ref, h_ref):
        gg = g_ref[...].astype(F32)
        rows = lax.broadcasted_iota(jnp.int32, gg.shape, 0)
        gc = b_ref[...] + w_ref[2:3, :] * gg
        gc = gc + w_ref[1:2, :] * _shift_down(gg, 1, rows)
        gc = gc + w_ref[0:1, :] * _shift_down(gg, 2, rows)
        h_ref[...] = (gc * _gate_sigmoid(gc) * u_ref[...].astype(F32)).astype(BF16)

    col = pl.BlockSpec((SEQ, TC), lambda j: (0, j))
    return pl.pallas_call(body, name=name, grid=(D_FF // TC,),
                          in_specs=[col, col, pl.BlockSpec((CONV_WIDTH, TC), lambda j: (0, j)),
                                    pl.BlockSpec((1, TC), lambda j: (0, j))],
                          out_specs=col, out_shape=jax.ShapeDtypeStruct((SEQ, D_FF), BF16),
                          compiler_params=_cp(("parallel",)))(g, u, cw, cb.reshape(1, -1))


def _conv_gate_bwd(name, g, u, cw, cb, dh, run_after=None):
    def body(g_ref, u_ref, w_ref, b_ref, dh_ref, *rest):
        dg_ref, du_ref, dw_ref, db_ref = rest[-4:]
        gg = g_ref[...].astype(F32)
        rows = lax.broadcasted_iota(jnp.int32, gg.shape, 0)
        g1 = _shift_down(gg, 1, rows)
        g2 = _shift_down(gg, 2, rows)
        gc = b_ref[...] + w_ref[2:3, :] * gg + w_ref[1:2, :] * g1 + w_ref[0:1, :] * g2
        sg = _gate_sigmoid(gc)
        act = gc * sg
        dh = dh_ref[...].astype(F32)
        du_ref[...] = (dh * act).astype(BF16)
        dgc = dh * u_ref[...].astype(F32) * (sg * (1.0 + gc * (1.0 - sg)))
        db_ref[...] = jnp.sum(dgc, 0, keepdims=True)
        dw_ref[2:3, :] = jnp.sum(dgc * gg, 0, keepdims=True)
        dw_ref[1:2, :] = jnp.sum(dgc * g1, 0, keepdims=True)
        dw_ref[0:1, :] = jnp.sum(dgc * g2, 0, keepdims=True)
        dg_ref[...] = (w_ref[2:3, :] * dgc + w_ref[1:2, :] * _shift_up(dgc, 1, rows)
                       + w_ref[0:1, :] * _shift_up(dgc, 2, rows)).astype(BF16)

    col = pl.BlockSpec((SEQ, TC), lambda j: (0, j))
    w3 = pl.BlockSpec((CONV_WIDTH, TC), lambda j: (0, j))
    w1 = pl.BlockSpec((1, TC), lambda j: (0, j))
    big = jax.ShapeDtypeStruct((SEQ, D_FF), BF16)
    extra = [] if run_after is None else [run_after]
    return pl.pallas_call(body, name=name, grid=(D_FF // TC,),
                          in_specs=[col, col, w3, w1, col] + [pl.BlockSpec(memory_space=pl.ANY)] * len(extra),
                          out_specs=[col, col, w3, w1],
                          out_shape=[big, big, jax.ShapeDtypeStruct((CONV_WIDTH, D_FF), F32),
                                     jax.ShapeDtypeStruct((1, D_FF), F32)],
                          compiler_params=_cp(("parallel",)))(g, u, cw, cb.reshape(1, -1), dh, *extra)


def _lbs_of(logits, layer):
    m = jnp.max(logits, 0, keepdims=True)
    e = jnp.exp(logits - m)
    p = e / jnp.sum(e, 0, keepdims=True)
    lb = jnp.zeros((1, BLK), F32)
    for r in range(1, layer + 1):
        lb = lb + p[r:r + 1, :]
    return lb, p


def _dlogits_of(p, dlb, layer):
    rows = lax.broadcasted_iota(jnp.int32, p.shape, 0)
    dp = jnp.where((rows >= 1) & (rows <= layer), dlb, 0.0)
    return p * (dp - jnp.sum(p * dp, 0, keepdims=True))


SROWS = SLAB * A_CHUNK
N_SLAB = N_CHUNK // SLAB


def _chunk_prefix(x, rowi):
    for s in (1, 2, 4, 8):
        x = x + jnp.where(rowi >= s, pltpu.roll(x, s, axis=0), 0.0)
    return x


def _chunk_suffix(x, rowi):
    for s in (1, 2, 4, 8):
        x = x + jnp.where(rowi < A_CHUNK - s, pltpu.roll(x, SROWS - s, axis=0), 0.0)
    return x


def _c3(x):
    return x.reshape(SLAB, A_CHUNK, BLK)


def _c2(x):
    return x.reshape(SROWS, BLK)


def _split(x):
    top = lax.bitcast_convert_type(lax.bitcast_convert_type(x, jnp.uint32) & jnp.uint32(0xFFFF0000), F32)
    return top.astype(BF16), (x - top).astype(BF16)


def _bmm(eq, a, b):
    ah, al = _split(a)
    bh, bl = _split(b)

    def mm(u, v):
        return jnp.einsum(eq, u, v, preferred_element_type=F32)

    return mm(ah, bh) + (mm(ah, bl) + mm(al, bh))


def _bmm_1pass(eq, a, b):
    return jnp.einsum(eq, a.astype(BF16), b.astype(BF16), preferred_element_type=F32)


def _slab_rows(s):
    return pl.ds(s * SROWS, SROWS)


def _hgrn_prep(q, f, lb):
    rowi = lax.broadcasted_iota(jnp.int32, (SROWS, BLK), 0) & (A_CHUNK - 1)
    sq = _gate_sigmoid(q)
    qc = q * sq
    sf = _sigmoid(f)
    fg = lb + (1.0 - lb) * sf
    kc = 1.0 - fg
    b = _chunk_prefix(jnp.log(fg), rowi)
    b3 = _c3(b)
    blast = b3[:, A_CHUNK - 1:A_CHUNK, :]
    eb = jnp.exp(b)
    ekb = _c2(jnp.exp(blast - b3))
    dec = jnp.exp(blast.reshape(SLAB, BLK))
    return rowi, sq, qc, sf, fg, kc, b, eb, ekb, dec


def _hgrn_slab_states(s, carry, v, ke, dec, dec_ref, u_ref, st_ref):
    dec_ref[pl.ds(s * SLAB, SLAB), :] = dec
    u_ref[...] = _bmm('ncv,nck->nvk', _c3(v), _c3(ke))

    def step(j, c):
        st_ref[j] = c
        return dec_ref[pl.ds(s * SLAB + j, 1), :] * c + u_ref[j]

    return lax.fori_loop(0, SLAB, step, carry)


def _hgrn_fwd(name, proj, lb_logits, nw, layer):
    def body(q_ref, f_ref, i_ref, g_ref, lg_ref, nw_ref, out_ref, outb_ref, raw_ref, dec_ref, u_ref, st_ref):
        lb, _ = _lbs_of(lg_ref[...], layer)
        ones = jnp.ones((BLK, BLK), BF16)
        carry = jnp.zeros((BLK, BLK), F32)
        for s in range(N_SLAB):
            rows = _slab_rows(s)
            v = i_ref[rows, :]
            rowi, sq, qc, sf, fg, kc, b, eb, ekb, dec = _hgrn_prep(q_ref[rows, :], f_ref[rows, :], lb)
            carry = _hgrn_slab_states(s, carry, v, kc * ekb, dec, dec_ref, u_ref, st_ref)
            o = _c2(_bmm('nck,nvk->ncv', _c3(qc * eb), st_ref[...]))
            qc3, kc3, b3, v3, row3 = _c3(qc), _c3(kc), _c3(b), _c3(v), _c3(rowi)
            col = lax.broadcasted_iota(jnp.int32, (SLAB, A_CHUNK, A_CHUNK), 2)
            att_all = jnp.zeros((SLAB, A_CHUNK, A_CHUNK), F32)
            for j in range(A_CHUNK):
                dj = jnp.exp(jnp.where(row3 >= j, b3 - b3[:, j:j + 1, :], -jnp.inf))
                a = jnp.dot(_c2(qc3 * dj * kc3[:, j:j + 1, :]).astype(BF16), ones, preferred_element_type=F32)
                att_all = jnp.where(col == j, _c3(a)[:, :, :A_CHUNK], att_all)
            o = o + _c2(_bmm_1pass('nij,njv->niv', att_all, v3))
            raw_ref[rows, :] = o
            r = lax.rsqrt(jnp.mean(o * o, -1, keepdims=True) + LN_EPS)
            gg = g_ref[rows, :]
            gated = o * r * nw_ref[...] * (gg * _gate_sigmoid(gg))
            out_ref[rows, :] = gated
            outb_ref[rows, :] = gated.astype(BF16)

    def colblk(c0):
        return pl.BlockSpec((SEQ, BLK), lambda h: (0, c0 + h))

    return pl.pallas_call(
        body, name=name, grid=(A_HEADS,),
        in_specs=[colblk(QA0), colblk(FA0), colblk(IA0), colblk(GA0),
                  pl.BlockSpec((DEPTH, BLK), lambda h: (0, h)), pl.BlockSpec((1, BLK), lambda h: (0, 0))],
        out_specs=[colblk(0), colblk(0), colblk(0)],
        out_shape=[jax.ShapeDtypeStruct((SEQ, MIX_WIDTH), F32), jax.ShapeDtypeStruct((SEQ, MIX_WIDTH), BF16),
                   jax.ShapeDtypeStruct((SEQ, A_HEADS * BLK), F32)],
        scratch_shapes=[pltpu.VMEM((N_CHUNK, BLK), F32), pltpu.VMEM((SLAB, BLK, BLK), F32),
                        pltpu.VMEM((SLAB, BLK, BLK), F32)],
        compiler_params=_cp(("parallel",)))(proj, proj, proj, proj, lb_logits, nw.reshape(1, -1))


def _col_block_copies(stage, sems, dst, col_blocks, first):
    return [pltpu.make_async_copy(stage.at[first + t], dst.at[:, pl.ds(pl.multiple_of(cb * BLK, BLK), BLK)],
                                  sems.at[first + t]) for t, cb in enumerate(col_blocks)]


def _start_col_blocks(stage, sems, dst, col_blocks, first=0):
    for cp in _col_block_copies(stage, sems, dst, col_blocks, first):
        cp.start()


def _wait_col_blocks(stage, sems, dst, count, first=0):
    for cp in _col_block_copies(stage, sems, dst, [0] * count, first):
        cp.wait()


def _hgrn_bwd(name, proj, raw, dmix, lb_logits, nw, layer, run_after=None):
    extra = [] if run_after is None else [run_after]

    def body(q_ref, f_ref, i_ref, g_ref, raw_ref, do_ref, lg_ref, nw_ref, *rest):
        (dproj_ref, dnw_ref, dlg_ref,
         dec_ref, u_ref, st_ref, h_ref, dbs_ref, dkc_ref, tot_ref, stage, stage_sem) = rest[-12:]
        dq_ref, df_ref, di_ref, dg_ref = (stage.at[t] for t in range(4))
        lb, p = _lbs_of(lg_ref[...], layer)
        ones = jnp.ones((BLK, BLK), BF16)
        nwv = nw_ref[...]

        carry = jnp.zeros((BLK, BLK), F32)
        for s in range(N_SLAB):
            rows = _slab_rows(s)
            rowi, sq, qc, sf, fg, kc, b, eb, ekb, dec = _hgrn_prep(q_ref[rows, :], f_ref[rows, :], lb)
            carry = _hgrn_slab_states(s, carry, i_ref[rows, :], kc * ekb, dec, dec_ref, u_ref,
                                      st_ref.at[pl.ds(s * SLAB, SLAB)])

        @pl.when(pl.program_id(0) > 0)
        def _():
            _wait_col_blocks(stage, stage_sem, dproj_ref, 4)

        carry = jnp.zeros((BLK, BLK), F32)
        dnw = jnp.zeros((1, BLK), F32)
        for s in reversed(range(N_SLAB)):
            rows = _slab_rows(s)
            q, v = q_ref[rows, :], i_ref[rows, :]
            rowi, sq, qc, sf, fg, kc, b, eb, ekb, dec = _hgrn_prep(q, f_ref[rows, :], lb)
            ke = kc * ekb
            qe = qc * eb

            o = raw_ref[rows, :]
            gg = g_ref[rows, :]
            sgg = _gate_sigmoid(gg)
            dout = do_ref[rows, :]
            r = lax.rsqrt(jnp.mean(o * o, -1, keepdims=True) + LN_EPS)
            oh = o * r
            dg_ref[rows, :] = (dout * oh * nwv * (sgg * (1.0 + gg * (1.0 - sgg)))).astype(BF16)
            dn = dout * (gg * sgg)
            dnw = dnw + jnp.sum(dn * oh, 0, keepdims=True)
            doh = dn * nwv
            do = r * (doh - oh * jnp.mean(doh * oh, -1, keepdims=True))
            do3, qe3, v3, ke3 = _c3(do), _c3(qe), _c3(v), _c3(ke)

            u_ref[...] = _bmm('ncv,nck->nvk', do3, qe3)

            def step(jj, c, s=s):
                j = SLAB - 1 - jj
                h_ref[j] = c
                return u_ref[j] + dec_ref[pl.ds(s * SLAB + j, 1), :] * c

            carry = lax.fori_loop(0, SLAB, step, carry)

            hh = h_ref[...]
            dqc = _c2(_bmm('ncv,nvk->nck', do3, st_ref[pl.ds(s * SLAB, SLAB)])) * eb
            dkc = _c2(_bmm('ncv,nvk->nck', v3, hh)) * ekb
            dv = _c2(_bmm_1pass('nck,nvk->ncv', ke3, hh))

            qc3, kc3, b3, row3 = _c3(qc), _c3(kc), _c3(b), _c3(rowi)
            datt_all = _bmm('niv,njv->nij', do3, v3)
            col = lax.broadcasted_iota(jnp.int32, datt_all.shape, 2)
            att_all = jnp.zeros_like(datt_all)
            for j in range(A_CHUNK):
                dj = jnp.exp(jnp.where(row3 >= j, b3 - b3[:, j:j + 1, :], -jnp.inf))
                kj = kc3[:, j:j + 1, :]
                att = _c3(jnp.dot(_c2(qc3 * dj * kj).astype(BF16), ones, preferred_element_type=F32))
                att_all = jnp.where(col == j, att[:, :, :A_CHUNK], att_all)
                md = dj * datt_all[:, :, j:j + 1]
                dqc = dqc + _c2(md * kj)
                dkc = dkc + _c2(jnp.where(row3 == j, jnp.sum(md * qc3, 1, keepdims=True), 0.0))
            dv = dv + _c2(_bmm_1pass('nij,niv->njv', att_all, do3))
            di_ref[rows, :] = dv.astype(BF16)
            dq_ref[rows, :] = (dqc * (sq * (1.0 + q * (1.0 - sq)))).astype(BF16)

            dbs = _chunk_suffix(qc * dqc - kc * dkc, rowi)
            dbs_ref[rows, :] = dbs
            dkc_ref[rows, :] = dkc
            tot_ref[pl.ds(s * SLAB, SLAB), :] = _c3(dbs)[:, 0:1, :].reshape(SLAB, BLK)
        dnw_ref[...] = jnp.broadcast_to(dnw, (8, BLK))

        rn = lax.broadcasted_iota(jnp.int32, (N_CHUNK, N_CHUNK), 0)
        cn = lax.broadcasted_iota(jnp.int32, (N_CHUNK, N_CHUNK), 1)
        tot_ref[...] = jnp.dot((cn > rn).astype(F32), tot_ref[...], preferred_element_type=F32, precision=HI)
        dlb = jnp.zeros((1, BLK), F32)
        for s in range(N_SLAB):
            rows = _slab_rows(s)
            sf = _sigmoid(f_ref[rows, :])
            fg = lb + (1.0 - lb) * sf
            later = tot_ref[pl.ds(s * SLAB, SLAB), :]
            dlg = _c2(_c3(dbs_ref[rows, :]) + later[:, None, :])
            dfg = dlg / fg - dkc_ref[rows, :]
            df_ref[rows, :] = (dfg * (1.0 - lb) * sf * (1.0 - sf)).astype(BF16)
            dlb = dlb + jnp.sum(dfg * (1.0 - sf), 0, keepdims=True)
        dlg_ref[...] = _dlogits_of(p, dlb, layer)
        _start_col_blocks(stage, stage_sem, dproj_ref, [c0 + pl.program_id(0) for c0 in (QA0, FA0, IA0, GA0)])

        @pl.when(pl.program_id(0) == A_HEADS - 1)
        def _():
            _wait_col_blocks(stage, stage_sem, dproj_ref, 4)

    def colblk(c0):
        return pl.BlockSpec((SEQ, BLK), lambda h: (0, c0 + h))

    return pl.pallas_call(
        body, name=name, grid=(A_HEADS,),
        in_specs=[colblk(QA0), colblk(FA0), colblk(IA0), colblk(GA0), colblk(0), colblk(0),
                  pl.BlockSpec((DEPTH, BLK), lambda h: (0, h)), pl.BlockSpec((1, BLK), lambda h: (0, 0))]
        + [pl.BlockSpec(memory_space=pl.ANY)] * len(extra),
        out_specs=[pl.BlockSpec(memory_space=pl.ANY),
                   pl.BlockSpec((8, BLK), lambda h: (h, 0)), pl.BlockSpec((DEPTH, BLK), lambda h: (0, h))],
        out_shape=[jax.ShapeDtypeStruct((SEQ, IN_WIDTH), BF16), jax.ShapeDtypeStruct((A_HEADS * 8, BLK), F32),
                   jax.ShapeDtypeStruct((DEPTH, A_HEADS * BLK), F32)],
        scratch_shapes=[pltpu.VMEM((N_CHUNK, BLK), F32), pltpu.VMEM((SLAB, BLK, BLK), F32),
                        pltpu.VMEM((N_CHUNK, BLK, BLK), F32), pltpu.VMEM((SLAB, BLK, BLK), F32),
                        pltpu.VMEM((SEQ, BLK), F32), pltpu.VMEM((SEQ, BLK), F32), pltpu.VMEM((N_CHUNK, BLK), F32),
                        pltpu.VMEM((4, SEQ, BLK), BF16), pltpu.SemaphoreType.DMA((4,))],
        compiler_params=_cp(("arbitrary",)))(proj, proj, proj, proj, raw, dmix, lb_logits, nw.reshape(1, -1), *extra)


SCALE = HEAD_DIM ** -0.5


def _rope_tables():
    half = ROPE_DIM // 2
    inv = ROPE_THETA ** (-jnp.arange(0, ROPE_DIM, 2, dtype=F32) / ROPE_DIM)
    ang = jnp.arange(SEQ, dtype=F32)[:, None] * inv[None, :]
    cos, sin = jnp.cos(ang), jnp.sin(ang)
    pad = jnp.zeros((SEQ, HEAD_DIM - ROPE_DIM), F32)
    zero = jnp.zeros((SEQ, half), F32)
    c = jnp.concatenate([cos, cos, pad + 1.0], 1)
    s_lo = jnp.concatenate([zero, sin, pad], 1)
    s_hi = jnp.concatenate([-sin, zero, pad], 1)
    return c, s_lo, s_hi


def _rope(x, c, s_lo, s_hi):
    half = ROPE_DIM // 2
    return x * c + pltpu.roll(x, half, axis=1) * s_lo + pltpu.roll(x, HEAD_DIM - half, axis=1) * s_hi


def _unrope(dy, c, s_lo, s_hi):
    half = ROPE_DIM // 2
    return dy * c + pltpu.roll(dy * s_lo, HEAD_DIM - half, axis=1) + pltpu.roll(dy * s_hi, half, axis=1)


N_BLK = SEQ // BLK


def _block_rows(dil):
    nb = N_BLK // dil
    return [pl.ds(r + n * BLK * dil, BLK, stride=dil) for r in range(dil) for n in range(nb)]


def _to_blocks(ref, dil):
    if dil == 1:
        return ref[...].reshape(N_BLK, BLK, BLK)
    return jnp.stack([ref[rows, :] for rows in _block_rows(dil)], 0)


def _from_blocks(ref, val, dil, add=False):
    if dil == 1:
        flat = val.reshape(SEQ, BLK)
        ref[...] = ref[...] + flat if add else flat
        return
    for b, rows in enumerate(_block_rows(dil)):
        ref[rows, :] = ref[rows, :] + val[b] if add else val[b]


def _prev_block(x):
    return jnp.concatenate([x[:1], x[:-1]], axis=0)


def _to_next_block(x):
    return jnp.concatenate([x[1:], jnp.zeros_like(x[:1])], axis=0)


def _band_masks(max_lag, dil):
    r = lax.broadcasted_iota(jnp.int32, (N_BLK, BLK, BLK), 1)
    c = lax.broadcasted_iota(jnp.int32, (N_BLK, BLK, BLK), 2)
    b = lax.broadcasted_iota(jnp.int32, (N_BLK, BLK, BLK), 0)
    has_prev = (b % (N_BLK // dil)) != 0
    return r >= c, has_prev & (BLK + r - c <= max_lag)


def _bdot(eq, a, b):
    return jnp.einsum(eq, a, b, preferred_element_type=F32)


def _attn_fwd(name, proj, tables, sink_b, mixed, mixed_bf, *, n_heads, rep, q0, k0, v0, m0, patterns):
    n_pat = len(patterns)
    has_sink = sink_b is not None

    def body(*refs):
        o_ref, ob_ref, l_ref, qr, kr, op, lse_ref = refs[-7:]
        q_ref, k_ref, v_ref, c_ref, sl_ref, sh_ref = refs[:6]
        if has_sink:
            sk = refs[6][0:1, 0:1]
        c, s_lo, s_hi = c_ref[...], sl_ref[...], sh_ref[...]
        qr[...] = _rope(q_ref[...], c, s_lo, s_hi)
        kr[...] = _rope(k_ref[...], c, s_lo, s_hi)
        for p, (max_lag, dil) in enumerate(patterns):
            qa = _to_blocks(qr, dil).astype(BF16)
            ka = _to_blocks(kr, dil).astype(BF16)
            va = _to_blocks(v_ref, dil).astype(BF16)
            own, before = _band_masks(max_lag, dil)
            s1 = jnp.where(own, _bdot('nqd,nkd->nqk', qa, ka) * SCALE, -jnp.inf)
            m = jnp.max(s1, -1, keepdims=True)
            with_prev = dil < N_BLK
            if with_prev:
                kp, vp = _prev_block(ka), _prev_block(va)
                s0 = jnp.where(before, _bdot('nqd,nkd->nqk', qa, kp) * SCALE, -jnp.inf)
                m = jnp.maximum(m, jnp.max(s0, -1, keepdims=True))
            if has_sink:
                m = jnp.maximum(m, sk)
            e1 = jnp.exp(s1 - m)
            den = jnp.sum(e1, -1, keepdims=True)
            o = _bdot('nqk,nkd->nqd', e1.astype(BF16), va)
            if with_prev:
                e0 = jnp.exp(s0 - m)
                den = den + jnp.sum(e0, -1, keepdims=True)
                o = o + _bdot('nqk,nkd->nqd', e0.astype(BF16), vp)
            if has_sink:
                den = den + jnp.exp(sk - m)
            _from_blocks(op.at[p], o / den, dil)
            _from_blocks(lse_ref.at[p], jnp.broadcast_to(m + jnp.log(den), (N_BLK, BLK, BLK)), dil)
        if n_pat == 1:
            acc = op[0]
            l_ref[...] = lse_ref[0]
        else:
            ls = [lse_ref[p] for p in range(n_pat)]
            m = functools.reduce(jnp.maximum, ls)
            es = [jnp.exp(l - m) for l in ls]
            tot = functools.reduce(jnp.add, es)
            acc = None
            for p in range(n_pat):
                t = (es[p] / tot) * op[p]
                acc = t if acc is None else acc + t
            l_ref[...] = m + jnp.log(tot)
        o_ref[...] = acc
        ob_ref[...] = acc.astype(BF16)

    def colblk(fn):
        return pl.BlockSpec((SEQ, BLK), fn)

    tab = pl.BlockSpec((SEQ, BLK), lambda h: (0, 0))
    in_specs = [colblk(lambda h: (0, q0 + h)), colblk(lambda h: (0, k0 + h // rep)), colblk(lambda h: (0, v0 + h // rep)),
                tab, tab, tab]
    args = [proj, proj, proj, *tables]
    if has_sink:
        in_specs.append(pl.BlockSpec((None, 8, BLK), lambda h: (h, 0, 0)))
        args.append(sink_b)
    n_in = len(args)
    in_specs += [pl.BlockSpec(memory_space=pl.ANY)] * 2
    args += [mixed, mixed_bf]
    pat = pltpu.VMEM((n_pat, SEQ, BLK), F32)
    return pl.pallas_call(
        body, name=name, grid=(n_heads,), in_specs=in_specs,
        out_specs=[colblk(lambda h: (0, m0 + h)), colblk(lambda h: (0, m0 + h)),
                   pl.BlockSpec((None, SEQ, BLK), lambda h: (h, 0, 0))],
        out_shape=[jax.ShapeDtypeStruct(mixed.shape, F32), jax.ShapeDtypeStruct(mixed.shape, BF16),
                   jax.ShapeDtypeStruct((n_heads, SEQ, BLK), F32)],
        input_output_aliases={n_in: 0, n_in + 1: 1},
        scratch_shapes=[pltpu.VMEM((SEQ, BLK), F32), pltpu.VMEM((SEQ, BLK), F32), pat, pat],
        compiler_params=_cp(("parallel",)))(*args)


def _attn_bwd(name, proj, mixed, dmix, lse, tables, sink_b, dproj, *, n_kv, rep, q0, k0, v0, m0, patterns):
    n_heads = n_kv * rep
    has_sink = sink_b is not None

    def body(*refs):
        q_ref, k_ref, v_ref, o_ref, do_ref, lse_ref, c_ref, sl_ref, sh_ref = refs[:9]
        sink_ref = refs[9] if has_sink else None
        dproj_ref, dsk_ref, qr, kr, dqa, dka, dva, dd, stage, stage_sem = refs[-10:]
        g, j = pl.program_id(0), pl.program_id(1)
        c, s_lo, s_hi = c_ref[...], sl_ref[...], sh_ref[...]
        qr[...] = _rope(q_ref[...], c, s_lo, s_hi)
        kr[...] = _rope(k_ref[...], c, s_lo, s_hi)
        dcol = jnp.sum(do_ref[...] * o_ref[...], -1, keepdims=True)
        dd[...] = jnp.broadcast_to(dcol, (SEQ, BLK))

        @pl.when(j == 0)
        def _():
            dka[...] = jnp.zeros((SEQ, BLK), F32)
            dva[...] = jnp.zeros((SEQ, BLK), F32)

        for p, (max_lag, dil) in enumerate(patterns):
            qa = _to_blocks(qr, dil).astype(BF16)
            ka = _to_blocks(kr, dil).astype(BF16)
            va = _to_blocks(v_ref, dil).astype(BF16)
            doa = _to_blocks(do_ref, dil).astype(BF16)
            lcol = _to_blocks(lse_ref, dil)[:, :, 0:1]
            dcb = _to_blocks(dd, dil)[:, :, 0:1]
            own, before = _band_masks(max_lag, dil)

            def probs_and_ds(kk, vv, valid):
                s = _bdot('nqd,nkd->nqk', qa, kk) * SCALE
                a = jnp.where(valid, jnp.exp(s - lcol), 0.0)
                ds = a * (_bdot('nqd,nkd->nqk', doa, vv) - dcb) * SCALE
                return a.astype(BF16), ds.astype(BF16)

            a1, ds1 = probs_and_ds(ka, va, own)
            dq = _bdot('nqk,nkd->nqd', ds1, ka)
            dk = _bdot('nqk,nqd->nkd', ds1, qa)
            dv = _bdot('nqk,nqd->nkd', a1, doa)
            if dil < N_BLK:
                kp, vp = _prev_block(ka), _prev_block(va)
                a0, ds0 = probs_and_ds(kp, vp, before)
                dq = dq + _bdot('nqk,nkd->nqd', ds0, kp)
                dk = dk + _to_next_block(_bdot('nqk,nqd->nkd', ds0, qa))
                dv = dv + _to_next_block(_bdot('nqk,nqd->nkd', a0, doa))
            _from_blocks(dqa, dq, dil, add=p > 0)
            _from_blocks(dka, dk, dil, add=True)
            _from_blocks(dva, dv, dil, add=True)

        if has_sink:
            sk = sink_ref[0:1, 0:1]
            ps = jnp.exp(sk - lse_ref[...][:, 0:1])
            dsk_ref[...] = jnp.full((8, BLK), -jnp.sum(ps * dcol), F32)
        else:
            dsk_ref[...] = jnp.zeros((8, BLK), F32)
        @pl.when(g * rep + j > 0)
        def _():
            _wait_col_blocks(stage, stage_sem, dproj_ref, 1)

        stage[0] = _unrope(dqa[...], c, s_lo, s_hi).astype(BF16)
        _start_col_blocks(stage, stage_sem, dproj_ref, [q0 + g * rep + j])

        @pl.when(j == rep - 1)
        def _():
            @pl.when(g > 0)
            def _():
                _wait_col_blocks(stage, stage_sem, dproj_ref, 2, first=1)

            stage[1] = _unrope(dka[...], c, s_lo, s_hi).astype(BF16)
            stage[2] = dva[...].astype(BF16)
            _start_col_blocks(stage, stage_sem, dproj_ref, [k0 + g, v0 + g], first=1)

        @pl.when((g == n_kv - 1) & (j == rep - 1))
        def _():
            _wait_col_blocks(stage, stage_sem, dproj_ref, 3)

    def colblk(fn):
        return pl.BlockSpec((SEQ, BLK), fn)

    tab = pl.BlockSpec((SEQ, BLK), lambda g, j: (0, 0))
    in_specs = [colblk(lambda g, j: (0, q0 + g * rep + j)), colblk(lambda g, j: (0, k0 + g)), colblk(lambda g, j: (0, v0 + g)),
                colblk(lambda g, j: (0, m0 + g * rep + j)), colblk(lambda g, j: (0, m0 + g * rep + j)),
                pl.BlockSpec((None, SEQ, BLK), lambda g, j: (g * rep + j, 0, 0)), tab, tab, tab]
    args = [proj, proj, proj, mixed, dmix, lse, *tables]
    if has_sink:
        in_specs.append(pl.BlockSpec((None, 8, BLK), lambda g, j: (g * rep + j, 0, 0)))
        args.append(sink_b)
    n_in = len(args)
    in_specs.append(pl.BlockSpec(memory_space=pl.ANY))
    args.append(dproj)
    acc = pltpu.VMEM((SEQ, BLK), F32)
    return pl.pallas_call(
        body, name=name, grid=(n_kv, rep), in_specs=in_specs,
        out_specs=[pl.BlockSpec(memory_space=pl.ANY), pl.BlockSpec((None, 8, BLK), lambda g, j: (g * rep + j, 0, 0))],
        out_shape=[jax.ShapeDtypeStruct(dproj.shape, BF16), jax.ShapeDtypeStruct((n_heads, 8, BLK), F32)],
        input_output_aliases={n_in: 0},
        scratch_shapes=[acc, acc, acc, acc, acc, acc, pltpu.VMEM((3, SEQ, BLK), BF16), pltpu.SemaphoreType.DMA((3,))],
        compiler_params=_cp(("arbitrary", "arbitrary")))(*args)


B_PATTERNS = tuple((w // d, d) for w, d in DILATED_PATTERNS)
C_PATTERNS = ((C_WINDOW - 1, 1),)


ANY = pl.BlockSpec(memory_space=pl.ANY)
CHIP_MASKS = ((1, 0), (0, 1), (1, 1))


def _coords():
    return lax.axis_index("x"), lax.axis_index("y"), lax.axis_index("c")


def _flip(v, m):
    return 1 - v if m else v


def _into_slot(name, w, layer, k_idx, dtype, run_after=None):
    _, rows, cols = w.shape
    tr = rows // 8 if rows % 64 == 0 else rows

    def body(k_ref, w_ref, *rest):
        rest[-1][...] = w_ref[...].astype(dtype)

    in_specs = [pl.BlockSpec((None, tr, cols), lambda i, k: (layer, i, 0))]
    args = [k_idx, w]
    if run_after is not None:
        in_specs.append(pl.BlockSpec(memory_space=pl.ANY))
        args.append(run_after)
    return pl.pallas_call(
        body, name=name,
        grid_spec=pltpu.PrefetchScalarGridSpec(
            num_scalar_prefetch=1, grid=(rows // tr,), in_specs=in_specs,
            out_specs=pl.BlockSpec((None, tr, cols), lambda i, k: (k[0], i, 0))),
        out_shape=jax.ShapeDtypeStruct((N_CHIPS, rows, cols), dtype),
        compiler_params=_cp(("parallel",)))(*args)


HBM_SPEC = pl.BlockSpec(memory_space=pltpu.HBM)
SEM_SPEC = pl.BlockSpec(memory_space=pltpu.SEMAPHORE)
TOKEN_SPEC = pl.BlockSpec(memory_space=pltpu.VMEM)
TOKEN_SHAPE = jax.ShapeDtypeStruct((8, BLK), F32)
DATAFLOW = pltpu.SideEffectType.DATAFLOW_SIDE_EFFECTING


def _hbm(a):
    return pltpu.with_memory_space_constraint(a, pltpu.HBM)


def _hbm_like(bufs):
    return [pltpu.HBM(b.shape, b.dtype) for b in bufs]


def _gather_start(name, stages):
    flat = [b for st in stages for b in st]
    n, ns = len(flat), len(stages)

    def body(*refs):
        ins = refs[:n]
        sems = refs[n:n + 2 * ns]
        token = refs[-1]
        x, y, c = _coords()
        k_me = 2 * x + y
        a = 0
        for s, st in enumerate(stages):
            for i in range(len(st)):
                mine = ins[a].at[k_me, c]
                for m, (mx, my) in enumerate(CHIP_MASKS):
                    pltpu.make_async_remote_copy(src_ref=mine, dst_ref=mine, send_sem=sems[2 * s].at[i * 3 + m],
                                                 recv_sem=sems[2 * s + 1].at[i * 3 + m],
                                                 device_id=(_flip(x, mx), _flip(y, my), c), device_id_type=MESH).start()
                a += 1
        token[...] = jnp.zeros_like(token)

    sem_shapes = []
    for st in stages:
        sem_shapes += [pltpu.SemaphoreType.DMA((3 * len(st),))] * 2
    out = pl.pallas_call(
        body, name=name, in_specs=[HBM_SPEC] * n,
        out_specs=tuple([SEM_SPEC] * (2 * ns) + [HBM_SPEC] * n + [TOKEN_SPEC]),
        out_shape=tuple(sem_shapes + _hbm_like(flat) + [TOKEN_SHAPE]),
        input_output_aliases={i: 2 * ns + i for i in range(n)},
        compiler_params=pltpu.CompilerParams(has_side_effects=DATAFLOW),
    )(*[_hbm(b) for b in flat])
    sems, bufs, token = out[:2 * ns], out[2 * ns:2 * ns + n], out[-1]
    res, a = [], 0
    for s, st in enumerate(stages):
        res.append((sems[2 * s], sems[2 * s + 1], list(bufs[a:a + len(st)])))
        a += len(st)
    return res, token


def _gather_forward(name, stage, after):
    ssem_in, rsem_in, bufs = stage
    n = len(bufs)

    def body(*refs):
        ins = refs[:n]
        s_in, r_in, _ = refs[n:n + 3]
        s_out, r_out = refs[n + 3:n + 5]
        token = refs[-1]
        x, y, c = _coords()
        for i in range(n):
            for m, (mx, my) in enumerate(CHIP_MASKS):
                kp = 2 * _flip(x, mx) + _flip(y, my)
                blk = ins[i].at[kp, c]
                got = pltpu.make_async_remote_copy(src_ref=blk, dst_ref=blk, send_sem=s_in.at[i * 3 + m],
                                                   recv_sem=r_in.at[i * 3 + m], device_id=(x, y, 1 - c), device_id_type=MESH)
                got.wait_send()
                got.wait_recv()
                pltpu.make_async_remote_copy(src_ref=blk, dst_ref=blk, send_sem=s_out.at[i * 3 + m],
                                             recv_sem=r_out.at[i * 3 + m], device_id=(x, y, 1 - c), device_id_type=MESH).start()
        token[...] = jnp.zeros_like(token)

    sem = pltpu.SemaphoreType.DMA((3 * n,))
    out = pl.pallas_call(
        body, name=name, in_specs=[HBM_SPEC] * n + [SEM_SPEC, SEM_SPEC, ANY],
        out_specs=tuple([SEM_SPEC, SEM_SPEC] + [HBM_SPEC] * n + [TOKEN_SPEC]),
        out_shape=tuple([sem, sem] + _hbm_like(bufs) + [TOKEN_SHAPE]),
        input_output_aliases={i: 2 + i for i in range(n)},
        compiler_params=pltpu.CompilerParams(has_side_effects=DATAFLOW),
    )(*bufs, ssem_in, rsem_in, after)
    return (out[0], out[1], list(out[2:2 + n])), out[-1]


def _gather_wait(name, stage, after):
    ssem, rsem, bufs = stage
    n = len(bufs)

    def body(*refs):
        ins = refs[:n]
        s_in, r_in, _ = refs[n:n + 3]
        x, y, c = _coords()
        for i in range(n):
            for m, (mx, my) in enumerate(CHIP_MASKS):
                kp = 2 * _flip(x, mx) + _flip(y, my)
                sent, got = ins[i].at[kp, c], ins[i].at[kp, 1 - c]
                cp = pltpu.make_async_remote_copy(src_ref=sent, dst_ref=got, send_sem=s_in.at[i * 3 + m],
                                                  recv_sem=r_in.at[i * 3 + m], device_id=(x, y, 1 - c), device_id_type=MESH)
                cp.wait_send()
                cp.wait_recv()

    out = pl.pallas_call(
        body, name=name, in_specs=[HBM_SPEC] * n + [SEM_SPEC, SEM_SPEC, ANY],
        out_specs=tuple([HBM_SPEC] * n), out_shape=tuple(_hbm_like(bufs)),
        input_output_aliases={i: i for i in range(n)},
        compiler_params=pltpu.CompilerParams(has_side_effects=DATAFLOW),
    )(*bufs, ssem, rsem, after)
    return list(out)


def _swap_start(name, grads):
    n = len(grads)

    def body(*refs):
        ins, lands = refs[:n], refs[n:2 * n]
        ssem, rsem = refs[2 * n:2 * n + 2]
        x, y, c = _coords()
        for a in range(n):
            for j in range(N_CHIPS):
                pltpu.make_async_remote_copy(src_ref=ins[a].at[j, 1 - c], dst_ref=lands[a].at[j],
                                             send_sem=ssem.at[a * N_CHIPS + j], recv_sem=rsem.at[a * N_CHIPS + j],
                                             device_id=(x, y, 1 - c), device_id_type=MESH).start()

    sem = pltpu.SemaphoreType.DMA((N_CHIPS * n,))
    land_shapes = [pltpu.HBM((N_CHIPS,) + g.shape[2:], g.dtype) for g in grads]
    out = pl.pallas_call(
        body, name=name, in_specs=[HBM_SPEC] * (2 * n),
        out_specs=tuple([SEM_SPEC, SEM_SPEC] + [HBM_SPEC] * (2 * n)),
        out_shape=tuple([sem, sem] + _hbm_like(grads) + land_shapes),
        input_output_aliases={i: 2 + i for i in range(2 * n)},
        compiler_params=pltpu.CompilerParams(has_side_effects=DATAFLOW),
    )(*[_hbm(g) for g in grads], *[_hbm(lax.empty((N_CHIPS,) + g.shape[2:], g.dtype)) for g in grads])
    return out[0], out[1], list(out[2:2 + n]), list(out[2 + n:])


def _swap_wait(name, started, after):
    ssem, rsem, grads, lands = started
    n = len(grads)
    after = after if isinstance(after, tuple) else (after,)

    def body(*refs):
        ins, lnd = refs[:n], refs[n:2 * n]
        s_in, r_in = refs[2 * n:2 * n + 2]
        x, y, c = _coords()
        for a in range(n):
            for j in range(N_CHIPS):
                cp = pltpu.make_async_remote_copy(src_ref=ins[a].at[j, 1 - c], dst_ref=lnd[a].at[j],
                                                  send_sem=s_in.at[a * N_CHIPS + j], recv_sem=r_in.at[a * N_CHIPS + j],
                                                  device_id=(x, y, 1 - c), device_id_type=MESH)
                cp.wait_send()
                cp.wait_recv()

    out = pl.pallas_call(
        body, name=name, in_specs=[HBM_SPEC] * (2 * n) + [SEM_SPEC, SEM_SPEC] + [ANY] * len(after),
        out_specs=tuple([HBM_SPEC] * (2 * n)), out_shape=tuple(_hbm_like(grads) + _hbm_like(lands)),
        input_output_aliases={i: i for i in range(2 * n)},
        compiler_params=pltpu.CompilerParams(has_side_effects=DATAFLOW),
    )(*grads, *lands, ssem, rsem, *after)
    return list(out[:n]), list(out[n:])


def _scatter_start(name, parts):
    n = len(parts)

    def body(*refs):
        ins, lands = refs[:n], refs[n:2 * n]
        ssem, rsem = refs[2 * n:2 * n + 2]
        x, y, c = _coords()
        k_me = 2 * x + y
        for a in range(n):
            for m, (mx, my) in enumerate(CHIP_MASKS):
                px, py = _flip(x, mx), _flip(y, my)
                pltpu.make_async_remote_copy(src_ref=ins[a].at[2 * px + py], dst_ref=lands[a].at[k_me],
                                             send_sem=ssem.at[a * 3 + m], recv_sem=rsem.at[a * 3 + m],
                                             device_id=(px, py, c), device_id_type=MESH).start()

    sem = pltpu.SemaphoreType.DMA((3 * n,))
    out = pl.pallas_call(
        body, name=name, in_specs=[HBM_SPEC] * (2 * n),
        out_specs=tuple([SEM_SPEC, SEM_SPEC] + [HBM_SPEC] * (2 * n)),
        out_shape=tuple([sem, sem] + _hbm_like(parts) + _hbm_like(parts)),
        input_output_aliases={i: 2 + i for i in range(2 * n)},
        compiler_params=pltpu.CompilerParams(has_side_effects=DATAFLOW),
    )(*[_hbm(p) for p in parts], *[_hbm(lax.empty(p.shape, p.dtype)) for p in parts])
    return out[0], out[1], list(out[2:2 + n]), list(out[2 + n:])


def _scatter_wait(name, started, after):
    ssem, rsem, parts, lands = started
    n = len(parts)

    def body(*refs):
        ins, lnd = refs[:n], refs[n:2 * n]
        s_in, r_in, _ = refs[2 * n:2 * n + 3]
        x, y, c = _coords()
        k_me = 2 * x + y
        for a in range(n):
            for m, (mx, my) in enumerate(CHIP_MASKS):
                px, py = _flip(x, mx), _flip(y, my)
                cp = pltpu.make_async_remote_copy(src_ref=ins[a].at[2 * px + py], dst_ref=lnd[a].at[k_me],
                                                  send_sem=s_in.at[a * 3 + m], recv_sem=r_in.at[a * 3 + m],
                                                  device_id=(px, py, c), device_id_type=MESH)
                cp.wait_send()
                cp.wait_recv()

    out = pl.pallas_call(
        body, name=name, in_specs=[HBM_SPEC] * (2 * n) + [SEM_SPEC, SEM_SPEC, ANY],
        out_specs=tuple([HBM_SPEC] * (2 * n)), out_shape=tuple(_hbm_like(parts) + _hbm_like(lands)),
        input_output_aliases={i: i for i in range(2 * n)},
        compiler_params=pltpu.CompilerParams(has_side_effects=DATAFLOW),
    )(*parts, *lands, ssem, rsem, after)
    return list(out[:n]), list(out[n:])


def _pair_gather_start(name, bufs):
    n = len(bufs)

    def body(*refs):
        ins = refs[:n]
        ssem, rsem = refs[n:n + 2]
        x, y, c = _coords()
        for a in range(n):
            mine = ins[a].at[c]
            pltpu.make_async_remote_copy(src_ref=mine, dst_ref=mine, send_sem=ssem.at[a], recv_sem=rsem.at[a],
                                         device_id=(x, y, 1 - c), device_id_type=MESH).start()

    sem = pltpu.SemaphoreType.DMA((n,))
    out = pl.pallas_call(
        body, name=name, in_specs=[HBM_SPEC] * n, out_specs=tuple([SEM_SPEC, SEM_SPEC] + [HBM_SPEC] * n),
        out_shape=tuple([sem, sem] + _hbm_like(bufs)),
        input_output_aliases={i: 2 + i for i in range(n)},
        compiler_params=pltpu.CompilerParams(has_side_effects=DATAFLOW),
    )(*[_hbm(b) for b in bufs])
    return out[0], out[1], list(out[2:])


def _pair_gather_wait(name, started, after):
    ssem, rsem, bufs = started
    n = len(bufs)

    def body(*refs):
        ins = refs[:n]
        s_in, r_in, _ = refs[n:n + 3]
        x, y, c = _coords()
        for a in range(n):
            cp = pltpu.make_async_remote_copy(src_ref=ins[a].at[c], dst_ref=ins[a].at[1 - c], send_sem=s_in.at[a],
                                              recv_sem=r_in.at[a], device_id=(x, y, 1 - c), device_id_type=MESH)
            cp.wait_send()
            cp.wait_recv()

    out = pl.pallas_call(
        body, name=name, in_specs=[HBM_SPEC] * n + [SEM_SPEC, SEM_SPEC, ANY],
        out_specs=tuple([HBM_SPEC] * n), out_shape=tuple(_hbm_like(bufs)),
        input_output_aliases={i: i for i in range(n)},
        compiler_params=pltpu.CompilerParams(has_side_effects=DATAFLOW),
    )(*bufs, ssem, rsem, after)
    return list(out)


DEV_MASKS = tuple((mx, my, mc) for mx in (0, 1) for my in (0, 1) for mc in (0, 1) if (mx, my, mc) != (0, 0, 0))


def _gather_small(buf, run_after):
    def body(in_ref, _, out_ref, ssem, rsem, lsem):
        x, y, c = _coords()
        me = 4 * x + 2 * y + c
        cps = [pltpu.make_async_copy(in_ref, out_ref.at[me], lsem)]
        cps[0].start()
        for t, (mx, my, mc) in enumerate(DEV_MASKS):
            cp = pltpu.make_async_remote_copy(src_ref=in_ref, dst_ref=out_ref.at[me], send_sem=ssem.at[t],
                                              recv_sem=rsem.at[t], device_id=(_flip(x, mx), _flip(y, my), _flip(c, mc)),
                                              device_id_type=MESH)
            cp.start()
            cps.append(cp)
        for cp in cps:
            cp.wait()

    return pl.pallas_call(
        body, name="gather_small", in_specs=[ANY, ANY], out_specs=ANY,
        out_shape=jax.ShapeDtypeStruct((N_DEV,) + buf.shape, buf.dtype),
        scratch_shapes=[pltpu.SemaphoreType.DMA((N_DEV - 1,)), pltpu.SemaphoreType.DMA((N_DEV - 1,)),
                        pltpu.SemaphoreType.DMA(())],
        compiler_params=pltpu.CompilerParams(has_side_effects=True),
    )(buf, run_after)


def _row_tile(rows):
    return rows // 2 if rows % 16 == 0 else rows


def _pair_add(name, grad, got, c_idx):
    _, _, r2, cols = grad.shape
    tr = _row_tile(r2)

    def body(c_ref, a_ref, b_ref, o_ref):
        o_ref[...] = (a_ref[...].astype(F32) + b_ref[...].astype(F32)).astype(BF16)

    return pl.pallas_call(
        body, name=name,
        grid_spec=pltpu.PrefetchScalarGridSpec(
            num_scalar_prefetch=1, grid=(N_CHIPS, r2 // tr),
            in_specs=[pl.BlockSpec((None, None, tr, cols), lambda j, i, c: (j, c[0], i, 0)),
                      pl.BlockSpec((None, tr, cols), lambda j, i, c: (j, i, 0))],
            out_specs=pl.BlockSpec((None, tr, cols), lambda j, i, c: (j, i, 0))),
        out_shape=jax.ShapeDtypeStruct((N_CHIPS, r2, cols), BF16),
        compiler_params=_cp(("parallel", "parallel")))(c_idx, grad, got)


def _chip_add(name, part, got, kc_idx):
    _, r2, cols = got.shape
    tr = _row_tile(r2)

    def body(k_ref, p_ref, g1_ref, g2_ref, g3_ref, o_ref):
        acc = p_ref[...].astype(F32)
        for g_ref in (g1_ref, g2_ref, g3_ref):
            acc = acc + g_ref[...].astype(F32)
        o_ref[...] = acc

    def slot(d):
        return pl.BlockSpec((None, tr, cols), lambda i, k: ((k[0] + d) % N_CHIPS, i, 0))

    return pl.pallas_call(
        body, name=name,
        grid_spec=pltpu.PrefetchScalarGridSpec(
            num_scalar_prefetch=1, grid=(r2 // tr,),
            in_specs=[slot(0), slot(1), slot(2), slot(3)],
            out_specs=pl.BlockSpec((None, tr, cols), lambda i, k: (k[1], i, 0))),
        out_shape=jax.ShapeDtypeStruct((2, r2, cols), F32),
        compiler_params=_cp(("parallel",)))(kc_idx, part, got, got, got)


def _adam_math(w, g, m, v):
    m2 = ADAM_B1 * m + (1.0 - ADAM_B1) * g
    v2 = ADAM_B2 * v + (1.0 - ADAM_B2) * (g * g)
    m_hat = m2 / (1.0 - ADAM_B1 ** ADAM_STEP)
    v_hat = v2 / (1.0 - ADAM_B2 ** ADAM_STEP)
    delta = -ADAM_LR * (m_hat / (jnp.sqrt(v_hat) + ADAM_EPS) + ADAM_WD * w)
    return delta, m2, v2


def _adamw_matrix(name, w, g_layers, m, v):
    _, rows, cols = w.shape
    tr = rows // 8

    def body(w_ref, g0_ref, g1_ref, m_ref, v_ref, go_ref, d_ref, mo_ref, vo_ref):
        g = jnp.where(pl.program_id(0) == 0, g0_ref[...], g1_ref[...])
        go_ref[...] = g
        d_ref[...], mo_ref[...], vo_ref[...] = _adam_math(w_ref[...], g, m_ref[...], v_ref[...])

    lay = pl.BlockSpec((None, tr, cols), lambda l, i: (l, i, 0))
    flat = pl.BlockSpec((tr, cols), lambda l, i: (i, 0))
    shp = jax.ShapeDtypeStruct(w.shape, F32)
    return pl.pallas_call(body, name=name, grid=(DEPTH, rows // tr), in_specs=[lay, flat, flat, lay, lay],
                          out_specs=[lay, lay, lay, lay], out_shape=[shp, shp, shp, shp],
                          compiler_params=_cp(("parallel", "parallel")))(w, g_layers[0], g_layers[1], m, v)


def _sum_small(gathered):
    def body(g_ref, o_ref):
        acc = g_ref[0]
        for d in range(1, N_DEV):
            acc = acc + g_ref[d]
        o_ref[...] = acc

    return pl.pallas_call(body, name="sum_small", out_shape=jax.ShapeDtypeStruct(gathered.shape[1:], F32),
                          compiler_params=_cp())(gathered)


def _adamw_small(w, g, m, v):
    def body(w_ref, g_ref, m_ref, v_ref, d_ref, mo_ref, vo_ref):
        d_ref[...], mo_ref[...], vo_ref[...] = _adam_math(w_ref[...], g_ref[...], m_ref[...], v_ref[...])

    shp = jax.ShapeDtypeStruct(w.shape, F32)
    return pl.pallas_call(body, name="adamw_small", out_shape=[shp, shp, shp], compiler_params=_cp())(w, g, m, v)


def _pack(arrays, rows):
    flat = jnp.concatenate([a.reshape(-1) for a in arrays])
    return jnp.pad(flat, (0, rows * BLK - flat.shape[0])).reshape(rows, BLK)


def _unpack(buf, shapes):
    flat = buf.reshape(-1)
    out, pos = [], 0
    for s in shapes:
        n = math.prod(s)
        out.append(flat[pos:pos + n].reshape(s))
        pos += n
    return out


def _rows_for(shapes):
    n = sum(math.prod(s) for s in shapes)
    return -(-n // (8 * BLK)) * 8


def _rs_swap(tag, grads):
    return _swap_start(f"rs_swap_start{tag}", [g.reshape(N_CHIPS, 2, g.shape[1] // 2, g.shape[2]) for g in grads])


def _rs_scatter(tag, swapping, after, c_idx):
    split, got = _swap_wait(f"rs_swap_wait{tag}", swapping, after)
    parts = [_pair_add(f"rs_pair_add{tag}_{i}", s, r, c_idx) for i, (s, r) in enumerate(zip(split, got))]
    return _scatter_start(f"rs_scatter_start{tag}", parts)


def _rs_reduce(tag, started, after, kc_idx):
    parts, lands = _scatter_wait(f"rs_scatter_wait{tag}", started, after)
    halves = [_chip_add(f"rs_chip_add{tag}_{i}", p, r, kc_idx) for i, (p, r) in enumerate(zip(parts, lands))]
    return _pair_gather_start(f"rs_pair_gather_start{tag}", halves)


def _rs_finish(tag, gathering, after):
    full = _pair_gather_wait(f"rs_pair_gather_wait{tag}", gathering, after)
    return [f.reshape(2 * f.shape[1], f.shape[2]) for f in full]


def kernel(x, w_in, lb_logits, a_norm_w, c_sinks, w_out, ln1_g, ln1_b, w_gate, w_up, conv_w, conv_b, w_down, ln2_g, ln2_b, loss_target, m_w_in, m_lb_logits, m_a_norm_w, m_c_sinks, m_w_out, m_ln1_g, m_ln1_b, m_w_gate, m_w_up, m_conv_w, m_conv_b, m_w_down, m_ln2_g, m_ln2_b, v_w_in, v_lb_logits, v_a_norm_w, v_c_sinks, v_w_out, v_ln1_g, v_ln1_b, v_w_gate, v_w_up, v_conv_w, v_conv_b, v_w_down, v_ln2_g, v_ln2_b):
    cx, cy, cc = _coords()
    c_idx = jnp.reshape(cc, (1,)).astype(jnp.int32)
    k_me = 2 * cx + cy
    k_idx = jnp.reshape(k_me, (1,)).astype(jnp.int32)
    kc_idx = jnp.stack([k_me, cc]).astype(jnp.int32)

    def slot(nm, w, l, run_after=None):
        b = _into_slot(f"slot_{nm}{l}", w, l, k_idx, BF16, run_after)
        return b.reshape(N_CHIPS, 2, b.shape[1] // 2, b.shape[2])

    cw_slot = _into_slot("slot_cw", conv_w.reshape(1, DEPTH * CONV_WIDTH, FF_SHARD), 0, k_idx, F32)
    cw_slot = cw_slot.reshape(N_CHIPS, DEPTH, CONV_WIDTH, FF_SHARD)
    first, token = _gather_start("gather_start0", [[slot("wi", w_in, 0), cw_slot]])
    sl = [{nm: slot(nm, w, l, token) for nm, w in (("wi", w_in), ("wo", w_out), ("wg", w_gate), ("wu", w_up), ("wd", w_down))
           if (nm, l) != ("wi", 0)} for l in range(DEPTH)]
    order = [(l, nm) for l in range(DEPTH) for nm in ("wi", "wo", "wg", "wu", "wd")][1:]
    rest, token = _gather_start("gather_start1", [[sl[l][nm]] for l, nm in order])
    stage_of = {key: st for key, st in zip(order, rest)}

    def mat(b):
        return b.reshape(N_CHIPS, 2 * b.shape[2], b.shape[3])

    h = x[0]
    h_bf = _to_bf16("x_bf16", h, token)
    fwd0, token = _gather_forward("gather_fwd0", first[0], h_bf)
    wi0, cw_all = _gather_wait("gather_wait0", fwd0, token)
    cw_full = jnp.transpose(cw_all, (1, 2, 0, 3)).reshape(DEPTH, CONV_WIDTH, D_FF)
    tables = _rope_tables()

    passing = {}

    def pass_on(l, nm, after):
        passing[(l, nm)] = _gather_forward(f"gather_fwd_{nm}{l}", stage_of[(l, nm)], after)

    def arrived(l, nm, after):
        i = order.index((l, nm))
        if i + 1 < len(order):
            pass_on(*order[i + 1], after)
            after = passing[order[i + 1]][1]
        return mat(_gather_wait(f"gather_wait_{nm}{l}", passing[(l, nm)][0], after)[0])

    saved = []
    weights = []
    for l in range(DEPTH):
        wi = mat(wi0) if l == 0 else arrived(l, "wi", h)
        proj = _fwd_colsharded(f"proj{l}", h_bf, wi)
        mixed, mixed_bf, raw = _hgrn_fwd(f"hgrn_fwd{l}", proj, lb_logits, a_norm_w[l], l)
        mixed, mixed_bf, lse_b = _attn_fwd(f"dilated_fwd{l}", proj, tables, None, mixed, mixed_bf, n_heads=B_HEADS, rep=1,
                                           q0=QB0, k0=KB0, v0=VB0, m0=A_HEADS, patterns=B_PATTERNS)
        if l == 0:
            pass_on(l, "wo", lse_b)
        sink_b = jnp.broadcast_to(c_sinks[l][:, None, None], (C_HEADS, 8, BLK))
        mixed, mixed_bf, lse_c = _attn_fwd(f"window_fwd{l}", proj, tables, sink_b, mixed, mixed_bf, n_heads=C_HEADS,
                                           rep=C_HEADS // C_KV_HEADS, q0=QC0, k0=KC0, v0=VC0, m0=A_HEADS + B_HEADS,
                                           patterns=C_PATTERNS)
        wo = arrived(l, "wo", lse_c)
        y1 = _fwd_rowsharded(f"wout{l}", mixed_bf, wo, OUT_SHARD)
        x1, x1_bf = _ln_fwd(f"ln1_fwd{l}", h, y1, ln1_g[l], ln1_b[l])
        wg = arrived(l, "wg", x1)
        g = _fwd_colsharded(f"gate{l}", x1_bf, wg, BF16)
        wu = arrived(l, "wu", g)
        u = _fwd_colsharded(f"up{l}", x1_bf, wu, BF16)
        hh = _conv_gate_fwd(f"conv_fwd{l}", g, u, cw_full[l], conv_b[l])
        wd = arrived(l, "wd", hh)
        y2 = _fwd_rowsharded(f"down{l}", hh, wd, FF_SHARD)
        x2, x2_bf = _ln_fwd(f"ln2_fwd{l}", x1, y2, ln2_g[l], ln2_b[l])
        weights.append(dict(wi=wi, wo=wo, wg=wg, wu=wu, wd=wd))
        saved.append((h, h_bf, proj, raw, lse_b, sink_b, lse_c, mixed, mixed_bf, y1, x1, x1_bf, g, u, hh, y2))
        h, h_bf = x2, x2_bf

    dy, loss_part = _loss_head(h, loss_target[0])

    d_res, d_path = None, dy
    small = [None] * DEPTH
    mat_grads = [None] * DEPTH
    late = {}
    prev_ffn = prev_mix_swap = None
    for l in reversed(range(DEPTH)):
        h_in, h_in_bf, proj, raw, lse_b, sink_b, lse_c, mixed, mixed_bf, y1, x1, x1_bf, g, u, hh, y2 = saved[l]
        wi, wo, wg, wu, wd = (weights[l][k] for k in ("wi", "wo", "wg", "wu", "wd"))
        dz2, dz2_bf, d_ln2g, d_ln2b = _ln_bwd(f"ln2_bwd{l}", x1, y2, ln2_g[l], d_res, d_path,
                                              run_after=prev_mix_swap[2][0] if prev_mix_swap else None)
        dhh = _bwd_act_rowsharded(f"down_dx{l}", dz2_bf, wd, FF_SHARD, BF16)
        prev_mix = _rs_scatter(f"{l + 1}m", prev_mix_swap, dhh, c_idx) if prev_mix_swap else None
        d_wd = _bwd_w_rowsharded(f"down_dw{l}", hh, dz2_bf, FF_SHARD)
        dg, du, d_cw, d_cb = _conv_gate_bwd(f"conv_bwd{l}", g, u, cw_full[l], conv_b[l], dhh,
                                            run_after=prev_mix[2][0] if prev_mix else None)
        dx1 = _bwd_act_colsharded(f"gateup_dx{l}", [(dg, wg), (du, wu)])
        d_wg = _bwd_w_colsharded(f"gate_dw{l}", x1_bf, dg)
        d_wu = _bwd_w_colsharded(f"up_dw{l}", x1_bf, du)
        pins = ()
        if prev_ffn:
            late[l + 1] = [_rs_reduce(f"{l + 1}f", prev_ffn, d_wu, kc_idx)]
        ffn_swap = _rs_swap(f"{l}f", [d_wg, d_wu, d_wd])
        dz1, dz1_bf, d_ln1g, d_ln1b = _ln_bwd(f"ln1_bwd{l}", h_in, y1, ln1_g[l], dz2, dx1, run_after=ffn_swap[2][0])
        dmix = _bwd_act_rowsharded(f"wout_dx{l}", dz1_bf, wo, OUT_SHARD)
        d_wo = _bwd_w_rowsharded(f"wout_dw{l}", mixed_bf, dz1_bf, OUT_SHARD)
        if prev_mix:
            late[l + 1].append(_rs_reduce(f"{l + 1}m", prev_mix, d_wo, kc_idx))
            pins = tuple(g[2][0] for g in late[l + 1])
        s_ffn = _rs_scatter(f"{l}f", ffn_swap, (d_wo,) + pins, c_idx)
        dproj, d_nw, d_lb = _hgrn_bwd(f"hgrn_bwd{l}", proj, raw, dmix, lb_logits, a_norm_w[l], l, run_after=s_ffn[2][0])
        dproj, _ = _attn_bwd(f"dilated_bwd{l}", proj, mixed, dmix, lse_b, tables, None, dproj, n_kv=B_HEADS, rep=1,
                             q0=QB0, k0=KB0, v0=VB0, m0=A_HEADS, patterns=B_PATTERNS)
        dproj, d_sink = _attn_bwd(f"window_bwd{l}", proj, mixed, dmix, lse_c, tables, sink_b, dproj, n_kv=C_KV_HEADS,
                                  rep=C_HEADS // C_KV_HEADS, q0=QC0, k0=KC0, v0=VC0, m0=A_HEADS + B_HEADS,
                                  patterns=C_PATTERNS)
        dxp = _bwd_act_colsharded_full(f"proj_dx{l}", dproj, wi)
        d_wi = _bwd_w_colsharded(f"proj_dw{l}", h_in_bf, dproj)
        d_res, d_path = dz1, dxp
        prev_ffn, prev_mix_swap = s_ffn, _rs_swap(f"{l}m", [d_wi, d_wo])
        small[l] = (d_lb, d_nw.reshape(A_HEADS, 8, BLK)[:, 0].sum(0), d_sink[:, 0, 0], d_ln1g[0], d_ln1b[0],
                    d_cw, d_cb[0], d_ln2g[0], d_ln2b[0])
    grad_x2 = _axpy("grad_x", d_res, d_path)
    grad_x = grad_x2[None]

    g_lb = small[0][0] + small[1][0]
    per_layer = [jnp.stack([small[0][i], small[1][i]]) for i in range(1, 9)]
    small_shapes = [(DEPTH, 4 * BLK), (DEPTH, BLK), (DEPTH, C_HEADS), (DEPTH, D_MODEL), (DEPTH, D_MODEL),
                    (DEPTH, CONV_WIDTH, D_FF), (DEPTH, D_FF), (DEPTH, D_MODEL), (DEPTH, D_MODEL), (BLK,)]
    rows = _rows_for(small_shapes)
    total = _sum_small(_gather_small(_pack([g_lb] + per_layer + [loss_part[0]], rows), prev_mix_swap[2][0]))
    g_lb, g_nw, g_sink, g_ln1g, g_ln1b, g_cw_full, g_cb, g_ln2g, g_ln2b, loss_row = _unpack(total, small_shapes)
    loss = loss_row[0]
    g_cw = lax.dynamic_slice_in_dim(g_cw_full, k_me * FF_SHARD, FF_SHARD, axis=2)

    sw = [lb_logits, a_norm_w, c_sinks, ln1_g, ln1_b, conv_w, conv_b, ln2_g, ln2_b]
    sg = [g_lb, g_nw, g_sink, g_ln1g, g_ln1b, g_cw, g_cb, g_ln2g, g_ln2b]
    sm = [m_lb_logits, m_a_norm_w, m_c_sinks, m_ln1_g, m_ln1_b, m_conv_w, m_conv_b, m_ln2_g, m_ln2_b]
    sv = [v_lb_logits, v_a_norm_w, v_c_sinks, v_ln1_g, v_ln1_b, v_conv_w, v_conv_b, v_ln2_g, v_ln2_b]
    shapes = [a.shape for a in sw]
    prow = _rows_for(shapes)
    sd, snm, snv = (_unpack(b, shapes) for b in _adamw_small(_pack(sw, prow), _pack(sg, prow), _pack(sm, prow), _pack(sv, prow)))

    names = ["w_in", "w_out", "w_gate", "w_up", "w_down"]
    mw = [w_in, w_out, w_gate, w_up, w_down]
    mm = [m_w_in, m_w_out, m_w_gate, m_w_up, m_w_down]
    mv = [v_w_in, v_w_out, v_w_gate, v_w_up, v_w_down]
    res = [None] * 5
    s_mix = _rs_scatter("0m", prev_mix_swap, total, c_idx)
    for l, (g_ffn, g_mix) in late.items():
        g_wg, g_wu, g_wd = _rs_finish(f"{l}f", g_ffn, s_mix[2][0])
        g_wi, g_wo = _rs_finish(f"{l}m", g_mix, s_mix[2][0])
        mat_grads[l] = [g_wi, g_wo, g_wg, g_wu, g_wd]
    ffn0 = _rs_finish("0f", _rs_reduce("0f", prev_ffn, s_mix[2][0], kc_idx), s_mix[2][0])
    for i, g0 in zip((2, 3, 4), ffn0):
        res[i] = _adamw_matrix(f"adamw_{names[i]}", mw[i], [g0, mat_grads[1][i]], mm[i], mv[i])
    mix0 = _rs_finish("0m", _rs_reduce("0m", s_mix, res[4][1], kc_idx), res[4][1])
    for i, g0 in zip((0, 1), mix0):
        res[i] = _adamw_matrix(f"adamw_{names[i]}", mw[i], [g0, mat_grads[1][i]], mm[i], mv[i])
    mg, md, mnm, mnv = ([r[j] for r in res] for j in range(4))

    def ordered(mat, sm_):
        return [mat[0], sm_[0], sm_[1], sm_[2], mat[1], sm_[3], sm_[4], mat[2], mat[3], sm_[5], sm_[6], mat[4], sm_[7], sm_[8]]

    return (loss, grad_x, *ordered(mg, sg), *ordered(md, sd), *ordered(mnm, snm), *ordered(mnv, snv))
```

```python
import functools
import math

import jax
import jax.numpy as jnp
from jax import lax
from jax.experimental import pallas as pl
from jax.experimental.pallas import tpu as pltpu

F32 = jnp.float32
BF16 = jnp.bfloat16

D_MODEL = 2048
SEQ = 2048
DEPTH = 2
HEAD_DIM = 128
A_HEADS = 4
B_HEADS = 6
C_HEADS = 6
C_KV_HEADS = 2
A_CHUNK = 16
DILATED_PATTERNS = ((128, 1), (512, 4), (2048, 16))
C_WINDOW = 128
ROPE_THETA = 500000.0
ROPE_DIM = HEAD_DIM // 4
D_FF = 5632
CONV_WIDTH = 3
LN_EPS = 1e-5
ALPHA = (2 * DEPTH) ** 0.25
IN_WIDTH = 5632
MIX_WIDTH = 2048
ADAM_LR = 0.001
ADAM_B1 = 0.9
ADAM_B2 = 0.999
ADAM_EPS = 1e-08
ADAM_WD = 0.01
ADAM_STEP = 10

N_CHIPS = 4
N_DEV = 8
FF_SHARD = D_FF // N_CHIPS
OUT_SHARD = MIX_WIDTH // N_CHIPS
BLK = 128
N_CHUNK = SEQ // A_CHUNK
SLAB = 32

QA0, FA0, IA0, GA0 = 0, 4, 8, 12
QB0, KB0, VB0 = 16, 22, 28
QC0, KC0, VC0 = 34, 40, 42

VMEM_LIMIT_V7X = 56 * 1024 * 1024
HI = lax.Precision.HIGHEST
MESH = pl.DeviceIdType.MESH


def _cp(sem=None, vmem=VMEM_LIMIT_V7X, **kw):
    return pltpu.CompilerParams(dimension_semantics=sem, vmem_limit_bytes=vmem, **kw)


def _sigmoid(x):
    return 1.0 / (1.0 + jnp.exp(-x))


def _gate_sigmoid(x):
    return 0.5 * jnp.tanh(0.5 * x) + 0.5


def _mm(name, pairs, dims, grid, a_specs, b_specs, out_spec, out_shape, nk=1, acc_shape=None):
    n_pairs = len(pairs)

    def body(*refs):
        o_ref = refs[2 * n_pairs]
        part = None
        for p in range(n_pairs):
            a = refs[2 * p][...].astype(BF16)
            b = refs[2 * p + 1][...].astype(BF16)
            t = lax.dot_general(a, b, dims, preferred_element_type=F32)
            part = t if part is None else part + t
        if nk == 1:
            o_ref[...] = part.astype(o_ref.dtype)
        else:
            acc = refs[2 * n_pairs + 1]
            k = pl.program_id(len(grid) - 1)

            @pl.when(k == 0)
            def _():
                acc[...] = part

            @pl.when(k > 0)
            def _():
                acc[...] += part

            @pl.when(k == nk - 1)
            def _():
                o_ref[...] = acc[...].astype(o_ref.dtype)

    in_specs, args = [], []
    for (a, b), sa, sb in zip(pairs, a_specs, b_specs):
        in_specs += [sa, sb]
        args += [a, b]
    sem = ("parallel",) * (len(grid) - (1 if nk > 1 else 0)) + (("arbitrary",) if nk > 1 else ())
    return pl.pallas_call(
        body, name=name, grid=grid, in_specs=in_specs, out_specs=out_spec, out_shape=out_shape,
        scratch_shapes=[pltpu.VMEM(acc_shape, F32)] if nk > 1 else [],
        compiler_params=_cp(sem),
    )(*args)


NN = (((1,), (0,)), ((), ()))
NT = (((1,), (1,)), ((), ()))
TN = (((0,), (0,)), ((), ()))
TM = 1024


def _fwd_colsharded(name, x, w_stk, out_dtype=F32):
    return _mm(name, [(x, w_stk)], NN, (N_CHIPS, SEQ // TM),
               [pl.BlockSpec((TM, D_MODEL), lambda j, i: (i, 0))],
               [pl.BlockSpec((None, D_MODEL, FF_SHARD), lambda j, i: (j, 0, 0))],
               pl.BlockSpec((TM, FF_SHARD), lambda j, i: (i, j)),
               jax.ShapeDtypeStruct((SEQ, D_FF), out_dtype))


def _fwd_rowsharded(name, a, w_stk, shard):
    tn = 512
    rows = N_CHIPS * shard
    return _mm(name, [(a, w_stk.reshape(rows, D_MODEL))], NN, (SEQ // TM, D_MODEL // tn),
               [pl.BlockSpec((TM, rows), lambda i, j: (i, 0))],
               [pl.BlockSpec((rows, tn), lambda i, j: (0, j))],
               pl.BlockSpec((TM, tn), lambda i, j: (i, j)),
               jax.ShapeDtypeStruct((SEQ, D_MODEL), F32))


def _bwd_act_colsharded(name, pairs):
    tn = 1024
    n = len(pairs)
    return _mm(name, pairs, NT, (SEQ // TM, D_MODEL // tn, N_CHIPS),
               [pl.BlockSpec((TM, FF_SHARD), lambda i, j, k: (i, k))] * n,
               [pl.BlockSpec((None, tn, FF_SHARD), lambda i, j, k: (k, j, 0))] * n,
               pl.BlockSpec((TM, tn), lambda i, j, k: (i, j)),
               jax.ShapeDtypeStruct((SEQ, D_MODEL), F32), nk=N_CHIPS, acc_shape=(TM, tn))


def _bwd_act_colsharded_full(name, dy, w_stk):
    tn = 512

    def body(a_ref, w_ref, o_ref):
        acc = None
        for k in range(N_CHIPS):
            t = lax.dot_general(a_ref[:, k * FF_SHARD:(k + 1) * FF_SHARD], w_ref[k], NT, preferred_element_type=F32)
            acc = t if acc is None else acc + t
        o_ref[...] = acc

    return pl.pallas_call(
        body, name=name, grid=(SEQ // TM, D_MODEL // tn),
        in_specs=[pl.BlockSpec((TM, D_FF), lambda i, j: (i, 0)),
                  pl.BlockSpec((N_CHIPS, tn, FF_SHARD), lambda i, j: (0, j, 0))],
        out_specs=pl.BlockSpec((TM, tn), lambda i, j: (i, j)),
        out_shape=jax.ShapeDtypeStruct((SEQ, D_MODEL), F32),
        compiler_params=_cp(("parallel", "parallel")))(dy, w_stk)


def _bwd_act_rowsharded(name, dy, w_stk, shard, out_dtype=F32):
    return _mm(name, [(dy, w_stk)], NT, (N_CHIPS, SEQ // TM),
               [pl.BlockSpec((TM, D_MODEL), lambda j, i: (i, 0))],
               [pl.BlockSpec((None, shard, D_MODEL), lambda j, i: (j, 0, 0))],
               pl.BlockSpec((TM, shard), lambda j, i: (i, j)),
               jax.ShapeDtypeStruct((SEQ, N_CHIPS * shard), out_dtype))


def _bwd_w_colsharded(name, x, dy):
    tm = 1024
    return _mm(name, [(x, dy)], TN, (N_CHIPS, D_MODEL // tm),
               [pl.BlockSpec((SEQ, tm), lambda j, i: (0, i))],
               [pl.BlockSpec((SEQ, FF_SHARD), lambda j, i: (0, j))],
               pl.BlockSpec((None, tm, FF_SHARD), lambda j, i: (j, i, 0)),
               jax.ShapeDtypeStruct((N_CHIPS, D_MODEL, FF_SHARD), BF16))


def _bwd_w_rowsharded(name, a, dy, shard):
    tn = 1024
    return _mm(name, [(a, dy)], TN, (N_CHIPS, D_MODEL // tn),
               [pl.BlockSpec((SEQ, shard), lambda j, i: (0, j))],
               [pl.BlockSpec((SEQ, tn), lambda j, i: (0, i))],
               pl.BlockSpec((None, shard, tn), lambda j, i: (j, 0, i)),
               jax.ShapeDtypeStruct((N_CHIPS, shard, D_MODEL), BF16))


TR = 256


def _ln_fwd(name, x, y, g, b):
    def body(x_ref, y_ref, g_ref, b_ref, o_ref, ob_ref):
        z = ALPHA * x_ref[...] + y_ref[...]
        mu = jnp.mean(z, -1, keepdims=True)
        zc = z - mu
        var = jnp.mean(zc * zc, -1, keepdims=True)
        o = zc * lax.rsqrt(var + LN_EPS) * g_ref[...] + b_ref[...]
        o_ref[...] = o
        ob_ref[...] = o.astype(BF16)

    row = pl.BlockSpec((TR, D_MODEL), lambda i: (i, 0))
    vec = pl.BlockSpec((1, D_MODEL), lambda i: (0, 0))
    return pl.pallas_call(body, name=name, grid=(SEQ // TR,), in_specs=[row, row, vec, vec], out_specs=[row, row],
                          out_shape=[jax.ShapeDtypeStruct((SEQ, D_MODEL), F32), jax.ShapeDtypeStruct((SEQ, D_MODEL), BF16)],
                          compiler_params=_cp(("parallel",)))(x, y, g.reshape(1, -1), b.reshape(1, -1))


def _to_bf16(name, x, run_after):
    def body(x_ref, _, o_ref):
        o_ref[...] = x_ref[...].astype(BF16)

    row = pl.BlockSpec((TR, D_MODEL), lambda i: (i, 0))
    return pl.pallas_call(body, name=name, grid=(SEQ // TR,), in_specs=[row, pl.BlockSpec(memory_space=pl.ANY)],
                          out_specs=row, out_shape=jax.ShapeDtypeStruct((SEQ, D_MODEL), BF16),
                          compiler_params=_cp(("parallel",)))(x, run_after)


def _ln_bwd(name, x, y, g, d_res, d_path, run_after):
    def body(x_ref, y_ref, g_ref, r_ref, p_ref, _, dz_ref, dzb_ref, dg_ref, db_ref):
        dout = ALPHA * r_ref[...] + p_ref[...]
        z = ALPHA * x_ref[...] + y_ref[...]
        mu = jnp.mean(z, -1, keepdims=True)
        zc = z - mu
        rstd = lax.rsqrt(jnp.mean(zc * zc, -1, keepdims=True) + LN_EPS)
        zh = zc * rstd
        dzh = dout * g_ref[...]
        dz = rstd * (dzh - jnp.mean(dzh, -1, keepdims=True) - zh * jnp.mean(dzh * zh, -1, keepdims=True))
        dz_ref[...] = dz
        dzb_ref[...] = dz.astype(BF16)
        pg = jnp.sum(dout * zh, 0, keepdims=True)
        pb = jnp.sum(dout, 0, keepdims=True)

        @pl.when(pl.program_id(0) == 0)
        def _():
            dg_ref[...] = pg
            db_ref[...] = pb

        @pl.when(pl.program_id(0) > 0)
        def _():
            dg_ref[...] += pg
            db_ref[...] += pb

    row = pl.BlockSpec((TR, D_MODEL), lambda i: (i, 0))
    vec = pl.BlockSpec((1, D_MODEL), lambda i: (0, 0))
    args = [x, y, g.reshape(1, -1), d_res, d_path, run_after]
    in_specs = [row, row, vec, row, row, pl.BlockSpec(memory_space=pl.ANY)]
    vshape = jax.ShapeDtypeStruct((1, D_MODEL), F32)
    return pl.pallas_call(body, name=name, grid=(SEQ // TR,), in_specs=in_specs, out_specs=[row, row, vec, vec],
                          out_shape=[jax.ShapeDtypeStruct((SEQ, D_MODEL), F32), jax.ShapeDtypeStruct((SEQ, D_MODEL), BF16),
                                     vshape, vshape],
                          compiler_params=_cp(("arbitrary",)))(*args)


def _ln_loss_bwd(name, x, y, g, b, target):
    def body(x_ref, y_ref, g_ref, b_ref, t_ref, dz_ref, dzb_ref, dg_ref, db_ref, l_ref):
        z = ALPHA * x_ref[...] + y_ref[...]
        mu = jnp.mean(z, -1, keepdims=True)
        zc = z - mu
        rstd = lax.rsqrt(jnp.mean(zc * zc, -1, keepdims=True) + LN_EPS)
        zh = zc * rstd
        e = zh * g_ref[...] + b_ref[...] - t_ref[...]
        dout = e * (1.0 / D_MODEL)
        dzh = dout * g_ref[...]
        dz = rstd * (dzh - jnp.mean(dzh, -1, keepdims=True) - zh * jnp.mean(dzh * zh, -1, keepdims=True))
        dz_ref[...] = dz
        dzb_ref[...] = dz.astype(BF16)
        pg = jnp.sum(dout * zh, 0, keepdims=True)
        pb = jnp.sum(dout, 0, keepdims=True)
        part = jnp.full((8, BLK), 0.5 / D_MODEL * jnp.sum(e * e), F32)

        @pl.when(pl.program_id(0) == 0)
        def _():
            dg_ref[...] = pg
            db_ref[...] = pb
            l_ref[...] = part

        @pl.when(pl.program_id(0) > 0)
        def _():
            dg_ref[...] += pg
            db_ref[...] += pb
            l_ref[...] += part

    row = pl.BlockSpec((TR, D_MODEL), lambda i: (i, 0))
    vec = pl.BlockSpec((1, D_MODEL), lambda i: (0, 0))
    vshape = jax.ShapeDtypeStruct((1, D_MODEL), F32)
    return pl.pallas_call(body, name=name, grid=(SEQ // TR,), in_specs=[row, row, vec, vec, row],
                          out_specs=[row, row, vec, vec, pl.BlockSpec((8, BLK), lambda i: (0, 0))],
                          out_shape=[jax.ShapeDtypeStruct((SEQ, D_MODEL), F32), jax.ShapeDtypeStruct((SEQ, D_MODEL), BF16),
                                     vshape, vshape, jax.ShapeDtypeStruct((8, BLK), F32)],
                          compiler_params=_cp(("arbitrary",)))(x, y, g.reshape(1, -1), b.reshape(1, -1), target)


def _axpy(name, a, b):
    def body(a_ref, b_ref, o_ref):
        o_ref[...] = ALPHA * a_ref[...] + b_ref[...]

    row = pl.BlockSpec((TR, D_MODEL), lambda i: (i, 0))
    return pl.pallas_call(body, name=name, grid=(SEQ // TR,), in_specs=[row, row], out_specs=row,
                          out_shape=jax.ShapeDtypeStruct((SEQ, D_MODEL), F32),
                          compiler_params=_cp(("parallel",)))(a, b)


TC = 512


def _shift_down(x, s, rows):
    if s == 0:
        return x
    return jnp.where(rows >= s, pltpu.roll(x, s, axis=0), 0.0)


def _shift_up(x, s, rows):
    if s == 0:
        return x
    return jnp.where(rows < SEQ - s, pltpu.roll(x, SEQ - s, axis=0), 0.0)


def _conv_gate_fwd(name, g, u, cw, cb):
    def body(g_ref, u_ref, w_ref, b_ref, h_ref):
        gg = g_ref[...].astype(F32)
        rows = lax.broadcasted_iota(jnp.int32, gg.shape, 0)
        gc = b_ref[...] + w_ref[2:3, :] * gg
        gc = gc + w_ref[1:2, :] * _shift_down(gg, 1, rows)
        gc = gc + w_ref[0:1, :] * _shift_down(gg, 2, rows)
        h_ref[...] = (gc * _gate_sigmoid(gc) * u_ref[...].astype(F32)).astype(BF16)

    col = pl.BlockSpec((SEQ, TC), lambda j: (0, j))
    return pl.pallas_call(body, name=name, grid=(D_FF // TC,),
                          in_specs=[col, col, pl.BlockSpec((CONV_WIDTH, TC), lambda j: (0, j)),
                                    pl.BlockSpec((1, TC), lambda j: (0, j))],
                          out_specs=col, out_shape=jax.ShapeDtypeStruct((SEQ, D_FF), BF16),
                          compiler_params=_cp(("parallel",)))(g, u, cw, cb.reshape(1, -1))


def _conv_gate_bwd(name, g, u, cw, cb, dh, run_after=None):
    def body(g_ref, u_ref, w_ref, b_ref, dh_ref, *rest):
        dg_ref, du_ref, dw_ref, db_ref = rest[-4:]
        gg = g_ref[...].astype(F32)
        rows = lax.broadcasted_iota(jnp.int32, gg.shape, 0)
        g1 = _shift_down(gg, 1, rows)
        g2 = _shift_down(gg, 2, rows)
        gc = b_ref[...] + w_ref[2:3, :] * gg + w_ref[1:2, :] * g1 + w_ref[0:1, :] * g2
        sg = _gate_sigmoid(gc)
        act = gc * sg
        dh = dh_ref[...].astype(F32)
        du_ref[...] = (dh * act).astype(BF16)
        dgc = dh * u_ref[...].astype(F32) * (sg * (1.0 + gc * (1.0 - sg)))
        db_ref[...] = jnp.sum(dgc, 0, keepdims=True)
        dw_ref[2:3, :] = jnp.sum(dgc * gg, 0, keepdims=True)
        dw_ref[1:2, :] = jnp.sum(dgc * g1, 0, keepdims=True)
        dw_ref[0:1, :] = jnp.sum(dgc * g2, 0, keepdims=True)
        dg_ref[...] = (w_ref[2:3, :] * dgc + w_ref[1:2, :] * _shift_up(dgc, 1, rows)
                       + w_ref[0:1, :] * _shift_up(dgc, 2, rows)).astype(BF16)

    col = pl.BlockSpec((SEQ, TC), lambda j: (0, j))
    w3 = pl.BlockSpec((CONV_WIDTH, TC), lambda j: (0, j))
    w1 = pl.BlockSpec((1, TC), lambda j: (0, j))
    big = jax.ShapeDtypeStruct((SEQ, D_FF), BF16)
    extra = [] if run_after is None else [run_after]
    return pl.pallas_call(body, name=name, grid=(D_FF // TC,),
                          in_specs=[col, col, w3, w1, col] + [pl.BlockSpec(memory_space=pl.ANY)] * len(extra),
                          out_specs=[col, col, w3, w1],
                          out_shape=[big, big, jax.ShapeDtypeStruct((CONV_WIDTH, D_FF), F32),
                                     jax.ShapeDtypeStruct((1, D_FF), F32)],
                          compiler_params=_cp(("parallel",)))(g, u, cw, cb.reshape(1, -1), dh, *extra)


def _lbs_of(logits, layer):
    m = jnp.max(logits, 0, keepdims=True)
    e = jnp.exp(logits - m)
    p = e / jnp.sum(e, 0, keepdims=True)
    lb = jnp.zeros((1, BLK), F32)
    for r in range(1, layer + 1):
        lb = lb + p[r:r + 1, :]
    return lb, p


def _dlogits_of(p, dlb, layer):
    rows = lax.broadcasted_iota(jnp.int32, p.shape, 0)
    dp = jnp.where((rows >= 1) & (rows <= layer), dlb, 0.0)
    return p * (dp - jnp.sum(p * dp, 0, keepdims=True))


SROWS = SLAB * A_CHUNK
N_SLAB = N_CHUNK // SLAB


def _chunk_prefix(x, rowi):
    for s in (1, 2, 4, 8):
        x = x + jnp.where(rowi >= s, pltpu.roll(x, s, axis=0), 0.0)
    return x


def _chunk_suffix(x, rowi):
    for s in (1, 2, 4, 8):
        x = x + jnp.where(rowi < A_CHUNK - s, pltpu.roll(x, SROWS - s, axis=0), 0.0)
    return x


def _c3(x):
    return x.reshape(SLAB, A_CHUNK, BLK)


def _c2(x):
    return x.reshape(SROWS, BLK)


def _split(x):
    top = lax.bitcast_convert_type(lax.bitcast_convert_type(x, jnp.uint32) & jnp.uint32(0xFFFF0000), F32)
    return top.astype(BF16), (x - top).astype(BF16)


def _bmm(eq, a, b):
    ah, al = _split(a)
    bh, bl = _split(b)

    def mm(u, v):
        return jnp.einsum(eq, u, v, preferred_element_type=F32)

    return mm(ah, bh) + (mm(ah, bl) + mm(al, bh))


def _bmm_1pass(eq, a, b):
    return jnp.einsum(eq, a.astype(BF16), b.astype(BF16), preferred_element_type=F32)


def _slab_rows(s):
    return pl.ds(s * SROWS, SROWS)


def _hgrn_prep(q, f, lb):
    rowi = lax.broadcasted_iota(jnp.int32, (SROWS, BLK), 0) & (A_CHUNK - 1)
    sq = _gate_sigmoid(q)
    qc = q * sq
    sf = _sigmoid(f)
    fg = lb + (1.0 - lb) * sf
    kc = 1.0 - fg
    b = _chunk_prefix(jnp.log(fg), rowi)
    b3 = _c3(b)
    blast = b3[:, A_CHUNK - 1:A_CHUNK, :]
    eb = jnp.exp(b)
    ekb = _c2(jnp.exp(blast - b3))
    dec = jnp.exp(blast.reshape(SLAB, BLK))
    return rowi, sq, qc, sf, fg, kc, b, eb, ekb, dec


def _hgrn_slab_states(s, carry, v, ke, dec, dec_ref, u_ref, st_ref):
    dec_ref[pl.ds(s * SLAB, SLAB), :] = dec
    u_ref[...] = _bmm('ncv,nck->nvk', _c3(v), _c3(ke))

    def step(j, c):
        st_ref[j] = c
        return dec_ref[pl.ds(s * SLAB + j, 1), :] * c + u_ref[j]

    return lax.fori_loop(0, SLAB, step, carry)


def _hgrn_fwd(name, proj, lb_logits, nw, layer):
    def body(q_ref, f_ref, i_ref, g_ref, lg_ref, nw_ref, out_ref, outb_ref, raw_ref, dec_ref, u_ref, st_ref):
        lb, _ = _lbs_of(lg_ref[...], layer)
        ones = jnp.ones((BLK, BLK), BF16)
        carry = jnp.zeros((BLK, BLK), F32)
        for s in range(N_SLAB):
            rows = _slab_rows(s)
            v = i_ref[rows, :]
            rowi, sq, qc, sf, fg, kc, b, eb, ekb, dec = _hgrn_prep(q_ref[rows, :], f_ref[rows, :], lb)
            carry = _hgrn_slab_states(s, carry, v, kc * ekb, dec, dec_ref, u_ref, st_ref)
            o = _c2(_bmm('nck,nvk->ncv', _c3(qc * eb), st_ref[...]))
            qc3, kc3, b3, v3, row3 = _c3(qc), _c3(kc), _c3(b), _c3(v), _c3(rowi)
            col = lax.broadcasted_iota(jnp.int32, (SLAB, A_CHUNK, A_CHUNK), 2)
            att_all = jnp.zeros((SLAB, A_CHUNK, A_CHUNK), F32)
            for j in range(A_CHUNK):
                dj = jnp.exp(jnp.where(row3 >= j, b3 - b3[:, j:j + 1, :], -jnp.inf))
                a = jnp.dot(_c2(qc3 * dj * kc3[:, j:j + 1, :]).astype(BF16), ones, preferred_element_type=F32)
                att_all = jnp.where(col == j, _c3(a)[:, :, :A_CHUNK], att_all)
            o = o + _c2(_bmm_1pass('nij,njv->niv', att_all, v3))
            raw_ref[rows, :] = o
            r = lax.rsqrt(jnp.mean(o * o, -1, keepdims=True) + LN_EPS)
            gg = g_ref[rows, :]
            gated = o * r * nw_ref[...] * (gg * _gate_sigmoid(gg))
            out_ref[rows, :] = gated
            outb_ref[rows, :] = gated.astype(BF16)

    def colblk(c0):
        return pl.BlockSpec((SEQ, BLK), lambda h: (0, c0 + h))

    return pl.pallas_call(
        body, name=name, grid=(A_HEADS,),
        in_specs=[colblk(QA0), colblk(FA0), colblk(IA0), colblk(GA0),
                  pl.BlockSpec((DEPTH, BLK), lambda h: (0, h)), pl.BlockSpec((1, BLK), lambda h: (0, 0))],
        out_specs=[colblk(0), colblk(0), colblk(0)],
        out_shape=[jax.ShapeDtypeStruct((SEQ, MIX_WIDTH), F32), jax.ShapeDtypeStruct((SEQ, MIX_WIDTH), BF16),
                   jax.ShapeDtypeStruct((SEQ, A_HEADS * BLK), F32)],
        scratch_shapes=[pltpu.VMEM((N_CHUNK, BLK), F32), pltpu.VMEM((SLAB, BLK, BLK), F32),
                        pltpu.VMEM((SLAB, BLK, BLK), F32)],
        compiler_params=_cp(("parallel",)))(proj, proj, proj, proj, lb_logits, nw.reshape(1, -1))


def _col_block_copies(stage, sems, dst, col_blocks, first):
    return [pltpu.make_async_copy(stage.at[first + t], dst.at[:, pl.ds(pl.multiple_of(cb * BLK, BLK), BLK)],
                                  sems.at[first + t]) for t, cb in enumerate(col_blocks)]


def _start_col_blocks(stage, sems, dst, col_blocks, first=0):
    for cp in _col_block_copies(stage, sems, dst, col_blocks, first):
        cp.start()


def _wait_col_blocks(stage, sems, dst, count, first=0):
    for cp in _col_block_copies(stage, sems, dst, [0] * count, first):
        cp.wait()


def _hgrn_bwd(name, proj, raw, dmix, lb_logits, nw, layer, run_after=None):
    extra = [] if run_after is None else [run_after]

    def body(q_ref, f_ref, i_ref, g_ref, raw_ref, do_ref, lg_ref, nw_ref, *rest):
        (dproj_ref, dnw_ref, dlg_ref,
         dec_ref, u_ref, st_ref, h_ref, dbs_ref, dkc_ref, tot_ref, stage, stage_sem) = rest[-12:]
        dq_ref, df_ref, di_ref, dg_ref = (stage.at[t] for t in range(4))
        lb, p = _lbs_of(lg_ref[...], layer)
        ones = jnp.ones((BLK, BLK), BF16)
        nwv = nw_ref[...]

        carry = jnp.zeros((BLK, BLK), F32)
        for s in range(N_SLAB):
            rows = _slab_rows(s)
            rowi, sq, qc, sf, fg, kc, b, eb, ekb, dec = _hgrn_prep(q_ref[rows, :], f_ref[rows, :], lb)
            carry = _hgrn_slab_states(s, carry, i_ref[rows, :], kc * ekb, dec, dec_ref, u_ref,
                                      st_ref.at[pl.ds(s * SLAB, SLAB)])

        @pl.when(pl.program_id(0) > 0)
        def _():
            _wait_col_blocks(stage, stage_sem, dproj_ref, 4)

        carry = jnp.zeros((BLK, BLK), F32)
        dnw = jnp.zeros((1, BLK), F32)
        for s in reversed(range(N_SLAB)):
            rows = _slab_rows(s)
            q, v = q_ref[rows, :], i_ref[rows, :]
            rowi, sq, qc, sf, fg, kc, b, eb, ekb, dec = _hgrn_prep(q, f_ref[rows, :], lb)
            ke = kc * ekb
            qe = qc * eb

            o = raw_ref[rows, :]
            gg = g_ref[rows, :]
            sgg = _gate_sigmoid(gg)
            dout = do_ref[rows, :]
            r = lax.rsqrt(jnp.mean(o * o, -1, keepdims=True) + LN_EPS)
            oh = o * r
            dg_ref[rows, :] = (dout * oh * nwv * (sgg * (1.0 + gg * (1.0 - sgg)))).astype(BF16)
            dn = dout * (gg * sgg)
            dnw = dnw + jnp.sum(dn * oh, 0, keepdims=True)
            doh = dn * nwv
            do = r * (doh - oh * jnp.mean(doh * oh, -1, keepdims=True))
            do3, qe3, v3, ke3 = _c3(do), _c3(qe), _c3(v), _c3(ke)

            u_ref[...] = _bmm('ncv,nck->nvk', do3, qe3)

            def step(jj, c, s=s):
                j = SLAB - 1 - jj
                h_ref[j] = c
                return u_ref[j] + dec_ref[pl.ds(s * SLAB + j, 1), :] * c

            carry = lax.fori_loop(0, SLAB, step, carry)

            hh = h_ref[...]
            dqc = _c2(_bmm('ncv,nvk->nck', do3, st_ref[pl.ds(s * SLAB, SLAB)])) * eb
            dkc = _c2(_bmm('ncv,nvk->nck', v3, hh)) * ekb
            dv = _c2(_bmm_1pass('nck,nvk->ncv', ke3, hh))

            qc3, kc3, b3, row3 = _c3(qc), _c3(kc), _c3(b), _c3(rowi)
            datt_all = _bmm('niv,njv->nij', do3, v3)
            col = lax.broadcasted_iota(jnp.int32, datt_all.shape, 2)
            att_all = jnp.zeros_like(datt_all)
            for j in range(A_CHUNK):
                dj = jnp.exp(jnp.where(row3 >= j, b3 - b3[:, j:j + 1, :], -jnp.inf))
                kj = kc3[:, j:j + 1, :]
                att = _c3(jnp.dot(_c2(qc3 * dj * kj).astype(BF16), ones, preferred_element_type=F32))
                att_all = jnp.where(col == j, att[:, :, :A_CHUNK], att_all)
                md = dj * datt_all[:, :, j:j + 1]
                dqc = dqc + _c2(md * kj)
                dkc = dkc + _c2(jnp.where(row3 == j, jnp.sum(md * qc3, 1, keepdims=True), 0.0))
            dv = dv + _c2(_bmm_1pass('nij,niv->njv', att_all, do3))
            di_ref[rows, :] = dv.astype(BF16)
            dq_ref[rows, :] = (dqc * (sq * (1.0 + q * (1.0 - sq)))).astype(BF16)

            dbs = _chunk_suffix(qc * dqc - kc * dkc, rowi)
            dbs_ref[rows, :] = dbs
            dkc_ref[rows, :] = dkc
            tot_ref[pl.ds(s * SLAB, SLAB), :] = _c3(dbs)[:, 0:1, :].reshape(SLAB, BLK)
        dnw_ref[...] = jnp.broadcast_to(dnw, (8, BLK))

        rn = lax.broadcasted_iota(jnp.int32, (N_CHUNK, N_CHUNK), 0)
        cn = lax.broadcasted_iota(jnp.int32, (N_CHUNK, N_CHUNK), 1)
        tot_ref[...] = jnp.dot((cn > rn).astype(F32), tot_ref[...], preferred_element_type=F32, precision=HI)
        dlb = jnp.zeros((1, BLK), F32)
        for s in range(N_SLAB):
            rows = _slab_rows(s)
            sf = _sigmoid(f_ref[rows, :])
            fg = lb + (1.0 - lb) * sf
            later = tot_ref[pl.ds(s * SLAB, SLAB), :]
            dlg = _c2(_c3(dbs_ref[rows, :]) + later[:, None, :])
            dfg = dlg / fg - dkc_ref[rows, :]
            df_ref[rows, :] = (dfg * (1.0 - lb) * sf * (1.0 - sf)).astype(BF16)
            dlb = dlb + jnp.sum(dfg * (1.0 - sf), 0, keepdims=True)
        dlg_ref[...] = _dlogits_of(p, dlb, layer)
        _start_col_blocks(stage, stage_sem, dproj_ref, [c0 + pl.program_id(0) for c0 in (QA0, FA0, IA0, GA0)])

        @pl.when(pl.program_id(0) == A_HEADS - 1)
        def _():
            _wait_col_blocks(stage, stage_sem, dproj_ref, 4)

    def colblk(c0):
        return pl.BlockSpec((SEQ, BLK), lambda h: (0, c0 + h))

    return pl.pallas_call(
        body, name=name, grid=(A_HEADS,),
        in_specs=[colblk(QA0), colblk(FA0), colblk(IA0), colblk(GA0), colblk(0), colblk(0),
                  pl.BlockSpec((DEPTH, BLK), lambda h: (0, h)), pl.BlockSpec((1, BLK), lambda h: (0, 0))]
        + [pl.BlockSpec(memory_space=pl.ANY)] * len(extra),
        out_specs=[pl.BlockSpec(memory_space=pl.ANY),
                   pl.BlockSpec((8, BLK), lambda h: (h, 0)), pl.BlockSpec((DEPTH, BLK), lambda h: (0, h))],
        out_shape=[jax.ShapeDtypeStruct((SEQ, IN_WIDTH), BF16), jax.ShapeDtypeStruct((A_HEADS * 8, BLK), F32),
                   jax.ShapeDtypeStruct((DEPTH, A_HEADS * BLK), F32)],
        scratch_shapes=[pltpu.VMEM((N_CHUNK, BLK), F32), pltpu.VMEM((SLAB, BLK, BLK), F32),
                        pltpu.VMEM((N_CHUNK, BLK, BLK), F32), pltpu.VMEM((SLAB, BLK, BLK), F32),
                        pltpu.VMEM((SEQ, BLK), F32), pltpu.VMEM((SEQ, BLK), F32), pltpu.VMEM((N_CHUNK, BLK), F32),
                        pltpu.VMEM((4, SEQ, BLK), BF16), pltpu.SemaphoreType.DMA((4,))],
        compiler_params=_cp(("arbitrary",)))(proj, proj, proj, proj, raw, dmix, lb_logits, nw.reshape(1, -1), *extra)


SCALE = HEAD_DIM ** -0.5


def _rope_tables():
    half = ROPE_DIM // 2
    inv = ROPE_THETA ** (-jnp.arange(0, ROPE_DIM, 2, dtype=F32) / ROPE_DIM)
    ang = jnp.arange(SEQ, dtype=F32)[:, None] * inv[None, :]
    cos, sin = jnp.cos(ang), jnp.sin(ang)
    pad = jnp.zeros((SEQ, HEAD_DIM - ROPE_DIM), F32)
    zero = jnp.zeros((SEQ, half), F32)
    c = jnp.concatenate([cos, cos, pad + 1.0], 1)
    s_lo = jnp.concatenate([zero, sin, pad], 1)
    s_hi = jnp.concatenate([-sin, zero, pad], 1)
    return c, s_lo, s_hi


def _rope(x, c, s_lo, s_hi):
    half = ROPE_DIM // 2
    return x * c + pltpu.roll(x, half, axis=1) * s_lo + pltpu.roll(x, HEAD_DIM - half, axis=1) * s_hi


def _unrope(dy, c, s_lo, s_hi):
    half = ROPE_DIM // 2
    return dy * c + pltpu.roll(dy * s_lo, HEAD_DIM - half, axis=1) + pltpu.roll(dy * s_hi, half, axis=1)


N_BLK = SEQ // BLK


def _block_rows(dil):
    nb = N_BLK // dil
    return [pl.ds(r + n * BLK * dil, BLK, stride=dil) for r in range(dil) for n in range(nb)]


def _to_blocks(ref, dil):
    if dil == 1:
        return ref[...].reshape(N_BLK, BLK, BLK)
    return jnp.stack([ref[rows, :] for rows in _block_rows(dil)], 0)


def _from_blocks(ref, val, dil, add=False):
    if dil == 1:
        flat = val.reshape(SEQ, BLK)
        ref[...] = ref[...] + flat if add else flat
        return
    for b, rows in enumerate(_block_rows(dil)):
        ref[rows, :] = ref[rows, :] + val[b] if add else val[b]


def _prev_block(x):
    return jnp.concatenate([x[:1], x[:-1]], axis=0)


def _to_next_block(x):
    return jnp.concatenate([x[1:], jnp.zeros_like(x[:1])], axis=0)


def _band_masks(max_lag, dil):
    r = lax.broadcasted_iota(jnp.int32, (N_BLK, BLK, BLK), 1)
    c = lax.broadcasted_iota(jnp.int32, (N_BLK, BLK, BLK), 2)
    b = lax.broadcasted_iota(jnp.int32, (N_BLK, BLK, BLK), 0)
    has_prev = (b % (N_BLK // dil)) != 0
    return r >= c, has_prev & (BLK + r - c <= max_lag)


def _bdot(eq, a, b):
    return jnp.einsum(eq, a, b, preferred_element_type=F32)


def _attn_fwd(name, proj, tables, sink_b, mixed, mixed_bf, *, n_heads, rep, q0, k0, v0, m0, patterns):
    n_pat = len(patterns)
    has_sink = sink_b is not None

    def body(*refs):
        o_ref, ob_ref, l_ref, qr, kr, op, lse_ref = refs[-7:]
        q_ref, k_ref, v_ref, c_ref, sl_ref, sh_ref = refs[:6]
        if has_sink:
            sk = refs[6][0:1, 0:1]
        c, s_lo, s_hi = c_ref[...], sl_ref[...], sh_ref[...]
        qr[...] = _rope(q_ref[...], c, s_lo, s_hi)
        kr[...] = _rope(k_ref[...], c, s_lo, s_hi)
        for p, (max_lag, dil) in enumerate(patterns):
            qa = _to_blocks(qr, dil).astype(BF16)
            ka = _to_blocks(kr, dil).astype(BF16)
            va = _to_blocks(v_ref, dil).astype(BF16)
            own, before = _band_masks(max_lag, dil)
            s1 = jnp.where(own, _bdot('nqd,nkd->nqk', qa, ka) * SCALE, -jnp.inf)
            m = jnp.max(s1, -1, keepdims=True)
            with_prev = dil < N_BLK
            if with_prev:
                kp, vp = _prev_block(ka), _prev_block(va)
                s0 = jnp.where(before, _bdot('nqd,nkd->nqk', qa, kp) * SCALE, -jnp.inf)
                m = jnp.maximum(m, jnp.max(s0, -1, keepdims=True))
            if has_sink:
                m = jnp.maximum(m, sk)
            e1 = jnp.exp(s1 - m)
            den = jnp.sum(e1, -1, keepdims=True)
            o = _bdot('nqk,nkd->nqd', e1.astype(BF16), va)
            if with_prev:
                e0 = jnp.exp(s0 - m)
                den = den + jnp.sum(e0, -1, keepdims=True)
                o = o + _bdot('nqk,nkd->nqd', e0.astype(BF16), vp)
            if has_sink:
                den = den + jnp.exp(sk - m)
            _from_blocks(op.at[p], o / den, dil)
            _from_blocks(lse_ref.at[p], jnp.broadcast_to(m + jnp.log(den), (N_BLK, BLK, BLK)), dil)
        if n_pat == 1:
            acc = op[0]
            l_ref[...] = lse_ref[0]
        else:
            ls = [lse_ref[p] for p in range(n_pat)]
            m = functools.reduce(jnp.maximum, ls)
            es = [jnp.exp(l - m) for l in ls]
            tot = functools.reduce(jnp.add, es)
            acc = None
            for p in range(n_pat):
                t = (es[p] / tot) * op[p]
                acc = t if acc is None else acc + t
            l_ref[...] = m + jnp.log(tot)
        o_ref[...] = acc
        ob_ref[...] = acc.astype(BF16)

    def colblk(fn):
        return pl.BlockSpec((SEQ, BLK), fn)

    tab = pl.BlockSpec((SEQ, BLK), lambda h: (0, 0))
    in_specs = [colblk(lambda h: (0, q0 + h)), colblk(lambda h: (0, k0 + h // rep)), colblk(lambda h: (0, v0 + h // rep)),
                tab, tab, tab]
    args = [proj, proj, proj, *tables]
    if has_sink:
        in_specs.append(pl.BlockSpec((None, 8, BLK), lambda h: (h, 0, 0)))
        args.append(sink_b)
    n_in = len(args)
    in_specs += [pl.BlockSpec(memory_space=pl.ANY)] * 2
    args += [mixed, mixed_bf]
    pat = pltpu.VMEM((n_pat, SEQ, BLK), F32)
    return pl.pallas_call(
        body, name=name, grid=(n_heads,), in_specs=in_specs,
        out_specs=[colblk(lambda h: (0, m0 + h)), colblk(lambda h: (0, m0 + h)),
                   pl.BlockSpec((None, SEQ, BLK), lambda h: (h, 0, 0))],
        out_shape=[jax.ShapeDtypeStruct(mixed.shape, F32), jax.ShapeDtypeStruct(mixed.shape, BF16),
                   jax.ShapeDtypeStruct((n_heads, SEQ, BLK), F32)],
        input_output_aliases={n_in: 0, n_in + 1: 1},
        scratch_shapes=[pltpu.VMEM((SEQ, BLK), F32), pltpu.VMEM((SEQ, BLK), F32), pat, pat],
        compiler_params=_cp(("parallel",)))(*args)


def _attn_bwd(name, proj, mixed, dmix, lse, tables, sink_b, dproj, *, n_kv, rep, q0, k0, v0, m0, patterns):
    n_heads = n_kv * rep
    has_sink = sink_b is not None

    def body(*refs):
        q_ref, k_ref, v_ref, o_ref, do_ref, lse_ref, c_ref, sl_ref, sh_ref = refs[:9]
        sink_ref = refs[9] if has_sink else None
        dproj_ref, dsk_ref, qr, kr, dqa, dka, dva, dd, stage, stage_sem = refs[-10:]
        g, j = pl.program_id(0), pl.program_id(1)
        c, s_lo, s_hi = c_ref[...], sl_ref[...], sh_ref[...]
        qr[...] = _rope(q_ref[...], c, s_lo, s_hi)
        kr[...] = _rope(k_ref[...], c, s_lo, s_hi)
        dcol = jnp.sum(do_ref[...] * o_ref[...], -1, keepdims=True)
        dd[...] = jnp.broadcast_to(dcol, (SEQ, BLK))

        @pl.when(j == 0)
        def _():
            dka[...] = jnp.zeros((SEQ, BLK), F32)
            dva[...] = jnp.zeros((SEQ, BLK), F32)

        for p, (max_lag, dil) in enumerate(patterns):
            qa = _to_blocks(qr, dil).astype(BF16)
            ka = _to_blocks(kr, dil).astype(BF16)
            va = _to_blocks(v_ref, dil).astype(BF16)
            doa = _to_blocks(do_ref, dil).astype(BF16)
            lcol = _to_blocks(lse_ref, dil)[:, :, 0:1]
            dcb = _to_blocks(dd, dil)[:, :, 0:1]
            own, before = _band_masks(max_lag, dil)

            def probs_and_ds(kk, vv, valid):
                s = _bdot('nqd,nkd->nqk', qa, kk) * SCALE
                a = jnp.where(valid, jnp.exp(s - lcol), 0.0)
                ds = a * (_bdot('nqd,nkd->nqk', doa, vv) - dcb) * SCALE
                return a.astype(BF16), ds.astype(BF16)

            a1, ds1 = probs_and_ds(ka, va, own)
            dq = _bdot('nqk,nkd->nqd', ds1, ka)
            dk = _bdot('nqk,nqd->nkd', ds1, qa)
            dv = _bdot('nqk,nqd->nkd', a1, doa)
            if dil < N_BLK:
                kp, vp = _prev_block(ka), _prev_block(va)
                a0, ds0 = probs_and_ds(kp, vp, before)
                dq = dq + _bdot('nqk,nkd->nqd', ds0, kp)
                dk = dk + _to_next_block(_bdot('nqk,nqd->nkd', ds0, qa))
                dv = dv + _to_next_block(_bdot('nqk,nqd->nkd', a0, doa))
            _from_blocks(dqa, dq, dil, add=p > 0)
            _from_blocks(dka, dk, dil, add=True)
            _from_blocks(dva, dv, dil, add=True)

        if has_sink:
            sk = sink_ref[0:1, 0:1]
            ps = jnp.exp(sk - lse_ref[...][:, 0:1])
            dsk_ref[...] = jnp.full((8, BLK), -jnp.sum(ps * dcol), F32)
        else:
            dsk_ref[...] = jnp.zeros((8, BLK), F32)
        @pl.when(g * rep + j > 0)
        def _():
            _wait_col_blocks(stage, stage_sem, dproj_ref, 1)

        stage[0] = _unrope(dqa[...], c, s_lo, s_hi).astype(BF16)
        _start_col_blocks(stage, stage_sem, dproj_ref, [q0 + g * rep + j])

        @pl.when(j == rep - 1)
        def _():
            @pl.when(g > 0)
            def _():
                _wait_col_blocks(stage, stage_sem, dproj_ref, 2, first=1)

            stage[1] = _unrope(dka[...], c, s_lo, s_hi).astype(BF16)
            stage[2] = dva[...].astype(BF16)
            _start_col_blocks(stage, stage_sem, dproj_ref, [k0 + g, v0 + g], first=1)

        @pl.when((g == n_kv - 1) & (j == rep - 1))
        def _():
            _wait_col_blocks(stage, stage_sem, dproj_ref, 3)

    def colblk(fn):
        return pl.BlockSpec((SEQ, BLK), fn)

    tab = pl.BlockSpec((SEQ, BLK), lambda g, j: (0, 0))
    in_specs = [colblk(lambda g, j: (0, q0 + g * rep + j)), colblk(lambda g, j: (0, k0 + g)), colblk(lambda g, j: (0, v0 + g)),
                colblk(lambda g, j: (0, m0 + g * rep + j)), colblk(lambda g, j: (0, m0 + g * rep + j)),
                pl.BlockSpec((None, SEQ, BLK), lambda g, j: (g * rep + j, 0, 0)), tab, tab, tab]
    args = [proj, proj, proj, mixed, dmix, lse, *tables]
    if has_sink:
        in_specs.append(pl.BlockSpec((None, 8, BLK), lambda g, j: (g * rep + j, 0, 0)))
        args.append(sink_b)
    n_in = len(args)
    in_specs.append(pl.BlockSpec(memory_space=pl.ANY))
    args.append(dproj)
    acc = pltpu.VMEM((SEQ, BLK), F32)
    return pl.pallas_call(
        body, name=name, grid=(n_kv, rep), in_specs=in_specs,
        out_specs=[pl.BlockSpec(memory_space=pl.ANY), pl.BlockSpec((None, 8, BLK), lambda g, j: (g * rep + j, 0, 0))],
        out_shape=[jax.ShapeDtypeStruct(dproj.shape, BF16), jax.ShapeDtypeStruct((n_heads, 8, BLK), F32)],
        input_output_aliases={n_in: 0},
        scratch_shapes=[acc, acc, acc, acc, acc, acc, pltpu.VMEM((3, SEQ, BLK), BF16), pltpu.SemaphoreType.DMA((3,))],
        compiler_params=_cp(("arbitrary", "arbitrary")))(*args)


B_PATTERNS = tuple((w // d, d) for w, d in DILATED_PATTERNS)
C_PATTERNS = ((C_WINDOW - 1, 1),)


ANY = pl.BlockSpec(memory_space=pl.ANY)
CHIP_MASKS = ((1, 0), (0, 1), (1, 1))


def _coords():
    return lax.axis_index("x"), lax.axis_index("y"), lax.axis_index("c")


def _flip(v, m):
    return 1 - v if m else v


def _into_slot(name, w, layer, k_idx, dtype, run_after=None):
    _, rows, cols = w.shape
    tr = rows // 8 if rows % 64 == 0 else rows

    def body(k_ref, w_ref, *rest):
        rest[-1][...] = w_ref[...].astype(dtype)

    in_specs = [pl.BlockSpec((None, tr, cols), lambda i, k: (layer, i, 0))]
    args = [k_idx, w]
    if run_after is not None:
        in_specs.append(pl.BlockSpec(memory_space=pl.ANY))
        args.append(run_after)
    return pl.pallas_call(
        body, name=name,
        grid_spec=pltpu.PrefetchScalarGridSpec(
            num_scalar_prefetch=1, grid=(rows // tr,), in_specs=in_specs,
            out_specs=pl.BlockSpec((None, tr, cols), lambda i, k: (k[0], i, 0))),
        out_shape=jax.ShapeDtypeStruct((N_CHIPS, rows, cols), dtype),
        compiler_params=_cp(("parallel",)))(*args)


HBM_SPEC = pl.BlockSpec(memory_space=pltpu.HBM)
SEM_SPEC = pl.BlockSpec(memory_space=pltpu.SEMAPHORE)
TOKEN_SPEC = pl.BlockSpec(memory_space=pltpu.VMEM)
TOKEN_SHAPE = jax.ShapeDtypeStruct((8, BLK), F32)
DATAFLOW = pltpu.SideEffectType.DATAFLOW_SIDE_EFFECTING


def _hbm(a):
    return pltpu.with_memory_space_constraint(a, pltpu.HBM)


def _hbm_like(bufs):
    return [pltpu.HBM(b.shape, b.dtype) for b in bufs]


def _gather_start(name, stages):
    flat = [b for st in stages for b in st]
    n, ns = len(flat), len(stages)

    def body(*refs):
        ins = refs[:n]
        sems = refs[n:n + 2 * ns]
        token = refs[-1]
        x, y, c = _coords()
        k_me = 2 * x + y
        a = 0
        for s, st in enumerate(stages):
            for i in range(len(st)):
                mine = ins[a].at[k_me, c]
                for m, (mx, my) in enumerate(CHIP_MASKS):
                    pltpu.make_async_remote_copy(src_ref=mine, dst_ref=mine, send_sem=sems[2 * s].at[i * 3 + m],
                                                 recv_sem=sems[2 * s + 1].at[i * 3 + m],
                                                 device_id=(_flip(x, mx), _flip(y, my), c), device_id_type=MESH).start()
                a += 1
        token[...] = jnp.zeros_like(token)

    sem_shapes = []
    for st in stages:
        sem_shapes += [pltpu.SemaphoreType.DMA((3 * len(st),))] * 2
    out = pl.pallas_call(
        body, name=name, in_specs=[HBM_SPEC] * n,
        out_specs=tuple([SEM_SPEC] * (2 * ns) + [HBM_SPEC] * n + [TOKEN_SPEC]),
        out_shape=tuple(sem_shapes + _hbm_like(flat) + [TOKEN_SHAPE]),
        input_output_aliases={i: 2 * ns + i for i in range(n)},
        compiler_params=pltpu.CompilerParams(has_side_effects=DATAFLOW),
    )(*[_hbm(b) for b in flat])
    sems, bufs, token = out[:2 * ns], out[2 * ns:2 * ns + n], out[-1]
    res, a = [], 0
    for s, st in enumerate(stages):
        res.append((sems[2 * s], sems[2 * s + 1], list(bufs[a:a + len(st)])))
        a += len(st)
    return res, token


def _gather_forward(name, stage, after):
    ssem_in, rsem_in, bufs = stage
    n = len(bufs)

    def body(*refs):
        ins = refs[:n]
        s_in, r_in, _ = refs[n:n + 3]
        s_out, r_out = refs[n + 3:n + 5]
        token = refs[-1]
        x, y, c = _coords()
        for i in range(n):
            for m, (mx, my) in enumerate(CHIP_MASKS):
                kp = 2 * _flip(x, mx) + _flip(y, my)
                blk = ins[i].at[kp, c]
                got = pltpu.make_async_remote_copy(src_ref=blk, dst_ref=blk, send_sem=s_in.at[i * 3 + m],
                                                   recv_sem=r_in.at[i * 3 + m], device_id=(x, y, 1 - c), device_id_type=MESH)
                got.wait_send()
                got.wait_recv()
                pltpu.make_async_remote_copy(src_ref=blk, dst_ref=blk, send_sem=s_out.at[i * 3 + m],
                                             recv_sem=r_out.at[i * 3 + m], device_id=(x, y, 1 - c), device_id_type=MESH).start()
        token[...] = jnp.zeros_like(token)

    sem = pltpu.SemaphoreType.DMA((3 * n,))
    out = pl.pallas_call(
        body, name=name, in_specs=[HBM_SPEC] * n + [SEM_SPEC, SEM_SPEC, ANY],
        out_specs=tuple([SEM_SPEC, SEM_SPEC] + [HBM_SPEC] * n + [TOKEN_SPEC]),
        out_shape=tuple([sem, sem] + _hbm_like(bufs) + [TOKEN_SHAPE]),
        input_output_aliases={i: 2 + i for i in range(n)},
        compiler_params=pltpu.CompilerParams(has_side_effects=DATAFLOW),
    )(*bufs, ssem_in, rsem_in, after)
    return (out[0], out[1], list(out[2:2 + n])), out[-1]


def _gather_wait(name, stage, after):
    ssem, rsem, bufs = stage
    n = len(bufs)

    def body(*refs):
        ins = refs[:n]
        s_in, r_in, _ = refs[n:n + 3]
        x, y, c = _coords()
        for i in range(n):
            for m, (mx, my) in enumerate(CHIP_MASKS):
                kp = 2 * _flip(x, mx) + _flip(y, my)
                sent, got = ins[i].at[kp, c], ins[i].at[kp, 1 - c]
                cp = pltpu.make_async_remote_copy(src_ref=sent, dst_ref=got, send_sem=s_in.at[i * 3 + m],
                                                  recv_sem=r_in.at[i * 3 + m], device_id=(x, y, 1 - c), device_id_type=MESH)
                cp.wait_send()
                cp.wait_recv()

    out = pl.pallas_call(
        body, name=name, in_specs=[HBM_SPEC] * n + [SEM_SPEC, SEM_SPEC, ANY],
        out_specs=tuple([HBM_SPEC] * n), out_shape=tuple(_hbm_like(bufs)),
        input_output_aliases={i: i for i in range(n)},
        compiler_params=pltpu.CompilerParams(has_side_effects=DATAFLOW),
    )(*bufs, ssem, rsem, after)
    return list(out)


def _swap_start(name, grads):
    n = len(grads)

    def body(*refs):
        ins, lands = refs[:n], refs[n:2 * n]
        ssem, rsem = refs[2 * n:2 * n + 2]
        x, y, c = _coords()
        for a in range(n):
            for j in range(N_CHIPS):
                pltpu.make_async_remote_copy(src_ref=ins[a].at[j, 1 - c], dst_ref=lands[a].at[j],
                                             send_sem=ssem.at[a * N_CHIPS + j], recv_sem=rsem.at[a * N_CHIPS + j],
                                             device_id=(x, y, 1 - c), device_id_type=MESH).start()

    sem = pltpu.SemaphoreType.DMA((N_CHIPS * n,))
    land_shapes = [pltpu.HBM((N_CHIPS,) + g.shape[2:], g.dtype) for g in grads]
    out = pl.pallas_call(
        body, name=name, in_specs=[HBM_SPEC] * (2 * n),
        out_specs=tuple([SEM_SPEC, SEM_SPEC] + [HBM_SPEC] * (2 * n)),
        out_shape=tuple([sem, sem] + _hbm_like(grads) + land_shapes),
        input_output_aliases={i: 2 + i for i in range(2 * n)},
        compiler_params=pltpu.CompilerParams(has_side_effects=DATAFLOW),
    )(*[_hbm(g) for g in grads], *[_hbm(lax.empty((N_CHIPS,) + g.shape[2:], g.dtype)) for g in grads])
    return out[0], out[1], list(out[2:2 + n]), list(out[2 + n:])


def _swap_wait(name, started, after):
    ssem, rsem, grads, lands = started
    n = len(grads)
    after = after if isinstance(after, tuple) else (after,)

    def body(*refs):
        ins, lnd = refs[:n], refs[n:2 * n]
        s_in, r_in = refs[2 * n:2 * n + 2]
        x, y, c = _coords()
        for a in range(n):
            for j in range(N_CHIPS):
                cp = pltpu.make_async_remote_copy(src_ref=ins[a].at[j, 1 - c], dst_ref=lnd[a].at[j],
                                                  send_sem=s_in.at[a * N_CHIPS + j], recv_sem=r_in.at[a * N_CHIPS + j],
                                                  device_id=(x, y, 1 - c), device_id_type=MESH)
                cp.wait_send()
                cp.wait_recv()

    out = pl.pallas_call(
        body, name=name, in_specs=[HBM_SPEC] * (2 * n) + [SEM_SPEC, SEM_SPEC] + [ANY] * len(after),
        out_specs=tuple([HBM_SPEC] * (2 * n)), out_shape=tuple(_hbm_like(grads) + _hbm_like(lands)),
        input_output_aliases={i: i for i in range(2 * n)},
        compiler_params=pltpu.CompilerParams(has_side_effects=DATAFLOW),
    )(*grads, *lands, ssem, rsem, *after)
    return list(out[:n]), list(out[n:])


def _scatter_start(name, parts):
    n = len(parts)

    def body(*refs):
        ins, lands = refs[:n], refs[n:2 * n]
        ssem, rsem = refs[2 * n:2 * n + 2]
        x, y, c = _coords()
        k_me = 2 * x + y
        for a in range(n):
            for m, (mx, my) in enumerate(CHIP_MASKS):
                px, py = _flip(x, mx), _flip(y, my)
                pltpu.make_async_remote_copy(src_ref=ins[a].at[2 * px + py], dst_ref=lands[a].at[k_me],
                                             send_sem=ssem.at[a * 3 + m], recv_sem=rsem.at[a * 3 + m],
                                             device_id=(px, py, c), device_id_type=MESH).start()

    sem = pltpu.SemaphoreType.DMA((3 * n,))
    out = pl.pallas_call(
        body, name=name, in_specs=[HBM_SPEC] * (2 * n),
        out_specs=tuple([SEM_SPEC, SEM_SPEC] + [HBM_SPEC] * (2 * n)),
        out_shape=tuple([sem, sem] + _hbm_like(parts) + _hbm_like(parts)),
        input_output_aliases={i: 2 + i for i in range(2 * n)},
        compiler_params=pltpu.CompilerParams(has_side_effects=DATAFLOW),
    )(*[_hbm(p) for p in parts], *[_hbm(lax.empty(p.shape, p.dtype)) for p in parts])
    return out[0], out[1], list(out[2:2 + n]), list(out[2 + n:])


def _scatter_wait(name, started, after):
    ssem, rsem, parts, lands = started
    n = len(parts)

    def body(*refs):
        ins, lnd = refs[:n], refs[n:2 * n]
        s_in, r_in, _ = refs[2 * n:2 * n + 3]
        x, y, c = _coords()
        k_me = 2 * x + y
        for a in range(n):
            for m, (mx, my) in enumerate(CHIP_MASKS):
                px, py = _flip(x, mx), _flip(y, my)
                cp = pltpu.make_async_remote_copy(src_ref=ins[a].at[2 * px + py], dst_ref=lnd[a].at[k_me],
                                                  send_sem=s_in.at[a * 3 + m], recv_sem=r_in.at[a * 3 + m],
                                                  device_id=(px, py, c), device_id_type=MESH)
                cp.wait_send()
                cp.wait_recv()

    out = pl.pallas_call(
        body, name=name, in_specs=[HBM_SPEC] * (2 * n) + [SEM_SPEC, SEM_SPEC, ANY],
        out_specs=tuple([HBM_SPEC] * (2 * n)), out_shape=tuple(_hbm_like(parts) + _hbm_like(lands)),
        input_output_aliases={i: i for i in range(2 * n)},
        compiler_params=pltpu.CompilerParams(has_side_effects=DATAFLOW),
    )(*parts, *lands, ssem, rsem, after)
    return list(out[:n]), list(out[n:])


def _pair_gather_start(name, bufs):
    n = len(bufs)

    def body(*refs):
        ins = refs[:n]
        ssem, rsem = refs[n:n + 2]
        x, y, c = _coords()
        for a in range(n):
            mine = ins[a].at[c]
            pltpu.make_async_remote_copy(src_ref=mine, dst_ref=mine, send_sem=ssem.at[a], recv_sem=rsem.at[a],
                                         device_id=(x, y, 1 - c), device_id_type=MESH).start()

    sem = pltpu.SemaphoreType.DMA((n,))
    out = pl.pallas_call(
        body, name=name, in_specs=[HBM_SPEC] * n, out_specs=tuple([SEM_SPEC, SEM_SPEC] + [HBM_SPEC] * n),
        out_shape=tuple([sem, sem] + _hbm_like(bufs)),
        input_output_aliases={i: 2 + i for i in range(n)},
        compiler_params=pltpu.CompilerParams(has_side_effects=DATAFLOW),
    )(*[_hbm(b) for b in bufs])
    return out[0], out[1], list(out[2:])


def _pair_gather_wait(name, started, after):
    ssem, rsem, bufs = started
    n = len(bufs)

    def body(*refs):
        ins = refs[:n]
        s_in, r_in, _ = refs[n:n + 3]
        x, y, c = _coords()
        for a in range(n):
            cp = pltpu.make_async_remote_copy(src_ref=ins[a].at[c], dst_ref=ins[a].at[1 - c], send_sem=s_in.at[a],
                                              recv_sem=r_in.at[a], device_id=(x, y, 1 - c), device_id_type=MESH)
            cp.wait_send()
            cp.wait_recv()

    out = pl.pallas_call(
        body, name=name, in_specs=[HBM_SPEC] * n + [SEM_SPEC, SEM_SPEC, ANY],
        out_specs=tuple([HBM_SPEC] * n), out_shape=tuple(_hbm_like(bufs)),
        input_output_aliases={i: i for i in range(n)},
        compiler_params=pltpu.CompilerParams(has_side_effects=DATAFLOW),
    )(*bufs, ssem, rsem, after)
    return list(out)


DEV_MASKS = tuple((mx, my, mc) for mx in (0, 1) for my in (0, 1) for mc in (0, 1) if (mx, my, mc) != (0, 0, 0))


def _gather_small(buf, run_after):
    def body(in_ref, _, out_ref, ssem, rsem, lsem):
        x, y, c = _coords()
        me = 4 * x + 2 * y + c
        cps = [pltpu.make_async_copy(in_ref, out_ref.at[me], lsem)]
        cps[0].start()
        for t, (mx, my, mc) in enumerate(DEV_MASKS):
            cp = pltpu.make_async_remote_copy(src_ref=in_ref, dst_ref=out_ref.at[me], send_sem=ssem.at[t],
                                              recv_sem=rsem.at[t], device_id=(_flip(x, mx), _flip(y, my), _flip(c, mc)),
                                              device_id_type=MESH)
            cp.start()
            cps.append(cp)
        for cp in cps:
            cp.wait()

    return pl.pallas_call(
        body, name="gather_small", in_specs=[ANY, ANY], out_specs=ANY,
        out_shape=jax.ShapeDtypeStruct((N_DEV,) + buf.shape, buf.dtype),
        scratch_shapes=[pltpu.SemaphoreType.DMA((N_DEV - 1,)), pltpu.SemaphoreType.DMA((N_DEV - 1,)),
                        pltpu.SemaphoreType.DMA(())],
        compiler_params=pltpu.CompilerParams(has_side_effects=True),
    )(buf, run_after)


def _row_tile(rows):
    return rows // 2 if rows % 16 == 0 else rows


def _pair_add(name, grad, got, c_idx):
    _, _, r2, cols = grad.shape
    tr = _row_tile(r2)

    def body(c_ref, a_ref, b_ref, o_ref):
        o_ref[...] = (a_ref[...].astype(F32) + b_ref[...].astype(F32)).astype(BF16)

    return pl.pallas_call(
        body, name=name,
        grid_spec=pltpu.PrefetchScalarGridSpec(
            num_scalar_prefetch=1, grid=(N_CHIPS, r2 // tr),
            in_specs=[pl.BlockSpec((None, None, tr, cols), lambda j, i, c: (j, c[0], i, 0)),
                      pl.BlockSpec((None, tr, cols), lambda j, i, c: (j, i, 0))],
            out_specs=pl.BlockSpec((None, tr, cols), lambda j, i, c: (j, i, 0))),
        out_shape=jax.ShapeDtypeStruct((N_CHIPS, r2, cols), BF16),
        compiler_params=_cp(("parallel", "parallel")))(c_idx, grad, got)


def _chip_add(name, part, got, kc_idx):
    _, r2, cols = got.shape
    tr = _row_tile(r2)

    def body(k_ref, p_ref, g1_ref, g2_ref, g3_ref, o_ref):
        acc = p_ref[...].astype(F32)
        for g_ref in (g1_ref, g2_ref, g3_ref):
            acc = acc + g_ref[...].astype(F32)
        o_ref[...] = acc

    def slot(d):
        return pl.BlockSpec((None, tr, cols), lambda i, k: ((k[0] + d) % N_CHIPS, i, 0))

    return pl.pallas_call(
        body, name=name,
        grid_spec=pltpu.PrefetchScalarGridSpec(
            num_scalar_prefetch=1, grid=(r2 // tr,),
            in_specs=[slot(0), slot(1), slot(2), slot(3)],
            out_specs=pl.BlockSpec((None, tr, cols), lambda i, k: (k[1], i, 0))),
        out_shape=jax.ShapeDtypeStruct((2, r2, cols), F32),
        compiler_params=_cp(("parallel",)))(kc_idx, part, got, got, got)


def _adam_math(w, g, m, v):
    m2 = ADAM_B1 * m + (1.0 - ADAM_B1) * g
    v2 = ADAM_B2 * v + (1.0 - ADAM_B2) * (g * g)
    m_hat = m2 / (1.0 - ADAM_B1 ** ADAM_STEP)
    v_hat = v2 / (1.0 - ADAM_B2 ** ADAM_STEP)
    delta = -ADAM_LR * (m_hat / (jnp.sqrt(v_hat) + ADAM_EPS) + ADAM_WD * w)
    return delta, m2, v2


def _adamw_matrix(name, w, g_layers, m, v):
    _, rows, cols = w.shape
    tr = rows // 8

    def body(w_ref, g0_ref, g1_ref, m_ref, v_ref, go_ref, d_ref, mo_ref, vo_ref):
        g = jnp.where(pl.program_id(0) == 0, g0_ref[...], g1_ref[...])
        go_ref[...] = g
        d_ref[...], mo_ref[...], vo_ref[...] = _adam_math(w_ref[...], g, m_ref[...], v_ref[...])

    lay = pl.BlockSpec((None, tr, cols), lambda l, i: (l, i, 0))
    flat = pl.BlockSpec((tr, cols), lambda l, i: (i, 0))
    shp = jax.ShapeDtypeStruct(w.shape, F32)
    return pl.pallas_call(body, name=name, grid=(DEPTH, rows // tr), in_specs=[lay, flat, flat, lay, lay],
                          out_specs=[lay, lay, lay, lay], out_shape=[shp, shp, shp, shp],
                          compiler_params=_cp(("parallel", "parallel")))(w, g_layers[0], g_layers[1], m, v)


def _sum_small(gathered):
    def body(g_ref, o_ref):
        acc = g_ref[0]
        for d in range(1, N_DEV):
            acc = acc + g_ref[d]
        o_ref[...] = acc

    return pl.pallas_call(body, name="sum_small", out_shape=jax.ShapeDtypeStruct(gathered.shape[1:], F32),
                          compiler_params=_cp())(gathered)


def _adamw_small(w, g, m, v):
    def body(w_ref, g_ref, m_ref, v_ref, d_ref, mo_ref, vo_ref):
        d_ref[...], mo_ref[...], vo_ref[...] = _adam_math(w_ref[...], g_ref[...], m_ref[...], v_ref[...])

    shp = jax.ShapeDtypeStruct(w.shape, F32)
    return pl.pallas_call(body, name="adamw_small", out_shape=[shp, shp, shp], compiler_params=_cp())(w, g, m, v)


def _pack(arrays, rows):
    flat = jnp.concatenate([a.reshape(-1) for a in arrays])
    return jnp.pad(flat, (0, rows * BLK - flat.shape[0])).reshape(rows, BLK)


def _unpack(buf, shapes):
    flat = buf.reshape(-1)
    out, pos = [], 0
    for s in shapes:
        n = math.prod(s)
        out.append(flat[pos:pos + n].reshape(s))
        pos += n
    return out


def _rows_for(shapes):
    n = sum(math.prod(s) for s in shapes)
    return -(-n // (8 * BLK)) * 8


def _rs_swap(tag, grads):
    return _swap_start(f"rs_swap_start{tag}", [g.reshape(N_CHIPS, 2, g.shape[1] // 2, g.shape[2]) for g in grads])


def _rs_scatter(tag, swapping, after, c_idx):
    split, got = _swap_wait(f"rs_swap_wait{tag}", swapping, after)
    parts = [_pair_add(f"rs_pair_add{tag}_{i}", s, r, c_idx) for i, (s, r) in enumerate(zip(split, got))]
    return _scatter_start(f"rs_scatter_start{tag}", parts)


def _rs_reduce(tag, started, after, kc_idx):
    parts, lands = _scatter_wait(f"rs_scatter_wait{tag}", started, after)
    halves = [_chip_add(f"rs_chip_add{tag}_{i}", p, r, kc_idx) for i, (p, r) in enumerate(zip(parts, lands))]
    return _pair_gather_start(f"rs_pair_gather_start{tag}", halves)


def _rs_finish(tag, gathering, after):
    full = _pair_gather_wait(f"rs_pair_gather_wait{tag}", gathering, after)
    return [f.reshape(2 * f.shape[1], f.shape[2]) for f in full]


def kernel(x, w_in, lb_logits, a_norm_w, c_sinks, w_out, ln1_g, ln1_b, w_gate, w_up, conv_w, conv_b, w_down, ln2_g, ln2_b, loss_target, m_w_in, m_lb_logits, m_a_norm_w, m_c_sinks, m_w_out, m_ln1_g, m_ln1_b, m_w_gate, m_w_up, m_conv_w, m_conv_b, m_w_down, m_ln2_g, m_ln2_b, v_w_in, v_lb_logits, v_a_norm_w, v_c_sinks, v_w_out, v_ln1_g, v_ln1_b, v_w_gate, v_w_up, v_conv_w, v_conv_b, v_w_down, v_ln2_g, v_ln2_b):
    cx, cy, cc = _coords()
    c_idx = jnp.reshape(cc, (1,)).astype(jnp.int32)
    k_me = 2 * cx + cy
    k_idx = jnp.reshape(k_me, (1,)).astype(jnp.int32)
    kc_idx = jnp.stack([k_me, cc]).astype(jnp.int32)

    def slot(nm, w, l, run_after=None):
        b = _into_slot(f"slot_{nm}{l}", w, l, k_idx, BF16, run_after)
        return b.reshape(N_CHIPS, 2, b.shape[1] // 2, b.shape[2])

    cw_slot = _into_slot("slot_cw", conv_w.reshape(1, DEPTH * CONV_WIDTH, FF_SHARD), 0, k_idx, F32)
    cw_slot = cw_slot.reshape(N_CHIPS, DEPTH, CONV_WIDTH, FF_SHARD)
    first, token = _gather_start("gather_start0", [[slot("wi", w_in, 0), cw_slot]])
    sl = [{nm: slot(nm, w, l, token) for nm, w in (("wi", w_in), ("wo", w_out), ("wg", w_gate), ("wu", w_up), ("wd", w_down))
           if (nm, l) != ("wi", 0)} for l in range(DEPTH)]
    order = [(l, nm) for l in range(DEPTH) for nm in ("wi", "wo", "wg", "wu", "wd")][1:]
    rest, token = _gather_start("gather_start1", [[sl[l][nm]] for l, nm in order])
    stage_of = {key: st for key, st in zip(order, rest)}

    def mat(b):
        return b.reshape(N_CHIPS, 2 * b.shape[2], b.shape[3])

    h = x[0]
    h_bf = _to_bf16("x_bf16", h, token)
    fwd0, token = _gather_forward("gather_fwd0", first[0], h_bf)
    wi0, cw_all = _gather_wait("gather_wait0", fwd0, token)
    cw_full = jnp.transpose(cw_all, (1, 2, 0, 3)).reshape(DEPTH, CONV_WIDTH, D_FF)
    tables = _rope_tables()

    passing = {}

    def pass_on(l, nm, after):
        passing[(l, nm)] = _gather_forward(f"gather_fwd_{nm}{l}", stage_of[(l, nm)], after)

    def arrived(l, nm, after):
        i = order.index((l, nm))
        if i + 1 < len(order):
            pass_on(*order[i + 1], after)
            after = passing[order[i + 1]][1]
        return mat(_gather_wait(f"gather_wait_{nm}{l}", passing[(l, nm)][0], after)[0])

    saved = []
    weights = []
    for l in range(DEPTH):
        wi = mat(wi0) if l == 0 else arrived(l, "wi", h)
        proj = _fwd_colsharded(f"proj{l}", h_bf, wi)
        mixed, mixed_bf, raw = _hgrn_fwd(f"hgrn_fwd{l}", proj, lb_logits, a_norm_w[l], l)
        mixed, mixed_bf, lse_b = _attn_fwd(f"dilated_fwd{l}", proj, tables, None, mixed, mixed_bf, n_heads=B_HEADS, rep=1,
                                           q0=QB0, k0=KB0, v0=VB0, m0=A_HEADS, patterns=B_PATTERNS)
        if l == 0:
            pass_on(l, "wo", lse_b)
        sink_b = jnp.broadcast_to(c_sinks[l][:, None, None], (C_HEADS, 8, BLK))
        mixed, mixed_bf, lse_c = _attn_fwd(f"window_fwd{l}", proj, tables, sink_b, mixed, mixed_bf, n_heads=C_HEADS,
                                           rep=C_HEADS // C_KV_HEADS, q0=QC0, k0=KC0, v0=VC0, m0=A_HEADS + B_HEADS,
                                           patterns=C_PATTERNS)
        wo = arrived(l, "wo", lse_c)
        y1 = _fwd_rowsharded(f"wout{l}", mixed_bf, wo, OUT_SHARD)
        x1, x1_bf = _ln_fwd(f"ln1_fwd{l}", h, y1, ln1_g[l], ln1_b[l])
        wg = arrived(l, "wg", x1)
        g = _fwd_colsharded(f"gate{l}", x1_bf, wg, BF16)
        wu = arrived(l, "wu", g)
        u = _fwd_colsharded(f"up{l}", x1_bf, wu, BF16)
        hh = _conv_gate_fwd(f"conv_fwd{l}", g, u, cw_full[l], conv_b[l])
        wd = arrived(l, "wd", hh)
        y2 = _fwd_rowsharded(f"down{l}", hh, wd, FF_SHARD)
        weights.append(dict(wi=wi, wo=wo, wg=wg, wu=wu, wd=wd))
        saved.append((h, h_bf, proj, raw, lse_b, sink_b, lse_c, mixed, mixed_bf, y1, x1, x1_bf, g, u, hh, y2))
        if l + 1 < DEPTH:
            h, h_bf = _ln_fwd(f"ln2_fwd{l}", x1, y2, ln2_g[l], ln2_b[l])

    d_res = d_path = None
    small = [None] * DEPTH
    mat_grads = [None] * DEPTH
    late = {}
    prev_ffn = prev_mix_swap = None
    for l in reversed(range(DEPTH)):
        h_in, h_in_bf, proj, raw, lse_b, sink_b, lse_c, mixed, mixed_bf, y1, x1, x1_bf, g, u, hh, y2 = saved[l]
        wi, wo, wg, wu, wd = (weights[l][k] for k in ("wi", "wo", "wg", "wu", "wd"))
        if l == DEPTH - 1:
            dz2, dz2_bf, d_ln2g, d_ln2b, loss_part = _ln_loss_bwd(f"ln2_loss_bwd{l}", x1, y2, ln2_g[l], ln2_b[l],
                                                                  loss_target[0])
        else:
            dz2, dz2_bf, d_ln2g, d_ln2b = _ln_bwd(f"ln2_bwd{l}", x1, y2, ln2_g[l], d_res, d_path,
                                                  run_after=prev_mix_swap[2][0])
        dhh = _bwd_act_rowsharded(f"down_dx{l}", dz2_bf, wd, FF_SHARD, BF16)
        prev_mix = _rs_scatter(f"{l + 1}m", prev_mix_swap, dhh, c_idx) if prev_mix_swap else None
        d_wd = _bwd_w_rowsharded(f"down_dw{l}", hh, dz2_bf, FF_SHARD)
        dg, du, d_cw, d_cb = _conv_gate_bwd(f"conv_bwd{l}", g, u, cw_full[l], conv_b[l], dhh,
                                            run_after=prev_mix[2][0] if prev_mix else None)
        dx1 = _bwd_act_colsharded(f"gateup_dx{l}", [(dg, wg), (du, wu)])
        d_wg = _bwd_w_colsharded(f"gate_dw{l}", x1_bf, dg)
        d_wu = _bwd_w_colsharded(f"up_dw{l}", x1_bf, du)
        pins = ()
        if prev_ffn:
            late[l + 1] = [_rs_reduce(f"{l + 1}f", prev_ffn, d_wu, kc_idx)]
        ffn_swap = _rs_swap(f"{l}f", [d_wg, d_wu, d_wd])
        dz1, dz1_bf, d_ln1g, d_ln1b = _ln_bwd(f"ln1_bwd{l}", h_in, y1, ln1_g[l], dz2, dx1, run_after=ffn_swap[2][0])
        dmix = _bwd_act_rowsharded(f"wout_dx{l}", dz1_bf, wo, OUT_SHARD)
        d_wo = _bwd_w_rowsharded(f"wout_dw{l}", mixed_bf, dz1_bf, OUT_SHARD)
        if prev_mix:
            late[l + 1].append(_rs_reduce(f"{l + 1}m", prev_mix, d_wo, kc_idx))
            pins = tuple(g[2][0] for g in late[l + 1])
        s_ffn = _rs_scatter(f"{l}f", ffn_swap, (d_wo,) + pins, c_idx)
        dproj, d_nw, d_lb = _hgrn_bwd(f"hgrn_bwd{l}", proj, raw, dmix, lb_logits, a_norm_w[l], l, run_after=s_ffn[2][0])
        dproj, _ = _attn_bwd(f"dilated_bwd{l}", proj, mixed, dmix, lse_b, tables, None, dproj, n_kv=B_HEADS, rep=1,
                             q0=QB0, k0=KB0, v0=VB0, m0=A_HEADS, patterns=B_PATTERNS)
        dproj, d_sink = _attn_bwd(f"window_bwd{l}", proj, mixed, dmix, lse_c, tables, sink_b, dproj, n_kv=C_KV_HEADS,
                                  rep=C_HEADS // C_KV_HEADS, q0=QC0, k0=KC0, v0=VC0, m0=A_HEADS + B_HEADS,
                                  patterns=C_PATTERNS)
        dxp = _bwd_act_colsharded_full(f"proj_dx{l}", dproj, wi)
        d_wi = _bwd_w_colsharded(f"proj_dw{l}", h_in_bf, dproj)
        d_res, d_path = dz1, dxp
        prev_ffn, prev_mix_swap = s_ffn, _rs_swap(f"{l}m", [d_wi, d_wo])
        small[l] = (d_lb, d_nw.reshape(A_HEADS, 8, BLK)[:, 0].sum(0), d_sink[:, 0, 0], d_ln1g[0], d_ln1b[0],
                    d_cw, d_cb[0], d_ln2g[0], d_ln2b[0])
    grad_x2 = _axpy("grad_x", d_res, d_path)
    grad_x = grad_x2[None]

    g_lb = small[0][0] + small[1][0]
    per_layer = [jnp.stack([small[0][i], small[1][i]]) for i in range(1, 9)]
    small_shapes = [(DEPTH, 4 * BLK), (DEPTH, BLK), (DEPTH, C_HEADS), (DEPTH, D_MODEL), (DEPTH, D_MODEL),
                    (DEPTH, CONV_WIDTH, D_FF), (DEPTH, D_FF), (DEPTH, D_MODEL), (DEPTH, D_MODEL), (BLK,)]
    rows = _rows_for(small_shapes)
    total = _sum_small(_gather_small(_pack([g_lb] + per_layer + [loss_part[0]], rows), prev_mix_swap[2][0]))
    g_lb, g_nw, g_sink, g_ln1g, g_ln1b, g_cw_full, g_cb, g_ln2g, g_ln2b, loss_row = _unpack(total, small_shapes)
    loss = loss_row[0]
    g_cw = lax.dynamic_slice_in_dim(g_cw_full, k_me * FF_SHARD, FF_SHARD, axis=2)

    sw = [lb_logits, a_norm_w, c_sinks, ln1_g, ln1_b, conv_w, conv_b, ln2_g, ln2_b]
    sg = [g_lb, g_nw, g_sink, g_ln1g, g_ln1b, g_cw, g_cb, g_ln2g, g_ln2b]
    sm = [m_lb_logits, m_a_norm_w, m_c_sinks, m_ln1_g, m_ln1_b, m_conv_w, m_conv_b, m_ln2_g, m_ln2_b]
    sv = [v_lb_logits, v_a_norm_w, v_c_sinks, v_ln1_g, v_ln1_b, v_conv_w, v_conv_b, v_ln2_g, v_ln2_b]
    shapes = [a.shape for a in sw]
    prow = _rows_for(shapes)
    sd, snm, snv = (_unpack(b, shapes) for b in _adamw_small(_pack(sw, prow), _pack(sg, prow), _pack(sm, prow), _pack(sv, prow)))

    names = ["w_in", "w_out", "w_gate", "w_up", "w_down"]
    mw = [w_in, w_out, w_gate, w_up, w_down]
    mm = [m_w_in, m_w_out, m_w_gate, m_w_up, m_w_down]
    mv = [v_w_in, v_w_out, v_w_gate, v_w_up, v_w_down]
    res = [None] * 5
    s_mix = _rs_scatter("0m", prev_mix_swap, total, c_idx)
    for l, (g_ffn, g_mix) in late.items():
        g_wg, g_wu, g_wd = _rs_finish(f"{l}f", g_ffn, s_mix[2][0])
        g_wi, g_wo = _rs_finish(f"{l}m", g_mix, s_mix[2][0])
        mat_grads[l] = [g_wi, g_wo, g_wg, g_wu, g_wd]
    ffn0 = _rs_finish("0f", _rs_reduce("0f", prev_ffn, s_mix[2][0], kc_idx), s_mix[2][0])
    for i, g0 in zip((2, 3, 4), ffn0):
        res[i] = _adamw_matrix(f"adamw_{names[i]}", mw[i], [g0, mat_grads[1][i]], mm[i], mv[i])
    mix0 = _rs_finish("0m", _rs_reduce("0m", s_mix, res[4][1], kc_idx), res[4][1])
    for i, g0 in zip((0, 1), mix0):
        res[i] = _adamw_matrix(f"adamw_{names[i]}", mw[i], [g0, mat_grads[1][i]], mm[i], mv[i])
    mg, md, mnm, mnv = ([r[j] for r in res] for j in range(4))

    def ordered(mat, sm_):
        return [mat[0], sm_[0], sm_[1], sm_[2], mat[1], sm_[3], sm_[4], mat[2], mat[3], sm_[5], sm_[6], mat[4], sm_[7], sm_[8]]

    return (loss, grad_x, *ordered(mg, sg), *ordered(md, sd), *ordered(mnm, snm), *ordered(mnv, snv))
```

```python
import functools
import math

import jax
import jax.numpy as jnp
from jax import lax
from jax.experimental import pallas as pl
from jax.experimental.pallas import tpu as pltpu

F32 = jnp.float32
BF16 = jnp.bfloat16

D_MODEL = 2048
SEQ = 2048
DEPTH = 2
HEAD_DIM = 128
A_HEADS = 4
B_HEADS = 6
C_HEADS = 6
C_KV_HEADS = 2
A_CHUNK = 16
DILATED_PATTERNS = ((128, 1), (512, 4), (2048, 16))
C_WINDOW = 128
ROPE_THETA = 500000.0
ROPE_DIM = HEAD_DIM // 4
D_FF = 5632
CONV_WIDTH = 3
LN_EPS = 1e-5
ALPHA = (2 * DEPTH) ** 0.25
IN_WIDTH = 5632
MIX_WIDTH = 2048
ADAM_LR = 0.001
ADAM_B1 = 0.9
ADAM_B2 = 0.999
ADAM_EPS = 1e-08
ADAM_WD = 0.01
ADAM_STEP = 10

N_CHIPS = 4
N_DEV = 8
FF_SHARD = D_FF // N_CHIPS
OUT_SHARD = MIX_WIDTH // N_CHIPS
BLK = 128
N_CHUNK = SEQ // A_CHUNK
SLAB = 32

QA0, FA0, IA0, GA0 = 0, 4, 8, 12
QB0, KB0, VB0 = 16, 22, 28
QC0, KC0, VC0 = 34, 40, 42

VMEM_LIMIT_V7X = 56 * 1024 * 1024
HI = lax.Precision.HIGHEST
MESH = pl.DeviceIdType.MESH


def _cp(sem=None, vmem=VMEM_LIMIT_V7X, **kw):
    return pltpu.CompilerParams(dimension_semantics=sem, vmem_limit_bytes=vmem, **kw)


def _sigmoid(x):
    return 1.0 / (1.0 + jnp.exp(-x))


def _gate_sigmoid(x):
    return 0.5 * jnp.tanh(0.5 * x) + 0.5


def _mm(name, pairs, dims, grid, a_specs, b_specs, out_spec, out_shape, nk=1, acc_shape=None):
    n_pairs = len(pairs)

    def body(*refs):
        o_ref = refs[2 * n_pairs]
        part = None
        for p in range(n_pairs):
            a = refs[2 * p][...].astype(BF16)
            b = refs[2 * p + 1][...].astype(BF16)
            t = lax.dot_general(a, b, dims, preferred_element_type=F32)
            part = t if part is None else part + t
        if nk == 1:
            o_ref[...] = part.astype(o_ref.dtype)
        else:
            acc = refs[2 * n_pairs + 1]
            k = pl.program_id(len(grid) - 1)

            @pl.when(k == 0)
            def _():
                acc[...] = part

            @pl.when(k > 0)
            def _():
                acc[...] += part

            @pl.when(k == nk - 1)
            def _():
                o_ref[...] = acc[...].astype(o_ref.dtype)

    in_specs, args = [], []
    for (a, b), sa, sb in zip(pairs, a_specs, b_specs):
        in_specs += [sa, sb]
        args += [a, b]
    sem = ("parallel",) * (len(grid) - (1 if nk > 1 else 0)) + (("arbitrary",) if nk > 1 else ())
    return pl.pallas_call(
        body, name=name, grid=grid, in_specs=in_specs, out_specs=out_spec, out_shape=out_shape,
        scratch_shapes=[pltpu.VMEM(acc_shape, F32)] if nk > 1 else [],
        compiler_params=_cp(sem),
    )(*args)


NN = (((1,), (0,)), ((), ()))
NT = (((1,), (1,)), ((), ()))
TN = (((0,), (0,)), ((), ()))
TM = 1024


def _fwd_colsharded(name, x, w_stk, out_dtype=F32):
    return _mm(name, [(x, w_stk)], NN, (N_CHIPS, SEQ // TM),
               [pl.BlockSpec((TM, D_MODEL), lambda j, i: (i, 0))],
               [pl.BlockSpec((None, D_MODEL, FF_SHARD), lambda j, i: (j, 0, 0))],
               pl.BlockSpec((TM, FF_SHARD), lambda j, i: (i, j)),
               jax.ShapeDtypeStruct((SEQ, D_FF), out_dtype))


def _fwd_rowsharded(name, a, w_stk, shard):
    tn = 512
    rows = N_CHIPS * shard
    return _mm(name, [(a, w_stk.reshape(rows, D_MODEL))], NN, (SEQ // TM, D_MODEL // tn),
               [pl.BlockSpec((TM, rows), lambda i, j: (i, 0))],
               [pl.BlockSpec((rows, tn), lambda i, j: (0, j))],
               pl.BlockSpec((TM, tn), lambda i, j: (i, j)),
               jax.ShapeDtypeStruct((SEQ, D_MODEL), F32))


def _bwd_act_colsharded(name, pairs):
    tn = 1024
    n = len(pairs)
    return _mm(name, pairs, NT, (SEQ // TM, D_MODEL // tn, N_CHIPS),
               [pl.BlockSpec((TM, FF_SHARD), lambda i, j, k: (i, k))] * n,
               [pl.BlockSpec((None, tn, FF_SHARD), lambda i, j, k: (k, j, 0))] * n,
               pl.BlockSpec((TM, tn), lambda i, j, k: (i, j)),
               jax.ShapeDtypeStruct((SEQ, D_MODEL), F32), nk=N_CHIPS, acc_shape=(TM, tn))


def _bwd_act_colsharded_full(name, dy, w_stk, residual=None):
    tn = 512

    def body(a_ref, w_ref, *rest):
        acc = None
        for k in range(N_CHIPS):
            t = lax.dot_general(a_ref[:, k * FF_SHARD:(k + 1) * FF_SHARD], w_ref[k], NT, preferred_element_type=F32)
            acc = t if acc is None else acc + t
        if residual is not None:
            acc = ALPHA * rest[0][...] + acc
        rest[-1][...] = acc

    out = pl.BlockSpec((TM, tn), lambda i, j: (i, j))
    extra = [] if residual is None else [residual]
    return pl.pallas_call(
        body, name=name, grid=(SEQ // TM, D_MODEL // tn),
        in_specs=[pl.BlockSpec((TM, D_FF), lambda i, j: (i, 0)),
                  pl.BlockSpec((N_CHIPS, tn, FF_SHARD), lambda i, j: (0, j, 0))] + [out] * len(extra),
        out_specs=out, out_shape=jax.ShapeDtypeStruct((SEQ, D_MODEL), F32),
        compiler_params=_cp(("parallel", "parallel")))(dy, w_stk, *extra)


def _bwd_act_rowsharded(name, dy, w_stk, shard, out_dtype=F32):
    return _mm(name, [(dy, w_stk)], NT, (N_CHIPS, SEQ // TM),
               [pl.BlockSpec((TM, D_MODEL), lambda j, i: (i, 0))],
               [pl.BlockSpec((None, shard, D_MODEL), lambda j, i: (j, 0, 0))],
               pl.BlockSpec((TM, shard), lambda j, i: (i, j)),
               jax.ShapeDtypeStruct((SEQ, N_CHIPS * shard), out_dtype))


def _bwd_w_colsharded(name, x, dy):
    tm = 1024
    return _mm(name, [(x, dy)], TN, (N_CHIPS, D_MODEL // tm),
               [pl.BlockSpec((SEQ, tm), lambda j, i: (0, i))],
               [pl.BlockSpec((SEQ, FF_SHARD), lambda j, i: (0, j))],
               pl.BlockSpec((None, tm, FF_SHARD), lambda j, i: (j, i, 0)),
               jax.ShapeDtypeStruct((N_CHIPS, D_MODEL, FF_SHARD), BF16))


def _bwd_w_rowsharded(name, a, dy, shard):
    tn = 1024
    return _mm(name, [(a, dy)], TN, (N_CHIPS, D_MODEL // tn),
               [pl.BlockSpec((SEQ, shard), lambda j, i: (0, j))],
               [pl.BlockSpec((SEQ, tn), lambda j, i: (0, i))],
               pl.BlockSpec((None, shard, tn), lambda j, i: (j, 0, i)),
               jax.ShapeDtypeStruct((N_CHIPS, shard, D_MODEL), BF16))


TR = 256


def _ln_fwd(name, x, y, g, b):
    def body(x_ref, y_ref, g_ref, b_ref, o_ref, ob_ref):
        z = ALPHA * x_ref[...] + y_ref[...]
        mu = jnp.mean(z, -1, keepdims=True)
        zc = z - mu
        var = jnp.mean(zc * zc, -1, keepdims=True)
        o = zc * lax.rsqrt(var + LN_EPS) * g_ref[...] + b_ref[...]
        o_ref[...] = o
        ob_ref[...] = o.astype(BF16)

    row = pl.BlockSpec((TR, D_MODEL), lambda i: (i, 0))
    vec = pl.BlockSpec((1, D_MODEL), lambda i: (0, 0))
    return pl.pallas_call(body, name=name, grid=(SEQ // TR,), in_specs=[row, row, vec, vec], out_specs=[row, row],
                          out_shape=[jax.ShapeDtypeStruct((SEQ, D_MODEL), F32), jax.ShapeDtypeStruct((SEQ, D_MODEL), BF16)],
                          compiler_params=_cp(("parallel",)))(x, y, g.reshape(1, -1), b.reshape(1, -1))


def _to_bf16(name, x, run_after):
    def body(x_ref, _, o_ref):
        o_ref[...] = x_ref[...].astype(BF16)

    row = pl.BlockSpec((TR, D_MODEL), lambda i: (i, 0))
    return pl.pallas_call(body, name=name, grid=(SEQ // TR,), in_specs=[row, pl.BlockSpec(memory_space=pl.ANY)],
                          out_specs=row, out_shape=jax.ShapeDtypeStruct((SEQ, D_MODEL), BF16),
                          compiler_params=_cp(("parallel",)))(x, run_after)


def _ln_bwd(name, x, y, g, d_res, d_path, run_after):
    def body(x_ref, y_ref, g_ref, r_ref, p_ref, _, dz_ref, dzb_ref, dg_ref, db_ref):
        dout = ALPHA * r_ref[...] + p_ref[...]
        z = ALPHA * x_ref[...] + y_ref[...]
        mu = jnp.mean(z, -1, keepdims=True)
        zc = z - mu
        rstd = lax.rsqrt(jnp.mean(zc * zc, -1, keepdims=True) + LN_EPS)
        zh = zc * rstd
        dzh = dout * g_ref[...]
        dz = rstd * (dzh - jnp.mean(dzh, -1, keepdims=True) - zh * jnp.mean(dzh * zh, -1, keepdims=True))
        dz_ref[...] = dz
        dzb_ref[...] = dz.astype(BF16)
        pg = jnp.sum(dout * zh, 0, keepdims=True)
        pb = jnp.sum(dout, 0, keepdims=True)

        @pl.when(pl.program_id(0) == 0)
        def _():
            dg_ref[...] = pg
            db_ref[...] = pb

        @pl.when(pl.program_id(0) > 0)
        def _():
            dg_ref[...] += pg
            db_ref[...] += pb

    row = pl.BlockSpec((TR, D_MODEL), lambda i: (i, 0))
    vec = pl.BlockSpec((1, D_MODEL), lambda i: (0, 0))
    args = [x, y, g.reshape(1, -1), d_res, d_path, run_after]
    in_specs = [row, row, vec, row, row, pl.BlockSpec(memory_space=pl.ANY)]
    vshape = jax.ShapeDtypeStruct((1, D_MODEL), F32)
    return pl.pallas_call(body, name=name, grid=(SEQ // TR,), in_specs=in_specs, out_specs=[row, row, vec, vec],
                          out_shape=[jax.ShapeDtypeStruct((SEQ, D_MODEL), F32), jax.ShapeDtypeStruct((SEQ, D_MODEL), BF16),
                                     vshape, vshape],
                          compiler_params=_cp(("arbitrary",)))(*args)


def _ln_loss_bwd(name, x, y, g, b, target):
    def body(x_ref, y_ref, g_ref, b_ref, t_ref, dz_ref, dzb_ref, dg_ref, db_ref, l_ref):
        z = ALPHA * x_ref[...] + y_ref[...]
        mu = jnp.mean(z, -1, keepdims=True)
        zc = z - mu
        rstd = lax.rsqrt(jnp.mean(zc * zc, -1, keepdims=True) + LN_EPS)
        zh = zc * rstd
        e = zh * g_ref[...] + b_ref[...] - t_ref[...]
        dout = e * (1.0 / D_MODEL)
        dzh = dout * g_ref[...]
        dz = rstd * (dzh - jnp.mean(dzh, -1, keepdims=True) - zh * jnp.mean(dzh * zh, -1, keepdims=True))
        dz_ref[...] = dz
        dzb_ref[...] = dz.astype(BF16)
        pg = jnp.sum(dout * zh, 0, keepdims=True)
        pb = jnp.sum(dout, 0, keepdims=True)
        part = jnp.full((8, BLK), 0.5 / D_MODEL * jnp.sum(e * e), F32)

        @pl.when(pl.program_id(0) == 0)
        def _():
            dg_ref[...] = pg
            db_ref[...] = pb
            l_ref[...] = part

        @pl.when(pl.program_id(0) > 0)
        def _():
            dg_ref[...] += pg
            db_ref[...] += pb
            l_ref[...] += part

    row = pl.BlockSpec((TR, D_MODEL), lambda i: (i, 0))
    vec = pl.BlockSpec((1, D_MODEL), lambda i: (0, 0))
    vshape = jax.ShapeDtypeStruct((1, D_MODEL), F32)
    return pl.pallas_call(body, name=name, grid=(SEQ // TR,), in_specs=[row, row, vec, vec, row],
                          out_specs=[row, row, vec, vec, pl.BlockSpec((8, BLK), lambda i: (0, 0))],
                          out_shape=[jax.ShapeDtypeStruct((SEQ, D_MODEL), F32), jax.ShapeDtypeStruct((SEQ, D_MODEL), BF16),
                                     vshape, vshape, jax.ShapeDtypeStruct((8, BLK), F32)],
                          compiler_params=_cp(("arbitrary",)))(x, y, g.reshape(1, -1), b.reshape(1, -1), target)


TC = 512


def _shift_down(x, s, rows):
    if s == 0:
        return x
    return jnp.where(rows >= s, pltpu.roll(x, s, axis=0), 0.0)


def _shift_up(x, s, rows):
    if s == 0:
        return x
    return jnp.where(rows < SEQ - s, pltpu.roll(x, SEQ - s, axis=0), 0.0)


def _conv_gate_fwd(name, g, u, cw, cb):
    def body(g_ref, u_ref, w_ref, b_ref, h_ref):
        gg = g_ref[...].astype(F32)
        rows = lax.broadcasted_iota(jnp.int32, gg.shape, 0)
        gc = b_ref[...] + w_ref[2:3, :] * gg
        gc = gc + w_ref[1:2, :] * _shift_down(gg, 1, rows)
        gc = gc + w_ref[0:1, :] * _shift_down(gg, 2, rows)
        h_ref[...] = (gc * _gate_sigmoid(gc) * u_ref[...].astype(F32)).astype(BF16)

    col = pl.BlockSpec((SEQ, TC), lambda j: (0, j))
    return pl.pallas_call(body, name=name, grid=(D_FF // TC,),
                          in_specs=[col, col, pl.BlockSpec((CONV_WIDTH, TC), lambda j: (0, j)),
                                    pl.BlockSpec((1, TC), lambda j: (0, j))],
                          out_specs=col, out_shape=jax.ShapeDtypeStruct((SEQ, D_FF), BF16),
                          compiler_params=_cp(("parallel",)))(g, u, cw, cb.reshape(1, -1))


def _conv_gate_bwd(name, g, u, cw, cb, dh, run_after=None):
    def body(g_ref, u_ref, w_ref, b_ref, dh_ref, *rest):
        dg_ref, du_ref, dw_ref, db_ref = rest[-4:]
        gg = g_ref[...].astype(F32)
        rows = lax.broadcasted_iota(jnp.int32, gg.shape, 0)
        g1 = _shift_down(gg, 1, rows)
        g2 = _shift_down(gg, 2, rows)
        gc = b_ref[...] + w_ref[2:3, :] * gg + w_ref[1:2, :] * g1 + w_ref[0:1, :] * g2
        sg = _gate_sigmoid(gc)
        act = gc * sg
        dh = dh_ref[...].astype(F32)
        du_ref[...] = (dh * act).astype(BF16)
        dgc = dh * u_ref[...].astype(F32) * (sg * (1.0 + gc * (1.0 - sg)))
        db_ref[...] = jnp.sum(dgc, 0, keepdims=True)
        dw_ref[2:3, :] = jnp.sum(dgc * gg, 0, keepdims=True)
        dw_ref[1:2, :] = jnp.sum(dgc * g1, 0, keepdims=True)
        dw_ref[0:1, :] = jnp.sum(dgc * g2, 0, keepdims=True)
        dg_ref[...] = (w_ref[2:3, :] * dgc + w_ref[1:2, :] * _shift_up(dgc, 1, rows)
                       + w_ref[0:1, :] * _shift_up(dgc, 2, rows)).astype(BF16)

    col = pl.BlockSpec((SEQ, TC), lambda j: (0, j))
    w3 = pl.BlockSpec((CONV_WIDTH, TC), lambda j: (0, j))
    w1 = pl.BlockSpec((1, TC), lambda j: (0, j))
    big = jax.ShapeDtypeStruct((SEQ, D_FF), BF16)
    extra = [] if run_after is None else [run_after]
    return pl.pallas_call(body, name=name, grid=(D_FF // TC,),
                          in_specs=[col, col, w3, w1, col] + [pl.BlockSpec(memory_space=pl.ANY)] * len(extra),
                          out_specs=[col, col, w3, w1],
                          out_shape=[big, big, jax.ShapeDtypeStruct((CONV_WIDTH, D_FF), F32),
                                     jax.ShapeDtypeStruct((1, D_FF), F32)],
                          compiler_params=_cp(("parallel",)))(g, u, cw, cb.reshape(1, -1), dh, *extra)


def _lbs_of(logits, layer):
    m = jnp.max(logits, 0, keepdims=True)
    e = jnp.exp(logits - m)
    p = e / jnp.sum(e, 0, keepdims=True)
    lb = jnp.zeros((1, BLK), F32)
    for r in range(1, layer + 1):
        lb = lb + p[r:r + 1, :]
    return lb, p


def _dlogits_of(p, dlb, layer):
    rows = lax.broadcasted_iota(jnp.int32, p.shape, 0)
    dp = jnp.where((rows >= 1) & (rows <= layer), dlb, 0.0)
    return p * (dp - jnp.sum(p * dp, 0, keepdims=True))


SROWS = SLAB * A_CHUNK
N_SLAB = N_CHUNK // SLAB


def _chunk_prefix(x, rowi):
    for s in (1, 2, 4, 8):
        x = x + jnp.where(rowi >= s, pltpu.roll(x, s, axis=0), 0.0)
    return x


def _chunk_suffix(x, rowi):
    for s in (1, 2, 4, 8):
        x = x + jnp.where(rowi < A_CHUNK - s, pltpu.roll(x, SROWS - s, axis=0), 0.0)
    return x


def _c3(x):
    return x.reshape(SLAB, A_CHUNK, BLK)


def _c2(x):
    return x.reshape(SROWS, BLK)


def _split(x):
    top = lax.bitcast_convert_type(lax.bitcast_convert_type(x, jnp.uint32) & jnp.uint32(0xFFFF0000), F32)
    return top.astype(BF16), (x - top).astype(BF16)


def _bmm(eq, a, b):
    ah, al = _split(a)
    bh, bl = _split(b)

    def mm(u, v):
        return jnp.einsum(eq, u, v, preferred_element_type=F32)

    return mm(ah, bh) + (mm(ah, bl) + mm(al, bh))


def _bmm_1pass(eq, a, b):
    return jnp.einsum(eq, a.astype(BF16), b.astype(BF16), preferred_element_type=F32)


def _slab_rows(s):
    return pl.ds(s * SROWS, SROWS)


def _hgrn_prep(q, f, lb):
    rowi = lax.broadcasted_iota(jnp.int32, (SROWS, BLK), 0) & (A_CHUNK - 1)
    sq = _gate_sigmoid(q)
    qc = q * sq
    sf = _sigmoid(f)
    fg = lb + (1.0 - lb) * sf
    kc = 1.0 - fg
    b = _chunk_prefix(jnp.log(fg), rowi)
    b3 = _c3(b)
    blast = b3[:, A_CHUNK - 1:A_CHUNK, :]
    eb = jnp.exp(b)
    ekb = _c2(jnp.exp(blast - b3))
    dec = jnp.exp(blast.reshape(SLAB, BLK))
    return rowi, sq, qc, sf, fg, kc, b, eb, ekb, dec


def _hgrn_slab_states(s, carry, v, ke, dec, dec_ref, u_ref, st_ref):
    dec_ref[pl.ds(s * SLAB, SLAB), :] = dec
    u_ref[...] = _bmm('ncv,nck->nvk', _c3(v), _c3(ke))

    def step(j, c):
        st_ref[j] = c
        return dec_ref[pl.ds(s * SLAB + j, 1), :] * c + u_ref[j]

    return lax.fori_loop(0, SLAB, step, carry)


def _hgrn_fwd(name, proj, lb_logits, nw, layer):
    def body(q_ref, f_ref, i_ref, g_ref, lg_ref, nw_ref, out_ref, outb_ref, raw_ref, dec_ref, u_ref, st_ref):
        lb, _ = _lbs_of(lg_ref[...], layer)
        ones = jnp.ones((BLK, BLK), BF16)
        carry = jnp.zeros((BLK, BLK), F32)
        for s in range(N_SLAB):
            rows = _slab_rows(s)
            v = i_ref[rows, :]
            rowi, sq, qc, sf, fg, kc, b, eb, ekb, dec = _hgrn_prep(q_ref[rows, :], f_ref[rows, :], lb)
            carry = _hgrn_slab_states(s, carry, v, kc * ekb, dec, dec_ref, u_ref, st_ref)
            o = _c2(_bmm('nck,nvk->ncv', _c3(qc * eb), st_ref[...]))
            qc3, kc3, b3, v3, row3 = _c3(qc), _c3(kc), _c3(b), _c3(v), _c3(rowi)
            col = lax.broadcasted_iota(jnp.int32, (SLAB, A_CHUNK, A_CHUNK), 2)
            att_all = jnp.zeros((SLAB, A_CHUNK, A_CHUNK), F32)
            for j in range(A_CHUNK):
                dj = jnp.exp(jnp.where(row3 >= j, b3 - b3[:, j:j + 1, :], -jnp.inf))
                a = jnp.dot(_c2(qc3 * dj * kc3[:, j:j + 1, :]).astype(BF16), ones, preferred_element_type=F32)
                att_all = jnp.where(col == j, _c3(a)[:, :, :A_CHUNK], att_all)
            o = o + _c2(_bmm_1pass('nij,njv->niv', att_all, v3))
            raw_ref[rows, :] = o
            r = lax.rsqrt(jnp.mean(o * o, -1, keepdims=True) + LN_EPS)
            gg = g_ref[rows, :]
            gated = o * r * nw_ref[...] * (gg * _gate_sigmoid(gg))
            out_ref[rows, :] = gated
            outb_ref[rows, :] = gated.astype(BF16)

    def colblk(c0):
        return pl.BlockSpec((SEQ, BLK), lambda h: (0, c0 + h))

    return pl.pallas_call(
        body, name=name, grid=(A_HEADS,),
        in_specs=[colblk(QA0), colblk(FA0), colblk(IA0), colblk(GA0),
                  pl.BlockSpec((DEPTH, BLK), lambda h: (0, h)), pl.BlockSpec((1, BLK), lambda h: (0, 0))],
        out_specs=[colblk(0), colblk(0), colblk(0)],
        out_shape=[jax.ShapeDtypeStruct((SEQ, MIX_WIDTH), F32), jax.ShapeDtypeStruct((SEQ, MIX_WIDTH), BF16),
                   jax.ShapeDtypeStruct((SEQ, A_HEADS * BLK), F32)],
        scratch_shapes=[pltpu.VMEM((N_CHUNK, BLK), F32), pltpu.VMEM((SLAB, BLK, BLK), F32),
                        pltpu.VMEM((SLAB, BLK, BLK), F32)],
        compiler_params=_cp(("parallel",)))(proj, proj, proj, proj, lb_logits, nw.reshape(1, -1))


def _col_block_copies(stage, sems, dst, col_blocks, first):
    return [pltpu.make_async_copy(stage.at[first + t], dst.at[:, pl.ds(pl.multiple_of(cb * BLK, BLK), BLK)],
                                  sems.at[first + t]) for t, cb in enumerate(col_blocks)]


def _start_col_blocks(stage, sems, dst, col_blocks, first=0):
    for cp in _col_block_copies(stage, sems, dst, col_blocks, first):
        cp.start()


def _wait_col_blocks(stage, sems, dst, count, first=0):
    for cp in _col_block_copies(stage, sems, dst, [0] * count, first):
        cp.wait()


def _hgrn_bwd(name, proj, raw, dmix, lb_logits, nw, layer, run_after=None):
    extra = [] if run_after is None else [run_after]

    def body(q_ref, f_ref, i_ref, g_ref, raw_ref, do_ref, lg_ref, nw_ref, *rest):
        (dproj_ref, dnw_ref, dlg_ref,
         dec_ref, u_ref, st_ref, h_ref, dbs_ref, dkc_ref, tot_ref, stage, stage_sem) = rest[-12:]
        dq_ref, df_ref, di_ref, dg_ref = (stage.at[t] for t in range(4))
        lb, p = _lbs_of(lg_ref[...], layer)
        ones = jnp.ones((BLK, BLK), BF16)
        nwv = nw_ref[...]

        carry = jnp.zeros((BLK, BLK), F32)
        for s in range(N_SLAB):
            rows = _slab_rows(s)
            rowi, sq, qc, sf, fg, kc, b, eb, ekb, dec = _hgrn_prep(q_ref[rows, :], f_ref[rows, :], lb)
            carry = _hgrn_slab_states(s, carry, i_ref[rows, :], kc * ekb, dec, dec_ref, u_ref,
                                      st_ref.at[pl.ds(s * SLAB, SLAB)])

        @pl.when(pl.program_id(0) > 0)
        def _():
            _wait_col_blocks(stage, stage_sem, dproj_ref, 4)

        carry = jnp.zeros((BLK, BLK), F32)
        dnw = jnp.zeros((1, BLK), F32)
        for s in reversed(range(N_SLAB)):
            rows = _slab_rows(s)
            q, v = q_ref[rows, :], i_ref[rows, :]
            rowi, sq, qc, sf, fg, kc, b, eb, ekb, dec = _hgrn_prep(q, f_ref[rows, :], lb)
            ke = kc * ekb
            qe = qc * eb

            o = raw_ref[rows, :]
            gg = g_ref[rows, :]
            sgg = _gate_sigmoid(gg)
            dout = do_ref[rows, :]
            r = lax.rsqrt(jnp.mean(o * o, -1, keepdims=True) + LN_EPS)
            oh = o * r
            dg_ref[rows, :] = (dout * oh * nwv * (sgg * (1.0 + gg * (1.0 - sgg)))).astype(BF16)
            dn = dout * (gg * sgg)
            dnw = dnw + jnp.sum(dn * oh, 0, keepdims=True)
            doh = dn * nwv
            do = r * (doh - oh * jnp.mean(doh * oh, -1, keepdims=True))
            do3, qe3, v3, ke3 = _c3(do), _c3(qe), _c3(v), _c3(ke)

            u_ref[...] = _bmm('ncv,nck->nvk', do3, qe3)

            def step(jj, c, s=s):
                j = SLAB - 1 - jj
                h_ref[j] = c
                return u_ref[j] + dec_ref[pl.ds(s * SLAB + j, 1), :] * c

            carry = lax.fori_loop(0, SLAB, step, carry)

            hh = h_ref[...]
            dqc = _c2(_bmm('ncv,nvk->nck', do3, st_ref[pl.ds(s * SLAB, SLAB)])) * eb
            dkc = _c2(_bmm('ncv,nvk->nck', v3, hh)) * ekb
            dv = _c2(_bmm_1pass('nck,nvk->ncv', ke3, hh))

            qc3, kc3, b3, row3 = _c3(qc), _c3(kc), _c3(b), _c3(rowi)
            datt_all = _bmm('niv,njv->nij', do3, v3)
            col = lax.broadcasted_iota(jnp.int32, datt_all.shape, 2)
            att_all = jnp.zeros_like(datt_all)
            for j in range(A_CHUNK):
                dj = jnp.exp(jnp.where(row3 >= j, b3 - b3[:, j:j + 1, :], -jnp.inf))
                kj = kc3[:, j:j + 1, :]
                att = _c3(jnp.dot(_c2(qc3 * dj * kj).astype(BF16), ones, preferred_element_type=F32))
                att_all = jnp.where(col == j, att[:, :, :A_CHUNK], att_all)
                md = dj * datt_all[:, :, j:j + 1]
                dqc = dqc + _c2(md * kj)
                dkc = dkc + _c2(jnp.where(row3 == j, jnp.sum(md * qc3, 1, keepdims=True), 0.0))
            dv = dv + _c2(_bmm_1pass('nij,niv->njv', att_all, do3))
            di_ref[rows, :] = dv.astype(BF16)
            dq_ref[rows, :] = (dqc * (sq * (1.0 + q * (1.0 - sq)))).astype(BF16)

            dbs = _chunk_suffix(qc * dqc - kc * dkc, rowi)
            dbs_ref[rows, :] = dbs
            dkc_ref[rows, :] = dkc
            tot_ref[pl.ds(s * SLAB, SLAB), :] = _c3(dbs)[:, 0:1, :].reshape(SLAB, BLK)
        dnw_ref[...] = jnp.broadcast_to(dnw, (8, BLK))

        rn = lax.broadcasted_iota(jnp.int32, (N_CHUNK, N_CHUNK), 0)
        cn = lax.broadcasted_iota(jnp.int32, (N_CHUNK, N_CHUNK), 1)
        tot_ref[...] = jnp.dot((cn > rn).astype(F32), tot_ref[...], preferred_element_type=F32, precision=HI)
        dlb = jnp.zeros((1, BLK), F32)
        for s in range(N_SLAB):
            rows = _slab_rows(s)
            sf = _sigmoid(f_ref[rows, :])
            fg = lb + (1.0 - lb) * sf
            later = tot_ref[pl.ds(s * SLAB, SLAB), :]
            dlg = _c2(_c3(dbs_ref[rows, :]) + later[:, None, :])
            dfg = dlg / fg - dkc_ref[rows, :]
            df_ref[rows, :] = (dfg * (1.0 - lb) * sf * (1.0 - sf)).astype(BF16)
            dlb = dlb + jnp.sum(dfg * (1.0 - sf), 0, keepdims=True)
        dlg_ref[...] = _dlogits_of(p, dlb, layer)
        _start_col_blocks(stage, stage_sem, dproj_ref, [c0 + pl.program_id(0) for c0 in (QA0, FA0, IA0, GA0)])

        @pl.when(pl.program_id(0) == A_HEADS - 1)
        def _():
            _wait_col_blocks(stage, stage_sem, dproj_ref, 4)

    def colblk(c0):
        return pl.BlockSpec((SEQ, BLK), lambda h: (0, c0 + h))

    return pl.pallas_call(
        body, name=name, grid=(A_HEADS,),
        in_specs=[colblk(QA0), colblk(FA0), colblk(IA0), colblk(GA0), colblk(0), colblk(0),
                  pl.BlockSpec((DEPTH, BLK), lambda h: (0, h)), pl.BlockSpec((1, BLK), lambda h: (0, 0))]
        + [pl.BlockSpec(memory_space=pl.ANY)] * len(extra),
        out_specs=[pl.BlockSpec(memory_space=pl.ANY),
                   pl.BlockSpec((8, BLK), lambda h: (h, 0)), pl.BlockSpec((DEPTH, BLK), lambda h: (0, h))],
        out_shape=[jax.ShapeDtypeStruct((SEQ, IN_WIDTH), BF16), jax.ShapeDtypeStruct((A_HEADS * 8, BLK), F32),
                   jax.ShapeDtypeStruct((DEPTH, A_HEADS * BLK), F32)],
        scratch_shapes=[pltpu.VMEM((N_CHUNK, BLK), F32), pltpu.VMEM((SLAB, BLK, BLK), F32),
                        pltpu.VMEM((N_CHUNK, BLK, BLK), F32), pltpu.VMEM((SLAB, BLK, BLK), F32),
                        pltpu.VMEM((SEQ, BLK), F32), pltpu.VMEM((SEQ, BLK), F32), pltpu.VMEM((N_CHUNK, BLK), F32),
                        pltpu.VMEM((4, SEQ, BLK), BF16), pltpu.SemaphoreType.DMA((4,))],
        compiler_params=_cp(("arbitrary",)))(proj, proj, proj, proj, raw, dmix, lb_logits, nw.reshape(1, -1), *extra)


SCALE = HEAD_DIM ** -0.5


def _rope_tables():
    half = ROPE_DIM // 2
    inv = ROPE_THETA ** (-jnp.arange(0, ROPE_DIM, 2, dtype=F32) / ROPE_DIM)
    ang = jnp.arange(SEQ, dtype=F32)[:, None] * inv[None, :]
    cos, sin = jnp.cos(ang), jnp.sin(ang)
    pad = jnp.zeros((SEQ, HEAD_DIM - ROPE_DIM), F32)
    zero = jnp.zeros((SEQ, half), F32)
    c = jnp.concatenate([cos, cos, pad + 1.0], 1)
    s_lo = jnp.concatenate([zero, sin, pad], 1)
    s_hi = jnp.concatenate([-sin, zero, pad], 1)
    return c, s_lo, s_hi


def _rope(x, c, s_lo, s_hi):
    half = ROPE_DIM // 2
    return x * c + pltpu.roll(x, half, axis=1) * s_lo + pltpu.roll(x, HEAD_DIM - half, axis=1) * s_hi


def _unrope(dy, c, s_lo, s_hi):
    half = ROPE_DIM // 2
    return dy * c + pltpu.roll(dy * s_lo, HEAD_DIM - half, axis=1) + pltpu.roll(dy * s_hi, half, axis=1)


N_BLK = SEQ // BLK


def _block_rows(dil):
    nb = N_BLK // dil
    return [pl.ds(r + n * BLK * dil, BLK, stride=dil) for r in range(dil) for n in range(nb)]


def _to_blocks(ref, dil):
    if dil == 1:
        return ref[...].reshape(N_BLK, BLK, BLK)
    return jnp.stack([ref[rows, :] for rows in _block_rows(dil)], 0)


def _from_blocks(ref, val, dil, add=False):
    if dil == 1:
        flat = val.reshape(SEQ, BLK)
        ref[...] = ref[...] + flat if add else flat
        return
    for b, rows in enumerate(_block_rows(dil)):
        ref[rows, :] = ref[rows, :] + val[b] if add else val[b]


def _prev_block(x):
    return jnp.concatenate([x[:1], x[:-1]], axis=0)


def _to_next_block(x):
    return jnp.concatenate([x[1:], jnp.zeros_like(x[:1])], axis=0)


def _band_masks(max_lag, dil):
    r = lax.broadcasted_iota(jnp.int32, (N_BLK, BLK, BLK), 1)
    c = lax.broadcasted_iota(jnp.int32, (N_BLK, BLK, BLK), 2)
    b = lax.broadcasted_iota(jnp.int32, (N_BLK, BLK, BLK), 0)
    has_prev = (b % (N_BLK // dil)) != 0
    return r >= c, has_prev & (BLK + r - c <= max_lag)


def _bdot(eq, a, b):
    return jnp.einsum(eq, a, b, preferred_element_type=F32)


def _attn_fwd(name, proj, tables, sink_b, mixed, mixed_bf, *, n_heads, rep, q0, k0, v0, m0, patterns):
    n_pat = len(patterns)
    has_sink = sink_b is not None

    def body(*refs):
        o_ref, ob_ref, l_ref, qr, kr, op, lse_ref = refs[-7:]
        q_ref, k_ref, v_ref, c_ref, sl_ref, sh_ref = refs[:6]
        if has_sink:
            sk = refs[6][0:1, 0:1]
        c, s_lo, s_hi = c_ref[...], sl_ref[...], sh_ref[...]
        qr[...] = _rope(q_ref[...], c, s_lo, s_hi)
        kr[...] = _rope(k_ref[...], c, s_lo, s_hi)
        for p, (max_lag, dil) in enumerate(patterns):
            qa = _to_blocks(qr, dil).astype(BF16)
            ka = _to_blocks(kr, dil).astype(BF16)
            va = _to_blocks(v_ref, dil).astype(BF16)
            own, before = _band_masks(max_lag, dil)
            s1 = jnp.where(own, _bdot('nqd,nkd->nqk', qa, ka) * SCALE, -jnp.inf)
            m = jnp.max(s1, -1, keepdims=True)
            with_prev = dil < N_BLK
            if with_prev:
                kp, vp = _prev_block(ka), _prev_block(va)
                s0 = jnp.where(before, _bdot('nqd,nkd->nqk', qa, kp) * SCALE, -jnp.inf)
                m = jnp.maximum(m, jnp.max(s0, -1, keepdims=True))
            if has_sink:
                m = jnp.maximum(m, sk)
            e1 = jnp.exp(s1 - m)
            den = jnp.sum(e1, -1, keepdims=True)
            o = _bdot('nqk,nkd->nqd', e1.astype(BF16), va)
            if with_prev:
                e0 = jnp.exp(s0 - m)
                den = den + jnp.sum(e0, -1, keepdims=True)
                o = o + _bdot('nqk,nkd->nqd', e0.astype(BF16), vp)
            if has_sink:
                den = den + jnp.exp(sk - m)
            _from_blocks(op.at[p], o / den, dil)
            _from_blocks(lse_ref.at[p], jnp.broadcast_to(m + jnp.log(den), (N_BLK, BLK, BLK)), dil)
        if n_pat == 1:
            acc = op[0]
            l_ref[...] = lse_ref[0]
        else:
            ls = [lse_ref[p] for p in range(n_pat)]
            m = functools.reduce(jnp.maximum, ls)
            es = [jnp.exp(l - m) for l in ls]
            tot = functools.reduce(jnp.add, es)
            acc = None
            for p in range(n_pat):
                t = (es[p] / tot) * op[p]
                acc = t if acc is None else acc + t
            l_ref[...] = m + jnp.log(tot)
        o_ref[...] = acc
        ob_ref[...] = acc.astype(BF16)

    def colblk(fn):
        return pl.BlockSpec((SEQ, BLK), fn)

    tab = pl.BlockSpec((SEQ, BLK), lambda h: (0, 0))
    in_specs = [colblk(lambda h: (0, q0 + h)), colblk(lambda h: (0, k0 + h // rep)), colblk(lambda h: (0, v0 + h // rep)),
                tab, tab, tab]
    args = [proj, proj, proj, *tables]
    if has_sink:
        in_specs.append(pl.BlockSpec((None, 8, BLK), lambda h: (h, 0, 0)))
        args.append(sink_b)
    n_in = len(args)
    in_specs += [pl.BlockSpec(memory_space=pl.ANY)] * 2
    args += [mixed, mixed_bf]
    pat = pltpu.VMEM((n_pat, SEQ, BLK), F32)
    return pl.pallas_call(
        body, name=name, grid=(n_heads,), in_specs=in_specs,
        out_specs=[colblk(lambda h: (0, m0 + h)), colblk(lambda h: (0, m0 + h)),
                   pl.BlockSpec((None, SEQ, BLK), lambda h: (h, 0, 0))],
        out_shape=[jax.ShapeDtypeStruct(mixed.shape, F32), jax.ShapeDtypeStruct(mixed.shape, BF16),
                   jax.ShapeDtypeStruct((n_heads, SEQ, BLK), F32)],
        input_output_aliases={n_in: 0, n_in + 1: 1},
        scratch_shapes=[pltpu.VMEM((SEQ, BLK), F32), pltpu.VMEM((SEQ, BLK), F32), pat, pat],
        compiler_params=_cp(("parallel",)))(*args)


def _attn_bwd(name, proj, mixed, dmix, lse, tables, sink_b, dproj, *, n_kv, rep, q0, k0, v0, m0, patterns):
    n_heads = n_kv * rep
    has_sink = sink_b is not None

    def body(*refs):
        q_ref, k_ref, v_ref, o_ref, do_ref, lse_ref, c_ref, sl_ref, sh_ref = refs[:9]
        sink_ref = refs[9] if has_sink else None
        dproj_ref, dsk_ref, qr, kr, dqa, dka, dva, dd, stage, stage_sem = refs[-10:]
        g, j = pl.program_id(0), pl.program_id(1)
        c, s_lo, s_hi = c_ref[...], sl_ref[...], sh_ref[...]
        qr[...] = _rope(q_ref[...], c, s_lo, s_hi)
        kr[...] = _rope(k_ref[...], c, s_lo, s_hi)
        dcol = jnp.sum(do_ref[...] * o_ref[...], -1, keepdims=True)
        dd[...] = jnp.broadcast_to(dcol, (SEQ, BLK))

        @pl.when(j == 0)
        def _():
            dka[...] = jnp.zeros((SEQ, BLK), F32)
            dva[...] = jnp.zeros((SEQ, BLK), F32)

        for p, (max_lag, dil) in enumerate(patterns):
            qa = _to_blocks(qr, dil).astype(BF16)
            ka = _to_blocks(kr, dil).astype(BF16)
            va = _to_blocks(v_ref, dil).astype(BF16)
            doa = _to_blocks(do_ref, dil).astype(BF16)
            lcol = _to_blocks(lse_ref, dil)[:, :, 0:1]
            dcb = _to_blocks(dd, dil)[:, :, 0:1]
            own, before = _band_masks(max_lag, dil)

            def probs_and_ds(kk, vv, valid):
                s = _bdot('nqd,nkd->nqk', qa, kk) * SCALE
                a = jnp.where(valid, jnp.exp(s - lcol), 0.0)
                ds = a * (_bdot('nqd,nkd->nqk', doa, vv) - dcb) * SCALE
                return a.astype(BF16), ds.astype(BF16)

            a1, ds1 = probs_and_ds(ka, va, own)
            dq = _bdot('nqk,nkd->nqd', ds1, ka)
            dk = _bdot('nqk,nqd->nkd', ds1, qa)
            dv = _bdot('nqk,nqd->nkd', a1, doa)
            if dil < N_BLK:
                kp, vp = _prev_block(ka), _prev_block(va)
                a0, ds0 = probs_and_ds(kp, vp, before)
                dq = dq + _bdot('nqk,nkd->nqd', ds0, kp)
                dk = dk + _to_next_block(_bdot('nqk,nqd->nkd', ds0, qa))
                dv = dv + _to_next_block(_bdot('nqk,nqd->nkd', a0, doa))
            _from_blocks(dqa, dq, dil, add=p > 0)
            _from_blocks(dka, dk, dil, add=True)
            _from_blocks(dva, dv, dil, add=True)

        if has_sink:
            sk = sink_ref[0:1, 0:1]
            ps = jnp.exp(sk - lse_ref[...][:, 0:1])
            dsk_ref[...] = jnp.full((8, BLK), -jnp.sum(ps * dcol), F32)
        else:
            dsk_ref[...] = jnp.zeros((8, BLK), F32)
        @pl.when(g * rep + j > 0)
        def _():
            _wait_col_blocks(stage, stage_sem, dproj_ref, 1)

        stage[0] = _unrope(dqa[...], c, s_lo, s_hi).astype(BF16)
        _start_col_blocks(stage, stage_sem, dproj_ref, [q0 + g * rep + j])

        @pl.when(j == rep - 1)
        def _():
            @pl.when(g > 0)
            def _():
                _wait_col_blocks(stage, stage_sem, dproj_ref, 2, first=1)

            stage[1] = _unrope(dka[...], c, s_lo, s_hi).astype(BF16)
            stage[2] = dva[...].astype(BF16)
            _start_col_blocks(stage, stage_sem, dproj_ref, [k0 + g, v0 + g], first=1)

        @pl.when((g == n_kv - 1) & (j == rep - 1))
        def _():
            _wait_col_blocks(stage, stage_sem, dproj_ref, 3)

    def colblk(fn):
        return pl.BlockSpec((SEQ, BLK), fn)

    tab = pl.BlockSpec((SEQ, BLK), lambda g, j: (0, 0))
    in_specs = [colblk(lambda g, j: (0, q0 + g * rep + j)), colblk(lambda g, j: (0, k0 + g)), colblk(lambda g, j: (0, v0 + g)),
                colblk(lambda g, j: (0, m0 + g * rep + j)), colblk(lambda g, j: (0, m0 + g * rep + j)),
                pl.BlockSpec((None, SEQ, BLK), lambda g, j: (g * rep + j, 0, 0)), tab, tab, tab]
    args = [proj, proj, proj, mixed, dmix, lse, *tables]
    if has_sink:
        in_specs.append(pl.BlockSpec((None, 8, BLK), lambda g, j: (g * rep + j, 0, 0)))
        args.append(sink_b)
    n_in = len(args)
    in_specs.append(pl.BlockSpec(memory_space=pl.ANY))
    args.append(dproj)
    acc = pltpu.VMEM((SEQ, BLK), F32)
    return pl.pallas_call(
        body, name=name, grid=(n_kv, rep), in_specs=in_specs,
        out_specs=[pl.BlockSpec(memory_space=pl.ANY), pl.BlockSpec((None, 8, BLK), lambda g, j: (g * rep + j, 0, 0))],
        out_shape=[jax.ShapeDtypeStruct(dproj.shape, BF16), jax.ShapeDtypeStruct((n_heads, 8, BLK), F32)],
        input_output_aliases={n_in: 0},
        scratch_shapes=[acc, acc, acc, acc, acc, acc, pltpu.VMEM((3, SEQ, BLK), BF16), pltpu.SemaphoreType.DMA((3,))],
        compiler_params=_cp(("arbitrary", "arbitrary")))(*args)


B_PATTERNS = tuple((w // d, d) for w, d in DILATED_PATTERNS)
C_PATTERNS = ((C_WINDOW - 1, 1),)


ANY = pl.BlockSpec(memory_space=pl.ANY)
CHIP_MASKS = ((1, 0), (0, 1), (1, 1))


def _coords():
    return lax.axis_index("x"), lax.axis_index("y"), lax.axis_index("c")


def _flip(v, m):
    return 1 - v if m else v


def _into_slot(name, w, layer, k_idx, dtype, run_after=None):
    _, rows, cols = w.shape
    tr = rows // 8 if rows % 64 == 0 else rows

    def body(k_ref, w_ref, *rest):
        rest[-1][...] = w_ref[...].astype(dtype)

    in_specs = [pl.BlockSpec((None, tr, cols), lambda i, k: (layer, i, 0))]
    args = [k_idx, w]
    if run_after is not None:
        in_specs.append(pl.BlockSpec(memory_space=pl.ANY))
        args.append(run_after)
    return pl.pallas_call(
        body, name=name,
        grid_spec=pltpu.PrefetchScalarGridSpec(
            num_scalar_prefetch=1, grid=(rows // tr,), in_specs=in_specs,
            out_specs=pl.BlockSpec((None, tr, cols), lambda i, k: (k[0], i, 0))),
        out_shape=jax.ShapeDtypeStruct((N_CHIPS, rows, cols), dtype),
        compiler_params=_cp(("parallel",)))(*args)


HBM_SPEC = pl.BlockSpec(memory_space=pltpu.HBM)
SEM_SPEC = pl.BlockSpec(memory_space=pltpu.SEMAPHORE)
TOKEN_SPEC = pl.BlockSpec(memory_space=pltpu.VMEM)
TOKEN_SHAPE = jax.ShapeDtypeStruct((8, BLK), F32)
DATAFLOW = pltpu.SideEffectType.DATAFLOW_SIDE_EFFECTING


def _hbm(a):
    return pltpu.with_memory_space_constraint(a, pltpu.HBM)


def _hbm_like(bufs):
    return [pltpu.HBM(b.shape, b.dtype) for b in bufs]


def _gather_start(name, stages):
    flat = [b for st in stages for b in st]
    n, ns = len(flat), len(stages)

    def body(*refs):
        ins = refs[:n]
        sems = refs[n:n + 2 * ns]
        token = refs[-1]
        x, y, c = _coords()
        k_me = 2 * x + y
        a = 0
        for s, st in enumerate(stages):
            for i in range(len(st)):
                mine = ins[a].at[k_me, c]
                for m, (mx, my) in enumerate(CHIP_MASKS):
                    pltpu.make_async_remote_copy(src_ref=mine, dst_ref=mine, send_sem=sems[2 * s].at[i * 3 + m],
                                                 recv_sem=sems[2 * s + 1].at[i * 3 + m],
                                                 device_id=(_flip(x, mx), _flip(y, my), c), device_id_type=MESH).start()
                a += 1
        token[...] = jnp.zeros_like(token)

    sem_shapes = []
    for st in stages:
        sem_shapes += [pltpu.SemaphoreType.DMA((3 * len(st),))] * 2
    out = pl.pallas_call(
        body, name=name, in_specs=[HBM_SPEC] * n,
        out_specs=tuple([SEM_SPEC] * (2 * ns) + [HBM_SPEC] * n + [TOKEN_SPEC]),
        out_shape=tuple(sem_shapes + _hbm_like(flat) + [TOKEN_SHAPE]),
        input_output_aliases={i: 2 * ns + i for i in range(n)},
        compiler_params=pltpu.CompilerParams(has_side_effects=DATAFLOW),
    )(*[_hbm(b) for b in flat])
    sems, bufs, token = out[:2 * ns], out[2 * ns:2 * ns + n], out[-1]
    res, a = [], 0
    for s, st in enumerate(stages):
        res.append((sems[2 * s], sems[2 * s + 1], list(bufs[a:a + len(st)])))
        a += len(st)
    return res, token


def _gather_forward(name, stage, after):
    ssem_in, rsem_in, bufs = stage
    n = len(bufs)

    def body(*refs):
        ins = refs[:n]
        s_in, r_in, _ = refs[n:n + 3]
        s_out, r_out = refs[n + 3:n + 5]
        token = refs[-1]
        x, y, c = _coords()
        for i in range(n):
            for m, (mx, my) in enumerate(CHIP_MASKS):
                kp = 2 * _flip(x, mx) + _flip(y, my)
                blk = ins[i].at[kp, c]
                got = pltpu.make_async_remote_copy(src_ref=blk, dst_ref=blk, send_sem=s_in.at[i * 3 + m],
                                                   recv_sem=r_in.at[i * 3 + m], device_id=(x, y, 1 - c), device_id_type=MESH)
                got.wait_send()
                got.wait_recv()
                pltpu.make_async_remote_copy(src_ref=blk, dst_ref=blk, send_sem=s_out.at[i * 3 + m],
                                             recv_sem=r_out.at[i * 3 + m], device_id=(x, y, 1 - c), device_id_type=MESH).start()
        token[...] = jnp.zeros_like(token)

    sem = pltpu.SemaphoreType.DMA((3 * n,))
    out = pl.pallas_call(
        body, name=name, in_specs=[HBM_SPEC] * n + [SEM_SPEC, SEM_SPEC, ANY],
        out_specs=tuple([SEM_SPEC, SEM_SPEC] + [HBM_SPEC] * n + [TOKEN_SPEC]),
        out_shape=tuple([sem, sem] + _hbm_like(bufs) + [TOKEN_SHAPE]),
        input_output_aliases={i: 2 + i for i in range(n)},
        compiler_params=pltpu.CompilerParams(has_side_effects=DATAFLOW),
    )(*bufs, ssem_in, rsem_in, after)
    return (out[0], out[1], list(out[2:2 + n])), out[-1]


def _gather_wait(name, stage, after):
    ssem, rsem, bufs = stage
    n = len(bufs)

    def body(*refs):
        ins = refs[:n]
        s_in, r_in, _ = refs[n:n + 3]
        x, y, c = _coords()
        for i in range(n):
            for m, (mx, my) in enumerate(CHIP_MASKS):
                kp = 2 * _flip(x, mx) + _flip(y, my)
                sent, got = ins[i].at[kp, c], ins[i].at[kp, 1 - c]
                cp = pltpu.make_async_remote_copy(src_ref=sent, dst_ref=got, send_sem=s_in.at[i * 3 + m],
                                                  recv_sem=r_in.at[i * 3 + m], device_id=(x, y, 1 - c), device_id_type=MESH)
                cp.wait_send()
                cp.wait_recv()

    out = pl.pallas_call(
        body, name=name, in_specs=[HBM_SPEC] * n + [SEM_SPEC, SEM_SPEC, ANY],
        out_specs=tuple([HBM_SPEC] * n), out_shape=tuple(_hbm_like(bufs)),
        input_output_aliases={i: i for i in range(n)},
        compiler_params=pltpu.CompilerParams(has_side_effects=DATAFLOW),
    )(*bufs, ssem, rsem, after)
    return list(out)


def _swap_start(name, grads):
    n = len(grads)

    def body(*refs):
        ins, lands = refs[:n], refs[n:2 * n]
        ssem, rsem = refs[2 * n:2 * n + 2]
        x, y, c = _coords()
        for a in range(n):
            for j in range(N_CHIPS):
                pltpu.make_async_remote_copy(src_ref=ins[a].at[j, 1 - c], dst_ref=lands[a].at[j],
                                             send_sem=ssem.at[a * N_CHIPS + j], recv_sem=rsem.at[a * N_CHIPS + j],
                                             device_id=(x, y, 1 - c), device_id_type=MESH).start()

    sem = pltpu.SemaphoreType.DMA((N_CHIPS * n,))
    land_shapes = [pltpu.HBM((N_CHIPS,) + g.shape[2:], g.dtype) for g in grads]
    out = pl.pallas_call(
        body, name=name, in_specs=[HBM_SPEC] * (2 * n),
        out_specs=tuple([SEM_SPEC, SEM_SPEC] + [HBM_SPEC] * (2 * n)),
        out_shape=tuple([sem, sem] + _hbm_like(grads) + land_shapes),
        input_output_aliases={i: 2 + i for i in range(2 * n)},
        compiler_params=pltpu.CompilerParams(has_side_effects=DATAFLOW),
    )(*[_hbm(g) for g in grads], *[_hbm(lax.empty((N_CHIPS,) + g.shape[2:], g.dtype)) for g in grads])
    return out[0], out[1], list(out[2:2 + n]), list(out[2 + n:])


def _swap_wait(name, started, after):
    ssem, rsem, grads, lands = started
    n = len(grads)
    after = after if isinstance(after, tuple) else (after,)

    def body(*refs):
        ins, lnd = refs[:n], refs[n:2 * n]
        s_in, r_in = refs[2 * n:2 * n + 2]
        x, y, c = _coords()
        for a in range(n):
            for j in range(N_CHIPS):
                cp = pltpu.make_async_remote_copy(src_ref=ins[a].at[j, 1 - c], dst_ref=lnd[a].at[j],
                                                  send_sem=s_in.at[a * N_CHIPS + j], recv_sem=r_in.at[a * N_CHIPS + j],
                                                  device_id=(x, y, 1 - c), device_id_type=MESH)
                cp.wait_send()
                cp.wait_recv()

    out = pl.pallas_call(
        body, name=name, in_specs=[HBM_SPEC] * (2 * n) + [SEM_SPEC, SEM_SPEC] + [ANY] * len(after),
        out_specs=tuple([HBM_SPEC] * (2 * n)), out_shape=tuple(_hbm_like(grads) + _hbm_like(lands)),
        input_output_aliases={i: i for i in range(2 * n)},
        compiler_params=pltpu.CompilerParams(has_side_effects=DATAFLOW),
    )(*grads, *lands, ssem, rsem, *after)
    return list(out[:n]), list(out[n:])


def _scatter_start(name, parts):
    n = len(parts)

    def body(*refs):
        ins, lands = refs[:n], refs[n:2 * n]
        ssem, rsem = refs[2 * n:2 * n + 2]
        x, y, c = _coords()
        k_me = 2 * x + y
        for a in range(n):
            for m, (mx, my) in enumerate(CHIP_MASKS):
                px, py = _flip(x, mx), _flip(y, my)
                pltpu.make_async_remote_copy(src_ref=ins[a].at[2 * px + py], dst_ref=lands[a].at[k_me],
                                             send_sem=ssem.at[a * 3 + m], recv_sem=rsem.at[a * 3 + m],
                                             device_id=(px, py, c), device_id_type=MESH).start()

    sem = pltpu.SemaphoreType.DMA((3 * n,))
    out = pl.pallas_call(
        body, name=name, in_specs=[HBM_SPEC] * (2 * n),
        out_specs=tuple([SEM_SPEC, SEM_SPEC] + [HBM_SPEC] * (2 * n)),
        out_shape=tuple([sem, sem] + _hbm_like(parts) + _hbm_like(parts)),
        input_output_aliases={i: 2 + i for i in range(2 * n)},
        compiler_params=pltpu.CompilerParams(has_side_effects=DATAFLOW),
    )(*[_hbm(p) for p in parts], *[_hbm(lax.empty(p.shape, p.dtype)) for p in parts])
    return out[0], out[1], list(out[2:2 + n]), list(out[2 + n:])


def _scatter_wait(name, started, after):
    ssem, rsem, parts, lands = started
    n = len(parts)

    def body(*refs):
        ins, lnd = refs[:n], refs[n:2 * n]
        s_in, r_in, _ = refs[2 * n:2 * n + 3]
        x, y, c = _coords()
        k_me = 2 * x + y
        for a in range(n):
            for m, (mx, my) in enumerate(CHIP_MASKS):
                px, py = _flip(x, mx), _flip(y, my)
                cp = pltpu.make_async_remote_copy(src_ref=ins[a].at[2 * px + py], dst_ref=lnd[a].at[k_me],
                                                  send_sem=s_in.at[a * 3 + m], recv_sem=r_in.at[a * 3 + m],
                                                  device_id=(px, py, c), device_id_type=MESH)
                cp.wait_send()
                cp.wait_recv()

    out = pl.pallas_call(
        body, name=name, in_specs=[HBM_SPEC] * (2 * n) + [SEM_SPEC, SEM_SPEC, ANY],
        out_specs=tuple([HBM_SPEC] * (2 * n)), out_shape=tuple(_hbm_like(parts) + _hbm_like(lands)),
        input_output_aliases={i: i for i in range(2 * n)},
        compiler_params=pltpu.CompilerParams(has_side_effects=DATAFLOW),
    )(*parts, *lands, ssem, rsem, after)
    return list(out[:n]), list(out[n:])


def _pair_gather_start(name, bufs):
    n = len(bufs)

    def body(*refs):
        ins = refs[:n]
        ssem, rsem = refs[n:n + 2]
        x, y, c = _coords()
        for a in range(n):
            mine = ins[a].at[c]
            pltpu.make_async_remote_copy(src_ref=mine, dst_ref=mine, send_sem=ssem.at[a], recv_sem=rsem.at[a],
                                         device_id=(x, y, 1 - c), device_id_type=MESH).start()

    sem = pltpu.SemaphoreType.DMA((n,))
    out = pl.pallas_call(
        body, name=name, in_specs=[HBM_SPEC] * n, out_specs=tuple([SEM_SPEC, SEM_SPEC] + [HBM_SPEC] * n),
        out_shape=tuple([sem, sem] + _hbm_like(bufs)),
        input_output_aliases={i: 2 + i for i in range(n)},
        compiler_params=pltpu.CompilerParams(has_side_effects=DATAFLOW),
    )(*[_hbm(b) for b in bufs])
    return out[0], out[1], list(out[2:])


def _pair_gather_wait(name, started, after):
    ssem, rsem, bufs = started
    n = len(bufs)

    def body(*refs):
        ins = refs[:n]
        s_in, r_in, _ = refs[n:n + 3]
        x, y, c = _coords()
        for a in range(n):
            cp = pltpu.make_async_remote_copy(src_ref=ins[a].at[c], dst_ref=ins[a].at[1 - c], send_sem=s_in.at[a],
                                              recv_sem=r_in.at[a], device_id=(x, y, 1 - c), device_id_type=MESH)
            cp.wait_send()
            cp.wait_recv()

    out = pl.pallas_call(
        body, name=name, in_specs=[HBM_SPEC] * n + [SEM_SPEC, SEM_SPEC, ANY],
        out_specs=tuple([HBM_SPEC] * n), out_shape=tuple(_hbm_like(bufs)),
        input_output_aliases={i: i for i in range(n)},
        compiler_params=pltpu.CompilerParams(has_side_effects=DATAFLOW),
    )(*bufs, ssem, rsem, after)
    return list(out)


DEV_MASKS = tuple((mx, my, mc) for mx in (0, 1) for my in (0, 1) for mc in (0, 1) if (mx, my, mc) != (0, 0, 0))


def _gather_small(buf, run_after):
    def body(in_ref, _, out_ref, ssem, rsem, lsem):
        x, y, c = _coords()
        me = 4 * x + 2 * y + c
        cps = [pltpu.make_async_copy(in_ref, out_ref.at[me], lsem)]
        cps[0].start()
        for t, (mx, my, mc) in enumerate(DEV_MASKS):
            cp = pltpu.make_async_remote_copy(src_ref=in_ref, dst_ref=out_ref.at[me], send_sem=ssem.at[t],
                                              recv_sem=rsem.at[t], device_id=(_flip(x, mx), _flip(y, my), _flip(c, mc)),
                                              device_id_type=MESH)
            cp.start()
            cps.append(cp)
        for cp in cps:
            cp.wait()

    return pl.pallas_call(
        body, name="gather_small", in_specs=[ANY, ANY], out_specs=ANY,
        out_shape=jax.ShapeDtypeStruct((N_DEV,) + buf.shape, buf.dtype),
        scratch_shapes=[pltpu.SemaphoreType.DMA((N_DEV - 1,)), pltpu.SemaphoreType.DMA((N_DEV - 1,)),
                        pltpu.SemaphoreType.DMA(())],
        compiler_params=pltpu.CompilerParams(has_side_effects=True),
    )(buf, run_after)


def _row_tile(rows):
    return rows // 2 if rows % 16 == 0 else rows


def _pair_add(name, grad, got, c_idx):
    _, _, r2, cols = grad.shape
    tr = _row_tile(r2)

    def body(c_ref, a_ref, b_ref, o_ref):
        o_ref[...] = (a_ref[...].astype(F32) + b_ref[...].astype(F32)).astype(BF16)

    return pl.pallas_call(
        body, name=name,
        grid_spec=pltpu.PrefetchScalarGridSpec(
            num_scalar_prefetch=1, grid=(N_CHIPS, r2 // tr),
            in_specs=[pl.BlockSpec((None, None, tr, cols), lambda j, i, c: (j, c[0], i, 0)),
                      pl.BlockSpec((None, tr, cols), lambda j, i, c: (j, i, 0))],
            out_specs=pl.BlockSpec((None, tr, cols), lambda j, i, c: (j, i, 0))),
        out_shape=jax.ShapeDtypeStruct((N_CHIPS, r2, cols), BF16),
        compiler_params=_cp(("parallel", "parallel")))(c_idx, grad, got)


def _chip_add(name, part, got, kc_idx):
    _, r2, cols = got.shape
    tr = _row_tile(r2)

    def body(k_ref, p_ref, g1_ref, g2_ref, g3_ref, o_ref):
        acc = p_ref[...].astype(F32)
        for g_ref in (g1_ref, g2_ref, g3_ref):
            acc = acc + g_ref[...].astype(F32)
        o_ref[...] = acc

    def slot(d):
        return pl.BlockSpec((None, tr, cols), lambda i, k: ((k[0] + d) % N_CHIPS, i, 0))

    return pl.pallas_call(
        body, name=name,
        grid_spec=pltpu.PrefetchScalarGridSpec(
            num_scalar_prefetch=1, grid=(r2 // tr,),
            in_specs=[slot(0), slot(1), slot(2), slot(3)],
            out_specs=pl.BlockSpec((None, tr, cols), lambda i, k: (k[1], i, 0))),
        out_shape=jax.ShapeDtypeStruct((2, r2, cols), F32),
        compiler_params=_cp(("parallel",)))(kc_idx, part, got, got, got)


def _adam_math(w, g, m, v):
    m2 = ADAM_B1 * m + (1.0 - ADAM_B1) * g
    v2 = ADAM_B2 * v + (1.0 - ADAM_B2) * (g * g)
    m_hat = m2 / (1.0 - ADAM_B1 ** ADAM_STEP)
    v_hat = v2 / (1.0 - ADAM_B2 ** ADAM_STEP)
    delta = -ADAM_LR * (m_hat / (jnp.sqrt(v_hat) + ADAM_EPS) + ADAM_WD * w)
    return delta, m2, v2


def _adamw_matrix(name, w, g_layers, m, v):
    _, rows, cols = w.shape
    tr = rows // 8

    def body(w_ref, g0_ref, g1_ref, m_ref, v_ref, go_ref, d_ref, mo_ref, vo_ref):
        g = jnp.where(pl.program_id(0) == 0, g0_ref[...], g1_ref[...])
        go_ref[...] = g
        d_ref[...], mo_ref[...], vo_ref[...] = _adam_math(w_ref[...], g, m_ref[...], v_ref[...])

    lay = pl.BlockSpec((None, tr, cols), lambda l, i: (l, i, 0))
    flat = pl.BlockSpec((tr, cols), lambda l, i: (i, 0))
    shp = jax.ShapeDtypeStruct(w.shape, F32)
    return pl.pallas_call(body, name=name, grid=(DEPTH, rows // tr), in_specs=[lay, flat, flat, lay, lay],
                          out_specs=[lay, lay, lay, lay], out_shape=[shp, shp, shp, shp],
                          compiler_params=_cp(("parallel", "parallel")))(w, g_layers[0], g_layers[1], m, v)


def _sum_small(gathered):
    def body(g_ref, o_ref):
        acc = g_ref[0]
        for d in range(1, N_DEV):
            acc = acc + g_ref[d]
        o_ref[...] = acc

    return pl.pallas_call(body, name="sum_small", out_shape=jax.ShapeDtypeStruct(gathered.shape[1:], F32),
                          compiler_params=_cp())(gathered)


def _adamw_small(w, g, m, v):
    def body(w_ref, g_ref, m_ref, v_ref, d_ref, mo_ref, vo_ref):
        d_ref[...], mo_ref[...], vo_ref[...] = _adam_math(w_ref[...], g_ref[...], m_ref[...], v_ref[...])

    shp = jax.ShapeDtypeStruct(w.shape, F32)
    return pl.pallas_call(body, name="adamw_small", out_shape=[shp, shp, shp], compiler_params=_cp())(w, g, m, v)


def _pack(arrays, rows):
    flat = jnp.concatenate([a.reshape(-1) for a in arrays])
    return jnp.pad(flat, (0, rows * BLK - flat.shape[0])).reshape(rows, BLK)


def _unpack(buf, shapes):
    flat = buf.reshape(-1)
    out, pos = [], 0
    for s in shapes:
        n = math.prod(s)
        out.append(flat[pos:pos + n].reshape(s))
        pos += n
    return out


def _rows_for(shapes):
    n = sum(math.prod(s) for s in shapes)
    return -(-n // (8 * BLK)) * 8


def _rs_swap(tag, grads):
    return _swap_start(f"rs_swap_start{tag}", [g.reshape(N_CHIPS, 2, g.shape[1] // 2, g.shape[2]) for g in grads])


def _rs_scatter(tag, swapping, after, c_idx):
    split, got = _swap_wait(f"rs_swap_wait{tag}", swapping, after)
    parts = [_pair_add(f"rs_pair_add{tag}_{i}", s, r, c_idx) for i, (s, r) in enumerate(zip(split, got))]
    return _scatter_start(f"rs_scatter_start{tag}", parts)


def _rs_reduce(tag, started, after, kc_idx):
    parts, lands = _scatter_wait(f"rs_scatter_wait{tag}", started, after)
    halves = [_chip_add(f"rs_chip_add{tag}_{i}", p, r, kc_idx) for i, (p, r) in enumerate(zip(parts, lands))]
    return _pair_gather_start(f"rs_pair_gather_start{tag}", halves)


def _rs_finish(tag, gathering, after):
    full = _pair_gather_wait(f"rs_pair_gather_wait{tag}", gathering, after)
    return [f.reshape(2 * f.shape[1], f.shape[2]) for f in full]


def kernel(x, w_in, lb_logits, a_norm_w, c_sinks, w_out, ln1_g, ln1_b, w_gate, w_up, conv_w, conv_b, w_down, ln2_g, ln2_b, loss_target, m_w_in, m_lb_logits, m_a_norm_w, m_c_sinks, m_w_out, m_ln1_g, m_ln1_b, m_w_gate, m_w_up, m_conv_w, m_conv_b, m_w_down, m_ln2_g, m_ln2_b, v_w_in, v_lb_logits, v_a_norm_w, v_c_sinks, v_w_out, v_ln1_g, v_ln1_b, v_w_gate, v_w_up, v_conv_w, v_conv_b, v_w_down, v_ln2_g, v_ln2_b):
    cx, cy, cc = _coords()
    c_idx = jnp.reshape(cc, (1,)).astype(jnp.int32)
    k_me = 2 * cx + cy
    k_idx = jnp.reshape(k_me, (1,)).astype(jnp.int32)
    kc_idx = jnp.stack([k_me, cc]).astype(jnp.int32)

    def slot(nm, w, l, run_after=None):
        b = _into_slot(f"slot_{nm}{l}", w, l, k_idx, BF16, run_after)
        return b.reshape(N_CHIPS, 2, b.shape[1] // 2, b.shape[2])

    cw_slot = _into_slot("slot_cw", conv_w.reshape(1, DEPTH * CONV_WIDTH, FF_SHARD), 0, k_idx, F32)
    cw_slot = cw_slot.reshape(N_CHIPS, DEPTH, CONV_WIDTH, FF_SHARD)
    first, token = _gather_start("gather_start0", [[slot("wi", w_in, 0), cw_slot]])
    sl = [{nm: slot(nm, w, l, token) for nm, w in (("wi", w_in), ("wo", w_out), ("wg", w_gate), ("wu", w_up), ("wd", w_down))
           if (nm, l) != ("wi", 0)} for l in range(DEPTH)]
    order = [(l, nm) for l in range(DEPTH) for nm in ("wi", "wo", "wg", "wu", "wd")][1:]
    rest, token = _gather_start("gather_start1", [[sl[l][nm]] for l, nm in order])
    stage_of = {key: st for key, st in zip(order, rest)}

    def mat(b):
        return b.reshape(N_CHIPS, 2 * b.shape[2], b.shape[3])

    h = x[0]
    h_bf = _to_bf16("x_bf16", h, token)
    fwd0, token = _gather_forward("gather_fwd0", first[0], h_bf)
    wi0, cw_all = _gather_wait("gather_wait0", fwd0, token)
    cw_full = jnp.transpose(cw_all, (1, 2, 0, 3)).reshape(DEPTH, CONV_WIDTH, D_FF)
    tables = _rope_tables()

    passing = {}

    def pass_on(l, nm, after):
        passing[(l, nm)] = _gather_forward(f"gather_fwd_{nm}{l}", stage_of[(l, nm)], after)

    def arrived(l, nm, after):
        i = order.index((l, nm))
        if i + 1 < len(order):
            pass_on(*order[i + 1], after)
            after = passing[order[i + 1]][1]
        return mat(_gather_wait(f"gather_wait_{nm}{l}", passing[(l, nm)][0], after)[0])

    saved = []
    weights = []
    for l in range(DEPTH):
        wi = mat(wi0) if l == 0 else arrived(l, "wi", h)
        proj = _fwd_colsharded(f"proj{l}", h_bf, wi)
        mixed, mixed_bf, raw = _hgrn_fwd(f"hgrn_fwd{l}", proj, lb_logits, a_norm_w[l], l)
        mixed, mixed_bf, lse_b = _attn_fwd(f"dilated_fwd{l}", proj, tables, None, mixed, mixed_bf, n_heads=B_HEADS, rep=1,
                                           q0=QB0, k0=KB0, v0=VB0, m0=A_HEADS, patterns=B_PATTERNS)
        if l == 0:
            pass_on(l, "wo", lse_b)
        sink_b = jnp.broadcast_to(c_sinks[l][:, None, None], (C_HEADS, 8, BLK))
        mixed, mixed_bf, lse_c = _attn_fwd(f"window_fwd{l}", proj, tables, sink_b, mixed, mixed_bf, n_heads=C_HEADS,
                                           rep=C_HEADS // C_KV_HEADS, q0=QC0, k0=KC0, v0=VC0, m0=A_HEADS + B_HEADS,
                                           patterns=C_PATTERNS)
        wo = arrived(l, "wo", lse_c)
        y1 = _fwd_rowsharded(f"wout{l}", mixed_bf, wo, OUT_SHARD)
        x1, x1_bf = _ln_fwd(f"ln1_fwd{l}", h, y1, ln1_g[l], ln1_b[l])
        wg = arrived(l, "wg", x1)
        g = _fwd_colsharded(f"gate{l}", x1_bf, wg, BF16)
        wu = arrived(l, "wu", g)
        u = _fwd_colsharded(f"up{l}", x1_bf, wu, BF16)
        hh = _conv_gate_fwd(f"conv_fwd{l}", g, u, cw_full[l], conv_b[l])
        wd = arrived(l, "wd", hh)
        y2 = _fwd_rowsharded(f"down{l}", hh, wd, FF_SHARD)
        weights.append(dict(wi=wi, wo=wo, wg=wg, wu=wu, wd=wd))
        saved.append((h, h_bf, proj, raw, lse_b, sink_b, lse_c, mixed, mixed_bf, y1, x1, x1_bf, g, u, hh, y2))
        if l + 1 < DEPTH:
            h, h_bf = _ln_fwd(f"ln2_fwd{l}", x1, y2, ln2_g[l], ln2_b[l])

    d_res = d_path = None
    small = [None] * DEPTH
    mat_grads = [None] * DEPTH
    late = {}
    prev_ffn = prev_mix_swap = None
    for l in reversed(range(DEPTH)):
        h_in, h_in_bf, proj, raw, lse_b, sink_b, lse_c, mixed, mixed_bf, y1, x1, x1_bf, g, u, hh, y2 = saved[l]
        wi, wo, wg, wu, wd = (weights[l][k] for k in ("wi", "wo", "wg", "wu", "wd"))
        if l == DEPTH - 1:
            dz2, dz2_bf, d_ln2g, d_ln2b, loss_part = _ln_loss_bwd(f"ln2_loss_bwd{l}", x1, y2, ln2_g[l], ln2_b[l],
                                                                  loss_target[0])
        else:
            dz2, dz2_bf, d_ln2g, d_ln2b = _ln_bwd(f"ln2_bwd{l}", x1, y2, ln2_g[l], d_res, d_path,
                                                  run_after=prev_mix_swap[2][0])
        dhh = _bwd_act_rowsharded(f"down_dx{l}", dz2_bf, wd, FF_SHARD, BF16)
        prev_mix = _rs_scatter(f"{l + 1}m", prev_mix_swap, dhh, c_idx) if prev_mix_swap else None
        d_wd = _bwd_w_rowsharded(f"down_dw{l}", hh, dz2_bf, FF_SHARD)
        dg, du, d_cw, d_cb = _conv_gate_bwd(f"conv_bwd{l}", g, u, cw_full[l], conv_b[l], dhh,
                                            run_after=prev_mix[2][0] if prev_mix else None)
        dx1 = _bwd_act_colsharded(f"gateup_dx{l}", [(dg, wg), (du, wu)])
        d_wg = _bwd_w_colsharded(f"gate_dw{l}", x1_bf, dg)
        d_wu = _bwd_w_colsharded(f"up_dw{l}", x1_bf, du)
        pins = ()
        if prev_ffn:
            late[l + 1] = [_rs_reduce(f"{l + 1}f", prev_ffn, d_wu, kc_idx)]
        ffn_swap = _rs_swap(f"{l}f", [d_wg, d_wu, d_wd])
        dz1, dz1_bf, d_ln1g, d_ln1b = _ln_bwd(f"ln1_bwd{l}", h_in, y1, ln1_g[l], dz2, dx1, run_after=ffn_swap[2][0])
        dmix = _bwd_act_rowsharded(f"wout_dx{l}", dz1_bf, wo, OUT_SHARD)
        d_wo = _bwd_w_rowsharded(f"wout_dw{l}", mixed_bf, dz1_bf, OUT_SHARD)
        if prev_mix:
            late[l + 1].append(_rs_reduce(f"{l + 1}m", prev_mix, d_wo, kc_idx))
            pins = tuple(g[2][0] for g in late[l + 1])
        s_ffn = _rs_scatter(f"{l}f", ffn_swap, (d_wo,) + pins, c_idx)
        dproj, d_nw, d_lb = _hgrn_bwd(f"hgrn_bwd{l}", proj, raw, dmix, lb_logits, a_norm_w[l], l, run_after=s_ffn[2][0])
        dproj, _ = _attn_bwd(f"dilated_bwd{l}", proj, mixed, dmix, lse_b, tables, None, dproj, n_kv=B_HEADS, rep=1,
                             q0=QB0, k0=KB0, v0=VB0, m0=A_HEADS, patterns=B_PATTERNS)
        dproj, d_sink = _attn_bwd(f"window_bwd{l}", proj, mixed, dmix, lse_c, tables, sink_b, dproj, n_kv=C_KV_HEADS,
                                  rep=C_HEADS // C_KV_HEADS, q0=QC0, k0=KC0, v0=VC0, m0=A_HEADS + B_HEADS,
                                  patterns=C_PATTERNS)
        dxp = _bwd_act_colsharded_full(f"proj_dx{l}", dproj, wi, residual=dz1 if l == 0 else None)
        d_wi = _bwd_w_colsharded(f"proj_dw{l}", h_in_bf, dproj)
        d_res, d_path = dz1, dxp
        prev_ffn, prev_mix_swap = s_ffn, _rs_swap(f"{l}m", [d_wi, d_wo])
        small[l] = (d_lb, d_nw.reshape(A_HEADS, 8, BLK)[:, 0].sum(0), d_sink[:, 0, 0], d_ln1g[0], d_ln1b[0],
                    d_cw, d_cb[0], d_ln2g[0], d_ln2b[0])
    grad_x2 = d_path
    grad_x = grad_x2[None]

    g_lb = small[0][0] + small[1][0]
    per_layer = [jnp.stack([small[0][i], small[1][i]]) for i in range(1, 9)]
    small_shapes = [(DEPTH, 4 * BLK), (DEPTH, BLK), (DEPTH, C_HEADS), (DEPTH, D_MODEL), (DEPTH, D_MODEL),
                    (DEPTH, CONV_WIDTH, D_FF), (DEPTH, D_FF), (DEPTH, D_MODEL), (DEPTH, D_MODEL), (BLK,)]
    rows = _rows_for(small_shapes)
    total = _sum_small(_gather_small(_pack([g_lb] + per_layer + [loss_part[0]], rows), prev_mix_swap[2][0]))
    g_lb, g_nw, g_sink, g_ln1g, g_ln1b, g_cw_full, g_cb, g_ln2g, g_ln2b, loss_row = _unpack(total, small_shapes)
    loss = loss_row[0]
    g_cw = lax.dynamic_slice_in_dim(g_cw_full, k_me * FF_SHARD, FF_SHARD, axis=2)

    sw = [lb_logits, a_norm_w, c_sinks, ln1_g, ln1_b, conv_w, conv_b, ln2_g, ln2_b]
    sg = [g_lb, g_nw, g_sink, g_ln1g, g_ln1b, g_cw, g_cb, g_ln2g, g_ln2b]
    sm = [m_lb_logits, m_a_norm_w, m_c_sinks, m_ln1_g, m_ln1_b, m_conv_w, m_conv_b, m_ln2_g, m_ln2_b]
    sv = [v_lb_logits, v_a_norm_w, v_c_sinks, v_ln1_g, v_ln1_b, v_conv_w, v_conv_b, v_ln2_g, v_ln2_b]
    shapes = [a.shape for a in sw]
    prow = _rows_for(shapes)
    sd, snm, snv = (_unpack(b, shapes) for b in _adamw_small(_pack(sw, prow), _pack(sg, prow), _pack(sm, prow), _pack(sv, prow)))

    names = ["w_in", "w_out", "w_gate", "w_up", "w_down"]
    mw = [w_in, w_out, w_gate, w_up, w_down]
    mm = [m_w_in, m_w_out, m_w_gate, m_w_up, m_w_down]
    mv = [v_w_in, v_w_out, v_w_gate, v_w_up, v_w_down]
    res = [None] * 5
    s_mix = _rs_scatter("0m", prev_mix_swap, total, c_idx)
    for l, (g_ffn, g_mix) in late.items():
        g_wg, g_wu, g_wd = _rs_finish(f"{l}f", g_ffn, s_mix[2][0])
        g_wi, g_wo = _rs_finish(f"{l}m", g_mix, s_mix[2][0])
        mat_grads[l] = [g_wi, g_wo, g_wg, g_wu, g_wd]
    ffn0 = _rs_finish("0f", _rs_reduce("0f", prev_ffn, s_mix[2][0], kc_idx), s_mix[2][0])
    for i, g0 in zip((2, 3, 4), ffn0):
        res[i] = _adamw_matrix(f"adamw_{names[i]}", mw[i], [g0, mat_grads[1][i]], mm[i], mv[i])
    mix0 = _rs_finish("0m", _rs_reduce("0m", s_mix, res[4][1], kc_idx), res[4][1])
    for i, g0 in zip((0, 1), mix0):
        res[i] = _adamw_matrix(f"adamw_{names[i]}", mw[i], [g0, mat_grads[1][i]], mm[i], mv[i])
    mg, md, mnm, mnv = ([r[j] for r in res] for j in range(4))

    def ordered(mat, sm_):
        return [mat[0], sm_[0], sm_[1], sm_[2], mat[1], sm_[3], sm_[4], mat[2], mat[3], sm_[5], sm_[6], mat[4], sm_[7], sm_[8]]

    return (loss, grad_x, *ordered(mg, sg), *ordered(md, sd), *ordered(mnm, snm), *ordered(mnv, snv))
```

```python
import functools
import math

import jax
import jax.numpy as jnp
from jax import lax
from jax.experimental import pallas as pl
from jax.experimental.pallas import tpu as pltpu

F32 = jnp.float32
BF16 = jnp.bfloat16

D_MODEL = 2048
SEQ = 2048
DEPTH = 2
HEAD_DIM = 128
A_HEADS = 4
B_HEADS = 6
C_HEADS = 6
C_KV_HEADS = 2
A_CHUNK = 16
DILATED_PATTERNS = ((128, 1), (512, 4), (2048, 16))
C_WINDOW = 128
ROPE_THETA = 500000.0
ROPE_DIM = HEAD_DIM // 4
D_FF = 5632
CONV_WIDTH = 3
LN_EPS = 1e-5
ALPHA = (2 * DEPTH) ** 0.25
IN_WIDTH = 5632
MIX_WIDTH = 2048
ADAM_LR = 0.001
ADAM_B1 = 0.9
ADAM_B2 = 0.999
ADAM_EPS = 1e-08
ADAM_WD = 0.01
ADAM_STEP = 10

N_CHIPS = 4
N_DEV = 8
FF_SHARD = D_FF // N_CHIPS
OUT_SHARD = MIX_WIDTH // N_CHIPS
BLK = 128
N_CHUNK = SEQ // A_CHUNK
SLAB = 32

QA0, FA0, IA0, GA0 = 0, 4, 8, 12
QB0, KB0, VB0 = 16, 22, 28
QC0, KC0, VC0 = 34, 40, 42

VMEM_LIMIT_V7X = 56 * 1024 * 1024
HI = lax.Precision.HIGHEST
MESH = pl.DeviceIdType.MESH


def _cp(sem=None, vmem=VMEM_LIMIT_V7X, **kw):
    return pltpu.CompilerParams(dimension_semantics=sem, vmem_limit_bytes=vmem, **kw)


def _sigmoid(x):
    return 1.0 / (1.0 + jnp.exp(-x))


def _gate_sigmoid(x):
    return 0.5 * jnp.tanh(0.5 * x) + 0.5


def _mm(name, pairs, dims, grid, a_specs, b_specs, out_spec, out_shape, nk=1, acc_shape=None):
    n_pairs = len(pairs)

    def body(*refs):
        o_ref = refs[2 * n_pairs]
        part = None
        for p in range(n_pairs):
            a = refs[2 * p][...].astype(BF16)
            b = refs[2 * p + 1][...].astype(BF16)
            t = lax.dot_general(a, b, dims, preferred_element_type=F32)
            part = t if part is None else part + t
        if nk == 1:
            o_ref[...] = part.astype(o_ref.dtype)
        else:
            acc = refs[2 * n_pairs + 1]
            k = pl.program_id(len(grid) - 1)

            @pl.when(k == 0)
            def _():
                acc[...] = part

            @pl.when(k > 0)
            def _():
                acc[...] += part

            @pl.when(k == nk - 1)
            def _():
                o_ref[...] = acc[...].astype(o_ref.dtype)

    in_specs, args = [], []
    for (a, b), sa, sb in zip(pairs, a_specs, b_specs):
        in_specs += [sa, sb]
        args += [a, b]
    sem = ("parallel",) * (len(grid) - (1 if nk > 1 else 0)) + (("arbitrary",) if nk > 1 else ())
    return pl.pallas_call(
        body, name=name, grid=grid, in_specs=in_specs, out_specs=out_spec, out_shape=out_shape,
        scratch_shapes=[pltpu.VMEM(acc_shape, F32)] if nk > 1 else [],
        compiler_params=_cp(sem),
    )(*args)


NN = (((1,), (0,)), ((), ()))
NT = (((1,), (1,)), ((), ()))
TN = (((0,), (0,)), ((), ()))
TM = 1024


def _fwd_colsharded(name, x, w_stk, out_dtype=F32):
    return _mm(name, [(x, w_stk)], NN, (N_CHIPS, SEQ // TM),
               [pl.BlockSpec((TM, D_MODEL), lambda j, i: (i, 0))],
               [pl.BlockSpec((None, D_MODEL, FF_SHARD), lambda j, i: (j, 0, 0))],
               pl.BlockSpec((TM, FF_SHARD), lambda j, i: (i, j)),
               jax.ShapeDtypeStruct((SEQ, D_FF), out_dtype))


def _fwd_rowsharded(name, a, w_stk, shard):
    tn = 512
    rows = N_CHIPS * shard
    return _mm(name, [(a, w_stk.reshape(rows, D_MODEL))], NN, (SEQ // TM, D_MODEL // tn),
               [pl.BlockSpec((TM, rows), lambda i, j: (i, 0))],
               [pl.BlockSpec((rows, tn), lambda i, j: (0, j))],
               pl.BlockSpec((TM, tn), lambda i, j: (i, j)),
               jax.ShapeDtypeStruct((SEQ, D_MODEL), F32))


def _bwd_act_colsharded(name, pairs):
    tn = 1024
    n = len(pairs)
    return _mm(name, pairs, NT, (SEQ // TM, D_MODEL // tn, N_CHIPS),
               [pl.BlockSpec((TM, FF_SHARD), lambda i, j, k: (i, k))] * n,
               [pl.BlockSpec((None, tn, FF_SHARD), lambda i, j, k: (k, j, 0))] * n,
               pl.BlockSpec((TM, tn), lambda i, j, k: (i, j)),
               jax.ShapeDtypeStruct((SEQ, D_MODEL), F32), nk=N_CHIPS, acc_shape=(TM, tn))


def _bwd_act_colsharded_full(name, dy, w_stk, residual=None):
    tn = 512

    def body(a_ref, w_ref, *rest):
        acc = None
        for k in range(N_CHIPS):
            t = lax.dot_general(a_ref[:, k * FF_SHARD:(k + 1) * FF_SHARD], w_ref[k], NT, preferred_element_type=F32)
            acc = t if acc is None else acc + t
        if residual is not None:
            acc = ALPHA * rest[0][...] + acc
        rest[-1][...] = acc

    out = pl.BlockSpec((TM, tn), lambda i, j: (i, j))
    extra = [] if residual is None else [residual]
    return pl.pallas_call(
        body, name=name, grid=(SEQ // TM, D_MODEL // tn),
        in_specs=[pl.BlockSpec((TM, D_FF), lambda i, j: (i, 0)),
                  pl.BlockSpec((N_CHIPS, tn, FF_SHARD), lambda i, j: (0, j, 0))] + [out] * len(extra),
        out_specs=out, out_shape=jax.ShapeDtypeStruct((SEQ, D_MODEL), F32),
        compiler_params=_cp(("parallel", "parallel")))(dy, w_stk, *extra)


def _bwd_act_rowsharded(name, dy, w_stk, shard, out_dtype=F32):
    return _mm(name, [(dy, w_stk)], NT, (N_CHIPS, SEQ // TM),
               [pl.BlockSpec((TM, D_MODEL), lambda j, i: (i, 0))],
               [pl.BlockSpec((None, shard, D_MODEL), lambda j, i: (j, 0, 0))],
               pl.BlockSpec((TM, shard), lambda j, i: (i, j)),
               jax.ShapeDtypeStruct((SEQ, N_CHIPS * shard), out_dtype))


def _bwd_w_colsharded(name, x, dy):
    tm = 1024
    return _mm(name, [(x, dy)], TN, (N_CHIPS, D_MODEL // tm),
               [pl.BlockSpec((SEQ, tm), lambda j, i: (0, i))],
               [pl.BlockSpec((SEQ, FF_SHARD), lambda j, i: (0, j))],
               pl.BlockSpec((None, tm, FF_SHARD), lambda j, i: (j, i, 0)),
               jax.ShapeDtypeStruct((N_CHIPS, D_MODEL, FF_SHARD), BF16))


def _bwd_w_rowsharded(name, a, dy, shard):
    tn = 1024
    return _mm(name, [(a, dy)], TN, (N_CHIPS, D_MODEL // tn),
               [pl.BlockSpec((SEQ, shard), lambda j, i: (0, j))],
               [pl.BlockSpec((SEQ, tn), lambda j, i: (0, i))],
               pl.BlockSpec((None, shard, tn), lambda j, i: (j, 0, i)),
               jax.ShapeDtypeStruct((N_CHIPS, shard, D_MODEL), BF16))


TR = 256


def _ln_fwd(name, x, y, g, b):
    def body(x_ref, y_ref, g_ref, b_ref, o_ref, ob_ref):
        z = ALPHA * x_ref[...] + y_ref[...]
        mu = jnp.mean(z, -1, keepdims=True)
        zc = z - mu
        var = jnp.mean(zc * zc, -1, keepdims=True)
        o = zc * lax.rsqrt(var + LN_EPS) * g_ref[...] + b_ref[...]
        o_ref[...] = o
        ob_ref[...] = o.astype(BF16)

    row = pl.BlockSpec((TR, D_MODEL), lambda i: (i, 0))
    vec = pl.BlockSpec((1, D_MODEL), lambda i: (0, 0))
    return pl.pallas_call(body, name=name, grid=(SEQ // TR,), in_specs=[row, row, vec, vec], out_specs=[row, row],
                          out_shape=[jax.ShapeDtypeStruct((SEQ, D_MODEL), F32), jax.ShapeDtypeStruct((SEQ, D_MODEL), BF16)],
                          compiler_params=_cp(("parallel",)))(x, y, g.reshape(1, -1), b.reshape(1, -1))


def _to_bf16(name, x, run_after):
    def body(x_ref, _, o_ref):
        o_ref[...] = x_ref[...].astype(BF16)

    row = pl.BlockSpec((TR, D_MODEL), lambda i: (i, 0))
    return pl.pallas_call(body, name=name, grid=(SEQ // TR,), in_specs=[row, pl.BlockSpec(memory_space=pl.ANY)],
                          out_specs=row, out_shape=jax.ShapeDtypeStruct((SEQ, D_MODEL), BF16),
                          compiler_params=_cp(("parallel",)))(x, run_after)


def _ln_bwd(name, x, y, g, d_res, d_path, run_after):
    def body(x_ref, y_ref, g_ref, r_ref, p_ref, _, dz_ref, dzb_ref, dg_ref, db_ref):
        dout = ALPHA * r_ref[...] + p_ref[...]
        z = ALPHA * x_ref[...] + y_ref[...]
        mu = jnp.mean(z, -1, keepdims=True)
        zc = z - mu
        rstd = lax.rsqrt(jnp.mean(zc * zc, -1, keepdims=True) + LN_EPS)
        zh = zc * rstd
        dzh = dout * g_ref[...]
        dz = rstd * (dzh - jnp.mean(dzh, -1, keepdims=True) - zh * jnp.mean(dzh * zh, -1, keepdims=True))
        dz_ref[...] = dz
        dzb_ref[...] = dz.astype(BF16)
        pg = jnp.sum(dout * zh, 0, keepdims=True)
        pb = jnp.sum(dout, 0, keepdims=True)

        @pl.when(pl.program_id(0) == 0)
        def _():
            dg_ref[...] = pg
            db_ref[...] = pb

        @pl.when(pl.program_id(0) > 0)
        def _():
            dg_ref[...] += pg
            db_ref[...] += pb

    row = pl.BlockSpec((TR, D_MODEL), lambda i: (i, 0))
    vec = pl.BlockSpec((1, D_MODEL), lambda i: (0, 0))
    args = [x, y, g.reshape(1, -1), d_res, d_path, run_after]
    in_specs = [row, row, vec, row, row, pl.BlockSpec(memory_space=pl.ANY)]
    vshape = jax.ShapeDtypeStruct((1, D_MODEL), F32)
    return pl.pallas_call(body, name=name, grid=(SEQ // TR,), in_specs=in_specs, out_specs=[row, row, vec, vec],
                          out_shape=[jax.ShapeDtypeStruct((SEQ, D_MODEL), F32), jax.ShapeDtypeStruct((SEQ, D_MODEL), BF16),
                                     vshape, vshape],
                          compiler_params=_cp(("arbitrary",)))(*args)


def _ln_loss_bwd(name, x, y, g, b, target):
    def body(x_ref, y_ref, g_ref, b_ref, t_ref, dz_ref, dzb_ref, dg_ref, db_ref, l_ref):
        z = ALPHA * x_ref[...] + y_ref[...]
        mu = jnp.mean(z, -1, keepdims=True)
        zc = z - mu
        rstd = lax.rsqrt(jnp.mean(zc * zc, -1, keepdims=True) + LN_EPS)
        zh = zc * rstd
        e = zh * g_ref[...] + b_ref[...] - t_ref[...]
        dout = e * (1.0 / D_MODEL)
        dzh = dout * g_ref[...]
        dz = rstd * (dzh - jnp.mean(dzh, -1, keepdims=True) - zh * jnp.mean(dzh * zh, -1, keepdims=True))
        dz_ref[...] = dz
        dzb_ref[...] = dz.astype(BF16)
        pg = jnp.sum(dout * zh, 0, keepdims=True)
        pb = jnp.sum(dout, 0, keepdims=True)
        part = jnp.full((8, BLK), 0.5 / D_MODEL * jnp.sum(e * e), F32)

        @pl.when(pl.program_id(0) == 0)
        def _():
            dg_ref[...] = pg
            db_ref[...] = pb
            l_ref[...] = part

        @pl.when(pl.program_id(0) > 0)
        def _():
            dg_ref[...] += pg
            db_ref[...] += pb
            l_ref[...] += part

    row = pl.BlockSpec((TR, D_MODEL), lambda i: (i, 0))
    vec = pl.BlockSpec((1, D_MODEL), lambda i: (0, 0))
    vshape = jax.ShapeDtypeStruct((1, D_MODEL), F32)
    return pl.pallas_call(body, name=name, grid=(SEQ // TR,), in_specs=[row, row, vec, vec, row],
                          out_specs=[row, row, vec, vec, pl.BlockSpec((8, BLK), lambda i: (0, 0))],
                          out_shape=[jax.ShapeDtypeStruct((SEQ, D_MODEL), F32), jax.ShapeDtypeStruct((SEQ, D_MODEL), BF16),
                                     vshape, vshape, jax.ShapeDtypeStruct((8, BLK), F32)],
                          compiler_params=_cp(("arbitrary",)))(x, y, g.reshape(1, -1), b.reshape(1, -1), target)


TC = 512


def _shift_down(x, s, rows):
    if s == 0:
        return x
    return jnp.where(rows >= s, pltpu.roll(x, s, axis=0), 0.0)


def _shift_up(x, s, rows):
    if s == 0:
        return x
    return jnp.where(rows < SEQ - s, pltpu.roll(x, SEQ - s, axis=0), 0.0)


def _conv_gate_fwd(name, g, u, cw, cb):
    def body(g_ref, u_ref, w_ref, b_ref, h_ref):
        gg = g_ref[...].astype(F32)
        rows = lax.broadcasted_iota(jnp.int32, gg.shape, 0)
        gc = b_ref[...] + w_ref[2:3, :] * gg
        gc = gc + w_ref[1:2, :] * _shift_down(gg, 1, rows)
        gc = gc + w_ref[0:1, :] * _shift_down(gg, 2, rows)
        h_ref[...] = (gc * _gate_sigmoid(gc) * u_ref[...].astype(F32)).astype(BF16)

    col = pl.BlockSpec((SEQ, TC), lambda j: (0, j))
    return pl.pallas_call(body, name=name, grid=(D_FF // TC,),
                          in_specs=[col, col, pl.BlockSpec((CONV_WIDTH, TC), lambda j: (0, j)),
                                    pl.BlockSpec((1, TC), lambda j: (0, j))],
                          out_specs=col, out_shape=jax.ShapeDtypeStruct((SEQ, D_FF), BF16),
                          compiler_params=_cp(("parallel",)))(g, u, cw, cb.reshape(1, -1))


def _conv_gate_bwd(name, g, u, cw, cb, dh, run_after=None):
    def body(g_ref, u_ref, w_ref, b_ref, dh_ref, *rest):
        dg_ref, du_ref, dw_ref, db_ref = rest[-4:]
        gg = g_ref[...].astype(F32)
        rows = lax.broadcasted_iota(jnp.int32, gg.shape, 0)
        g1 = _shift_down(gg, 1, rows)
        g2 = _shift_down(gg, 2, rows)
        gc = b_ref[...] + w_ref[2:3, :] * gg + w_ref[1:2, :] * g1 + w_ref[0:1, :] * g2
        sg = _gate_sigmoid(gc)
        act = gc * sg
        dh = dh_ref[...].astype(F32)
        du_ref[...] = (dh * act).astype(BF16)
        dgc = dh * u_ref[...].astype(F32) * (sg * (1.0 + gc * (1.0 - sg)))
        db_ref[...] = jnp.sum(dgc, 0, keepdims=True)
        dw_ref[2:3, :] = jnp.sum(dgc * gg, 0, keepdims=True)
        dw_ref[1:2, :] = jnp.sum(dgc * g1, 0, keepdims=True)
        dw_ref[0:1, :] = jnp.sum(dgc * g2, 0, keepdims=True)
        dg_ref[...] = (w_ref[2:3, :] * dgc + w_ref[1:2, :] * _shift_up(dgc, 1, rows)
                       + w_ref[0:1, :] * _shift_up(dgc, 2, rows)).astype(BF16)

    col = pl.BlockSpec((SEQ, TC), lambda j: (0, j))
    w3 = pl.BlockSpec((CONV_WIDTH, TC), lambda j: (0, j))
    w1 = pl.BlockSpec((1, TC), lambda j: (0, j))
    big = jax.ShapeDtypeStruct((SEQ, D_FF), BF16)
    extra = [] if run_after is None else [run_after]
    return pl.pallas_call(body, name=name, grid=(D_FF // TC,),
                          in_specs=[col, col, w3, w1, col] + [pl.BlockSpec(memory_space=pl.ANY)] * len(extra),
                          out_specs=[col, col, w3, w1],
                          out_shape=[big, big, jax.ShapeDtypeStruct((CONV_WIDTH, D_FF), F32),
                                     jax.ShapeDtypeStruct((1, D_FF), F32)],
                          compiler_params=_cp(("parallel",)))(g, u, cw, cb.reshape(1, -1), dh, *extra)


def _lbs_of(logits, layer):
    m = jnp.max(logits, 0, keepdims=True)
    e = jnp.exp(logits - m)
    p = e / jnp.sum(e, 0, keepdims=True)
    lb = jnp.zeros((1, BLK), F32)
    for r in range(1, layer + 1):
        lb = lb + p[r:r + 1, :]
    return lb, p


def _dlogits_of(p, dlb, layer):
    rows = lax.broadcasted_iota(jnp.int32, p.shape, 0)
    dp = jnp.where((rows >= 1) & (rows <= layer), dlb, 0.0)
    return p * (dp - jnp.sum(p * dp, 0, keepdims=True))


SROWS = SLAB * A_CHUNK
N_SLAB = N_CHUNK // SLAB


def _chunk_prefix(x, rowi):
    for s in (1, 2, 4, 8):
        x = x + jnp.where(rowi >= s, pltpu.roll(x, s, axis=0), 0.0)
    return x


def _chunk_suffix(x, rowi):
    for s in (1, 2, 4, 8):
        x = x + jnp.where(rowi < A_CHUNK - s, pltpu.roll(x, SROWS - s, axis=0), 0.0)
    return x


def _c3(x):
    return x.reshape(SLAB, A_CHUNK, BLK)


def _c2(x):
    return x.reshape(SROWS, BLK)


def _split(x):
    top = lax.bitcast_convert_type(lax.bitcast_convert_type(x, jnp.uint32) & jnp.uint32(0xFFFF0000), F32)
    return top.astype(BF16), (x - top).astype(BF16)


def _bmm(eq, a, b):
    ah, al = _split(a)
    bh, bl = _split(b)

    def mm(u, v):
        return jnp.einsum(eq, u, v, preferred_element_type=F32)

    return mm(ah, bh) + (mm(ah, bl) + mm(al, bh))


def _bmm_1pass(eq, a, b):
    return jnp.einsum(eq, a.astype(BF16), b.astype(BF16), preferred_element_type=F32)


def _slab_rows(s):
    return pl.ds(s * SROWS, SROWS)


def _hgrn_prep(q, f, lb):
    rowi = lax.broadcasted_iota(jnp.int32, (SROWS, BLK), 0) & (A_CHUNK - 1)
    sq = _gate_sigmoid(q)
    qc = q * sq
    sf = _sigmoid(f)
    fg = lb + (1.0 - lb) * sf
    kc = 1.0 - fg
    b = _chunk_prefix(jnp.log(fg), rowi)
    b3 = _c3(b)
    blast = b3[:, A_CHUNK - 1:A_CHUNK, :]
    eb = jnp.exp(b)
    ekb = _c2(jnp.exp(blast - b3))
    dec = jnp.exp(blast.reshape(SLAB, BLK))
    return rowi, sq, qc, sf, fg, kc, b, eb, ekb, dec


def _hgrn_slab_states(s, carry, v, ke, dec, dec_ref, u_ref, st_ref):
    dec_ref[pl.ds(s * SLAB, SLAB), :] = dec
    u_ref[...] = _bmm('ncv,nck->nvk', _c3(v), _c3(ke))

    def step(j, c):
        st_ref[j] = c
        return dec_ref[pl.ds(s * SLAB + j, 1), :] * c + u_ref[j]

    return lax.fori_loop(0, SLAB, step, carry)


def _hgrn_fwd(name, proj, lb_logits, nw, layer):
    def body(q_ref, f_ref, i_ref, g_ref, lg_ref, nw_ref, out_ref, outb_ref, raw_ref, dec_ref, u_ref, st_ref):
        lb, _ = _lbs_of(lg_ref[...], layer)
        ones = jnp.ones((BLK, BLK), BF16)
        carry = jnp.zeros((BLK, BLK), F32)
        for s in range(N_SLAB):
            rows = _slab_rows(s)
            v = i_ref[rows, :]
            rowi, sq, qc, sf, fg, kc, b, eb, ekb, dec = _hgrn_prep(q_ref[rows, :], f_ref[rows, :], lb)
            carry = _hgrn_slab_states(s, carry, v, kc * ekb, dec, dec_ref, u_ref, st_ref)
            o = _c2(_bmm('nck,nvk->ncv', _c3(qc * eb), st_ref[...]))
            qc3, kc3, b3, v3, row3 = _c3(qc), _c3(kc), _c3(b), _c3(v), _c3(rowi)
            col = lax.broadcasted_iota(jnp.int32, (SLAB, A_CHUNK, A_CHUNK), 2)
            att_all = jnp.zeros((SLAB, A_CHUNK, A_CHUNK), F32)
            for j in range(A_CHUNK):
                dj = jnp.exp(jnp.where(row3 >= j, b3 - b3[:, j:j + 1, :], -jnp.inf))
                a = jnp.dot(_c2(qc3 * dj * kc3[:, j:j + 1, :]).astype(BF16), ones, preferred_element_type=F32)
                att_all = jnp.where(col == j, _c3(a)[:, :, :A_CHUNK], att_all)
            o = o + _c2(_bmm_1pass('nij,njv->niv', att_all, v3))
            raw_ref[rows, :] = o
            r = lax.rsqrt(jnp.mean(o * o, -1, keepdims=True) + LN_EPS)
            gg = g_ref[rows, :]
            gated = o * r * nw_ref[...] * (gg * _gate_sigmoid(gg))
            out_ref[rows, :] = gated
            outb_ref[rows, :] = gated.astype(BF16)

    def colblk(c0):
        return pl.BlockSpec((SEQ, BLK), lambda h: (0, c0 + h))

    return pl.pallas_call(
        body, name=name, grid=(A_HEADS,),
        in_specs=[colblk(QA0), colblk(FA0), colblk(IA0), colblk(GA0),
                  pl.BlockSpec((DEPTH, BLK), lambda h: (0, h)), pl.BlockSpec((1, BLK), lambda h: (0, 0))],
        out_specs=[colblk(0), colblk(0), colblk(0)],
        out_shape=[jax.ShapeDtypeStruct((SEQ, MIX_WIDTH), F32), jax.ShapeDtypeStruct((SEQ, MIX_WIDTH), BF16),
                   jax.ShapeDtypeStruct((SEQ, A_HEADS * BLK), F32)],
        scratch_shapes=[pltpu.VMEM((N_CHUNK, BLK), F32), pltpu.VMEM((SLAB, BLK, BLK), F32),
                        pltpu.VMEM((SLAB, BLK, BLK), F32)],
        compiler_params=_cp(("parallel",)))(proj, proj, proj, proj, lb_logits, nw.reshape(1, -1))


def _col_block_copies(stage, sems, dst, col_blocks, first):
    return [pltpu.make_async_copy(stage.at[first + t], dst.at[:, pl.ds(pl.multiple_of(cb * BLK, BLK), BLK)],
                                  sems.at[first + t]) for t, cb in enumerate(col_blocks)]


def _start_col_blocks(stage, sems, dst, col_blocks, first=0):
    for cp in _col_block_copies(stage, sems, dst, col_blocks, first):
        cp.start()


def _wait_col_blocks(stage, sems, dst, count, first=0):
    for cp in _col_block_copies(stage, sems, dst, [0] * count, first):
        cp.wait()


def _hgrn_bwd(name, proj, raw, dmix, lb_logits, nw, layer, run_after=None):
    extra = [] if run_after is None else [run_after]

    def body(q_ref, f_ref, i_ref, g_ref, raw_ref, do_ref, lg_ref, nw_ref, *rest):
        (dproj_ref, dnw_ref, dlg_ref,
         dec_ref, u_ref, st_ref, h_ref, dbs_ref, dkc_ref, tot_ref, stage, stage_sem) = rest[-12:]
        dq_ref, df_ref, di_ref, dg_ref = (stage.at[t] for t in range(4))
        lb, p = _lbs_of(lg_ref[...], layer)
        ones = jnp.ones((BLK, BLK), BF16)
        nwv = nw_ref[...]

        carry = jnp.zeros((BLK, BLK), F32)
        for s in range(N_SLAB):
            rows = _slab_rows(s)
            rowi, sq, qc, sf, fg, kc, b, eb, ekb, dec = _hgrn_prep(q_ref[rows, :], f_ref[rows, :], lb)
            carry = _hgrn_slab_states(s, carry, i_ref[rows, :], kc * ekb, dec, dec_ref, u_ref,
                                      st_ref.at[pl.ds(s * SLAB, SLAB)])

        @pl.when(pl.program_id(0) > 0)
        def _():
            _wait_col_blocks(stage, stage_sem, dproj_ref, 4)

        carry = jnp.zeros((BLK, BLK), F32)
        dnw = jnp.zeros((1, BLK), F32)
        for s in reversed(range(N_SLAB)):
            rows = _slab_rows(s)
            q, v = q_ref[rows, :], i_ref[rows, :]
            rowi, sq, qc, sf, fg, kc, b, eb, ekb, dec = _hgrn_prep(q, f_ref[rows, :], lb)
            ke = kc * ekb
            qe = qc * eb

            o = raw_ref[rows, :]
            gg = g_ref[rows, :]
            sgg = _gate_sigmoid(gg)
            dout = do_ref[rows, :]
            r = lax.rsqrt(jnp.mean(o * o, -1, keepdims=True) + LN_EPS)
            oh = o * r
            dg_ref[rows, :] = (dout * oh * nwv * (sgg * (1.0 + gg * (1.0 - sgg)))).astype(BF16)
            dn = dout * (gg * sgg)
            dnw = dnw + jnp.sum(dn * oh, 0, keepdims=True)
            doh = dn * nwv
            do = r * (doh - oh * jnp.mean(doh * oh, -1, keepdims=True))
            do3, qe3, v3, ke3 = _c3(do), _c3(qe), _c3(v), _c3(ke)

            u_ref[...] = _bmm('ncv,nck->nvk', do3, qe3)

            def step(jj, c, s=s):
                j = SLAB - 1 - jj
                h_ref[j] = c
                return u_ref[j] + dec_ref[pl.ds(s * SLAB + j, 1), :] * c

            carry = lax.fori_loop(0, SLAB, step, carry)

            hh = h_ref[...]
            dqc = _c2(_bmm('ncv,nvk->nck', do3, st_ref[pl.ds(s * SLAB, SLAB)])) * eb
            dkc = _c2(_bmm('ncv,nvk->nck', v3, hh)) * ekb
            dv = _c2(_bmm_1pass('nck,nvk->ncv', ke3, hh))

            qc3, kc3, b3, row3 = _c3(qc), _c3(kc), _c3(b), _c3(rowi)
            datt_all = _bmm('niv,njv->nij', do3, v3)
            col = lax.broadcasted_iota(jnp.int32, datt_all.shape, 2)
            att_all = jnp.zeros_like(datt_all)
            for j in range(A_CHUNK):
                dj = jnp.exp(jnp.where(row3 >= j, b3 - b3[:, j:j + 1, :], -jnp.inf))
                kj = kc3[:, j:j + 1, :]
                att = _c3(jnp.dot(_c2(qc3 * dj * kj).astype(BF16), ones, preferred_element_type=F32))
                att_all = jnp.where(col == j, att[:, :, :A_CHUNK], att_all)
                md = dj * datt_all[:, :, j:j + 1]
                dqc = dqc + _c2(md * kj)
                dkc = dkc + _c2(jnp.where(row3 == j, jnp.sum(md * qc3, 1, keepdims=True), 0.0))
            dv = dv + _c2(_bmm_1pass('nij,niv->njv', att_all, do3))
            di_ref[rows, :] = dv.astype(BF16)
            dq_ref[rows, :] = (dqc * (sq * (1.0 + q * (1.0 - sq)))).astype(BF16)

            dbs = _chunk_suffix(qc * dqc - kc * dkc, rowi)
            dbs_ref[rows, :] = dbs
            dkc_ref[rows, :] = dkc
            tot_ref[pl.ds(s * SLAB, SLAB), :] = _c3(dbs)[:, 0:1, :].reshape(SLAB, BLK)
        dnw_ref[...] = jnp.broadcast_to(dnw, (8, BLK))

        rn = lax.broadcasted_iota(jnp.int32, (N_CHUNK, N_CHUNK), 0)
        cn = lax.broadcasted_iota(jnp.int32, (N_CHUNK, N_CHUNK), 1)
        tot_ref[...] = jnp.dot((cn > rn).astype(F32), tot_ref[...], preferred_element_type=F32, precision=HI)
        dlb = jnp.zeros((1, BLK), F32)
        for s in range(N_SLAB):
            rows = _slab_rows(s)
            sf = _sigmoid(f_ref[rows, :])
            fg = lb + (1.0 - lb) * sf
            later = tot_ref[pl.ds(s * SLAB, SLAB), :]
            dlg = _c2(_c3(dbs_ref[rows, :]) + later[:, None, :])
            dfg = dlg / fg - dkc_ref[rows, :]
            df_ref[rows, :] = (dfg * (1.0 - lb) * sf * (1.0 - sf)).astype(BF16)
            dlb = dlb + jnp.sum(dfg * (1.0 - sf), 0, keepdims=True)
        dlg_ref[...] = _dlogits_of(p, dlb, layer)
        _start_col_blocks(stage, stage_sem, dproj_ref, [c0 + pl.program_id(0) for c0 in (QA0, FA0, IA0, GA0)])

        @pl.when(pl.program_id(0) == A_HEADS - 1)
        def _():
            _wait_col_blocks(stage, stage_sem, dproj_ref, 4)

    def colblk(c0):
        return pl.BlockSpec((SEQ, BLK), lambda h: (0, c0 + h))

    return pl.pallas_call(
        body, name=name, grid=(A_HEADS,),
        in_specs=[colblk(QA0), colblk(FA0), colblk(IA0), colblk(GA0), colblk(0), colblk(0),
                  pl.BlockSpec((DEPTH, BLK), lambda h: (0, h)), pl.BlockSpec((1, BLK), lambda h: (0, 0))]
        + [pl.BlockSpec(memory_space=pl.ANY)] * len(extra),
        out_specs=[pl.BlockSpec(memory_space=pl.ANY),
                   pl.BlockSpec((8, BLK), lambda h: (h, 0)), pl.BlockSpec((DEPTH, BLK), lambda h: (0, h))],
        out_shape=[jax.ShapeDtypeStruct((SEQ, IN_WIDTH), BF16), jax.ShapeDtypeStruct((A_HEADS * 8, BLK), F32),
                   jax.ShapeDtypeStruct((DEPTH, A_HEADS * BLK), F32)],
        scratch_shapes=[pltpu.VMEM((N_CHUNK, BLK), F32), pltpu.VMEM((SLAB, BLK, BLK), F32),
                        pltpu.VMEM((N_CHUNK, BLK, BLK), F32), pltpu.VMEM((SLAB, BLK, BLK), F32),
                        pltpu.VMEM((SEQ, BLK), F32), pltpu.VMEM((SEQ, BLK), F32), pltpu.VMEM((N_CHUNK, BLK), F32),
                        pltpu.VMEM((4, SEQ, BLK), BF16), pltpu.SemaphoreType.DMA((4,))],
        compiler_params=_cp(("arbitrary",)))(proj, proj, proj, proj, raw, dmix, lb_logits, nw.reshape(1, -1), *extra)


SCALE = HEAD_DIM ** -0.5


def _rope_tables():
    half = ROPE_DIM // 2
    inv = ROPE_THETA ** (-jnp.arange(0, ROPE_DIM, 2, dtype=F32) / ROPE_DIM)
    ang = jnp.arange(SEQ, dtype=F32)[:, None] * inv[None, :]
    cos, sin = jnp.cos(ang), jnp.sin(ang)
    pad = jnp.zeros((SEQ, HEAD_DIM - ROPE_DIM), F32)
    zero = jnp.zeros((SEQ, half), F32)
    c = jnp.concatenate([cos, cos, pad + 1.0], 1)
    s_lo = jnp.concatenate([zero, sin, pad], 1)
    s_hi = jnp.concatenate([-sin, zero, pad], 1)
    return c, s_lo, s_hi


def _rope(x, c, s_lo, s_hi):
    half = ROPE_DIM // 2
    return x * c + pltpu.roll(x, half, axis=1) * s_lo + pltpu.roll(x, HEAD_DIM - half, axis=1) * s_hi


def _unrope(dy, c, s_lo, s_hi):
    half = ROPE_DIM // 2
    return dy * c + pltpu.roll(dy * s_lo, HEAD_DIM - half, axis=1) + pltpu.roll(dy * s_hi, half, axis=1)


N_BLK = SEQ // BLK


def _block_rows(dil):
    nb = N_BLK // dil
    return [pl.ds(r + n * BLK * dil, BLK, stride=dil) for r in range(dil) for n in range(nb)]


def _to_blocks(ref, dil):
    if dil == 1:
        return ref[...].reshape(N_BLK, BLK, BLK)
    return jnp.stack([ref[rows, :] for rows in _block_rows(dil)], 0)


def _from_blocks(ref, val, dil, add=False):
    if dil == 1:
        flat = val.reshape(SEQ, BLK)
        ref[...] = ref[...] + flat if add else flat
        return
    for b, rows in enumerate(_block_rows(dil)):
        ref[rows, :] = ref[rows, :] + val[b] if add else val[b]


def _prev_block(x):
    return jnp.concatenate([x[:1], x[:-1]], axis=0)


def _to_next_block(x):
    return jnp.concatenate([x[1:], jnp.zeros_like(x[:1])], axis=0)


def _band_masks(max_lag, dil):
    r = lax.broadcasted_iota(jnp.int32, (N_BLK, BLK, BLK), 1)
    c = lax.broadcasted_iota(jnp.int32, (N_BLK, BLK, BLK), 2)
    b = lax.broadcasted_iota(jnp.int32, (N_BLK, BLK, BLK), 0)
    has_prev = (b % (N_BLK // dil)) != 0
    return r >= c, has_prev & (BLK + r - c <= max_lag)


def _bdot(eq, a, b):
    return jnp.einsum(eq, a, b, preferred_element_type=F32)


def _attn_fwd(name, proj, tables, sink_b, mixed, mixed_bf, *, n_heads, rep, q0, k0, v0, m0, patterns):
    n_pat = len(patterns)
    has_sink = sink_b is not None

    def body(*refs):
        o_ref, ob_ref, l_ref, qr, kr, op, lse_ref = refs[-7:]
        q_ref, k_ref, v_ref, c_ref, sl_ref, sh_ref = refs[:6]
        if has_sink:
            sk = refs[6][0:1, 0:1]
        c, s_lo, s_hi = c_ref[...], sl_ref[...], sh_ref[...]
        qr[...] = _rope(q_ref[...], c, s_lo, s_hi)
        kr[...] = _rope(k_ref[...], c, s_lo, s_hi)
        for p, (max_lag, dil) in enumerate(patterns):
            qa = _to_blocks(qr, dil).astype(BF16)
            ka = _to_blocks(kr, dil).astype(BF16)
            va = _to_blocks(v_ref, dil).astype(BF16)
            own, before = _band_masks(max_lag, dil)
            s1 = jnp.where(own, _bdot('nqd,nkd->nqk', qa, ka) * SCALE, -jnp.inf)
            m = jnp.max(s1, -1, keepdims=True)
            with_prev = dil < N_BLK
            if with_prev:
                kp, vp = _prev_block(ka), _prev_block(va)
                s0 = jnp.where(before, _bdot('nqd,nkd->nqk', qa, kp) * SCALE, -jnp.inf)
                m = jnp.maximum(m, jnp.max(s0, -1, keepdims=True))
            if has_sink:
                m = jnp.maximum(m, sk)
            e1 = jnp.exp(s1 - m)
            den = jnp.sum(e1, -1, keepdims=True)
            o = _bdot('nqk,nkd->nqd', e1.astype(BF16), va)
            if with_prev:
                e0 = jnp.exp(s0 - m)
                den = den + jnp.sum(e0, -1, keepdims=True)
                o = o + _bdot('nqk,nkd->nqd', e0.astype(BF16), vp)
            if has_sink:
                den = den + jnp.exp(sk - m)
            _from_blocks(op.at[p], o / den, dil)
            _from_blocks(lse_ref.at[p], jnp.broadcast_to(m + jnp.log(den), (N_BLK, BLK, BLK)), dil)
        if n_pat == 1:
            acc = op[0]
            l_ref[...] = lse_ref[0]
        else:
            ls = [lse_ref[p] for p in range(n_pat)]
            m = functools.reduce(jnp.maximum, ls)
            es = [jnp.exp(l - m) for l in ls]
            tot = functools.reduce(jnp.add, es)
            acc = None
            for p in range(n_pat):
                t = (es[p] / tot) * op[p]
                acc = t if acc is None else acc + t
            l_ref[...] = m + jnp.log(tot)
        o_ref[...] = acc
        ob_ref[...] = acc.astype(BF16)

    def colblk(fn):
        return pl.BlockSpec((SEQ, BLK), fn)

    tab = pl.BlockSpec((SEQ, BLK), lambda h: (0, 0))
    in_specs = [colblk(lambda h: (0, q0 + h)), colblk(lambda h: (0, k0 + h // rep)), colblk(lambda h: (0, v0 + h // rep)),
                tab, tab, tab]
    args = [proj, proj, proj, *tables]
    if has_sink:
        in_specs.append(pl.BlockSpec((None, 8, BLK), lambda h: (h, 0, 0)))
        args.append(sink_b)
    n_in = len(args)
    in_specs += [pl.BlockSpec(memory_space=pl.ANY)] * 2
    args += [mixed, mixed_bf]
    pat = pltpu.VMEM((n_pat, SEQ, BLK), F32)
    return pl.pallas_call(
        body, name=name, grid=(n_heads,), in_specs=in_specs,
        out_specs=[colblk(lambda h: (0, m0 + h)), colblk(lambda h: (0, m0 + h)),
                   pl.BlockSpec((None, SEQ, BLK), lambda h: (h, 0, 0))],
        out_shape=[jax.ShapeDtypeStruct(mixed.shape, F32), jax.ShapeDtypeStruct(mixed.shape, BF16),
                   jax.ShapeDtypeStruct((n_heads, SEQ, BLK), F32)],
        input_output_aliases={n_in: 0, n_in + 1: 1},
        scratch_shapes=[pltpu.VMEM((SEQ, BLK), F32), pltpu.VMEM((SEQ, BLK), F32), pat, pat],
        compiler_params=_cp(("parallel",)))(*args)


def _attn_bwd(name, proj, mixed, dmix, lse, tables, sink_b, dproj, *, n_kv, rep, q0, k0, v0, m0, patterns):
    n_heads = n_kv * rep
    has_sink = sink_b is not None

    def body(*refs):
        q_ref, k_ref, v_ref, o_ref, do_ref, lse_ref, c_ref, sl_ref, sh_ref = refs[:9]
        sink_ref = refs[9] if has_sink else None
        dproj_ref, dsk_ref, qr, kr, dqa, dka, dva, dd, stage, stage_sem = refs[-10:]
        g, j = pl.program_id(0), pl.program_id(1)
        c, s_lo, s_hi = c_ref[...], sl_ref[...], sh_ref[...]
        qr[...] = _rope(q_ref[...], c, s_lo, s_hi)
        kr[...] = _rope(k_ref[...], c, s_lo, s_hi)
        dcol = jnp.sum(do_ref[...] * o_ref[...], -1, keepdims=True)
        dd[...] = jnp.broadcast_to(dcol, (SEQ, BLK))

        @pl.when(j == 0)
        def _():
            dka[...] = jnp.zeros((SEQ, BLK), F32)
            dva[...] = jnp.zeros((SEQ, BLK), F32)

        for p, (max_lag, dil) in enumerate(patterns):
            qa = _to_blocks(qr, dil).astype(BF16)
            ka = _to_blocks(kr, dil).astype(BF16)
            va = _to_blocks(v_ref, dil).astype(BF16)
            doa = _to_blocks(do_ref, dil).astype(BF16)
            lcol = _to_blocks(lse_ref, dil)[:, :, 0:1]
            dcb = _to_blocks(dd, dil)[:, :, 0:1]
            own, before = _band_masks(max_lag, dil)

            def probs_and_ds(kk, vv, valid):
                s = _bdot('nqd,nkd->nqk', qa, kk) * SCALE
                a = jnp.where(valid, jnp.exp(s - lcol), 0.0)
                ds = a * (_bdot('nqd,nkd->nqk', doa, vv) - dcb) * SCALE
                return a.astype(BF16), ds.astype(BF16)

            a1, ds1 = probs_and_ds(ka, va, own)
            dq = _bdot('nqk,nkd->nqd', ds1, ka)
            dk = _bdot('nqk,nqd->nkd', ds1, qa)
            dv = _bdot('nqk,nqd->nkd', a1, doa)
            if dil < N_BLK:
                kp, vp = _prev_block(ka), _prev_block(va)
                a0, ds0 = probs_and_ds(kp, vp, before)
                dq = dq + _bdot('nqk,nkd->nqd', ds0, kp)
                dk = dk + _to_next_block(_bdot('nqk,nqd->nkd', ds0, qa))
                dv = dv + _to_next_block(_bdot('nqk,nqd->nkd', a0, doa))
            _from_blocks(dqa, dq, dil, add=p > 0)
            _from_blocks(dka, dk, dil, add=True)
            _from_blocks(dva, dv, dil, add=True)

        if has_sink:
            sk = sink_ref[0:1, 0:1]
            ps = jnp.exp(sk - lse_ref[...][:, 0:1])
            dsk_ref[...] = jnp.full((8, BLK), -jnp.sum(ps * dcol), F32)
        else:
            dsk_ref[...] = jnp.zeros((8, BLK), F32)
        @pl.when(g * rep + j > 0)
        def _():
            _wait_col_blocks(stage, stage_sem, dproj_ref, 1)

        stage[0] = _unrope(dqa[...], c, s_lo, s_hi).astype(BF16)
        _start_col_blocks(stage, stage_sem, dproj_ref, [q0 + g * rep + j])

        @pl.when(j == rep - 1)
        def _():
            @pl.when(g > 0)
            def _():
                _wait_col_blocks(stage, stage_sem, dproj_ref, 2, first=1)

            stage[1] = _unrope(dka[...], c, s_lo, s_hi).astype(BF16)
            stage[2] = dva[...].astype(BF16)
            _start_col_blocks(stage, stage_sem, dproj_ref, [k0 + g, v0 + g], first=1)

        @pl.when((g == n_kv - 1) & (j == rep - 1))
        def _():
            _wait_col_blocks(stage, stage_sem, dproj_ref, 3)

    def colblk(fn):
        return pl.BlockSpec((SEQ, BLK), fn)

    tab = pl.BlockSpec((SEQ, BLK), lambda g, j: (0, 0))
    in_specs = [colblk(lambda g, j: (0, q0 + g * rep + j)), colblk(lambda g, j: (0, k0 + g)), colblk(lambda g, j: (0, v0 + g)),
                colblk(lambda g, j: (0, m0 + g * rep + j)), colblk(lambda g, j: (0, m0 + g * rep + j)),
                pl.BlockSpec((None, SEQ, BLK), lambda g, j: (g * rep + j, 0, 0)), tab, tab, tab]
    args = [proj, proj, proj, mixed, dmix, lse, *tables]
    if has_sink:
        in_specs.append(pl.BlockSpec((None, 8, BLK), lambda g, j: (g * rep + j, 0, 0)))
        args.append(sink_b)
    n_in = len(args)
    in_specs.append(pl.BlockSpec(memory_space=pl.ANY))
    args.append(dproj)
    acc = pltpu.VMEM((SEQ, BLK), F32)
    return pl.pallas_call(
        body, name=name, grid=(n_kv, rep), in_specs=in_specs,
        out_specs=[pl.BlockSpec(memory_space=pl.ANY), pl.BlockSpec((None, 8, BLK), lambda g, j: (g * rep + j, 0, 0))],
        out_shape=[jax.ShapeDtypeStruct(dproj.shape, BF16), jax.ShapeDtypeStruct((n_heads, 8, BLK), F32)],
        input_output_aliases={n_in: 0},
        scratch_shapes=[acc, acc, acc, acc, acc, acc, pltpu.VMEM((3, SEQ, BLK), BF16), pltpu.SemaphoreType.DMA((3,))],
        compiler_params=_cp(("arbitrary", "arbitrary")))(*args)


B_PATTERNS = tuple((w // d, d) for w, d in DILATED_PATTERNS)
C_PATTERNS = ((C_WINDOW - 1, 1),)


ANY = pl.BlockSpec(memory_space=pl.ANY)
CHIP_MASKS = ((1, 0), (0, 1), (1, 1))


def _coords():
    return lax.axis_index("x"), lax.axis_index("y"), lax.axis_index("c")


def _flip(v, m):
    return 1 - v if m else v


def _into_slot(name, w, layer, k_idx, dtype, run_after=None):
    _, rows, cols = w.shape
    tr = rows // 8 if rows % 64 == 0 else rows

    def body(k_ref, w_ref, *rest):
        rest[-1][...] = w_ref[...].astype(dtype)

    in_specs = [pl.BlockSpec((None, tr, cols), lambda i, k: (layer, i, 0))]
    args = [k_idx, w]
    if run_after is not None:
        in_specs.append(pl.BlockSpec(memory_space=pl.ANY))
        args.append(run_after)
    return pl.pallas_call(
        body, name=name,
        grid_spec=pltpu.PrefetchScalarGridSpec(
            num_scalar_prefetch=1, grid=(rows // tr,), in_specs=in_specs,
            out_specs=pl.BlockSpec((None, tr, cols), lambda i, k: (k[0], i, 0))),
        out_shape=jax.ShapeDtypeStruct((N_CHIPS, rows, cols), dtype),
        compiler_params=_cp(("parallel",)))(*args)


HBM_SPEC = pl.BlockSpec(memory_space=pltpu.HBM)
SEM_SPEC = pl.BlockSpec(memory_space=pltpu.SEMAPHORE)
TOKEN_SPEC = pl.BlockSpec(memory_space=pltpu.VMEM)
TOKEN_SHAPE = jax.ShapeDtypeStruct((8, BLK), F32)
DATAFLOW = pltpu.SideEffectType.DATAFLOW_SIDE_EFFECTING


def _hbm(a):
    return pltpu.with_memory_space_constraint(a, pltpu.HBM)


def _hbm_like(bufs):
    return [pltpu.HBM(b.shape, b.dtype) for b in bufs]


def _gather_start(name, stages):
    flat = [b for st in stages for b in st]
    n, ns = len(flat), len(stages)

    def body(*refs):
        ins = refs[:n]
        sems = refs[n:n + 2 * ns]
        token = refs[-1]
        x, y, c = _coords()
        k_me = 2 * x + y
        a = 0
        for s, st in enumerate(stages):
            for i in range(len(st)):
                mine = ins[a].at[k_me, c]
                for m, (mx, my) in enumerate(CHIP_MASKS):
                    pltpu.make_async_remote_copy(src_ref=mine, dst_ref=mine, send_sem=sems[2 * s].at[i * 3 + m],
                                                 recv_sem=sems[2 * s + 1].at[i * 3 + m],
                                                 device_id=(_flip(x, mx), _flip(y, my), c), device_id_type=MESH).start()
                a += 1
        token[...] = jnp.zeros_like(token)

    sem_shapes = []
    for st in stages:
        sem_shapes += [pltpu.SemaphoreType.DMA((3 * len(st),))] * 2
    out = pl.pallas_call(
        body, name=name, in_specs=[HBM_SPEC] * n,
        out_specs=tuple([SEM_SPEC] * (2 * ns) + [HBM_SPEC] * n + [TOKEN_SPEC]),
        out_shape=tuple(sem_shapes + _hbm_like(flat) + [TOKEN_SHAPE]),
        input_output_aliases={i: 2 * ns + i for i in range(n)},
        compiler_params=pltpu.CompilerParams(has_side_effects=DATAFLOW),
    )(*[_hbm(b) for b in flat])
    sems, bufs, token = out[:2 * ns], out[2 * ns:2 * ns + n], out[-1]
    res, a = [], 0
    for s, st in enumerate(stages):
        res.append((sems[2 * s], sems[2 * s + 1], list(bufs[a:a + len(st)])))
        a += len(st)
    return res, token


def _gather_forward(name, stage, after):
    ssem_in, rsem_in, bufs = stage
    n = len(bufs)

    def body(*refs):
        ins = refs[:n]
        s_in, r_in, _ = refs[n:n + 3]
        s_out, r_out = refs[n + 3:n + 5]
        token = refs[-1]
        x, y, c = _coords()
        for i in range(n):
            for m, (mx, my) in enumerate(CHIP_MASKS):
                kp = 2 * _flip(x, mx) + _flip(y, my)
                blk = ins[i].at[kp, c]
                got = pltpu.make_async_remote_copy(src_ref=blk, dst_ref=blk, send_sem=s_in.at[i * 3 + m],
                                                   recv_sem=r_in.at[i * 3 + m], device_id=(x, y, 1 - c), device_id_type=MESH)
                got.wait_send()
                got.wait_recv()
                pltpu.make_async_remote_copy(src_ref=blk, dst_ref=blk, send_sem=s_out.at[i * 3 + m],
                                             recv_sem=r_out.at[i * 3 + m], device_id=(x, y, 1 - c), device_id_type=MESH).start()
        token[...] = jnp.zeros_like(token)

    sem = pltpu.SemaphoreType.DMA((3 * n,))
    out = pl.pallas_call(
        body, name=name, in_specs=[HBM_SPEC] * n + [SEM_SPEC, SEM_SPEC, ANY],
        out_specs=tuple([SEM_SPEC, SEM_SPEC] + [HBM_SPEC] * n + [TOKEN_SPEC]),
        out_shape=tuple([sem, sem] + _hbm_like(bufs) + [TOKEN_SHAPE]),
        input_output_aliases={i: 2 + i for i in range(n)},
        compiler_params=pltpu.CompilerParams(has_side_effects=DATAFLOW),
    )(*bufs, ssem_in, rsem_in, after)
    return (out[0], out[1], list(out[2:2 + n])), out[-1]


def _gather_wait(name, stage, after):
    ssem, rsem, bufs = stage
    n = len(bufs)

    def body(*refs):
        ins = refs[:n]
        s_in, r_in, _ = refs[n:n + 3]
        x, y, c = _coords()
        for i in range(n):
            for m, (mx, my) in enumerate(CHIP_MASKS):
                kp = 2 * _flip(x, mx) + _flip(y, my)
                sent, got = ins[i].at[kp, c], ins[i].at[kp, 1 - c]
                cp = pltpu.make_async_remote_copy(src_ref=sent, dst_ref=got, send_sem=s_in.at[i * 3 + m],
                                                  recv_sem=r_in.at[i * 3 + m], device_id=(x, y, 1 - c), device_id_type=MESH)
                cp.wait_send()
                cp.wait_recv()

    out = pl.pallas_call(
        body, name=name, in_specs=[HBM_SPEC] * n + [SEM_SPEC, SEM_SPEC, ANY],
        out_specs=tuple([HBM_SPEC] * n), out_shape=tuple(_hbm_like(bufs)),
        input_output_aliases={i: i for i in range(n)},
        compiler_params=pltpu.CompilerParams(has_side_effects=DATAFLOW),
    )(*bufs, ssem, rsem, after)
    return list(out)


def _swap_start(name, grads):
    n = len(grads)

    def body(*refs):
        ins, lands = refs[:n], refs[n:2 * n]
        ssem, rsem = refs[2 * n:2 * n + 2]
        x, y, c = _coords()
        for a in range(n):
            for j in range(N_CHIPS):
                pltpu.make_async_remote_copy(src_ref=ins[a].at[j, 1 - c], dst_ref=lands[a].at[j],
                                             send_sem=ssem.at[a * N_CHIPS + j], recv_sem=rsem.at[a * N_CHIPS + j],
                                             device_id=(x, y, 1 - c), device_id_type=MESH).start()

    sem = pltpu.SemaphoreType.DMA((N_CHIPS * n,))
    land_shapes = [pltpu.HBM((N_CHIPS,) + g.shape[2:], g.dtype) for g in grads]
    out = pl.pallas_call(
        body, name=name, in_specs=[HBM_SPEC] * (2 * n),
        out_specs=tuple([SEM_SPEC, SEM_SPEC] + [HBM_SPEC] * (2 * n)),
        out_shape=tuple([sem, sem] + _hbm_like(grads) + land_shapes),
        input_output_aliases={i: 2 + i for i in range(2 * n)},
        compiler_params=pltpu.CompilerParams(has_side_effects=DATAFLOW),
    )(*[_hbm(g) for g in grads], *[_hbm(lax.empty((N_CHIPS,) + g.shape[2:], g.dtype)) for g in grads])
    return out[0], out[1], list(out[2:2 + n]), list(out[2 + n:])


def _swap_wait(name, started, after):
    ssem, rsem, grads, lands = started
    n = len(grads)
    after = after if isinstance(after, tuple) else (after,)

    def body(*refs):
        ins, lnd = refs[:n], refs[n:2 * n]
        s_in, r_in = refs[2 * n:2 * n + 2]
        x, y, c = _coords()
        for a in range(n):
            for j in range(N_CHIPS):
                cp = pltpu.make_async_remote_copy(src_ref=ins[a].at[j, 1 - c], dst_ref=lnd[a].at[j],
                                                  send_sem=s_in.at[a * N_CHIPS + j], recv_sem=r_in.at[a * N_CHIPS + j],
                                                  device_id=(x, y, 1 - c), device_id_type=MESH)
                cp.wait_send()
                cp.wait_recv()

    out = pl.pallas_call(
        body, name=name, in_specs=[HBM_SPEC] * (2 * n) + [SEM_SPEC, SEM_SPEC] + [ANY] * len(after),
        out_specs=tuple([HBM_SPEC] * (2 * n)), out_shape=tuple(_hbm_like(grads) + _hbm_like(lands)),
        input_output_aliases={i: i for i in range(2 * n)},
        compiler_params=pltpu.CompilerParams(has_side_effects=DATAFLOW),
    )(*grads, *lands, ssem, rsem, *after)
    return list(out[:n]), list(out[n:])


def _scatter_start(name, parts):
    n = len(parts)

    def body(*refs):
        ins, lands = refs[:n], refs[n:2 * n]
        ssem, rsem = refs[2 * n:2 * n + 2]
        x, y, c = _coords()
        k_me = 2 * x + y
        for a in range(n):
            for m, (mx, my) in enumerate(CHIP_MASKS):
                px, py = _flip(x, mx), _flip(y, my)
                pltpu.make_async_remote_copy(src_ref=ins[a].at[2 * px + py], dst_ref=lands[a].at[k_me],
                                             send_sem=ssem.at[a * 3 + m], recv_sem=rsem.at[a * 3 + m],
                                             device_id=(px, py, c), device_id_type=MESH).start()

    sem = pltpu.SemaphoreType.DMA((3 * n,))
    out = pl.pallas_call(
        body, name=name, in_specs=[HBM_SPEC] * (2 * n),
        out_specs=tuple([SEM_SPEC, SEM_SPEC] + [HBM_SPEC] * (2 * n)),
        out_shape=tuple([sem, sem] + _hbm_like(parts) + _hbm_like(parts)),
        input_output_aliases={i: 2 + i for i in range(2 * n)},
        compiler_params=pltpu.CompilerParams(has_side_effects=DATAFLOW),
    )(*[_hbm(p) for p in parts], *[_hbm(lax.empty(p.shape, p.dtype)) for p in parts])
    return out[0], out[1], list(out[2:2 + n]), list(out[2 + n:])


def _scatter_wait(name, started, after):
    ssem, rsem, parts, lands = started
    n = len(parts)

    def body(*refs):
        ins, lnd = refs[:n], refs[n:2 * n]
        s_in, r_in, _ = refs[2 * n:2 * n + 3]
        x, y, c = _coords()
        k_me = 2 * x + y
        for a in range(n):
            for m, (mx, my) in enumerate(CHIP_MASKS):
                px, py = _flip(x, mx), _flip(y, my)
                cp = pltpu.make_async_remote_copy(src_ref=ins[a].at[2 * px + py], dst_ref=lnd[a].at[k_me],
                                                  send_sem=s_in.at[a * 3 + m], recv_sem=r_in.at[a * 3 + m],
                                                  device_id=(px, py, c), device_id_type=MESH)
                cp.wait_send()
                cp.wait_recv()

    out = pl.pallas_call(
        body, name=name, in_specs=[HBM_SPEC] * (2 * n) + [SEM_SPEC, SEM_SPEC, ANY],
        out_specs=tuple([HBM_SPEC] * (2 * n)), out_shape=tuple(_hbm_like(parts) + _hbm_like(lands)),
        input_output_aliases={i: i for i in range(2 * n)},
        compiler_params=pltpu.CompilerParams(has_side_effects=DATAFLOW),
    )(*parts, *lands, ssem, rsem, after)
    return list(out[:n]), list(out[n:])


def _pair_gather_start(name, bufs):
    n = len(bufs)

    def body(*refs):
        ins = refs[:n]
        ssem, rsem = refs[n:n + 2]
        x, y, c = _coords()
        for a in range(n):
            mine = ins[a].at[c]
            pltpu.make_async_remote_copy(src_ref=mine, dst_ref=mine, send_sem=ssem.at[a], recv_sem=rsem.at[a],
                                         device_id=(x, y, 1 - c), device_id_type=MESH).start()

    sem = pltpu.SemaphoreType.DMA((n,))
    out = pl.pallas_call(
        body, name=name, in_specs=[HBM_SPEC] * n, out_specs=tuple([SEM_SPEC, SEM_SPEC] + [HBM_SPEC] * n),
        out_shape=tuple([sem, sem] + _hbm_like(bufs)),
        input_output_aliases={i: 2 + i for i in range(n)},
        compiler_params=pltpu.CompilerParams(has_side_effects=DATAFLOW),
    )(*[_hbm(b) for b in bufs])
    return out[0], out[1], list(out[2:])


def _pair_gather_wait(name, started, after):
    ssem, rsem, bufs = started
    n = len(bufs)

    def body(*refs):
        ins = refs[:n]
        s_in, r_in, _ = refs[n:n + 3]
        x, y, c = _coords()
        for a in range(n):
            cp = pltpu.make_async_remote_copy(src_ref=ins[a].at[c], dst_ref=ins[a].at[1 - c], send_sem=s_in.at[a],
                                              recv_sem=r_in.at[a], device_id=(x, y, 1 - c), device_id_type=MESH)
            cp.wait_send()
            cp.wait_recv()

    out = pl.pallas_call(
        body, name=name, in_specs=[HBM_SPEC] * n + [SEM_SPEC, SEM_SPEC, ANY],
        out_specs=tuple([HBM_SPEC] * n), out_shape=tuple(_hbm_like(bufs)),
        input_output_aliases={i: i for i in range(n)},
        compiler_params=pltpu.CompilerParams(has_side_effects=DATAFLOW),
    )(*bufs, ssem, rsem, after)
    return list(out)


DEV_MASKS = tuple((mx, my, mc) for mx in (0, 1) for my in (0, 1) for mc in (0, 1) if (mx, my, mc) != (0, 0, 0))


def _gather_small(buf, run_after):
    def body(in_ref, _, out_ref, ssem, rsem, lsem):
        x, y, c = _coords()
        me = 4 * x + 2 * y + c
        cps = [pltpu.make_async_copy(in_ref, out_ref.at[me], lsem)]
        cps[0].start()
        for t, (mx, my, mc) in enumerate(DEV_MASKS):
            cp = pltpu.make_async_remote_copy(src_ref=in_ref, dst_ref=out_ref.at[me], send_sem=ssem.at[t],
                                              recv_sem=rsem.at[t], device_id=(_flip(x, mx), _flip(y, my), _flip(c, mc)),
                                              device_id_type=MESH)
            cp.start()
            cps.append(cp)
        for cp in cps:
            cp.wait()

    return pl.pallas_call(
        body, name="gather_small", in_specs=[ANY, ANY], out_specs=ANY,
        out_shape=jax.ShapeDtypeStruct((N_DEV,) + buf.shape, buf.dtype),
        scratch_shapes=[pltpu.SemaphoreType.DMA((N_DEV - 1,)), pltpu.SemaphoreType.DMA((N_DEV - 1,)),
                        pltpu.SemaphoreType.DMA(())],
        compiler_params=pltpu.CompilerParams(has_side_effects=True),
    )(buf, run_after)


def _row_tile(rows):
    return rows // 2 if rows % 16 == 0 else rows


def _pair_add(name, grads, gots, c_idx):
    n = len(grads)
    tiles = [(g.shape[2] // 2, g.shape[3]) for g in grads]

    def body(c_ref, *refs):
        for a in range(n):
            refs[2 * n + a][...] = (refs[2 * a][...].astype(F32) + refs[2 * a + 1][...].astype(F32)).astype(BF16)

    in_specs, out_specs, args = [], [], []
    for g, r, (tr, cols) in zip(grads, gots, tiles):
        in_specs += [pl.BlockSpec((None, None, tr, cols), lambda j, i, c: (j, c[0], i, 0)),
                     pl.BlockSpec((None, tr, cols), lambda j, i, c: (j, i, 0))]
        out_specs.append(pl.BlockSpec((None, tr, cols), lambda j, i, c: (j, i, 0)))
        args += [g, r]
    return pl.pallas_call(
        body, name=name,
        grid_spec=pltpu.PrefetchScalarGridSpec(num_scalar_prefetch=1, grid=(N_CHIPS, 2), in_specs=in_specs,
                                               out_specs=out_specs),
        out_shape=[jax.ShapeDtypeStruct((N_CHIPS, g.shape[2], g.shape[3]), BF16) for g in grads],
        compiler_params=_cp(("parallel", "parallel")))(c_idx, *args)


def _chip_add(name, part, got, kc_idx):
    _, r2, cols = got.shape
    tr = _row_tile(r2)

    def body(k_ref, p_ref, g1_ref, g2_ref, g3_ref, o_ref):
        acc = p_ref[...].astype(F32)
        for g_ref in (g1_ref, g2_ref, g3_ref):
            acc = acc + g_ref[...].astype(F32)
        o_ref[...] = acc

    def slot(d):
        return pl.BlockSpec((None, tr, cols), lambda i, k: ((k[0] + d) % N_CHIPS, i, 0))

    return pl.pallas_call(
        body, name=name,
        grid_spec=pltpu.PrefetchScalarGridSpec(
            num_scalar_prefetch=1, grid=(r2 // tr,),
            in_specs=[slot(0), slot(1), slot(2), slot(3)],
            out_specs=pl.BlockSpec((None, tr, cols), lambda i, k: (k[1], i, 0))),
        out_shape=jax.ShapeDtypeStruct((2, r2, cols), F32),
        compiler_params=_cp(("parallel",)))(kc_idx, part, got, got, got)


def _adam_math(w, g, m, v):
    m2 = ADAM_B1 * m + (1.0 - ADAM_B1) * g
    v2 = ADAM_B2 * v + (1.0 - ADAM_B2) * (g * g)
    m_hat = m2 / (1.0 - ADAM_B1 ** ADAM_STEP)
    v_hat = v2 / (1.0 - ADAM_B2 ** ADAM_STEP)
    delta = -ADAM_LR * (m_hat / (jnp.sqrt(v_hat) + ADAM_EPS) + ADAM_WD * w)
    return delta, m2, v2


def _adamw_matrix(name, w, g_layers, m, v):
    _, rows, cols = w.shape
    tr = rows // 8

    def body(w_ref, g0_ref, g1_ref, m_ref, v_ref, go_ref, d_ref, mo_ref, vo_ref):
        g = jnp.where(pl.program_id(0) == 0, g0_ref[...], g1_ref[...])
        go_ref[...] = g
        d_ref[...], mo_ref[...], vo_ref[...] = _adam_math(w_ref[...], g, m_ref[...], v_ref[...])

    lay = pl.BlockSpec((None, tr, cols), lambda l, i: (l, i, 0))
    flat = pl.BlockSpec((tr, cols), lambda l, i: (i, 0))
    shp = jax.ShapeDtypeStruct(w.shape, F32)
    return pl.pallas_call(body, name=name, grid=(DEPTH, rows // tr), in_specs=[lay, flat, flat, lay, lay],
                          out_specs=[lay, lay, lay, lay], out_shape=[shp, shp, shp, shp],
                          compiler_params=_cp(("parallel", "parallel")))(w, g_layers[0], g_layers[1], m, v)


def _sum_small(gathered):
    def body(g_ref, o_ref):
        acc = g_ref[0]
        for d in range(1, N_DEV):
            acc = acc + g_ref[d]
        o_ref[...] = acc

    return pl.pallas_call(body, name="sum_small", out_shape=jax.ShapeDtypeStruct(gathered.shape[1:], F32),
                          compiler_params=_cp())(gathered)


def _adamw_small(w, g, m, v):
    def body(w_ref, g_ref, m_ref, v_ref, d_ref, mo_ref, vo_ref):
        d_ref[...], mo_ref[...], vo_ref[...] = _adam_math(w_ref[...], g_ref[...], m_ref[...], v_ref[...])

    shp = jax.ShapeDtypeStruct(w.shape, F32)
    return pl.pallas_call(body, name="adamw_small", out_shape=[shp, shp, shp], compiler_params=_cp())(w, g, m, v)


def _pack(arrays, rows):
    flat = jnp.concatenate([a.reshape(-1) for a in arrays])
    return jnp.pad(flat, (0, rows * BLK - flat.shape[0])).reshape(rows, BLK)


def _unpack(buf, shapes):
    flat = buf.reshape(-1)
    out, pos = [], 0
    for s in shapes:
        n = math.prod(s)
        out.append(flat[pos:pos + n].reshape(s))
        pos += n
    return out


def _rows_for(shapes):
    n = sum(math.prod(s) for s in shapes)
    return -(-n // (8 * BLK)) * 8


def _rs_swap(tag, grads):
    return _swap_start(f"rs_swap_start{tag}", [g.reshape(N_CHIPS, 2, g.shape[1] // 2, g.shape[2]) for g in grads])


def _rs_scatter(tag, swapping, after, c_idx):
    split, got = _swap_wait(f"rs_swap_wait{tag}", swapping, after)
    parts = _pair_add(f"rs_pair_add{tag}", split, got, c_idx)
    return _scatter_start(f"rs_scatter_start{tag}", parts)


def _rs_reduce(tag, started, after, kc_idx):
    parts, lands = _scatter_wait(f"rs_scatter_wait{tag}", started, after)
    halves = [_chip_add(f"rs_chip_add{tag}_{i}", p, r, kc_idx) for i, (p, r) in enumerate(zip(parts, lands))]
    return _pair_gather_start(f"rs_pair_gather_start{tag}", halves)


def _rs_finish(tag, gathering, after):
    full = _pair_gather_wait(f"rs_pair_gather_wait{tag}", gathering, after)
    return [f.reshape(2 * f.shape[1], f.shape[2]) for f in full]


def kernel(x, w_in, lb_logits, a_norm_w, c_sinks, w_out, ln1_g, ln1_b, w_gate, w_up, conv_w, conv_b, w_down, ln2_g, ln2_b, loss_target, m_w_in, m_lb_logits, m_a_norm_w, m_c_sinks, m_w_out, m_ln1_g, m_ln1_b, m_w_gate, m_w_up, m_conv_w, m_conv_b, m_w_down, m_ln2_g, m_ln2_b, v_w_in, v_lb_logits, v_a_norm_w, v_c_sinks, v_w_out, v_ln1_g, v_ln1_b, v_w_gate, v_w_up, v_conv_w, v_conv_b, v_w_down, v_ln2_g, v_ln2_b):
    cx, cy, cc = _coords()
    c_idx = jnp.reshape(cc, (1,)).astype(jnp.int32)
    k_me = 2 * cx + cy
    k_idx = jnp.reshape(k_me, (1,)).astype(jnp.int32)
    kc_idx = jnp.stack([k_me, cc]).astype(jnp.int32)

    def slot(nm, w, l, run_after=None):
        b = _into_slot(f"slot_{nm}{l}", w, l, k_idx, BF16, run_after)
        return b.reshape(N_CHIPS, 2, b.shape[1] // 2, b.shape[2])

    cw_slot = _into_slot("slot_cw", conv_w.reshape(1, DEPTH * CONV_WIDTH, FF_SHARD), 0, k_idx, F32)
    cw_slot = cw_slot.reshape(N_CHIPS, DEPTH, CONV_WIDTH, FF_SHARD)
    first, token = _gather_start("gather_start0", [[slot("wi", w_in, 0), cw_slot]])
    sl = [{nm: slot(nm, w, l, token) for nm, w in (("wi", w_in), ("wo", w_out), ("wg", w_gate), ("wu", w_up), ("wd", w_down))
           if (nm, l) != ("wi", 0)} for l in range(DEPTH)]
    order = [(l, nm) for l in range(DEPTH) for nm in ("wi", "wo", "wg", "wu", "wd")][1:]
    rest, token = _gather_start("gather_start1", [[sl[l][nm]] for l, nm in order])
    stage_of = {key: st for key, st in zip(order, rest)}

    def mat(b):
        return b.reshape(N_CHIPS, 2 * b.shape[2], b.shape[3])

    h = x[0]
    h_bf = _to_bf16("x_bf16", h, token)
    fwd0, token = _gather_forward("gather_fwd0", first[0], h_bf)
    wi0, cw_all = _gather_wait("gather_wait0", fwd0, token)
    cw_full = jnp.transpose(cw_all, (1, 2, 0, 3)).reshape(DEPTH, CONV_WIDTH, D_FF)
    tables = _rope_tables()

    passing = {}

    def pass_on(l, nm, after):
        passing[(l, nm)] = _gather_forward(f"gather_fwd_{nm}{l}", stage_of[(l, nm)], after)

    def arrived(l, nm, after):
        i = order.index((l, nm))
        if i + 1 < len(order):
            pass_on(*order[i + 1], after)
            after = passing[order[i + 1]][1]
        return mat(_gather_wait(f"gather_wait_{nm}{l}", passing[(l, nm)][0], after)[0])

    saved = []
    weights = []
    for l in range(DEPTH):
        wi = mat(wi0) if l == 0 else arrived(l, "wi", h)
        proj = _fwd_colsharded(f"proj{l}", h_bf, wi)
        mixed, mixed_bf, raw = _hgrn_fwd(f"hgrn_fwd{l}", proj, lb_logits, a_norm_w[l], l)
        mixed, mixed_bf, lse_b = _attn_fwd(f"dilated_fwd{l}", proj, tables, None, mixed, mixed_bf, n_heads=B_HEADS, rep=1,
                                           q0=QB0, k0=KB0, v0=VB0, m0=A_HEADS, patterns=B_PATTERNS)
        if l == 0:
            pass_on(l, "wo", lse_b)
        sink_b = jnp.broadcast_to(c_sinks[l][:, None, None], (C_HEADS, 8, BLK))
        mixed, mixed_bf, lse_c = _attn_fwd(f"window_fwd{l}", proj, tables, sink_b, mixed, mixed_bf, n_heads=C_HEADS,
                                           rep=C_HEADS // C_KV_HEADS, q0=QC0, k0=KC0, v0=VC0, m0=A_HEADS + B_HEADS,
                                           patterns=C_PATTERNS)
        wo = arrived(l, "wo", lse_c)
        y1 = _fwd_rowsharded(f"wout{l}", mixed_bf, wo, OUT_SHARD)
        x1, x1_bf = _ln_fwd(f"ln1_fwd{l}", h, y1, ln1_g[l], ln1_b[l])
        wg = arrived(l, "wg", x1)
        g = _fwd_colsharded(f"gate{l}", x1_bf, wg, BF16)
        wu = arrived(l, "wu", g)
        u = _fwd_colsharded(f"up{l}", x1_bf, wu, BF16)
        hh = _conv_gate_fwd(f"conv_fwd{l}", g, u, cw_full[l], conv_b[l])
        wd = arrived(l, "wd", hh)
        y2 = _fwd_rowsharded(f"down{l}", hh, wd, FF_SHARD)
        weights.append(dict(wi=wi, wo=wo, wg=wg, wu=wu, wd=wd))
        saved.append((h, h_bf, proj, raw, lse_b, sink_b, lse_c, mixed, mixed_bf, y1, x1, x1_bf, g, u, hh, y2))
        if l + 1 < DEPTH:
            h, h_bf = _ln_fwd(f"ln2_fwd{l}", x1, y2, ln2_g[l], ln2_b[l])

    d_res = d_path = None
    small = [None] * DEPTH
    mat_grads = [None] * DEPTH
    late = {}
    prev_ffn = prev_mix_swap = None
    for l in reversed(range(DEPTH)):
        h_in, h_in_bf, proj, raw, lse_b, sink_b, lse_c, mixed, mixed_bf, y1, x1, x1_bf, g, u, hh, y2 = saved[l]
        wi, wo, wg, wu, wd = (weights[l][k] for k in ("wi", "wo", "wg", "wu", "wd"))
        if l == DEPTH - 1:
            dz2, dz2_bf, d_ln2g, d_ln2b, loss_part = _ln_loss_bwd(f"ln2_loss_bwd{l}", x1, y2, ln2_g[l], ln2_b[l],
                                                                  loss_target[0])
        else:
            dz2, dz2_bf, d_ln2g, d_ln2b = _ln_bwd(f"ln2_bwd{l}", x1, y2, ln2_g[l], d_res, d_path,
                                                  run_after=prev_mix_swap[2][0])
        dhh = _bwd_act_rowsharded(f"down_dx{l}", dz2_bf, wd, FF_SHARD, BF16)
        prev_mix = _rs_scatter(f"{l + 1}m", prev_mix_swap, dhh, c_idx) if prev_mix_swap else None
        d_wd = _bwd_w_rowsharded(f"down_dw{l}", hh, dz2_bf, FF_SHARD)
        dg, du, d_cw, d_cb = _conv_gate_bwd(f"conv_bwd{l}", g, u, cw_full[l], conv_b[l], dhh,
                                            run_after=prev_mix[2][0] if prev_mix else None)
        dx1 = _bwd_act_colsharded(f"gateup_dx{l}", [(dg, wg), (du, wu)])
        d_wg = _bwd_w_colsharded(f"gate_dw{l}", x1_bf, dg)
        d_wu = _bwd_w_colsharded(f"up_dw{l}", x1_bf, du)
        pins = ()
        if prev_ffn:
            late[l + 1] = [_rs_reduce(f"{l + 1}f", prev_ffn, d_wu, kc_idx)]
        ffn_swap = _rs_swap(f"{l}f", [d_wg, d_wu, d_wd])
        dz1, dz1_bf, d_ln1g, d_ln1b = _ln_bwd(f"ln1_bwd{l}", h_in, y1, ln1_g[l], dz2, dx1, run_after=ffn_swap[2][0])
        dmix = _bwd_act_rowsharded(f"wout_dx{l}", dz1_bf, wo, OUT_SHARD)
        d_wo = _bwd_w_rowsharded(f"wout_dw{l}", mixed_bf, dz1_bf, OUT_SHARD)
        if prev_mix:
            late[l + 1].append(_rs_reduce(f"{l + 1}m", prev_mix, d_wo, kc_idx))
            pins = tuple(g[2][0] for g in late[l + 1])
        s_ffn = _rs_scatter(f"{l}f", ffn_swap, (d_wo,) + pins, c_idx)
        dproj, d_nw, d_lb = _hgrn_bwd(f"hgrn_bwd{l}", proj, raw, dmix, lb_logits, a_norm_w[l], l, run_after=s_ffn[2][0])
        dproj, _ = _attn_bwd(f"dilated_bwd{l}", proj, mixed, dmix, lse_b, tables, None, dproj, n_kv=B_HEADS, rep=1,
                             q0=QB0, k0=KB0, v0=VB0, m0=A_HEADS, patterns=B_PATTERNS)
        dproj, d_sink = _attn_bwd(f"window_bwd{l}", proj, mixed, dmix, lse_c, tables, sink_b, dproj, n_kv=C_KV_HEADS,
                                  rep=C_HEADS // C_KV_HEADS, q0=QC0, k0=KC0, v0=VC0, m0=A_HEADS + B_HEADS,
                                  patterns=C_PATTERNS)
        dxp = _bwd_act_colsharded_full(f"proj_dx{l}", dproj, wi, residual=dz1 if l == 0 else None)
        d_wi = _bwd_w_colsharded(f"proj_dw{l}", h_in_bf, dproj)
        d_res, d_path = dz1, dxp
        prev_ffn, prev_mix_swap = s_ffn, _rs_swap(f"{l}m", [d_wi, d_wo])
        small[l] = (d_lb, d_nw.reshape(A_HEADS, 8, BLK)[:, 0].sum(0), d_sink[:, 0, 0], d_ln1g[0], d_ln1b[0],
                    d_cw, d_cb[0], d_ln2g[0], d_ln2b[0])
    grad_x2 = d_path
    grad_x = grad_x2[None]

    g_lb = small[0][0] + small[1][0]
    per_layer = [jnp.stack([small[0][i], small[1][i]]) for i in range(1, 9)]
    small_shapes = [(DEPTH, 4 * BLK), (DEPTH, BLK), (DEPTH, C_HEADS), (DEPTH, D_MODEL), (DEPTH, D_MODEL),
                    (DEPTH, CONV_WIDTH, D_FF), (DEPTH, D_FF), (DEPTH, D_MODEL), (DEPTH, D_MODEL), (BLK,)]
    rows = _rows_for(small_shapes)
    total = _sum_small(_gather_small(_pack([g_lb] + per_layer + [loss_part[0]], rows), prev_mix_swap[2][0]))
    g_lb, g_nw, g_sink, g_ln1g, g_ln1b, g_cw_full, g_cb, g_ln2g, g_ln2b, loss_row = _unpack(total, small_shapes)
    loss = loss_row[0]
    g_cw = lax.dynamic_slice_in_dim(g_cw_full, k_me * FF_SHARD, FF_SHARD, axis=2)

    sw = [lb_logits, a_norm_w, c_sinks, ln1_g, ln1_b, conv_w, conv_b, ln2_g, ln2_b]
    sg = [g_lb, g_nw, g_sink, g_ln1g, g_ln1b, g_cw, g_cb, g_ln2g, g_ln2b]
    sm = [m_lb_logits, m_a_norm_w, m_c_sinks, m_ln1_g, m_ln1_b, m_conv_w, m_conv_b, m_ln2_g, m_ln2_b]
    sv = [v_lb_logits, v_a_norm_w, v_c_sinks, v_ln1_g, v_ln1_b, v_conv_w, v_conv_b, v_ln2_g, v_ln2_b]
    shapes = [a.shape for a in sw]
    prow = _rows_for(shapes)
    sd, snm, snv = (_unpack(b, shapes) for b in _adamw_small(_pack(sw, prow), _pack(sg, prow), _pack(sm, prow), _pack(sv, prow)))

    names = ["w_in", "w_out", "w_gate", "w_up", "w_down"]
    mw = [w_in, w_out, w_gate, w_up, w_down]
    mm = [m_w_in, m_w_out, m_w_gate, m_w_up, m_w_down]
    mv = [v_w_in, v_w_out, v_w_gate, v_w_up, v_w_down]
    res = [None] * 5
    s_mix = _rs_scatter("0m", prev_mix_swap, total, c_idx)
    for l, (g_ffn, g_mix) in late.items():
        g_wg, g_wu, g_wd = _rs_finish(f"{l}f", g_ffn, s_mix[2][0])
        g_wi, g_wo = _rs_finish(f"{l}m", g_mix, s_mix[2][0])
        mat_grads[l] = [g_wi, g_wo, g_wg, g_wu, g_wd]
    ffn0 = _rs_finish("0f", _rs_reduce("0f", prev_ffn, s_mix[2][0], kc_idx), s_mix[2][0])
    for i, g0 in zip((2, 3, 4), ffn0):
        res[i] = _adamw_matrix(f"adamw_{names[i]}", mw[i], [g0, mat_grads[1][i]], mm[i], mv[i])
    mix0 = _rs_finish("0m", _rs_reduce("0m", s_mix, res[4][1], kc_idx), res[4][1])
    for i, g0 in zip((0, 1), mix0):
        res[i] = _adamw_matrix(f"adamw_{names[i]}", mw[i], [g0, mat_grads[1][i]], mm[i], mv[i])
    mg, md, mnm, mnv = ([r[j] for r in res] for j in range(4))

    def ordered(mat, sm_):
        return [mat[0], sm_[0], sm_[1], sm_[2], mat[1], sm_[3], sm_[4], mat[2], mat[3], sm_[5], sm_[6], mat[4], sm_[7], sm_[8]]

    return (loss, grad_x, *ordered(mg, sg), *ordered(md, sd), *ordered(mnm, snm), *ordered(mnv, snv))
```

```python
import functools
import math

import jax
import jax.numpy as jnp
from jax import lax
from jax.experimental import pallas as pl
from jax.experimental.pallas import tpu as pltpu

F32 = jnp.float32
BF16 = jnp.bfloat16

D_MODEL = 2048
SEQ = 2048
DEPTH = 2
HEAD_DIM = 128
A_HEADS = 4
B_HEADS = 6
C_HEADS = 6
C_KV_HEADS = 2
A_CHUNK = 16
DILATED_PATTERNS = ((128, 1), (512, 4), (2048, 16))
C_WINDOW = 128
ROPE_THETA = 500000.0
ROPE_DIM = HEAD_DIM // 4
D_FF = 5632
CONV_WIDTH = 3
LN_EPS = 1e-5
ALPHA = (2 * DEPTH) ** 0.25
IN_WIDTH = 5632
MIX_WIDTH = 2048
ADAM_LR = 0.001
ADAM_B1 = 0.9
ADAM_B2 = 0.999
ADAM_EPS = 1e-08
ADAM_WD = 0.01
ADAM_STEP = 10

N_CHIPS = 4
N_DEV = 8
FF_SHARD = D_FF // N_CHIPS
OUT_SHARD = MIX_WIDTH // N_CHIPS
BLK = 128
N_CHUNK = SEQ // A_CHUNK
SLAB = 32

QA0, FA0, IA0, GA0 = 0, 4, 8, 12
QB0, KB0, VB0 = 16, 22, 28
QC0, KC0, VC0 = 34, 40, 42

VMEM_LIMIT_V7X = 56 * 1024 * 1024
HI = lax.Precision.HIGHEST
MESH = pl.DeviceIdType.MESH


def _cp(sem=None, vmem=VMEM_LIMIT_V7X, **kw):
    return pltpu.CompilerParams(dimension_semantics=sem, vmem_limit_bytes=vmem, **kw)


def _sigmoid(x):
    return 1.0 / (1.0 + jnp.exp(-x))


def _gate_sigmoid(x):
    return 0.5 * jnp.tanh(0.5 * x) + 0.5


def _mm(name, pairs, dims, grid, a_specs, b_specs, out_spec, out_shape, nk=1, acc_shape=None):
    n_pairs = len(pairs)

    def body(*refs):
        o_ref = refs[2 * n_pairs]
        part = None
        for p in range(n_pairs):
            a = refs[2 * p][...].astype(BF16)
            b = refs[2 * p + 1][...].astype(BF16)
            t = lax.dot_general(a, b, dims, preferred_element_type=F32)
            part = t if part is None else part + t
        if nk == 1:
            o_ref[...] = part.astype(o_ref.dtype)
        else:
            acc = refs[2 * n_pairs + 1]
            k = pl.program_id(len(grid) - 1)

            @pl.when(k == 0)
            def _():
                acc[...] = part

            @pl.when(k > 0)
            def _():
                acc[...] += part

            @pl.when(k == nk - 1)
            def _():
                o_ref[...] = acc[...].astype(o_ref.dtype)

    in_specs, args = [], []
    for (a, b), sa, sb in zip(pairs, a_specs, b_specs):
        in_specs += [sa, sb]
        args += [a, b]
    sem = ("parallel",) * (len(grid) - (1 if nk > 1 else 0)) + (("arbitrary",) if nk > 1 else ())
    return pl.pallas_call(
        body, name=name, grid=grid, in_specs=in_specs, out_specs=out_spec, out_shape=out_shape,
        scratch_shapes=[pltpu.VMEM(acc_shape, F32)] if nk > 1 else [],
        compiler_params=_cp(sem),
    )(*args)


NN = (((1,), (0,)), ((), ()))
NT = (((1,), (1,)), ((), ()))
TN = (((0,), (0,)), ((), ()))
TM = 1024


def _fwd_colsharded(name, x, w_stk, out_dtype=F32):
    return _mm(name, [(x, w_stk)], NN, (N_CHIPS, SEQ // TM),
               [pl.BlockSpec((TM, D_MODEL), lambda j, i: (i, 0))],
               [pl.BlockSpec((None, D_MODEL, FF_SHARD), lambda j, i: (j, 0, 0))],
               pl.BlockSpec((TM, FF_SHARD), lambda j, i: (i, j)),
               jax.ShapeDtypeStruct((SEQ, D_FF), out_dtype))


def _fwd_rowsharded(name, a, w_stk, shard):
    tn = 512
    rows = N_CHIPS * shard
    return _mm(name, [(a, w_stk.reshape(rows, D_MODEL))], NN, (SEQ // TM, D_MODEL // tn),
               [pl.BlockSpec((TM, rows), lambda i, j: (i, 0))],
               [pl.BlockSpec((rows, tn), lambda i, j: (0, j))],
               pl.BlockSpec((TM, tn), lambda i, j: (i, j)),
               jax.ShapeDtypeStruct((SEQ, D_MODEL), F32))


def _bwd_act_colsharded(name, pairs):
    tn = 1024
    n = len(pairs)
    return _mm(name, pairs, NT, (SEQ // TM, D_MODEL // tn, N_CHIPS),
               [pl.BlockSpec((TM, FF_SHARD), lambda i, j, k: (i, k))] * n,
               [pl.BlockSpec((None, tn, FF_SHARD), lambda i, j, k: (k, j, 0))] * n,
               pl.BlockSpec((TM, tn), lambda i, j, k: (i, j)),
               jax.ShapeDtypeStruct((SEQ, D_MODEL), F32), nk=N_CHIPS, acc_shape=(TM, tn))


def _bwd_act_colsharded_full(name, dy, w_stk, residual=None):
    tn = 512

    def body(a_ref, w_ref, *rest):
        acc = None
        for k in range(N_CHIPS):
            t = lax.dot_general(a_ref[:, k * FF_SHARD:(k + 1) * FF_SHARD], w_ref[k], NT, preferred_element_type=F32)
            acc = t if acc is None else acc + t
        if residual is not None:
            acc = ALPHA * rest[0][...] + acc
        rest[-1][...] = acc

    out = pl.BlockSpec((TM, tn), lambda i, j: (i, j))
    extra = [] if residual is None else [residual]
    return pl.pallas_call(
        body, name=name, grid=(SEQ // TM, D_MODEL // tn),
        in_specs=[pl.BlockSpec((TM, D_FF), lambda i, j: (i, 0)),
                  pl.BlockSpec((N_CHIPS, tn, FF_SHARD), lambda i, j: (0, j, 0))] + [out] * len(extra),
        out_specs=out, out_shape=jax.ShapeDtypeStruct((SEQ, D_MODEL), F32),
        compiler_params=_cp(("parallel", "parallel")))(dy, w_stk, *extra)


def _bwd_act_rowsharded(name, dy, w_stk, shard, out_dtype=F32):
    return _mm(name, [(dy, w_stk)], NT, (N_CHIPS, SEQ // TM),
               [pl.BlockSpec((TM, D_MODEL), lambda j, i: (i, 0))],
               [pl.BlockSpec((None, shard, D_MODEL), lambda j, i: (j, 0, 0))],
               pl.BlockSpec((TM, shard), lambda j, i: (i, j)),
               jax.ShapeDtypeStruct((SEQ, N_CHIPS * shard), out_dtype))


def _bwd_w_colsharded(name, x, dy):
    tm = 1024
    return _mm(name, [(x, dy)], TN, (N_CHIPS, D_MODEL // tm),
               [pl.BlockSpec((SEQ, tm), lambda j, i: (0, i))],
               [pl.BlockSpec((SEQ, FF_SHARD), lambda j, i: (0, j))],
               pl.BlockSpec((None, tm, FF_SHARD), lambda j, i: (j, i, 0)),
               jax.ShapeDtypeStruct((N_CHIPS, D_MODEL, FF_SHARD), BF16))


def _bwd_w_rowsharded(name, a, dy, shard):
    tn = 1024
    return _mm(name, [(a, dy)], TN, (N_CHIPS, D_MODEL // tn),
               [pl.BlockSpec((SEQ, shard), lambda j, i: (0, j))],
               [pl.BlockSpec((SEQ, tn), lambda j, i: (0, i))],
               pl.BlockSpec((None, shard, tn), lambda j, i: (j, 0, i)),
               jax.ShapeDtypeStruct((N_CHIPS, shard, D_MODEL), BF16))


TR = 256


def _ln_fwd(name, x, y, g, b):
    def body(x_ref, y_ref, g_ref, b_ref, o_ref, ob_ref):
        z = ALPHA * x_ref[...] + y_ref[...]
        mu = jnp.mean(z, -1, keepdims=True)
        zc = z - mu
        var = jnp.mean(zc * zc, -1, keepdims=True)
        o = zc * lax.rsqrt(var + LN_EPS) * g_ref[...] + b_ref[...]
        o_ref[...] = o
        ob_ref[...] = o.astype(BF16)

    row = pl.BlockSpec((TR, D_MODEL), lambda i: (i, 0))
    vec = pl.BlockSpec((1, D_MODEL), lambda i: (0, 0))
    return pl.pallas_call(body, name=name, grid=(SEQ // TR,), in_specs=[row, row, vec, vec], out_specs=[row, row],
                          out_shape=[jax.ShapeDtypeStruct((SEQ, D_MODEL), F32), jax.ShapeDtypeStruct((SEQ, D_MODEL), BF16)],
                          compiler_params=_cp(("parallel",)))(x, y, g.reshape(1, -1), b.reshape(1, -1))


def _to_bf16(name, x, run_after):
    def body(x_ref, _, o_ref):
        o_ref[...] = x_ref[...].astype(BF16)

    row = pl.BlockSpec((TR, D_MODEL), lambda i: (i, 0))
    return pl.pallas_call(body, name=name, grid=(SEQ // TR,), in_specs=[row, pl.BlockSpec(memory_space=pl.ANY)],
                          out_specs=row, out_shape=jax.ShapeDtypeStruct((SEQ, D_MODEL), BF16),
                          compiler_params=_cp(("parallel",)))(x, run_after)


def _ln_bwd(name, x, y, g, d_res, d_path, run_after):
    def body(x_ref, y_ref, g_ref, r_ref, p_ref, _, dz_ref, dzb_ref, dg_ref, db_ref):
        dout = ALPHA * r_ref[...] + p_ref[...]
        z = ALPHA * x_ref[...] + y_ref[...]
        mu = jnp.mean(z, -1, keepdims=True)
        zc = z - mu
        rstd = lax.rsqrt(jnp.mean(zc * zc, -1, keepdims=True) + LN_EPS)
        zh = zc * rstd
        dzh = dout * g_ref[...]
        dz = rstd * (dzh - jnp.mean(dzh, -1, keepdims=True) - zh * jnp.mean(dzh * zh, -1, keepdims=True))
        dz_ref[...] = dz
        dzb_ref[...] = dz.astype(BF16)
        pg = jnp.sum(dout * zh, 0, keepdims=True)
        pb = jnp.sum(dout, 0, keepdims=True)

        @pl.when(pl.program_id(0) == 0)
        def _():
            dg_ref[...] = pg
            db_ref[...] = pb

        @pl.when(pl.program_id(0) > 0)
        def _():
            dg_ref[...] += pg
            db_ref[...] += pb

    row = pl.BlockSpec((TR, D_MODEL), lambda i: (i, 0))
    vec = pl.BlockSpec((1, D_MODEL), lambda i: (0, 0))
    args = [x, y, g.reshape(1, -1), d_res, d_path, run_after]
    in_specs = [row, row, vec, row, row, pl.BlockSpec(memory_space=pl.ANY)]
    vshape = jax.ShapeDtypeStruct((1, D_MODEL), F32)
    return pl.pallas_call(body, name=name, grid=(SEQ // TR,), in_specs=in_specs, out_specs=[row, row, vec, vec],
                          out_shape=[jax.ShapeDtypeStruct((SEQ, D_MODEL), F32), jax.ShapeDtypeStruct((SEQ, D_MODEL), BF16),
                                     vshape, vshape],
                          compiler_params=_cp(("arbitrary",)))(*args)


def _ln_loss_bwd(name, x, y, g, b, target):
    def body(x_ref, y_ref, g_ref, b_ref, t_ref, dz_ref, dzb_ref, dg_ref, db_ref, l_ref):
        z = ALPHA * x_ref[...] + y_ref[...]
        mu = jnp.mean(z, -1, keepdims=True)
        zc = z - mu
        rstd = lax.rsqrt(jnp.mean(zc * zc, -1, keepdims=True) + LN_EPS)
        zh = zc * rstd
        e = zh * g_ref[...] + b_ref[...] - t_ref[...]
        dout = e * (1.0 / D_MODEL)
        dzh = dout * g_ref[...]
        dz = rstd * (dzh - jnp.mean(dzh, -1, keepdims=True) - zh * jnp.mean(dzh * zh, -1, keepdims=True))
        dz_ref[...] = dz
        dzb_ref[...] = dz.astype(BF16)
        pg = jnp.sum(dout * zh, 0, keepdims=True)
        pb = jnp.sum(dout, 0, keepdims=True)
        part = jnp.full((8, BLK), 0.5 / D_MODEL * jnp.sum(e * e), F32)

        @pl.when(pl.program_id(0) == 0)
        def _():
            dg_ref[...] = pg
            db_ref[...] = pb
            l_ref[...] = part

        @pl.when(pl.program_id(0) > 0)
        def _():
            dg_ref[...] += pg
            db_ref[...] += pb
            l_ref[...] += part

    row = pl.BlockSpec((TR, D_MODEL), lambda i: (i, 0))
    vec = pl.BlockSpec((1, D_MODEL), lambda i: (0, 0))
    vshape = jax.ShapeDtypeStruct((1, D_MODEL), F32)
    return pl.pallas_call(body, name=name, grid=(SEQ // TR,), in_specs=[row, row, vec, vec, row],
                          out_specs=[row, row, vec, vec, pl.BlockSpec((8, BLK), lambda i: (0, 0))],
                          out_shape=[jax.ShapeDtypeStruct((SEQ, D_MODEL), F32), jax.ShapeDtypeStruct((SEQ, D_MODEL), BF16),
                                     vshape, vshape, jax.ShapeDtypeStruct((8, BLK), F32)],
                          compiler_params=_cp(("arbitrary",)))(x, y, g.reshape(1, -1), b.reshape(1, -1), target)


TC = 512


def _shift_down(x, s, rows):
    if s == 0:
        return x
    return jnp.where(rows >= s, pltpu.roll(x, s, axis=0), 0.0)


def _shift_up(x, s, rows):
    if s == 0:
        return x
    return jnp.where(rows < SEQ - s, pltpu.roll(x, SEQ - s, axis=0), 0.0)


def _conv_gate_fwd(name, g, u, cw, cb):
    def body(g_ref, u_ref, w_ref, b_ref, h_ref):
        gg = g_ref[...].astype(F32)
        rows = lax.broadcasted_iota(jnp.int32, gg.shape, 0)
        gc = b_ref[...] + w_ref[2:3, :] * gg
        gc = gc + w_ref[1:2, :] * _shift_down(gg, 1, rows)
        gc = gc + w_ref[0:1, :] * _shift_down(gg, 2, rows)
        h_ref[...] = (gc * _gate_sigmoid(gc) * u_ref[...].astype(F32)).astype(BF16)

    col = pl.BlockSpec((SEQ, TC), lambda j: (0, j))
    return pl.pallas_call(body, name=name, grid=(D_FF // TC,),
                          in_specs=[col, col, pl.BlockSpec((CONV_WIDTH, TC), lambda j: (0, j)),
                                    pl.BlockSpec((1, TC), lambda j: (0, j))],
                          out_specs=col, out_shape=jax.ShapeDtypeStruct((SEQ, D_FF), BF16),
                          compiler_params=_cp(("parallel",)))(g, u, cw, cb.reshape(1, -1))


def _conv_gate_bwd(name, g, u, cw, cb, dh, run_after=None):
    def body(g_ref, u_ref, w_ref, b_ref, dh_ref, *rest):
        dg_ref, du_ref, dw_ref, db_ref = rest[-4:]
        gg = g_ref[...].astype(F32)
        rows = lax.broadcasted_iota(jnp.int32, gg.shape, 0)
        g1 = _shift_down(gg, 1, rows)
        g2 = _shift_down(gg, 2, rows)
        gc = b_ref[...] + w_ref[2:3, :] * gg + w_ref[1:2, :] * g1 + w_ref[0:1, :] * g2
        sg = _gate_sigmoid(gc)
        act = gc * sg
        dh = dh_ref[...].astype(F32)
        du_ref[...] = (dh * act).astype(BF16)
        dgc = dh * u_ref[...].astype(F32) * (sg * (1.0 + gc * (1.0 - sg)))
        db_ref[...] = jnp.sum(dgc, 0, keepdims=True)
        dw_ref[2:3, :] = jnp.sum(dgc * gg, 0, keepdims=True)
        dw_ref[1:2, :] = jnp.sum(dgc * g1, 0, keepdims=True)
        dw_ref[0:1, :] = jnp.sum(dgc * g2, 0, keepdims=True)
        dg_ref[...] = (w_ref[2:3, :] * dgc + w_ref[1:2, :] * _shift_up(dgc, 1, rows)
                       + w_ref[0:1, :] * _shift_up(dgc, 2, rows)).astype(BF16)

    col = pl.BlockSpec((SEQ, TC), lambda j: (0, j))
    w3 = pl.BlockSpec((CONV_WIDTH, TC), lambda j: (0, j))
    w1 = pl.BlockSpec((1, TC), lambda j: (0, j))
    big = jax.ShapeDtypeStruct((SEQ, D_FF), BF16)
    extra = [] if run_after is None else [run_after]
    return pl.pallas_call(body, name=name, grid=(D_FF // TC,),
                          in_specs=[col, col, w3, w1, col] + [pl.BlockSpec(memory_space=pl.ANY)] * len(extra),
                          out_specs=[col, col, w3, w1],
                          out_shape=[big, big, jax.ShapeDtypeStruct((CONV_WIDTH, D_FF), F32),
                                     jax.ShapeDtypeStruct((1, D_FF), F32)],
                          compiler_params=_cp(("parallel",)))(g, u, cw, cb.reshape(1, -1), dh, *extra)


def _lbs_of(logits, layer):
    m = jnp.max(logits, 0, keepdims=True)
    e = jnp.exp(logits - m)
    p = e / jnp.sum(e, 0, keepdims=True)
    lb = jnp.zeros((1, BLK), F32)
    for r in range(1, layer + 1):
        lb = lb + p[r:r + 1, :]
    return lb, p


def _dlogits_of(p, dlb, layer):
    rows = lax.broadcasted_iota(jnp.int32, p.shape, 0)
    dp = jnp.where((rows >= 1) & (rows <= layer), dlb, 0.0)
    return p * (dp - jnp.sum(p * dp, 0, keepdims=True))


SROWS = SLAB * A_CHUNK
N_SLAB = N_CHUNK // SLAB


def _chunk_prefix(x, rowi):
    for s in (1, 2, 4, 8):
        x = x + jnp.where(rowi >= s, pltpu.roll(x, s, axis=0), 0.0)
    return x


def _chunk_suffix(x, rowi):
    for s in (1, 2, 4, 8):
        x = x + jnp.where(rowi < A_CHUNK - s, pltpu.roll(x, SROWS - s, axis=0), 0.0)
    return x


def _c3(x):
    return x.reshape(SLAB, A_CHUNK, BLK)


def _c2(x):
    return x.reshape(SROWS, BLK)


def _split(x):
    top = lax.bitcast_convert_type(lax.bitcast_convert_type(x, jnp.uint32) & jnp.uint32(0xFFFF0000), F32)
    return top.astype(BF16), (x - top).astype(BF16)


def _bmm(eq, a, b):
    ah, al = _split(a)
    bh, bl = _split(b)

    def mm(u, v):
        return jnp.einsum(eq, u, v, preferred_element_type=F32)

    return mm(ah, bh) + (mm(ah, bl) + mm(al, bh))


def _bmm_1pass(eq, a, b):
    return jnp.einsum(eq, a.astype(BF16), b.astype(BF16), preferred_element_type=F32)


def _slab_rows(s):
    return pl.ds(s * SROWS, SROWS)


def _hgrn_prep(q, f, lb):
    rowi = lax.broadcasted_iota(jnp.int32, (SROWS, BLK), 0) & (A_CHUNK - 1)
    sq = _gate_sigmoid(q)
    qc = q * sq
    sf = _sigmoid(f)
    fg = lb + (1.0 - lb) * sf
    kc = 1.0 - fg
    b = _chunk_prefix(jnp.log(fg), rowi)
    b3 = _c3(b)
    blast = b3[:, A_CHUNK - 1:A_CHUNK, :]
    eb = jnp.exp(b)
    ekb = _c2(jnp.exp(blast - b3))
    dec = jnp.exp(blast.reshape(SLAB, BLK))
    return rowi, sq, qc, sf, fg, kc, b, eb, ekb, dec


def _hgrn_slab_states(s, carry, v, ke, dec, dec_ref, u_ref, st_ref):
    dec_ref[pl.ds(s * SLAB, SLAB), :] = dec
    u_ref[...] = _bmm('ncv,nck->nvk', _c3(v), _c3(ke))

    def step(j, c):
        st_ref[j] = c
        return dec_ref[pl.ds(s * SLAB + j, 1), :] * c + u_ref[j]

    return lax.fori_loop(0, SLAB, step, carry)


def _hgrn_fwd(name, proj, lb_logits, nw, layer):
    def body(q_ref, f_ref, i_ref, g_ref, lg_ref, nw_ref, out_ref, outb_ref, raw_ref, dec_ref, u_ref, st_ref):
        lb, _ = _lbs_of(lg_ref[...], layer)
        ones = jnp.ones((BLK, BLK), BF16)
        carry = jnp.zeros((BLK, BLK), F32)
        for s in range(N_SLAB):
            rows = _slab_rows(s)
            v = i_ref[rows, :]
            rowi, sq, qc, sf, fg, kc, b, eb, ekb, dec = _hgrn_prep(q_ref[rows, :], f_ref[rows, :], lb)
            carry = _hgrn_slab_states(s, carry, v, kc * ekb, dec, dec_ref, u_ref, st_ref)
            o = _c2(_bmm('nck,nvk->ncv', _c3(qc * eb), st_ref[...]))
            qc3, kc3, b3, v3, row3 = _c3(qc), _c3(kc), _c3(b), _c3(v), _c3(rowi)
            col = lax.broadcasted_iota(jnp.int32, (SLAB, A_CHUNK, A_CHUNK), 2)
            att_all = jnp.zeros((SLAB, A_CHUNK, A_CHUNK), F32)
            for j in range(A_CHUNK):
                dj = jnp.exp(jnp.where(row3 >= j, b3 - b3[:, j:j + 1, :], -jnp.inf))
                a = jnp.dot(_c2(qc3 * dj * kc3[:, j:j + 1, :]).astype(BF16), ones, preferred_element_type=F32)
                att_all = jnp.where(col == j, _c3(a)[:, :, :A_CHUNK], att_all)
            o = o + _c2(_bmm_1pass('nij,njv->niv', att_all, v3))
            raw_ref[rows, :] = o
            r = lax.rsqrt(jnp.mean(o * o, -1, keepdims=True) + LN_EPS)
            gg = g_ref[rows, :]
            gated = o * r * nw_ref[...] * (gg * _gate_sigmoid(gg))
            out_ref[rows, :] = gated
            outb_ref[rows, :] = gated.astype(BF16)

    def colblk(c0):
        return pl.BlockSpec((SEQ, BLK), lambda h: (0, c0 + h))

    return pl.pallas_call(
        body, name=name, grid=(A_HEADS,),
        in_specs=[colblk(QA0), colblk(FA0), colblk(IA0), colblk(GA0),
                  pl.BlockSpec((DEPTH, BLK), lambda h: (0, h)), pl.BlockSpec((1, BLK), lambda h: (0, 0))],
        out_specs=[colblk(0), colblk(0), colblk(0)],
        out_shape=[jax.ShapeDtypeStruct((SEQ, MIX_WIDTH), F32), jax.ShapeDtypeStruct((SEQ, MIX_WIDTH), BF16),
                   jax.ShapeDtypeStruct((SEQ, A_HEADS * BLK), F32)],
        scratch_shapes=[pltpu.VMEM((N_CHUNK, BLK), F32), pltpu.VMEM((SLAB, BLK, BLK), F32),
                        pltpu.VMEM((SLAB, BLK, BLK), F32)],
        compiler_params=_cp(("parallel",)))(proj, proj, proj, proj, lb_logits, nw.reshape(1, -1))


def _col_block_copies(stage, sems, dst, col_blocks, first):
    return [pltpu.make_async_copy(stage.at[first + t], dst.at[:, pl.ds(pl.multiple_of(cb * BLK, BLK), BLK)],
                                  sems.at[first + t]) for t, cb in enumerate(col_blocks)]


def _start_col_blocks(stage, sems, dst, col_blocks, first=0):
    for cp in _col_block_copies(stage, sems, dst, col_blocks, first):
        cp.start()


def _wait_col_blocks(stage, sems, dst, count, first=0):
    for cp in _col_block_copies(stage, sems, dst, [0] * count, first):
        cp.wait()


def _hgrn_bwd(name, proj, raw, dmix, lb_logits, nw, layer, run_after=None):
    extra = [] if run_after is None else [run_after]

    def body(q_ref, f_ref, i_ref, g_ref, raw_ref, do_ref, lg_ref, nw_ref, *rest):
        (dproj_ref, dnw_ref, dlg_ref,
         dec_ref, u_ref, st_ref, h_ref, dbs_ref, dkc_ref, tot_ref, stage, stage_sem) = rest[-12:]
        dq_ref, df_ref, di_ref, dg_ref = (stage.at[t] for t in range(4))
        lb, p = _lbs_of(lg_ref[...], layer)
        ones = jnp.ones((BLK, BLK), BF16)
        nwv = nw_ref[...]

        carry = jnp.zeros((BLK, BLK), F32)
        for s in range(N_SLAB):
            rows = _slab_rows(s)
            rowi, sq, qc, sf, fg, kc, b, eb, ekb, dec = _hgrn_prep(q_ref[rows, :], f_ref[rows, :], lb)
            carry = _hgrn_slab_states(s, carry, i_ref[rows, :], kc * ekb, dec, dec_ref, u_ref,
                                      st_ref.at[pl.ds(s * SLAB, SLAB)])

        @pl.when(pl.program_id(0) > 0)
        def _():
            _wait_col_blocks(stage, stage_sem, dproj_ref, 4)

        carry = jnp.zeros((BLK, BLK), F32)
        dnw = jnp.zeros((1, BLK), F32)
        for s in reversed(range(N_SLAB)):
            rows = _slab_rows(s)
            q, v = q_ref[rows, :], i_ref[rows, :]
            rowi, sq, qc, sf, fg, kc, b, eb, ekb, dec = _hgrn_prep(q, f_ref[rows, :], lb)
            ke = kc * ekb
            qe = qc * eb

            o = raw_ref[rows, :]
            gg = g_ref[rows, :]
            sgg = _gate_sigmoid(gg)
            dout = do_ref[rows, :]
            r = lax.rsqrt(jnp.mean(o * o, -1, keepdims=True) + LN_EPS)
            oh = o * r
            dg_ref[rows, :] = (dout * oh * nwv * (sgg * (1.0 + gg * (1.0 - sgg)))).astype(BF16)
            dn = dout * (gg * sgg)
            dnw = dnw + jnp.sum(dn * oh, 0, keepdims=True)
            doh = dn * nwv
            do = r * (doh - oh * jnp.mean(doh * oh, -1, keepdims=True))
            do3, qe3, v3, ke3 = _c3(do), _c3(qe), _c3(v), _c3(ke)

            u_ref[...] = _bmm('ncv,nck->nvk', do3, qe3)

            def step(jj, c, s=s):
                j = SLAB - 1 - jj
                h_ref[j] = c
                return u_ref[j] + dec_ref[pl.ds(s * SLAB + j, 1), :] * c

            carry = lax.fori_loop(0, SLAB, step, carry)

            hh = h_ref[...]
            dqc = _c2(_bmm('ncv,nvk->nck', do3, st_ref[pl.ds(s * SLAB, SLAB)])) * eb
            dkc = _c2(_bmm('ncv,nvk->nck', v3, hh)) * ekb
            dv = _c2(_bmm_1pass('nck,nvk->ncv', ke3, hh))

            qc3, kc3, b3, row3 = _c3(qc), _c3(kc), _c3(b), _c3(rowi)
            datt_all = _bmm('niv,njv->nij', do3, v3)
            col = lax.broadcasted_iota(jnp.int32, datt_all.shape, 2)
            att_all = jnp.zeros_like(datt_all)
            for j in range(A_CHUNK):
                dj = jnp.exp(jnp.where(row3 >= j, b3 - b3[:, j:j + 1, :], -jnp.inf))
                kj = kc3[:, j:j + 1, :]
                att = _c3(jnp.dot(_c2(qc3 * dj * kj).astype(BF16), ones, preferred_element_type=F32))
                att_all = jnp.where(col == j, att[:, :, :A_CHUNK], att_all)
                md = dj * datt_all[:, :, j:j + 1]
                dqc = dqc + _c2(md * kj)
                dkc = dkc + _c2(jnp.where(row3 == j, jnp.sum(md * qc3, 1, keepdims=True), 0.0))
            dv = dv + _c2(_bmm_1pass('nij,niv->njv', att_all, do3))
            di_ref[rows, :] = dv.astype(BF16)
            dq_ref[rows, :] = (dqc * (sq * (1.0 + q * (1.0 - sq)))).astype(BF16)

            dbs = _chunk_suffix(qc * dqc - kc * dkc, rowi)
            dbs_ref[rows, :] = dbs
            dkc_ref[rows, :] = dkc
            tot_ref[pl.ds(s * SLAB, SLAB), :] = _c3(dbs)[:, 0:1, :].reshape(SLAB, BLK)
        dnw_ref[...] = jnp.broadcast_to(dnw, (8, BLK))

        rn = lax.broadcasted_iota(jnp.int32, (N_CHUNK, N_CHUNK), 0)
        cn = lax.broadcasted_iota(jnp.int32, (N_CHUNK, N_CHUNK), 1)
        tot_ref[...] = jnp.dot((cn > rn).astype(F32), tot_ref[...], preferred_element_type=F32, precision=HI)
        dlb = jnp.zeros((1, BLK), F32)
        for s in range(N_SLAB):
            rows = _slab_rows(s)
            sf = _sigmoid(f_ref[rows, :])
            fg = lb + (1.0 - lb) * sf
            later = tot_ref[pl.ds(s * SLAB, SLAB), :]
            dlg = _c2(_c3(dbs_ref[rows, :]) + later[:, None, :])
            dfg = dlg / fg - dkc_ref[rows, :]
            df_ref[rows, :] = (dfg * (1.0 - lb) * sf * (1.0 - sf)).astype(BF16)
            dlb = dlb + jnp.sum(dfg * (1.0 - sf), 0, keepdims=True)
        dlg_ref[...] = _dlogits_of(p, dlb, layer)
        _start_col_blocks(stage, stage_sem, dproj_ref, [c0 + pl.program_id(0) for c0 in (QA0, FA0, IA0, GA0)])

        @pl.when(pl.program_id(0) == A_HEADS - 1)
        def _():
            _wait_col_blocks(stage, stage_sem, dproj_ref, 4)

    def colblk(c0):
        return pl.BlockSpec((SEQ, BLK), lambda h: (0, c0 + h))

    return pl.pallas_call(
        body, name=name, grid=(A_HEADS,),
        in_specs=[colblk(QA0), colblk(FA0), colblk(IA0), colblk(GA0), colblk(0), colblk(0),
                  pl.BlockSpec((DEPTH, BLK), lambda h: (0, h)), pl.BlockSpec((1, BLK), lambda h: (0, 0))]
        + [pl.BlockSpec(memory_space=pl.ANY)] * len(extra),
        out_specs=[pl.BlockSpec(memory_space=pl.ANY),
                   pl.BlockSpec((8, BLK), lambda h: (h, 0)), pl.BlockSpec((DEPTH, BLK), lambda h: (0, h))],
        out_shape=[jax.ShapeDtypeStruct((SEQ, IN_WIDTH), BF16), jax.ShapeDtypeStruct((A_HEADS * 8, BLK), F32),
                   jax.ShapeDtypeStruct((DEPTH, A_HEADS * BLK), F32)],
        scratch_shapes=[pltpu.VMEM((N_CHUNK, BLK), F32), pltpu.VMEM((SLAB, BLK, BLK), F32),
                        pltpu.VMEM((N_CHUNK, BLK, BLK), F32), pltpu.VMEM((SLAB, BLK, BLK), F32),
                        pltpu.VMEM((SEQ, BLK), F32), pltpu.VMEM((SEQ, BLK), F32), pltpu.VMEM((N_CHUNK, BLK), F32),
                        pltpu.VMEM((4, SEQ, BLK), BF16), pltpu.SemaphoreType.DMA((4,))],
        compiler_params=_cp(("arbitrary",)))(proj, proj, proj, proj, raw, dmix, lb_logits, nw.reshape(1, -1), *extra)


SCALE = HEAD_DIM ** -0.5


def _rope_tables():
    half = ROPE_DIM // 2
    inv = ROPE_THETA ** (-jnp.arange(0, ROPE_DIM, 2, dtype=F32) / ROPE_DIM)
    ang = jnp.arange(SEQ, dtype=F32)[:, None] * inv[None, :]
    cos, sin = jnp.cos(ang), jnp.sin(ang)
    pad = jnp.zeros((SEQ, HEAD_DIM - ROPE_DIM), F32)
    zero = jnp.zeros((SEQ, half), F32)
    c = jnp.concatenate([cos, cos, pad + 1.0], 1)
    s_lo = jnp.concatenate([zero, sin, pad], 1)
    s_hi = jnp.concatenate([-sin, zero, pad], 1)
    return c, s_lo, s_hi


def _rope(x, c, s_lo, s_hi):
    half = ROPE_DIM // 2
    return x * c + pltpu.roll(x, half, axis=1) * s_lo + pltpu.roll(x, HEAD_DIM - half, axis=1) * s_hi


def _unrope(dy, c, s_lo, s_hi):
    half = ROPE_DIM // 2
    return dy * c + pltpu.roll(dy * s_lo, HEAD_DIM - half, axis=1) + pltpu.roll(dy * s_hi, half, axis=1)


N_BLK = SEQ // BLK


def _block_rows(dil):
    nb = N_BLK // dil
    return [pl.ds(r + n * BLK * dil, BLK, stride=dil) for r in range(dil) for n in range(nb)]


def _to_blocks(ref, dil):
    if dil == 1:
        return ref[...].reshape(N_BLK, BLK, BLK)
    return jnp.stack([ref[rows, :] for rows in _block_rows(dil)], 0)


def _from_blocks(ref, val, dil, add=False):
    if dil == 1:
        flat = val.reshape(SEQ, BLK)
        ref[...] = ref[...] + flat if add else flat
        return
    for b, rows in enumerate(_block_rows(dil)):
        ref[rows, :] = ref[rows, :] + val[b] if add else val[b]


def _prev_block(x):
    return jnp.concatenate([x[:1], x[:-1]], axis=0)


def _to_next_block(x):
    return jnp.concatenate([x[1:], jnp.zeros_like(x[:1])], axis=0)


def _band_masks(max_lag, dil):
    r = lax.broadcasted_iota(jnp.int32, (N_BLK, BLK, BLK), 1)
    c = lax.broadcasted_iota(jnp.int32, (N_BLK, BLK, BLK), 2)
    b = lax.broadcasted_iota(jnp.int32, (N_BLK, BLK, BLK), 0)
    has_prev = (b % (N_BLK // dil)) != 0
    return r >= c, has_prev & (BLK + r - c <= max_lag)


def _bdot(eq, a, b):
    return jnp.einsum(eq, a, b, preferred_element_type=F32)


def _attn_fwd(name, proj, tables, sink_b, mixed, mixed_bf, *, n_heads, rep, q0, k0, v0, m0, patterns):
    n_pat = len(patterns)
    has_sink = sink_b is not None

    def body(*refs):
        o_ref, ob_ref, l_ref, qr, kr, op, lse_ref = refs[-7:]
        q_ref, k_ref, v_ref, c_ref, sl_ref, sh_ref = refs[:6]
        if has_sink:
            sk = refs[6][0:1, 0:1]
        c, s_lo, s_hi = c_ref[...], sl_ref[...], sh_ref[...]
        qr[...] = _rope(q_ref[...], c, s_lo, s_hi)
        kr[...] = _rope(k_ref[...], c, s_lo, s_hi)
        for p, (max_lag, dil) in enumerate(patterns):
            qa = _to_blocks(qr, dil).astype(BF16)
            ka = _to_blocks(kr, dil).astype(BF16)
            va = _to_blocks(v_ref, dil).astype(BF16)
            own, before = _band_masks(max_lag, dil)
            s1 = jnp.where(own, _bdot('nqd,nkd->nqk', qa, ka) * SCALE, -jnp.inf)
            m = jnp.max(s1, -1, keepdims=True)
            with_prev = dil < N_BLK
            if with_prev:
                kp, vp = _prev_block(ka), _prev_block(va)
                s0 = jnp.where(before, _bdot('nqd,nkd->nqk', qa, kp) * SCALE, -jnp.inf)
                m = jnp.maximum(m, jnp.max(s0, -1, keepdims=True))
            if has_sink:
                m = jnp.maximum(m, sk)
            e1 = jnp.exp(s1 - m)
            den = jnp.sum(e1, -1, keepdims=True)
            o = _bdot('nqk,nkd->nqd', e1.astype(BF16), va)
            if with_prev:
                e0 = jnp.exp(s0 - m)
                den = den + jnp.sum(e0, -1, keepdims=True)
                o = o + _bdot('nqk,nkd->nqd', e0.astype(BF16), vp)
            if has_sink:
                den = den + jnp.exp(sk - m)
            _from_blocks(op.at[p], o / den, dil)
            _from_blocks(lse_ref.at[p], jnp.broadcast_to(m + jnp.log(den), (N_BLK, BLK, BLK)), dil)
        if n_pat == 1:
            acc = op[0]
            l_ref[...] = lse_ref[0]
        else:
            ls = [lse_ref[p] for p in range(n_pat)]
            m = functools.reduce(jnp.maximum, ls)
            es = [jnp.exp(l - m) for l in ls]
            tot = functools.reduce(jnp.add, es)
            acc = None
            for p in range(n_pat):
                t = (es[p] / tot) * op[p]
                acc = t if acc is None else acc + t
            l_ref[...] = m + jnp.log(tot)
        o_ref[...] = acc
        ob_ref[...] = acc.astype(BF16)

    def colblk(fn):
        return pl.BlockSpec((SEQ, BLK), fn)

    tab = pl.BlockSpec((SEQ, BLK), lambda h: (0, 0))
    in_specs = [colblk(lambda h: (0, q0 + h)), colblk(lambda h: (0, k0 + h // rep)), colblk(lambda h: (0, v0 + h // rep)),
                tab, tab, tab]
    args = [proj, proj, proj, *tables]
    if has_sink:
        in_specs.append(pl.BlockSpec((None, 8, BLK), lambda h: (h, 0, 0)))
        args.append(sink_b)
    n_in = len(args)
    in_specs += [pl.BlockSpec(memory_space=pl.ANY)] * 2
    args += [mixed, mixed_bf]
    pat = pltpu.VMEM((n_pat, SEQ, BLK), F32)
    return pl.pallas_call(
        body, name=name, grid=(n_heads,), in_specs=in_specs,
        out_specs=[colblk(lambda h: (0, m0 + h)), colblk(lambda h: (0, m0 + h)),
                   pl.BlockSpec((None, SEQ, BLK), lambda h: (h, 0, 0))],
        out_shape=[jax.ShapeDtypeStruct(mixed.shape, F32), jax.ShapeDtypeStruct(mixed.shape, BF16),
                   jax.ShapeDtypeStruct((n_heads, SEQ, BLK), F32)],
        input_output_aliases={n_in: 0, n_in + 1: 1},
        scratch_shapes=[pltpu.VMEM((SEQ, BLK), F32), pltpu.VMEM((SEQ, BLK), F32), pat, pat],
        compiler_params=_cp(("parallel",)))(*args)


def _attn_bwd(name, proj, mixed, dmix, lse, tables, sink_b, dproj, *, n_kv, rep, q0, k0, v0, m0, patterns):
    n_heads = n_kv * rep
    has_sink = sink_b is not None

    def body(*refs):
        q_ref, k_ref, v_ref, o_ref, do_ref, lse_ref, c_ref, sl_ref, sh_ref = refs[:9]
        sink_ref = refs[9] if has_sink else None
        dproj_ref, dsk_ref, qr, kr, dqa, dka, dva, dd, stage, stage_sem = refs[-10:]
        g, j = pl.program_id(0), pl.program_id(1)
        c, s_lo, s_hi = c_ref[...], sl_ref[...], sh_ref[...]
        qr[...] = _rope(q_ref[...], c, s_lo, s_hi)
        kr[...] = _rope(k_ref[...], c, s_lo, s_hi)
        dcol = jnp.sum(do_ref[...] * o_ref[...], -1, keepdims=True)
        dd[...] = jnp.broadcast_to(dcol, (SEQ, BLK))

        @pl.when(j == 0)
        def _():
            dka[...] = jnp.zeros((SEQ, BLK), F32)
            dva[...] = jnp.zeros((SEQ, BLK), F32)

        for p, (max_lag, dil) in enumerate(patterns):
            qa = _to_blocks(qr, dil).astype(BF16)
            ka = _to_blocks(kr, dil).astype(BF16)
            va = _to_blocks(v_ref, dil).astype(BF16)
            doa = _to_blocks(do_ref, dil).astype(BF16)
            lcol = _to_blocks(lse_ref, dil)[:, :, 0:1]
            dcb = _to_blocks(dd, dil)[:, :, 0:1]
            own, before = _band_masks(max_lag, dil)

            def probs_and_ds(kk, vv, valid):
                s = _bdot('nqd,nkd->nqk', qa, kk) * SCALE
                a = jnp.where(valid, jnp.exp(s - lcol), 0.0)
                ds = a * (_bdot('nqd,nkd->nqk', doa, vv) - dcb) * SCALE
                return a.astype(BF16), ds.astype(BF16)

            a1, ds1 = probs_and_ds(ka, va, own)
            dq = _bdot('nqk,nkd->nqd', ds1, ka)
            dk = _bdot('nqk,nqd->nkd', ds1, qa)
            dv = _bdot('nqk,nqd->nkd', a1, doa)
            if dil < N_BLK:
                kp, vp = _prev_block(ka), _prev_block(va)
                a0, ds0 = probs_and_ds(kp, vp, before)
                dq = dq + _bdot('nqk,nkd->nqd', ds0, kp)
                dk = dk + _to_next_block(_bdot('nqk,nqd->nkd', ds0, qa))
                dv = dv + _to_next_block(_bdot('nqk,nqd->nkd', a0, doa))
            _from_blocks(dqa, dq, dil, add=p > 0)
            _from_blocks(dka, dk, dil, add=True)
            _from_blocks(dva, dv, dil, add=True)

        if has_sink:
            sk = sink_ref[0:1, 0:1]
            ps = jnp.exp(sk - lse_ref[...][:, 0:1])
            dsk_ref[...] = jnp.full((8, BLK), -jnp.sum(ps * dcol), F32)
        else:
            dsk_ref[...] = jnp.zeros((8, BLK), F32)
        @pl.when(g * rep + j > 0)
        def _():
            _wait_col_blocks(stage, stage_sem, dproj_ref, 1)

        stage[0] = _unrope(dqa[...], c, s_lo, s_hi).astype(BF16)
        _start_col_blocks(stage, stage_sem, dproj_ref, [q0 + g * rep + j])

        @pl.when(j == rep - 1)
        def _():
            @pl.when(g > 0)
            def _():
                _wait_col_blocks(stage, stage_sem, dproj_ref, 2, first=1)

            stage[1] = _unrope(dka[...], c, s_lo, s_hi).astype(BF16)
            stage[2] = dva[...].astype(BF16)
            _start_col_blocks(stage, stage_sem, dproj_ref, [k0 + g, v0 + g], first=1)

        @pl.when((g == n_kv - 1) & (j == rep - 1))
        def _():
            _wait_col_blocks(stage, stage_sem, dproj_ref, 3)

    def colblk(fn):
        return pl.BlockSpec((SEQ, BLK), fn)

    tab = pl.BlockSpec((SEQ, BLK), lambda g, j: (0, 0))
    in_specs = [colblk(lambda g, j: (0, q0 + g * rep + j)), colblk(lambda g, j: (0, k0 + g)), colblk(lambda g, j: (0, v0 + g)),
                colblk(lambda g, j: (0, m0 + g * rep + j)), colblk(lambda g, j: (0, m0 + g * rep + j)),
                pl.BlockSpec((None, SEQ, BLK), lambda g, j: (g * rep + j, 0, 0)), tab, tab, tab]
    args = [proj, proj, proj, mixed, dmix, lse, *tables]
    if has_sink:
        in_specs.append(pl.BlockSpec((None, 8, BLK), lambda g, j: (g * rep + j, 0, 0)))
        args.append(sink_b)
    n_in = len(args)
    in_specs.append(pl.BlockSpec(memory_space=pl.ANY))
    args.append(dproj)
    acc = pltpu.VMEM((SEQ, BLK), F32)
    return pl.pallas_call(
        body, name=name, grid=(n_kv, rep), in_specs=in_specs,
        out_specs=[pl.BlockSpec(memory_space=pl.ANY), pl.BlockSpec((None, 8, BLK), lambda g, j: (g * rep + j, 0, 0))],
        out_shape=[jax.ShapeDtypeStruct(dproj.shape, BF16), jax.ShapeDtypeStruct((n_heads, 8, BLK), F32)],
        input_output_aliases={n_in: 0},
        scratch_shapes=[acc, acc, acc, acc, acc, acc, pltpu.VMEM((3, SEQ, BLK), BF16), pltpu.SemaphoreType.DMA((3,))],
        compiler_params=_cp(("arbitrary", "arbitrary")))(*args)


B_PATTERNS = tuple((w // d, d) for w, d in DILATED_PATTERNS)
C_PATTERNS = ((C_WINDOW - 1, 1),)


ANY = pl.BlockSpec(memory_space=pl.ANY)
CHIP_MASKS = ((1, 0), (0, 1), (1, 1))


def _coords():
    return lax.axis_index("x"), lax.axis_index("y"), lax.axis_index("c")


def _flip(v, m):
    return 1 - v if m else v


def _into_slot(name, w, layer, k_idx, dtype, run_after=None):
    _, rows, cols = w.shape
    tr = rows // 8 if rows % 64 == 0 else rows

    def body(k_ref, w_ref, *rest):
        rest[-1][...] = w_ref[...].astype(dtype)

    in_specs = [pl.BlockSpec((None, tr, cols), lambda i, k: (layer, i, 0))]
    args = [k_idx, w]
    if run_after is not None:
        in_specs.append(pl.BlockSpec(memory_space=pl.ANY))
        args.append(run_after)
    return pl.pallas_call(
        body, name=name,
        grid_spec=pltpu.PrefetchScalarGridSpec(
            num_scalar_prefetch=1, grid=(rows // tr,), in_specs=in_specs,
            out_specs=pl.BlockSpec((None, tr, cols), lambda i, k: (k[0], i, 0))),
        out_shape=jax.ShapeDtypeStruct((N_CHIPS, rows, cols), dtype),
        compiler_params=_cp(("parallel",)))(*args)


HBM_SPEC = pl.BlockSpec(memory_space=pltpu.HBM)
SEM_SPEC = pl.BlockSpec(memory_space=pltpu.SEMAPHORE)
TOKEN_SPEC = pl.BlockSpec(memory_space=pltpu.VMEM)
TOKEN_SHAPE = jax.ShapeDtypeStruct((8, BLK), F32)
DATAFLOW = pltpu.SideEffectType.DATAFLOW_SIDE_EFFECTING


def _hbm(a):
    return pltpu.with_memory_space_constraint(a, pltpu.HBM)


def _hbm_like(bufs):
    return [pltpu.HBM(b.shape, b.dtype) for b in bufs]


def _gather_start(name, stages):
    flat = [b for st in stages for b in st]
    n, ns = len(flat), len(stages)

    def body(*refs):
        ins = refs[:n]
        sems = refs[n:n + 2 * ns]
        token = refs[-1]
        x, y, c = _coords()
        k_me = 2 * x + y
        a = 0
        for s, st in enumerate(stages):
            for i in range(len(st)):
                mine = ins[a].at[k_me, c]
                for m, (mx, my) in enumerate(CHIP_MASKS):
                    pltpu.make_async_remote_copy(src_ref=mine, dst_ref=mine, send_sem=sems[2 * s].at[i * 3 + m],
                                                 recv_sem=sems[2 * s + 1].at[i * 3 + m],
                                                 device_id=(_flip(x, mx), _flip(y, my), c), device_id_type=MESH).start()
                a += 1
        token[...] = jnp.zeros_like(token)

    sem_shapes = []
    for st in stages:
        sem_shapes += [pltpu.SemaphoreType.DMA((3 * len(st),))] * 2
    out = pl.pallas_call(
        body, name=name, in_specs=[HBM_SPEC] * n,
        out_specs=tuple([SEM_SPEC] * (2 * ns) + [HBM_SPEC] * n + [TOKEN_SPEC]),
        out_shape=tuple(sem_shapes + _hbm_like(flat) + [TOKEN_SHAPE]),
        input_output_aliases={i: 2 * ns + i for i in range(n)},
        compiler_params=pltpu.CompilerParams(has_side_effects=DATAFLOW),
    )(*[_hbm(b) for b in flat])
    sems, bufs, token = out[:2 * ns], out[2 * ns:2 * ns + n], out[-1]
    res, a = [], 0
    for s, st in enumerate(stages):
        res.append((sems[2 * s], sems[2 * s + 1], list(bufs[a:a + len(st)])))
        a += len(st)
    return res, token


def _gather_forward(name, stage, after):
    ssem_in, rsem_in, bufs = stage
    n = len(bufs)

    def body(*refs):
        ins = refs[:n]
        s_in, r_in, _ = refs[n:n + 3]
        s_out, r_out = refs[n + 3:n + 5]
        token = refs[-1]
        x, y, c = _coords()
        for i in range(n):
            for m, (mx, my) in enumerate(CHIP_MASKS):
                kp = 2 * _flip(x, mx) + _flip(y, my)
                blk = ins[i].at[kp, c]
                got = pltpu.make_async_remote_copy(src_ref=blk, dst_ref=blk, send_sem=s_in.at[i * 3 + m],
                                                   recv_sem=r_in.at[i * 3 + m], device_id=(x, y, 1 - c), device_id_type=MESH)
                got.wait_send()
                got.wait_recv()
                pltpu.make_async_remote_copy(src_ref=blk, dst_ref=blk, send_sem=s_out.at[i * 3 + m],
                                             recv_sem=r_out.at[i * 3 + m], device_id=(x, y, 1 - c), device_id_type=MESH).start()
        token[...] = jnp.zeros_like(token)

    sem = pltpu.SemaphoreType.DMA((3 * n,))
    out = pl.pallas_call(
        body, name=name, in_specs=[HBM_SPEC] * n + [SEM_SPEC, SEM_SPEC, ANY],
        out_specs=tuple([SEM_SPEC, SEM_SPEC] + [HBM_SPEC] * n + [TOKEN_SPEC]),
        out_shape=tuple([sem, sem] + _hbm_like(bufs) + [TOKEN_SHAPE]),
        input_output_aliases={i: 2 + i for i in range(n)},
        compiler_params=pltpu.CompilerParams(has_side_effects=DATAFLOW),
    )(*bufs, ssem_in, rsem_in, after)
    return (out[0], out[1], list(out[2:2 + n])), out[-1]


def _gather_wait(name, stage, after):
    ssem, rsem, bufs = stage
    n = len(bufs)

    def body(*refs):
        ins = refs[:n]
        s_in, r_in, _ = refs[n:n + 3]
        x, y, c = _coords()
        for i in range(n):
            for m, (mx, my) in enumerate(CHIP_MASKS):
                kp = 2 * _flip(x, mx) + _flip(y, my)
                sent, got = ins[i].at[kp, c], ins[i].at[kp, 1 - c]
                cp = pltpu.make_async_remote_copy(src_ref=sent, dst_ref=got, send_sem=s_in.at[i * 3 + m],
                                                  recv_sem=r_in.at[i * 3 + m], device_id=(x, y, 1 - c), device_id_type=MESH)
                cp.wait_send()
                cp.wait_recv()

    out = pl.pallas_call(
        body, name=name, in_specs=[HBM_SPEC] * n + [SEM_SPEC, SEM_SPEC, ANY],
        out_specs=tuple([HBM_SPEC] * n), out_shape=tuple(_hbm_like(bufs)),
        input_output_aliases={i: i for i in range(n)},
        compiler_params=pltpu.CompilerParams(has_side_effects=DATAFLOW),
    )(*bufs, ssem, rsem, after)
    return list(out)


def _swap_start(name, grads):
    n = len(grads)

    def body(*refs):
        ins, lands = refs[:n], refs[n:2 * n]
        ssem, rsem = refs[2 * n:2 * n + 2]
        x, y, c = _coords()
        for a in range(n):
            for j in range(N_CHIPS):
                pltpu.make_async_remote_copy(src_ref=ins[a].at[j, 1 - c], dst_ref=lands[a].at[j],
                                             send_sem=ssem.at[a * N_CHIPS + j], recv_sem=rsem.at[a * N_CHIPS + j],
                                             device_id=(x, y, 1 - c), device_id_type=MESH).start()

    sem = pltpu.SemaphoreType.DMA((N_CHIPS * n,))
    land_shapes = [pltpu.HBM((N_CHIPS,) + g.shape[2:], g.dtype) for g in grads]
    out = pl.pallas_call(
        body, name=name, in_specs=[HBM_SPEC] * (2 * n),
        out_specs=tuple([SEM_SPEC, SEM_SPEC] + [HBM_SPEC] * (2 * n)),
        out_shape=tuple([sem, sem] + _hbm_like(grads) + land_shapes),
        input_output_aliases={i: 2 + i for i in range(2 * n)},
        compiler_params=pltpu.CompilerParams(has_side_effects=DATAFLOW),
    )(*[_hbm(g) for g in grads], *[_hbm(lax.empty((N_CHIPS,) + g.shape[2:], g.dtype)) for g in grads])
    return out[0], out[1], list(out[2:2 + n]), list(out[2 + n:])


def _swap_wait(name, started, after):
    ssem, rsem, grads, lands = started
    n = len(grads)
    after = after if isinstance(after, tuple) else (after,)

    def body(*refs):
        ins, lnd = refs[:n], refs[n:2 * n]
        s_in, r_in = refs[2 * n:2 * n + 2]
        x, y, c = _coords()
        for a in range(n):
            for j in range(N_CHIPS):
                cp = pltpu.make_async_remote_copy(src_ref=ins[a].at[j, 1 - c], dst_ref=lnd[a].at[j],
                                                  send_sem=s_in.at[a * N_CHIPS + j], recv_sem=r_in.at[a * N_CHIPS + j],
                                                  device_id=(x, y, 1 - c), device_id_type=MESH)
                cp.wait_send()
                cp.wait_recv()

    out = pl.pallas_call(
        body, name=name, in_specs=[HBM_SPEC] * (2 * n) + [SEM_SPEC, SEM_SPEC] + [ANY] * len(after),
        out_specs=tuple([HBM_SPEC] * (2 * n)), out_shape=tuple(_hbm_like(grads) + _hbm_like(lands)),
        input_output_aliases={i: i for i in range(2 * n)},
        compiler_params=pltpu.CompilerParams(has_side_effects=DATAFLOW),
    )(*grads, *lands, ssem, rsem, *after)
    return list(out[:n]), list(out[n:])


def _scatter_start(name, parts):
    n = len(parts)

    def body(*refs):
        ins, lands = refs[:n], refs[n:2 * n]
        ssem, rsem = refs[2 * n:2 * n + 2]
        x, y, c = _coords()
        k_me = 2 * x + y
        for a in range(n):
            for m, (mx, my) in enumerate(CHIP_MASKS):
                px, py = _flip(x, mx), _flip(y, my)
                pltpu.make_async_remote_copy(src_ref=ins[a].at[2 * px + py], dst_ref=lands[a].at[k_me],
                                             send_sem=ssem.at[a * 3 + m], recv_sem=rsem.at[a * 3 + m],
                                             device_id=(px, py, c), device_id_type=MESH).start()

    sem = pltpu.SemaphoreType.DMA((3 * n,))
    out = pl.pallas_call(
        body, name=name, in_specs=[HBM_SPEC] * (2 * n),
        out_specs=tuple([SEM_SPEC, SEM_SPEC] + [HBM_SPEC] * (2 * n)),
        out_shape=tuple([sem, sem] + _hbm_like(parts) + _hbm_like(parts)),
        input_output_aliases={i: 2 + i for i in range(2 * n)},
        compiler_params=pltpu.CompilerParams(has_side_effects=DATAFLOW),
    )(*[_hbm(p) for p in parts], *[_hbm(lax.empty(p.shape, p.dtype)) for p in parts])
    return out[0], out[1], list(out[2:2 + n]), list(out[2 + n:])


def _scatter_wait(name, started, after):
    ssem, rsem, parts, lands = started
    n = len(parts)

    def body(*refs):
        ins, lnd = refs[:n], refs[n:2 * n]
        s_in, r_in, _ = refs[2 * n:2 * n + 3]
        x, y, c = _coords()
        k_me = 2 * x + y
        for a in range(n):
            for m, (mx, my) in enumerate(CHIP_MASKS):
                px, py = _flip(x, mx), _flip(y, my)
                cp = pltpu.make_async_remote_copy(src_ref=ins[a].at[2 * px + py], dst_ref=lnd[a].at[k_me],
                                                  send_sem=s_in.at[a * 3 + m], recv_sem=r_in.at[a * 3 + m],
                                                  device_id=(px, py, c), device_id_type=MESH)
                cp.wait_send()
                cp.wait_recv()

    out = pl.pallas_call(
        body, name=name, in_specs=[HBM_SPEC] * (2 * n) + [SEM_SPEC, SEM_SPEC, ANY],
        out_specs=tuple([HBM_SPEC] * (2 * n)), out_shape=tuple(_hbm_like(parts) + _hbm_like(lands)),
        input_output_aliases={i: i for i in range(2 * n)},
        compiler_params=pltpu.CompilerParams(has_side_effects=DATAFLOW),
    )(*parts, *lands, ssem, rsem, after)
    return list(out[:n]), list(out[n:])


def _pair_gather_start(name, bufs):
    n = len(bufs)

    def body(*refs):
        ins = refs[:n]
        ssem, rsem = refs[n:n + 2]
        x, y, c = _coords()
        for a in range(n):
            mine = ins[a].at[c]
            pltpu.make_async_remote_copy(src_ref=mine, dst_ref=mine, send_sem=ssem.at[a], recv_sem=rsem.at[a],
                                         device_id=(x, y, 1 - c), device_id_type=MESH).start()

    sem = pltpu.SemaphoreType.DMA((n,))
    out = pl.pallas_call(
        body, name=name, in_specs=[HBM_SPEC] * n, out_specs=tuple([SEM_SPEC, SEM_SPEC] + [HBM_SPEC] * n),
        out_shape=tuple([sem, sem] + _hbm_like(bufs)),
        input_output_aliases={i: 2 + i for i in range(n)},
        compiler_params=pltpu.CompilerParams(has_side_effects=DATAFLOW),
    )(*[_hbm(b) for b in bufs])
    return out[0], out[1], list(out[2:])


def _pair_gather_wait(name, started, after):
    ssem, rsem, bufs = started
    n = len(bufs)

    def body(*refs):
        ins = refs[:n]
        s_in, r_in, _ = refs[n:n + 3]
        x, y, c = _coords()
        for a in range(n):
            cp = pltpu.make_async_remote_copy(src_ref=ins[a].at[c], dst_ref=ins[a].at[1 - c], send_sem=s_in.at[a],
                                              recv_sem=r_in.at[a], device_id=(x, y, 1 - c), device_id_type=MESH)
            cp.wait_send()
            cp.wait_recv()

    out = pl.pallas_call(
        body, name=name, in_specs=[HBM_SPEC] * n + [SEM_SPEC, SEM_SPEC, ANY],
        out_specs=tuple([HBM_SPEC] * n), out_shape=tuple(_hbm_like(bufs)),
        input_output_aliases={i: i for i in range(n)},
        compiler_params=pltpu.CompilerParams(has_side_effects=DATAFLOW),
    )(*bufs, ssem, rsem, after)
    return list(out)


DEV_MASKS = tuple((mx, my, mc) for mx in (0, 1) for my in (0, 1) for mc in (0, 1) if (mx, my, mc) != (0, 0, 0))


def _gather_small(buf, run_after):
    def body(in_ref, _, out_ref, ssem, rsem, lsem):
        x, y, c = _coords()
        me = 4 * x + 2 * y + c
        cps = [pltpu.make_async_copy(in_ref, out_ref.at[me], lsem)]
        cps[0].start()
        for t, (mx, my, mc) in enumerate(DEV_MASKS):
            cp = pltpu.make_async_remote_copy(src_ref=in_ref, dst_ref=out_ref.at[me], send_sem=ssem.at[t],
                                              recv_sem=rsem.at[t], device_id=(_flip(x, mx), _flip(y, my), _flip(c, mc)),
                                              device_id_type=MESH)
            cp.start()
            cps.append(cp)
        for cp in cps:
            cp.wait()

    return pl.pallas_call(
        body, name="gather_small", in_specs=[ANY, ANY], out_specs=ANY,
        out_shape=jax.ShapeDtypeStruct((N_DEV,) + buf.shape, buf.dtype),
        scratch_shapes=[pltpu.SemaphoreType.DMA((N_DEV - 1,)), pltpu.SemaphoreType.DMA((N_DEV - 1,)),
                        pltpu.SemaphoreType.DMA(())],
        compiler_params=pltpu.CompilerParams(has_side_effects=True),
    )(buf, run_after)


def _pair_add(name, grads, gots, c_idx):
    n = len(grads)
    tiles = [(g.shape[2] // 2, g.shape[3]) for g in grads]

    def body(c_ref, *refs):
        for a in range(n):
            refs[2 * n + a][...] = (refs[2 * a][...].astype(F32) + refs[2 * a + 1][...].astype(F32)).astype(BF16)

    in_specs, out_specs, args = [], [], []
    for g, r, (tr, cols) in zip(grads, gots, tiles):
        in_specs += [pl.BlockSpec((None, None, tr, cols), lambda j, i, c: (j, c[0], i, 0)),
                     pl.BlockSpec((None, tr, cols), lambda j, i, c: (j, i, 0))]
        out_specs.append(pl.BlockSpec((None, tr, cols), lambda j, i, c: (j, i, 0)))
        args += [g, r]
    return pl.pallas_call(
        body, name=name,
        grid_spec=pltpu.PrefetchScalarGridSpec(num_scalar_prefetch=1, grid=(N_CHIPS, 2), in_specs=in_specs,
                                               out_specs=out_specs),
        out_shape=[jax.ShapeDtypeStruct((N_CHIPS, g.shape[2], g.shape[3]), BF16) for g in grads],
        compiler_params=_cp(("parallel", "parallel")))(c_idx, *args)


def _chip_add(name, parts, gots, kc_idx):
    n = len(parts)

    def body(k_ref, *refs):
        for a in range(n):
            acc = refs[4 * a][...].astype(F32)
            for d in range(1, N_CHIPS):
                acc = acc + refs[4 * a + d][...].astype(F32)
            refs[4 * n + a][...] = acc

    in_specs, out_specs, args = [], [], []
    for p, g in zip(parts, gots):
        tr, cols = g.shape[1] // 4, g.shape[2]
        in_specs += [pl.BlockSpec((None, tr, cols), lambda i, k, d=d: ((k[0] + d) % N_CHIPS, i, 0)) for d in range(N_CHIPS)]
        out_specs.append(pl.BlockSpec((None, tr, cols), lambda i, k: (k[1], i, 0)))
        args += [p, g, g, g]
    return pl.pallas_call(
        body, name=name,
        grid_spec=pltpu.PrefetchScalarGridSpec(num_scalar_prefetch=1, grid=(4,), in_specs=in_specs, out_specs=out_specs),
        out_shape=[jax.ShapeDtypeStruct((2,) + g.shape[1:], F32) for g in gots],
        compiler_params=_cp(("parallel",)))(kc_idx, *args)


def _adam_math(w, g, m, v):
    m2 = ADAM_B1 * m + (1.0 - ADAM_B1) * g
    v2 = ADAM_B2 * v + (1.0 - ADAM_B2) * (g * g)
    m_hat = m2 / (1.0 - ADAM_B1 ** ADAM_STEP)
    v_hat = v2 / (1.0 - ADAM_B2 ** ADAM_STEP)
    delta = -ADAM_LR * (m_hat / (jnp.sqrt(v_hat) + ADAM_EPS) + ADAM_WD * w)
    return delta, m2, v2


def _adamw_matrix(name, w, g_layers, m, v):
    _, rows, cols = w.shape
    tr = rows // 8

    def body(w_ref, g0_ref, g1_ref, m_ref, v_ref, go_ref, d_ref, mo_ref, vo_ref):
        g = jnp.where(pl.program_id(0) == 0, g0_ref[...], g1_ref[...])
        go_ref[...] = g
        d_ref[...], mo_ref[...], vo_ref[...] = _adam_math(w_ref[...], g, m_ref[...], v_ref[...])

    lay = pl.BlockSpec((None, tr, cols), lambda l, i: (l, i, 0))
    flat = pl.BlockSpec((tr, cols), lambda l, i: (i, 0))
    shp = jax.ShapeDtypeStruct(w.shape, F32)
    return pl.pallas_call(body, name=name, grid=(DEPTH, rows // tr), in_specs=[lay, flat, flat, lay, lay],
                          out_specs=[lay, lay, lay, lay], out_shape=[shp, shp, shp, shp],
                          compiler_params=_cp(("parallel", "parallel")))(w, g_layers[0], g_layers[1], m, v)


def _sum_small(gathered):
    def body(g_ref, o_ref):
        acc = g_ref[0]
        for d in range(1, N_DEV):
            acc = acc + g_ref[d]
        o_ref[...] = acc

    return pl.pallas_call(body, name="sum_small", out_shape=jax.ShapeDtypeStruct(gathered.shape[1:], F32),
                          compiler_params=_cp())(gathered)


def _adamw_small(w, g, m, v):
    def body(w_ref, g_ref, m_ref, v_ref, d_ref, mo_ref, vo_ref):
        d_ref[...], mo_ref[...], vo_ref[...] = _adam_math(w_ref[...], g_ref[...], m_ref[...], v_ref[...])

    shp = jax.ShapeDtypeStruct(w.shape, F32)
    return pl.pallas_call(body, name="adamw_small", out_shape=[shp, shp, shp], compiler_params=_cp())(w, g, m, v)


def _pack(arrays, rows):
    flat = jnp.concatenate([a.reshape(-1) for a in arrays])
    return jnp.pad(flat, (0, rows * BLK - flat.shape[0])).reshape(rows, BLK)


def _unpack(buf, shapes):
    flat = buf.reshape(-1)
    out, pos = [], 0
    for s in shapes:
        n = math.prod(s)
        out.append(flat[pos:pos + n].reshape(s))
        pos += n
    return out


def _rows_for(shapes):
    n = sum(math.prod(s) for s in shapes)
    return -(-n // (8 * BLK)) * 8


def _rs_swap(tag, grads):
    return _swap_start(f"rs_swap_start{tag}", [g.reshape(N_CHIPS, 2, g.shape[1] // 2, g.shape[2]) for g in grads])


def _rs_scatter(tag, swapping, after, c_idx):
    split, got = _swap_wait(f"rs_swap_wait{tag}", swapping, after)
    parts = _pair_add(f"rs_pair_add{tag}", split, got, c_idx)
    return _scatter_start(f"rs_scatter_start{tag}", parts)


def _rs_reduce(tag, started, after, kc_idx):
    parts, lands = _scatter_wait(f"rs_scatter_wait{tag}", started, after)
    halves = _chip_add(f"rs_chip_add{tag}", parts, lands, kc_idx)
    return _pair_gather_start(f"rs_pair_gather_start{tag}", halves)


def _rs_finish(tag, gathering, after):
    full = _pair_gather_wait(f"rs_pair_gather_wait{tag}", gathering, after)
    return [f.reshape(2 * f.shape[1], f.shape[2]) for f in full]


def kernel(x, w_in, lb_logits, a_norm_w, c_sinks, w_out, ln1_g, ln1_b, w_gate, w_up, conv_w, conv_b, w_down, ln2_g, ln2_b, loss_target, m_w_in, m_lb_logits, m_a_norm_w, m_c_sinks, m_w_out, m_ln1_g, m_ln1_b, m_w_gate, m_w_up, m_conv_w, m_conv_b, m_w_down, m_ln2_g, m_ln2_b, v_w_in, v_lb_logits, v_a_norm_w, v_c_sinks, v_w_out, v_ln1_g, v_ln1_b, v_w_gate, v_w_up, v_conv_w, v_conv_b, v_w_down, v_ln2_g, v_ln2_b):
    cx, cy, cc = _coords()
    c_idx = jnp.reshape(cc, (1,)).astype(jnp.int32)
    k_me = 2 * cx + cy
    k_idx = jnp.reshape(k_me, (1,)).astype(jnp.int32)
    kc_idx = jnp.stack([k_me, cc]).astype(jnp.int32)

    def slot(nm, w, l, run_after=None):
        b = _into_slot(f"slot_{nm}{l}", w, l, k_idx, BF16, run_after)
        return b.reshape(N_CHIPS, 2, b.shape[1] // 2, b.shape[2])

    cw_slot = _into_slot("slot_cw", conv_w.reshape(1, DEPTH * CONV_WIDTH, FF_SHARD), 0, k_idx, F32)
    cw_slot = cw_slot.reshape(N_CHIPS, DEPTH, CONV_WIDTH, FF_SHARD)
    first, token = _gather_start("gather_start0", [[slot("wi", w_in, 0), cw_slot]])
    sl = [{nm: slot(nm, w, l, token) for nm, w in (("wi", w_in), ("wo", w_out), ("wg", w_gate), ("wu", w_up), ("wd", w_down))
           if (nm, l) != ("wi", 0)} for l in range(DEPTH)]
    order = [(l, nm) for l in range(DEPTH) for nm in ("wi", "wo", "wg", "wu", "wd")][1:]
    rest, token = _gather_start("gather_start1", [[sl[l][nm]] for l, nm in order])
    stage_of = {key: st for key, st in zip(order, rest)}

    def mat(b):
        return b.reshape(N_CHIPS, 2 * b.shape[2], b.shape[3])

    h = x[0]
    h_bf = _to_bf16("x_bf16", h, token)
    fwd0, token = _gather_forward("gather_fwd0", first[0], h_bf)
    wi0, cw_all = _gather_wait("gather_wait0", fwd0, token)
    cw_full = jnp.transpose(cw_all, (1, 2, 0, 3)).reshape(DEPTH, CONV_WIDTH, D_FF)
    tables = _rope_tables()

    passing = {}

    def pass_on(l, nm, after):
        passing[(l, nm)] = _gather_forward(f"gather_fwd_{nm}{l}", stage_of[(l, nm)], after)

    def arrived(l, nm, after):
        i = order.index((l, nm))
        if i + 1 < len(order):
            pass_on(*order[i + 1], after)
            after = passing[order[i + 1]][1]
        return mat(_gather_wait(f"gather_wait_{nm}{l}", passing[(l, nm)][0], after)[0])

    saved = []
    weights = []
    for l in range(DEPTH):
        wi = mat(wi0) if l == 0 else arrived(l, "wi", h)
        proj = _fwd_colsharded(f"proj{l}", h_bf, wi)
        mixed, mixed_bf, raw = _hgrn_fwd(f"hgrn_fwd{l}", proj, lb_logits, a_norm_w[l], l)
        mixed, mixed_bf, lse_b = _attn_fwd(f"dilated_fwd{l}", proj, tables, None, mixed, mixed_bf, n_heads=B_HEADS, rep=1,
                                           q0=QB0, k0=KB0, v0=VB0, m0=A_HEADS, patterns=B_PATTERNS)
        if l == 0:
            pass_on(l, "wo", lse_b)
        sink_b = jnp.broadcast_to(c_sinks[l][:, None, None], (C_HEADS, 8, BLK))
        mixed, mixed_bf, lse_c = _attn_fwd(f"window_fwd{l}", proj, tables, sink_b, mixed, mixed_bf, n_heads=C_HEADS,
                                           rep=C_HEADS // C_KV_HEADS, q0=QC0, k0=KC0, v0=VC0, m0=A_HEADS + B_HEADS,
                                           patterns=C_PATTERNS)
        wo = arrived(l, "wo", lse_c)
        y1 = _fwd_rowsharded(f"wout{l}", mixed_bf, wo, OUT_SHARD)
        x1, x1_bf = _ln_fwd(f"ln1_fwd{l}", h, y1, ln1_g[l], ln1_b[l])
        wg = arrived(l, "wg", x1)
        g = _fwd_colsharded(f"gate{l}", x1_bf, wg, BF16)
        wu = arrived(l, "wu", g)
        u = _fwd_colsharded(f"up{l}", x1_bf, wu, BF16)
        hh = _conv_gate_fwd(f"conv_fwd{l}", g, u, cw_full[l], conv_b[l])
        wd = arrived(l, "wd", hh)
        y2 = _fwd_rowsharded(f"down{l}", hh, wd, FF_SHARD)
        weights.append(dict(wi=wi, wo=wo, wg=wg, wu=wu, wd=wd))
        saved.append((h, h_bf, proj, raw, lse_b, sink_b, lse_c, mixed, mixed_bf, y1, x1, x1_bf, g, u, hh, y2))
        if l + 1 < DEPTH:
            h, h_bf = _ln_fwd(f"ln2_fwd{l}", x1, y2, ln2_g[l], ln2_b[l])

    d_res = d_path = None
    small = [None] * DEPTH
    mat_grads = [None] * DEPTH
    late = {}
    prev_ffn = prev_mix_swap = None
    for l in reversed(range(DEPTH)):
        h_in, h_in_bf, proj, raw, lse_b, sink_b, lse_c, mixed, mixed_bf, y1, x1, x1_bf, g, u, hh, y2 = saved[l]
        wi, wo, wg, wu, wd = (weights[l][k] for k in ("wi", "wo", "wg", "wu", "wd"))
        if l == DEPTH - 1:
            dz2, dz2_bf, d_ln2g, d_ln2b, loss_part = _ln_loss_bwd(f"ln2_loss_bwd{l}", x1, y2, ln2_g[l], ln2_b[l],
                                                                  loss_target[0])
        else:
            dz2, dz2_bf, d_ln2g, d_ln2b = _ln_bwd(f"ln2_bwd{l}", x1, y2, ln2_g[l], d_res, d_path,
                                                  run_after=prev_mix_swap[2][0])
        dhh = _bwd_act_rowsharded(f"down_dx{l}", dz2_bf, wd, FF_SHARD, BF16)
        prev_mix = _rs_scatter(f"{l + 1}m", prev_mix_swap, dhh, c_idx) if prev_mix_swap else None
        d_wd = _bwd_w_rowsharded(f"down_dw{l}", hh, dz2_bf, FF_SHARD)
        dg, du, d_cw, d_cb = _conv_gate_bwd(f"conv_bwd{l}", g, u, cw_full[l], conv_b[l], dhh,
                                            run_after=prev_mix[2][0] if prev_mix else None)
        dx1 = _bwd_act_colsharded(f"gateup_dx{l}", [(dg, wg), (du, wu)])
        d_wg = _bwd_w_colsharded(f"gate_dw{l}", x1_bf, dg)
        d_wu = _bwd_w_colsharded(f"up_dw{l}", x1_bf, du)
        pins = ()
        if prev_ffn:
            late[l + 1] = [_rs_reduce(f"{l + 1}f", prev_ffn, d_wu, kc_idx)]
        ffn_swap = _rs_swap(f"{l}f", [d_wg, d_wu, d_wd])
        dz1, dz1_bf, d_ln1g, d_ln1b = _ln_bwd(f"ln1_bwd{l}", h_in, y1, ln1_g[l], dz2, dx1, run_after=ffn_swap[2][0])
        dmix = _bwd_act_rowsharded(f"wout_dx{l}", dz1_bf, wo, OUT_SHARD)
        d_wo = _bwd_w_rowsharded(f"wout_dw{l}", mixed_bf, dz1_bf, OUT_SHARD)
        if prev_mix:
            late[l + 1].append(_rs_reduce(f"{l + 1}m", prev_mix, d_wo, kc_idx))
            pins = tuple(g[2][0] for g in late[l + 1])
        s_ffn = _rs_scatter(f"{l}f", ffn_swap, (d_wo,) + pins, c_idx)
        dproj, d_nw, d_lb = _hgrn_bwd(f"hgrn_bwd{l}", proj, raw, dmix, lb_logits, a_norm_w[l], l, run_after=s_ffn[2][0])
        dproj, _ = _attn_bwd(f"dilated_bwd{l}", proj, mixed, dmix, lse_b, tables, None, dproj, n_kv=B_HEADS, rep=1,
                             q0=QB0, k0=KB0, v0=VB0, m0=A_HEADS, patterns=B_PATTERNS)
        dproj, d_sink = _attn_bwd(f"window_bwd{l}", proj, mixed, dmix, lse_c, tables, sink_b, dproj, n_kv=C_KV_HEADS,
                                  rep=C_HEADS // C_KV_HEADS, q0=QC0, k0=KC0, v0=VC0, m0=A_HEADS + B_HEADS,
                                  patterns=C_PATTERNS)
        dxp = _bwd_act_colsharded_full(f"proj_dx{l}", dproj, wi, residual=dz1 if l == 0 else None)
        d_wi = _bwd_w_colsharded(f"proj_dw{l}", h_in_bf, dproj)
        d_res, d_path = dz1, dxp
        prev_ffn, prev_mix_swap = s_ffn, _rs_swap(f"{l}m", [d_wi, d_wo])
        small[l] = (d_lb, d_nw.reshape(A_HEADS, 8, BLK)[:, 0].sum(0), d_sink[:, 0, 0], d_ln1g[0], d_ln1b[0],
                    d_cw, d_cb[0], d_ln2g[0], d_ln2b[0])
    grad_x2 = d_path
    grad_x = grad_x2[None]

    g_lb = small[0][0] + small[1][0]
    per_layer = [jnp.stack([small[0][i], small[1][i]]) for i in range(1, 9)]
    small_shapes = [(DEPTH, 4 * BLK), (DEPTH, BLK), (DEPTH, C_HEADS), (DEPTH, D_MODEL), (DEPTH, D_MODEL),
                    (DEPTH, CONV_WIDTH, D_FF), (DEPTH, D_FF), (DEPTH, D_MODEL), (DEPTH, D_MODEL), (BLK,)]
    rows = _rows_for(small_shapes)
    total = _sum_small(_gather_small(_pack([g_lb] + per_layer + [loss_part[0]], rows), prev_mix_swap[2][0]))
    g_lb, g_nw, g_sink, g_ln1g, g_ln1b, g_cw_full, g_cb, g_ln2g, g_ln2b, loss_row = _unpack(total, small_shapes)
    loss = loss_row[0]
    g_cw = lax.dynamic_slice_in_dim(g_cw_full, k_me * FF_SHARD, FF_SHARD, axis=2)

    sw = [lb_logits, a_norm_w, c_sinks, ln1_g, ln1_b, conv_w, conv_b, ln2_g, ln2_b]
    sg = [g_lb, g_nw, g_sink, g_ln1g, g_ln1b, g_cw, g_cb, g_ln2g, g_ln2b]
    sm = [m_lb_logits, m_a_norm_w, m_c_sinks, m_ln1_g, m_ln1_b, m_conv_w, m_conv_b, m_ln2_g, m_ln2_b]
    sv = [v_lb_logits, v_a_norm_w, v_c_sinks, v_ln1_g, v_ln1_b, v_conv_w, v_conv_b, v_ln2_g, v_ln2_b]
    shapes = [a.shape for a in sw]
    prow = _rows_for(shapes)
    sd, snm, snv = (_unpack(b, shapes) for b in _adamw_small(_pack(sw, prow), _pack(sg, prow), _pack(sm, prow), _pack(sv, prow)))

    names = ["w_in", "w_out", "w_gate", "w_up", "w_down"]
    mw = [w_in, w_out, w_gate, w_up, w_down]
    mm = [m_w_in, m_w_out, m_w_gate, m_w_up, m_w_down]
    mv = [v_w_in, v_w_out, v_w_gate, v_w_up, v_w_down]
    res = [None] * 5
    s_mix = _rs_scatter("0m", prev_mix_swap, total, c_idx)
    for l, (g_ffn, g_mix) in late.items():
        g_wg, g_wu, g_wd = _rs_finish(f"{l}f", g_ffn, s_mix[2][0])
        g_wi, g_wo = _rs_finish(f"{l}m", g_mix, s_mix[2][0])
        mat_grads[l] = [g_wi, g_wo, g_wg, g_wu, g_wd]
    ffn0 = _rs_finish("0f", _rs_reduce("0f", prev_ffn, s_mix[2][0], kc_idx), s_mix[2][0])
    for i, g0 in zip((2, 3, 4), ffn0):
        res[i] = _adamw_matrix(f"adamw_{names[i]}", mw[i], [g0, mat_grads[1][i]], mm[i], mv[i])
    mix0 = _rs_finish("0m", _rs_reduce("0m", s_mix, res[4][1], kc_idx), res[4][1])
    for i, g0 in zip((0, 1), mix0):
        res[i] = _adamw_matrix(f"adamw_{names[i]}", mw[i], [g0, mat_grads[1][i]], mm[i], mv[i])
    mg, md, mnm, mnv = ([r[j] for r in res] for j in range(4))

    def ordered(mat, sm_):
        return [mat[0], sm_[0], sm_[1], sm_[2], mat[1], sm_[3], sm_[4], mat[2], mat[3], sm_[5], sm_[6], mat[4], sm_[7], sm_[8]]

    return (loss, grad_x, *ordered(mg, sg), *ordered(md, sd), *ordered(mnm, snm), *ordered(mnv, snv))
```
